```python
import math
import jax, jax.numpy as jnp
from jax import lax
import numpy as np

D_MODEL = 1024
BATCH = 2
SEQ = 8192
DEPTH = 2
DEC_BATCH = 32
DEC_SEQ = 4
PAST_LEN = 8192
PAGE_SIZE = 128

N_EVEN = (DEPTH + 1) // 2
N_ODD = DEPTH // 2

GLA_HEADS = 4
GLA_DV = D_MODEL // 2 // GLA_HEADS
GLA_DK = GLA_DV // 2
GLA_RANK = 16
GLA_TAU = 16.0
GLA_CHUNK = 64

NSA_HEADS = 8
NSA_KV_HEADS = 2
NSA_GROUP = NSA_HEADS // NSA_KV_HEADS
HEAD_DIM = D_MODEL // 2 // NSA_HEADS
CMP_BLK = 32
CMP_STRIDE = 16
SEL_BLK = 64
SEL_TOPN = 16
WINDOW = 512
Q_BLK = 128
FORCE_BONUS = 100.0
ROPE_DIM = HEAD_DIM // 4
ROPE_THETA = 500000.0

GLA_SIZES = (GLA_HEADS * GLA_DK, GLA_HEADS * GLA_DK, GLA_HEADS * GLA_DV, GLA_HEADS * GLA_DV, GLA_RANK)
NSA_SIZES = (NSA_HEADS * HEAD_DIM, 6 * NSA_KV_HEADS * HEAD_DIM, 3 * NSA_HEADS)
IN_AB = sum(GLA_SIZES) + sum(NSA_SIZES)
MIX_AB = GLA_HEADS * GLA_DV + NSA_HEADS * HEAD_DIM

CONV_W = 31
D_CONV = D_MODEL

N_EXPERTS = 64
N_GROUPS = 8
TOPK_GROUPS = 4
TOP_K = 8
D_EXPERT = 256
D_SHARED = 256
ROUTE_SCALE = 2.5
MOE_BLK = 128

ALPHA = (2 * DEPTH) ** 0.25
BETA = (8 * DEPTH) ** -0.25
LN_EPS = 1e-5

kernel_name = 'gla_nsa_conformer_moe_step'


def split_cols(h, sizes):
    return jnp.split(h, np.cumsum(sizes)[:-1].tolist(), axis=-1)


def layer_norm(x, g, b):
    xf = x.astype(jnp.float32)
    mu = xf.mean(-1, keepdims=True)
    var = jnp.square(xf - mu).mean(-1, keepdims=True)
    return ((xf - mu) * lax.rsqrt(var + LN_EPS) * g + b).astype(x.dtype)


def rms_norm(x, g):
    xf = x.astype(jnp.float32)
    return (xf * lax.rsqrt(jnp.mean(xf * xf, -1, keepdims=True) + LN_EPS) * g).astype(x.dtype)


def partial_rope(x, pos):
    half = ROPE_DIM // 2
    inv_freq = jnp.power(ROPE_THETA, -jnp.arange(half, dtype=jnp.float32) / half)
    ang = pos.astype(jnp.float32)[:, None] * inv_freq
    ang = ang.reshape(ang.shape[0], *([1] * (x.ndim - 3)), half)
    cos, sin = jnp.cos(ang), jnp.sin(ang)
    x1 = x[..., :half].astype(jnp.float32)
    x2 = x[..., half:ROPE_DIM].astype(jnp.float32)
    rot = jnp.concatenate([x1 * cos - x2 * sin, x2 * cos + x1 * sin], -1).astype(x.dtype)
    return jnp.concatenate([rot, x[..., ROPE_DIM:]], -1)


def masked_softmax(s, mask):
    s = jnp.where(mask, s.astype(jnp.float32), -jnp.inf)
    m = jnp.max(s, axis=-1, keepdims=True)
    m = jnp.where(jnp.isfinite(m), m, 0.0)
    p = jnp.exp(s - m)
    return p / jnp.maximum(p.sum(-1, keepdims=True), 1e-30)


def gla_recurrence(q, k, v, log_a, s0):
    bsz, t_, nh, _ = q.shape
    c = math.gcd(t_, GLA_CHUNK)
    n = t_ // c

    def chunks(a):
        return jnp.moveaxis(a.astype(jnp.float32).reshape(bsz, n, c, *a.shape[2:]), 1, 0)

    causal = jnp.tril(jnp.ones((c, c), dtype=bool))[None, :, :, None, None]

    def step(s, inp):
        qc, kc, vc, lc = inp
        bc = jnp.cumsum(lc, axis=1)
        decay = jnp.exp(jnp.where(causal, bc[:, :, None] - bc[:, None, :], -jnp.inf))
        attn = jnp.einsum('bijhd,bjhd->bhij', qc[:, :, None] * decay, kc)
        o = jnp.einsum('bhij,bjhe->bihe', attn, vc) + jnp.einsum('bihd,bhde->bihe', qc * jnp.exp(bc), s)
        bl = bc[:, -1]
        s = jnp.exp(bl)[..., None] * s + jnp.einsum('bjhd,bjhe->bhde', kc * jnp.exp(bl[:, None] - bc), vc)
        return s, o

    s_fin, o = lax.scan(step, s0.astype(jnp.float32), (chunks(q), chunks(k), chunks(v), chunks(log_a)))
    return jnp.moveaxis(o, 0, 1).reshape(bsz, t_, nh, -1), s_fin


def compress(k, v, w_pool):
    bsz, length = k.shape[:2]
    n_sub = length // CMP_STRIDE

    def pool(a, w):
        sub = a[:, :n_sub * CMP_STRIDE].reshape(bsz, n_sub, CMP_STRIDE, *a.shape[2:])
        first = jnp.einsum('bnjhd,j->bnhd', sub, w[:CMP_STRIDE])
        second = jnp.einsum('bnjhd,j->bnhd', sub, w[CMP_STRIDE:])
        return first[:, :-1] + second[:, 1:]

    cend = jnp.arange(n_sub - 1) * CMP_STRIDE + CMP_BLK - 1
    return pool(k, w_pool[0].astype(k.dtype)), pool(v, w_pool[1].astype(v.dtype)), cend


def to_sel_blocks(a, n_sel):
    bsz, length = a.shape[:2]
    a = jnp.pad(a, ((0, 0), (0, n_sel * SEL_BLK - length), (0, 0), (0, 0)))
    return a.reshape(bsz, n_sel, SEL_BLK, NSA_KV_HEADS, HEAD_DIM).transpose(0, 3, 1, 2, 4)


def nsa_attend(q_raw, q_rot, qpos, gates, kc, vc, cend, ksb, vsb, kw, vw, kwpos):
    scale = HEAD_DIM ** -0.5
    bsz, tq = q_raw.shape[:2]
    n_cmp, n_sel = kc.shape[1], ksb.shape[2]
    s_c = jnp.einsum('bqhgd,bnhd->bhgqn', q_raw, kc).astype(jnp.float32) * scale
    p_c = masked_softmax(s_c, cend[None, :] <= qpos[:, None])
    o_c = jnp.einsum('bhgqn,bnhd->bqhgd', p_c.astype(vc.dtype), vc)
    ratio = SEL_BLK // CMP_STRIDE
    imp = p_c.sum(axis=2)
    imp = jnp.pad(imp, ((0, 0), (0, 0), (0, 0), (1, ratio * (n_sel + 1) - 1 - n_cmp)))
    imp = imp.reshape(bsz, NSA_KV_HEADS, tq, n_sel + 1, ratio)
    imp_s = imp[..., :n_sel, :].sum(-1) + imp[..., 1:, 0]
    blk = jnp.arange(n_sel)[None, :]
    cur = (qpos // SEL_BLK)[:, None]
    valid = blk * SEL_BLK <= qpos[:, None]
    forced = (blk == 0) | (blk == cur) | (blk == cur - 1)
    score = jnp.where(valid, imp_s + jnp.where(forced, FORCE_BONUS, 0.0), -jnp.inf)
    k_top = min(SEL_TOPN, n_sel)
    _, sel = lax.top_k(score, k_top)
    take = jax.vmap(jax.vmap(lambda blocks, idx: blocks[idx]))
    ks = take(ksb, sel).reshape(bsz, NSA_KV_HEADS, tq, k_top * SEL_BLK, HEAD_DIM)
    vs = take(vsb, sel).reshape(bsz, NSA_KV_HEADS, tq, k_top * SEL_BLK, HEAD_DIM)
    kpos = (sel[..., None] * SEL_BLK + jnp.arange(SEL_BLK)).reshape(bsz, NSA_KV_HEADS, tq, k_top * SEL_BLK)
    s_s = jnp.einsum('bqhgd,bhqkd->bhgqk', q_rot, ks).astype(jnp.float32) * scale
    p_s = masked_softmax(s_s, (kpos <= qpos[:, None])[:, :, None])
    o_s = jnp.einsum('bhgqk,bhqkd->bqhgd', p_s.astype(vs.dtype), vs)
    s_w = jnp.einsum('bqhgd,bkhd->bhgqk', q_rot, kw).astype(jnp.float32) * scale
    kp, qp = kwpos[None, :], qpos[:, None]
    p_w = masked_softmax(s_w, (kp <= qp) & (kp > qp - WINDOW) & (kp >= 0))
    o_w = jnp.einsum('bhgqk,bkhd->bqhgd', p_w.astype(vw.dtype), vw)
    out = gates[..., 0:1] * o_c + gates[..., 1:2] * o_s + gates[..., 2:3] * o_w
    return out.astype(q_raw.dtype)


def ab_mixer(x, pos, w_in, w_gla_gate, b_gla_gate, gla_norm_g, w_cmp_pool, w_out,
             gla_state, nsa_cache, page_table, win_buf):
    bsz, t_, _ = x.shape
    dt = x.dtype
    gq, gk, gv, gr, glr, nq, nkv, ngate = split_cols(x @ w_in, GLA_SIZES + NSA_SIZES)
    q_a = gq.reshape(bsz, t_, GLA_HEADS, GLA_DK) * (GLA_DK ** -0.5)
    k_a = gk.reshape(bsz, t_, GLA_HEADS, GLA_DK)
    v_a = gv.reshape(bsz, t_, GLA_HEADS, GLA_DV)
    log_a = (jax.nn.log_sigmoid((glr @ w_gla_gate + b_gla_gate).astype(jnp.float32)) / GLA_TAU
             ).reshape(bsz, t_, GLA_HEADS, GLA_DK)
    if gla_state is None:
        gla_state = jnp.zeros((bsz, GLA_HEADS, GLA_DK, GLA_DV), jnp.float32)
    o_a, s_a = gla_recurrence(q_a, k_a, v_a, log_a, gla_state)
    o_a = (rms_norm(o_a, gla_norm_g).reshape(bsz, t_, -1) * jax.nn.silu(gr.astype(jnp.float32))).astype(dt)
    q_raw = nq.reshape(bsz, t_, NSA_KV_HEADS, NSA_GROUP, HEAD_DIM)
    q_rot = partial_rope(q_raw, pos)
    kv = nkv.reshape(bsz, t_, 6, NSA_KV_HEADS, HEAD_DIM)
    k_sel = partial_rope(kv[:, :, 2], pos)
    k_win = partial_rope(kv[:, :, 4], pos)
    rows_full = jnp.stack([kv[:, :, 0], kv[:, :, 1], k_sel, kv[:, :, 3]], axis=2)
    rows_win = jnp.stack([k_win, kv[:, :, 5]], axis=2)
    gates = jax.nn.sigmoid(ngate.astype(jnp.float32)).reshape(bsz, t_, NSA_KV_HEADS, NSA_GROUP, 3)
    if nsa_cache is None:
        keys = rows_full
    else:
        past = nsa_cache[page_table].reshape(bsz, -1, 4, NSA_KV_HEADS, HEAD_DIM).astype(dt)
        keys = jnp.concatenate([past, rows_full], axis=1)
    length = keys.shape[1]
    kc, vc, cend = compress(keys[:, :, 0], keys[:, :, 1], w_cmp_pool)
    n_sel = -(-length // SEL_BLK)
    ksb = to_sel_blocks(keys[:, :, 2], n_sel)
    vsb = to_sel_blocks(keys[:, :, 3], n_sel)
    if nsa_cache is None:
        n_blk = t_ // Q_BLK
        n_shift = WINDOW // Q_BLK + 1
        kw_all = jnp.pad(rows_win, ((0, 0), (WINDOW, 0), (0, 0), (0, 0), (0, 0)))
        kw_all = kw_all.reshape(bsz, n_blk + n_shift - 1, Q_BLK, 2, NSA_KV_HEADS, HEAD_DIM)
        kw_b = jnp.concatenate([kw_all[:, j:j + n_blk] for j in range(n_shift)], axis=2)
        kwpos = jnp.arange(n_blk)[:, None] * Q_BLK - WINDOW + jnp.arange(WINDOW + Q_BLK)[None, :]

        def to_blocks(a):
            return jnp.moveaxis(a.reshape(bsz, n_blk, Q_BLK, *a.shape[2:]), 1, 0)

        def attend_block(args):
            qr, qo, qp, gt, kwb, kwp = args
            return nsa_attend(qr, qo, qp, gt, kc, vc, cend, ksb, vsb, kwb[:, :, 0], kwb[:, :, 1], kwp)

        o_b = lax.map(attend_block, (to_blocks(q_raw), to_blocks(q_rot), pos.reshape(n_blk, Q_BLK),
                                     to_blocks(gates), jnp.moveaxis(kw_b, 1, 0), kwpos))
        o_b = jnp.moveaxis(o_b, 0, 1).reshape(bsz, t_, -1)
        new_win = rows_win[:, -min(WINDOW, t_):]
    else:
        w_buf = win_buf.shape[1]
        kw = jnp.concatenate([win_buf.astype(dt), rows_win], axis=1)
        kwpos = (length - t_) - w_buf + jnp.arange(kw.shape[1])
        o_b = nsa_attend(q_raw, q_rot, pos, gates, kc, vc, cend, ksb, vsb, kw[:, :, 0], kw[:, :, 1], kwpos)
        o_b = o_b.reshape(bsz, t_, -1)
        new_win = kw[:, -w_buf:]
    y = jnp.concatenate([o_a, o_b.astype(dt)], axis=-1) @ w_out
    return y, s_a.astype(dt), rows_full, new_win


def conv_module(x, conv_buf, w_pw1, b_pw1, w_dw, b_dw, ln_g, ln_b, w_pw2, b_pw2):
    bsz = x.shape[0]
    a, g = jnp.split(x @ w_pw1 + b_pw1, 2, axis=-1)
    u = a * jax.nn.sigmoid(g)
    if conv_buf is None:
        conv_buf = jnp.zeros((bsz, CONV_W - 1, D_CONV), u.dtype)
    ext = jnp.concatenate([conv_buf.astype(u.dtype), u], axis=1)
    c = lax.conv_general_dilated(ext, w_dw[:, None, :].astype(u.dtype), (1,), 'VALID',
                                 dimension_numbers=('NWC', 'WIO', 'NWC'),
                                 feature_group_count=D_CONV) + b_dw
    c = jax.nn.silu(layer_norm(c, ln_g, ln_b))
    return c @ w_pw2 + b_pw2, ext[:, -(CONV_W - 1):]


def moe_ffn(x, w_router, b_router, w_exp_gu, w_exp_down, w_sh_gu, w_sh_down):
    bsz, t_, d = x.shape
    dt = x.dtype
    xf = x.reshape(-1, d)
    m = xf.shape[0]
    s = jax.nn.sigmoid((xf @ w_router).astype(jnp.float32))
    sb = s + b_router.astype(jnp.float32)
    per = N_EXPERTS // N_GROUPS
    gscore = lax.top_k(sb.reshape(m, N_GROUPS, per), 2)[0].sum(-1)
    _, gidx = lax.top_k(gscore, TOPK_GROUPS)
    gmask = jax.nn.one_hot(gidx, N_GROUPS, dtype=jnp.float32).sum(1) > 0
    emask = jnp.repeat(gmask, per, axis=1)
    _, eidx = lax.top_k(jnp.where(emask, sb, -jnp.inf), TOP_K)
    gate = jnp.take_along_axis(s, eidx, axis=1)
    gate = gate / gate.sum(-1, keepdims=True) * ROUTE_SCALE
    n = m * TOP_K
    e_flat = eidx.reshape(-1)
    order = jnp.argsort(e_flat)
    e_sorted = e_flat[order]
    counts = jnp.bincount(e_flat, length=N_EXPERTS)
    padded = (counts + MOE_BLK - 1) // MOE_BLK * MOE_BLK
    pad_end = jnp.cumsum(padded)
    dest = (pad_end - padded)[e_sorted] + jnp.arange(n) - (jnp.cumsum(counts) - counts)[e_sorted]
    n_blk = -(-n // MOE_BLK) + N_EXPERTS
    slot_tok = jnp.full((n_blk * MOE_BLK,), m, jnp.int32).at[dest].set((order // TOP_K).astype(jnp.int32))
    slot_gate = jnp.zeros((n_blk * MOE_BLK,), jnp.float32).at[dest].set(gate.reshape(-1)[order])
    blk_exp = jnp.minimum(jnp.searchsorted(pad_end, jnp.arange(n_blk) * MOE_BLK, side='right'), N_EXPERTS - 1)
    x_pad = jnp.concatenate([xf, jnp.zeros((1, d), dt)], axis=0)

    def expert_block(args):
        tok, e = args
        hg, hu = jnp.split(x_pad[tok] @ w_exp_gu[e], 2, axis=-1)
        return (jax.nn.silu(hg) * hu) @ w_exp_down[e]

    yb = lax.map(expert_block, (slot_tok.reshape(n_blk, MOE_BLK), blk_exp))
    routed = jnp.zeros((m + 1, d), jnp.float32).at[slot_tok].add(
        yb.reshape(-1, d).astype(jnp.float32) * slot_gate[:, None])[:m]
    sg, su = jnp.split(xf @ w_sh_gu, 2, axis=-1)
    shared = (jax.nn.silu(sg) * su) @ w_sh_down
    return (routed + shared.astype(jnp.float32)).astype(dt).reshape(bsz, t_, d)


def trunk(x, pos, gla_state, nsa_cache, page_table, win_buf, conv_buf,
          w_in_ab, w_gla_gate, b_gla_gate, gla_norm_g, w_cmp_pool, w_out_ab,
          w_pw1, b_pw1, w_dw, b_dw, conv_ln_g, conv_ln_b, w_pw2, b_pw2,
          ln_g, ln_b, w_router, b_router, w_exp_gu, w_exp_down, w_sh_gu, w_sh_down):
    new_gla, new_rows, new_win, new_conv = [], [], [], []
    for layer in range(DEPTH):
        i = layer // 2
        if layer % 2 == 0:
            mix, s_a, rows, win = ab_mixer(
                x, pos, w_in_ab[i], w_gla_gate[i], b_gla_gate[i], gla_norm_g[i], w_cmp_pool[i], w_out_ab[i],
                None if gla_state is None else gla_state[i],
                None if nsa_cache is None else nsa_cache[i], page_table,
                None if win_buf is None else win_buf[i])
            new_gla.append(s_a)
            new_rows.append(rows)
            new_win.append(win)
        else:
            mix, cb = conv_module(x, None if conv_buf is None else conv_buf[i], w_pw1[i], b_pw1[i],
                                  w_dw[i], b_dw[i], conv_ln_g[i], conv_ln_b[i], w_pw2[i], b_pw2[i])
            new_conv.append(cb)
        x = layer_norm(ALPHA * x + mix, ln_g[layer, 0], ln_b[layer, 0])
        ffn = moe_ffn(x, w_router[layer], b_router[layer], w_exp_gu[layer], w_exp_down[layer],
                      w_sh_gu[layer], w_sh_down[layer])
        x = layer_norm(ALPHA * x + ffn, ln_g[layer, 1], ln_b[layer, 1])
    return x, jnp.stack(new_gla), jnp.stack(new_rows), jnp.stack(new_win), jnp.stack(new_conv)


def setup_inputs(seed: int = 0) -> dict:
    key = jax.random.key(seed)
    ks = iter(jax.random.split(key, 32))

    def nrm(shape, scale):
        return jax.random.normal(next(ks), shape, jnp.float32) * scale

    n_pages = PAST_LEN // PAGE_SIZE
    n_used = DEC_BATCH * n_pages
    n_pool = n_used + max(1, n_used // 4)
    w_buf = min(WINDOW, PAST_LEN)
    return {
        'x_prompt': nrm((BATCH, SEQ, D_MODEL), 1.0),
        'x_sample': nrm((DEC_BATCH, DEC_SEQ, D_MODEL), 1.0),
        'state_gla': nrm((N_EVEN, DEC_BATCH, GLA_HEADS, GLA_DK, GLA_DV), 0.5),
        'cache_nsa_kv': nrm((N_EVEN, n_pool, PAGE_SIZE, 4, NSA_KV_HEADS, HEAD_DIM), 1.0),
        'state_nsa_win': nrm((N_EVEN, DEC_BATCH, w_buf, 2, NSA_KV_HEADS, HEAD_DIM), 1.0),
        'state_conv': nrm((N_ODD, DEC_BATCH, CONV_W - 1, D_CONV), 0.5),
        'page_table': jax.random.permutation(next(ks), n_pool)[:n_used].reshape(DEC_BATCH, n_pages).astype(jnp.int32),
        'w_in_ab': nrm((N_EVEN, D_MODEL, IN_AB), D_MODEL ** -0.5),
        'w_gla_gate': nrm((N_EVEN, GLA_RANK, GLA_HEADS * GLA_DK), GLA_RANK ** -0.5),
        'b_gla_gate': nrm((N_EVEN, GLA_HEADS * GLA_DK), 0.1),
        'gla_norm_g': 1.0 + nrm((N_EVEN, GLA_DV), 0.02),
        'w_cmp_pool': (1.0 + nrm((N_EVEN, 2, CMP_BLK), 0.1)) / CMP_BLK,
        'w_out_ab': nrm((N_EVEN, MIX_AB, D_MODEL), MIX_AB ** -0.5 * BETA),
        'w_pw1': nrm((N_ODD, D_MODEL, 2 * D_CONV), D_MODEL ** -0.5),
        'b_pw1': nrm((N_ODD, 2 * D_CONV), 0.02),
        'w_dw': nrm((N_ODD, CONV_W, D_CONV), CONV_W ** -0.5),
        'b_dw': nrm((N_ODD, D_CONV), 0.02),
        'conv_ln_g': 1.0 + nrm((N_ODD, D_CONV), 0.02),
        'conv_ln_b': nrm((N_ODD, D_CONV), 0.02),
        'w_pw2': nrm((N_ODD, D_CONV, D_MODEL), D_CONV ** -0.5 * BETA),
        'b_pw2': nrm((N_ODD, D_MODEL), 0.02),
        'ln_g': 1.0 + nrm((DEPTH, 2, D_MODEL), 0.02),
        'ln_b': nrm((DEPTH, 2, D_MODEL), 0.02),
        'w_router': nrm((DEPTH, D_MODEL, N_EXPERTS), D_MODEL ** -0.5),
        'b_router': nrm((DEPTH, N_EXPERTS), 0.01),
        'w_exp_gu': nrm((DEPTH, N_EXPERTS, D_MODEL, 2 * D_EXPERT), D_MODEL ** -0.5),
        'w_exp_down': nrm((DEPTH, N_EXPERTS, D_EXPERT, D_MODEL), D_EXPERT ** -0.5 * BETA),
        'w_sh_gu': nrm((DEPTH, D_MODEL, 2 * D_SHARED), D_MODEL ** -0.5),
        'w_sh_down': nrm((DEPTH, D_SHARED, D_MODEL), D_SHARED ** -0.5 * BETA),
    }


def reference(x_prompt, x_sample, state_gla, cache_nsa_kv, state_nsa_win, state_conv, page_table,
              w_in_ab, w_gla_gate, b_gla_gate, gla_norm_g, w_cmp_pool, w_out_ab,
              w_pw1, b_pw1, w_dw, b_dw, conv_ln_g, conv_ln_b, w_pw2, b_pw2,
              ln_g, ln_b, w_router, b_router, w_exp_gu, w_exp_down, w_sh_gu, w_sh_down):
    weights = (w_in_ab, w_gla_gate, b_gla_gate, gla_norm_g, w_cmp_pool, w_out_ab,
               w_pw1, b_pw1, w_dw, b_dw, conv_ln_g, conv_ln_b, w_pw2, b_pw2,
               ln_g, ln_b, w_router, b_router, w_exp_gu, w_exp_down, w_sh_gu, w_sh_down)
    past_len = page_table.shape[1] * PAGE_SIZE
    pos_p = jnp.arange(x_prompt.shape[1])
    pos_s = past_len + jnp.arange(x_sample.shape[1])
    y_prompt, gla_p, rows_p, win_p, conv_p = trunk(x_prompt, pos_p, None, None, None, None, None, *weights)
    y_sample, gla_s, rows_s, win_s, conv_s = trunk(x_sample, pos_s, state_gla, cache_nsa_kv, page_table,
                                                   state_nsa_win, state_conv, *weights)
    return (y_prompt, y_sample, gla_p, gla_s, rows_p, rows_s, win_p, win_s, conv_p, conv_s)
```

```python
import functools
import math

import jax
import jax.numpy as jnp
import numpy as np
from jax import lax
from jax.experimental import pallas as pl
from jax.experimental.pallas import tpu as pltpu

D_MODEL = 1024
DEPTH = 2
PAGE_SIZE = 128

GLA_HEADS = 4
GLA_DV = D_MODEL // 2 // GLA_HEADS
GLA_DK = GLA_DV // 2
GLA_RANK = 16
GLA_TAU = 16.0
GLA_CHUNK = 64

NSA_HEADS = 8
NSA_KV_HEADS = 2
NSA_GROUP = NSA_HEADS // NSA_KV_HEADS
HEAD_DIM = D_MODEL // 2 // NSA_HEADS
CMP_BLK = 32
CMP_STRIDE = 16
SEL_BLK = 64
SEL_TOPN = 16
WINDOW = 512
Q_BLK = 128
FORCE_BONUS = 100.0
ROPE_DIM = HEAD_DIM // 4
ROPE_THETA = 500000.0

GLA_SIZES = (GLA_HEADS * GLA_DK, GLA_HEADS * GLA_DK, GLA_HEADS * GLA_DV, GLA_HEADS * GLA_DV, GLA_RANK)
NSA_SIZES = (NSA_HEADS * HEAD_DIM, 6 * NSA_KV_HEADS * HEAD_DIM, 3 * NSA_HEADS)

CONV_W = 31
D_CONV = D_MODEL

N_EXPERTS = 64
N_GROUPS = 8
TOPK_GROUPS = 4
TOP_K = 8
D_EXPERT = 256
ROUTE_SCALE = 2.5
MOE_BLK = 128

ALPHA = (2 * DEPTH) ** 0.25
LN_EPS = 1e-5

LANE = 128


def _mm_body(x_ref, w_ref, o_ref):
    o_ref[...] = jnp.dot(x_ref[...].astype(jnp.bfloat16), w_ref[...].astype(jnp.bfloat16),
                         preferred_element_type=jnp.float32)


def _mm(x, w):
    m, k = x.shape
    n = w.shape[1]
    n_pad = -(-n // LANE) * LANE
    if n_pad != n:
        w = jnp.pad(w, ((0, 0), (0, n_pad - n)))
    tm = min(m, 512)
    tn = next(t for t in (512, 256, 128) if n_pad % t == 0)
    out = pl.pallas_call(
        _mm_body,
        grid=(m // tm, n_pad // tn),
        in_specs=[pl.BlockSpec((tm, k), lambda i, j: (i, 0)),
                  pl.BlockSpec((k, tn), lambda i, j: (0, j))],
        out_specs=pl.BlockSpec((tm, tn), lambda i, j: (i, j)),
        out_shape=jax.ShapeDtypeStruct((m, n_pad), jnp.float32),
        compiler_params=pltpu.CompilerParams(dimension_semantics=("arbitrary", "arbitrary")),
        name="mm",
    )(x, w)
    return out[:, :n] if n_pad != n else out


def _mm3(x, w):
    b, t, d = x.shape
    return _mm(x.reshape(b * t, d), w).reshape(b, t, -1)


def _split_cols(h, sizes):
    return jnp.split(h, np.cumsum(sizes)[:-1].tolist(), axis=-1)


def _layer_norm(x, g, b):
    mu = x.mean(-1, keepdims=True)
    var = jnp.square(x - mu).mean(-1, keepdims=True)
    return (x - mu) * lax.rsqrt(var + LN_EPS) * g + b


def _rms_norm(x, g):
    return x * lax.rsqrt(jnp.mean(x * x, -1, keepdims=True) + LN_EPS) * g


def _partial_rope(x, pos):
    half = ROPE_DIM // 2
    inv_freq = jnp.power(ROPE_THETA, -jnp.arange(half, dtype=jnp.float32) / half)
    ang = pos.astype(jnp.float32)[:, None] * inv_freq
    ang = ang.reshape(ang.shape[0], *([1] * (x.ndim - 3)), half)
    cos, sin = jnp.cos(ang), jnp.sin(ang)
    x1 = x[..., :half]
    x2 = x[..., half:ROPE_DIM]
    rot = jnp.concatenate([x1 * cos - x2 * sin, x2 * cos + x1 * sin], -1)
    return jnp.concatenate([rot, x[..., ROPE_DIM:]], -1)


def _masked_softmax(s, mask):
    s = jnp.where(mask, s, -jnp.inf)
    m = jnp.max(s, axis=-1, keepdims=True)
    m = jnp.where(jnp.isfinite(m), m, 0.0)
    p = jnp.exp(s - m)
    return p / jnp.maximum(p.sum(-1, keepdims=True), 1e-30)


def _gla_recurrence(q, k, v, log_a, s0):
    bsz, t_, nh, _ = q.shape
    c = math.gcd(t_, GLA_CHUNK)
    n = t_ // c

    def chunks(a):
        return jnp.moveaxis(a.reshape(bsz, n, c, *a.shape[2:]), 1, 0)

    causal = jnp.tril(jnp.ones((c, c), dtype=bool))[None, :, :, None, None]

    def step(s, inp):
        qc, kc, vc, lc = inp
        bc = jnp.cumsum(lc, axis=1)
        decay = jnp.exp(jnp.where(causal, bc[:, :, None] - bc[:, None, :], -jnp.inf))
        attn = jnp.einsum('bijhd,bjhd->bhij', qc[:, :, None] * decay, kc)
        o = jnp.einsum('bhij,bjhe->bihe', attn, vc) + jnp.einsum('bihd,bhde->bihe', qc * jnp.exp(bc), s)
        bl = bc[:, -1]
        s = jnp.exp(bl)[..., None] * s + jnp.einsum('bjhd,bjhe->bhde', kc * jnp.exp(bl[:, None] - bc), vc)
        return s, o

    s_fin, o = lax.scan(step, s0, (chunks(q), chunks(k), chunks(v), chunks(log_a)))
    return jnp.moveaxis(o, 0, 1).reshape(bsz, t_, nh, -1), s_fin


def _compress(k, v, w_pool):
    bsz, length = k.shape[:2]
    n_sub = length // CMP_STRIDE

    def pool(a, w):
        sub = a[:, :n_sub * CMP_STRIDE].reshape(bsz, n_sub, CMP_STRIDE, *a.shape[2:])
        first = jnp.einsum('bnjhd,j->bnhd', sub, w[:CMP_STRIDE])
        second = jnp.einsum('bnjhd,j->bnhd', sub, w[CMP_STRIDE:])
        return first[:, :-1] + second[:, 1:]

    cend = jnp.arange(n_sub - 1) * CMP_STRIDE + CMP_BLK - 1
    return pool(k, w_pool[0]), pool(v, w_pool[1]), cend


def _to_sel_blocks(a, n_sel):
    bsz, length = a.shape[:2]
    a = jnp.pad(a, ((0, 0), (0, n_sel * SEL_BLK - length), (0, 0), (0, 0)))
    return a.reshape(bsz, n_sel, SEL_BLK, NSA_KV_HEADS, HEAD_DIM).transpose(0, 3, 1, 2, 4)


def _nsa_attend(q_raw, q_rot, qpos, gates, kc, vc, cend, ksb, vsb, kw, vw, kwpos):
    scale = HEAD_DIM ** -0.5
    bsz, tq = q_raw.shape[:2]
    n_cmp, n_sel = kc.shape[1], ksb.shape[2]
    s_c = jnp.einsum('bqhgd,bnhd->bhgqn', q_raw, kc) * scale
    p_c = _masked_softmax(s_c, cend[None, :] <= qpos[:, None])
    o_c = jnp.einsum('bhgqn,bnhd->bqhgd', p_c, vc)
    ratio = SEL_BLK // CMP_STRIDE
    imp = p_c.sum(axis=2)
    imp = jnp.pad(imp, ((0, 0), (0, 0), (0, 0), (1, ratio * (n_sel + 1) - 1 - n_cmp)))
    imp = imp.reshape(bsz, NSA_KV_HEADS, tq, n_sel + 1, ratio)
    imp_s = imp[..., :n_sel, :].sum(-1) + imp[..., 1:, 0]
    blk = jnp.arange(n_sel)[None, :]
    cur = (qpos // SEL_BLK)[:, None]
    valid = blk * SEL_BLK <= qpos[:, None]
    forced = (blk == 0) | (blk == cur) | (blk == cur - 1)
    score = jnp.where(valid, imp_s + jnp.where(forced, FORCE_BONUS, 0.0), -jnp.inf)
    k_top = min(SEL_TOPN, n_sel)
    _, sel = lax.top_k(score, k_top)
    take = jax.vmap(jax.vmap(lambda blocks, idx: blocks[idx]))
    ks = take(ksb, sel).reshape(bsz, NSA_KV_HEADS, tq, k_top * SEL_BLK, HEAD_DIM)
    vs = take(vsb, sel).reshape(bsz, NSA_KV_HEADS, tq, k_top * SEL_BLK, HEAD_DIM)
    kpos = (sel[..., None] * SEL_BLK + jnp.arange(SEL_BLK)).reshape(bsz, NSA_KV_HEADS, tq, k_top * SEL_BLK)
    s_s = jnp.einsum('bqhgd,bhqkd->bhgqk', q_rot, ks) * scale
    p_s = _masked_softmax(s_s, (kpos <= qpos[:, None])[:, :, None])
    o_s = jnp.einsum('bhgqk,bhqkd->bqhgd', p_s, vs)
    s_w = jnp.einsum('bqhgd,bkhd->bhgqk', q_rot, kw) * scale
    kp, qp = kwpos[None, :], qpos[:, None]
    p_w = _masked_softmax(s_w, (kp <= qp) & (kp > qp - WINDOW) & (kp >= 0))
    o_w = jnp.einsum('bhgqk,bkhd->bqhgd', p_w, vw)
    return gates[..., 0:1] * o_c + gates[..., 1:2] * o_s + gates[..., 2:3] * o_w


def _ab_mixer(x, pos, w_in, w_gla_gate, b_gla_gate, gla_norm_g, w_cmp_pool, w_out,
              gla_state, nsa_cache, page_table, win_buf):
    bsz, t_, _ = x.shape
    gq, gk, gv, gr, glr, nq, nkv, ngate = _split_cols(_mm3(x, w_in), GLA_SIZES + NSA_SIZES)
    q_a = gq.reshape(bsz, t_, GLA_HEADS, GLA_DK) * (GLA_DK ** -0.5)
    k_a = gk.reshape(bsz, t_, GLA_HEADS, GLA_DK)
    v_a = gv.reshape(bsz, t_, GLA_HEADS, GLA_DV)
    log_a = (jax.nn.log_sigmoid(glr @ w_gla_gate + b_gla_gate) / GLA_TAU).reshape(bsz, t_, GLA_HEADS, GLA_DK)
    if gla_state is None:
        gla_state = jnp.zeros((bsz, GLA_HEADS, GLA_DK, GLA_DV), jnp.float32)
    o_a, s_a = _gla_recurrence(q_a, k_a, v_a, log_a, gla_state)
    o_a = _rms_norm(o_a, gla_norm_g).reshape(bsz, t_, -1) * jax.nn.silu(gr)
    q_raw = nq.reshape(bsz, t_, NSA_KV_HEADS, NSA_GROUP, HEAD_DIM)
    q_rot = _partial_rope(q_raw, pos)
    kv = nkv.reshape(bsz, t_, 6, NSA_KV_HEADS, HEAD_DIM)
    k_sel = _partial_rope(kv[:, :, 2], pos)
    k_win = _partial_rope(kv[:, :, 4], pos)
    rows_full = jnp.stack([kv[:, :, 0], kv[:, :, 1], k_sel, kv[:, :, 3]], axis=2)
    rows_win = jnp.stack([k_win, kv[:, :, 5]], axis=2)
    gates = jax.nn.sigmoid(ngate).reshape(bsz, t_, NSA_KV_HEADS, NSA_GROUP, 3)
    if nsa_cache is None:
        keys = rows_full
    else:
        past = nsa_cache[page_table].reshape(bsz, -1, 4, NSA_KV_HEADS, HEAD_DIM)
        keys = jnp.concatenate([past, rows_full], axis=1)
    length = keys.shape[1]
    kc, vc, cend = _compress(keys[:, :, 0], keys[:, :, 1], w_cmp_pool)
    n_sel = -(-length // SEL_BLK)
    ksb = _to_sel_blocks(keys[:, :, 2], n_sel)
    vsb = _to_sel_blocks(keys[:, :, 3], n_sel)
    if nsa_cache is None:
        n_blk = t_ // Q_BLK
        n_shift = WINDOW // Q_BLK + 1
        kw_all = jnp.pad(rows_win, ((0, 0), (WINDOW, 0), (0, 0), (0, 0), (0, 0)))
        kw_all = kw_all.reshape(bsz, n_blk + n_shift - 1, Q_BLK, 2, NSA_KV_HEADS, HEAD_DIM)
        kw_b = jnp.concatenate([kw_all[:, j:j + n_blk] for j in range(n_shift)], axis=2)
        kwpos = jnp.arange(n_blk)[:, None] * Q_BLK - WINDOW + jnp.arange(WINDOW + Q_BLK)[None, :]

        def to_blocks(a):
            return jnp.moveaxis(a.reshape(bsz, n_blk, Q_BLK, *a.shape[2:]), 1, 0)

        def attend_block(args):
            qr, qo, qp, gt, kwb, kwp = args
            return _nsa_attend(qr, qo, qp, gt, kc, vc, cend, ksb, vsb, kwb[:, :, 0], kwb[:, :, 1], kwp)

        o_b = lax.map(attend_block, (to_blocks(q_raw), to_blocks(q_rot), pos.reshape(n_blk, Q_BLK),
                                     to_blocks(gates), jnp.moveaxis(kw_b, 1, 0), kwpos))
        o_b = jnp.moveaxis(o_b, 0, 1).reshape(bsz, t_, -1)
        new_win = rows_win[:, -min(WINDOW, t_):]
    else:
        w_buf = win_buf.shape[1]
        kw = jnp.concatenate([win_buf, rows_win], axis=1)
        kwpos = (length - t_) - w_buf + jnp.arange(kw.shape[1])
        o_b = _nsa_attend(q_raw, q_rot, pos, gates, kc, vc, cend, ksb, vsb, kw[:, :, 0], kw[:, :, 1], kwpos)
        o_b = o_b.reshape(bsz, t_, -1)
        new_win = kw[:, -w_buf:]
    y = _mm3(jnp.concatenate([o_a, o_b], axis=-1), w_out)
    return y, s_a, rows_full, new_win


def _conv_module(x, conv_buf, w_pw1, b_pw1, w_dw, b_dw, ln_g, ln_b, w_pw2, b_pw2):
    bsz = x.shape[0]
    a, g = jnp.split(_mm3(x, w_pw1) + b_pw1, 2, axis=-1)
    u = a * jax.nn.sigmoid(g)
    if conv_buf is None:
        conv_buf = jnp.zeros((bsz, CONV_W - 1, D_CONV), u.dtype)
    ext = jnp.concatenate([conv_buf, u], axis=1)
    c = lax.conv_general_dilated(ext, w_dw[:, None, :], (1,), 'VALID',
                                 dimension_numbers=('NWC', 'WIO', 'NWC'),
                                 feature_group_count=D_CONV) + b_dw
    c = jax.nn.silu(_layer_norm(c, ln_g, ln_b))
    return _mm3(c, w_pw2) + b_pw2, ext[:, -(CONV_W - 1):]


def _moe_ffn(x, w_router, b_router, w_exp_gu, w_exp_down, w_sh_gu, w_sh_down):
    bsz, t_, d = x.shape
    xf = x.reshape(-1, d)
    m = xf.shape[0]
    s = jax.nn.sigmoid(_mm(xf, w_router))
    sb = s + b_router
    per = N_EXPERTS // N_GROUPS
    gscore = lax.top_k(sb.reshape(m, N_GROUPS, per), 2)[0].sum(-1)
    _, gidx = lax.top_k(gscore, TOPK_GROUPS)
    gmask = jax.nn.one_hot(gidx, N_GROUPS, dtype=jnp.float32).sum(1) > 0
    emask = jnp.repeat(gmask, per, axis=1)
    _, eidx = lax.top_k(jnp.where(emask, sb, -jnp.inf), TOP_K)
    gate = jnp.take_along_axis(s, eidx, axis=1)
    gate = gate / gate.sum(-1, keepdims=True) * ROUTE_SCALE
    n = m * TOP_K
    e_flat = eidx.reshape(-1)
    order = jnp.argsort(e_flat)
    e_sorted = e_flat[order]
    counts = jnp.bincount(e_flat, length=N_EXPERTS)
    padded = (counts + MOE_BLK - 1) // MOE_BLK * MOE_BLK
    pad_end = jnp.cumsum(padded)
    dest = (pad_end - padded)[e_sorted] + jnp.arange(n) - (jnp.cumsum(counts) - counts)[e_sorted]
    n_blk = -(-n // MOE_BLK) + N_EXPERTS
    slot_tok = jnp.full((n_blk * MOE_BLK,), m, jnp.int32).at[dest].set((order // TOP_K).astype(jnp.int32))
    slot_gate = jnp.zeros((n_blk * MOE_BLK,), jnp.float32).at[dest].set(gate.reshape(-1)[order])
    blk_exp = jnp.minimum(jnp.searchsorted(pad_end, jnp.arange(n_blk) * MOE_BLK, side='right'), N_EXPERTS - 1)
    x_pad = jnp.concatenate([xf, jnp.zeros((1, d), xf.dtype)], axis=0)

    def expert_block(args):
        tok, e = args
        hg, hu = jnp.split(x_pad[tok] @ w_exp_gu[e], 2, axis=-1)
        return (jax.nn.silu(hg) * hu) @ w_exp_down[e]

    yb = lax.map(expert_block, (slot_tok.reshape(n_blk, MOE_BLK), blk_exp))
    routed = jnp.zeros((m + 1, d), jnp.float32).at[slot_tok].add(yb.reshape(-1, d) * slot_gate[:, None])[:m]
    sg, su = jnp.split(_mm(xf, w_sh_gu), 2, axis=-1)
    shared = _mm(jax.nn.silu(sg) * su, w_sh_down)
    return (routed + shared).reshape(bsz, t_, d)


def _trunk(x, pos, gla_state, nsa_cache, page_table, win_buf, conv_buf,
           w_in_ab, w_gla_gate, b_gla_gate, gla_norm_g, w_cmp_pool, w_out_ab,
           w_pw1, b_pw1, w_dw, b_dw, conv_ln_g, conv_ln_b, w_pw2, b_pw2,
           ln_g, ln_b, w_router, b_router, w_exp_gu, w_exp_down, w_sh_gu, w_sh_down):
    new_gla, new_rows, new_win, new_conv = [], [], [], []
    for layer in range(DEPTH):
        i = layer // 2
        if layer % 2 == 0:
            mix, s_a, rows, win = _ab_mixer(
                x, pos, w_in_ab[i], w_gla_gate[i], b_gla_gate[i], gla_norm_g[i], w_cmp_pool[i], w_out_ab[i],
                None if gla_state is None else gla_state[i],
                None if nsa_cache is None else nsa_cache[i], page_table,
                None if win_buf is None else win_buf[i])
            new_gla.append(s_a)
            new_rows.append(rows)
            new_win.append(win)
        else:
            mix, cb = _conv_module(x, None if conv_buf is None else conv_buf[i], w_pw1[i], b_pw1[i],
                                   w_dw[i], b_dw[i], conv_ln_g[i], conv_ln_b[i], w_pw2[i], b_pw2[i])
            new_conv.append(cb)
        x = _layer_norm(ALPHA * x + mix, ln_g[layer, 0], ln_b[layer, 0])
        ffn = _moe_ffn(x, w_router[layer], b_router[layer], w_exp_gu[layer], w_exp_down[layer],
                       w_sh_gu[layer], w_sh_down[layer])
        x = _layer_norm(ALPHA * x + ffn, ln_g[layer, 1], ln_b[layer, 1])
    return x, jnp.stack(new_gla), jnp.stack(new_rows), jnp.stack(new_win), jnp.stack(new_conv)


def kernel(x_prompt, x_sample, state_gla, cache_nsa_kv, state_nsa_win, state_conv, page_table,
           w_in_ab, w_gla_gate, b_gla_gate, gla_norm_g, w_cmp_pool, w_out_ab,
           w_pw1, b_pw1, w_dw, b_dw, conv_ln_g, conv_ln_b, w_pw2, b_pw2,
           ln_g, ln_b, w_router, b_router, w_exp_gu, w_exp_down, w_sh_gu, w_sh_down):
    weights = (w_in_ab, w_gla_gate, b_gla_gate, gla_norm_g, w_cmp_pool, w_out_ab,
               w_pw1, b_pw1, w_dw, b_dw, conv_ln_g, conv_ln_b, w_pw2, b_pw2,
               ln_g, ln_b, w_router, b_router, w_exp_gu, w_exp_down, w_sh_gu, w_sh_down)
    past_len = page_table.shape[1] * PAGE_SIZE
    pos_p = jnp.arange(x_prompt.shape[1])
    pos_s = past_len + jnp.arange(x_sample.shape[1])
    y_prompt, gla_p, rows_p, win_p, conv_p = _trunk(x_prompt, pos_p, None, None, None, None, None, *weights)
    y_sample, gla_s, rows_s, win_s, conv_s = _trunk(x_sample, pos_s, state_gla, cache_nsa_kv, page_table,
                                                    state_nsa_win, state_conv, *weights)
    return (y_prompt, y_sample, gla_p, gla_s, rows_p, rows_s, win_p, win_s, conv_p, conv_s)
```

```python
import functools
import math

import jax
import jax.numpy as jnp
import numpy as np
from jax import lax
from jax.experimental import pallas as pl
from jax.experimental.pallas import tpu as pltpu

D_MODEL = 1024
DEPTH = 2
PAGE_SIZE = 128

GLA_HEADS = 4
GLA_DV = D_MODEL // 2 // GLA_HEADS
GLA_DK = GLA_DV // 2
GLA_RANK = 16
GLA_TAU = 16.0
GLA_CHUNK = 64

NSA_HEADS = 8
NSA_KV_HEADS = 2
NSA_GROUP = NSA_HEADS // NSA_KV_HEADS
HEAD_DIM = D_MODEL // 2 // NSA_HEADS
CMP_BLK = 32
CMP_STRIDE = 16
SEL_BLK = 64
SEL_TOPN = 16
WINDOW = 512
Q_BLK = 128
FORCE_BONUS = 100.0
ROPE_DIM = HEAD_DIM // 4
ROPE_THETA = 500000.0

GLA_SIZES = (GLA_HEADS * GLA_DK, GLA_HEADS * GLA_DK, GLA_HEADS * GLA_DV, GLA_HEADS * GLA_DV, GLA_RANK)
NSA_SIZES = (NSA_HEADS * HEAD_DIM, 6 * NSA_KV_HEADS * HEAD_DIM, 3 * NSA_HEADS)

CONV_W = 31
D_CONV = D_MODEL

N_EXPERTS = 64
N_GROUPS = 8
TOPK_GROUPS = 4
TOP_K = 8
D_EXPERT = 256
ROUTE_SCALE = 2.5
MOE_BLK = 128

ALPHA = (2 * DEPTH) ** 0.25
LN_EPS = 1e-5

LANE = 128


def _dot(a, b):
    return jnp.dot(a, b, preferred_element_type=jnp.float32)


def _dot_nt(a, b):
    return lax.dot_general(a, b, (((1,), (1,)), ((), ())), preferred_element_type=jnp.float32)


def _mm_body(x_ref, w_ref, o_ref):
    o_ref[...] = _dot(x_ref[...].astype(jnp.bfloat16), w_ref[...].astype(jnp.bfloat16))


def _mm(x, w):
    m, k = x.shape
    n = w.shape[1]
    n_pad = -(-n // LANE) * LANE
    if n_pad != n:
        w = jnp.pad(w, ((0, 0), (0, n_pad - n)))
    tm = min(m, 512)
    tn = next(t for t in (512, 256, 128) if n_pad % t == 0)
    out = pl.pallas_call(
        _mm_body,
        grid=(m // tm, n_pad // tn),
        in_specs=[pl.BlockSpec((tm, k), lambda i, j: (i, 0)),
                  pl.BlockSpec((k, tn), lambda i, j: (0, j))],
        out_specs=pl.BlockSpec((tm, tn), lambda i, j: (i, j)),
        out_shape=jax.ShapeDtypeStruct((m, n_pad), jnp.float32),
        compiler_params=pltpu.CompilerParams(dimension_semantics=("arbitrary", "arbitrary")),
        name="mm",
    )(x, w)
    return out[:, :n] if n_pad != n else out


def _mm3(x, w):
    b, t, d = x.shape
    return _mm(x.reshape(b * t, d), w).reshape(b, t, -1)


def _split_cols(h, sizes):
    return jnp.split(h, np.cumsum(sizes)[:-1].tolist(), axis=-1)


def _layer_norm(x, g, b):
    mu = x.mean(-1, keepdims=True)
    var = jnp.square(x - mu).mean(-1, keepdims=True)
    return (x - mu) * lax.rsqrt(var + LN_EPS) * g + b


def _rms_norm(x, g):
    return x * lax.rsqrt(jnp.mean(x * x, -1, keepdims=True) + LN_EPS) * g


def _partial_rope(x, pos):
    half = ROPE_DIM // 2
    inv_freq = jnp.power(ROPE_THETA, -jnp.arange(half, dtype=jnp.float32) / half)
    ang = pos.astype(jnp.float32)[:, None] * inv_freq
    ang = ang.reshape(ang.shape[0], *([1] * (x.ndim - 3)), half)
    cos, sin = jnp.cos(ang), jnp.sin(ang)
    x1 = x[..., :half]
    x2 = x[..., half:ROPE_DIM]
    rot = jnp.concatenate([x1 * cos - x2 * sin, x2 * cos + x1 * sin], -1)
    return jnp.concatenate([rot, x[..., ROPE_DIM:]], -1)


def _masked_softmax(s, mask):
    s = jnp.where(mask, s, -jnp.inf)
    m = jnp.max(s, axis=-1, keepdims=True)
    m = jnp.where(jnp.isfinite(m), m, 0.0)
    p = jnp.exp(s - m)
    return p / jnp.maximum(p.sum(-1, keepdims=True), 1e-30)


def _gla_recurrence(q, k, v, log_a, s0):
    bsz, t_, nh, _ = q.shape
    c = math.gcd(t_, GLA_CHUNK)
    n = t_ // c

    def chunks(a):
        return jnp.moveaxis(a.reshape(bsz, n, c, *a.shape[2:]), 1, 0)

    causal = jnp.tril(jnp.ones((c, c), dtype=bool))[None, :, :, None, None]

    def step(s, inp):
        qc, kc, vc, lc = inp
        bc = jnp.cumsum(lc, axis=1)
        decay = jnp.exp(jnp.where(causal, bc[:, :, None] - bc[:, None, :], -jnp.inf))
        attn = jnp.einsum('bijhd,bjhd->bhij', qc[:, :, None] * decay, kc)
        o = jnp.einsum('bhij,bjhe->bihe', attn, vc) + jnp.einsum('bihd,bhde->bihe', qc * jnp.exp(bc), s)
        bl = bc[:, -1]
        s = jnp.exp(bl)[..., None] * s + jnp.einsum('bjhd,bjhe->bhde', kc * jnp.exp(bl[:, None] - bc), vc)
        return s, o

    s_fin, o = lax.scan(step, s0, (chunks(q), chunks(k), chunks(v), chunks(log_a)))
    return jnp.moveaxis(o, 0, 1).reshape(bsz, t_, nh, -1), s_fin


def _compress(k, v, w_pool):
    bsz, length = k.shape[:2]
    n_sub = length // CMP_STRIDE

    def pool(a, w):
        sub = a[:, :n_sub * CMP_STRIDE].reshape(bsz, n_sub, CMP_STRIDE, *a.shape[2:])
        first = jnp.einsum('bnjhd,j->bnhd', sub, w[:CMP_STRIDE])
        second = jnp.einsum('bnjhd,j->bnhd', sub, w[CMP_STRIDE:])
        return first[:, :-1] + second[:, 1:]

    cend = jnp.arange(n_sub - 1) * CMP_STRIDE + CMP_BLK - 1
    return pool(k, w_pool[0]), pool(v, w_pool[1]), cend


def _to_sel_blocks(a, n_sel):
    bsz, length = a.shape[:2]
    a = jnp.pad(a, ((0, 0), (0, n_sel * SEL_BLK - length), (0, 0), (0, 0)))
    return a.reshape(bsz, n_sel, SEL_BLK, NSA_KV_HEADS, HEAD_DIM).transpose(0, 3, 1, 2, 4)


def _nsa_attend(q_raw, q_rot, qpos, gates, kc, vc, cend, ksb, vsb, kw, vw, kwpos):
    scale = HEAD_DIM ** -0.5
    bsz, tq = q_raw.shape[:2]
    n_cmp, n_sel = kc.shape[1], ksb.shape[2]
    s_c = jnp.einsum('bqhgd,bnhd->bhgqn', q_raw, kc) * scale
    p_c = _masked_softmax(s_c, cend[None, :] <= qpos[:, None])
    o_c = jnp.einsum('bhgqn,bnhd->bqhgd', p_c, vc)
    ratio = SEL_BLK // CMP_STRIDE
    imp = p_c.sum(axis=2)
    imp = jnp.pad(imp, ((0, 0), (0, 0), (0, 0), (1, ratio * (n_sel + 1) - 1 - n_cmp)))
    imp = imp.reshape(bsz, NSA_KV_HEADS, tq, n_sel + 1, ratio)
    imp_s = imp[..., :n_sel, :].sum(-1) + imp[..., 1:, 0]
    blk = jnp.arange(n_sel)[None, :]
    cur = (qpos // SEL_BLK)[:, None]
    valid = blk * SEL_BLK <= qpos[:, None]
    forced = (blk == 0) | (blk == cur) | (blk == cur - 1)
    score = jnp.where(valid, imp_s + jnp.where(forced, FORCE_BONUS, 0.0), -jnp.inf)
    k_top = min(SEL_TOPN, n_sel)
    _, sel = lax.top_k(score, k_top)
    take = jax.vmap(jax.vmap(lambda blocks, idx: blocks[idx]))
    ks = take(ksb, sel).reshape(bsz, NSA_KV_HEADS, tq, k_top * SEL_BLK, HEAD_DIM)
    vs = take(vsb, sel).reshape(bsz, NSA_KV_HEADS, tq, k_top * SEL_BLK, HEAD_DIM)
    kpos = (sel[..., None] * SEL_BLK + jnp.arange(SEL_BLK)).reshape(bsz, NSA_KV_HEADS, tq, k_top * SEL_BLK)
    s_s = jnp.einsum('bqhgd,bhqkd->bhgqk', q_rot, ks) * scale
    p_s = _masked_softmax(s_s, (kpos <= qpos[:, None])[:, :, None])
    o_s = jnp.einsum('bhgqk,bhqkd->bqhgd', p_s, vs)
    s_w = jnp.einsum('bqhgd,bkhd->bhgqk', q_rot, kw) * scale
    kp, qp = kwpos[None, :], qpos[:, None]
    p_w = _masked_softmax(s_w, (kp <= qp) & (kp > qp - WINDOW) & (kp >= 0))
    o_w = jnp.einsum('bhgqk,bkhd->bqhgd', p_w, vw)
    return gates[..., 0:1] * o_c + gates[..., 1:2] * o_s + gates[..., 2:3] * o_w


NSA_ROWS = NSA_GROUP * Q_BLK
SEL_KT = 512
N_SELB = 128
MASKED = -1e9
WIN_KEYS = WINDOW + Q_BLK


def _nsa_prompt_body(qr_ref, qo_ref, kc_ref, vct_ref, kk_ref, vv_ref, g_ref, o_ref,
                     imp_ref, m_ref, l_ref, acc_ref):
    f32, bf16 = jnp.float32, jnp.bfloat16
    qb = pl.program_id(2)
    q0 = qb * Q_BLK
    qr = qr_ref[0, 0, 0]
    qo = qo_ref[0, 0, 0]
    n_cmp = kc_ref.shape[2]

    s_c = _dot_nt(kc_ref[0, 0], qr)
    n_idx = lax.broadcasted_iota(jnp.int32, (n_cmp, NSA_ROWS), 0)
    qpos_c = q0 + (lax.broadcasted_iota(jnp.int32, (n_cmp, NSA_ROWS), 1) & (Q_BLK - 1))
    cmask = (n_idx * CMP_STRIDE + (CMP_BLK - 1)) <= qpos_c
    s_c = jnp.where(cmask, s_c, MASKED)
    m_c = jnp.max(s_c, axis=0, keepdims=True)
    p_c = jnp.where(cmask, jnp.exp(s_c - m_c), 0.0)
    p_c = p_c / jnp.maximum(jnp.sum(p_c, axis=0, keepdims=True), 1e-30)
    o_ct = _dot(vct_ref[0, 0], p_c.astype(bf16))

    imp = (p_c[:, 0:Q_BLK] + p_c[:, Q_BLK:2 * Q_BLK]) + p_c[:, 2 * Q_BLK:3 * Q_BLK] + p_c[:, 3 * Q_BLK:]
    imp_ref[0:8, :] = jnp.zeros((8, Q_BLK), f32)
    imp_ref[8:8 + n_cmp, :] = imp
    ratio = SEL_BLK // CMP_STRIDE
    n_selb = n_cmp // ratio
    imp_s = imp_ref[pl.ds(7, n_selb, stride=ratio), :]
    for r in range(ratio):
        imp_s = imp_s + imp_ref[pl.ds(8 + r, n_selb, stride=ratio), :]
    blk = lax.broadcasted_iota(jnp.int32, (n_selb, Q_BLK), 0)
    qpos_s = q0 + lax.broadcasted_iota(jnp.int32, (n_selb, Q_BLK), 1)
    cur = lax.shift_right_logical(qpos_s, int(math.log2(SEL_BLK)))
    valid = blk * SEL_BLK <= qpos_s
    forced = (blk == 0) | (blk == cur) | (blk == cur - 1)
    score = jnp.where(valid, imp_s + jnp.where(forced, FORCE_BONUS, 0.0), -1e30)
    picked = jnp.zeros((n_selb, Q_BLK), f32)
    for _ in range(SEL_TOPN):
        best = jnp.max(score, axis=0, keepdims=True)
        first = jnp.min(jnp.where(score == best, blk, n_selb), axis=0, keepdims=True)
        hit = blk == first
        picked = jnp.where(hit, 1.0, picked)
        score = jnp.where(hit, -3e38, score)
    selb_t = jnp.where(valid, picked, 0.0)
    if n_selb < N_SELB:
        selb_t = jnp.concatenate([selb_t, jnp.zeros((N_SELB - n_selb, Q_BLK), f32)], axis=0)
    selb = ((selb_t.T - 1.0) * (-MASKED)).astype(bf16)
    selb = jnp.concatenate([selb] * NSA_GROUP, axis=0)

    zeros_q = jnp.zeros((NSA_ROWS, HEAD_DIM), bf16)
    q_sel = jnp.concatenate([qo, zeros_q, selb], axis=1)
    q_win = jnp.concatenate([zeros_q, qo, jnp.zeros((NSA_ROWS, N_SELB), bf16)], axis=1)
    qpos_r = q0 + (lax.broadcasted_iota(jnp.int32, (NSA_ROWS, 1), 0) & (Q_BLK - 1))

    m_ref[...] = jnp.full(m_ref.shape, MASKED, f32)
    l_ref[...] = jnp.zeros(l_ref.shape, f32)
    acc_ref[...] = jnp.zeros(acc_ref.shape, f32)

    def sel_tile(t, causal):
        k0 = pl.multiple_of(t * SEL_KT, SEL_KT)
        s = _dot_nt(q_sel, kk_ref[0, 0, pl.ds(k0, SEL_KT), :])
        if causal:
            kpos = k0 + lax.broadcasted_iota(jnp.int32, (NSA_ROWS, SEL_KT), 1)
            s = jnp.where(kpos <= qpos_r, s, MASKED)
        m_old = m_ref[...]
        m_new = jnp.maximum(m_old, jnp.max(s, axis=1, keepdims=True))
        alpha = jnp.exp(m_old - m_new)
        p = jnp.exp(s - m_new)
        l_ref[...] = alpha * l_ref[...] + jnp.sum(p, axis=1, keepdims=True)
        acc_ref[...] = alpha * acc_ref[...] + _dot(p.astype(bf16), vv_ref[0, 0, pl.ds(k0, SEL_KT), :])
        m_ref[...] = m_new

    n_full = q0 // SEL_KT

    def full_step(t, c):
        sel_tile(t, False)
        return c

    lax.fori_loop(0, n_full, full_step, 0)
    sel_tile(n_full, True)
    o_s = acc_ref[:, 0:HEAD_DIM] / l_ref[...]

    w0 = pl.multiple_of(jnp.maximum(q0 - WINDOW, 0), Q_BLK)
    s_w = _dot_nt(q_win, kk_ref[0, 0, pl.ds(w0, WIN_KEYS), :])
    kpos_w = w0 + lax.broadcasted_iota(jnp.int32, (NSA_ROWS, WIN_KEYS), 1)
    s_w = jnp.where((kpos_w <= qpos_r) & (kpos_w > qpos_r - WINDOW), s_w, MASKED)
    p_w = jnp.exp(s_w - jnp.max(s_w, axis=1, keepdims=True))
    l_w = jnp.sum(p_w, axis=1, keepdims=True)
    acc_w = _dot(p_w.astype(bf16), vv_ref[0, 0, pl.ds(w0, WIN_KEYS), :])
    o_w = acc_w[:, HEAD_DIM:2 * HEAD_DIM] / l_w

    g = g_ref[0, 0, 0]
    o_ref[0, 0, 0] = g[:, 0:1] * o_ct.T + g[:, 1:2] * o_s + g[:, 2:3] * o_w


def _nsa_prompt(q_raw, q_rot, gates, kc, vc, rows_full, rows_win):
    bsz, t_ = q_raw.shape[:2]
    bf16 = jnp.bfloat16
    nqb = t_ // Q_BLK
    scale = HEAD_DIM ** -0.5
    n_cmp = kc.shape[1] + 1

    def q_rows(a):
        a = a.reshape(bsz, nqb, Q_BLK, NSA_KV_HEADS, NSA_GROUP, a.shape[-1])
        return a.transpose(0, 3, 1, 4, 2, 5).reshape(bsz, NSA_KV_HEADS, nqb, NSA_ROWS, a.shape[-1])

    qr = q_rows((q_raw * scale).astype(bf16))
    qo = q_rows((q_rot * scale).astype(bf16))
    gt = q_rows(gates)
    kc_p = jnp.pad(kc, ((0, 0), (0, 1), (0, 0), (0, 0))).transpose(0, 2, 1, 3).astype(bf16)
    vct = jnp.pad(vc, ((0, 0), (0, 1), (0, 0), (0, 0))).transpose(0, 2, 3, 1).astype(bf16)
    onehot = (jnp.arange(t_)[:, None] // SEL_BLK == jnp.arange(N_SELB)[None, :]).astype(bf16)
    onehot = jnp.broadcast_to(onehot, (bsz, NSA_KV_HEADS, t_, N_SELB))
    kk = jnp.concatenate([rows_full[:, :, 2].transpose(0, 2, 1, 3).astype(bf16),
                          rows_win[:, :, 0].transpose(0, 2, 1, 3).astype(bf16), onehot], axis=-1)
    vv = jnp.concatenate([rows_full[:, :, 3].transpose(0, 2, 1, 3),
                          rows_win[:, :, 1].transpose(0, 2, 1, 3)], axis=-1).astype(bf16)
    grid = (bsz, NSA_KV_HEADS, nqb)
    per_blk = lambda b, h, i: (b, h, i, 0, 0)
    per_head = lambda b, h, i: (b, h, 0, 0)
    o = pl.pallas_call(
        _nsa_prompt_body,
        grid=grid,
        in_specs=[pl.BlockSpec((1, 1, 1, NSA_ROWS, HEAD_DIM), per_blk),
                  pl.BlockSpec((1, 1, 1, NSA_ROWS, HEAD_DIM), per_blk),
                  pl.BlockSpec((1, 1, n_cmp, HEAD_DIM), per_head),
                  pl.BlockSpec((1, 1, HEAD_DIM, n_cmp), per_head),
                  pl.BlockSpec((1, 1, t_, 2 * HEAD_DIM + N_SELB), per_head),
                  pl.BlockSpec((1, 1, t_, 2 * HEAD_DIM), per_head),
                  pl.BlockSpec((1, 1, 1, NSA_ROWS, 3), per_blk)],
        out_specs=pl.BlockSpec((1, 1, 1, NSA_ROWS, HEAD_DIM), per_blk),
        out_shape=jax.ShapeDtypeStruct((bsz, NSA_KV_HEADS, nqb, NSA_ROWS, HEAD_DIM), jnp.float32),
        scratch_shapes=[pltpu.VMEM((8 + n_cmp, Q_BLK), jnp.float32),
                        pltpu.VMEM((NSA_ROWS, 1), jnp.float32),
                        pltpu.VMEM((NSA_ROWS, 1), jnp.float32),
                        pltpu.VMEM((NSA_ROWS, 2 * HEAD_DIM), jnp.float32)],
        compiler_params=pltpu.CompilerParams(
            dimension_semantics=("arbitrary", "arbitrary", "arbitrary"),
            vmem_limit_bytes=48 * 1024 * 1024),
        name="nsa_prompt",
    )(qr, qo, kc_p, vct, kk, vv, gt)
    o = o.reshape(bsz, NSA_KV_HEADS, nqb, NSA_GROUP, Q_BLK, HEAD_DIM).transpose(0, 2, 4, 1, 3, 5)
    return o.reshape(bsz, t_, NSA_HEADS * HEAD_DIM)


def _ab_mixer(x, pos, w_in, w_gla_gate, b_gla_gate, gla_norm_g, w_cmp_pool, w_out,
              gla_state, nsa_cache, page_table, win_buf):
    bsz, t_, _ = x.shape
    gq, gk, gv, gr, glr, nq, nkv, ngate = _split_cols(_mm3(x, w_in), GLA_SIZES + NSA_SIZES)
    q_a = gq.reshape(bsz, t_, GLA_HEADS, GLA_DK) * (GLA_DK ** -0.5)
    k_a = gk.reshape(bsz, t_, GLA_HEADS, GLA_DK)
    v_a = gv.reshape(bsz, t_, GLA_HEADS, GLA_DV)
    log_a = (jax.nn.log_sigmoid(glr @ w_gla_gate + b_gla_gate) / GLA_TAU).reshape(bsz, t_, GLA_HEADS, GLA_DK)
    if gla_state is None:
        gla_state = jnp.zeros((bsz, GLA_HEADS, GLA_DK, GLA_DV), jnp.float32)
    o_a, s_a = _gla_recurrence(q_a, k_a, v_a, log_a, gla_state)
    o_a = _rms_norm(o_a, gla_norm_g).reshape(bsz, t_, -1) * jax.nn.silu(gr)
    q_raw = nq.reshape(bsz, t_, NSA_KV_HEADS, NSA_GROUP, HEAD_DIM)
    q_rot = _partial_rope(q_raw, pos)
    kv = nkv.reshape(bsz, t_, 6, NSA_KV_HEADS, HEAD_DIM)
    k_sel = _partial_rope(kv[:, :, 2], pos)
    k_win = _partial_rope(kv[:, :, 4], pos)
    rows_full = jnp.stack([kv[:, :, 0], kv[:, :, 1], k_sel, kv[:, :, 3]], axis=2)
    rows_win = jnp.stack([k_win, kv[:, :, 5]], axis=2)
    gates = jax.nn.sigmoid(ngate).reshape(bsz, t_, NSA_KV_HEADS, NSA_GROUP, 3)
    if nsa_cache is None:
        keys = rows_full
    else:
        past = nsa_cache[page_table].reshape(bsz, -1, 4, NSA_KV_HEADS, HEAD_DIM)
        keys = jnp.concatenate([past, rows_full], axis=1)
    length = keys.shape[1]
    kc, vc, cend = _compress(keys[:, :, 0], keys[:, :, 1], w_cmp_pool)
    n_sel = -(-length // SEL_BLK)
    ksb = _to_sel_blocks(keys[:, :, 2], n_sel)
    vsb = _to_sel_blocks(keys[:, :, 3], n_sel)
    if nsa_cache is None:
        o_b = _nsa_prompt(q_raw, q_rot, gates, kc, vc, rows_full, rows_win)
        new_win = rows_win[:, -min(WINDOW, t_):]
    else:
        w_buf = win_buf.shape[1]
        kw = jnp.concatenate([win_buf, rows_win], axis=1)
        kwpos = (length - t_) - w_buf + jnp.arange(kw.shape[1])
        o_b = _nsa_attend(q_raw, q_rot, pos, gates, kc, vc, cend, ksb, vsb, kw[:, :, 0], kw[:, :, 1], kwpos)
        o_b = o_b.reshape(bsz, t_, -1)
        new_win = kw[:, -w_buf:]
    y = _mm3(jnp.concatenate([o_a, o_b], axis=-1), w_out)
    return y, s_a, rows_full, new_win


def _conv_module(x, conv_buf, w_pw1, b_pw1, w_dw, b_dw, ln_g, ln_b, w_pw2, b_pw2):
    bsz = x.shape[0]
    a, g = jnp.split(_mm3(x, w_pw1) + b_pw1, 2, axis=-1)
    u = a * jax.nn.sigmoid(g)
    if conv_buf is None:
        conv_buf = jnp.zeros((bsz, CONV_W - 1, D_CONV), u.dtype)
    ext = jnp.concatenate([conv_buf, u], axis=1)
    c = lax.conv_general_dilated(ext, w_dw[:, None, :], (1,), 'VALID',
                                 dimension_numbers=('NWC', 'WIO', 'NWC'),
                                 feature_group_count=D_CONV) + b_dw
    c = jax.nn.silu(_layer_norm(c, ln_g, ln_b))
    return _mm3(c, w_pw2) + b_pw2, ext[:, -(CONV_W - 1):]


PER_GROUP = N_EXPERTS // N_GROUPS
PICKED = -3e38


def _ln_rows(v, g, b):
    mu = jnp.mean(v, axis=-1, keepdims=True)
    c = v - mu
    var = jnp.mean(c * c, axis=-1, keepdims=True)
    return c * lax.rsqrt(var + LN_EPS) * g + b


def _first_max(v, ids, axes, sentinel):
    best = v
    for a in axes:
        best = jnp.max(best, axis=a, keepdims=True)
    first = jnp.where(v == best, ids, sentinel)
    for a in axes:
        first = jnp.min(first, axis=a, keepdims=True)
    return best, first


def _sum_axes(v, axes):
    for a in axes:
        v = jnp.sum(v, axis=a, keepdims=True)
    return v


def _moe_pre_body(x_ref, mix_ref, g_ref, b_ref, wr_ref, br_ref, wgu_ref, wdn_ref,
                  x1_ref, x1b_ref, sh_ref, eidx_ref, gate_ref, rank_ref, cnt_ref, run_ref):
    f32, bf16 = jnp.float32, jnp.bfloat16
    tm = x_ref.shape[0]

    @pl.when(pl.program_id(0) == 0)
    def _():
        run_ref[...] = jnp.zeros(run_ref.shape, f32)

    x1 = _ln_rows(ALPHA * x_ref[...] + mix_ref[...], g_ref[...], b_ref[...])
    x1_ref[...] = x1
    x1b = x1.astype(bf16)
    x1b_ref[...] = x1b

    h = _dot(x1b, wgu_ref[...])
    d_sh = h.shape[1] // 2
    act = (jax.nn.silu(h[:, :d_sh]) * h[:, d_sh:]).astype(bf16)
    sh_ref[...] = _dot(act, wdn_ref[...])

    s = jax.nn.sigmoid(_dot_nt(wr_ref[...], x1b)).reshape(N_GROUPS, PER_GROUP, tm)
    sb = s + br_ref[...].reshape(N_GROUPS, PER_GROUP, 1)
    shape3 = (N_GROUPS, PER_GROUP, tm)
    pid = lax.broadcasted_iota(jnp.int32, shape3, 1)
    gid = lax.broadcasted_iota(jnp.int32, (N_GROUPS, 1, tm), 0)
    eid = lax.broadcasted_iota(jnp.int32, shape3, 0) * PER_GROUP + pid
    top1, i1 = _first_max(sb, pid, (1,), PER_GROUP)
    top2 = jnp.max(jnp.where(pid == i1, PICKED, sb), axis=1, keepdims=True)
    gscore = top1 + top2
    gsel = jnp.zeros((N_GROUPS, 1, tm), f32)
    for _ in range(TOPK_GROUPS):
        _, first = _first_max(gscore, gid, (0,), N_GROUPS)
        hit = gid == first
        gsel = jnp.where(hit, 1.0, gsel)
        gscore = jnp.where(hit, PICKED, gscore)
    cand = jnp.where(gsel > 0.0, sb, -1e30)
    firsts, gates = [], []
    picked = jnp.zeros(shape3, f32)
    for _ in range(TOP_K):
        _, first = _first_max(cand, eid, (0, 1), N_EXPERTS)
        hit = eid == first
        firsts.append(first)
        gates.append(_sum_axes(jnp.where(hit, s, 0.0), (0, 1)))
        picked = jnp.where(hit, 1.0, picked)
        cand = jnp.where(hit, PICKED, cand)
    gsum = gates[0]
    for gk in gates[1:]:
        gsum = gsum + gk
    earlier = (lax.broadcasted_iota(jnp.int32, (tm, tm), 0) < lax.broadcasted_iota(jnp.int32, (tm, tm), 1))
    picked2 = picked.reshape(N_EXPERTS, tm)
    rank = run_ref[...] + _dot(picked2.astype(bf16), jnp.where(earlier, 1.0, 0.0).astype(bf16))
    run_new = run_ref[...] + jnp.sum(picked2, axis=1, keepdims=True)
    run_ref[...] = run_new
    cnt_ref[...] = jnp.broadcast_to(run_new, cnt_ref.shape)
    rank3 = rank.reshape(shape3)
    for k in range(TOP_K):
        hit = eid == firsts[k]
        eidx_ref[k:k + 1, :] = firsts[k].reshape(1, tm)
        gate_ref[k:k + 1, :] = (gates[k] / gsum * ROUTE_SCALE).reshape(1, tm)
        rank_ref[k:k + 1, :] = _sum_axes(jnp.where(hit, rank3, 0.0), (0, 1)).reshape(1, tm).astype(jnp.int32)


def _moe_pre(x, mix, g, b, w_router, b_router, w_sh_gu, w_sh_down):
    m, d = x.shape
    bf16 = jnp.bfloat16
    tm = min(m, 512)
    row = lambda i: (i, 0)
    col = lambda i: (0, i)
    fixed = lambda i: (0, 0)
    d_sh2 = w_sh_gu.shape[1]
    return pl.pallas_call(
        _moe_pre_body,
        grid=(m // tm,),
        in_specs=[pl.BlockSpec((tm, d), row), pl.BlockSpec((tm, d), row),
                  pl.BlockSpec((1, d), fixed), pl.BlockSpec((1, d), fixed),
                  pl.BlockSpec((N_EXPERTS, d), fixed), pl.BlockSpec((N_EXPERTS, 1), fixed),
                  pl.BlockSpec((d, d_sh2), fixed), pl.BlockSpec((d_sh2 // 2, d), fixed)],
        out_specs=[pl.BlockSpec((tm, d), row), pl.BlockSpec((tm, d), row), pl.BlockSpec((tm, d), row),
                   pl.BlockSpec((TOP_K, tm), col), pl.BlockSpec((TOP_K, tm), col), pl.BlockSpec((TOP_K, tm), col),
                   pl.BlockSpec((N_EXPERTS, LANE), fixed)],
        out_shape=[jax.ShapeDtypeStruct((m, d), jnp.float32), jax.ShapeDtypeStruct((m, d), bf16),
                   jax.ShapeDtypeStruct((m, d), jnp.float32),
                   jax.ShapeDtypeStruct((TOP_K, m), jnp.int32), jax.ShapeDtypeStruct((TOP_K, m), jnp.float32),
                   jax.ShapeDtypeStruct((TOP_K, m), jnp.int32),
                   jax.ShapeDtypeStruct((N_EXPERTS, LANE), jnp.float32)],
        scratch_shapes=[pltpu.VMEM((N_EXPERTS, 1), jnp.float32)],
        compiler_params=pltpu.CompilerParams(dimension_semantics=("arbitrary",),
                                             vmem_limit_bytes=48 * 1024 * 1024),
        name="moe_pre",
    )(x, mix, g.reshape(1, d), b.reshape(1, d), w_router.T.astype(bf16), b_router.reshape(N_EXPERTS, 1),
      w_sh_gu.astype(bf16), w_sh_down.astype(bf16))


def _moe_expert_body(exp_ref, first_ref, active_ref, xs_ref, sg_ref, wgu_ref, wdn_ref, y_ref, wgu_bf, wdn_bf):
    i = pl.program_id(0)
    bf16 = jnp.bfloat16

    @pl.when(first_ref[i] == 1)
    def _():
        wgu_bf[...] = wgu_ref[0].astype(bf16)
        wdn_bf[...] = wdn_ref[0].astype(bf16)

    @pl.when(active_ref[i] == 1)
    def _():
        h = _dot(xs_ref[...], wgu_bf[...])
        d_e = h.shape[1] // 2
        act = (jax.nn.silu(h[:, :d_e]) * h[:, d_e:]).astype(bf16)
        y_ref[...] = _dot(act, wdn_bf[...]) * sg_ref[...]

    @pl.when(active_ref[i] == 0)
    def _():
        y_ref[...] = jnp.zeros(y_ref.shape, jnp.float32)


def _moe_experts(xs, slot_gate, blk_exp, blk_first, blk_active, w_exp_gu, w_exp_down, bm):
    n_slots, d = xs.shape
    n_blk = n_slots // bm
    d_e2 = w_exp_gu.shape[2]
    grid_spec = pltpu.PrefetchScalarGridSpec(
        num_scalar_prefetch=3,
        grid=(n_blk,),
        in_specs=[pl.BlockSpec((bm, d), lambda i, e, f, a: (i, 0)),
                  pl.BlockSpec((bm, 1), lambda i, e, f, a: (i, 0)),
                  pl.BlockSpec((1, d, d_e2), lambda i, e, f, a: (e[i], 0, 0)),
                  pl.BlockSpec((1, d_e2 // 2, d), lambda i, e, f, a: (e[i], 0, 0))],
        out_specs=pl.BlockSpec((bm, d), lambda i, e, f, a: (i, 0)),
        scratch_shapes=[pltpu.VMEM((d, d_e2), jnp.bfloat16), pltpu.VMEM((d_e2 // 2, d), jnp.bfloat16)])
    return pl.pallas_call(
        _moe_expert_body,
        grid_spec=grid_spec,
        out_shape=jax.ShapeDtypeStruct((n_slots, d), jnp.float32),
        compiler_params=pltpu.CompilerParams(dimension_semantics=("arbitrary",),
                                             vmem_limit_bytes=48 * 1024 * 1024),
        name="moe_experts",
    )(blk_exp, blk_first, blk_active, xs, slot_gate, w_exp_gu, w_exp_down)


def _add_ln_body(x_ref, a_ref, b2_ref, g_ref, b_ref, o_ref):
    o_ref[...] = _ln_rows(ALPHA * x_ref[...] + (a_ref[...] + b2_ref[...]), g_ref[...], b_ref[...])


def _add_ln(x, a, b2, g, b):
    m, d = x.shape
    tm = min(m, 512)
    row = lambda i: (i, 0)
    fixed = lambda i: (0, 0)
    return pl.pallas_call(
        _add_ln_body,
        grid=(m // tm,),
        in_specs=[pl.BlockSpec((tm, d), row)] * 3 + [pl.BlockSpec((1, d), fixed)] * 2,
        out_specs=pl.BlockSpec((tm, d), row),
        out_shape=jax.ShapeDtypeStruct((m, d), jnp.float32),
        compiler_params=pltpu.CompilerParams(dimension_semantics=("arbitrary",)),
        name="add_ln",
    )(x, a, b2, g.reshape(1, d), b.reshape(1, d))


def _moe_layer(x, mix, ln1_g, ln1_b, ln2_g, ln2_b, w_router, b_router, w_exp_gu, w_exp_down, w_sh_gu, w_sh_down):
    m, d = x.shape
    x1, x1b, shared, eidx, gate8, rank8, counts = _moe_pre(x, mix, ln1_g, ln1_b, w_router, b_router,
                                                           w_sh_gu, w_sh_down)
    bm = 256 if m * TOP_K >= 256 * N_EXPERTS else MOE_BLK
    n_blk = (m * TOP_K) // bm + N_EXPERTS
    counts = counts[:, 0].astype(jnp.int32)
    padded = (counts + bm - 1) // bm * bm
    pad_end = jnp.cumsum(padded)
    dest = ((pad_end - padded)[eidx] + rank8).reshape(-1)
    tok = jnp.tile(jnp.arange(m, dtype=jnp.int32), TOP_K)
    slot_tok = jnp.zeros((n_blk * bm,), jnp.int32).at[dest].set(tok)
    slot_gate = jnp.zeros((n_blk * bm,), jnp.float32).at[dest].set(gate8.reshape(-1))
    blk_start = jnp.arange(n_blk, dtype=jnp.int32) * bm
    blk_exp = jnp.minimum(jnp.searchsorted(pad_end, blk_start, side='right'), N_EXPERTS - 1).astype(jnp.int32)
    blk_active = (blk_start < pad_end[-1]).astype(jnp.int32)
    blk_first = jnp.concatenate([jnp.ones((1,), jnp.int32), (blk_exp[1:] != blk_exp[:-1]).astype(jnp.int32)])
    xs = x1b[slot_tok]
    y = _moe_experts(xs, slot_gate.reshape(-1, 1), blk_exp, blk_first, blk_active, w_exp_gu, w_exp_down, bm)
    routed = y[dest.reshape(TOP_K, m)].sum(axis=0)
    return _add_ln(x1, routed, shared, ln2_g, ln2_b)


def _trunk(x, pos, gla_state, nsa_cache, page_table, win_buf, conv_buf,
           w_in_ab, w_gla_gate, b_gla_gate, gla_norm_g, w_cmp_pool, w_out_ab,
           w_pw1, b_pw1, w_dw, b_dw, conv_ln_g, conv_ln_b, w_pw2, b_pw2,
           ln_g, ln_b, w_router, b_router, w_exp_gu, w_exp_down, w_sh_gu, w_sh_down):
    new_gla, new_rows, new_win, new_conv = [], [], [], []
    for layer in range(DEPTH):
        i = layer // 2
        if layer % 2 == 0:
            mix, s_a, rows, win = _ab_mixer(
                x, pos, w_in_ab[i], w_gla_gate[i], b_gla_gate[i], gla_norm_g[i], w_cmp_pool[i], w_out_ab[i],
                None if gla_state is None else gla_state[i],
                None if nsa_cache is None else nsa_cache[i], page_table,
                None if win_buf is None else win_buf[i])
            new_gla.append(s_a)
            new_rows.append(rows)
            new_win.append(win)
        else:
            mix, cb = _conv_module(x, None if conv_buf is None else conv_buf[i], w_pw1[i], b_pw1[i],
                                   w_dw[i], b_dw[i], conv_ln_g[i], conv_ln_b[i], w_pw2[i], b_pw2[i])
            new_conv.append(cb)
        bsz, t_, d = x.shape
        x = _moe_layer(x.reshape(-1, d), mix.reshape(-1, d), ln_g[layer, 0], ln_b[layer, 0],
                       ln_g[layer, 1], ln_b[layer, 1], w_router[layer], b_router[layer],
                       w_exp_gu[layer], w_exp_down[layer], w_sh_gu[layer], w_sh_down[layer]).reshape(bsz, t_, d)
    return x, jnp.stack(new_gla), jnp.stack(new_rows), jnp.stack(new_win), jnp.stack(new_conv)


def kernel(x_prompt, x_sample, state_gla, cache_nsa_kv, state_nsa_win, state_conv, page_table,
           w_in_ab, w_gla_gate, b_gla_gate, gla_norm_g, w_cmp_pool, w_out_ab,
           w_pw1, b_pw1, w_dw, b_dw, conv_ln_g, conv_ln_b, w_pw2, b_pw2,
           ln_g, ln_b, w_router, b_router, w_exp_gu, w_exp_down, w_sh_gu, w_sh_down):
    weights = (w_in_ab, w_gla_gate, b_gla_gate, gla_norm_g, w_cmp_pool, w_out_ab,
               w_pw1, b_pw1, w_dw, b_dw, conv_ln_g, conv_ln_b, w_pw2, b_pw2,
               ln_g, ln_b, w_router, b_router, w_exp_gu, w_exp_down, w_sh_gu, w_sh_down)
    past_len = page_table.shape[1] * PAGE_SIZE
    pos_p = jnp.arange(x_prompt.shape[1])
    pos_s = past_len + jnp.arange(x_sample.shape[1])
    y_prompt, gla_p, rows_p, win_p, conv_p = _trunk(x_prompt, pos_p, None, None, None, None, None, *weights)
    y_sample, gla_s, rows_s, win_s, conv_s = _trunk(x_sample, pos_s, state_gla, cache_nsa_kv, page_table,
                                                    state_nsa_win, state_conv, *weights)
    return (y_prompt, y_sample, gla_p, gla_s, rows_p, rows_s, win_p, win_s, conv_p, conv_s)
```

```python
import functools
import math

import jax
import jax.numpy as jnp
import numpy as np
from jax import lax
from jax.experimental import pallas as pl
from jax.experimental.pallas import tpu as pltpu

D_MODEL = 1024
DEPTH = 2
PAGE_SIZE = 128

GLA_HEADS = 4
GLA_DV = D_MODEL // 2 // GLA_HEADS
GLA_DK = GLA_DV // 2
GLA_RANK = 16
GLA_TAU = 16.0
GLA_CHUNK = 64

NSA_HEADS = 8
NSA_KV_HEADS = 2
NSA_GROUP = NSA_HEADS // NSA_KV_HEADS
HEAD_DIM = D_MODEL // 2 // NSA_HEADS
CMP_BLK = 32
CMP_STRIDE = 16
SEL_BLK = 64
SEL_TOPN = 16
WINDOW = 512
Q_BLK = 128
FORCE_BONUS = 100.0
ROPE_DIM = HEAD_DIM // 4
ROPE_THETA = 500000.0

GLA_SIZES = (GLA_HEADS * GLA_DK, GLA_HEADS * GLA_DK, GLA_HEADS * GLA_DV, GLA_HEADS * GLA_DV, GLA_RANK)
NSA_SIZES = (NSA_HEADS * HEAD_DIM, 6 * NSA_KV_HEADS * HEAD_DIM, 3 * NSA_HEADS)

CONV_W = 31
D_CONV = D_MODEL

N_EXPERTS = 64
N_GROUPS = 8
TOPK_GROUPS = 4
TOP_K = 8
D_EXPERT = 256
ROUTE_SCALE = 2.5
MOE_BLK = 128

ALPHA = (2 * DEPTH) ** 0.25
LN_EPS = 1e-5

LANE = 128


def _dot(a, b):
    return jnp.dot(a, b, preferred_element_type=jnp.float32)


def _dot_nt(a, b):
    return lax.dot_general(a, b, (((1,), (1,)), ((), ())), preferred_element_type=jnp.float32)


def _mm_body(x_ref, w_ref, o_ref):
    o_ref[...] = _dot(x_ref[...].astype(jnp.bfloat16), w_ref[...].astype(jnp.bfloat16))


def _mm(x, w):
    m, k = x.shape
    n = w.shape[1]
    n_pad = -(-n // LANE) * LANE
    if n_pad != n:
        w = jnp.pad(w, ((0, 0), (0, n_pad - n)))
    tm = min(m, 512)
    tn = next(t for t in (512, 256, 128) if n_pad % t == 0)
    out = pl.pallas_call(
        _mm_body,
        grid=(m // tm, n_pad // tn),
        in_specs=[pl.BlockSpec((tm, k), lambda i, j: (i, 0)),
                  pl.BlockSpec((k, tn), lambda i, j: (0, j))],
        out_specs=pl.BlockSpec((tm, tn), lambda i, j: (i, j)),
        out_shape=jax.ShapeDtypeStruct((m, n_pad), jnp.float32),
        compiler_params=pltpu.CompilerParams(dimension_semantics=("arbitrary", "arbitrary")),
        name="mm",
    )(x, w)
    return out[:, :n] if n_pad != n else out


def _mm3(x, w):
    b, t, d = x.shape
    return _mm(x.reshape(b * t, d), w).reshape(b, t, -1)


def _split_cols(h, sizes):
    return jnp.split(h, np.cumsum(sizes)[:-1].tolist(), axis=-1)


def _layer_norm(x, g, b):
    mu = x.mean(-1, keepdims=True)
    var = jnp.square(x - mu).mean(-1, keepdims=True)
    return (x - mu) * lax.rsqrt(var + LN_EPS) * g + b


def _rms_norm(x, g):
    return x * lax.rsqrt(jnp.mean(x * x, -1, keepdims=True) + LN_EPS) * g


def _partial_rope(x, pos):
    half = ROPE_DIM // 2
    inv_freq = jnp.power(ROPE_THETA, -jnp.arange(half, dtype=jnp.float32) / half)
    ang = pos.astype(jnp.float32)[:, None] * inv_freq
    ang = ang.reshape(ang.shape[0], *([1] * (x.ndim - 3)), half)
    cos, sin = jnp.cos(ang), jnp.sin(ang)
    x1 = x[..., :half]
    x2 = x[..., half:ROPE_DIM]
    rot = jnp.concatenate([x1 * cos - x2 * sin, x2 * cos + x1 * sin], -1)
    return jnp.concatenate([rot, x[..., ROPE_DIM:]], -1)


def _masked_softmax(s, mask):
    s = jnp.where(mask, s, -jnp.inf)
    m = jnp.max(s, axis=-1, keepdims=True)
    m = jnp.where(jnp.isfinite(m), m, 0.0)
    p = jnp.exp(s - m)
    return p / jnp.maximum(p.sum(-1, keepdims=True), 1e-30)


def _gla_recurrence(q, k, v, log_a, s0):
    bsz, t_, nh, _ = q.shape
    c = math.gcd(t_, GLA_CHUNK)
    n = t_ // c

    def chunks(a):
        return jnp.moveaxis(a.reshape(bsz, n, c, *a.shape[2:]), 1, 0)

    causal = jnp.tril(jnp.ones((c, c), dtype=bool))[None, :, :, None, None]

    def step(s, inp):
        qc, kc, vc, lc = inp
        bc = jnp.cumsum(lc, axis=1)
        decay = jnp.exp(jnp.where(causal, bc[:, :, None] - bc[:, None, :], -jnp.inf))
        attn = jnp.einsum('bijhd,bjhd->bhij', qc[:, :, None] * decay, kc)
        o = jnp.einsum('bhij,bjhe->bihe', attn, vc) + jnp.einsum('bihd,bhde->bihe', qc * jnp.exp(bc), s)
        bl = bc[:, -1]
        s = jnp.exp(bl)[..., None] * s + jnp.einsum('bjhd,bjhe->bhde', kc * jnp.exp(bl[:, None] - bc), vc)
        return s, o

    s_fin, o = lax.scan(step, s0, (chunks(q), chunks(k), chunks(v), chunks(log_a)))
    return jnp.moveaxis(o, 0, 1).reshape(bsz, t_, nh, -1), s_fin


def _compress(k, v, w_pool):
    bsz, length = k.shape[:2]
    n_sub = length // CMP_STRIDE

    def pool(a, w):
        sub = a[:, :n_sub * CMP_STRIDE].reshape(bsz, n_sub, CMP_STRIDE, *a.shape[2:])
        first = jnp.einsum('bnjhd,j->bnhd', sub, w[:CMP_STRIDE])
        second = jnp.einsum('bnjhd,j->bnhd', sub, w[CMP_STRIDE:])
        return first[:, :-1] + second[:, 1:]

    cend = jnp.arange(n_sub - 1) * CMP_STRIDE + CMP_BLK - 1
    return pool(k, w_pool[0]), pool(v, w_pool[1]), cend


def _to_sel_blocks(a, n_sel):
    bsz, length = a.shape[:2]
    a = jnp.pad(a, ((0, 0), (0, n_sel * SEL_BLK - length), (0, 0), (0, 0)))
    return a.reshape(bsz, n_sel, SEL_BLK, NSA_KV_HEADS, HEAD_DIM).transpose(0, 3, 1, 2, 4)


def _nsa_attend(q_raw, q_rot, qpos, gates, kc, vc, cend, ksb, vsb, kw, vw, kwpos):
    scale = HEAD_DIM ** -0.5
    bsz, tq = q_raw.shape[:2]
    n_cmp, n_sel = kc.shape[1], ksb.shape[2]
    s_c = jnp.einsum('bqhgd,bnhd->bhgqn', q_raw, kc) * scale
    p_c = _masked_softmax(s_c, cend[None, :] <= qpos[:, None])
    o_c = jnp.einsum('bhgqn,bnhd->bqhgd', p_c, vc)
    ratio = SEL_BLK // CMP_STRIDE
    imp = p_c.sum(axis=2)
    imp = jnp.pad(imp, ((0, 0), (0, 0), (0, 0), (1, ratio * (n_sel + 1) - 1 - n_cmp)))
    imp = imp.reshape(bsz, NSA_KV_HEADS, tq, n_sel + 1, ratio)
    imp_s = imp[..., :n_sel, :].sum(-1) + imp[..., 1:, 0]
    blk = jnp.arange(n_sel)[None, :]
    cur = (qpos // SEL_BLK)[:, None]
    valid = blk * SEL_BLK <= qpos[:, None]
    forced = (blk == 0) | (blk == cur) | (blk == cur - 1)
    score = jnp.where(valid, imp_s + jnp.where(forced, FORCE_BONUS, 0.0), -jnp.inf)
    k_top = min(SEL_TOPN, n_sel)
    _, sel = lax.top_k(score, k_top)
    take = jax.vmap(jax.vmap(lambda blocks, idx: blocks[idx]))
    ks = take(ksb, sel).reshape(bsz, NSA_KV_HEADS, tq, k_top * SEL_BLK, HEAD_DIM)
    vs = take(vsb, sel).reshape(bsz, NSA_KV_HEADS, tq, k_top * SEL_BLK, HEAD_DIM)
    kpos = (sel[..., None] * SEL_BLK + jnp.arange(SEL_BLK)).reshape(bsz, NSA_KV_HEADS, tq, k_top * SEL_BLK)
    s_s = jnp.einsum('bqhgd,bhqkd->bhgqk', q_rot, ks) * scale
    p_s = _masked_softmax(s_s, (kpos <= qpos[:, None])[:, :, None])
    o_s = jnp.einsum('bhgqk,bhqkd->bqhgd', p_s, vs)
    s_w = jnp.einsum('bqhgd,bkhd->bhgqk', q_rot, kw) * scale
    kp, qp = kwpos[None, :], qpos[:, None]
    p_w = _masked_softmax(s_w, (kp <= qp) & (kp > qp - WINDOW) & (kp >= 0))
    o_w = jnp.einsum('bhgqk,bkhd->bqhgd', p_w, vw)
    return gates[..., 0:1] * o_c + gates[..., 1:2] * o_s + gates[..., 2:3] * o_w


NSA_ROWS = NSA_GROUP * Q_BLK
SEL_KT = 512
N_SELB = 128
MASKED = -1e9
WIN_KEYS = WINDOW + Q_BLK


def _nsa_prompt_body(qr_ref, qo_ref, kc_ref, vct_ref, kk_ref, vv_ref, g_ref, o_ref,
                     imp_ref, m_ref, l_ref, acc_ref):
    f32, bf16 = jnp.float32, jnp.bfloat16
    qb = pl.program_id(2)
    q0 = qb * Q_BLK
    qr = qr_ref[0, 0, 0]
    qo = qo_ref[0, 0, 0]
    n_cmp = kc_ref.shape[2]

    s_c = _dot_nt(kc_ref[0, 0], qr)
    n_idx = lax.broadcasted_iota(jnp.int32, (n_cmp, NSA_ROWS), 0)
    qpos_c = q0 + (lax.broadcasted_iota(jnp.int32, (n_cmp, NSA_ROWS), 1) & (Q_BLK - 1))
    cmask = (n_idx * CMP_STRIDE + (CMP_BLK - 1)) <= qpos_c
    s_c = jnp.where(cmask, s_c, MASKED)
    m_c = jnp.max(s_c, axis=0, keepdims=True)
    p_c = jnp.where(cmask, jnp.exp(s_c - m_c), 0.0)
    p_c = p_c / jnp.maximum(jnp.sum(p_c, axis=0, keepdims=True), 1e-30)
    o_ct = _dot(vct_ref[0, 0], p_c.astype(bf16))

    imp = (p_c[:, 0:Q_BLK] + p_c[:, Q_BLK:2 * Q_BLK]) + p_c[:, 2 * Q_BLK:3 * Q_BLK] + p_c[:, 3 * Q_BLK:]
    imp_ref[0:8, :] = jnp.zeros((8, Q_BLK), f32)
    imp_ref[8:8 + n_cmp, :] = imp
    ratio = SEL_BLK // CMP_STRIDE
    n_selb = n_cmp // ratio
    imp_s = imp_ref[pl.ds(7, n_selb, stride=ratio), :]
    for r in range(ratio):
        imp_s = imp_s + imp_ref[pl.ds(8 + r, n_selb, stride=ratio), :]
    blk = lax.broadcasted_iota(jnp.int32, (n_selb, Q_BLK), 0)
    qpos_s = q0 + lax.broadcasted_iota(jnp.int32, (n_selb, Q_BLK), 1)
    cur = lax.shift_right_logical(qpos_s, int(math.log2(SEL_BLK)))
    valid = blk * SEL_BLK <= qpos_s
    forced = (blk == 0) | (blk == cur) | (blk == cur - 1)
    score = jnp.where(valid, imp_s + jnp.where(forced, FORCE_BONUS, 0.0), -1e30)
    picked = jnp.zeros((n_selb, Q_BLK), f32)
    for _ in range(SEL_TOPN):
        best = jnp.max(score, axis=0, keepdims=True)
        first = jnp.min(jnp.where(score == best, blk, n_selb), axis=0, keepdims=True)
        hit = blk == first
        picked = jnp.where(hit, 1.0, picked)
        score = jnp.where(hit, -3e38, score)
    selb_t = jnp.where(valid, picked, 0.0)
    if n_selb < N_SELB:
        selb_t = jnp.concatenate([selb_t, jnp.zeros((N_SELB - n_selb, Q_BLK), f32)], axis=0)
    selb = ((selb_t.T - 1.0) * (-MASKED)).astype(bf16)
    selb = jnp.concatenate([selb] * NSA_GROUP, axis=0)

    zeros_q = jnp.zeros((NSA_ROWS, HEAD_DIM), bf16)
    q_sel = jnp.concatenate([qo, zeros_q, selb], axis=1)
    q_win = jnp.concatenate([zeros_q, qo, jnp.zeros((NSA_ROWS, N_SELB), bf16)], axis=1)
    qpos_r = q0 + (lax.broadcasted_iota(jnp.int32, (NSA_ROWS, 1), 0) & (Q_BLK - 1))

    m_ref[...] = jnp.full(m_ref.shape, MASKED, f32)
    l_ref[...] = jnp.zeros(l_ref.shape, f32)
    acc_ref[...] = jnp.zeros(acc_ref.shape, f32)

    def sel_tile(t, causal):
        k0 = pl.multiple_of(t * SEL_KT, SEL_KT)
        s = _dot_nt(q_sel, kk_ref[0, 0, pl.ds(k0, SEL_KT), :])
        if causal:
            kpos = k0 + lax.broadcasted_iota(jnp.int32, (NSA_ROWS, SEL_KT), 1)
            s = jnp.where(kpos <= qpos_r, s, MASKED)
        m_old = m_ref[...]
        m_new = jnp.maximum(m_old, jnp.max(s, axis=1, keepdims=True))
        alpha = jnp.exp(m_old - m_new)
        p = jnp.exp(s - m_new)
        l_ref[...] = alpha * l_ref[...] + jnp.sum(p, axis=1, keepdims=True)
        acc_ref[...] = alpha * acc_ref[...] + _dot(p.astype(bf16), vv_ref[0, 0, pl.ds(k0, SEL_KT), :])
        m_ref[...] = m_new

    n_full = q0 // SEL_KT

    def full_step(t, c):
        sel_tile(t, False)
        return c

    lax.fori_loop(0, n_full, full_step, 0)
    sel_tile(n_full, True)
    o_s = acc_ref[:, 0:HEAD_DIM] / l_ref[...]

    w0 = pl.multiple_of(jnp.maximum(q0 - WINDOW, 0), Q_BLK)
    s_w = _dot_nt(q_win, kk_ref[0, 0, pl.ds(w0, WIN_KEYS), :])
    kpos_w = w0 + lax.broadcasted_iota(jnp.int32, (NSA_ROWS, WIN_KEYS), 1)
    s_w = jnp.where((kpos_w <= qpos_r) & (kpos_w > qpos_r - WINDOW), s_w, MASKED)
    p_w = jnp.exp(s_w - jnp.max(s_w, axis=1, keepdims=True))
    l_w = jnp.sum(p_w, axis=1, keepdims=True)
    acc_w = _dot(p_w.astype(bf16), vv_ref[0, 0, pl.ds(w0, WIN_KEYS), :])
    o_w = acc_w[:, HEAD_DIM:2 * HEAD_DIM] / l_w

    g = g_ref[0, 0, 0]
    o_ref[0, 0, 0] = g[:, 0:1] * o_ct.T + g[:, 1:2] * o_s + g[:, 2:3] * o_w


def _nsa_prompt(q_raw, q_rot, gates, kc, vc, rows_full, rows_win):
    bsz, t_ = q_raw.shape[:2]
    bf16 = jnp.bfloat16
    nqb = t_ // Q_BLK
    scale = HEAD_DIM ** -0.5
    n_cmp = kc.shape[1] + 1

    def q_rows(a):
        a = a.reshape(bsz, nqb, Q_BLK, NSA_KV_HEADS, NSA_GROUP, a.shape[-1])
        return a.transpose(0, 3, 1, 4, 2, 5).reshape(bsz, NSA_KV_HEADS, nqb, NSA_ROWS, a.shape[-1])

    qr = q_rows((q_raw * scale).astype(bf16))
    qo = q_rows((q_rot * scale).astype(bf16))
    gt = q_rows(gates)
    kc_p = jnp.pad(kc, ((0, 0), (0, 1), (0, 0), (0, 0))).transpose(0, 2, 1, 3).astype(bf16)
    vct = jnp.pad(vc, ((0, 0), (0, 1), (0, 0), (0, 0))).transpose(0, 2, 3, 1).astype(bf16)
    onehot = (jnp.arange(t_)[:, None] // SEL_BLK == jnp.arange(N_SELB)[None, :]).astype(bf16)
    onehot = jnp.broadcast_to(onehot, (bsz, NSA_KV_HEADS, t_, N_SELB))
    kk = jnp.concatenate([rows_full[:, :, 2].transpose(0, 2, 1, 3).astype(bf16),
                          rows_win[:, :, 0].transpose(0, 2, 1, 3).astype(bf16), onehot], axis=-1)
    vv = jnp.concatenate([rows_full[:, :, 3].transpose(0, 2, 1, 3),
                          rows_win[:, :, 1].transpose(0, 2, 1, 3)], axis=-1).astype(bf16)
    grid = (bsz, NSA_KV_HEADS, nqb)
    per_blk = lambda b, h, i: (b, h, i, 0, 0)
    per_head = lambda b, h, i: (b, h, 0, 0)
    o = pl.pallas_call(
        _nsa_prompt_body,
        grid=grid,
        in_specs=[pl.BlockSpec((1, 1, 1, NSA_ROWS, HEAD_DIM), per_blk),
                  pl.BlockSpec((1, 1, 1, NSA_ROWS, HEAD_DIM), per_blk),
                  pl.BlockSpec((1, 1, n_cmp, HEAD_DIM), per_head),
                  pl.BlockSpec((1, 1, HEAD_DIM, n_cmp), per_head),
                  pl.BlockSpec((1, 1, t_, 2 * HEAD_DIM + N_SELB), per_head),
                  pl.BlockSpec((1, 1, t_, 2 * HEAD_DIM), per_head),
                  pl.BlockSpec((1, 1, 1, NSA_ROWS, 3), per_blk)],
        out_specs=pl.BlockSpec((1, 1, 1, NSA_ROWS, HEAD_DIM), per_blk),
        out_shape=jax.ShapeDtypeStruct((bsz, NSA_KV_HEADS, nqb, NSA_ROWS, HEAD_DIM), jnp.float32),
        scratch_shapes=[pltpu.VMEM((8 + n_cmp, Q_BLK), jnp.float32),
                        pltpu.VMEM((NSA_ROWS, 1), jnp.float32),
                        pltpu.VMEM((NSA_ROWS, 1), jnp.float32),
                        pltpu.VMEM((NSA_ROWS, 2 * HEAD_DIM), jnp.float32)],
        compiler_params=pltpu.CompilerParams(
            dimension_semantics=("arbitrary", "arbitrary", "arbitrary"),
            vmem_limit_bytes=48 * 1024 * 1024),
        name="nsa_prompt",
    )(qr, qo, kc_p, vct, kk, vv, gt)
    o = o.reshape(bsz, NSA_KV_HEADS, nqb, NSA_GROUP, Q_BLK, HEAD_DIM).transpose(0, 2, 4, 1, 3, 5)
    return o.reshape(bsz, t_, NSA_HEADS * HEAD_DIM)


GLA_SUB = 16
GLA_QK = GLA_HEADS * GLA_DK
GLA_V = GLA_HEADS * GLA_DV


def _dot_tn(a, b):
    return lax.dot_general(a, b, (((0,), (0,)), ((), ())), preferred_element_type=jnp.float32)


def _gla_body(q_ref, k_ref, v_ref, gr_ref, glr_ref, wg_ref, bg_ref, ng_ref, s0_ref, exp_ref, bd_ref,
              o_ref, sfin_ref, st_ref, b_ref, qd_ref, *, t_valid):
    f32, bf16 = jnp.float32, jnp.bfloat16
    tt = q_ref.shape[1]
    ti = pl.program_id(1)

    @pl.when(ti == 0)
    def _():
        st_ref[...] = s0_ref[0]

    row = lax.broadcasted_iota(jnp.int32, (tt, 1), 0)
    z = _dot(glr_ref[0][:, :GLA_RANK].astype(bf16), wg_ref[...]) + bg_ref[...]
    la = (jnp.minimum(z, 0.0) - jnp.log1p(jnp.exp(-jnp.abs(z)))) * (1.0 / GLA_TAU)
    la = jnp.where(ti * tt + row < t_valid, la, 0.0)
    seg = row & (GLA_SUB - 1)
    b = la
    for s in (1, 2, 4, 8):
        b = b + jnp.where(seg >= s, pltpu.roll(b, s, axis=0), 0.0)
    q = q_ref[0] * (GLA_DK ** -0.5)
    k = k_ref[0]
    v = v_ref[0]
    o = _dot((q * k).astype(bf16), exp_ref[...]) * v
    for d in range(1, GLA_SUB):
        decay = jnp.exp(jnp.minimum(b - pltpu.roll(b, d, axis=0), 0.0))
        w = jnp.where(seg >= d, q * pltpu.roll(k, d, axis=0) * decay, 0.0)
        o = o + _dot(w.astype(bf16), exp_ref[...]) * pltpu.roll(v, d, axis=0)
    o_ref[0] = o
    b_ref[...] = b
    qd_ref[...] = (q * jnp.exp(b)).astype(bf16)

    def block_step(c, carry):
        rows = pl.ds(pl.multiple_of(c * GLA_SUB, GLA_SUB), GLA_SUB)
        st = st_ref[...]
        o_ref[0, rows, :] += _dot_nt(qd_ref[rows, :], st.astype(bf16))
        bc = b_ref[rows, :]
        bl = bc[GLA_SUB - 1:GLA_SUB, :]
        kc = (k_ref[0, rows, :] * jnp.exp(bl - bc)).astype(bf16)
        upd = _dot_tn(v_ref[0, rows, :].astype(bf16), kc)
        st_ref[...] = st * jnp.exp(bl) + upd * bd_ref[...]
        return carry

    lax.fori_loop(0, tt // GLA_SUB, block_step, 0)
    sfin_ref[0] = st_ref[...]
    gr = gr_ref[0]
    gate = gr * jax.nn.sigmoid(gr)
    for h in range(GLA_HEADS):
        cols = slice(h * GLA_DV, (h + 1) * GLA_DV)
        oh = o_ref[0, :, cols]
        ms = jnp.mean(oh * oh, axis=-1, keepdims=True)
        o_ref[0, :, cols] = oh * lax.rsqrt(ms + LN_EPS) * ng_ref[...] * gate[:, cols]


def _gla(h, w_gla_gate, b_gla_gate, gla_norm_g, gla_state):
    bsz, t_, n_in = h.shape
    tp = -(-t_ // GLA_SUB) * GLA_SUB
    if tp != t_:
        h = jnp.pad(h, ((0, 0), (0, tp - t_), (0, 0)))
    tt = min(tp, 256)
    heads = np.arange(GLA_HEADS)
    expand = np.repeat(np.repeat(np.eye(GLA_HEADS, dtype=np.float32), GLA_DK, 0), GLA_DV, 1)
    bdmask = jnp.asarray(expand.T)
    if gla_state is None:
        s0 = jnp.zeros((bsz, GLA_V, GLA_QK), jnp.float32)
    else:
        s0 = jnp.zeros((bsz, GLA_HEADS, GLA_DV, GLA_HEADS, GLA_DK), jnp.float32)
        s0 = s0.at[:, heads, :, heads, :].set(gla_state.transpose(1, 0, 3, 2)).reshape(bsz, GLA_V, GLA_QK)
    tile = lambda width, blk: pl.BlockSpec((1, tt, width), lambda b, i: (b, i, blk))
    fixed2 = lambda shape: pl.BlockSpec(shape, lambda b, i: (0, 0))
    per_b = pl.BlockSpec((1, GLA_V, GLA_QK), lambda b, i: (b, 0, 0))
    o, s_t = pl.pallas_call(
        functools.partial(_gla_body, t_valid=t_),
        grid=(bsz, tp // tt),
        in_specs=[tile(GLA_QK, 0), tile(GLA_QK, 1), tile(GLA_V, 1), tile(GLA_V, 2),
                  tile(LANE, (2 * GLA_QK + 2 * GLA_V) // LANE),
                  fixed2((GLA_RANK, GLA_QK)), fixed2((1, GLA_QK)), fixed2((1, GLA_DV)), per_b,
                  fixed2((GLA_QK, GLA_V)), fixed2((GLA_V, GLA_QK))],
        out_specs=[pl.BlockSpec((1, tt, GLA_V), lambda b, i: (b, i, 0)), per_b],
        out_shape=[jax.ShapeDtypeStruct((bsz, tp, GLA_V), jnp.float32),
                   jax.ShapeDtypeStruct((bsz, GLA_V, GLA_QK), jnp.float32)],
        scratch_shapes=[pltpu.VMEM((GLA_V, GLA_QK), jnp.float32), pltpu.VMEM((tt, GLA_QK), jnp.float32),
                        pltpu.VMEM((tt, GLA_QK), jnp.bfloat16)],
        compiler_params=pltpu.CompilerParams(dimension_semantics=("arbitrary", "arbitrary"),
                                             vmem_limit_bytes=48 * 1024 * 1024),
        name="gla",
    )(h, h, h, h, h, w_gla_gate.astype(jnp.bfloat16), b_gla_gate.reshape(1, GLA_QK),
      gla_norm_g.reshape(1, GLA_DV), s0, jnp.asarray(expand, jnp.bfloat16), bdmask)
    s_new = s_t.reshape(bsz, GLA_HEADS, GLA_DV, GLA_HEADS, GLA_DK)[:, heads, :, heads, :]
    return o[:, :t_], s_new.transpose(1, 0, 3, 2)


def _ab_mixer(x, pos, w_in, w_gla_gate, b_gla_gate, gla_norm_g, w_cmp_pool, w_out,
              gla_state, nsa_cache, page_table, win_buf):
    bsz, t_, _ = x.shape
    h_in = _mm3(x, w_in)
    gq, gk, gv, gr, glr, nq, nkv, ngate = _split_cols(h_in, GLA_SIZES + NSA_SIZES)
    o_a, s_a = _gla(h_in, w_gla_gate, b_gla_gate, gla_norm_g, gla_state)
    q_raw = nq.reshape(bsz, t_, NSA_KV_HEADS, NSA_GROUP, HEAD_DIM)
    q_rot = _partial_rope(q_raw, pos)
    kv = nkv.reshape(bsz, t_, 6, NSA_KV_HEADS, HEAD_DIM)
    k_sel = _partial_rope(kv[:, :, 2], pos)
    k_win = _partial_rope(kv[:, :, 4], pos)
    rows_full = jnp.stack([kv[:, :, 0], kv[:, :, 1], k_sel, kv[:, :, 3]], axis=2)
    rows_win = jnp.stack([k_win, kv[:, :, 5]], axis=2)
    gates = jax.nn.sigmoid(ngate).reshape(bsz, t_, NSA_KV_HEADS, NSA_GROUP, 3)
    if nsa_cache is None:
        keys = rows_full
    else:
        past = nsa_cache[page_table].reshape(bsz, -1, 4, NSA_KV_HEADS, HEAD_DIM)
        keys = jnp.concatenate([past, rows_full], axis=1)
    length = keys.shape[1]
    kc, vc, cend = _compress(keys[:, :, 0], keys[:, :, 1], w_cmp_pool)
    n_sel = -(-length // SEL_BLK)
    ksb = _to_sel_blocks(keys[:, :, 2], n_sel)
    vsb = _to_sel_blocks(keys[:, :, 3], n_sel)
    if nsa_cache is None:
        o_b = _nsa_prompt(q_raw, q_rot, gates, kc, vc, rows_full, rows_win)
        new_win = rows_win[:, -min(WINDOW, t_):]
    else:
        w_buf = win_buf.shape[1]
        kw = jnp.concatenate([win_buf, rows_win], axis=1)
        kwpos = (length - t_) - w_buf + jnp.arange(kw.shape[1])
        o_b = _nsa_attend(q_raw, q_rot, pos, gates, kc, vc, cend, ksb, vsb, kw[:, :, 0], kw[:, :, 1], kwpos)
        o_b = o_b.reshape(bsz, t_, -1)
        new_win = kw[:, -w_buf:]
    y = _mm3(jnp.concatenate([o_a, o_b], axis=-1), w_out)
    return y, s_a, rows_full, new_win


def _conv_module(x, conv_buf, w_pw1, b_pw1, w_dw, b_dw, ln_g, ln_b, w_pw2, b_pw2):
    bsz = x.shape[0]
    a, g = jnp.split(_mm3(x, w_pw1) + b_pw1, 2, axis=-1)
    u = a * jax.nn.sigmoid(g)
    if conv_buf is None:
        conv_buf = jnp.zeros((bsz, CONV_W - 1, D_CONV), u.dtype)
    ext = jnp.concatenate([conv_buf, u], axis=1)
    c = lax.conv_general_dilated(ext, w_dw[:, None, :], (1,), 'VALID',
                                 dimension_numbers=('NWC', 'WIO', 'NWC'),
                                 feature_group_count=D_CONV) + b_dw
    c = jax.nn.silu(_layer_norm(c, ln_g, ln_b))
    return _mm3(c, w_pw2) + b_pw2, ext[:, -(CONV_W - 1):]


PER_GROUP = N_EXPERTS // N_GROUPS
PICKED = -3e38


def _ln_rows(v, g, b):
    mu = jnp.mean(v, axis=-1, keepdims=True)
    c = v - mu
    var = jnp.mean(c * c, axis=-1, keepdims=True)
    return c * lax.rsqrt(var + LN_EPS) * g + b


def _first_max(v, ids, axes, sentinel):
    best = v
    for a in axes:
        best = jnp.max(best, axis=a, keepdims=True)
    first = jnp.where(v == best, ids, sentinel)
    for a in axes:
        first = jnp.min(first, axis=a, keepdims=True)
    return best, first


def _sum_axes(v, axes):
    for a in axes:
        v = jnp.sum(v, axis=a, keepdims=True)
    return v


def _moe_pre_body(x_ref, mix_ref, g_ref, b_ref, wr_ref, br_ref, wgu_ref, wdn_ref,
                  x1_ref, x1b_ref, sh_ref, eidx_ref, gate_ref, rank_ref, cnt_ref, run_ref):
    f32, bf16 = jnp.float32, jnp.bfloat16
    tm = x_ref.shape[0]

    @pl.when(pl.program_id(0) == 0)
    def _():
        run_ref[...] = jnp.zeros(run_ref.shape, f32)

    x1 = _ln_rows(ALPHA * x_ref[...] + mix_ref[...], g_ref[...], b_ref[...])
    x1_ref[...] = x1
    x1b = x1.astype(bf16)
    x1b_ref[...] = x1b

    h = _dot(x1b, wgu_ref[...])
    d_sh = h.shape[1] // 2
    act = (jax.nn.silu(h[:, :d_sh]) * h[:, d_sh:]).astype(bf16)
    sh_ref[...] = _dot(act, wdn_ref[...])

    s = jax.nn.sigmoid(_dot_nt(wr_ref[...], x1b)).reshape(N_GROUPS, PER_GROUP, tm)
    sb = s + br_ref[...].reshape(N_GROUPS, PER_GROUP, 1)
    shape3 = (N_GROUPS, PER_GROUP, tm)
    pid = lax.broadcasted_iota(jnp.int32, shape3, 1)
    gid = lax.broadcasted_iota(jnp.int32, (N_GROUPS, 1, tm), 0)
    eid = lax.broadcasted_iota(jnp.int32, shape3, 0) * PER_GROUP + pid
    top1, i1 = _first_max(sb, pid, (1,), PER_GROUP)
    top2 = jnp.max(jnp.where(pid == i1, PICKED, sb), axis=1, keepdims=True)
    gscore = top1 + top2
    gsel = jnp.zeros((N_GROUPS, 1, tm), f32)
    for _ in range(TOPK_GROUPS):
        _, first = _first_max(gscore, gid, (0,), N_GROUPS)
        hit = gid == first
        gsel = jnp.where(hit, 1.0, gsel)
        gscore = jnp.where(hit, PICKED, gscore)
    cand = jnp.where(gsel > 0.0, sb, -1e30)
    firsts, gates = [], []
    picked = jnp.zeros(shape3, f32)
    for _ in range(TOP_K):
        _, first = _first_max(cand, eid, (0, 1), N_EXPERTS)
        hit = eid == first
        firsts.append(first)
        gates.append(_sum_axes(jnp.where(hit, s, 0.0), (0, 1)))
        picked = jnp.where(hit, 1.0, picked)
        cand = jnp.where(hit, PICKED, cand)
    gsum = gates[0]
    for gk in gates[1:]:
        gsum = gsum + gk
    earlier = (lax.broadcasted_iota(jnp.int32, (tm, tm), 0) < lax.broadcasted_iota(jnp.int32, (tm, tm), 1))
    picked2 = picked.reshape(N_EXPERTS, tm)
    rank = run_ref[...] + _dot(picked2.astype(bf16), jnp.where(earlier, 1.0, 0.0).astype(bf16))
    run_new = run_ref[...] + jnp.sum(picked2, axis=1, keepdims=True)
    run_ref[...] = run_new
    cnt_ref[...] = jnp.broadcast_to(run_new, cnt_ref.shape)
    rank3 = rank.reshape(shape3)
    for k in range(TOP_K):
        hit = eid == firsts[k]
        eidx_ref[k:k + 1, :] = firsts[k].reshape(1, tm)
        gate_ref[k:k + 1, :] = (gates[k] / gsum * ROUTE_SCALE).reshape(1, tm)
        rank_ref[k:k + 1, :] = _sum_axes(jnp.where(hit, rank3, 0.0), (0, 1)).reshape(1, tm).astype(jnp.int32)


def _moe_pre(x, mix, g, b, w_router, b_router, w_sh_gu, w_sh_down):
    m, d = x.shape
    bf16 = jnp.bfloat16
    tm = min(m, 512)
    row = lambda i: (i, 0)
    col = lambda i: (0, i)
    fixed = lambda i: (0, 0)
    d_sh2 = w_sh_gu.shape[1]
    return pl.pallas_call(
        _moe_pre_body,
        grid=(m // tm,),
        in_specs=[pl.BlockSpec((tm, d), row), pl.BlockSpec((tm, d), row),
                  pl.BlockSpec((1, d), fixed), pl.BlockSpec((1, d), fixed),
                  pl.BlockSpec((N_EXPERTS, d), fixed), pl.BlockSpec((N_EXPERTS, 1), fixed),
                  pl.BlockSpec((d, d_sh2), fixed), pl.BlockSpec((d_sh2 // 2, d), fixed)],
        out_specs=[pl.BlockSpec((tm, d), row), pl.BlockSpec((tm, d), row), pl.BlockSpec((tm, d), row),
                   pl.BlockSpec((TOP_K, tm), col), pl.BlockSpec((TOP_K, tm), col), pl.BlockSpec((TOP_K, tm), col),
                   pl.BlockSpec((N_EXPERTS, LANE), fixed)],
        out_shape=[jax.ShapeDtypeStruct((m, d), jnp.float32), jax.ShapeDtypeStruct((m, d), bf16),
                   jax.ShapeDtypeStruct((m, d), jnp.float32),
                   jax.ShapeDtypeStruct((TOP_K, m), jnp.int32), jax.ShapeDtypeStruct((TOP_K, m), jnp.float32),
                   jax.ShapeDtypeStruct((TOP_K, m), jnp.int32),
                   jax.ShapeDtypeStruct((N_EXPERTS, LANE), jnp.float32)],
        scratch_shapes=[pltpu.VMEM((N_EXPERTS, 1), jnp.float32)],
        compiler_params=pltpu.CompilerParams(dimension_semantics=("arbitrary",),
                                             vmem_limit_bytes=48 * 1024 * 1024),
        name="moe_pre",
    )(x, mix, g.reshape(1, d), b.reshape(1, d), w_router.T.astype(bf16), b_router.reshape(N_EXPERTS, 1),
      w_sh_gu.astype(bf16), w_sh_down.astype(bf16))


def _moe_expert_body(exp_ref, first_ref, active_ref, xs_ref, wgu_ref, wdn_ref, y_ref, wgu_bf, wdn_bf):
    i = pl.program_id(0)
    bf16 = jnp.bfloat16

    @pl.when(first_ref[i] == 1)
    def _():
        wgu_bf[...] = wgu_ref[0].astype(bf16)
        wdn_bf[...] = wdn_ref[0].astype(bf16)

    @pl.when(active_ref[i] == 1)
    def _():
        h = _dot(xs_ref[...], wgu_bf[...])
        d_e = h.shape[1] // 2
        act = (jax.nn.silu(h[:, :d_e]) * h[:, d_e:]).astype(bf16)
        y_ref[...] = _dot(act, wdn_bf[...]).astype(y_ref.dtype)

    @pl.when(active_ref[i] == 0)
    def _():
        y_ref[...] = jnp.zeros(y_ref.shape, y_ref.dtype)


def _moe_experts(xs, blk_exp, blk_first, blk_active, w_exp_gu, w_exp_down, bm):
    n_slots, d = xs.shape
    n_blk = n_slots // bm
    d_e2 = w_exp_gu.shape[2]
    grid_spec = pltpu.PrefetchScalarGridSpec(
        num_scalar_prefetch=3,
        grid=(n_blk,),
        in_specs=[pl.BlockSpec((bm, d), lambda i, e, f, a: (i, 0)),
                  pl.BlockSpec((1, d, d_e2), lambda i, e, f, a: (e[i], 0, 0)),
                  pl.BlockSpec((1, d_e2 // 2, d), lambda i, e, f, a: (e[i], 0, 0))],
        out_specs=pl.BlockSpec((bm, d), lambda i, e, f, a: (i, 0)),
        scratch_shapes=[pltpu.VMEM((d, d_e2), jnp.bfloat16), pltpu.VMEM((d_e2 // 2, d), jnp.bfloat16)])
    return pl.pallas_call(
        _moe_expert_body,
        grid_spec=grid_spec,
        out_shape=jax.ShapeDtypeStruct((n_slots, d), jnp.bfloat16),
        compiler_params=pltpu.CompilerParams(dimension_semantics=("arbitrary",),
                                             vmem_limit_bytes=48 * 1024 * 1024),
        name="moe_experts",
    )(blk_exp, blk_first, blk_active, xs, w_exp_gu, w_exp_down)


def _add_ln_body(x_ref, a_ref, b2_ref, g_ref, b_ref, o_ref):
    o_ref[...] = _ln_rows(ALPHA * x_ref[...] + (a_ref[...] + b2_ref[...]), g_ref[...], b_ref[...])


def _add_ln(x, a, b2, g, b):
    m, d = x.shape
    tm = min(m, 512)
    row = lambda i: (i, 0)
    fixed = lambda i: (0, 0)
    return pl.pallas_call(
        _add_ln_body,
        grid=(m // tm,),
        in_specs=[pl.BlockSpec((tm, d), row)] * 3 + [pl.BlockSpec((1, d), fixed)] * 2,
        out_specs=pl.BlockSpec((tm, d), row),
        out_shape=jax.ShapeDtypeStruct((m, d), jnp.float32),
        compiler_params=pltpu.CompilerParams(dimension_semantics=("arbitrary",)),
        name="add_ln",
    )(x, a, b2, g.reshape(1, d), b.reshape(1, d))


def _moe_layer(x, mix, ln1_g, ln1_b, ln2_g, ln2_b, w_router, b_router, w_exp_gu, w_exp_down, w_sh_gu, w_sh_down):
    m, d = x.shape
    x1, x1b, shared, eidx, gate8, rank8, counts = _moe_pre(x, mix, ln1_g, ln1_b, w_router, b_router,
                                                           w_sh_gu, w_sh_down)
    bm = 256 if m * TOP_K >= 256 * N_EXPERTS else MOE_BLK
    n_blk = (m * TOP_K) // bm + N_EXPERTS
    counts = counts[:, 0].astype(jnp.int32)
    padded = (counts + bm - 1) // bm * bm
    pad_end = jnp.cumsum(padded)
    dest = ((pad_end - padded)[eidx] + rank8).reshape(-1)
    tok = jnp.tile(jnp.arange(m, dtype=jnp.int32), TOP_K)
    slot_tok = jnp.zeros((n_blk * bm,), jnp.int32).at[dest].set(tok)
    blk_start = jnp.arange(n_blk, dtype=jnp.int32) * bm
    blk_exp = jnp.minimum(jnp.sum(pad_end[None, :] <= blk_start[:, None], axis=1), N_EXPERTS - 1).astype(jnp.int32)
    blk_active = (blk_start < pad_end[-1]).astype(jnp.int32)
    blk_first = jnp.concatenate([jnp.ones((1,), jnp.int32), (blk_exp[1:] != blk_exp[:-1]).astype(jnp.int32)])
    xs = x1b[slot_tok]
    y = _moe_experts(xs, blk_exp, blk_first, blk_active, w_exp_gu, w_exp_down, bm)
    routed = (y[dest].reshape(TOP_K, m, d).astype(jnp.float32) * gate8[:, :, None]).sum(axis=0)
    return _add_ln(x1, routed, shared, ln2_g, ln2_b)


def _trunk(x, pos, gla_state, nsa_cache, page_table, win_buf, conv_buf,
           w_in_ab, w_gla_gate, b_gla_gate, gla_norm_g, w_cmp_pool, w_out_ab,
           w_pw1, b_pw1, w_dw, b_dw, conv_ln_g, conv_ln_b, w_pw2, b_pw2,
           ln_g, ln_b, w_router, b_router, w_exp_gu, w_exp_down, w_sh_gu, w_sh_down):
    new_gla, new_rows, new_win, new_conv = [], [], [], []
    for layer in range(DEPTH):
        i = layer // 2
        if layer % 2 == 0:
            mix, s_a, rows, win = _ab_mixer(
                x, pos, w_in_ab[i], w_gla_gate[i], b_gla_gate[i], gla_norm_g[i], w_cmp_pool[i], w_out_ab[i],
                None if gla_state is None else gla_state[i],
                None if nsa_cache is None else nsa_cache[i], page_table,
                None if win_buf is None else win_buf[i])
            new_gla.append(s_a)
            new_rows.append(rows)
            new_win.append(win)
        else:
            mix, cb = _conv_module(x, None if conv_buf is None else conv_buf[i], w_pw1[i], b_pw1[i],
                                   w_dw[i], b_dw[i], conv_ln_g[i], conv_ln_b[i], w_pw2[i], b_pw2[i])
            new_conv.append(cb)
        bsz, t_, d = x.shape
        x = _moe_layer(x.reshape(-1, d), mix.reshape(-1, d), ln_g[layer, 0], ln_b[layer, 0],
                       ln_g[layer, 1], ln_b[layer, 1], w_router[layer], b_router[layer],
                       w_exp_gu[layer], w_exp_down[layer], w_sh_gu[layer], w_sh_down[layer]).reshape(bsz, t_, d)
    return x, jnp.stack(new_gla), jnp.stack(new_rows), jnp.stack(new_win), jnp.stack(new_conv)


def kernel(x_prompt, x_sample, state_gla, cache_nsa_kv, state_nsa_win, state_conv, page_table,
           w_in_ab, w_gla_gate, b_gla_gate, gla_norm_g, w_cmp_pool, w_out_ab,
           w_pw1, b_pw1, w_dw, b_dw, conv_ln_g, conv_ln_b, w_pw2, b_pw2,
           ln_g, ln_b, w_router, b_router, w_exp_gu, w_exp_down, w_sh_gu, w_sh_down):
    weights = (w_in_ab, w_gla_gate, b_gla_gate, gla_norm_g, w_cmp_pool, w_out_ab,
               w_pw1, b_pw1, w_dw, b_dw, conv_ln_g, conv_ln_b, w_pw2, b_pw2,
               ln_g, ln_b, w_router, b_router, w_exp_gu, w_exp_down, w_sh_gu, w_sh_down)
    past_len = page_table.shape[1] * PAGE_SIZE
    pos_p = jnp.arange(x_prompt.shape[1])
    pos_s = past_len + jnp.arange(x_sample.shape[1])
    y_prompt, gla_p, rows_p, win_p, conv_p = _trunk(x_prompt, pos_p, None, None, None, None, None, *weights)
    y_sample, gla_s, rows_s, win_s, conv_s = _trunk(x_sample, pos_s, state_gla, cache_nsa_kv, page_table,
                                                    state_nsa_win, state_conv, *weights)
    return (y_prompt, y_sample, gla_p, gla_s, rows_p, rows_s, win_p, win_s, conv_p, conv_s)
```

```python
import functools
import math

import jax
import jax.numpy as jnp
import numpy as np
from jax import lax
from jax.experimental import pallas as pl
from jax.experimental.pallas import tpu as pltpu
from jax.experimental.pallas import tpu_sc as plsc

D_MODEL = 1024
DEPTH = 2
PAGE_SIZE = 128

GLA_HEADS = 4
GLA_DV = D_MODEL // 2 // GLA_HEADS
GLA_DK = GLA_DV // 2
GLA_RANK = 16
GLA_TAU = 16.0
GLA_CHUNK = 64

NSA_HEADS = 8
NSA_KV_HEADS = 2
NSA_GROUP = NSA_HEADS // NSA_KV_HEADS
HEAD_DIM = D_MODEL // 2 // NSA_HEADS
CMP_BLK = 32
CMP_STRIDE = 16
SEL_BLK = 64
SEL_TOPN = 16
WINDOW = 512
Q_BLK = 128
FORCE_BONUS = 100.0
ROPE_DIM = HEAD_DIM // 4
ROPE_THETA = 500000.0

GLA_SIZES = (GLA_HEADS * GLA_DK, GLA_HEADS * GLA_DK, GLA_HEADS * GLA_DV, GLA_HEADS * GLA_DV, GLA_RANK)
NSA_SIZES = (NSA_HEADS * HEAD_DIM, 6 * NSA_KV_HEADS * HEAD_DIM, 3 * NSA_HEADS)

CONV_W = 31
D_CONV = D_MODEL

N_EXPERTS = 64
N_GROUPS = 8
TOPK_GROUPS = 4
TOP_K = 8
D_EXPERT = 256
ROUTE_SCALE = 2.5
MOE_BLK = 128

ALPHA = (2 * DEPTH) ** 0.25
LN_EPS = 1e-5

LANE = 128


def _dot(a, b):
    return jnp.dot(a, b, preferred_element_type=jnp.float32)


def _dot_nt(a, b):
    return lax.dot_general(a, b, (((1,), (1,)), ((), ())), preferred_element_type=jnp.float32)


def _mm_body(x_ref, w_ref, o_ref):
    o_ref[...] = _dot(x_ref[...].astype(jnp.bfloat16), w_ref[...].astype(jnp.bfloat16))


def _mm(x, w):
    m, k = x.shape
    n = w.shape[1]
    n_pad = -(-n // LANE) * LANE
    if n_pad != n:
        w = jnp.pad(w, ((0, 0), (0, n_pad - n)))
    tm = min(m, 512)
    tn = next(t for t in (512, 256, 128) if n_pad % t == 0)
    out = pl.pallas_call(
        _mm_body,
        grid=(m // tm, n_pad // tn),
        in_specs=[pl.BlockSpec((tm, k), lambda i, j: (i, 0)),
                  pl.BlockSpec((k, tn), lambda i, j: (0, j))],
        out_specs=pl.BlockSpec((tm, tn), lambda i, j: (i, j)),
        out_shape=jax.ShapeDtypeStruct((m, n_pad), jnp.float32),
        compiler_params=pltpu.CompilerParams(dimension_semantics=("arbitrary", "arbitrary")),
        name="mm",
    )(x, w)
    return out[:, :n] if n_pad != n else out


def _mm3(x, w):
    b, t, d = x.shape
    return _mm(x.reshape(b * t, d), w).reshape(b, t, -1)


def _split_cols(h, sizes):
    return jnp.split(h, np.cumsum(sizes)[:-1].tolist(), axis=-1)


def _layer_norm(x, g, b):
    mu = x.mean(-1, keepdims=True)
    var = jnp.square(x - mu).mean(-1, keepdims=True)
    return (x - mu) * lax.rsqrt(var + LN_EPS) * g + b


def _rms_norm(x, g):
    return x * lax.rsqrt(jnp.mean(x * x, -1, keepdims=True) + LN_EPS) * g


def _partial_rope(x, pos):
    half = ROPE_DIM // 2
    inv_freq = jnp.power(ROPE_THETA, -jnp.arange(half, dtype=jnp.float32) / half)
    ang = pos.astype(jnp.float32)[:, None] * inv_freq
    ang = ang.reshape(ang.shape[0], *([1] * (x.ndim - 3)), half)
    cos, sin = jnp.cos(ang), jnp.sin(ang)
    x1 = x[..., :half]
    x2 = x[..., half:ROPE_DIM]
    rot = jnp.concatenate([x1 * cos - x2 * sin, x2 * cos + x1 * sin], -1)
    return jnp.concatenate([rot, x[..., ROPE_DIM:]], -1)


def _masked_softmax(s, mask):
    s = jnp.where(mask, s, -jnp.inf)
    m = jnp.max(s, axis=-1, keepdims=True)
    m = jnp.where(jnp.isfinite(m), m, 0.0)
    p = jnp.exp(s - m)
    return p / jnp.maximum(p.sum(-1, keepdims=True), 1e-30)


def _gla_recurrence(q, k, v, log_a, s0):
    bsz, t_, nh, _ = q.shape
    c = math.gcd(t_, GLA_CHUNK)
    n = t_ // c

    def chunks(a):
        return jnp.moveaxis(a.reshape(bsz, n, c, *a.shape[2:]), 1, 0)

    causal = jnp.tril(jnp.ones((c, c), dtype=bool))[None, :, :, None, None]

    def step(s, inp):
        qc, kc, vc, lc = inp
        bc = jnp.cumsum(lc, axis=1)
        decay = jnp.exp(jnp.where(causal, bc[:, :, None] - bc[:, None, :], -jnp.inf))
        attn = jnp.einsum('bijhd,bjhd->bhij', qc[:, :, None] * decay, kc)
        o = jnp.einsum('bhij,bjhe->bihe', attn, vc) + jnp.einsum('bihd,bhde->bihe', qc * jnp.exp(bc), s)
        bl = bc[:, -1]
        s = jnp.exp(bl)[..., None] * s + jnp.einsum('bjhd,bjhe->bhde', kc * jnp.exp(bl[:, None] - bc), vc)
        return s, o

    s_fin, o = lax.scan(step, s0, (chunks(q), chunks(k), chunks(v), chunks(log_a)))
    return jnp.moveaxis(o, 0, 1).reshape(bsz, t_, nh, -1), s_fin


def _compress(k, v, w_pool):
    bsz, length = k.shape[:2]
    n_sub = length // CMP_STRIDE

    def pool(a, w):
        sub = a[:, :n_sub * CMP_STRIDE].reshape(bsz, n_sub, CMP_STRIDE, *a.shape[2:])
        first = jnp.einsum('bnjhd,j->bnhd', sub, w[:CMP_STRIDE])
        second = jnp.einsum('bnjhd,j->bnhd', sub, w[CMP_STRIDE:])
        return first[:, :-1] + second[:, 1:]

    cend = jnp.arange(n_sub - 1) * CMP_STRIDE + CMP_BLK - 1
    return pool(k, w_pool[0]), pool(v, w_pool[1]), cend


def _to_sel_blocks(a, n_sel):
    bsz, length = a.shape[:2]
    a = jnp.pad(a, ((0, 0), (0, n_sel * SEL_BLK - length), (0, 0), (0, 0)))
    return a.reshape(bsz, n_sel, SEL_BLK, NSA_KV_HEADS, HEAD_DIM).transpose(0, 3, 1, 2, 4)


def _nsa_attend(q_raw, q_rot, qpos, gates, kc, vc, cend, ksb, vsb, kw, vw, kwpos):
    scale = HEAD_DIM ** -0.5
    bsz, tq = q_raw.shape[:2]
    n_cmp, n_sel = kc.shape[1], ksb.shape[2]
    s_c = jnp.einsum('bqhgd,bnhd->bhgqn', q_raw, kc) * scale
    p_c = _masked_softmax(s_c, cend[None, :] <= qpos[:, None])
    o_c = jnp.einsum('bhgqn,bnhd->bqhgd', p_c, vc)
    ratio = SEL_BLK // CMP_STRIDE
    imp = p_c.sum(axis=2)
    imp = jnp.pad(imp, ((0, 0), (0, 0), (0, 0), (1, ratio * (n_sel + 1) - 1 - n_cmp)))
    imp = imp.reshape(bsz, NSA_KV_HEADS, tq, n_sel + 1, ratio)
    imp_s = imp[..., :n_sel, :].sum(-1) + imp[..., 1:, 0]
    blk = jnp.arange(n_sel)[None, :]
    cur = (qpos // SEL_BLK)[:, None]
    valid = blk * SEL_BLK <= qpos[:, None]
    forced = (blk == 0) | (blk == cur) | (blk == cur - 1)
    score = jnp.where(valid, imp_s + jnp.where(forced, FORCE_BONUS, 0.0), -jnp.inf)
    k_top = min(SEL_TOPN, n_sel)
    _, sel = lax.top_k(score, k_top)
    take = jax.vmap(jax.vmap(lambda blocks, idx: blocks[idx]))
    ks = take(ksb, sel).reshape(bsz, NSA_KV_HEADS, tq, k_top * SEL_BLK, HEAD_DIM)
    vs = take(vsb, sel).reshape(bsz, NSA_KV_HEADS, tq, k_top * SEL_BLK, HEAD_DIM)
    kpos = (sel[..., None] * SEL_BLK + jnp.arange(SEL_BLK)).reshape(bsz, NSA_KV_HEADS, tq, k_top * SEL_BLK)
    s_s = jnp.einsum('bqhgd,bhqkd->bhgqk', q_rot, ks) * scale
    p_s = _masked_softmax(s_s, (kpos <= qpos[:, None])[:, :, None])
    o_s = jnp.einsum('bhgqk,bhqkd->bqhgd', p_s, vs)
    s_w = jnp.einsum('bqhgd,bkhd->bhgqk', q_rot, kw) * scale
    kp, qp = kwpos[None, :], qpos[:, None]
    p_w = _masked_softmax(s_w, (kp <= qp) & (kp > qp - WINDOW) & (kp >= 0))
    o_w = jnp.einsum('bhgqk,bkhd->bqhgd', p_w, vw)
    return gates[..., 0:1] * o_c + gates[..., 1:2] * o_s + gates[..., 2:3] * o_w


NSA_ROWS = NSA_GROUP * Q_BLK
SEL_KT = 512
N_SELB = 128
MASKED = -1e9
WIN_KEYS = WINDOW + Q_BLK


def _nsa_prompt_body(qr_ref, qo_ref, kc_ref, vct_ref, kk_ref, vv_ref, g_ref, o_ref,
                     imp_ref, m_ref, l_ref, acc_ref):
    f32, bf16 = jnp.float32, jnp.bfloat16
    qb = pl.program_id(2)
    q0 = qb * Q_BLK
    qr = qr_ref[0, 0, 0]
    qo = qo_ref[0, 0, 0]
    n_cmp = kc_ref.shape[2]

    s_c = _dot_nt(kc_ref[0, 0], qr)
    n_idx = lax.broadcasted_iota(jnp.int32, (n_cmp, NSA_ROWS), 0)
    qpos_c = q0 + (lax.broadcasted_iota(jnp.int32, (n_cmp, NSA_ROWS), 1) & (Q_BLK - 1))
    cmask = (n_idx * CMP_STRIDE + (CMP_BLK - 1)) <= qpos_c
    s_c = jnp.where(cmask, s_c, MASKED)
    m_c = jnp.max(s_c, axis=0, keepdims=True)
    p_c = jnp.where(cmask, jnp.exp(s_c - m_c), 0.0)
    p_c = p_c / jnp.maximum(jnp.sum(p_c, axis=0, keepdims=True), 1e-30)
    o_ct = _dot(vct_ref[0, 0], p_c.astype(bf16))

    imp = (p_c[:, 0:Q_BLK] + p_c[:, Q_BLK:2 * Q_BLK]) + p_c[:, 2 * Q_BLK:3 * Q_BLK] + p_c[:, 3 * Q_BLK:]
    imp_ref[0:8, :] = jnp.zeros((8, Q_BLK), f32)
    imp_ref[8:8 + n_cmp, :] = imp
    ratio = SEL_BLK // CMP_STRIDE
    n_selb = n_cmp // ratio
    imp_s = imp_ref[pl.ds(7, n_selb, stride=ratio), :]
    for r in range(ratio):
        imp_s = imp_s + imp_ref[pl.ds(8 + r, n_selb, stride=ratio), :]
    blk = lax.broadcasted_iota(jnp.int32, (n_selb, Q_BLK), 0)
    qpos_s = q0 + lax.broadcasted_iota(jnp.int32, (n_selb, Q_BLK), 1)
    cur = lax.shift_right_logical(qpos_s, int(math.log2(SEL_BLK)))
    valid = blk * SEL_BLK <= qpos_s
    forced = (blk == 0) | (blk == cur) | (blk == cur - 1)
    score = jnp.where(valid, imp_s + jnp.where(forced, FORCE_BONUS, 0.0), -1e30)
    picked = jnp.zeros((n_selb, Q_BLK), f32)
    for _ in range(SEL_TOPN):
        best = jnp.max(score, axis=0, keepdims=True)
        first = jnp.min(jnp.where(score == best, blk, n_selb), axis=0, keepdims=True)
        hit = blk == first
        picked = jnp.where(hit, 1.0, picked)
        score = jnp.where(hit, -3e38, score)
    selb_t = jnp.where(valid, picked, 0.0)
    if n_selb < N_SELB:
        selb_t = jnp.concatenate([selb_t, jnp.zeros((N_SELB - n_selb, Q_BLK), f32)], axis=0)
    selb = ((selb_t.T - 1.0) * (-MASKED)).astype(bf16)
    selb = jnp.concatenate([selb] * NSA_GROUP, axis=0)

    zeros_q = jnp.zeros((NSA_ROWS, HEAD_DIM), bf16)
    q_sel = jnp.concatenate([qo, zeros_q, selb], axis=1)
    q_win = jnp.concatenate([zeros_q, qo, jnp.zeros((NSA_ROWS, N_SELB), bf16)], axis=1)
    qpos_r = q0 + (lax.broadcasted_iota(jnp.int32, (NSA_ROWS, 1), 0) & (Q_BLK - 1))

    m_ref[...] = jnp.full(m_ref.shape, MASKED, f32)
    l_ref[...] = jnp.zeros(l_ref.shape, f32)
    acc_ref[...] = jnp.zeros(acc_ref.shape, f32)

    def sel_tile(t, causal):
        k0 = pl.multiple_of(t * SEL_KT, SEL_KT)
        s = _dot_nt(q_sel, kk_ref[0, 0, pl.ds(k0, SEL_KT), :])
        if causal:
            kpos = k0 + lax.broadcasted_iota(jnp.int32, (NSA_ROWS, SEL_KT), 1)
            s = jnp.where(kpos <= qpos_r, s, MASKED)
        m_old = m_ref[...]
        m_new = jnp.maximum(m_old, jnp.max(s, axis=1, keepdims=True))
        alpha = jnp.exp(m_old - m_new)
        p = jnp.exp(s - m_new)
        l_ref[...] = alpha * l_ref[...] + jnp.sum(p, axis=1, keepdims=True)
        acc_ref[...] = alpha * acc_ref[...] + _dot(p.astype(bf16), vv_ref[0, 0, pl.ds(k0, SEL_KT), :])
        m_ref[...] = m_new

    n_full = q0 // SEL_KT

    def full_step(t, c):
        sel_tile(t, False)
        return c

    lax.fori_loop(0, n_full, full_step, 0)
    sel_tile(n_full, True)
    o_s = acc_ref[:, 0:HEAD_DIM] / l_ref[...]

    w0 = pl.multiple_of(jnp.maximum(q0 - WINDOW, 0), Q_BLK)
    s_w = _dot_nt(q_win, kk_ref[0, 0, pl.ds(w0, WIN_KEYS), :])
    kpos_w = w0 + lax.broadcasted_iota(jnp.int32, (NSA_ROWS, WIN_KEYS), 1)
    s_w = jnp.where((kpos_w <= qpos_r) & (kpos_w > qpos_r - WINDOW), s_w, MASKED)
    p_w = jnp.exp(s_w - jnp.max(s_w, axis=1, keepdims=True))
    l_w = jnp.sum(p_w, axis=1, keepdims=True)
    acc_w = _dot(p_w.astype(bf16), vv_ref[0, 0, pl.ds(w0, WIN_KEYS), :])
    o_w = acc_w[:, HEAD_DIM:2 * HEAD_DIM] / l_w

    g = g_ref[0, 0, 0]
    o_ref[0, 0, 0] = g[:, 0:1] * o_ct.T + g[:, 1:2] * o_s + g[:, 2:3] * o_w


def _nsa_prompt(q_raw, q_rot, gates, kc, vc, rows_full, rows_win):
    bsz, t_ = q_raw.shape[:2]
    bf16 = jnp.bfloat16
    nqb = t_ // Q_BLK
    scale = HEAD_DIM ** -0.5
    n_cmp = kc.shape[1] + 1

    def q_rows(a):
        a = a.reshape(bsz, nqb, Q_BLK, NSA_KV_HEADS, NSA_GROUP, a.shape[-1])
        return a.transpose(0, 3, 1, 4, 2, 5).reshape(bsz, NSA_KV_HEADS, nqb, NSA_ROWS, a.shape[-1])

    qr = q_rows((q_raw * scale).astype(bf16))
    qo = q_rows((q_rot * scale).astype(bf16))
    gt = q_rows(gates)
    kc_p = jnp.pad(kc, ((0, 0), (0, 1), (0, 0), (0, 0))).transpose(0, 2, 1, 3).astype(bf16)
    vct = jnp.pad(vc, ((0, 0), (0, 1), (0, 0), (0, 0))).transpose(0, 2, 3, 1).astype(bf16)
    onehot = (jnp.arange(t_)[:, None] // SEL_BLK == jnp.arange(N_SELB)[None, :]).astype(bf16)
    onehot = jnp.broadcast_to(onehot, (bsz, NSA_KV_HEADS, t_, N_SELB))
    kk = jnp.concatenate([rows_full[:, :, 2].transpose(0, 2, 1, 3).astype(bf16),
                          rows_win[:, :, 0].transpose(0, 2, 1, 3).astype(bf16), onehot], axis=-1)
    vv = jnp.concatenate([rows_full[:, :, 3].transpose(0, 2, 1, 3),
                          rows_win[:, :, 1].transpose(0, 2, 1, 3)], axis=-1).astype(bf16)
    grid = (bsz, NSA_KV_HEADS, nqb)
    per_blk = lambda b, h, i: (b, h, i, 0, 0)
    per_head = lambda b, h, i: (b, h, 0, 0)
    o = pl.pallas_call(
        _nsa_prompt_body,
        grid=grid,
        in_specs=[pl.BlockSpec((1, 1, 1, NSA_ROWS, HEAD_DIM), per_blk),
                  pl.BlockSpec((1, 1, 1, NSA_ROWS, HEAD_DIM), per_blk),
                  pl.BlockSpec((1, 1, n_cmp, HEAD_DIM), per_head),
                  pl.BlockSpec((1, 1, HEAD_DIM, n_cmp), per_head),
                  pl.BlockSpec((1, 1, t_, 2 * HEAD_DIM + N_SELB), per_head),
                  pl.BlockSpec((1, 1, t_, 2 * HEAD_DIM), per_head),
                  pl.BlockSpec((1, 1, 1, NSA_ROWS, 3), per_blk)],
        out_specs=pl.BlockSpec((1, 1, 1, NSA_ROWS, HEAD_DIM), per_blk),
        out_shape=jax.ShapeDtypeStruct((bsz, NSA_KV_HEADS, nqb, NSA_ROWS, HEAD_DIM), jnp.float32),
        scratch_shapes=[pltpu.VMEM((8 + n_cmp, Q_BLK), jnp.float32),
                        pltpu.VMEM((NSA_ROWS, 1), jnp.float32),
                        pltpu.VMEM((NSA_ROWS, 1), jnp.float32),
                        pltpu.VMEM((NSA_ROWS, 2 * HEAD_DIM), jnp.float32)],
        compiler_params=pltpu.CompilerParams(
            dimension_semantics=("arbitrary", "arbitrary", "arbitrary"),
            vmem_limit_bytes=48 * 1024 * 1024),
        name="nsa_prompt",
    )(qr, qo, kc_p, vct, kk, vv, gt)
    o = o.reshape(bsz, NSA_KV_HEADS, nqb, NSA_GROUP, Q_BLK, HEAD_DIM).transpose(0, 2, 4, 1, 3, 5)
    return o.reshape(bsz, t_, NSA_HEADS * HEAD_DIM)


GLA_SUB = 16
GLA_QK = GLA_HEADS * GLA_DK
GLA_V = GLA_HEADS * GLA_DV


def _dot_tn(a, b):
    return lax.dot_general(a, b, (((0,), (0,)), ((), ())), preferred_element_type=jnp.float32)


def _gla_body(q_ref, k_ref, v_ref, gr_ref, glr_ref, wg_ref, bg_ref, ng_ref, s0_ref, exp_ref, bd_ref,
              o_ref, sfin_ref, st_ref, b_ref, qd_ref, *, t_valid):
    f32, bf16 = jnp.float32, jnp.bfloat16
    tt = q_ref.shape[1]
    ti = pl.program_id(1)

    @pl.when(ti == 0)
    def _():
        st_ref[...] = s0_ref[0]

    row = lax.broadcasted_iota(jnp.int32, (tt, 1), 0)
    z = _dot(glr_ref[0][:, :GLA_RANK].astype(bf16), wg_ref[...]) + bg_ref[...]
    la = (jnp.minimum(z, 0.0) - jnp.log1p(jnp.exp(-jnp.abs(z)))) * (1.0 / GLA_TAU)
    la = jnp.where(ti * tt + row < t_valid, la, 0.0)
    seg = row & (GLA_SUB - 1)
    b = la
    for s in (1, 2, 4, 8):
        b = b + jnp.where(seg >= s, pltpu.roll(b, s, axis=0), 0.0)
    q = q_ref[0] * (GLA_DK ** -0.5)
    k = k_ref[0]
    v = v_ref[0]
    o = _dot((q * k).astype(bf16), exp_ref[...]) * v
    for d in range(1, GLA_SUB):
        decay = jnp.exp(jnp.minimum(b - pltpu.roll(b, d, axis=0), 0.0))
        w = jnp.where(seg >= d, q * pltpu.roll(k, d, axis=0) * decay, 0.0)
        o = o + _dot(w.astype(bf16), exp_ref[...]) * pltpu.roll(v, d, axis=0)
    o_ref[0] = o
    b_ref[...] = b
    qd_ref[...] = (q * jnp.exp(b)).astype(bf16)

    def block_step(c, carry):
        rows = pl.ds(pl.multiple_of(c * GLA_SUB, GLA_SUB), GLA_SUB)
        st = st_ref[...]
        o_ref[0, rows, :] += _dot_nt(qd_ref[rows, :], st.astype(bf16))
        bc = b_ref[rows, :]
        bl = bc[GLA_SUB - 1:GLA_SUB, :]
        kc = (k_ref[0, rows, :] * jnp.exp(bl - bc)).astype(bf16)
        upd = _dot_tn(v_ref[0, rows, :].astype(bf16), kc)
        st_ref[...] = st * jnp.exp(bl) + upd * bd_ref[...]
        return carry

    lax.fori_loop(0, tt // GLA_SUB, block_step, 0)
    sfin_ref[0] = st_ref[...]
    gr = gr_ref[0]
    gate = gr * jax.nn.sigmoid(gr)
    for h in range(GLA_HEADS):
        cols = slice(h * GLA_DV, (h + 1) * GLA_DV)
        oh = o_ref[0, :, cols]
        ms = jnp.mean(oh * oh, axis=-1, keepdims=True)
        o_ref[0, :, cols] = oh * lax.rsqrt(ms + LN_EPS) * ng_ref[...] * gate[:, cols]


def _gla(h, w_gla_gate, b_gla_gate, gla_norm_g, gla_state):
    bsz, t_, n_in = h.shape
    tp = -(-t_ // GLA_SUB) * GLA_SUB
    if tp != t_:
        h = jnp.pad(h, ((0, 0), (0, tp - t_), (0, 0)))
    tt = min(tp, 256)
    heads = np.arange(GLA_HEADS)
    expand = np.repeat(np.repeat(np.eye(GLA_HEADS, dtype=np.float32), GLA_DK, 0), GLA_DV, 1)
    bdmask = jnp.asarray(expand.T)
    if gla_state is None:
        s0 = jnp.zeros((bsz, GLA_V, GLA_QK), jnp.float32)
    else:
        s0 = jnp.zeros((bsz, GLA_HEADS, GLA_DV, GLA_HEADS, GLA_DK), jnp.float32)
        s0 = s0.at[:, heads, :, heads, :].set(gla_state.transpose(1, 0, 3, 2)).reshape(bsz, GLA_V, GLA_QK)
    tile = lambda width, blk: pl.BlockSpec((1, tt, width), lambda b, i: (b, i, blk))
    fixed2 = lambda shape: pl.BlockSpec(shape, lambda b, i: (0, 0))
    per_b = pl.BlockSpec((1, GLA_V, GLA_QK), lambda b, i: (b, 0, 0))
    o, s_t = pl.pallas_call(
        functools.partial(_gla_body, t_valid=t_),
        grid=(bsz, tp // tt),
        in_specs=[tile(GLA_QK, 0), tile(GLA_QK, 1), tile(GLA_V, 1), tile(GLA_V, 2),
                  tile(LANE, (2 * GLA_QK + 2 * GLA_V) // LANE),
                  fixed2((GLA_RANK, GLA_QK)), fixed2((1, GLA_QK)), fixed2((1, GLA_DV)), per_b,
                  fixed2((GLA_QK, GLA_V)), fixed2((GLA_V, GLA_QK))],
        out_specs=[pl.BlockSpec((1, tt, GLA_V), lambda b, i: (b, i, 0)), per_b],
        out_shape=[jax.ShapeDtypeStruct((bsz, tp, GLA_V), jnp.float32),
                   jax.ShapeDtypeStruct((bsz, GLA_V, GLA_QK), jnp.float32)],
        scratch_shapes=[pltpu.VMEM((GLA_V, GLA_QK), jnp.float32), pltpu.VMEM((tt, GLA_QK), jnp.float32),
                        pltpu.VMEM((tt, GLA_QK), jnp.bfloat16)],
        compiler_params=pltpu.CompilerParams(dimension_semantics=("arbitrary", "arbitrary"),
                                             vmem_limit_bytes=48 * 1024 * 1024),
        name="gla",
    )(h, h, h, h, h, w_gla_gate.astype(jnp.bfloat16), b_gla_gate.reshape(1, GLA_QK),
      gla_norm_g.reshape(1, GLA_DV), s0, jnp.asarray(expand, jnp.bfloat16), bdmask)
    s_new = s_t.reshape(bsz, GLA_HEADS, GLA_DV, GLA_HEADS, GLA_DK)[:, heads, :, heads, :]
    return o[:, :t_], s_new.transpose(1, 0, 3, 2)


def _ab_mixer(x, pos, w_in, w_gla_gate, b_gla_gate, gla_norm_g, w_cmp_pool, w_out,
              gla_state, nsa_cache, page_table, win_buf):
    bsz, t_, _ = x.shape
    h_in = _mm3(x, w_in)
    gq, gk, gv, gr, glr, nq, nkv, ngate = _split_cols(h_in, GLA_SIZES + NSA_SIZES)
    o_a, s_a = _gla(h_in, w_gla_gate, b_gla_gate, gla_norm_g, gla_state)
    q_raw = nq.reshape(bsz, t_, NSA_KV_HEADS, NSA_GROUP, HEAD_DIM)
    q_rot = _partial_rope(q_raw, pos)
    kv = nkv.reshape(bsz, t_, 6, NSA_KV_HEADS, HEAD_DIM)
    k_sel = _partial_rope(kv[:, :, 2], pos)
    k_win = _partial_rope(kv[:, :, 4], pos)
    rows_full = jnp.stack([kv[:, :, 0], kv[:, :, 1], k_sel, kv[:, :, 3]], axis=2)
    rows_win = jnp.stack([k_win, kv[:, :, 5]], axis=2)
    gates = jax.nn.sigmoid(ngate).reshape(bsz, t_, NSA_KV_HEADS, NSA_GROUP, 3)
    if nsa_cache is None:
        keys = rows_full
    else:
        past = nsa_cache[page_table].reshape(bsz, -1, 4, NSA_KV_HEADS, HEAD_DIM)
        keys = jnp.concatenate([past, rows_full], axis=1)
    length = keys.shape[1]
    kc, vc, cend = _compress(keys[:, :, 0], keys[:, :, 1], w_cmp_pool)
    n_sel = -(-length // SEL_BLK)
    ksb = _to_sel_blocks(keys[:, :, 2], n_sel)
    vsb = _to_sel_blocks(keys[:, :, 3], n_sel)
    if nsa_cache is None:
        o_b = _nsa_prompt(q_raw, q_rot, gates, kc, vc, rows_full, rows_win)
        new_win = rows_win[:, -min(WINDOW, t_):]
    else:
        w_buf = win_buf.shape[1]
        kw = jnp.concatenate([win_buf, rows_win], axis=1)
        kwpos = (length - t_) - w_buf + jnp.arange(kw.shape[1])
        o_b = _nsa_attend(q_raw, q_rot, pos, gates, kc, vc, cend, ksb, vsb, kw[:, :, 0], kw[:, :, 1], kwpos)
        o_b = o_b.reshape(bsz, t_, -1)
        new_win = kw[:, -w_buf:]
    y = _mm3(jnp.concatenate([o_a, o_b], axis=-1), w_out)
    return y, s_a, rows_full, new_win


def _conv_module(x, conv_buf, w_pw1, b_pw1, w_dw, b_dw, ln_g, ln_b, w_pw2, b_pw2):
    bsz = x.shape[0]
    a, g = jnp.split(_mm3(x, w_pw1) + b_pw1, 2, axis=-1)
    u = a * jax.nn.sigmoid(g)
    if conv_buf is None:
        conv_buf = jnp.zeros((bsz, CONV_W - 1, D_CONV), u.dtype)
    ext = jnp.concatenate([conv_buf, u], axis=1)
    c = lax.conv_general_dilated(ext, w_dw[:, None, :], (1,), 'VALID',
                                 dimension_numbers=('NWC', 'WIO', 'NWC'),
                                 feature_group_count=D_CONV) + b_dw
    c = jax.nn.silu(_layer_norm(c, ln_g, ln_b))
    return _mm3(c, w_pw2) + b_pw2, ext[:, -(CONV_W - 1):]


PACK_W = 256
SC_WINDOW = 128
SC_TILES = 32


def _pack_rows(y):
    out = []
    for h in range(2):
        lo = lax.bitcast_convert_type(y[:, 2 * h * PACK_W:(2 * h + 1) * PACK_W].astype(jnp.bfloat16)
                                      .astype(jnp.float32), jnp.uint32)
        hi = lax.bitcast_convert_type(y[:, (2 * h + 1) * PACK_W:(2 * h + 2) * PACK_W].astype(jnp.bfloat16)
                                      .astype(jnp.float32), jnp.uint32)
        out.append(lax.bitcast_convert_type((lo >> 16) | hi, jnp.int32))
    return out


def _unpack_words(w):
    u = lax.bitcast_convert_type(w, jnp.uint32)
    lo = lax.bitcast_convert_type(u << 16, jnp.float32)
    hi = lax.bitcast_convert_type(u & jnp.uint32(0xFFFF0000), jnp.float32)
    return lo, hi


def _gather_rows(src, idx):
    n = idx.shape[0]
    if n % (SC_WINDOW * SC_TILES) != 0:
        return jnp.take(src, idx, axis=0)
    mesh = plsc.VectorSubcoreMesh(core_axis_name="core", subcore_axis_name="subcore")

    @pl.kernel(out_type=jax.ShapeDtypeStruct((n, src.shape[1]), src.dtype), mesh=mesh)
    def gather_kernel(src_hbm, idx_hbm, out_hbm):
        def step(idx_vmem, out_vmem):
            pltpu.sync_copy(src_hbm.at[idx_vmem.at[0]], out_vmem)

        pltpu.emit_pipeline(
            step, grid=(n // SC_WINDOW,),
            in_specs=[pl.BlockSpec((1, SC_WINDOW), index_map=lambda i: (0, i))],
            out_specs=[pl.BlockSpec((SC_WINDOW, src.shape[1]), index_map=lambda i: (i, 0))],
            core_axis_name=("core", "subcore"),
            dimension_semantics=(pltpu.PARALLEL,),
        )(idx_hbm, out_hbm)

    return gather_kernel(src, idx.reshape(1, n))


PER_GROUP = N_EXPERTS // N_GROUPS
PICKED = -3e38


def _ln_rows(v, g, b):
    mu = jnp.mean(v, axis=-1, keepdims=True)
    c = v - mu
    var = jnp.mean(c * c, axis=-1, keepdims=True)
    return c * lax.rsqrt(var + LN_EPS) * g + b


def _first_max(v, ids, axes, sentinel):
    best = v
    for a in axes:
        best = jnp.max(best, axis=a, keepdims=True)
    first = jnp.where(v == best, ids, sentinel)
    for a in axes:
        first = jnp.min(first, axis=a, keepdims=True)
    return best, first


def _sum_axes(v, axes):
    for a in axes:
        v = jnp.sum(v, axis=a, keepdims=True)
    return v


def _moe_pre_body(x_ref, mix_ref, g_ref, b_ref, wr_ref, br_ref, wgu_ref, wdn_ref,
                  x1_ref, xp_ref, sh_ref, eidx_ref, gate_ref, rank_ref, cnt_ref, run_ref):
    f32, bf16 = jnp.float32, jnp.bfloat16
    tm = x_ref.shape[0]

    @pl.when(pl.program_id(0) == 0)
    def _():
        run_ref[...] = jnp.zeros(run_ref.shape, f32)

    x1 = _ln_rows(ALPHA * x_ref[...] + mix_ref[...], g_ref[...], b_ref[...])
    x1_ref[...] = x1
    x1b = x1.astype(bf16)
    xp_ref[0], xp_ref[1] = _pack_rows(x1)

    h = _dot(x1b, wgu_ref[...])
    d_sh = h.shape[1] // 2
    act = (jax.nn.silu(h[:, :d_sh]) * h[:, d_sh:]).astype(bf16)
    sh_ref[...] = _dot(act, wdn_ref[...])

    s = jax.nn.sigmoid(_dot_nt(wr_ref[...], x1b)).reshape(N_GROUPS, PER_GROUP, tm)
    sb = s + br_ref[...].reshape(N_GROUPS, PER_GROUP, 1)
    shape3 = (N_GROUPS, PER_GROUP, tm)
    pid = lax.broadcasted_iota(jnp.int32, shape3, 1)
    gid = lax.broadcasted_iota(jnp.int32, (N_GROUPS, 1, tm), 0)
    eid = lax.broadcasted_iota(jnp.int32, shape3, 0) * PER_GROUP + pid
    top1, i1 = _first_max(sb, pid, (1,), PER_GROUP)
    top2 = jnp.max(jnp.where(pid == i1, PICKED, sb), axis=1, keepdims=True)
    gscore = top1 + top2
    gsel = jnp.zeros((N_GROUPS, 1, tm), f32)
    for _ in range(TOPK_GROUPS):
        _, first = _first_max(gscore, gid, (0,), N_GROUPS)
        hit = gid == first
        gsel = jnp.where(hit, 1.0, gsel)
        gscore = jnp.where(hit, PICKED, gscore)
    cand = jnp.where(gsel > 0.0, sb, -1e30)
    firsts, gates = [], []
    picked = jnp.zeros(shape3, f32)
    for _ in range(TOP_K):
        _, first = _first_max(cand, eid, (0, 1), N_EXPERTS)
        hit = eid == first
        firsts.append(first)
        gates.append(_sum_axes(jnp.where(hit, s, 0.0), (0, 1)))
        picked = jnp.where(hit, 1.0, picked)
        cand = jnp.where(hit, PICKED, cand)
    gsum = gates[0]
    for gk in gates[1:]:
        gsum = gsum + gk
    earlier = (lax.broadcasted_iota(jnp.int32, (tm, tm), 0) < lax.broadcasted_iota(jnp.int32, (tm, tm), 1))
    picked2 = picked.reshape(N_EXPERTS, tm)
    rank = run_ref[...] + _dot(picked2.astype(bf16), jnp.where(earlier, 1.0, 0.0).astype(bf16))
    run_new = run_ref[...] + jnp.sum(picked2, axis=1, keepdims=True)
    run_ref[...] = run_new
    cnt_ref[...] = jnp.broadcast_to(run_new, cnt_ref.shape)
    rank3 = rank.reshape(shape3)
    for k in range(TOP_K):
        hit = eid == firsts[k]
        eidx_ref[k:k + 1, :] = firsts[k].reshape(1, tm)
        gate_ref[k:k + 1, :] = (gates[k] / gsum * ROUTE_SCALE).reshape(1, tm)
        rank_ref[k:k + 1, :] = _sum_axes(jnp.where(hit, rank3, 0.0), (0, 1)).reshape(1, tm).astype(jnp.int32)


def _moe_pre(x, mix, g, b, w_router, b_router, w_sh_gu, w_sh_down):
    m, d = x.shape
    bf16 = jnp.bfloat16
    tm = min(m, 512)
    row = lambda i: (i, 0)
    col = lambda i: (0, i)
    fixed = lambda i: (0, 0)
    d_sh2 = w_sh_gu.shape[1]
    return pl.pallas_call(
        _moe_pre_body,
        grid=(m // tm,),
        in_specs=[pl.BlockSpec((tm, d), row), pl.BlockSpec((tm, d), row),
                  pl.BlockSpec((1, d), fixed), pl.BlockSpec((1, d), fixed),
                  pl.BlockSpec((N_EXPERTS, d), fixed), pl.BlockSpec((N_EXPERTS, 1), fixed),
                  pl.BlockSpec((d, d_sh2), fixed), pl.BlockSpec((d_sh2 // 2, d), fixed)],
        out_specs=[pl.BlockSpec((tm, d), row), pl.BlockSpec((2, tm, PACK_W), lambda i: (0, i, 0)),
                   pl.BlockSpec((tm, d), row),
                   pl.BlockSpec((TOP_K, tm), col), pl.BlockSpec((TOP_K, tm), col), pl.BlockSpec((TOP_K, tm), col),
                   pl.BlockSpec((N_EXPERTS, LANE), fixed)],
        out_shape=[jax.ShapeDtypeStruct((m, d), jnp.float32), jax.ShapeDtypeStruct((2, m, PACK_W), jnp.int32),
                   jax.ShapeDtypeStruct((m, d), jnp.float32),
                   jax.ShapeDtypeStruct((TOP_K, m), jnp.int32), jax.ShapeDtypeStruct((TOP_K, m), jnp.float32),
                   jax.ShapeDtypeStruct((TOP_K, m), jnp.int32),
                   jax.ShapeDtypeStruct((N_EXPERTS, LANE), jnp.float32)],
        scratch_shapes=[pltpu.VMEM((N_EXPERTS, 1), jnp.float32)],
        compiler_params=pltpu.CompilerParams(dimension_semantics=("arbitrary",),
                                             vmem_limit_bytes=48 * 1024 * 1024),
        name="moe_pre",
    )(x, mix, g.reshape(1, d), b.reshape(1, d), w_router.T.astype(bf16), b_router.reshape(N_EXPERTS, 1),
      w_sh_gu.astype(bf16), w_sh_down.astype(bf16))


def _moe_expert_body(exp_ref, first_ref, active_ref, xs_ref, wgu_ref, wdn_ref, y_ref, wgu_bf, wdn_bf):
    i = pl.program_id(0)
    bf16 = jnp.bfloat16

    @pl.when(first_ref[i] == 1)
    def _():
        wgu_bf[...] = wgu_ref[0].astype(bf16)
        wdn_bf[...] = wdn_ref[0].astype(bf16)

    @pl.when(active_ref[i] == 1)
    def _():
        h = None
        for hw in range(2):
            for q, xq in enumerate(_unpack_words(xs_ref[hw])):
                r0 = (2 * hw + q) * PACK_W
                part = _dot(xq.astype(bf16), wgu_bf[r0:r0 + PACK_W, :])
                h = part if h is None else h + part
        d_e = h.shape[1] // 2
        act = (jax.nn.silu(h[:, :d_e]) * h[:, d_e:]).astype(bf16)
        y_ref[0], y_ref[1] = _pack_rows(_dot(act, wdn_bf[...]))

    @pl.when(active_ref[i] == 0)
    def _():
        y_ref[...] = jnp.zeros(y_ref.shape, y_ref.dtype)


def _moe_experts(xs, blk_exp, blk_first, blk_active, w_exp_gu, w_exp_down, bm):
    n_slots = xs.shape[1]
    d = w_exp_gu.shape[1]
    n_blk = n_slots // bm
    d_e2 = w_exp_gu.shape[2]
    words = lambda i, e, f, a: (0, i, 0)
    grid_spec = pltpu.PrefetchScalarGridSpec(
        num_scalar_prefetch=3,
        grid=(n_blk,),
        in_specs=[pl.BlockSpec((2, bm, PACK_W), words),
                  pl.BlockSpec((1, d, d_e2), lambda i, e, f, a: (e[i], 0, 0)),
                  pl.BlockSpec((1, d_e2 // 2, d), lambda i, e, f, a: (e[i], 0, 0))],
        out_specs=pl.BlockSpec((2, bm, PACK_W), words),
        scratch_shapes=[pltpu.VMEM((d, d_e2), jnp.bfloat16), pltpu.VMEM((d_e2 // 2, d), jnp.bfloat16)])
    return pl.pallas_call(
        _moe_expert_body,
        grid_spec=grid_spec,
        out_shape=jax.ShapeDtypeStruct((2, n_slots, PACK_W), jnp.int32),
        compiler_params=pltpu.CompilerParams(dimension_semantics=("arbitrary",),
                                             vmem_limit_bytes=48 * 1024 * 1024),
        name="moe_experts",
    )(blk_exp, blk_first, blk_active, xs, w_exp_gu, w_exp_down)


def _combine_ln_body(x_ref, yg_ref, gt_ref, sh_ref, g_ref, b_ref, o_ref):
    gt = gt_ref[...]
    parts = []
    for hw in range(2):
        lo_acc = hi_acc = None
        for k in range(TOP_K):
            lo, hi = _unpack_words(yg_ref[hw, k])
            gk = gt[:, k:k + 1]
            lo_acc = lo * gk if lo_acc is None else lo_acc + lo * gk
            hi_acc = hi * gk if hi_acc is None else hi_acc + hi * gk
        parts += [lo_acc, hi_acc]
    routed = jnp.concatenate(parts, axis=1)
    o_ref[...] = _ln_rows(ALPHA * x_ref[...] + (routed + sh_ref[...]), g_ref[...], b_ref[...])


def _combine_ln(x, yg, gate_t, shared, g, b):
    m, d = x.shape
    tm = min(m, 256)
    row = lambda i: (i, 0)
    fixed = lambda i: (0, 0)
    return pl.pallas_call(
        _combine_ln_body,
        grid=(m // tm,),
        in_specs=[pl.BlockSpec((tm, d), row), pl.BlockSpec((2, TOP_K, tm, PACK_W), lambda i: (0, 0, i, 0)),
                  pl.BlockSpec((tm, TOP_K), row), pl.BlockSpec((tm, d), row),
                  pl.BlockSpec((1, d), fixed), pl.BlockSpec((1, d), fixed)],
        out_specs=pl.BlockSpec((tm, d), row),
        out_shape=jax.ShapeDtypeStruct((m, d), jnp.float32),
        compiler_params=pltpu.CompilerParams(dimension_semantics=("arbitrary",)),
        name="combine_ln",
    )(x, yg, gate_t, shared, g.reshape(1, d), b.reshape(1, d))


def _moe_layer(x, mix, ln1_g, ln1_b, ln2_g, ln2_b, w_router, b_router, w_exp_gu, w_exp_down, w_sh_gu, w_sh_down):
    m, d = x.shape
    x1, xp, shared, eidx, gate8, rank8, counts = _moe_pre(x, mix, ln1_g, ln1_b, w_router, b_router,
                                                           w_sh_gu, w_sh_down)
    bm = 256 if m * TOP_K >= 256 * N_EXPERTS else MOE_BLK
    n_blk = (m * TOP_K) // bm + N_EXPERTS
    counts = counts[:, 0].astype(jnp.int32)
    padded = (counts + bm - 1) // bm * bm
    pad_end = jnp.cumsum(padded)
    dest = ((pad_end - padded)[eidx] + rank8).reshape(-1)
    tok = jnp.tile(jnp.arange(m, dtype=jnp.int32), TOP_K)
    slot_tok = jnp.zeros((n_blk * bm,), jnp.int32).at[dest].set(tok)
    blk_start = jnp.arange(n_blk, dtype=jnp.int32) * bm
    blk_exp = jnp.minimum(jnp.sum(pad_end[None, :] <= blk_start[:, None], axis=1), N_EXPERTS - 1).astype(jnp.int32)
    blk_active = (blk_start < pad_end[-1]).astype(jnp.int32)
    blk_first = jnp.concatenate([jnp.ones((1,), jnp.int32), (blk_exp[1:] != blk_exp[:-1]).astype(jnp.int32)])
    n_slots = n_blk * bm
    xs = _gather_rows(xp.reshape(2 * m, PACK_W), jnp.concatenate([slot_tok, slot_tok + m]))
    y = _moe_experts(xs.reshape(2, n_slots, PACK_W), blk_exp, blk_first, blk_active, w_exp_gu, w_exp_down, bm)
    yg = _gather_rows(y.reshape(2 * n_slots, PACK_W), jnp.concatenate([dest, dest + n_slots]))
    return _combine_ln(x1, yg.reshape(2, TOP_K, m, PACK_W), gate8.T, shared, ln2_g, ln2_b)


def _trunk(x, pos, gla_state, nsa_cache, page_table, win_buf, conv_buf,
           w_in_ab, w_gla_gate, b_gla_gate, gla_norm_g, w_cmp_pool, w_out_ab,
           w_pw1, b_pw1, w_dw, b_dw, conv_ln_g, conv_ln_b, w_pw2, b_pw2,
           ln_g, ln_b, w_router, b_router, w_exp_gu, w_exp_down, w_sh_gu, w_sh_down):
    new_gla, new_rows, new_win, new_conv = [], [], [], []
    for layer in range(DEPTH):
        i = layer // 2
        if layer % 2 == 0:
            mix, s_a, rows, win = _ab_mixer(
                x, pos, w_in_ab[i], w_gla_gate[i], b_gla_gate[i], gla_norm_g[i], w_cmp_pool[i], w_out_ab[i],
                None if gla_state is None else gla_state[i],
                None if nsa_cache is None else nsa_cache[i], page_table,
                None if win_buf is None else win_buf[i])
            new_gla.append(s_a)
            new_rows.append(rows)
            new_win.append(win)
        else:
            mix, cb = _conv_module(x, None if conv_buf is None else conv_buf[i], w_pw1[i], b_pw1[i],
                                   w_dw[i], b_dw[i], conv_ln_g[i], conv_ln_b[i], w_pw2[i], b_pw2[i])
            new_conv.append(cb)
        bsz, t_, d = x.shape
        x = _moe_layer(x.reshape(-1, d), mix.reshape(-1, d), ln_g[layer, 0], ln_b[layer, 0],
                       ln_g[layer, 1], ln_b[layer, 1], w_router[layer], b_router[layer],
                       w_exp_gu[layer], w_exp_down[layer], w_sh_gu[layer], w_sh_down[layer]).reshape(bsz, t_, d)
    return x, jnp.stack(new_gla), jnp.stack(new_rows), jnp.stack(new_win), jnp.stack(new_conv)


def kernel(x_prompt, x_sample, state_gla, cache_nsa_kv, state_nsa_win, state_conv, page_table,
           w_in_ab, w_gla_gate, b_gla_gate, gla_norm_g, w_cmp_pool, w_out_ab,
           w_pw1, b_pw1, w_dw, b_dw, conv_ln_g, conv_ln_b, w_pw2, b_pw2,
           ln_g, ln_b, w_router, b_router, w_exp_gu, w_exp_down, w_sh_gu, w_sh_down):
    weights = (w_in_ab, w_gla_gate, b_gla_gate, gla_norm_g, w_cmp_pool, w_out_ab,
               w_pw1, b_pw1, w_dw, b_dw, conv_ln_g, conv_ln_b, w_pw2, b_pw2,
               ln_g, ln_b, w_router, b_router, w_exp_gu, w_exp_down, w_sh_gu, w_sh_down)
    past_len = page_table.shape[1] * PAGE_SIZE
    pos_p = jnp.arange(x_prompt.shape[1])
    pos_s = past_len + jnp.arange(x_sample.shape[1])
    y_prompt, gla_p, rows_p, win_p, conv_p = _trunk(x_prompt, pos_p, None, None, None, None, None, *weights)
    y_sample, gla_s, rows_s, win_s, conv_s = _trunk(x_sample, pos_s, state_gla, cache_nsa_kv, page_table,
                                                    state_nsa_win, state_conv, *weights)
    return (y_prompt, y_sample, gla_p, gla_s, rows_p, rows_s, win_p, win_s, conv_p, conv_s)
```

```python
import functools
import math

import jax
import jax.numpy as jnp
import numpy as np
from jax import lax
from jax.experimental import pallas as pl
from jax.experimental.pallas import tpu as pltpu
from jax.experimental.pallas import tpu_sc as plsc

D_MODEL = 1024
DEPTH = 2
PAGE_SIZE = 128

GLA_HEADS = 4
GLA_DV = D_MODEL // 2 // GLA_HEADS
GLA_DK = GLA_DV // 2
GLA_RANK = 16
GLA_TAU = 16.0
GLA_CHUNK = 64

NSA_HEADS = 8
NSA_KV_HEADS = 2
NSA_GROUP = NSA_HEADS // NSA_KV_HEADS
HEAD_DIM = D_MODEL // 2 // NSA_HEADS
CMP_BLK = 32
CMP_STRIDE = 16
SEL_BLK = 64
SEL_TOPN = 16
WINDOW = 512
Q_BLK = 128
FORCE_BONUS = 100.0
ROPE_DIM = HEAD_DIM // 4
ROPE_THETA = 500000.0

GLA_SIZES = (GLA_HEADS * GLA_DK, GLA_HEADS * GLA_DK, GLA_HEADS * GLA_DV, GLA_HEADS * GLA_DV, GLA_RANK)
NSA_SIZES = (NSA_HEADS * HEAD_DIM, 6 * NSA_KV_HEADS * HEAD_DIM, 3 * NSA_HEADS)

CONV_W = 31
D_CONV = D_MODEL

N_EXPERTS = 64
N_GROUPS = 8
TOPK_GROUPS = 4
TOP_K = 8
D_EXPERT = 256
ROUTE_SCALE = 2.5
MOE_BLK = 128

ALPHA = (2 * DEPTH) ** 0.25
LN_EPS = 1e-5

LANE = 128


def _dot(a, b):
    return jnp.dot(a, b, preferred_element_type=jnp.float32)


def _dot_nt(a, b):
    return lax.dot_general(a, b, (((1,), (1,)), ((), ())), preferred_element_type=jnp.float32)


def _mm_body(x_ref, w_ref, o_ref):
    o_ref[...] = _dot(x_ref[...].astype(jnp.bfloat16), w_ref[...].astype(jnp.bfloat16))


def _mm(x, w, keep_pad=False):
    m, k = x.shape
    n = w.shape[1]
    n_pad = -(-n // LANE) * LANE
    w = w.astype(jnp.bfloat16)
    if n_pad != n:
        w = jnp.pad(w, ((0, 0), (0, n_pad - n)))
    tm = min(m, 512)
    out = pl.pallas_call(
        _mm_body,
        grid=(m // tm,),
        in_specs=[pl.BlockSpec((tm, k), lambda i: (i, 0)),
                  pl.BlockSpec((k, n_pad), lambda i: (0, 0))],
        out_specs=pl.BlockSpec((tm, n_pad), lambda i: (i, 0)),
        out_shape=jax.ShapeDtypeStruct((m, n_pad), jnp.float32),
        compiler_params=pltpu.CompilerParams(dimension_semantics=("arbitrary",),
                                             vmem_limit_bytes=48 * 1024 * 1024),
        name="mm",
    )(x, w)
    return out if keep_pad or n_pad == n else out[:, :n]


def _mm3(x, w):
    b, t, d = x.shape
    return _mm(x.reshape(b * t, d), w).reshape(b, t, -1)


def _split_cols(h, sizes):
    return jnp.split(h, np.cumsum(sizes)[:-1].tolist(), axis=-1)


def _layer_norm(x, g, b):
    mu = x.mean(-1, keepdims=True)
    var = jnp.square(x - mu).mean(-1, keepdims=True)
    return (x - mu) * lax.rsqrt(var + LN_EPS) * g + b


def _rms_norm(x, g):
    return x * lax.rsqrt(jnp.mean(x * x, -1, keepdims=True) + LN_EPS) * g


def _partial_rope(x, pos):
    half = ROPE_DIM // 2
    inv_freq = jnp.power(ROPE_THETA, -jnp.arange(half, dtype=jnp.float32) / half)
    ang = pos.astype(jnp.float32)[:, None] * inv_freq
    ang = ang.reshape(ang.shape[0], *([1] * (x.ndim - 3)), half)
    cos, sin = jnp.cos(ang), jnp.sin(ang)
    x1 = x[..., :half]
    x2 = x[..., half:ROPE_DIM]
    rot = jnp.concatenate([x1 * cos - x2 * sin, x2 * cos + x1 * sin], -1)
    return jnp.concatenate([rot, x[..., ROPE_DIM:]], -1)


def _masked_softmax(s, mask):
    s = jnp.where(mask, s, -jnp.inf)
    m = jnp.max(s, axis=-1, keepdims=True)
    m = jnp.where(jnp.isfinite(m), m, 0.0)
    p = jnp.exp(s - m)
    return p / jnp.maximum(p.sum(-1, keepdims=True), 1e-30)


def _gla_recurrence(q, k, v, log_a, s0):
    bsz, t_, nh, _ = q.shape
    c = math.gcd(t_, GLA_CHUNK)
    n = t_ // c

    def chunks(a):
        return jnp.moveaxis(a.reshape(bsz, n, c, *a.shape[2:]), 1, 0)

    causal = jnp.tril(jnp.ones((c, c), dtype=bool))[None, :, :, None, None]

    def step(s, inp):
        qc, kc, vc, lc = inp
        bc = jnp.cumsum(lc, axis=1)
        decay = jnp.exp(jnp.where(causal, bc[:, :, None] - bc[:, None, :], -jnp.inf))
        attn = jnp.einsum('bijhd,bjhd->bhij', qc[:, :, None] * decay, kc)
        o = jnp.einsum('bhij,bjhe->bihe', attn, vc) + jnp.einsum('bihd,bhde->bihe', qc * jnp.exp(bc), s)
        bl = bc[:, -1]
        s = jnp.exp(bl)[..., None] * s + jnp.einsum('bjhd,bjhe->bhde', kc * jnp.exp(bl[:, None] - bc), vc)
        return s, o

    s_fin, o = lax.scan(step, s0, (chunks(q), chunks(k), chunks(v), chunks(log_a)))
    return jnp.moveaxis(o, 0, 1).reshape(bsz, t_, nh, -1), s_fin


def _compress(k, v, w_pool):
    bsz, length = k.shape[:2]
    n_sub = length // CMP_STRIDE

    def pool(a, w):
        sub = a[:, :n_sub * CMP_STRIDE].reshape(bsz, n_sub, CMP_STRIDE, *a.shape[2:])
        first = jnp.einsum('bnjhd,j->bnhd', sub, w[:CMP_STRIDE])
        second = jnp.einsum('bnjhd,j->bnhd', sub, w[CMP_STRIDE:])
        return first[:, :-1] + second[:, 1:]

    cend = jnp.arange(n_sub - 1) * CMP_STRIDE + CMP_BLK - 1
    return pool(k, w_pool[0]), pool(v, w_pool[1]), cend


def _to_sel_blocks(a, n_sel):
    bsz, length = a.shape[:2]
    a = jnp.pad(a, ((0, 0), (0, n_sel * SEL_BLK - length), (0, 0), (0, 0)))
    return a.reshape(bsz, n_sel, SEL_BLK, NSA_KV_HEADS, HEAD_DIM).transpose(0, 3, 1, 2, 4)


def _nsa_attend(q_raw, q_rot, qpos, gates, kc, vc, cend, ksb, vsb, kw, vw, kwpos):
    scale = HEAD_DIM ** -0.5
    bsz, tq = q_raw.shape[:2]
    n_cmp, n_sel = kc.shape[1], ksb.shape[2]
    s_c = jnp.einsum('bqhgd,bnhd->bhgqn', q_raw, kc) * scale
    p_c = _masked_softmax(s_c, cend[None, :] <= qpos[:, None])
    o_c = jnp.einsum('bhgqn,bnhd->bqhgd', p_c, vc)
    ratio = SEL_BLK // CMP_STRIDE
    imp = p_c.sum(axis=2)
    imp = jnp.pad(imp, ((0, 0), (0, 0), (0, 0), (1, ratio * (n_sel + 1) - 1 - n_cmp)))
    imp = imp.reshape(bsz, NSA_KV_HEADS, tq, n_sel + 1, ratio)
    imp_s = imp[..., :n_sel, :].sum(-1) + imp[..., 1:, 0]
    blk = jnp.arange(n_sel)[None, :]
    cur = (qpos // SEL_BLK)[:, None]
    valid = blk * SEL_BLK <= qpos[:, None]
    forced = (blk == 0) | (blk == cur) | (blk == cur - 1)
    score = jnp.where(valid, imp_s + jnp.where(forced, FORCE_BONUS, 0.0), -jnp.inf)
    k_top = min(SEL_TOPN, n_sel)
    _, sel = lax.top_k(score, k_top)
    take = jax.vmap(jax.vmap(lambda blocks, idx: blocks[idx]))
    ks = take(ksb, sel).reshape(bsz, NSA_KV_HEADS, tq, k_top * SEL_BLK, HEAD_DIM)
    vs = take(vsb, sel).reshape(bsz, NSA_KV_HEADS, tq, k_top * SEL_BLK, HEAD_DIM)
    kpos = (sel[..., None] * SEL_BLK + jnp.arange(SEL_BLK)).reshape(bsz, NSA_KV_HEADS, tq, k_top * SEL_BLK)
    s_s = jnp.einsum('bqhgd,bhqkd->bhgqk', q_rot, ks) * scale
    p_s = _masked_softmax(s_s, (kpos <= qpos[:, None])[:, :, None])
    o_s = jnp.einsum('bhgqk,bhqkd->bqhgd', p_s, vs)
    s_w = jnp.einsum('bqhgd,bkhd->bhgqk', q_rot, kw) * scale
    kp, qp = kwpos[None, :], qpos[:, None]
    p_w = _masked_softmax(s_w, (kp <= qp) & (kp > qp - WINDOW) & (kp >= 0))
    o_w = jnp.einsum('bhgqk,bkhd->bqhgd', p_w, vw)
    return gates[..., 0:1] * o_c + gates[..., 1:2] * o_s + gates[..., 2:3] * o_w


NSA_ROWS = NSA_GROUP * Q_BLK
SEL_KT = 512
N_SELB = 128
MASKED = -1e9
WIN_KEYS = WINDOW + Q_BLK
KK_W = 2 * HEAD_DIM + N_SELB


def _nsa_prompt_body(qr_ref, qo_ref, kc_ref, vct_ref, kk_ref, vvt_ref, g_ref, o_ref,
                     imp_ref, m_ref, l_ref, acc_ref):
    f32, bf16 = jnp.float32, jnp.bfloat16
    qb = pl.program_id(2)
    q0 = qb * Q_BLK
    qr_t = qr_ref[0, 0, 0]
    qo_t = qo_ref[0, 0, 0]
    n_cmp = kc_ref.shape[2]

    s_c = _dot(kc_ref[0, 0], qr_t)
    n_idx = lax.broadcasted_iota(jnp.int32, (n_cmp, NSA_ROWS), 0)
    qpos_c = q0 + (lax.broadcasted_iota(jnp.int32, (n_cmp, NSA_ROWS), 1) & (Q_BLK - 1))
    cmask = (n_idx * CMP_STRIDE + (CMP_BLK - 1)) <= qpos_c
    s_c = jnp.where(cmask, s_c, MASKED)
    m_c = jnp.max(s_c, axis=0, keepdims=True)
    p_c = jnp.where(cmask, jnp.exp(s_c - m_c), 0.0)
    p_c = p_c / jnp.maximum(jnp.sum(p_c, axis=0, keepdims=True), 1e-30)
    o_ct = _dot(vct_ref[0, 0], p_c.astype(bf16))

    imp = (p_c[:, 0:Q_BLK] + p_c[:, Q_BLK:2 * Q_BLK]) + p_c[:, 2 * Q_BLK:3 * Q_BLK] + p_c[:, 3 * Q_BLK:]
    imp_ref[0:8, :] = jnp.zeros((8, Q_BLK), f32)
    imp_ref[8:8 + n_cmp, :] = imp
    ratio = SEL_BLK // CMP_STRIDE
    n_selb = n_cmp // ratio
    imp_s = imp_ref[pl.ds(7, n_selb, stride=ratio), :]
    for r in range(ratio):
        imp_s = imp_s + imp_ref[pl.ds(8 + r, n_selb, stride=ratio), :]
    blk = lax.broadcasted_iota(jnp.int32, (n_selb, Q_BLK), 0)
    qpos_s = q0 + lax.broadcasted_iota(jnp.int32, (n_selb, Q_BLK), 1)
    cur = lax.shift_right_logical(qpos_s, int(math.log2(SEL_BLK)))
    valid = blk * SEL_BLK <= qpos_s
    forced = (blk == 0) | (blk == cur) | (blk == cur - 1)
    score = jnp.where(valid, imp_s + jnp.where(forced, FORCE_BONUS, 0.0), -1e30)
    picked = jnp.zeros((n_selb, Q_BLK), f32)
    for _ in range(SEL_TOPN):
        best = jnp.max(score, axis=0, keepdims=True)
        first = jnp.min(jnp.where(score == best, blk, n_selb), axis=0, keepdims=True)
        hit = blk == first
        picked = jnp.where(hit, 1.0, picked)
        score = jnp.where(hit, -3e38, score)
    selb_t = jnp.where(valid, picked, 0.0)
    if n_selb < N_SELB:
        selb_t = jnp.concatenate([selb_t, jnp.zeros((N_SELB - n_selb, Q_BLK), f32)], axis=0)
    selb_t = ((selb_t - 1.0) * (-MASKED)).astype(bf16)
    selb_t = jnp.concatenate([selb_t] * NSA_GROUP, axis=1)

    zeros_q = jnp.zeros((HEAD_DIM, NSA_ROWS), bf16)
    q_sel = jnp.concatenate([qo_t, zeros_q, selb_t], axis=0)
    q_win = jnp.concatenate([zeros_q, qo_t, jnp.zeros((N_SELB, NSA_ROWS), bf16)], axis=0)
    qpos_r = q0 + (lax.broadcasted_iota(jnp.int32, (1, NSA_ROWS), 1) & (Q_BLK - 1))

    def v_tiles(first, count):
        return jnp.concatenate([vvt_ref[0, 0, first + j] for j in range(count)], axis=1)

    m_ref[...] = jnp.full(m_ref.shape, MASKED, f32)
    l_ref[...] = jnp.zeros(l_ref.shape, f32)
    acc_ref[...] = jnp.zeros(acc_ref.shape, f32)

    def sel_tile(k0, kt, causal):
        s = _dot(kk_ref[0, 0, pl.ds(k0, kt), :], q_sel)
        if causal:
            kpos = k0 + lax.broadcasted_iota(jnp.int32, (kt, NSA_ROWS), 0)
            s = jnp.where(kpos <= qpos_r, s, MASKED)
        m_old = m_ref[...]
        m_new = jnp.maximum(m_old, jnp.max(s, axis=0, keepdims=True))
        alpha = jnp.exp(m_old - m_new)
        p = jnp.exp(s - m_new)
        l_ref[...] = alpha * l_ref[...] + jnp.sum(p, axis=0, keepdims=True)
        vt = v_tiles(k0 // Q_BLK, kt // Q_BLK)
        acc_ref[...] = alpha * acc_ref[...] + _dot(vt, p.astype(bf16))
        m_ref[...] = m_new

    n_full = q0 // SEL_KT

    def full_step(t, c):
        sel_tile(pl.multiple_of(t * SEL_KT, SEL_KT), SEL_KT, False)
        return c

    lax.fori_loop(0, n_full, full_step, 0)
    d0 = n_full * SEL_KT

    def diag_step(t, c):
        sel_tile(pl.multiple_of(d0 + t * Q_BLK, Q_BLK), Q_BLK, False)
        return c

    lax.fori_loop(0, (q0 - d0) // Q_BLK, diag_step, 0)
    sel_tile(pl.multiple_of(q0, Q_BLK), Q_BLK, True)
    o_st = acc_ref[0:HEAD_DIM, :] / l_ref[...]

    w0 = pl.multiple_of(jnp.maximum(q0 - WINDOW, 0), Q_BLK)
    s_w = _dot(kk_ref[0, 0, pl.ds(w0, WIN_KEYS), :], q_win)
    kpos_w = w0 + lax.broadcasted_iota(jnp.int32, (WIN_KEYS, NSA_ROWS), 0)
    s_w = jnp.where((kpos_w <= qpos_r) & (kpos_w > qpos_r - WINDOW), s_w, MASKED)
    p_w = jnp.exp(s_w - jnp.max(s_w, axis=0, keepdims=True))
    l_w = jnp.sum(p_w, axis=0, keepdims=True)
    acc_w = _dot(v_tiles(w0 // Q_BLK, WIN_KEYS // Q_BLK), p_w.astype(bf16))
    o_wt = acc_w[HEAD_DIM:2 * HEAD_DIM, :] / l_w

    g = g_ref[0, 0, 0]
    o_ref[0, 0, 0] = g[0:1, :] * o_ct + g[1:2, :] * o_st + g[2:3, :] * o_wt


def _nsa_prompt(q_raw, q_rot, gates, kc, vc, rows_full, rows_win):
    bsz, t_ = q_raw.shape[:2]
    bf16 = jnp.bfloat16
    nqb = t_ // Q_BLK
    scale = HEAD_DIM ** -0.5
    n_cmp = kc.shape[1] + 1

    def q_cols(a):
        a = a.reshape(bsz, nqb, Q_BLK, NSA_KV_HEADS, NSA_GROUP, a.shape[-1])
        return a.transpose(0, 3, 1, 5, 4, 2).reshape(bsz, NSA_KV_HEADS, nqb, a.shape[-1], NSA_ROWS)

    qr = q_cols((q_raw * scale).astype(bf16))
    qo = q_cols((q_rot * scale).astype(bf16))
    gt = q_cols(gates)
    kc_p = jnp.pad(kc, ((0, 0), (0, 1), (0, 0), (0, 0))).transpose(0, 2, 1, 3).astype(bf16)
    vct = jnp.pad(vc, ((0, 0), (0, 1), (0, 0), (0, 0))).transpose(0, 2, 3, 1).astype(bf16)
    onehot = (jnp.arange(t_)[:, None] // SEL_BLK == jnp.arange(N_SELB)[None, :]).astype(bf16)
    onehot = jnp.broadcast_to(onehot, (bsz, NSA_KV_HEADS, t_, N_SELB))
    kk = jnp.concatenate([rows_full[:, :, 2].transpose(0, 2, 1, 3).astype(bf16),
                          rows_win[:, :, 0].transpose(0, 2, 1, 3).astype(bf16), onehot], axis=-1)
    vv = jnp.concatenate([rows_full[:, :, 3], rows_win[:, :, 1]], axis=-1).astype(bf16)
    vvt = vv.reshape(bsz, nqb, Q_BLK, NSA_KV_HEADS, 2 * HEAD_DIM).transpose(0, 3, 1, 4, 2)
    grid = (bsz, NSA_KV_HEADS, nqb)
    per_blk = lambda b, h, i: (b, h, i, 0, 0)
    per_head = lambda b, h, i: (b, h, 0, 0)
    o = pl.pallas_call(
        _nsa_prompt_body,
        grid=grid,
        in_specs=[pl.BlockSpec((1, 1, 1, HEAD_DIM, NSA_ROWS), per_blk),
                  pl.BlockSpec((1, 1, 1, HEAD_DIM, NSA_ROWS), per_blk),
                  pl.BlockSpec((1, 1, n_cmp, HEAD_DIM), per_head),
                  pl.BlockSpec((1, 1, HEAD_DIM, n_cmp), per_head),
                  pl.BlockSpec((1, 1, t_, KK_W), per_head),
                  pl.BlockSpec((1, 1, nqb, 2 * HEAD_DIM, Q_BLK), lambda b, h, i: (b, h, 0, 0, 0)),
                  pl.BlockSpec((1, 1, 1, 3, NSA_ROWS), per_blk)],
        out_specs=pl.BlockSpec((1, 1, 1, HEAD_DIM, NSA_ROWS), per_blk),
        out_shape=jax.ShapeDtypeStruct((bsz, NSA_KV_HEADS, nqb, HEAD_DIM, NSA_ROWS), jnp.float32),
        scratch_shapes=[pltpu.VMEM((8 + n_cmp, Q_BLK), jnp.float32),
                        pltpu.VMEM((1, NSA_ROWS), jnp.float32),
                        pltpu.VMEM((1, NSA_ROWS), jnp.float32),
                        pltpu.VMEM((2 * HEAD_DIM, NSA_ROWS), jnp.float32)],
        compiler_params=pltpu.CompilerParams(
            dimension_semantics=("arbitrary", "arbitrary", "arbitrary"),
            vmem_limit_bytes=48 * 1024 * 1024),
        name="nsa_prompt",
    )(qr, qo, kc_p, vct, kk, vvt, gt)
    o = o.reshape(bsz, NSA_KV_HEADS, nqb, HEAD_DIM, NSA_GROUP, Q_BLK).transpose(0, 2, 5, 1, 4, 3)
    return o.reshape(bsz, t_, NSA_HEADS * HEAD_DIM)


GLA_SUB = 16
GLA_QK = GLA_HEADS * GLA_DK
GLA_V = GLA_HEADS * GLA_DV


def _dot_tn(a, b):
    return lax.dot_general(a, b, (((0,), (0,)), ((), ())), preferred_element_type=jnp.float32)


def _gla_body(q_ref, k_ref, v_ref, gr_ref, glr_ref, wg_ref, bg_ref, ng_ref, s0_ref, exp_ref, bd_ref,
              o_ref, sfin_ref, st_ref, b_ref, qd_ref, *, t_valid):
    f32, bf16 = jnp.float32, jnp.bfloat16
    tt = q_ref.shape[1]
    ti = pl.program_id(1)

    @pl.when(ti == 0)
    def _():
        st_ref[...] = s0_ref[0]

    row = lax.broadcasted_iota(jnp.int32, (tt, 1), 0)
    z = _dot(glr_ref[0][:, :GLA_RANK].astype(bf16), wg_ref[...]) + bg_ref[...]
    la = (jnp.minimum(z, 0.0) - jnp.log1p(jnp.exp(-jnp.abs(z)))) * (1.0 / GLA_TAU)
    la = jnp.where(ti * tt + row < t_valid, la, 0.0)
    seg = row & (GLA_SUB - 1)
    b = la
    for s in (1, 2, 4, 8):
        b = b + jnp.where(seg >= s, pltpu.roll(b, s, axis=0), 0.0)
    q = q_ref[0] * (GLA_DK ** -0.5)
    k = k_ref[0]
    v = v_ref[0]
    o = _dot((q * k).astype(bf16), exp_ref[...]) * v
    for d in range(1, GLA_SUB):
        decay = jnp.exp(jnp.minimum(b - pltpu.roll(b, d, axis=0), 0.0))
        w = jnp.where(seg >= d, q * pltpu.roll(k, d, axis=0) * decay, 0.0)
        o = o + _dot(w.astype(bf16), exp_ref[...]) * pltpu.roll(v, d, axis=0)
    o_ref[0] = o
    b_ref[...] = b
    qd_ref[...] = (q * jnp.exp(b)).astype(bf16)

    def block_step(c, carry):
        rows = pl.ds(pl.multiple_of(c * GLA_SUB, GLA_SUB), GLA_SUB)
        st = st_ref[...]
        o_ref[0, rows, :] += _dot_nt(qd_ref[rows, :], st.astype(bf16))
        bc = b_ref[rows, :]
        bl = bc[GLA_SUB - 1:GLA_SUB, :]
        kc = (k_ref[0, rows, :] * jnp.exp(bl - bc)).astype(bf16)
        upd = _dot_tn(v_ref[0, rows, :].astype(bf16), kc)
        st_ref[...] = st * jnp.exp(bl) + upd * bd_ref[...]
        return carry

    lax.fori_loop(0, tt // GLA_SUB, block_step, 0)
    sfin_ref[0] = st_ref[...]
    gr = gr_ref[0]
    gate = gr * jax.nn.sigmoid(gr)
    for h in range(GLA_HEADS):
        cols = slice(h * GLA_DV, (h + 1) * GLA_DV)
        oh = o_ref[0, :, cols]
        ms = jnp.mean(oh * oh, axis=-1, keepdims=True)
        o_ref[0, :, cols] = oh * lax.rsqrt(ms + LN_EPS) * ng_ref[...] * gate[:, cols]


def _gla(h, w_gla_gate, b_gla_gate, gla_norm_g, gla_state):
    bsz, t_, n_in = h.shape
    tp = -(-t_ // GLA_SUB) * GLA_SUB
    if tp != t_:
        h = jnp.pad(h, ((0, 0), (0, tp - t_), (0, 0)))
    tt = min(tp, 256)
    heads = np.arange(GLA_HEADS)
    expand = np.repeat(np.repeat(np.eye(GLA_HEADS, dtype=np.float32), GLA_DK, 0), GLA_DV, 1)
    bdmask = jnp.asarray(expand.T)
    if gla_state is None:
        s0 = jnp.zeros((bsz, GLA_V, GLA_QK), jnp.float32)
    else:
        s0 = jnp.zeros((bsz, GLA_HEADS, GLA_DV, GLA_HEADS, GLA_DK), jnp.float32)
        s0 = s0.at[:, heads, :, heads, :].set(gla_state.transpose(1, 0, 3, 2)).reshape(bsz, GLA_V, GLA_QK)
    tile = lambda width, blk: pl.BlockSpec((1, tt, width), lambda b, i: (b, i, blk))
    fixed2 = lambda shape: pl.BlockSpec(shape, lambda b, i: (0, 0))
    per_b = pl.BlockSpec((1, GLA_V, GLA_QK), lambda b, i: (b, 0, 0))
    o, s_t = pl.pallas_call(
        functools.partial(_gla_body, t_valid=t_),
        grid=(bsz, tp // tt),
        in_specs=[tile(GLA_QK, 0), tile(GLA_QK, 1), tile(GLA_V, 1), tile(GLA_V, 2),
                  tile(LANE, (2 * GLA_QK + 2 * GLA_V) // LANE),
                  fixed2((GLA_RANK, GLA_QK)), fixed2((1, GLA_QK)), fixed2((1, GLA_DV)), per_b,
                  fixed2((GLA_QK, GLA_V)), fixed2((GLA_V, GLA_QK))],
        out_specs=[pl.BlockSpec((1, tt, GLA_V), lambda b, i: (b, i, 0)), per_b],
        out_shape=[jax.ShapeDtypeStruct((bsz, tp, GLA_V), jnp.float32),
                   jax.ShapeDtypeStruct((bsz, GLA_V, GLA_QK), jnp.float32)],
        scratch_shapes=[pltpu.VMEM((GLA_V, GLA_QK), jnp.float32), pltpu.VMEM((tt, GLA_QK), jnp.float32),
                        pltpu.VMEM((tt, GLA_QK), jnp.bfloat16)],
        compiler_params=pltpu.CompilerParams(dimension_semantics=("arbitrary", "arbitrary"),
                                             vmem_limit_bytes=48 * 1024 * 1024),
        name="gla",
    )(h, h, h, h, h, w_gla_gate.astype(jnp.bfloat16), b_gla_gate.reshape(1, GLA_QK),
      gla_norm_g.reshape(1, GLA_DV), s0, jnp.asarray(expand, jnp.bfloat16), bdmask)
    s_new = s_t.reshape(bsz, GLA_HEADS, GLA_DV, GLA_HEADS, GLA_DK)[:, heads, :, heads, :]
    return o[:, :t_], s_new.transpose(1, 0, 3, 2)


def _ab_mixer(x, pos, w_in, w_gla_gate, b_gla_gate, gla_norm_g, w_cmp_pool, w_out,
              gla_state, nsa_cache, page_table, win_buf):
    bsz, t_, _ = x.shape
    h_in = _mm(x.reshape(bsz * t_, -1), w_in, keep_pad=True).reshape(bsz, t_, -1)
    n_gla = sum(GLA_SIZES)
    nq, nkv, ngate = _split_cols(h_in[..., n_gla:n_gla + sum(NSA_SIZES)], NSA_SIZES)
    o_a, s_a = _gla(h_in, w_gla_gate, b_gla_gate, gla_norm_g, gla_state)
    q_raw = nq.reshape(bsz, t_, NSA_KV_HEADS, NSA_GROUP, HEAD_DIM)
    q_rot = _partial_rope(q_raw, pos)
    kv = nkv.reshape(bsz, t_, 6, NSA_KV_HEADS, HEAD_DIM)
    k_sel = _partial_rope(kv[:, :, 2], pos)
    k_win = _partial_rope(kv[:, :, 4], pos)
    rows_full = jnp.stack([kv[:, :, 0], kv[:, :, 1], k_sel, kv[:, :, 3]], axis=2)
    rows_win = jnp.stack([k_win, kv[:, :, 5]], axis=2)
    gates = jax.nn.sigmoid(ngate).reshape(bsz, t_, NSA_KV_HEADS, NSA_GROUP, 3)
    if nsa_cache is None:
        keys = rows_full
    else:
        past = nsa_cache[page_table].reshape(bsz, -1, 4, NSA_KV_HEADS, HEAD_DIM)
        keys = jnp.concatenate([past, rows_full], axis=1)
    length = keys.shape[1]
    kc, vc, cend = _compress(keys[:, :, 0], keys[:, :, 1], w_cmp_pool)
    n_sel = -(-length // SEL_BLK)
    ksb = _to_sel_blocks(keys[:, :, 2], n_sel)
    vsb = _to_sel_blocks(keys[:, :, 3], n_sel)
    if nsa_cache is None:
        o_b = _nsa_prompt(q_raw, q_rot, gates, kc, vc, rows_full, rows_win)
        new_win = rows_win[:, -min(WINDOW, t_):]
    else:
        w_buf = win_buf.shape[1]
        kw = jnp.concatenate([win_buf, rows_win], axis=1)
        kwpos = (length - t_) - w_buf + jnp.arange(kw.shape[1])
        o_b = _nsa_attend(q_raw, q_rot, pos, gates, kc, vc, cend, ksb, vsb, kw[:, :, 0], kw[:, :, 1], kwpos)
        o_b = o_b.reshape(bsz, t_, -1)
        new_win = kw[:, -w_buf:]
    y = _mm3(jnp.concatenate([o_a, o_b], axis=-1), w_out)
    return y, s_a, rows_full, new_win


def _conv_module(x, conv_buf, w_pw1, b_pw1, w_dw, b_dw, ln_g, ln_b, w_pw2, b_pw2):
    bsz = x.shape[0]
    a, g = jnp.split(_mm3(x, w_pw1) + b_pw1, 2, axis=-1)
    u = a * jax.nn.sigmoid(g)
    if conv_buf is None:
        conv_buf = jnp.zeros((bsz, CONV_W - 1, D_CONV), u.dtype)
    ext = jnp.concatenate([conv_buf, u], axis=1)
    c = lax.conv_general_dilated(ext, w_dw[:, None, :], (1,), 'VALID',
                                 dimension_numbers=('NWC', 'WIO', 'NWC'),
                                 feature_group_count=D_CONV) + b_dw
    c = jax.nn.silu(_layer_norm(c, ln_g, ln_b))
    return _mm3(c, w_pw2) + b_pw2, ext[:, -(CONV_W - 1):]


PACK_W = 256
SC_WINDOW = 128
SC_TILES = 32


def _pack_rows(y):
    out = []
    for h in range(2):
        lo = lax.bitcast_convert_type(y[:, 2 * h * PACK_W:(2 * h + 1) * PACK_W].astype(jnp.bfloat16)
                                      .astype(jnp.float32), jnp.uint32)
        hi = lax.bitcast_convert_type(y[:, (2 * h + 1) * PACK_W:(2 * h + 2) * PACK_W].astype(jnp.bfloat16)
                                      .astype(jnp.float32), jnp.uint32)
        out.append(lax.bitcast_convert_type((lo >> 16) | hi, jnp.int32))
    return out


def _unpack_words(w):
    u = lax.bitcast_convert_type(w, jnp.uint32)
    lo = lax.bitcast_convert_type(u << 16, jnp.float32)
    hi = lax.bitcast_convert_type(u & jnp.uint32(0xFFFF0000), jnp.float32)
    return lo, hi


def _gather_rows(src, idx):
    n = idx.shape[0]
    if n % (SC_WINDOW * SC_TILES) != 0:
        return jnp.take(src, idx, axis=0)
    mesh = plsc.VectorSubcoreMesh(core_axis_name="core", subcore_axis_name="subcore")

    @pl.kernel(out_type=jax.ShapeDtypeStruct((n, src.shape[1]), src.dtype), mesh=mesh)
    def gather_kernel(src_hbm, idx_hbm, out_hbm):
        def step(idx_vmem, out_vmem):
            pltpu.sync_copy(src_hbm.at[idx_vmem.at[0]], out_vmem)

        pltpu.emit_pipeline(
            step, grid=(n // SC_WINDOW,),
            in_specs=[pl.BlockSpec((1, SC_WINDOW), index_map=lambda i: (0, i))],
            out_specs=[pl.BlockSpec((SC_WINDOW, src.shape[1]), index_map=lambda i: (i, 0))],
            core_axis_name=("core", "subcore"),
            dimension_semantics=(pltpu.PARALLEL,),
        )(idx_hbm, out_hbm)

    return gather_kernel(src, idx.reshape(1, n))


PER_GROUP = N_EXPERTS // N_GROUPS
PICKED = -3e38


def _ln_rows(v, g, b):
    mu = jnp.mean(v, axis=-1, keepdims=True)
    c = v - mu
    var = jnp.mean(c * c, axis=-1, keepdims=True)
    return c * lax.rsqrt(var + LN_EPS) * g + b


def _first_max(v, ids, axes, sentinel):
    best = v
    for a in axes:
        best = jnp.max(best, axis=a, keepdims=True)
    first = jnp.where(v == best, ids, sentinel)
    for a in axes:
        first = jnp.min(first, axis=a, keepdims=True)
    return best, first


def _sum_axes(v, axes):
    for a in axes:
        v = jnp.sum(v, axis=a, keepdims=True)
    return v


def _moe_pre_body(x_ref, mix_ref, g_ref, b_ref, wr_ref, br_ref, wgu_ref, wdn_ref,
                  x1_ref, xp_ref, sh_ref, eidx_ref, gate_ref, rank_ref, cnt_ref, run_ref):
    f32, bf16 = jnp.float32, jnp.bfloat16
    tm = x_ref.shape[0]

    @pl.when(pl.program_id(0) == 0)
    def _():
        run_ref[...] = jnp.zeros(run_ref.shape, f32)

    x1 = _ln_rows(ALPHA * x_ref[...] + mix_ref[...], g_ref[...], b_ref[...])
    x1_ref[...] = x1
    x1b = x1.astype(bf16)
    xp_ref[0], xp_ref[1] = _pack_rows(x1)

    h = _dot(x1b, wgu_ref[...])
    d_sh = h.shape[1] // 2
    act = (jax.nn.silu(h[:, :d_sh]) * h[:, d_sh:]).astype(bf16)
    sh_ref[...] = _dot(act, wdn_ref[...])

    s = jax.nn.sigmoid(_dot_nt(wr_ref[...], x1b)).reshape(N_GROUPS, PER_GROUP, tm)
    sb = s + br_ref[...].reshape(N_GROUPS, PER_GROUP, 1)
    shape3 = (N_GROUPS, PER_GROUP, tm)
    pid = lax.broadcasted_iota(jnp.int32, shape3, 1)
    gid = lax.broadcasted_iota(jnp.int32, (N_GROUPS, 1, tm), 0)
    eid = lax.broadcasted_iota(jnp.int32, shape3, 0) * PER_GROUP + pid
    top1, i1 = _first_max(sb, pid, (1,), PER_GROUP)
    top2 = jnp.max(jnp.where(pid == i1, PICKED, sb), axis=1, keepdims=True)
    gscore = top1 + top2
    gsel = jnp.zeros((N_GROUPS, 1, tm), f32)
    for _ in range(TOPK_GROUPS):
        _, first = _first_max(gscore, gid, (0,), N_GROUPS)
        hit = gid == first
        gsel = jnp.where(hit, 1.0, gsel)
        gscore = jnp.where(hit, PICKED, gscore)
    cand = jnp.where(gsel > 0.0, sb, -1e30)
    firsts, gates = [], []
    picked = jnp.zeros(shape3, f32)
    for _ in range(TOP_K):
        _, first = _first_max(cand, eid, (0, 1), N_EXPERTS)
        hit = eid == first
        firsts.append(first)
        gates.append(_sum_axes(jnp.where(hit, s, 0.0), (0, 1)))
        picked = jnp.where(hit, 1.0, picked)
        cand = jnp.where(hit, PICKED, cand)
    gsum = gates[0]
    for gk in gates[1:]:
        gsum = gsum + gk
    earlier = (lax.broadcasted_iota(jnp.int32, (tm, tm), 0) < lax.broadcasted_iota(jnp.int32, (tm, tm), 1))
    picked2 = picked.reshape(N_EXPERTS, tm)
    rank = run_ref[...] + _dot(picked2.astype(bf16), jnp.where(earlier, 1.0, 0.0).astype(bf16))
    run_new = run_ref[...] + jnp.sum(picked2, axis=1, keepdims=True)
    run_ref[...] = run_new
    cnt_ref[...] = jnp.broadcast_to(run_new, cnt_ref.shape)
    rank3 = rank.reshape(shape3)
    for k in range(TOP_K):
        hit = eid == firsts[k]
        eidx_ref[k:k + 1, :] = firsts[k].reshape(1, tm)
        gate_ref[k:k + 1, :] = (gates[k] / gsum * ROUTE_SCALE).reshape(1, tm)
        rank_ref[k:k + 1, :] = _sum_axes(jnp.where(hit, rank3, 0.0), (0, 1)).reshape(1, tm).astype(jnp.int32)


def _moe_pre(x, mix, g, b, w_router, b_router, w_sh_gu, w_sh_down):
    m, d = x.shape
    bf16 = jnp.bfloat16
    tm = min(m, 512)
    row = lambda i: (i, 0)
    col = lambda i: (0, i)
    fixed = lambda i: (0, 0)
    d_sh2 = w_sh_gu.shape[1]
    return pl.pallas_call(
        _moe_pre_body,
        grid=(m // tm,),
        in_specs=[pl.BlockSpec((tm, d), row), pl.BlockSpec((tm, d), row),
                  pl.BlockSpec((1, d), fixed), pl.BlockSpec((1, d), fixed),
                  pl.BlockSpec((N_EXPERTS, d), fixed), pl.BlockSpec((N_EXPERTS, 1), fixed),
                  pl.BlockSpec((d, d_sh2), fixed), pl.BlockSpec((d_sh2 // 2, d), fixed)],
        out_specs=[pl.BlockSpec((tm, d), row), pl.BlockSpec((2, tm, PACK_W), lambda i: (0, i, 0)),
                   pl.BlockSpec((tm, d), row),
                   pl.BlockSpec((TOP_K, tm), col), pl.BlockSpec((TOP_K, tm), col), pl.BlockSpec((TOP_K, tm), col),
                   pl.BlockSpec((N_EXPERTS, LANE), fixed)],
        out_shape=[jax.ShapeDtypeStruct((m, d), jnp.float32), jax.ShapeDtypeStruct((2, m, PACK_W), jnp.int32),
                   jax.ShapeDtypeStruct((m, d), jnp.float32),
                   jax.ShapeDtypeStruct((TOP_K, m), jnp.int32), jax.ShapeDtypeStruct((TOP_K, m), jnp.float32),
                   jax.ShapeDtypeStruct((TOP_K, m), jnp.int32),
                   jax.ShapeDtypeStruct((N_EXPERTS, LANE), jnp.float32)],
        scratch_shapes=[pltpu.VMEM((N_EXPERTS, 1), jnp.float32)],
        compiler_params=pltpu.CompilerParams(dimension_semantics=("arbitrary",),
                                             vmem_limit_bytes=48 * 1024 * 1024),
        name="moe_pre",
    )(x, mix, g.reshape(1, d), b.reshape(1, d), w_router.T.astype(bf16), b_router.reshape(N_EXPERTS, 1),
      w_sh_gu.astype(bf16), w_sh_down.astype(bf16))


def _moe_expert_body(exp_ref, first_ref, active_ref, xs_ref, wgu_ref, wdn_ref, y_ref, wgu_bf, wdn_bf):
    i = pl.program_id(0)
    bf16 = jnp.bfloat16

    @pl.when(first_ref[i] == 1)
    def _():
        wgu_bf[...] = wgu_ref[0].astype(bf16)
        wdn_bf[...] = wdn_ref[0].astype(bf16)

    @pl.when(active_ref[i] == 1)
    def _():
        h = None
        for hw in range(2):
            for q, xq in enumerate(_unpack_words(xs_ref[hw])):
                r0 = (2 * hw + q) * PACK_W
                part = _dot(xq.astype(bf16), wgu_bf[r0:r0 + PACK_W, :])
                h = part if h is None else h + part
        d_e = h.shape[1] // 2
        act = (jax.nn.silu(h[:, :d_e]) * h[:, d_e:]).astype(bf16)
        y_ref[0], y_ref[1] = _pack_rows(_dot(act, wdn_bf[...]))

    @pl.when(active_ref[i] == 0)
    def _():
        y_ref[...] = jnp.zeros(y_ref.shape, y_ref.dtype)


def _moe_experts(xs, blk_exp, blk_first, blk_active, w_exp_gu, w_exp_down, bm):
    n_slots = xs.shape[1]
    d = w_exp_gu.shape[1]
    n_blk = n_slots // bm
    d_e2 = w_exp_gu.shape[2]
    words = lambda i, e, f, a: (0, i, 0)
    grid_spec = pltpu.PrefetchScalarGridSpec(
        num_scalar_prefetch=3,
        grid=(n_blk,),
        in_specs=[pl.BlockSpec((2, bm, PACK_W), words),
                  pl.BlockSpec((1, d, d_e2), lambda i, e, f, a: (e[i], 0, 0)),
                  pl.BlockSpec((1, d_e2 // 2, d), lambda i, e, f, a: (e[i], 0, 0))],
        out_specs=pl.BlockSpec((2, bm, PACK_W), words),
        scratch_shapes=[pltpu.VMEM((d, d_e2), jnp.bfloat16), pltpu.VMEM((d_e2 // 2, d), jnp.bfloat16)])
    return pl.pallas_call(
        _moe_expert_body,
        grid_spec=grid_spec,
        out_shape=jax.ShapeDtypeStruct((2, n_slots, PACK_W), jnp.int32),
        compiler_params=pltpu.CompilerParams(dimension_semantics=("arbitrary",),
                                             vmem_limit_bytes=48 * 1024 * 1024),
        name="moe_experts",
    )(blk_exp, blk_first, blk_active, xs, w_exp_gu, w_exp_down)


def _combine_ln_body(x_ref, yg_ref, gt_ref, sh_ref, g_ref, b_ref, o_ref):
    gt = gt_ref[...]
    parts = []
    for hw in range(2):
        lo_acc = hi_acc = None
        for k in range(TOP_K):
            lo, hi = _unpack_words(yg_ref[hw, k])
            gk = gt[:, k:k + 1]
            lo_acc = lo * gk if lo_acc is None else lo_acc + lo * gk
            hi_acc = hi * gk if hi_acc is None else hi_acc + hi * gk
        parts += [lo_acc, hi_acc]
    routed = jnp.concatenate(parts, axis=1)
    o_ref[...] = _ln_rows(ALPHA * x_ref[...] + (routed + sh_ref[...]), g_ref[...], b_ref[...])


def _combine_ln(x, yg, gate_t, shared, g, b):
    m, d = x.shape
    tm = min(m, 256)
    row = lambda i: (i, 0)
    fixed = lambda i: (0, 0)
    return pl.pallas_call(
        _combine_ln_body,
        grid=(m // tm,),
        in_specs=[pl.BlockSpec((tm, d), row), pl.BlockSpec((2, TOP_K, tm, PACK_W), lambda i: (0, 0, i, 0)),
                  pl.BlockSpec((tm, TOP_K), row), pl.BlockSpec((tm, d), row),
                  pl.BlockSpec((1, d), fixed), pl.BlockSpec((1, d), fixed)],
        out_specs=pl.BlockSpec((tm, d), row),
        out_shape=jax.ShapeDtypeStruct((m, d), jnp.float32),
        compiler_params=pltpu.CompilerParams(dimension_semantics=("arbitrary",)),
        name="combine_ln",
    )(x, yg, gate_t, shared, g.reshape(1, d), b.reshape(1, d))


def _moe_layer(x, mix, ln1_g, ln1_b, ln2_g, ln2_b, w_router, b_router, w_exp_gu, w_exp_down, w_sh_gu, w_sh_down):
    m, d = x.shape
    x1, xp, shared, eidx, gate8, rank8, counts = _moe_pre(x, mix, ln1_g, ln1_b, w_router, b_router,
                                                           w_sh_gu, w_sh_down)
    bm = 256 if m * TOP_K >= 256 * N_EXPERTS else MOE_BLK
    n_blk = (m * TOP_K) // bm + N_EXPERTS
    counts = counts[:, 0].astype(jnp.int32)
    padded = (counts + bm - 1) // bm * bm
    pad_end = jnp.cumsum(padded)
    pad_start = pad_end - padded
    start_of = jnp.sum(jnp.where(eidx[:, :, None] == jnp.arange(N_EXPERTS), pad_start, 0), axis=-1)
    dest = (start_of + rank8).reshape(-1)
    tok = jnp.tile(jnp.arange(m, dtype=jnp.int32), TOP_K)
    slot_tok = jnp.zeros((n_blk * bm,), jnp.int32).at[dest].set(tok)
    blk_start = jnp.arange(n_blk, dtype=jnp.int32) * bm
    blk_exp = jnp.minimum(jnp.sum(pad_end[None, :] <= blk_start[:, None], axis=1), N_EXPERTS - 1).astype(jnp.int32)
    blk_active = (blk_start < pad_end[-1]).astype(jnp.int32)
    blk_first = jnp.concatenate([jnp.ones((1,), jnp.int32), (blk_exp[1:] != blk_exp[:-1]).astype(jnp.int32)])
    n_slots = n_blk * bm
    xs = _gather_rows(xp.reshape(2 * m, PACK_W), jnp.concatenate([slot_tok, slot_tok + m]))
    y = _moe_experts(xs.reshape(2, n_slots, PACK_W), blk_exp, blk_first, blk_active, w_exp_gu, w_exp_down, bm)
    yg = _gather_rows(y.reshape(2 * n_slots, PACK_W), jnp.concatenate([dest, dest + n_slots]))
    return _combine_ln(x1, yg.reshape(2, TOP_K, m, PACK_W), gate8.T, shared, ln2_g, ln2_b)


def _trunk(x, pos, gla_state, nsa_cache, page_table, win_buf, conv_buf,
           w_in_ab, w_gla_gate, b_gla_gate, gla_norm_g, w_cmp_pool, w_out_ab,
           w_pw1, b_pw1, w_dw, b_dw, conv_ln_g, conv_ln_b, w_pw2, b_pw2,
           ln_g, ln_b, w_router, b_router, w_exp_gu, w_exp_down, w_sh_gu, w_sh_down):
    new_gla, new_rows, new_win, new_conv = [], [], [], []
    for layer in range(DEPTH):
        i = layer // 2
        if layer % 2 == 0:
            mix, s_a, rows, win = _ab_mixer(
                x, pos, w_in_ab[i], w_gla_gate[i], b_gla_gate[i], gla_norm_g[i], w_cmp_pool[i], w_out_ab[i],
                None if gla_state is None else gla_state[i],
                None if nsa_cache is None else nsa_cache[i], page_table,
                None if win_buf is None else win_buf[i])
            new_gla.append(s_a)
            new_rows.append(rows)
            new_win.append(win)
        else:
            mix, cb = _conv_module(x, None if conv_buf is None else conv_buf[i], w_pw1[i], b_pw1[i],
                                   w_dw[i], b_dw[i], conv_ln_g[i], conv_ln_b[i], w_pw2[i], b_pw2[i])
            new_conv.append(cb)
        bsz, t_, d = x.shape
        x = _moe_layer(x.reshape(-1, d), mix.reshape(-1, d), ln_g[layer, 0], ln_b[layer, 0],
                       ln_g[layer, 1], ln_b[layer, 1], w_router[layer], b_router[layer],
                       w_exp_gu[layer], w_exp_down[layer], w_sh_gu[layer], w_sh_down[layer]).reshape(bsz, t_, d)
    return x, jnp.stack(new_gla), jnp.stack(new_rows), jnp.stack(new_win), jnp.stack(new_conv)


def kernel(x_prompt, x_sample, state_gla, cache_nsa_kv, state_nsa_win, state_conv, page_table,
           w_in_ab, w_gla_gate, b_gla_gate, gla_norm_g, w_cmp_pool, w_out_ab,
           w_pw1, b_pw1, w_dw, b_dw, conv_ln_g, conv_ln_b, w_pw2, b_pw2,
           ln_g, ln_b, w_router, b_router, w_exp_gu, w_exp_down, w_sh_gu, w_sh_down):
    weights = (w_in_ab, w_gla_gate, b_gla_gate, gla_norm_g, w_cmp_pool, w_out_ab,
               w_pw1, b_pw1, w_dw, b_dw, conv_ln_g, conv_ln_b, w_pw2, b_pw2,
               ln_g, ln_b, w_router, b_router, w_exp_gu, w_exp_down, w_sh_gu, w_sh_down)
    past_len = page_table.shape[1] * PAGE_SIZE
    pos_p = jnp.arange(x_prompt.shape[1])
    pos_s = past_len + jnp.arange(x_sample.shape[1])
    y_prompt, gla_p, rows_p, win_p, conv_p = _trunk(x_prompt, pos_p, None, None, None, None, None, *weights)
    y_sample, gla_s, rows_s, win_s, conv_s = _trunk(x_sample, pos_s, state_gla, cache_nsa_kv, page_table,
                                                    state_nsa_win, state_conv, *weights)
    return (y_prompt, y_sample, gla_p, gla_s, rows_p, rows_s, win_p, win_s, conv_p, conv_s)
```

```python
import functools
import math

import jax
import jax.numpy as jnp
import numpy as np
from jax import lax
from jax.experimental import pallas as pl
from jax.experimental.pallas import tpu as pltpu
from jax.experimental.pallas import tpu_sc as plsc

D_MODEL = 1024
DEPTH = 2
PAGE_SIZE = 128

GLA_HEADS = 4
GLA_DV = D_MODEL // 2 // GLA_HEADS
GLA_DK = GLA_DV // 2
GLA_RANK = 16
GLA_TAU = 16.0
GLA_CHUNK = 64

NSA_HEADS = 8
NSA_KV_HEADS = 2
NSA_GROUP = NSA_HEADS // NSA_KV_HEADS
HEAD_DIM = D_MODEL // 2 // NSA_HEADS
CMP_BLK = 32
CMP_STRIDE = 16
SEL_BLK = 64
SEL_TOPN = 16
WINDOW = 512
Q_BLK = 128
FORCE_BONUS = 100.0
ROPE_DIM = HEAD_DIM // 4
ROPE_THETA = 500000.0

GLA_SIZES = (GLA_HEADS * GLA_DK, GLA_HEADS * GLA_DK, GLA_HEADS * GLA_DV, GLA_HEADS * GLA_DV, GLA_RANK)
NSA_SIZES = (NSA_HEADS * HEAD_DIM, 6 * NSA_KV_HEADS * HEAD_DIM, 3 * NSA_HEADS)

CONV_W = 31
D_CONV = D_MODEL

N_EXPERTS = 64
N_GROUPS = 8
TOPK_GROUPS = 4
TOP_K = 8
D_EXPERT = 256
ROUTE_SCALE = 2.5
MOE_BLK = 128

ALPHA = (2 * DEPTH) ** 0.25
LN_EPS = 1e-5

LANE = 128


def _dot(a, b):
    return jnp.dot(a, b, preferred_element_type=jnp.float32)


def _dot_nt(a, b):
    return lax.dot_general(a, b, (((1,), (1,)), ((), ())), preferred_element_type=jnp.float32)


def _mm_body(x_ref, w_ref, o_ref):
    o_ref[...] = _dot(x_ref[...].astype(jnp.bfloat16), w_ref[...].astype(jnp.bfloat16))


def _mm(x, w, keep_pad=False):
    m, k = x.shape
    n = w.shape[1]
    n_pad = -(-n // LANE) * LANE
    w = w.astype(jnp.bfloat16)
    if n_pad != n:
        w = jnp.pad(w, ((0, 0), (0, n_pad - n)))
    tm = min(m, 512)
    out = pl.pallas_call(
        _mm_body,
        grid=(m // tm,),
        in_specs=[pl.BlockSpec((tm, k), lambda i: (i, 0)),
                  pl.BlockSpec((k, n_pad), lambda i: (0, 0))],
        out_specs=pl.BlockSpec((tm, n_pad), lambda i: (i, 0)),
        out_shape=jax.ShapeDtypeStruct((m, n_pad), jnp.float32),
        compiler_params=pltpu.CompilerParams(dimension_semantics=("arbitrary",),
                                             vmem_limit_bytes=48 * 1024 * 1024),
        name="mm",
    )(x, w)
    return out if keep_pad or n_pad == n else out[:, :n]


def _mm3(x, w):
    b, t, d = x.shape
    return _mm(x.reshape(b * t, d), w).reshape(b, t, -1)


def _split_cols(h, sizes):
    return jnp.split(h, np.cumsum(sizes)[:-1].tolist(), axis=-1)


def _layer_norm(x, g, b):
    mu = x.mean(-1, keepdims=True)
    var = jnp.square(x - mu).mean(-1, keepdims=True)
    return (x - mu) * lax.rsqrt(var + LN_EPS) * g + b


def _rms_norm(x, g):
    return x * lax.rsqrt(jnp.mean(x * x, -1, keepdims=True) + LN_EPS) * g


def _partial_rope(x, pos):
    half = ROPE_DIM // 2
    inv_freq = jnp.power(ROPE_THETA, -jnp.arange(half, dtype=jnp.float32) / half)
    ang = pos.astype(jnp.float32)[:, None] * inv_freq
    ang = ang.reshape(ang.shape[0], *([1] * (x.ndim - 3)), half)
    cos, sin = jnp.cos(ang), jnp.sin(ang)
    x1 = x[..., :half]
    x2 = x[..., half:ROPE_DIM]
    rot = jnp.concatenate([x1 * cos - x2 * sin, x2 * cos + x1 * sin], -1)
    return jnp.concatenate([rot, x[..., ROPE_DIM:]], -1)


def _masked_softmax(s, mask):
    s = jnp.where(mask, s, -jnp.inf)
    m = jnp.max(s, axis=-1, keepdims=True)
    m = jnp.where(jnp.isfinite(m), m, 0.0)
    p = jnp.exp(s - m)
    return p / jnp.maximum(p.sum(-1, keepdims=True), 1e-30)


def _gla_recurrence(q, k, v, log_a, s0):
    bsz, t_, nh, _ = q.shape
    c = math.gcd(t_, GLA_CHUNK)
    n = t_ // c

    def chunks(a):
        return jnp.moveaxis(a.reshape(bsz, n, c, *a.shape[2:]), 1, 0)

    causal = jnp.tril(jnp.ones((c, c), dtype=bool))[None, :, :, None, None]

    def step(s, inp):
        qc, kc, vc, lc = inp
        bc = jnp.cumsum(lc, axis=1)
        decay = jnp.exp(jnp.where(causal, bc[:, :, None] - bc[:, None, :], -jnp.inf))
        attn = jnp.einsum('bijhd,bjhd->bhij', qc[:, :, None] * decay, kc)
        o = jnp.einsum('bhij,bjhe->bihe', attn, vc) + jnp.einsum('bihd,bhde->bihe', qc * jnp.exp(bc), s)
        bl = bc[:, -1]
        s = jnp.exp(bl)[..., None] * s + jnp.einsum('bjhd,bjhe->bhde', kc * jnp.exp(bl[:, None] - bc), vc)
        return s, o

    s_fin, o = lax.scan(step, s0, (chunks(q), chunks(k), chunks(v), chunks(log_a)))
    return jnp.moveaxis(o, 0, 1).reshape(bsz, t_, nh, -1), s_fin


def _compress(k, v, w_pool):
    bsz, length = k.shape[:2]
    n_sub = length // CMP_STRIDE

    def pool(a, w):
        sub = a[:, :n_sub * CMP_STRIDE].reshape(bsz, n_sub, CMP_STRIDE, *a.shape[2:])
        first = jnp.einsum('bnjhd,j->bnhd', sub, w[:CMP_STRIDE])
        second = jnp.einsum('bnjhd,j->bnhd', sub, w[CMP_STRIDE:])
        return first[:, :-1] + second[:, 1:]

    cend = jnp.arange(n_sub - 1) * CMP_STRIDE + CMP_BLK - 1
    return pool(k, w_pool[0]), pool(v, w_pool[1]), cend


def _to_sel_blocks(a, n_sel):
    bsz, length = a.shape[:2]
    a = jnp.pad(a, ((0, 0), (0, n_sel * SEL_BLK - length), (0, 0), (0, 0)))
    return a.reshape(bsz, n_sel, SEL_BLK, NSA_KV_HEADS, HEAD_DIM).transpose(0, 3, 1, 2, 4)


def _nsa_attend(q_raw, q_rot, qpos, gates, kc, vc, cend, ksb, vsb, kw, vw, kwpos):
    scale = HEAD_DIM ** -0.5
    bsz, tq = q_raw.shape[:2]
    n_cmp, n_sel = kc.shape[1], ksb.shape[2]
    s_c = jnp.einsum('bqhgd,bnhd->bhgqn', q_raw, kc) * scale
    p_c = _masked_softmax(s_c, cend[None, :] <= qpos[:, None])
    o_c = jnp.einsum('bhgqn,bnhd->bqhgd', p_c, vc)
    ratio = SEL_BLK // CMP_STRIDE
    imp = p_c.sum(axis=2)
    imp = jnp.pad(imp, ((0, 0), (0, 0), (0, 0), (1, ratio * (n_sel + 1) - 1 - n_cmp)))
    imp = imp.reshape(bsz, NSA_KV_HEADS, tq, n_sel + 1, ratio)
    imp_s = imp[..., :n_sel, :].sum(-1) + imp[..., 1:, 0]
    blk = jnp.arange(n_sel)[None, :]
    cur = (qpos // SEL_BLK)[:, None]
    valid = blk * SEL_BLK <= qpos[:, None]
    forced = (blk == 0) | (blk == cur) | (blk == cur - 1)
    score = jnp.where(valid, imp_s + jnp.where(forced, FORCE_BONUS, 0.0), -jnp.inf)
    k_top = min(SEL_TOPN, n_sel)
    _, sel = lax.top_k(score, k_top)
    take = jax.vmap(jax.vmap(lambda blocks, idx: blocks[idx]))
    ks = take(ksb, sel).reshape(bsz, NSA_KV_HEADS, tq, k_top * SEL_BLK, HEAD_DIM)
    vs = take(vsb, sel).reshape(bsz, NSA_KV_HEADS, tq, k_top * SEL_BLK, HEAD_DIM)
    kpos = (sel[..., None] * SEL_BLK + jnp.arange(SEL_BLK)).reshape(bsz, NSA_KV_HEADS, tq, k_top * SEL_BLK)
    s_s = jnp.einsum('bqhgd,bhqkd->bhgqk', q_rot, ks) * scale
    p_s = _masked_softmax(s_s, (kpos <= qpos[:, None])[:, :, None])
    o_s = jnp.einsum('bhgqk,bhqkd->bqhgd', p_s, vs)
    s_w = jnp.einsum('bqhgd,bkhd->bhgqk', q_rot, kw) * scale
    kp, qp = kwpos[None, :], qpos[:, None]
    p_w = _masked_softmax(s_w, (kp <= qp) & (kp > qp - WINDOW) & (kp >= 0))
    o_w = jnp.einsum('bhgqk,bkhd->bqhgd', p_w, vw)
    return gates[..., 0:1] * o_c + gates[..., 1:2] * o_s + gates[..., 2:3] * o_w


NSA_ROWS = NSA_GROUP * Q_BLK
SEL_KT = 512
N_SELB = 128
MASKED = -1e9
WIN_KEYS = WINDOW + Q_BLK
KK_W = 2 * HEAD_DIM + N_SELB


def _nsa_prompt_body(qr_ref, qo_ref, kc_ref, vct_ref, kk_ref, vvt_ref, g_ref, o_ref,
                     imp_ref, m_ref, l_ref, acc_ref):
    f32, bf16 = jnp.float32, jnp.bfloat16
    qb = pl.program_id(2)
    q0 = qb * Q_BLK
    qr_t = qr_ref[0, 0, 0]
    qo_t = qo_ref[0, 0, 0]
    n_cmp = kc_ref.shape[2]

    s_c = _dot(kc_ref[0, 0], qr_t)
    n_idx = lax.broadcasted_iota(jnp.int32, (n_cmp, NSA_ROWS), 0)
    qpos_c = q0 + (lax.broadcasted_iota(jnp.int32, (n_cmp, NSA_ROWS), 1) & (Q_BLK - 1))
    cmask = (n_idx * CMP_STRIDE + (CMP_BLK - 1)) <= qpos_c
    s_c = jnp.where(cmask, s_c, MASKED)
    m_c = jnp.max(s_c, axis=0, keepdims=True)
    p_c = jnp.where(cmask, jnp.exp(s_c - m_c), 0.0)
    p_c = p_c / jnp.maximum(jnp.sum(p_c, axis=0, keepdims=True), 1e-30)
    o_ct = _dot(vct_ref[0, 0], p_c.astype(bf16))

    imp = (p_c[:, 0:Q_BLK] + p_c[:, Q_BLK:2 * Q_BLK]) + p_c[:, 2 * Q_BLK:3 * Q_BLK] + p_c[:, 3 * Q_BLK:]
    imp_ref[0:8, :] = jnp.zeros((8, Q_BLK), f32)
    imp_ref[8:8 + n_cmp, :] = imp
    ratio = SEL_BLK // CMP_STRIDE
    n_selb = n_cmp // ratio
    imp_s = imp_ref[pl.ds(7, n_selb, stride=ratio), :]
    for r in range(ratio):
        imp_s = imp_s + imp_ref[pl.ds(8 + r, n_selb, stride=ratio), :]
    blk = lax.broadcasted_iota(jnp.int32, (n_selb, Q_BLK), 0)
    qpos_s = q0 + lax.broadcasted_iota(jnp.int32, (n_selb, Q_BLK), 1)
    cur = lax.shift_right_logical(qpos_s, int(math.log2(SEL_BLK)))
    valid = blk * SEL_BLK <= qpos_s
    forced = (blk == 0) | (blk == cur) | (blk == cur - 1)
    score = jnp.where(valid, imp_s + jnp.where(forced, FORCE_BONUS, 0.0), -1e30)
    picked = jnp.zeros((n_selb, Q_BLK), f32)
    for _ in range(SEL_TOPN):
        best = jnp.max(score, axis=0, keepdims=True)
        first = jnp.min(jnp.where(score == best, blk, n_selb), axis=0, keepdims=True)
        hit = blk == first
        picked = jnp.where(hit, 1.0, picked)
        score = jnp.where(hit, -3e38, score)
    selb_t = jnp.where(valid, picked, 0.0)
    if n_selb < N_SELB:
        selb_t = jnp.concatenate([selb_t, jnp.zeros((N_SELB - n_selb, Q_BLK), f32)], axis=0)
    selb_t = ((selb_t - 1.0) * (-MASKED)).astype(bf16)
    selb_t = jnp.concatenate([selb_t] * NSA_GROUP, axis=1)

    zeros_q = jnp.zeros((HEAD_DIM, NSA_ROWS), bf16)
    q_sel = jnp.concatenate([qo_t, zeros_q, selb_t], axis=0)
    q_win = jnp.concatenate([zeros_q, qo_t, jnp.zeros((N_SELB, NSA_ROWS), bf16)], axis=0)
    qpos_r = q0 + (lax.broadcasted_iota(jnp.int32, (1, NSA_ROWS), 1) & (Q_BLK - 1))

    def v_tiles(first, count):
        return jnp.concatenate([vvt_ref[0, 0, first + j] for j in range(count)], axis=1)

    m_ref[...] = jnp.full(m_ref.shape, MASKED, f32)
    l_ref[...] = jnp.zeros(l_ref.shape, f32)
    acc_ref[...] = jnp.zeros(acc_ref.shape, f32)

    def sel_tile(k0, kt, causal):
        s = _dot(kk_ref[0, 0, pl.ds(k0, kt), :], q_sel)
        if causal:
            kpos = k0 + lax.broadcasted_iota(jnp.int32, (kt, NSA_ROWS), 0)
            s = jnp.where(kpos <= qpos_r, s, MASKED)
        m_old = m_ref[...]
        m_new = jnp.maximum(m_old, jnp.max(s, axis=0, keepdims=True))
        alpha = jnp.exp(m_old - m_new)
        p = jnp.exp(s - m_new)
        l_ref[...] = alpha * l_ref[...] + jnp.sum(p, axis=0, keepdims=True)
        vt = v_tiles(k0 // Q_BLK, kt // Q_BLK)
        acc_ref[...] = alpha * acc_ref[...] + _dot(vt, p.astype(bf16))
        m_ref[...] = m_new

    n_full = q0 // SEL_KT

    def full_step(t, c):
        sel_tile(pl.multiple_of(t * SEL_KT, SEL_KT), SEL_KT, False)
        return c

    lax.fori_loop(0, n_full, full_step, 0)
    d0 = n_full * SEL_KT

    def diag_step(t, c):
        sel_tile(pl.multiple_of(d0 + t * Q_BLK, Q_BLK), Q_BLK, False)
        return c

    lax.fori_loop(0, (q0 - d0) // Q_BLK, diag_step, 0)
    sel_tile(pl.multiple_of(q0, Q_BLK), Q_BLK, True)
    o_st = acc_ref[0:HEAD_DIM, :] / l_ref[...]

    w0 = pl.multiple_of(jnp.maximum(q0 - WINDOW, 0), Q_BLK)
    s_w = _dot(kk_ref[0, 0, pl.ds(w0, WIN_KEYS), :], q_win)
    kpos_w = w0 + lax.broadcasted_iota(jnp.int32, (WIN_KEYS, NSA_ROWS), 0)
    s_w = jnp.where((kpos_w <= qpos_r) & (kpos_w > qpos_r - WINDOW), s_w, MASKED)
    p_w = jnp.exp(s_w - jnp.max(s_w, axis=0, keepdims=True))
    l_w = jnp.sum(p_w, axis=0, keepdims=True)
    acc_w = _dot(v_tiles(w0 // Q_BLK, WIN_KEYS // Q_BLK), p_w.astype(bf16))
    o_wt = acc_w[HEAD_DIM:2 * HEAD_DIM, :] / l_w

    g = g_ref[0, 0, 0]
    out_t = g[0:1, :] * o_ct + g[1:2, :] * o_st + g[2:3, :] * o_wt
    o_ref[0] = jnp.concatenate([out_t[:, g_ * Q_BLK:(g_ + 1) * Q_BLK] for g_ in range(NSA_GROUP)], axis=0).T


def _nsa_prompt(qr, qo, gt, kc_p, vct, kk, vvt):
    bsz, _, nqb = qr.shape[:3]
    t_ = nqb * Q_BLK
    n_cmp = kc_p.shape[2]
    per_blk = lambda b, h, i: (b, h, i, 0, 0)
    per_head = lambda b, h, i: (b, h, 0, 0)
    return pl.pallas_call(
        _nsa_prompt_body,
        grid=(bsz, NSA_KV_HEADS, nqb),
        in_specs=[pl.BlockSpec((1, 1, 1, HEAD_DIM, NSA_ROWS), per_blk),
                  pl.BlockSpec((1, 1, 1, HEAD_DIM, NSA_ROWS), per_blk),
                  pl.BlockSpec((1, 1, n_cmp, HEAD_DIM), per_head),
                  pl.BlockSpec((1, 1, HEAD_DIM, n_cmp), per_head),
                  pl.BlockSpec((1, 1, t_, KK_W), per_head),
                  pl.BlockSpec((1, 1, nqb, 2 * HEAD_DIM, Q_BLK), lambda b, h, i: (b, h, 0, 0, 0)),
                  pl.BlockSpec((1, 1, 1, 3, NSA_ROWS), per_blk)],
        out_specs=pl.BlockSpec((1, Q_BLK, NSA_GROUP * HEAD_DIM), lambda b, h, i: (b, i, h)),
        out_shape=jax.ShapeDtypeStruct((bsz, t_, NSA_HEADS * HEAD_DIM), jnp.float32),
        scratch_shapes=[pltpu.VMEM((8 + n_cmp, Q_BLK), jnp.float32),
                        pltpu.VMEM((1, NSA_ROWS), jnp.float32),
                        pltpu.VMEM((1, NSA_ROWS), jnp.float32),
                        pltpu.VMEM((2 * HEAD_DIM, NSA_ROWS), jnp.float32)],
        compiler_params=pltpu.CompilerParams(
            dimension_semantics=("arbitrary", "arbitrary", "arbitrary"),
            vmem_limit_bytes=48 * 1024 * 1024),
        name="nsa_prompt",
    )(qr, qo, kc_p, vct, kk, vvt, gt)


GLA_SUB = 16
GLA_QK = GLA_HEADS * GLA_DK
GLA_V = GLA_HEADS * GLA_DV


def _dot_tn(a, b):
    return lax.dot_general(a, b, (((0,), (0,)), ((), ())), preferred_element_type=jnp.float32)


def _gla_body(q_ref, k_ref, v_ref, gr_ref, glr_ref, wg_ref, bg_ref, ng_ref, s0_ref, exp_ref, bd_ref,
              o_ref, sfin_ref, st_ref, b_ref, qd_ref, *, t_valid):
    f32, bf16 = jnp.float32, jnp.bfloat16
    tt = q_ref.shape[1]
    ti = pl.program_id(1)

    @pl.when(ti == 0)
    def _():
        st_ref[...] = s0_ref[0]

    row = lax.broadcasted_iota(jnp.int32, (tt, 1), 0)
    z = _dot(glr_ref[0][:, :GLA_RANK].astype(bf16), wg_ref[...]) + bg_ref[...]
    la = (jnp.minimum(z, 0.0) - jnp.log1p(jnp.exp(-jnp.abs(z)))) * (1.0 / GLA_TAU)
    la = jnp.where(ti * tt + row < t_valid, la, 0.0)
    seg = row & (GLA_SUB - 1)
    b = la
    for s in (1, 2, 4, 8):
        b = b + jnp.where(seg >= s, pltpu.roll(b, s, axis=0), 0.0)
    q = q_ref[0] * (GLA_DK ** -0.5)
    k = k_ref[0]
    v = v_ref[0]
    o = _dot((q * k).astype(bf16), exp_ref[...]) * v
    for d in range(1, GLA_SUB):
        decay = jnp.exp(jnp.minimum(b - pltpu.roll(b, d, axis=0), 0.0))
        w = jnp.where(seg >= d, q * pltpu.roll(k, d, axis=0) * decay, 0.0)
        o = o + _dot(w.astype(bf16), exp_ref[...]) * pltpu.roll(v, d, axis=0)
    o_ref[0] = o
    b_ref[...] = b
    qd_ref[...] = (q * jnp.exp(b)).astype(bf16)

    def block_step(c, carry):
        rows = pl.ds(pl.multiple_of(c * GLA_SUB, GLA_SUB), GLA_SUB)
        st = st_ref[...]
        o_ref[0, rows, :] += _dot_nt(qd_ref[rows, :], st.astype(bf16))
        bc = b_ref[rows, :]
        bl = bc[GLA_SUB - 1:GLA_SUB, :]
        kc = (k_ref[0, rows, :] * jnp.exp(bl - bc)).astype(bf16)
        upd = _dot_tn(v_ref[0, rows, :].astype(bf16), kc)
        st_ref[...] = st * jnp.exp(bl) + upd * bd_ref[...]
        return carry

    lax.fori_loop(0, tt // GLA_SUB, block_step, 0)
    sfin_ref[0] = st_ref[...]
    gr = gr_ref[0]
    gate = gr * jax.nn.sigmoid(gr)
    for h in range(GLA_HEADS):
        cols = slice(h * GLA_DV, (h + 1) * GLA_DV)
        oh = o_ref[0, :, cols]
        ms = jnp.mean(oh * oh, axis=-1, keepdims=True)
        o_ref[0, :, cols] = oh * lax.rsqrt(ms + LN_EPS) * ng_ref[...] * gate[:, cols]


def _gla(h, w_gla_gate, b_gla_gate, gla_norm_g, gla_state):
    bsz, t_, n_in = h.shape
    tp = -(-t_ // GLA_SUB) * GLA_SUB
    if tp != t_:
        h = jnp.pad(h, ((0, 0), (0, tp - t_), (0, 0)))
    tt = min(tp, 256)
    heads = np.arange(GLA_HEADS)
    expand = np.repeat(np.repeat(np.eye(GLA_HEADS, dtype=np.float32), GLA_DK, 0), GLA_DV, 1)
    bdmask = jnp.asarray(expand.T)
    if gla_state is None:
        s0 = jnp.zeros((bsz, GLA_V, GLA_QK), jnp.float32)
    else:
        s0 = jnp.zeros((bsz, GLA_HEADS, GLA_DV, GLA_HEADS, GLA_DK), jnp.float32)
        s0 = s0.at[:, heads, :, heads, :].set(gla_state.transpose(1, 0, 3, 2)).reshape(bsz, GLA_V, GLA_QK)
    tile = lambda width, blk: pl.BlockSpec((1, tt, width), lambda b, i: (b, i, blk))
    fixed2 = lambda shape: pl.BlockSpec(shape, lambda b, i: (0, 0))
    per_b = pl.BlockSpec((1, GLA_V, GLA_QK), lambda b, i: (b, 0, 0))
    o, s_t = pl.pallas_call(
        functools.partial(_gla_body, t_valid=t_),
        grid=(bsz, tp // tt),
        in_specs=[tile(GLA_QK, 0), tile(GLA_QK, 1), tile(GLA_V, 1), tile(GLA_V, 2),
                  tile(LANE, (2 * GLA_QK + 2 * GLA_V + NSA_SIZES[0] + NSA_SIZES[1]) // LANE),
                  fixed2((GLA_RANK, GLA_QK)), fixed2((1, GLA_QK)), fixed2((1, GLA_DV)), per_b,
                  fixed2((GLA_QK, GLA_V)), fixed2((GLA_V, GLA_QK))],
        out_specs=[pl.BlockSpec((1, tt, GLA_V), lambda b, i: (b, i, 0)), per_b],
        out_shape=[jax.ShapeDtypeStruct((bsz, tp, GLA_V), jnp.float32),
                   jax.ShapeDtypeStruct((bsz, GLA_V, GLA_QK), jnp.float32)],
        scratch_shapes=[pltpu.VMEM((GLA_V, GLA_QK), jnp.float32), pltpu.VMEM((tt, GLA_QK), jnp.float32),
                        pltpu.VMEM((tt, GLA_QK), jnp.bfloat16)],
        compiler_params=pltpu.CompilerParams(dimension_semantics=("arbitrary", "arbitrary"),
                                             vmem_limit_bytes=48 * 1024 * 1024),
        name="gla",
    )(h, h, h, h, h, w_gla_gate.astype(jnp.bfloat16), b_gla_gate.reshape(1, GLA_QK),
      gla_norm_g.reshape(1, GLA_DV), s0, jnp.asarray(expand, jnp.bfloat16), bdmask)
    s_new = s_t.reshape(bsz, GLA_HEADS, GLA_DV, GLA_HEADS, GLA_DK)[:, heads, :, heads, :]
    return o[:, :t_], s_new.transpose(1, 0, 3, 2)


COL_NQ = 2 * GLA_QK + 2 * GLA_V
COL_NKV = COL_NQ + NSA_SIZES[0]
COL_TAIL = COL_NKV + NSA_SIZES[1]
TAIL_GATE = GLA_RANK
_ORIG = np.cumsum((0,) + GLA_SIZES + NSA_SIZES)
IN_AB_PERM = np.concatenate([np.arange(_ORIG[0], _ORIG[4]), np.arange(_ORIG[5], _ORIG[7]),
                             np.arange(_ORIG[4], _ORIG[5]), np.arange(_ORIG[7], _ORIG[8])])
SUBS = Q_BLK // CMP_STRIDE


def _nsa_prep_body(nq_ref, kv0_ref, kv1_ref, kv2_ref, tail_ref, rc_ref, ru_ref, rd_ref, pool_ref,
                   rows_ref, win_ref, kk_ref, vvt_ref, qr_ref, qo_ref, g_ref, pooled_ref):
    bf16 = jnp.bfloat16
    q0 = pl.program_id(1) * Q_BLK
    kv_w = NSA_KV_HEADS * HEAD_DIM

    def rope(x):
        reps = x.shape[1] // LANE
        wide = lambda r: jnp.concatenate([r[...]] * reps, axis=1) if reps > 1 else r[...]
        half = ROPE_DIM // 2
        return (x * wide(rc_ref) + pltpu.roll(x, half, axis=1) * wide(ru_ref)
                + pltpu.roll(x, x.shape[1] - half, axis=1) * wide(rd_ref))

    kv0, kv1, kv2 = kv0_ref[0], kv1_ref[0], kv2_ref[0]
    k_sel, v_sel = rope(kv1[:, :kv_w]), kv1[:, kv_w:]
    k_win, v_win = rope(kv2[:, :kv_w]), kv2[:, kv_w:]
    rows_ref[0] = jnp.concatenate([kv0, k_sel, v_sel], axis=1)
    win_ref[0] = jnp.concatenate([k_win, v_win], axis=1)
    blk_id = lax.shift_right_logical(q0 + lax.broadcasted_iota(jnp.int32, (Q_BLK, N_SELB), 0),
                                     int(math.log2(SEL_BLK)))
    onehot = jnp.where(lax.broadcasted_iota(jnp.int32, (Q_BLK, N_SELB), 1) == blk_id, 1.0, 0.0).astype(bf16)
    q = nq_ref[0] * (HEAD_DIM ** -0.5)
    q_rot = rope(q)
    gates_t = jax.nn.sigmoid(tail_ref[0]).T
    for h in range(NSA_KV_HEADS):
        hs = slice(h * HEAD_DIM, (h + 1) * HEAD_DIM)
        kk_ref[0, h] = jnp.concatenate([k_sel[:, hs].astype(bf16), k_win[:, hs].astype(bf16), onehot], axis=1)
        vvt_ref[0, h, 0] = jnp.concatenate([v_sel[:, hs], v_win[:, hs]], axis=1).T.astype(bf16)
        gw = NSA_GROUP * HEAD_DIM
        for src, dst in ((q, qr_ref), (q_rot, qo_ref)):
            t = src[:, h * gw:(h + 1) * gw].T
            dst[0, h, 0] = jnp.concatenate([t[g * HEAD_DIM:(g + 1) * HEAD_DIM] for g in range(NSA_GROUP)],
                                           axis=1).astype(bf16)
        base = TAIL_GATE + h * NSA_GROUP * 3
        g_ref[0, h, 0] = jnp.concatenate(
            [jnp.concatenate([gates_t[base + 3 * g + c:base + 3 * g + c + 1] for g in range(NSA_GROUP)], axis=1)
             for c in range(3)], axis=0)
    kc_in, vc_in = kv0[:, :kv_w].astype(bf16), kv0[:, kv_w:].astype(bf16)
    pooled_ref[0] = jnp.concatenate([_dot(pool_ref[0], kc_in), _dot(pool_ref[1], kc_in),
                                     _dot(pool_ref[2], vc_in), _dot(pool_ref[3], vc_in)], axis=1)


def _nsa_prep(h, pos, w_cmp_pool):
    bsz, t_, _ = h.shape
    nqb = t_ // Q_BLK
    bf16 = jnp.bfloat16
    half = ROPE_DIM // 2
    inv_freq = jnp.power(ROPE_THETA, -jnp.arange(half, dtype=jnp.float32) / half)
    ang = pos.astype(jnp.float32)[:, None] * inv_freq
    cos, sin = jnp.cos(ang), jnp.sin(ang)
    rest = HEAD_DIM - ROPE_DIM
    z8, zr = jnp.zeros((t_, half), jnp.float32), jnp.zeros((t_, rest), jnp.float32)
    two = lambda a: jnp.concatenate([a, a], axis=1)
    rc = two(jnp.concatenate([cos, cos, jnp.ones((t_, rest), jnp.float32)], axis=1))
    ru = two(jnp.concatenate([z8, sin, zr], axis=1))
    rd = two(jnp.concatenate([-sin, z8, zr], axis=1))
    sub = np.arange(Q_BLK) // CMP_STRIDE == np.arange(SUBS)[:, None]
    w_rep = jnp.tile(w_cmp_pool.reshape(2, 2, CMP_STRIDE), (1, 1, SUBS))
    pool = jnp.where(sub[None, None], w_rep[:, :, None, :], 0.0).reshape(4, SUBS, Q_BLK).astype(bf16)
    kv_w = NSA_KV_HEADS * HEAD_DIM
    col = lambda width, off: pl.BlockSpec((1, Q_BLK, width), lambda b, i: (b, i, off // width))
    rows_t = pl.BlockSpec((Q_BLK, LANE), lambda b, i: (i, 0))
    head4 = lambda r, c: pl.BlockSpec((1, NSA_KV_HEADS, 1, r, c), lambda b, i: (b, 0, i, 0, 0))
    return pl.pallas_call(
        _nsa_prep_body,
        grid=(bsz, nqb),
        in_specs=[col(NSA_SIZES[0], COL_NQ), col(2 * kv_w, COL_NKV), col(2 * kv_w, COL_NKV + 2 * kv_w),
                  col(2 * kv_w, COL_NKV + 4 * kv_w), col(LANE, COL_TAIL), rows_t, rows_t, rows_t,
                  pl.BlockSpec((4, SUBS, Q_BLK), lambda b, i: (0, 0, 0))],
        out_specs=[pl.BlockSpec((1, Q_BLK, 4 * kv_w), lambda b, i: (b, i, 0)),
                   pl.BlockSpec((1, Q_BLK, 2 * kv_w), lambda b, i: (b, i, 0)),
                   pl.BlockSpec((1, NSA_KV_HEADS, Q_BLK, KK_W), lambda b, i: (b, 0, i, 0)),
                   head4(2 * HEAD_DIM, Q_BLK), head4(HEAD_DIM, NSA_ROWS), head4(HEAD_DIM, NSA_ROWS),
                   head4(3, NSA_ROWS),
                   pl.BlockSpec((1, SUBS, 4 * kv_w), lambda b, i: (b, i, 0))],
        out_shape=[jax.ShapeDtypeStruct((bsz, t_, 4 * kv_w), jnp.float32),
                   jax.ShapeDtypeStruct((bsz, t_, 2 * kv_w), jnp.float32),
                   jax.ShapeDtypeStruct((bsz, NSA_KV_HEADS, t_, KK_W), bf16),
                   jax.ShapeDtypeStruct((bsz, NSA_KV_HEADS, nqb, 2 * HEAD_DIM, Q_BLK), bf16),
                   jax.ShapeDtypeStruct((bsz, NSA_KV_HEADS, nqb, HEAD_DIM, NSA_ROWS), bf16),
                   jax.ShapeDtypeStruct((bsz, NSA_KV_HEADS, nqb, HEAD_DIM, NSA_ROWS), bf16),
                   jax.ShapeDtypeStruct((bsz, NSA_KV_HEADS, nqb, 3, NSA_ROWS), jnp.float32),
                   jax.ShapeDtypeStruct((bsz, t_ // CMP_STRIDE, 4 * kv_w), jnp.float32)],
        compiler_params=pltpu.CompilerParams(dimension_semantics=("arbitrary", "arbitrary")),
        name="nsa_prep",
    )(h, h, h, h, h, rc, ru, rd, pool)


def _ab_mixer(x, pos, w_in, w_gla_gate, b_gla_gate, gla_norm_g, w_cmp_pool, w_out,
              gla_state, nsa_cache, page_table, win_buf):
    bsz, t_, _ = x.shape
    h_in = _mm(x.reshape(bsz * t_, -1), w_in[:, IN_AB_PERM], keep_pad=True).reshape(bsz, t_, -1)
    o_a, s_a = _gla(h_in, w_gla_gate, b_gla_gate, gla_norm_g, gla_state)
    kv_w = NSA_KV_HEADS * HEAD_DIM
    if nsa_cache is None:
        rows2, win2, kk, vvt, qr, qo, gt, pooled = _nsa_prep(h_in, pos, w_cmp_pool)
        pooled = pooled.reshape(bsz, t_ // CMP_STRIDE, 4, NSA_KV_HEADS, HEAD_DIM)
        kc = pooled[:, :-1, 0] + pooled[:, 1:, 1]
        vc = pooled[:, :-1, 2] + pooled[:, 1:, 3]
        kc_p = jnp.pad(kc, ((0, 0), (0, 1), (0, 0), (0, 0))).transpose(0, 2, 1, 3).astype(jnp.bfloat16)
        vct = jnp.pad(vc, ((0, 0), (0, 1), (0, 0), (0, 0))).transpose(0, 2, 3, 1).astype(jnp.bfloat16)
        o_b = _nsa_prompt(qr, qo, gt, kc_p, vct, kk, vvt)
        rows_full = rows2.reshape(bsz, t_, 4, NSA_KV_HEADS, HEAD_DIM)
        new_win = win2[:, -min(WINDOW, t_):].reshape(bsz, -1, 2, NSA_KV_HEADS, HEAD_DIM)
    else:
        nq = h_in[..., COL_NQ:COL_NKV]
        nkv = h_in[..., COL_NKV:COL_TAIL]
        ngate = h_in[..., COL_TAIL + TAIL_GATE:COL_TAIL + TAIL_GATE + NSA_SIZES[2]]
        q_raw = nq.reshape(bsz, t_, NSA_KV_HEADS, NSA_GROUP, HEAD_DIM)
        q_rot = _partial_rope(q_raw, pos)
        kv = nkv.reshape(bsz, t_, 6, NSA_KV_HEADS, HEAD_DIM)
        k_sel = _partial_rope(kv[:, :, 2], pos)
        k_win = _partial_rope(kv[:, :, 4], pos)
        rows_full = jnp.stack([kv[:, :, 0], kv[:, :, 1], k_sel, kv[:, :, 3]], axis=2)
        rows_win = jnp.stack([k_win, kv[:, :, 5]], axis=2)
        gates = jax.nn.sigmoid(ngate).reshape(bsz, t_, NSA_KV_HEADS, NSA_GROUP, 3)
        past = nsa_cache[page_table].reshape(bsz, -1, 4, NSA_KV_HEADS, HEAD_DIM)
        keys = jnp.concatenate([past, rows_full], axis=1)
        length = keys.shape[1]
        kc, vc, cend = _compress(keys[:, :, 0], keys[:, :, 1], w_cmp_pool)
        n_sel = -(-length // SEL_BLK)
        ksb = _to_sel_blocks(keys[:, :, 2], n_sel)
        vsb = _to_sel_blocks(keys[:, :, 3], n_sel)
        w_buf = win_buf.shape[1]
        kw = jnp.concatenate([win_buf, rows_win], axis=1)
        kwpos = (length - t_) - w_buf + jnp.arange(kw.shape[1])
        o_b = _nsa_attend(q_raw, q_rot, pos, gates, kc, vc, cend, ksb, vsb, kw[:, :, 0], kw[:, :, 1], kwpos)
        o_b = o_b.reshape(bsz, t_, -1)
        new_win = kw[:, -w_buf:]
    y = _mm3(jnp.concatenate([o_a, o_b], axis=-1), w_out)
    return y, s_a, rows_full, new_win


def _conv_module(x, conv_buf, w_pw1, b_pw1, w_dw, b_dw, ln_g, ln_b, w_pw2, b_pw2):
    bsz = x.shape[0]
    a, g = jnp.split(_mm3(x, w_pw1) + b_pw1, 2, axis=-1)
    u = a * jax.nn.sigmoid(g)
    if conv_buf is None:
        conv_buf = jnp.zeros((bsz, CONV_W - 1, D_CONV), u.dtype)
    ext = jnp.concatenate([conv_buf, u], axis=1)
    c = lax.conv_general_dilated(ext, w_dw[:, None, :], (1,), 'VALID',
                                 dimension_numbers=('NWC', 'WIO', 'NWC'),
                                 feature_group_count=D_CONV) + b_dw
    c = jax.nn.silu(_layer_norm(c, ln_g, ln_b))
    return _mm3(c, w_pw2) + b_pw2, ext[:, -(CONV_W - 1):]


PACK_W = 256
SC_WINDOW = 128
SC_TILES = 32


def _pack_rows(y):
    out = []
    for h in range(2):
        lo = lax.bitcast_convert_type(y[:, 2 * h * PACK_W:(2 * h + 1) * PACK_W].astype(jnp.bfloat16)
                                      .astype(jnp.float32), jnp.uint32)
        hi = lax.bitcast_convert_type(y[:, (2 * h + 1) * PACK_W:(2 * h + 2) * PACK_W].astype(jnp.bfloat16)
                                      .astype(jnp.float32), jnp.uint32)
        out.append(lax.bitcast_convert_type((lo >> 16) | hi, jnp.int32))
    return out


def _unpack_words(w):
    u = lax.bitcast_convert_type(w, jnp.uint32)
    lo = lax.bitcast_convert_type(u << 16, jnp.float32)
    hi = lax.bitcast_convert_type(u & jnp.uint32(0xFFFF0000), jnp.float32)
    return lo, hi


def _gather_rows(src, idx):
    n = idx.shape[0]
    if n % (SC_WINDOW * SC_TILES) != 0:
        return jnp.take(src, idx, axis=0)
    mesh = plsc.VectorSubcoreMesh(core_axis_name="core", subcore_axis_name="subcore")

    @pl.kernel(out_type=jax.ShapeDtypeStruct((n, src.shape[1]), src.dtype), mesh=mesh)
    def gather_kernel(src_hbm, idx_hbm, out_hbm):
        def step(idx_vmem, out_vmem):
            pltpu.sync_copy(src_hbm.at[idx_vmem.at[0]], out_vmem)

        pltpu.emit_pipeline(
            step, grid=(n // SC_WINDOW,),
            in_specs=[pl.BlockSpec((1, SC_WINDOW), index_map=lambda i: (0, i))],
            out_specs=[pl.BlockSpec((SC_WINDOW, src.shape[1]), index_map=lambda i: (i, 0))],
            core_axis_name=("core", "subcore"),
            dimension_semantics=(pltpu.PARALLEL,),
        )(idx_hbm, out_hbm)

    return gather_kernel(src, idx.reshape(1, n))


PER_GROUP = N_EXPERTS // N_GROUPS
PICKED = -3e38


def _ln_rows(v, g, b):
    mu = jnp.mean(v, axis=-1, keepdims=True)
    c = v - mu
    var = jnp.mean(c * c, axis=-1, keepdims=True)
    return c * lax.rsqrt(var + LN_EPS) * g + b


def _first_max(v, ids, axes, sentinel):
    best = v
    for a in axes:
        best = jnp.max(best, axis=a, keepdims=True)
    first = jnp.where(v == best, ids, sentinel)
    for a in axes:
        first = jnp.min(first, axis=a, keepdims=True)
    return best, first


def _sum_axes(v, axes):
    for a in axes:
        v = jnp.sum(v, axis=a, keepdims=True)
    return v


def _moe_pre_body(x_ref, mix_ref, g_ref, b_ref, wr_ref, br_ref, wgu_ref, wdn_ref,
                  x1_ref, xp_ref, sh_ref, eidx_ref, gate_ref, rank_ref, cnt_ref, run_ref):
    f32, bf16 = jnp.float32, jnp.bfloat16
    tm = x_ref.shape[0]

    @pl.when(pl.program_id(0) == 0)
    def _():
        run_ref[...] = jnp.zeros(run_ref.shape, f32)

    x1 = _ln_rows(ALPHA * x_ref[...] + mix_ref[...], g_ref[...], b_ref[...])
    x1_ref[...] = x1
    x1b = x1.astype(bf16)
    xp_ref[0], xp_ref[1] = _pack_rows(x1)

    h = _dot(x1b, wgu_ref[...])
    d_sh = h.shape[1] // 2
    act = (jax.nn.silu(h[:, :d_sh]) * h[:, d_sh:]).astype(bf16)
    sh_ref[...] = _dot(act, wdn_ref[...])

    s = jax.nn.sigmoid(_dot_nt(wr_ref[...], x1b)).reshape(N_GROUPS, PER_GROUP, tm)
    sb = s + br_ref[...].reshape(N_GROUPS, PER_GROUP, 1)
    shape3 = (N_GROUPS, PER_GROUP, tm)
    pid = lax.broadcasted_iota(jnp.int32, shape3, 1)
    gid = lax.broadcasted_iota(jnp.int32, (N_GROUPS, 1, tm), 0)
    eid = lax.broadcasted_iota(jnp.int32, shape3, 0) * PER_GROUP + pid
    top1, i1 = _first_max(sb, pid, (1,), PER_GROUP)
    top2 = jnp.max(jnp.where(pid == i1, PICKED, sb), axis=1, keepdims=True)
    gscore = top1 + top2
    gsel = jnp.zeros((N_GROUPS, 1, tm), f32)
    for _ in range(TOPK_GROUPS):
        _, first = _first_max(gscore, gid, (0,), N_GROUPS)
        hit = gid == first
        gsel = jnp.where(hit, 1.0, gsel)
        gscore = jnp.where(hit, PICKED, gscore)
    cand = jnp.where(gsel > 0.0, sb, -1e30)
    firsts, gates = [], []
    picked = jnp.zeros(shape3, f32)
    for _ in range(TOP_K):
        _, first = _first_max(cand, eid, (0, 1), N_EXPERTS)
        hit = eid == first
        firsts.append(first)
        gates.append(_sum_axes(jnp.where(hit, s, 0.0), (0, 1)))
        picked = jnp.where(hit, 1.0, picked)
        cand = jnp.where(hit, PICKED, cand)
    gsum = gates[0]
    for gk in gates[1:]:
        gsum = gsum + gk
    earlier = (lax.broadcasted_iota(jnp.int32, (tm, tm), 0) < lax.broadcasted_iota(jnp.int32, (tm, tm), 1))
    picked2 = picked.reshape(N_EXPERTS, tm)
    rank = run_ref[...] + _dot(picked2.astype(bf16), jnp.where(earlier, 1.0, 0.0).astype(bf16))
    run_new = run_ref[...] + jnp.sum(picked2, axis=1, keepdims=True)
    run_ref[...] = run_new
    cnt_ref[...] = jnp.broadcast_to(run_new, cnt_ref.shape)
    rank3 = rank.reshape(shape3)
    for k in range(TOP_K):
        hit = eid == firsts[k]
        eidx_ref[k:k + 1, :] = firsts[k].reshape(1, tm)
        gate_ref[k:k + 1, :] = (gates[k] / gsum * ROUTE_SCALE).reshape(1, tm)
        rank_ref[k:k + 1, :] = _sum_axes(jnp.where(hit, rank3, 0.0), (0, 1)).reshape(1, tm).astype(jnp.int32)


def _moe_pre(x, mix, g, b, w_router, b_router, w_sh_gu, w_sh_down):
    m, d = x.shape
    bf16 = jnp.bfloat16
    tm = min(m, 512)
    row = lambda i: (i, 0)
    col = lambda i: (0, i)
    fixed = lambda i: (0, 0)
    d_sh2 = w_sh_gu.shape[1]
    return pl.pallas_call(
        _moe_pre_body,
        grid=(m // tm,),
        in_specs=[pl.BlockSpec((tm, d), row), pl.BlockSpec((tm, d), row),
                  pl.BlockSpec((1, d), fixed), pl.BlockSpec((1, d), fixed),
                  pl.BlockSpec((N_EXPERTS, d), fixed), pl.BlockSpec((N_EXPERTS, 1), fixed),
                  pl.BlockSpec((d, d_sh2), fixed), pl.BlockSpec((d_sh2 // 2, d), fixed)],
        out_specs=[pl.BlockSpec((tm, d), row), pl.BlockSpec((2, tm, PACK_W), lambda i: (0, i, 0)),
                   pl.BlockSpec((tm, d), row),
                   pl.BlockSpec((TOP_K, tm), col), pl.BlockSpec((TOP_K, tm), col), pl.BlockSpec((TOP_K, tm), col),
                   pl.BlockSpec((N_EXPERTS, LANE), fixed)],
        out_shape=[jax.ShapeDtypeStruct((m, d), jnp.float32), jax.ShapeDtypeStruct((2, m, PACK_W), jnp.int32),
                   jax.ShapeDtypeStruct((m, d), jnp.float32),
                   jax.ShapeDtypeStruct((TOP_K, m), jnp.int32), jax.ShapeDtypeStruct((TOP_K, m), jnp.float32),
                   jax.ShapeDtypeStruct((TOP_K, m), jnp.int32),
                   jax.ShapeDtypeStruct((N_EXPERTS, LANE), jnp.float32)],
        scratch_shapes=[pltpu.VMEM((N_EXPERTS, 1), jnp.float32)],
        compiler_params=pltpu.CompilerParams(dimension_semantics=("arbitrary",),
                                             vmem_limit_bytes=48 * 1024 * 1024),
        name="moe_pre",
    )(x, mix, g.reshape(1, d), b.reshape(1, d), w_router.T.astype(bf16), b_router.reshape(N_EXPERTS, 1),
      w_sh_gu.astype(bf16), w_sh_down.astype(bf16))


def _moe_expert_body(exp_ref, first_ref, active_ref, xs_ref, wgu_ref, wdn_ref, y_ref, wgu_bf, wdn_bf):
    i = pl.program_id(0)
    bf16 = jnp.bfloat16

    @pl.when(first_ref[i] == 1)
    def _():
        wgu_bf[...] = wgu_ref[0].astype(bf16)
        wdn_bf[...] = wdn_ref[0].astype(bf16)

    @pl.when(active_ref[i] == 1)
    def _():
        h = None
        for hw in range(2):
            for q, xq in enumerate(_unpack_words(xs_ref[hw])):
                r0 = (2 * hw + q) * PACK_W
                part = _dot(xq.astype(bf16), wgu_bf[r0:r0 + PACK_W, :])
                h = part if h is None else h + part
        d_e = h.shape[1] // 2
        act = (jax.nn.silu(h[:, :d_e]) * h[:, d_e:]).astype(bf16)
        y_ref[0], y_ref[1] = _pack_rows(_dot(act, wdn_bf[...]))

    @pl.when(active_ref[i] == 0)
    def _():
        y_ref[...] = jnp.zeros(y_ref.shape, y_ref.dtype)


def _moe_experts(xs, blk_exp, blk_first, blk_active, w_exp_gu, w_exp_down, bm):
    n_slots = xs.shape[1]
    d = w_exp_gu.shape[1]
    n_blk = n_slots // bm
    d_e2 = w_exp_gu.shape[2]
    words = lambda i, e, f, a: (0, i, 0)
    grid_spec = pltpu.PrefetchScalarGridSpec(
        num_scalar_prefetch=3,
        grid=(n_blk,),
        in_specs=[pl.BlockSpec((2, bm, PACK_W), words),
                  pl.BlockSpec((1, d, d_e2), lambda i, e, f, a: (e[i], 0, 0)),
                  pl.BlockSpec((1, d_e2 // 2, d), lambda i, e, f, a: (e[i], 0, 0))],
        out_specs=pl.BlockSpec((2, bm, PACK_W), words),
        scratch_shapes=[pltpu.VMEM((d, d_e2), jnp.bfloat16), pltpu.VMEM((d_e2 // 2, d), jnp.bfloat16)])
    return pl.pallas_call(
        _moe_expert_body,
        grid_spec=grid_spec,
        out_shape=jax.ShapeDtypeStruct((2, n_slots, PACK_W), jnp.int32),
        compiler_params=pltpu.CompilerParams(dimension_semantics=("arbitrary",),
                                             vmem_limit_bytes=48 * 1024 * 1024),
        name="moe_experts",
    )(blk_exp, blk_first, blk_active, xs, w_exp_gu, w_exp_down)


def _combine_ln_body(x_ref, yg_ref, gt_ref, sh_ref, g_ref, b_ref, o_ref):
    gt = gt_ref[...]
    parts = []
    for hw in range(2):
        lo_acc = hi_acc = None
        for k in range(TOP_K):
            lo, hi = _unpack_words(yg_ref[hw, k])
            gk = gt[:, k:k + 1]
            lo_acc = lo * gk if lo_acc is None else lo_acc + lo * gk
            hi_acc = hi * gk if hi_acc is None else hi_acc + hi * gk
        parts += [lo_acc, hi_acc]
    routed = jnp.concatenate(parts, axis=1)
    o_ref[...] = _ln_rows(ALPHA * x_ref[...] + (routed + sh_ref[...]), g_ref[...], b_ref[...])


def _combine_ln(x, yg, gate_t, shared, g, b):
    m, d = x.shape
    tm = min(m, 256)
    row = lambda i: (i, 0)
    fixed = lambda i: (0, 0)
    return pl.pallas_call(
        _combine_ln_body,
        grid=(m // tm,),
        in_specs=[pl.BlockSpec((tm, d), row), pl.BlockSpec((2, TOP_K, tm, PACK_W), lambda i: (0, 0, i, 0)),
                  pl.BlockSpec((tm, TOP_K), row), pl.BlockSpec((tm, d), row),
                  pl.BlockSpec((1, d), fixed), pl.BlockSpec((1, d), fixed)],
        out_specs=pl.BlockSpec((tm, d), row),
        out_shape=jax.ShapeDtypeStruct((m, d), jnp.float32),
        compiler_params=pltpu.CompilerParams(dimension_semantics=("arbitrary",)),
        name="combine_ln",
    )(x, yg, gate_t, shared, g.reshape(1, d), b.reshape(1, d))


def _moe_layer(x, mix, ln1_g, ln1_b, ln2_g, ln2_b, w_router, b_router, w_exp_gu, w_exp_down, w_sh_gu, w_sh_down):
    m, d = x.shape
    x1, xp, shared, eidx, gate8, rank8, counts = _moe_pre(x, mix, ln1_g, ln1_b, w_router, b_router,
                                                           w_sh_gu, w_sh_down)
    bm = 512 if m * TOP_K >= 512 * N_EXPERTS else MOE_BLK
    n_blk = (m * TOP_K) // bm + N_EXPERTS
    counts = counts[:, 0].astype(jnp.int32)
    padded = (counts + bm - 1) // bm * bm
    pad_end = jnp.cumsum(padded)
    pad_start = pad_end - padded
    start_of = jnp.sum(jnp.where(eidx[:, :, None] == jnp.arange(N_EXPERTS), pad_start, 0), axis=-1)
    dest = (start_of + rank8).reshape(-1)
    tok = jnp.tile(jnp.arange(m, dtype=jnp.int32), TOP_K)
    slot_tok = jnp.zeros((n_blk * bm,), jnp.int32).at[dest].set(tok)
    blk_start = jnp.arange(n_blk, dtype=jnp.int32) * bm
    blk_exp = jnp.minimum(jnp.sum(pad_end[None, :] <= blk_start[:, None], axis=1), N_EXPERTS - 1).astype(jnp.int32)
    blk_active = (blk_start < pad_end[-1]).astype(jnp.int32)
    blk_first = jnp.concatenate([jnp.ones((1,), jnp.int32), (blk_exp[1:] != blk_exp[:-1]).astype(jnp.int32)])
    n_slots = n_blk * bm
    xs = _gather_rows(xp.reshape(2 * m, PACK_W), jnp.concatenate([slot_tok, slot_tok + m]))
    y = _moe_experts(xs.reshape(2, n_slots, PACK_W), blk_exp, blk_first, blk_active, w_exp_gu, w_exp_down, bm)
    yg = _gather_rows(y.reshape(2 * n_slots, PACK_W), jnp.concatenate([dest, dest + n_slots]))
    return _combine_ln(x1, yg.reshape(2, TOP_K, m, PACK_W), gate8.T, shared, ln2_g, ln2_b)


def _trunk(x, pos, gla_state, nsa_cache, page_table, win_buf, conv_buf,
           w_in_ab, w_gla_gate, b_gla_gate, gla_norm_g, w_cmp_pool, w_out_ab,
           w_pw1, b_pw1, w_dw, b_dw, conv_ln_g, conv_ln_b, w_pw2, b_pw2,
           ln_g, ln_b, w_router, b_router, w_exp_gu, w_exp_down, w_sh_gu, w_sh_down):
    new_gla, new_rows, new_win, new_conv = [], [], [], []
    for layer in range(DEPTH):
        i = layer // 2
        if layer % 2 == 0:
            mix, s_a, rows, win = _ab_mixer(
                x, pos, w_in_ab[i], w_gla_gate[i], b_gla_gate[i], gla_norm_g[i], w_cmp_pool[i], w_out_ab[i],
                None if gla_state is None else gla_state[i],
                None if nsa_cache is None else nsa_cache[i], page_table,
                None if win_buf is None else win_buf[i])
            new_gla.append(s_a)
            new_rows.append(rows)
            new_win.append(win)
        else:
            mix, cb = _conv_module(x, None if conv_buf is None else conv_buf[i], w_pw1[i], b_pw1[i],
                                   w_dw[i], b_dw[i], conv_ln_g[i], conv_ln_b[i], w_pw2[i], b_pw2[i])
            new_conv.append(cb)
        bsz, t_, d = x.shape
        x = _moe_layer(x.reshape(-1, d), mix.reshape(-1, d), ln_g[layer, 0], ln_b[layer, 0],
                       ln_g[layer, 1], ln_b[layer, 1], w_router[layer], b_router[layer],
                       w_exp_gu[layer], w_exp_down[layer], w_sh_gu[layer], w_sh_down[layer]).reshape(bsz, t_, d)
    return x, jnp.stack(new_gla), jnp.stack(new_rows), jnp.stack(new_win), jnp.stack(new_conv)


def kernel(x_prompt, x_sample, state_gla, cache_nsa_kv, state_nsa_win, state_conv, page_table,
           w_in_ab, w_gla_gate, b_gla_gate, gla_norm_g, w_cmp_pool, w_out_ab,
           w_pw1, b_pw1, w_dw, b_dw, conv_ln_g, conv_ln_b, w_pw2, b_pw2,
           ln_g, ln_b, w_router, b_router, w_exp_gu, w_exp_down, w_sh_gu, w_sh_down):
    weights = (w_in_ab, w_gla_gate, b_gla_gate, gla_norm_g, w_cmp_pool, w_out_ab,
               w_pw1, b_pw1, w_dw, b_dw, conv_ln_g, conv_ln_b, w_pw2, b_pw2,
               ln_g, ln_b, w_router, b_router, w_exp_gu, w_exp_down, w_sh_gu, w_sh_down)
    past_len = page_table.shape[1] * PAGE_SIZE
    pos_p = jnp.arange(x_prompt.shape[1])
    pos_s = past_len + jnp.arange(x_sample.shape[1])
    y_prompt, gla_p, rows_p, win_p, conv_p = _trunk(x_prompt, pos_p, None, None, None, None, None, *weights)
    y_sample, gla_s, rows_s, win_s, conv_s = _trunk(x_sample, pos_s, state_gla, cache_nsa_kv, page_table,
                                                    state_nsa_win, state_conv, *weights)
    return (y_prompt, y_sample, gla_p, gla_s, rows_p, rows_s, win_p, win_s, conv_p, conv_s)
```

```python
import functools
import math

import jax
import jax.numpy as jnp
import numpy as np
from jax import lax
from jax.experimental import pallas as pl
from jax.experimental.pallas import tpu as pltpu
from jax.experimental.pallas import tpu_sc as plsc

D_MODEL = 1024
DEPTH = 2
PAGE_SIZE = 128

GLA_HEADS = 4
GLA_DV = D_MODEL // 2 // GLA_HEADS
GLA_DK = GLA_DV // 2
GLA_RANK = 16
GLA_TAU = 16.0
GLA_CHUNK = 64

NSA_HEADS = 8
NSA_KV_HEADS = 2
NSA_GROUP = NSA_HEADS // NSA_KV_HEADS
HEAD_DIM = D_MODEL // 2 // NSA_HEADS
CMP_BLK = 32
CMP_STRIDE = 16
SEL_BLK = 64
SEL_TOPN = 16
WINDOW = 512
Q_BLK = 128
FORCE_BONUS = 100.0
ROPE_DIM = HEAD_DIM // 4
ROPE_THETA = 500000.0

GLA_SIZES = (GLA_HEADS * GLA_DK, GLA_HEADS * GLA_DK, GLA_HEADS * GLA_DV, GLA_HEADS * GLA_DV, GLA_RANK)
NSA_SIZES = (NSA_HEADS * HEAD_DIM, 6 * NSA_KV_HEADS * HEAD_DIM, 3 * NSA_HEADS)

CONV_W = 31
D_CONV = D_MODEL

N_EXPERTS = 64
N_GROUPS = 8
TOPK_GROUPS = 4
TOP_K = 8
D_EXPERT = 256
ROUTE_SCALE = 2.5
MOE_BLK = 128

ALPHA = (2 * DEPTH) ** 0.25
LN_EPS = 1e-5

LANE = 128


def _dot(a, b):
    return jnp.dot(a, b, preferred_element_type=jnp.float32)


def _dot_nt(a, b):
    return lax.dot_general(a, b, (((1,), (1,)), ((), ())), preferred_element_type=jnp.float32)


def _mm_body(x_ref, w_ref, o_ref):
    o_ref[...] = _dot(x_ref[...].astype(jnp.bfloat16), w_ref[...].astype(jnp.bfloat16))


def _mm(x, w, keep_pad=False):
    m, k = x.shape
    n = w.shape[1]
    n_pad = -(-n // LANE) * LANE
    w = w.astype(jnp.bfloat16)
    if n_pad != n:
        w = jnp.pad(w, ((0, 0), (0, n_pad - n)))
    tm = min(m, 512)
    out = pl.pallas_call(
        _mm_body,
        grid=(m // tm,),
        in_specs=[pl.BlockSpec((tm, k), lambda i: (i, 0)),
                  pl.BlockSpec((k, n_pad), lambda i: (0, 0))],
        out_specs=pl.BlockSpec((tm, n_pad), lambda i: (i, 0)),
        out_shape=jax.ShapeDtypeStruct((m, n_pad), jnp.float32),
        compiler_params=pltpu.CompilerParams(dimension_semantics=("arbitrary",),
                                             vmem_limit_bytes=48 * 1024 * 1024),
        name="mm",
    )(x, w)
    return out if keep_pad or n_pad == n else out[:, :n]


def _mm3(x, w):
    b, t, d = x.shape
    return _mm(x.reshape(b * t, d), w).reshape(b, t, -1)


def _split_cols(h, sizes):
    return jnp.split(h, np.cumsum(sizes)[:-1].tolist(), axis=-1)


def _layer_norm(x, g, b):
    mu = x.mean(-1, keepdims=True)
    var = jnp.square(x - mu).mean(-1, keepdims=True)
    return (x - mu) * lax.rsqrt(var + LN_EPS) * g + b


def _rms_norm(x, g):
    return x * lax.rsqrt(jnp.mean(x * x, -1, keepdims=True) + LN_EPS) * g


def _partial_rope(x, pos):
    half = ROPE_DIM // 2
    inv_freq = jnp.power(ROPE_THETA, -jnp.arange(half, dtype=jnp.float32) / half)
    ang = pos.astype(jnp.float32)[:, None] * inv_freq
    ang = ang.reshape(ang.shape[0], *([1] * (x.ndim - 3)), half)
    cos, sin = jnp.cos(ang), jnp.sin(ang)
    x1 = x[..., :half]
    x2 = x[..., half:ROPE_DIM]
    rot = jnp.concatenate([x1 * cos - x2 * sin, x2 * cos + x1 * sin], -1)
    return jnp.concatenate([rot, x[..., ROPE_DIM:]], -1)


def _masked_softmax(s, mask):
    s = jnp.where(mask, s, -jnp.inf)
    m = jnp.max(s, axis=-1, keepdims=True)
    m = jnp.where(jnp.isfinite(m), m, 0.0)
    p = jnp.exp(s - m)
    return p / jnp.maximum(p.sum(-1, keepdims=True), 1e-30)


def _gla_recurrence(q, k, v, log_a, s0):
    bsz, t_, nh, _ = q.shape
    c = math.gcd(t_, GLA_CHUNK)
    n = t_ // c

    def chunks(a):
        return jnp.moveaxis(a.reshape(bsz, n, c, *a.shape[2:]), 1, 0)

    causal = jnp.tril(jnp.ones((c, c), dtype=bool))[None, :, :, None, None]

    def step(s, inp):
        qc, kc, vc, lc = inp
        bc = jnp.cumsum(lc, axis=1)
        decay = jnp.exp(jnp.where(causal, bc[:, :, None] - bc[:, None, :], -jnp.inf))
        attn = jnp.einsum('bijhd,bjhd->bhij', qc[:, :, None] * decay, kc)
        o = jnp.einsum('bhij,bjhe->bihe', attn, vc) + jnp.einsum('bihd,bhde->bihe', qc * jnp.exp(bc), s)
        bl = bc[:, -1]
        s = jnp.exp(bl)[..., None] * s + jnp.einsum('bjhd,bjhe->bhde', kc * jnp.exp(bl[:, None] - bc), vc)
        return s, o

    s_fin, o = lax.scan(step, s0, (chunks(q), chunks(k), chunks(v), chunks(log_a)))
    return jnp.moveaxis(o, 0, 1).reshape(bsz, t_, nh, -1), s_fin


def _compress(k, v, w_pool):
    bsz, length = k.shape[:2]
    n_sub = length // CMP_STRIDE

    def pool(a, w):
        sub = a[:, :n_sub * CMP_STRIDE].reshape(bsz, n_sub, CMP_STRIDE, *a.shape[2:])
        first = jnp.einsum('bnjhd,j->bnhd', sub, w[:CMP_STRIDE])
        second = jnp.einsum('bnjhd,j->bnhd', sub, w[CMP_STRIDE:])
        return first[:, :-1] + second[:, 1:]

    cend = jnp.arange(n_sub - 1) * CMP_STRIDE + CMP_BLK - 1
    return pool(k, w_pool[0]), pool(v, w_pool[1]), cend


def _to_sel_blocks(a, n_sel):
    bsz, length = a.shape[:2]
    a = jnp.pad(a, ((0, 0), (0, n_sel * SEL_BLK - length), (0, 0), (0, 0)))
    return a.reshape(bsz, n_sel, SEL_BLK, NSA_KV_HEADS, HEAD_DIM).transpose(0, 3, 1, 2, 4)


def _nsa_attend(q_raw, q_rot, qpos, gates, kc, vc, cend, ksb, vsb, kw, vw, kwpos):
    scale = HEAD_DIM ** -0.5
    bsz, tq = q_raw.shape[:2]
    n_cmp, n_sel = kc.shape[1], ksb.shape[2]
    s_c = jnp.einsum('bqhgd,bnhd->bhgqn', q_raw, kc) * scale
    p_c = _masked_softmax(s_c, cend[None, :] <= qpos[:, None])
    o_c = jnp.einsum('bhgqn,bnhd->bqhgd', p_c, vc)
    ratio = SEL_BLK // CMP_STRIDE
    imp = p_c.sum(axis=2)
    imp = jnp.pad(imp, ((0, 0), (0, 0), (0, 0), (1, ratio * (n_sel + 1) - 1 - n_cmp)))
    imp = imp.reshape(bsz, NSA_KV_HEADS, tq, n_sel + 1, ratio)
    imp_s = imp[..., :n_sel, :].sum(-1) + imp[..., 1:, 0]
    blk = jnp.arange(n_sel)[None, :]
    cur = (qpos // SEL_BLK)[:, None]
    valid = blk * SEL_BLK <= qpos[:, None]
    forced = (blk == 0) | (blk == cur) | (blk == cur - 1)
    score = jnp.where(valid, imp_s + jnp.where(forced, FORCE_BONUS, 0.0), -jnp.inf)
    k_top = min(SEL_TOPN, n_sel)
    _, sel = lax.top_k(score, k_top)
    take = jax.vmap(jax.vmap(lambda blocks, idx: blocks[idx]))
    ks = take(ksb, sel).reshape(bsz, NSA_KV_HEADS, tq, k_top * SEL_BLK, HEAD_DIM)
    vs = take(vsb, sel).reshape(bsz, NSA_KV_HEADS, tq, k_top * SEL_BLK, HEAD_DIM)
    kpos = (sel[..., None] * SEL_BLK + jnp.arange(SEL_BLK)).reshape(bsz, NSA_KV_HEADS, tq, k_top * SEL_BLK)
    s_s = jnp.einsum('bqhgd,bhqkd->bhgqk', q_rot, ks) * scale
    p_s = _masked_softmax(s_s, (kpos <= qpos[:, None])[:, :, None])
    o_s = jnp.einsum('bhgqk,bhqkd->bqhgd', p_s, vs)
    s_w = jnp.einsum('bqhgd,bkhd->bhgqk', q_rot, kw) * scale
    kp, qp = kwpos[None, :], qpos[:, None]
    p_w = _masked_softmax(s_w, (kp <= qp) & (kp > qp - WINDOW) & (kp >= 0))
    o_w = jnp.einsum('bhgqk,bkhd->bqhgd', p_w, vw)
    return gates[..., 0:1] * o_c + gates[..., 1:2] * o_s + gates[..., 2:3] * o_w


NSA_ROWS = NSA_GROUP * Q_BLK
SEL_KT = 512
N_SELB = 128
MASKED = -1e9
WIN_KEYS = WINDOW + Q_BLK
KK_W = 2 * HEAD_DIM + N_SELB


def _nsa_prompt_body(qr_ref, qo_ref, kc_ref, vct_ref, kk_ref, vvt_ref, g_ref, o_ref,
                     imp_ref, m_ref, l_ref, acc_ref):
    f32, bf16 = jnp.float32, jnp.bfloat16
    qb = pl.program_id(2)
    q0 = qb * Q_BLK
    qr_t = qr_ref[0, 0, 0]
    qo_t = qo_ref[0, 0, 0]
    n_cmp = kc_ref.shape[2]

    s_c = _dot(kc_ref[0, 0], qr_t)
    n_idx = lax.broadcasted_iota(jnp.int32, (n_cmp, NSA_ROWS), 0)
    qpos_c = q0 + (lax.broadcasted_iota(jnp.int32, (n_cmp, NSA_ROWS), 1) & (Q_BLK - 1))
    cmask = (n_idx * CMP_STRIDE + (CMP_BLK - 1)) <= qpos_c
    s_c = jnp.where(cmask, s_c, MASKED)
    m_c = jnp.max(s_c, axis=0, keepdims=True)
    p_c = jnp.where(cmask, jnp.exp(s_c - m_c), 0.0)
    p_c = p_c / jnp.maximum(jnp.sum(p_c, axis=0, keepdims=True), 1e-30)
    o_ct = _dot(vct_ref[0, 0], p_c.astype(bf16))

    imp = (p_c[:, 0:Q_BLK] + p_c[:, Q_BLK:2 * Q_BLK]) + p_c[:, 2 * Q_BLK:3 * Q_BLK] + p_c[:, 3 * Q_BLK:]
    imp_ref[0:8, :] = jnp.zeros((8, Q_BLK), f32)
    imp_ref[8:8 + n_cmp, :] = imp
    ratio = SEL_BLK // CMP_STRIDE
    n_selb = n_cmp // ratio
    imp_s = imp_ref[pl.ds(7, n_selb, stride=ratio), :]
    for r in range(ratio):
        imp_s = imp_s + imp_ref[pl.ds(8 + r, n_selb, stride=ratio), :]
    blk = lax.broadcasted_iota(jnp.int32, (n_selb, Q_BLK), 0)
    qpos_s = q0 + lax.broadcasted_iota(jnp.int32, (n_selb, Q_BLK), 1)
    cur = lax.shift_right_logical(qpos_s, int(math.log2(SEL_BLK)))
    valid = blk * SEL_BLK <= qpos_s
    forced = (blk == 0) | (blk == cur) | (blk == cur - 1)
    score = jnp.where(valid, imp_s + jnp.where(forced, FORCE_BONUS, 0.0), -1e30)
    picked = jnp.zeros((n_selb, Q_BLK), f32)
    for _ in range(SEL_TOPN):
        best = jnp.max(score, axis=0, keepdims=True)
        first = jnp.min(jnp.where(score == best, blk, n_selb), axis=0, keepdims=True)
        hit = blk == first
        picked = jnp.where(hit, 1.0, picked)
        score = jnp.where(hit, -3e38, score)
    selb_t = jnp.where(valid, picked, 0.0)
    if n_selb < N_SELB:
        selb_t = jnp.concatenate([selb_t, jnp.zeros((N_SELB - n_selb, Q_BLK), f32)], axis=0)
    selb_t = ((selb_t - 1.0) * (-MASKED)).astype(bf16)
    selb_t = jnp.concatenate([selb_t] * NSA_GROUP, axis=1)

    zeros_q = jnp.zeros((HEAD_DIM, NSA_ROWS), bf16)
    q_sel = jnp.concatenate([qo_t, zeros_q, selb_t], axis=0)
    q_win = jnp.concatenate([zeros_q, qo_t, jnp.zeros((N_SELB, NSA_ROWS), bf16)], axis=0)
    qpos_r = q0 + (lax.broadcasted_iota(jnp.int32, (1, NSA_ROWS), 1) & (Q_BLK - 1))

    def v_tiles(first, count):
        return jnp.concatenate([vvt_ref[0, 0, first + j] for j in range(count)], axis=1)

    m_ref[...] = jnp.full(m_ref.shape, MASKED, f32)
    l_ref[...] = jnp.zeros(l_ref.shape, f32)
    acc_ref[...] = jnp.zeros(acc_ref.shape, f32)

    def sel_tile(k0, kt, causal):
        s = _dot(kk_ref[0, 0, pl.ds(k0, kt), :], q_sel)
        if causal:
            kpos = k0 + lax.broadcasted_iota(jnp.int32, (kt, NSA_ROWS), 0)
            s = jnp.where(kpos <= qpos_r, s, MASKED)
        m_old = m_ref[...]
        m_new = jnp.maximum(m_old, jnp.max(s, axis=0, keepdims=True))
        alpha = jnp.exp(m_old - m_new)
        p = jnp.exp(s - m_new)
        l_ref[...] = alpha * l_ref[...] + jnp.sum(p, axis=0, keepdims=True)
        vt = v_tiles(k0 // Q_BLK, kt // Q_BLK)
        acc_ref[...] = alpha * acc_ref[...] + _dot(vt, p.astype(bf16))
        m_ref[...] = m_new

    n_full = q0 // SEL_KT

    def full_step(t, c):
        sel_tile(pl.multiple_of(t * SEL_KT, SEL_KT), SEL_KT, False)
        return c

    lax.fori_loop(0, n_full, full_step, 0)
    d0 = n_full * SEL_KT

    def diag_step(t, c):
        sel_tile(pl.multiple_of(d0 + t * Q_BLK, Q_BLK), Q_BLK, False)
        return c

    lax.fori_loop(0, (q0 - d0) // Q_BLK, diag_step, 0)
    sel_tile(pl.multiple_of(q0, Q_BLK), Q_BLK, True)
    o_st = acc_ref[0:HEAD_DIM, :] / l_ref[...]

    w0 = pl.multiple_of(jnp.maximum(q0 - WINDOW, 0), Q_BLK)
    s_w = _dot(kk_ref[0, 0, pl.ds(w0, WIN_KEYS), :], q_win)
    kpos_w = w0 + lax.broadcasted_iota(jnp.int32, (WIN_KEYS, NSA_ROWS), 0)
    s_w = jnp.where((kpos_w <= qpos_r) & (kpos_w > qpos_r - WINDOW), s_w, MASKED)
    p_w = jnp.exp(s_w - jnp.max(s_w, axis=0, keepdims=True))
    l_w = jnp.sum(p_w, axis=0, keepdims=True)
    acc_w = _dot(v_tiles(w0 // Q_BLK, WIN_KEYS // Q_BLK), p_w.astype(bf16))
    o_wt = acc_w[HEAD_DIM:2 * HEAD_DIM, :] / l_w

    g = g_ref[0, 0, 0]
    out_t = g[0:1, :] * o_ct + g[1:2, :] * o_st + g[2:3, :] * o_wt
    o_ref[0] = jnp.concatenate([out_t[:, g_ * Q_BLK:(g_ + 1) * Q_BLK] for g_ in range(NSA_GROUP)], axis=0).T


def _nsa_prompt(qr, qo, gt, kc_p, vct, kk, vvt):
    bsz, _, nqb = qr.shape[:3]
    t_ = nqb * Q_BLK
    n_cmp = kc_p.shape[2]
    per_blk = lambda b, h, i: (b, h, i, 0, 0)
    per_head = lambda b, h, i: (b, h, 0, 0)
    return pl.pallas_call(
        _nsa_prompt_body,
        grid=(bsz, NSA_KV_HEADS, nqb),
        in_specs=[pl.BlockSpec((1, 1, 1, HEAD_DIM, NSA_ROWS), per_blk),
                  pl.BlockSpec((1, 1, 1, HEAD_DIM, NSA_ROWS), per_blk),
                  pl.BlockSpec((1, 1, n_cmp, HEAD_DIM), per_head),
                  pl.BlockSpec((1, 1, HEAD_DIM, n_cmp), per_head),
                  pl.BlockSpec((1, 1, t_, KK_W), per_head),
                  pl.BlockSpec((1, 1, nqb, 2 * HEAD_DIM, Q_BLK), lambda b, h, i: (b, h, 0, 0, 0)),
                  pl.BlockSpec((1, 1, 1, 3, NSA_ROWS), per_blk)],
        out_specs=pl.BlockSpec((1, Q_BLK, NSA_GROUP * HEAD_DIM), lambda b, h, i: (b, i, h)),
        out_shape=jax.ShapeDtypeStruct((bsz, t_, NSA_HEADS * HEAD_DIM), jnp.float32),
        scratch_shapes=[pltpu.VMEM((8 + n_cmp, Q_BLK), jnp.float32),
                        pltpu.VMEM((1, NSA_ROWS), jnp.float32),
                        pltpu.VMEM((1, NSA_ROWS), jnp.float32),
                        pltpu.VMEM((2 * HEAD_DIM, NSA_ROWS), jnp.float32)],
        compiler_params=pltpu.CompilerParams(
            dimension_semantics=("arbitrary", "arbitrary", "arbitrary"),
            vmem_limit_bytes=48 * 1024 * 1024),
        name="nsa_prompt",
    )(qr, qo, kc_p, vct, kk, vvt, gt)


GLA_SUB = 16
GLA_QK = GLA_HEADS * GLA_DK
GLA_V = GLA_HEADS * GLA_DV


def _dot_tn(a, b):
    return lax.dot_general(a, b, (((0,), (0,)), ((), ())), preferred_element_type=jnp.float32)


def _gla_body(q_ref, k_ref, v_ref, gr_ref, glr_ref, wg_ref, bg_ref, ng_ref, s0_ref, exp_ref, bd_ref,
              o_ref, sfin_ref, st_ref, b_ref, qd_ref, *, t_valid):
    f32, bf16 = jnp.float32, jnp.bfloat16
    tt = q_ref.shape[1]
    ti = pl.program_id(1)

    @pl.when(ti == 0)
    def _():
        st_ref[...] = s0_ref[0]

    row = lax.broadcasted_iota(jnp.int32, (tt, 1), 0)
    z = _dot(glr_ref[0][:, :GLA_RANK].astype(bf16), wg_ref[...]) + bg_ref[...]
    la = (jnp.minimum(z, 0.0) - jnp.log1p(jnp.exp(-jnp.abs(z)))) * (1.0 / GLA_TAU)
    la = jnp.where(ti * tt + row < t_valid, la, 0.0)
    seg = row & (GLA_SUB - 1)
    b = la
    for s in (1, 2, 4, 8):
        b = b + jnp.where(seg >= s, pltpu.roll(b, s, axis=0), 0.0)
    q = q_ref[0] * (GLA_DK ** -0.5)
    k = k_ref[0]
    v = v_ref[0]
    o = _dot((q * k).astype(bf16), exp_ref[...]) * v
    for d in range(1, GLA_SUB):
        decay = jnp.exp(jnp.minimum(b - pltpu.roll(b, d, axis=0), 0.0))
        w = jnp.where(seg >= d, q * pltpu.roll(k, d, axis=0) * decay, 0.0)
        o = o + _dot(w.astype(bf16), exp_ref[...]) * pltpu.roll(v, d, axis=0)
    o_ref[0] = o
    b_ref[...] = b
    qd_ref[...] = (q * jnp.exp(b)).astype(bf16)

    def block_step(c, carry):
        rows = pl.ds(pl.multiple_of(c * GLA_SUB, GLA_SUB), GLA_SUB)
        st = st_ref[...]
        o_ref[0, rows, :] += _dot_nt(qd_ref[rows, :], st.astype(bf16))
        bc = b_ref[rows, :]
        bl = bc[GLA_SUB - 1:GLA_SUB, :]
        kc = (k_ref[0, rows, :] * jnp.exp(bl - bc)).astype(bf16)
        upd = _dot_tn(v_ref[0, rows, :].astype(bf16), kc)
        st_ref[...] = st * jnp.exp(bl) + upd * bd_ref[...]
        return carry

    lax.fori_loop(0, tt // GLA_SUB, block_step, 0)
    sfin_ref[0] = st_ref[...]
    gr = gr_ref[0]
    gate = gr * jax.nn.sigmoid(gr)
    for h in range(GLA_HEADS):
        cols = slice(h * GLA_DV, (h + 1) * GLA_DV)
        oh = o_ref[0, :, cols]
        ms = jnp.mean(oh * oh, axis=-1, keepdims=True)
        o_ref[0, :, cols] = oh * lax.rsqrt(ms + LN_EPS) * ng_ref[...] * gate[:, cols]


def _gla(h, w_gla_gate, b_gla_gate, gla_norm_g, gla_state):
    bsz, t_, n_in = h.shape
    tp = -(-t_ // GLA_SUB) * GLA_SUB
    if tp != t_:
        h = jnp.pad(h, ((0, 0), (0, tp - t_), (0, 0)))
    tt = min(tp, 256)
    heads = np.arange(GLA_HEADS)
    expand = np.repeat(np.repeat(np.eye(GLA_HEADS, dtype=np.float32), GLA_DK, 0), GLA_DV, 1)
    bdmask = jnp.asarray(expand.T)
    if gla_state is None:
        s0 = jnp.zeros((bsz, GLA_V, GLA_QK), jnp.float32)
    else:
        s0 = jnp.zeros((bsz, GLA_HEADS, GLA_DV, GLA_HEADS, GLA_DK), jnp.float32)
        s0 = s0.at[:, heads, :, heads, :].set(gla_state.transpose(1, 0, 3, 2)).reshape(bsz, GLA_V, GLA_QK)
    tile = lambda width, blk: pl.BlockSpec((1, tt, width), lambda b, i: (b, i, blk))
    fixed2 = lambda shape: pl.BlockSpec(shape, lambda b, i: (0, 0))
    per_b = pl.BlockSpec((1, GLA_V, GLA_QK), lambda b, i: (b, 0, 0))
    o, s_t = pl.pallas_call(
        functools.partial(_gla_body, t_valid=t_),
        grid=(bsz, tp // tt),
        in_specs=[tile(GLA_QK, 0), tile(GLA_QK, 1), tile(GLA_V, 1), tile(GLA_V, 2),
                  tile(LANE, (2 * GLA_QK + 2 * GLA_V + NSA_SIZES[0] + NSA_SIZES[1]) // LANE),
                  fixed2((GLA_RANK, GLA_QK)), fixed2((1, GLA_QK)), fixed2((1, GLA_DV)), per_b,
                  fixed2((GLA_QK, GLA_V)), fixed2((GLA_V, GLA_QK))],
        out_specs=[pl.BlockSpec((1, tt, GLA_V), lambda b, i: (b, i, 0)), per_b],
        out_shape=[jax.ShapeDtypeStruct((bsz, tp, GLA_V), jnp.float32),
                   jax.ShapeDtypeStruct((bsz, GLA_V, GLA_QK), jnp.float32)],
        scratch_shapes=[pltpu.VMEM((GLA_V, GLA_QK), jnp.float32), pltpu.VMEM((tt, GLA_QK), jnp.float32),
                        pltpu.VMEM((tt, GLA_QK), jnp.bfloat16)],
        compiler_params=pltpu.CompilerParams(dimension_semantics=("arbitrary", "arbitrary"),
                                             vmem_limit_bytes=48 * 1024 * 1024),
        name="gla",
    )(h, h, h, h, h, w_gla_gate.astype(jnp.bfloat16), b_gla_gate.reshape(1, GLA_QK),
      gla_norm_g.reshape(1, GLA_DV), s0, jnp.asarray(expand, jnp.bfloat16), bdmask)
    s_new = s_t.reshape(bsz, GLA_HEADS, GLA_DV, GLA_HEADS, GLA_DK)[:, heads, :, heads, :]
    return o[:, :t_], s_new.transpose(1, 0, 3, 2)


COL_NQ = 2 * GLA_QK + 2 * GLA_V
COL_NKV = COL_NQ + NSA_SIZES[0]
COL_TAIL = COL_NKV + NSA_SIZES[1]
TAIL_GATE = GLA_RANK
_ORIG = np.cumsum((0,) + GLA_SIZES + NSA_SIZES)
IN_AB_PERM = np.concatenate([np.arange(_ORIG[0], _ORIG[4]), np.arange(_ORIG[5], _ORIG[7]),
                             np.arange(_ORIG[4], _ORIG[5]), np.arange(_ORIG[7], _ORIG[8])])
SUBS = Q_BLK // CMP_STRIDE


def _nsa_prep_body(nq_ref, kv0_ref, kv1_ref, kv2_ref, tail_ref, rc_ref, ru_ref, rd_ref, pool_ref,
                   rows_ref, win_ref, kk_ref, vvt_ref, qr_ref, qo_ref, g_ref, pooled_ref):
    bf16 = jnp.bfloat16
    q0 = pl.program_id(1) * Q_BLK
    kv_w = NSA_KV_HEADS * HEAD_DIM

    def rope(x):
        reps = x.shape[1] // LANE
        wide = lambda r: jnp.concatenate([r[...]] * reps, axis=1) if reps > 1 else r[...]
        half = ROPE_DIM // 2
        return (x * wide(rc_ref) + pltpu.roll(x, half, axis=1) * wide(ru_ref)
                + pltpu.roll(x, x.shape[1] - half, axis=1) * wide(rd_ref))

    kv0, kv1, kv2 = kv0_ref[0], kv1_ref[0], kv2_ref[0]
    k_sel, v_sel = rope(kv1[:, :kv_w]), kv1[:, kv_w:]
    k_win, v_win = rope(kv2[:, :kv_w]), kv2[:, kv_w:]
    rows_ref[0] = jnp.concatenate([kv0, k_sel, v_sel], axis=1)
    win_ref[0] = jnp.concatenate([k_win, v_win], axis=1)
    blk_id = lax.shift_right_logical(q0 + lax.broadcasted_iota(jnp.int32, (Q_BLK, N_SELB), 0),
                                     int(math.log2(SEL_BLK)))
    onehot = jnp.where(lax.broadcasted_iota(jnp.int32, (Q_BLK, N_SELB), 1) == blk_id, 1.0, 0.0).astype(bf16)
    q = nq_ref[0] * (HEAD_DIM ** -0.5)
    q_rot = rope(q)
    gates_t = jax.nn.sigmoid(tail_ref[0]).T
    for h in range(NSA_KV_HEADS):
        hs = slice(h * HEAD_DIM, (h + 1) * HEAD_DIM)
        kk_ref[0, h] = jnp.concatenate([k_sel[:, hs].astype(bf16), k_win[:, hs].astype(bf16), onehot], axis=1)
        vvt_ref[0, h, 0] = jnp.concatenate([v_sel[:, hs], v_win[:, hs]], axis=1).T.astype(bf16)
        gw = NSA_GROUP * HEAD_DIM
        for src, dst in ((q, qr_ref), (q_rot, qo_ref)):
            t = src[:, h * gw:(h + 1) * gw].T
            dst[0, h, 0] = jnp.concatenate([t[g * HEAD_DIM:(g + 1) * HEAD_DIM] for g in range(NSA_GROUP)],
                                           axis=1).astype(bf16)
        base = TAIL_GATE + h * NSA_GROUP * 3
        g_ref[0, h, 0] = jnp.concatenate(
            [jnp.concatenate([gates_t[base + 3 * g + c:base + 3 * g + c + 1] for g in range(NSA_GROUP)], axis=1)
             for c in range(3)], axis=0)
    kc_in, vc_in = kv0[:, :kv_w].astype(bf16), kv0[:, kv_w:].astype(bf16)
    pooled_ref[0] = jnp.concatenate([_dot(pool_ref[0], kc_in), _dot(pool_ref[1], kc_in),
                                     _dot(pool_ref[2], vc_in), _dot(pool_ref[3], vc_in)], axis=1)


def _nsa_prep(h, pos, w_cmp_pool):
    bsz, t_, _ = h.shape
    nqb = t_ // Q_BLK
    bf16 = jnp.bfloat16
    half = ROPE_DIM // 2
    inv_freq = jnp.power(ROPE_THETA, -jnp.arange(half, dtype=jnp.float32) / half)
    ang = pos.astype(jnp.float32)[:, None] * inv_freq
    cos, sin = jnp.cos(ang), jnp.sin(ang)
    rest = HEAD_DIM - ROPE_DIM
    z8, zr = jnp.zeros((t_, half), jnp.float32), jnp.zeros((t_, rest), jnp.float32)
    two = lambda a: jnp.concatenate([a, a], axis=1)
    rc = two(jnp.concatenate([cos, cos, jnp.ones((t_, rest), jnp.float32)], axis=1))
    ru = two(jnp.concatenate([z8, sin, zr], axis=1))
    rd = two(jnp.concatenate([-sin, z8, zr], axis=1))
    sub = np.arange(Q_BLK) // CMP_STRIDE == np.arange(SUBS)[:, None]
    w_rep = jnp.tile(w_cmp_pool.reshape(2, 2, CMP_STRIDE), (1, 1, SUBS))
    pool = jnp.where(sub[None, None], w_rep[:, :, None, :], 0.0).reshape(4, SUBS, Q_BLK).astype(bf16)
    kv_w = NSA_KV_HEADS * HEAD_DIM
    col = lambda width, off: pl.BlockSpec((1, Q_BLK, width), lambda b, i: (b, i, off // width))
    rows_t = pl.BlockSpec((Q_BLK, LANE), lambda b, i: (i, 0))
    head4 = lambda r, c: pl.BlockSpec((1, NSA_KV_HEADS, 1, r, c), lambda b, i: (b, 0, i, 0, 0))
    return pl.pallas_call(
        _nsa_prep_body,
        grid=(bsz, nqb),
        in_specs=[col(NSA_SIZES[0], COL_NQ), col(2 * kv_w, COL_NKV), col(2 * kv_w, COL_NKV + 2 * kv_w),
                  col(2 * kv_w, COL_NKV + 4 * kv_w), col(LANE, COL_TAIL), rows_t, rows_t, rows_t,
                  pl.BlockSpec((4, SUBS, Q_BLK), lambda b, i: (0, 0, 0))],
        out_specs=[pl.BlockSpec((1, Q_BLK, 4 * kv_w), lambda b, i: (b, i, 0)),
                   pl.BlockSpec((1, Q_BLK, 2 * kv_w), lambda b, i: (b, i, 0)),
                   pl.BlockSpec((1, NSA_KV_HEADS, Q_BLK, KK_W), lambda b, i: (b, 0, i, 0)),
                   head4(2 * HEAD_DIM, Q_BLK), head4(HEAD_DIM, NSA_ROWS), head4(HEAD_DIM, NSA_ROWS),
                   head4(3, NSA_ROWS),
                   pl.BlockSpec((1, SUBS, 4 * kv_w), lambda b, i: (b, i, 0))],
        out_shape=[jax.ShapeDtypeStruct((bsz, t_, 4 * kv_w), jnp.float32),
                   jax.ShapeDtypeStruct((bsz, t_, 2 * kv_w), jnp.float32),
                   jax.ShapeDtypeStruct((bsz, NSA_KV_HEADS, t_, KK_W), bf16),
                   jax.ShapeDtypeStruct((bsz, NSA_KV_HEADS, nqb, 2 * HEAD_DIM, Q_BLK), bf16),
                   jax.ShapeDtypeStruct((bsz, NSA_KV_HEADS, nqb, HEAD_DIM, NSA_ROWS), bf16),
                   jax.ShapeDtypeStruct((bsz, NSA_KV_HEADS, nqb, HEAD_DIM, NSA_ROWS), bf16),
                   jax.ShapeDtypeStruct((bsz, NSA_KV_HEADS, nqb, 3, NSA_ROWS), jnp.float32),
                   jax.ShapeDtypeStruct((bsz, t_ // CMP_STRIDE, 4 * kv_w), jnp.float32)],
        compiler_params=pltpu.CompilerParams(dimension_semantics=("arbitrary", "arbitrary")),
        name="nsa_prep",
    )(h, h, h, h, h, rc, ru, rd, pool)


def _ab_mixer(x, pos, w_in, w_gla_gate, b_gla_gate, gla_norm_g, w_cmp_pool, w_out,
              gla_state, nsa_cache, page_table, win_buf):
    bsz, t_, _ = x.shape
    h_in = _mm(x.reshape(bsz * t_, -1), w_in[:, IN_AB_PERM], keep_pad=True).reshape(bsz, t_, -1)
    o_a, s_a = _gla(h_in, w_gla_gate, b_gla_gate, gla_norm_g, gla_state)
    kv_w = NSA_KV_HEADS * HEAD_DIM
    if nsa_cache is None:
        rows2, win2, kk, vvt, qr, qo, gt, pooled = _nsa_prep(h_in, pos, w_cmp_pool)
        pooled = pooled.reshape(bsz, t_ // CMP_STRIDE, 4, NSA_KV_HEADS, HEAD_DIM)
        kc = pooled[:, :-1, 0] + pooled[:, 1:, 1]
        vc = pooled[:, :-1, 2] + pooled[:, 1:, 3]
        kc_p = jnp.pad(kc, ((0, 0), (0, 1), (0, 0), (0, 0))).transpose(0, 2, 1, 3).astype(jnp.bfloat16)
        vct = jnp.pad(vc, ((0, 0), (0, 1), (0, 0), (0, 0))).transpose(0, 2, 3, 1).astype(jnp.bfloat16)
        o_b = _nsa_prompt(qr, qo, gt, kc_p, vct, kk, vvt)
        rows_full = rows2.reshape(bsz, t_, 4, NSA_KV_HEADS, HEAD_DIM)
        new_win = win2[:, -min(WINDOW, t_):].reshape(bsz, -1, 2, NSA_KV_HEADS, HEAD_DIM)
    else:
        nq = h_in[..., COL_NQ:COL_NKV]
        nkv = h_in[..., COL_NKV:COL_TAIL]
        ngate = h_in[..., COL_TAIL + TAIL_GATE:COL_TAIL + TAIL_GATE + NSA_SIZES[2]]
        q_raw = nq.reshape(bsz, t_, NSA_KV_HEADS, NSA_GROUP, HEAD_DIM)
        q_rot = _partial_rope(q_raw, pos)
        kv = nkv.reshape(bsz, t_, 6, NSA_KV_HEADS, HEAD_DIM)
        k_sel = _partial_rope(kv[:, :, 2], pos)
        k_win = _partial_rope(kv[:, :, 4], pos)
        rows_full = jnp.stack([kv[:, :, 0], kv[:, :, 1], k_sel, kv[:, :, 3]], axis=2)
        rows_win = jnp.stack([k_win, kv[:, :, 5]], axis=2)
        gates = jax.nn.sigmoid(ngate).reshape(bsz, t_, NSA_KV_HEADS, NSA_GROUP, 3)
        past = nsa_cache[page_table].reshape(bsz, -1, 4, NSA_KV_HEADS, HEAD_DIM)
        keys = jnp.concatenate([past, rows_full], axis=1)
        length = keys.shape[1]
        kc, vc, cend = _compress(keys[:, :, 0], keys[:, :, 1], w_cmp_pool)
        n_sel = -(-length // SEL_BLK)
        ksb = _to_sel_blocks(keys[:, :, 2], n_sel)
        vsb = _to_sel_blocks(keys[:, :, 3], n_sel)
        w_buf = win_buf.shape[1]
        kw = jnp.concatenate([win_buf, rows_win], axis=1)
        kwpos = (length - t_) - w_buf + jnp.arange(kw.shape[1])
        o_b = _nsa_attend(q_raw, q_rot, pos, gates, kc, vc, cend, ksb, vsb, kw[:, :, 0], kw[:, :, 1], kwpos)
        o_b = o_b.reshape(bsz, t_, -1)
        new_win = kw[:, -w_buf:]
    y = _mm3(jnp.concatenate([o_a, o_b], axis=-1), w_out)
    return y, s_a, rows_full, new_win


def _conv_module(x, conv_buf, w_pw1, b_pw1, w_dw, b_dw, ln_g, ln_b, w_pw2, b_pw2):
    bsz = x.shape[0]
    a, g = jnp.split(_mm3(x, w_pw1) + b_pw1, 2, axis=-1)
    u = a * jax.nn.sigmoid(g)
    if conv_buf is None:
        conv_buf = jnp.zeros((bsz, CONV_W - 1, D_CONV), u.dtype)
    ext = jnp.concatenate([conv_buf, u], axis=1)
    c = lax.conv_general_dilated(ext, w_dw[:, None, :], (1,), 'VALID',
                                 dimension_numbers=('NWC', 'WIO', 'NWC'),
                                 feature_group_count=D_CONV) + b_dw
    c = jax.nn.silu(_layer_norm(c, ln_g, ln_b))
    return _mm3(c, w_pw2) + b_pw2, ext[:, -(CONV_W - 1):]


PACK_W = 256
SC_WINDOW = 128
SC_TILES = 32


def _pack_rows(y):
    out = []
    for h in range(2):
        lo = lax.bitcast_convert_type(y[:, 2 * h * PACK_W:(2 * h + 1) * PACK_W].astype(jnp.bfloat16)
                                      .astype(jnp.float32), jnp.uint32)
        hi = lax.bitcast_convert_type(y[:, (2 * h + 1) * PACK_W:(2 * h + 2) * PACK_W].astype(jnp.bfloat16)
                                      .astype(jnp.float32), jnp.uint32)
        out.append(lax.bitcast_convert_type((lo >> 16) | hi, jnp.int32))
    return out


def _unpack_words(w):
    u = lax.bitcast_convert_type(w, jnp.uint32)
    lo = lax.bitcast_convert_type(u << 16, jnp.float32)
    hi = lax.bitcast_convert_type(u & jnp.uint32(0xFFFF0000), jnp.float32)
    return lo, hi


def _gather_rows(src, idx):
    n = idx.shape[0]
    if n % (SC_WINDOW * SC_TILES) != 0:
        return jnp.take(src, idx, axis=0)
    mesh = plsc.VectorSubcoreMesh(core_axis_name="core", subcore_axis_name="subcore")

    @pl.kernel(out_type=jax.ShapeDtypeStruct((n, src.shape[1]), src.dtype), mesh=mesh)
    def gather_kernel(src_hbm, idx_hbm, out_hbm):
        def step(idx_vmem, out_vmem):
            pltpu.sync_copy(src_hbm.at[idx_vmem.at[0]], out_vmem)

        pltpu.emit_pipeline(
            step, grid=(n // SC_WINDOW,),
            in_specs=[pl.BlockSpec((1, SC_WINDOW), index_map=lambda i: (0, i))],
            out_specs=[pl.BlockSpec((SC_WINDOW, src.shape[1]), index_map=lambda i: (i, 0))],
            core_axis_name=("core", "subcore"),
            dimension_semantics=(pltpu.PARALLEL,),
        )(idx_hbm, out_hbm)

    return gather_kernel(src, idx.reshape(1, n))


PER_GROUP = N_EXPERTS // N_GROUPS
PICKED = -3e38


def _ln_rows(v, g, b):
    mu = jnp.mean(v, axis=-1, keepdims=True)
    c = v - mu
    var = jnp.mean(c * c, axis=-1, keepdims=True)
    return c * lax.rsqrt(var + LN_EPS) * g + b


def _first_max(v, ids, axes, sentinel):
    best = v
    for a in axes:
        best = jnp.max(best, axis=a, keepdims=True)
    first = jnp.where(v == best, ids, sentinel)
    for a in axes:
        first = jnp.min(first, axis=a, keepdims=True)
    return best, first


def _sum_axes(v, axes):
    for a in axes:
        v = jnp.sum(v, axis=a, keepdims=True)
    return v


def _moe_pre_body(x_ref, mix_ref, g_ref, b_ref, wr_ref, br_ref, wgu_ref, wdn_ref,
                  x1_ref, xp_ref, sh_ref, eidx_ref, gate_ref, rank_ref, cnt_ref, run_ref):
    f32, bf16 = jnp.float32, jnp.bfloat16
    tm = x_ref.shape[0]

    @pl.when(pl.program_id(0) == 0)
    def _():
        run_ref[...] = jnp.zeros(run_ref.shape, f32)

    x1 = _ln_rows(ALPHA * x_ref[...] + mix_ref[...], g_ref[...], b_ref[...])
    x1_ref[...] = x1
    x1b = x1.astype(bf16)
    xp_ref[0], xp_ref[1] = _pack_rows(x1)

    h = _dot(x1b, wgu_ref[...])
    d_sh = h.shape[1] // 2
    act = (jax.nn.silu(h[:, :d_sh]) * h[:, d_sh:]).astype(bf16)
    sh_ref[...] = _dot(act, wdn_ref[...])

    s = jax.nn.sigmoid(_dot_nt(wr_ref[...], x1b)).reshape(N_GROUPS, PER_GROUP, tm)
    sb = s + br_ref[...].reshape(N_GROUPS, PER_GROUP, 1)
    shape3 = (N_GROUPS, PER_GROUP, tm)
    pid = lax.broadcasted_iota(jnp.int32, shape3, 1)
    gid = lax.broadcasted_iota(jnp.int32, (N_GROUPS, 1, tm), 0)
    eid = lax.broadcasted_iota(jnp.int32, shape3, 0) * PER_GROUP + pid
    top1, i1 = _first_max(sb, pid, (1,), PER_GROUP)
    top2 = jnp.max(jnp.where(pid == i1, PICKED, sb), axis=1, keepdims=True)
    gscore = top1 + top2
    gsel = jnp.zeros((N_GROUPS, 1, tm), f32)
    for _ in range(TOPK_GROUPS):
        _, first = _first_max(gscore, gid, (0,), N_GROUPS)
        hit = gid == first
        gsel = jnp.where(hit, 1.0, gsel)
        gscore = jnp.where(hit, PICKED, gscore)
    cand = jnp.where(gsel > 0.0, sb, -1e30)
    firsts, gates = [], []
    picked = jnp.zeros(shape3, f32)
    for _ in range(TOP_K):
        _, first = _first_max(cand, eid, (0, 1), N_EXPERTS)
        hit = eid == first
        firsts.append(first)
        gates.append(_sum_axes(jnp.where(hit, s, 0.0), (0, 1)))
        picked = jnp.where(hit, 1.0, picked)
        cand = jnp.where(hit, PICKED, cand)
    gsum = gates[0]
    for gk in gates[1:]:
        gsum = gsum + gk
    earlier = (lax.broadcasted_iota(jnp.int32, (tm, tm), 0) < lax.broadcasted_iota(jnp.int32, (tm, tm), 1))
    picked2 = picked.reshape(N_EXPERTS, tm)
    rank = run_ref[...] + _dot(picked2.astype(bf16), jnp.where(earlier, 1.0, 0.0).astype(bf16))
    run_new = run_ref[...] + jnp.sum(picked2, axis=1, keepdims=True)
    run_ref[...] = run_new
    cnt_ref[...] = jnp.broadcast_to(run_new, cnt_ref.shape)
    rank3 = rank.reshape(shape3)
    for k in range(TOP_K):
        hit = eid == firsts[k]
        eidx_ref[k:k + 1, :] = firsts[k].reshape(1, tm)
        gate_ref[k:k + 1, :] = (gates[k] / gsum * ROUTE_SCALE).reshape(1, tm)
        rank_ref[k:k + 1, :] = _sum_axes(jnp.where(hit, rank3, 0.0), (0, 1)).reshape(1, tm).astype(jnp.int32)


def _moe_pre(x, mix, g, b, w_router, b_router, w_sh_gu, w_sh_down):
    m, d = x.shape
    bf16 = jnp.bfloat16
    tm = min(m, 512)
    row = lambda i: (i, 0)
    col = lambda i: (0, i)
    fixed = lambda i: (0, 0)
    d_sh2 = w_sh_gu.shape[1]
    return pl.pallas_call(
        _moe_pre_body,
        grid=(m // tm,),
        in_specs=[pl.BlockSpec((tm, d), row), pl.BlockSpec((tm, d), row),
                  pl.BlockSpec((1, d), fixed), pl.BlockSpec((1, d), fixed),
                  pl.BlockSpec((N_EXPERTS, d), fixed), pl.BlockSpec((N_EXPERTS, 1), fixed),
                  pl.BlockSpec((d, d_sh2), fixed), pl.BlockSpec((d_sh2 // 2, d), fixed)],
        out_specs=[pl.BlockSpec((tm, d), row), pl.BlockSpec((2, tm, PACK_W), lambda i: (0, i, 0)),
                   pl.BlockSpec((tm, d), row),
                   pl.BlockSpec((TOP_K, tm), col), pl.BlockSpec((TOP_K, tm), col), pl.BlockSpec((TOP_K, tm), col),
                   pl.BlockSpec((N_EXPERTS, LANE), fixed)],
        out_shape=[jax.ShapeDtypeStruct((m, d), jnp.float32), jax.ShapeDtypeStruct((2, m, PACK_W), jnp.int32),
                   jax.ShapeDtypeStruct((m, d), jnp.float32),
                   jax.ShapeDtypeStruct((TOP_K, m), jnp.int32), jax.ShapeDtypeStruct((TOP_K, m), jnp.float32),
                   jax.ShapeDtypeStruct((TOP_K, m), jnp.int32),
                   jax.ShapeDtypeStruct((N_EXPERTS, LANE), jnp.float32)],
        scratch_shapes=[pltpu.VMEM((N_EXPERTS, 1), jnp.float32)],
        compiler_params=pltpu.CompilerParams(dimension_semantics=("arbitrary",),
                                             vmem_limit_bytes=48 * 1024 * 1024),
        name="moe_pre",
    )(x, mix, g.reshape(1, d), b.reshape(1, d), w_router.T.astype(bf16), b_router.reshape(N_EXPERTS, 1),
      w_sh_gu.astype(bf16), w_sh_down.astype(bf16))


def _moe_expert_body(exp_ref, first_ref, active_ref, xs_ref, wgu_ref, wdn_ref, y_ref, wgu_bf, wdn_bf):
    i = pl.program_id(0)
    bf16 = jnp.bfloat16

    @pl.when(first_ref[i] == 1)
    def _():
        wgu_bf[...] = wgu_ref[0].astype(bf16)
        wdn_bf[...] = wdn_ref[0].astype(bf16)

    @pl.when(active_ref[i] == 1)
    def _():
        h = None
        for hw in range(2):
            for q, xq in enumerate(_unpack_words(xs_ref[hw])):
                r0 = (2 * hw + q) * PACK_W
                part = _dot(xq.astype(bf16), wgu_bf[r0:r0 + PACK_W, :])
                h = part if h is None else h + part
        d_e = h.shape[1] // 2
        act = (jax.nn.silu(h[:, :d_e]) * h[:, d_e:]).astype(bf16)
        y_ref[0], y_ref[1] = _pack_rows(_dot(act, wdn_bf[...]))

    @pl.when(active_ref[i] == 0)
    def _():
        y_ref[...] = jnp.zeros(y_ref.shape, y_ref.dtype)


def _moe_experts(xs, blk_exp, blk_first, blk_active, w_exp_gu, w_exp_down, bm):
    n_slots = xs.shape[1]
    d = w_exp_gu.shape[1]
    n_blk = n_slots // bm
    d_e2 = w_exp_gu.shape[2]
    words = lambda i, e, f, a: (0, i, 0)
    grid_spec = pltpu.PrefetchScalarGridSpec(
        num_scalar_prefetch=3,
        grid=(n_blk,),
        in_specs=[pl.BlockSpec((2, bm, PACK_W), words),
                  pl.BlockSpec((1, d, d_e2), lambda i, e, f, a: (e[i], 0, 0)),
                  pl.BlockSpec((1, d_e2 // 2, d), lambda i, e, f, a: (e[i], 0, 0))],
        out_specs=pl.BlockSpec((2, bm, PACK_W), words),
        scratch_shapes=[pltpu.VMEM((d, d_e2), jnp.bfloat16), pltpu.VMEM((d_e2 // 2, d), jnp.bfloat16)])
    return pl.pallas_call(
        _moe_expert_body,
        grid_spec=grid_spec,
        out_shape=jax.ShapeDtypeStruct((2, n_slots, PACK_W), jnp.int32),
        compiler_params=pltpu.CompilerParams(dimension_semantics=("arbitrary",),
                                             vmem_limit_bytes=48 * 1024 * 1024),
        name="moe_experts",
    )(blk_exp, blk_first, blk_active, xs, w_exp_gu, w_exp_down)


def _combine_ln_body(x_ref, yg_ref, gt_ref, sh_ref, g_ref, b_ref, o_ref):
    gt = gt_ref[...]
    parts = []
    for hw in range(2):
        lo_acc = hi_acc = None
        for k in range(TOP_K):
            lo, hi = _unpack_words(yg_ref[hw, k])
            gk = gt[:, k:k + 1]
            lo_acc = lo * gk if lo_acc is None else lo_acc + lo * gk
            hi_acc = hi * gk if hi_acc is None else hi_acc + hi * gk
        parts += [lo_acc, hi_acc]
    routed = jnp.concatenate(parts, axis=1)
    o_ref[...] = _ln_rows(ALPHA * x_ref[...] + (routed + sh_ref[...]), g_ref[...], b_ref[...])


def _combine_ln(x, yg, gate_t, shared, g, b):
    m, d = x.shape
    tm = min(m, 256)
    row = lambda i: (i, 0)
    fixed = lambda i: (0, 0)
    return pl.pallas_call(
        _combine_ln_body,
        grid=(m // tm,),
        in_specs=[pl.BlockSpec((tm, d), row), pl.BlockSpec((2, TOP_K, tm, PACK_W), lambda i: (0, 0, i, 0)),
                  pl.BlockSpec((tm, TOP_K), row), pl.BlockSpec((tm, d), row),
                  pl.BlockSpec((1, d), fixed), pl.BlockSpec((1, d), fixed)],
        out_specs=pl.BlockSpec((tm, d), row),
        out_shape=jax.ShapeDtypeStruct((m, d), jnp.float32),
        compiler_params=pltpu.CompilerParams(dimension_semantics=("arbitrary",)),
        name="combine_ln",
    )(x, yg, gate_t, shared, g.reshape(1, d), b.reshape(1, d))


def _moe_layer(x, mix, ln1_g, ln1_b, ln2_g, ln2_b, w_router, b_router, w_exp_gu, w_exp_down, w_sh_gu, w_sh_down):
    m, d = x.shape
    x1, xp, shared, eidx, gate8, rank8, counts = _moe_pre(x, mix, ln1_g, ln1_b, w_router, b_router,
                                                           w_sh_gu, w_sh_down)
    bm = 512 if m * TOP_K >= 512 * N_EXPERTS else MOE_BLK
    n_blk = (m * TOP_K) // bm + N_EXPERTS
    counts = counts[:, 0].astype(jnp.int32)
    padded = (counts + bm - 1) // bm * bm
    pad_end = jnp.cumsum(padded)
    pad_start = pad_end - padded
    start_of = jnp.sum(jnp.where(eidx[:, :, None] == jnp.arange(N_EXPERTS), pad_start, 0), axis=-1)
    dest = (start_of + rank8).reshape(-1)
    tok = jnp.tile(jnp.arange(m, dtype=jnp.int32), TOP_K)
    slot_tok = (jnp.arange(n_blk * bm, dtype=jnp.int32) % m).at[dest].set(tok)
    blk_start = jnp.arange(n_blk, dtype=jnp.int32) * bm
    blk_exp = jnp.minimum(jnp.sum(pad_end[None, :] <= blk_start[:, None], axis=1), N_EXPERTS - 1).astype(jnp.int32)
    blk_active = (blk_start < pad_end[-1]).astype(jnp.int32)
    blk_first = jnp.concatenate([jnp.ones((1,), jnp.int32), (blk_exp[1:] != blk_exp[:-1]).astype(jnp.int32)])
    n_slots = n_blk * bm
    xs = _gather_rows(xp.reshape(2 * m, PACK_W), jnp.concatenate([slot_tok, slot_tok + m]))
    y = _moe_experts(xs.reshape(2, n_slots, PACK_W), blk_exp, blk_first, blk_active, w_exp_gu, w_exp_down, bm)
    yg = _gather_rows(y.reshape(2 * n_slots, PACK_W), jnp.concatenate([dest, dest + n_slots]))
    return _combine_ln(x1, yg.reshape(2, TOP_K, m, PACK_W), gate8.T, shared, ln2_g, ln2_b)


def _trunk(x, pos, gla_state, nsa_cache, page_table, win_buf, conv_buf,
           w_in_ab, w_gla_gate, b_gla_gate, gla_norm_g, w_cmp_pool, w_out_ab,
           w_pw1, b_pw1, w_dw, b_dw, conv_ln_g, conv_ln_b, w_pw2, b_pw2,
           ln_g, ln_b, w_router, b_router, w_exp_gu, w_exp_down, w_sh_gu, w_sh_down):
    new_gla, new_rows, new_win, new_conv = [], [], [], []
    for layer in range(DEPTH):
        i = layer // 2
        if layer % 2 == 0:
            mix, s_a, rows, win = _ab_mixer(
                x, pos, w_in_ab[i], w_gla_gate[i], b_gla_gate[i], gla_norm_g[i], w_cmp_pool[i], w_out_ab[i],
                None if gla_state is None else gla_state[i],
                None if nsa_cache is None else nsa_cache[i], page_table,
                None if win_buf is None else win_buf[i])
            new_gla.append(s_a)
            new_rows.append(rows)
            new_win.append(win)
        else:
            mix, cb = _conv_module(x, None if conv_buf is None else conv_buf[i], w_pw1[i], b_pw1[i],
                                   w_dw[i], b_dw[i], conv_ln_g[i], conv_ln_b[i], w_pw2[i], b_pw2[i])
            new_conv.append(cb)
        bsz, t_, d = x.shape
        x = _moe_layer(x.reshape(-1, d), mix.reshape(-1, d), ln_g[layer, 0], ln_b[layer, 0],
                       ln_g[layer, 1], ln_b[layer, 1], w_router[layer], b_router[layer],
                       w_exp_gu[layer], w_exp_down[layer], w_sh_gu[layer], w_sh_down[layer]).reshape(bsz, t_, d)
    return x, jnp.stack(new_gla), jnp.stack(new_rows), jnp.stack(new_win), jnp.stack(new_conv)


def kernel(x_prompt, x_sample, state_gla, cache_nsa_kv, state_nsa_win, state_conv, page_table,
           w_in_ab, w_gla_gate, b_gla_gate, gla_norm_g, w_cmp_pool, w_out_ab,
           w_pw1, b_pw1, w_dw, b_dw, conv_ln_g, conv_ln_b, w_pw2, b_pw2,
           ln_g, ln_b, w_router, b_router, w_exp_gu, w_exp_down, w_sh_gu, w_sh_down):
    weights = (w_in_ab, w_gla_gate, b_gla_gate, gla_norm_g, w_cmp_pool, w_out_ab,
               w_pw1, b_pw1, w_dw, b_dw, conv_ln_g, conv_ln_b, w_pw2, b_pw2,
               ln_g, ln_b, w_router, b_router, w_exp_gu, w_exp_down, w_sh_gu, w_sh_down)
    past_len = page_table.shape[1] * PAGE_SIZE
    pos_p = jnp.arange(x_prompt.shape[1])
    pos_s = past_len + jnp.arange(x_sample.shape[1])
    y_prompt, gla_p, rows_p, win_p, conv_p = _trunk(x_prompt, pos_p, None, None, None, None, None, *weights)
    y_sample, gla_s, rows_s, win_s, conv_s = _trunk(x_sample, pos_s, state_gla, cache_nsa_kv, page_table,
                                                    state_nsa_win, state_conv, *weights)
    return (y_prompt, y_sample, gla_p, gla_s, rows_p, rows_s, win_p, win_s, conv_p, conv_s)
```

```python
import functools
import math

import jax
import jax.numpy as jnp
import numpy as np
from jax import lax
from jax.experimental import pallas as pl
from jax.experimental.pallas import tpu as pltpu
from jax.experimental.pallas import tpu_sc as plsc

D_MODEL = 1024
DEPTH = 2
PAGE_SIZE = 128

GLA_HEADS = 4
GLA_DV = D_MODEL // 2 // GLA_HEADS
GLA_DK = GLA_DV // 2
GLA_RANK = 16
GLA_TAU = 16.0
GLA_CHUNK = 64

NSA_HEADS = 8
NSA_KV_HEADS = 2
NSA_GROUP = NSA_HEADS // NSA_KV_HEADS
HEAD_DIM = D_MODEL // 2 // NSA_HEADS
CMP_BLK = 32
CMP_STRIDE = 16
SEL_BLK = 64
SEL_TOPN = 16
WINDOW = 512
Q_BLK = 128
FORCE_BONUS = 100.0
ROPE_DIM = HEAD_DIM // 4
ROPE_THETA = 500000.0

GLA_SIZES = (GLA_HEADS * GLA_DK, GLA_HEADS * GLA_DK, GLA_HEADS * GLA_DV, GLA_HEADS * GLA_DV, GLA_RANK)
NSA_SIZES = (NSA_HEADS * HEAD_DIM, 6 * NSA_KV_HEADS * HEAD_DIM, 3 * NSA_HEADS)

CONV_W = 31
D_CONV = D_MODEL

N_EXPERTS = 64
N_GROUPS = 8
TOPK_GROUPS = 4
TOP_K = 8
D_EXPERT = 256
ROUTE_SCALE = 2.5
MOE_BLK = 128

ALPHA = (2 * DEPTH) ** 0.25
LN_EPS = 1e-5

LANE = 128


def _dot(a, b):
    return jnp.dot(a, b, preferred_element_type=jnp.float32)


def _dot_nt(a, b):
    return lax.dot_general(a, b, (((1,), (1,)), ((), ())), preferred_element_type=jnp.float32)


def _mm_body(x_ref, w_ref, o_ref):
    o_ref[...] = _dot(x_ref[...].astype(jnp.bfloat16), w_ref[...].astype(jnp.bfloat16))


def _mm(x, w, keep_pad=False):
    m, k = x.shape
    n = w.shape[1]
    n_pad = -(-n // LANE) * LANE
    w = w.astype(jnp.bfloat16)
    if n_pad != n:
        w = jnp.pad(w, ((0, 0), (0, n_pad - n)))
    tm = min(m, 512)
    out = pl.pallas_call(
        _mm_body,
        grid=(m // tm,),
        in_specs=[pl.BlockSpec((tm, k), lambda i: (i, 0)),
                  pl.BlockSpec((k, n_pad), lambda i: (0, 0))],
        out_specs=pl.BlockSpec((tm, n_pad), lambda i: (i, 0)),
        out_shape=jax.ShapeDtypeStruct((m, n_pad), jnp.float32),
        compiler_params=pltpu.CompilerParams(dimension_semantics=("arbitrary",),
                                             vmem_limit_bytes=48 * 1024 * 1024),
        name="mm",
    )(x, w)
    return out if keep_pad or n_pad == n else out[:, :n]


def _mm3(x, w):
    b, t, d = x.shape
    return _mm(x.reshape(b * t, d), w).reshape(b, t, -1)


def _split_cols(h, sizes):
    return jnp.split(h, np.cumsum(sizes)[:-1].tolist(), axis=-1)


def _layer_norm(x, g, b):
    mu = x.mean(-1, keepdims=True)
    var = jnp.square(x - mu).mean(-1, keepdims=True)
    return (x - mu) * lax.rsqrt(var + LN_EPS) * g + b


def _rms_norm(x, g):
    return x * lax.rsqrt(jnp.mean(x * x, -1, keepdims=True) + LN_EPS) * g


def _partial_rope(x, pos):
    half = ROPE_DIM // 2
    inv_freq = jnp.power(ROPE_THETA, -jnp.arange(half, dtype=jnp.float32) / half)
    ang = pos.astype(jnp.float32)[:, None] * inv_freq
    ang = ang.reshape(ang.shape[0], *([1] * (x.ndim - 3)), half)
    cos, sin = jnp.cos(ang), jnp.sin(ang)
    x1 = x[..., :half]
    x2 = x[..., half:ROPE_DIM]
    rot = jnp.concatenate([x1 * cos - x2 * sin, x2 * cos + x1 * sin], -1)
    return jnp.concatenate([rot, x[..., ROPE_DIM:]], -1)


def _masked_softmax(s, mask):
    s = jnp.where(mask, s, -jnp.inf)
    m = jnp.max(s, axis=-1, keepdims=True)
    m = jnp.where(jnp.isfinite(m), m, 0.0)
    p = jnp.exp(s - m)
    return p / jnp.maximum(p.sum(-1, keepdims=True), 1e-30)


def _gla_recurrence(q, k, v, log_a, s0):
    bsz, t_, nh, _ = q.shape
    c = math.gcd(t_, GLA_CHUNK)
    n = t_ // c

    def chunks(a):
        return jnp.moveaxis(a.reshape(bsz, n, c, *a.shape[2:]), 1, 0)

    causal = jnp.tril(jnp.ones((c, c), dtype=bool))[None, :, :, None, None]

    def step(s, inp):
        qc, kc, vc, lc = inp
        bc = jnp.cumsum(lc, axis=1)
        decay = jnp.exp(jnp.where(causal, bc[:, :, None] - bc[:, None, :], -jnp.inf))
        attn = jnp.einsum('bijhd,bjhd->bhij', qc[:, :, None] * decay, kc)
        o = jnp.einsum('bhij,bjhe->bihe', attn, vc) + jnp.einsum('bihd,bhde->bihe', qc * jnp.exp(bc), s)
        bl = bc[:, -1]
        s = jnp.exp(bl)[..., None] * s + jnp.einsum('bjhd,bjhe->bhde', kc * jnp.exp(bl[:, None] - bc), vc)
        return s, o

    s_fin, o = lax.scan(step, s0, (chunks(q), chunks(k), chunks(v), chunks(log_a)))
    return jnp.moveaxis(o, 0, 1).reshape(bsz, t_, nh, -1), s_fin


def _compress(k, v, w_pool):
    bsz, length = k.shape[:2]
    n_sub = length // CMP_STRIDE

    def pool(a, w):
        sub = a[:, :n_sub * CMP_STRIDE].reshape(bsz, n_sub, CMP_STRIDE, *a.shape[2:])
        first = jnp.einsum('bnjhd,j->bnhd', sub, w[:CMP_STRIDE])
        second = jnp.einsum('bnjhd,j->bnhd', sub, w[CMP_STRIDE:])
        return first[:, :-1] + second[:, 1:]

    cend = jnp.arange(n_sub - 1) * CMP_STRIDE + CMP_BLK - 1
    return pool(k, w_pool[0]), pool(v, w_pool[1]), cend


def _to_sel_blocks(a, n_sel):
    bsz, length = a.shape[:2]
    a = jnp.pad(a, ((0, 0), (0, n_sel * SEL_BLK - length), (0, 0), (0, 0)))
    return a.reshape(bsz, n_sel, SEL_BLK, NSA_KV_HEADS, HEAD_DIM).transpose(0, 3, 1, 2, 4)


def _nsa_attend(q_raw, q_rot, qpos, gates, kc, vc, cend, ksb, vsb, kw, vw, kwpos):
    scale = HEAD_DIM ** -0.5
    bsz, tq = q_raw.shape[:2]
    n_cmp, n_sel = kc.shape[1], ksb.shape[2]
    s_c = jnp.einsum('bqhgd,bnhd->bhgqn', q_raw, kc) * scale
    p_c = _masked_softmax(s_c, cend[None, :] <= qpos[:, None])
    o_c = jnp.einsum('bhgqn,bnhd->bqhgd', p_c, vc)
    ratio = SEL_BLK // CMP_STRIDE
    imp = p_c.sum(axis=2)
    imp = jnp.pad(imp, ((0, 0), (0, 0), (0, 0), (1, ratio * (n_sel + 1) - 1 - n_cmp)))
    imp = imp.reshape(bsz, NSA_KV_HEADS, tq, n_sel + 1, ratio)
    imp_s = imp[..., :n_sel, :].sum(-1) + imp[..., 1:, 0]
    blk = jnp.arange(n_sel)[None, :]
    cur = (qpos // SEL_BLK)[:, None]
    valid = blk * SEL_BLK <= qpos[:, None]
    forced = (blk == 0) | (blk == cur) | (blk == cur - 1)
    score = jnp.where(valid, imp_s + jnp.where(forced, FORCE_BONUS, 0.0), -jnp.inf)
    k_top = min(SEL_TOPN, n_sel)
    _, sel = lax.top_k(score, k_top)
    take = jax.vmap(jax.vmap(lambda blocks, idx: blocks[idx]))
    ks = take(ksb, sel).reshape(bsz, NSA_KV_HEADS, tq, k_top * SEL_BLK, HEAD_DIM)
    vs = take(vsb, sel).reshape(bsz, NSA_KV_HEADS, tq, k_top * SEL_BLK, HEAD_DIM)
    kpos = (sel[..., None] * SEL_BLK + jnp.arange(SEL_BLK)).reshape(bsz, NSA_KV_HEADS, tq, k_top * SEL_BLK)
    s_s = jnp.einsum('bqhgd,bhqkd->bhgqk', q_rot, ks) * scale
    p_s = _masked_softmax(s_s, (kpos <= qpos[:, None])[:, :, None])
    o_s = jnp.einsum('bhgqk,bhqkd->bqhgd', p_s, vs)
    s_w = jnp.einsum('bqhgd,bkhd->bhgqk', q_rot, kw) * scale
    kp, qp = kwpos[None, :], qpos[:, None]
    p_w = _masked_softmax(s_w, (kp <= qp) & (kp > qp - WINDOW) & (kp >= 0))
    o_w = jnp.einsum('bhgqk,bkhd->bqhgd', p_w, vw)
    return gates[..., 0:1] * o_c + gates[..., 1:2] * o_s + gates[..., 2:3] * o_w


NSA_ROWS = NSA_GROUP * Q_BLK
SEL_KT = 512
N_SELB = 128
MASKED = -1e9
WIN_KEYS = WINDOW + Q_BLK
KK_W = 2 * HEAD_DIM + N_SELB


def _nsa_prompt_body(qr_ref, qo_ref, kc_ref, vct_ref, kk_ref, vvt_ref, g_ref, o_ref,
                     imp_ref, m_ref, l_ref, acc_ref):
    f32, bf16 = jnp.float32, jnp.bfloat16
    qb = pl.program_id(2)
    q0 = qb * Q_BLK
    qr_t = qr_ref[0, 0, 0]
    qo_t = qo_ref[0, 0, 0]
    n_cmp = kc_ref.shape[2]

    s_c = _dot(kc_ref[0, 0], qr_t)
    n_idx = lax.broadcasted_iota(jnp.int32, (n_cmp, NSA_ROWS), 0)
    qpos_c = q0 + (lax.broadcasted_iota(jnp.int32, (n_cmp, NSA_ROWS), 1) & (Q_BLK - 1))
    cmask = (n_idx * CMP_STRIDE + (CMP_BLK - 1)) <= qpos_c
    s_c = jnp.where(cmask, s_c, MASKED)
    m_c = jnp.max(s_c, axis=0, keepdims=True)
    p_c = jnp.where(cmask, jnp.exp(s_c - m_c), 0.0)
    p_c = p_c / jnp.maximum(jnp.sum(p_c, axis=0, keepdims=True), 1e-30)
    o_ct = _dot(vct_ref[0, 0], p_c.astype(bf16))

    imp = (p_c[:, 0:Q_BLK] + p_c[:, Q_BLK:2 * Q_BLK]) + p_c[:, 2 * Q_BLK:3 * Q_BLK] + p_c[:, 3 * Q_BLK:]
    imp_ref[0:8, :] = jnp.zeros((8, Q_BLK), f32)
    imp_ref[8:8 + n_cmp, :] = imp
    ratio = SEL_BLK // CMP_STRIDE
    n_selb = n_cmp // ratio
    imp_s = imp_ref[pl.ds(7, n_selb, stride=ratio), :]
    for r in range(ratio):
        imp_s = imp_s + imp_ref[pl.ds(8 + r, n_selb, stride=ratio), :]
    blk = lax.broadcasted_iota(jnp.int32, (n_selb, Q_BLK), 0)
    qpos_s = q0 + lax.broadcasted_iota(jnp.int32, (n_selb, Q_BLK), 1)
    cur = lax.shift_right_logical(qpos_s, int(math.log2(SEL_BLK)))
    valid = blk * SEL_BLK <= qpos_s
    forced = (blk == 0) | (blk == cur) | (blk == cur - 1)
    score = jnp.where(valid, imp_s + jnp.where(forced, FORCE_BONUS, 0.0), -1e30)
    picked = jnp.zeros((n_selb, Q_BLK), f32)
    for _ in range(SEL_TOPN):
        best = jnp.max(score, axis=0, keepdims=True)
        first = jnp.min(jnp.where(score == best, blk, n_selb), axis=0, keepdims=True)
        hit = blk == first
        picked = jnp.where(hit, 1.0, picked)
        score = jnp.where(hit, -3e38, score)
    selb_t = jnp.where(valid, picked, 0.0)
    if n_selb < N_SELB:
        selb_t = jnp.concatenate([selb_t, jnp.zeros((N_SELB - n_selb, Q_BLK), f32)], axis=0)
    selb_t = ((selb_t - 1.0) * (-MASKED)).astype(bf16)
    selb_t = jnp.concatenate([selb_t] * NSA_GROUP, axis=1)

    zeros_q = jnp.zeros((HEAD_DIM, NSA_ROWS), bf16)
    q_sel = jnp.concatenate([qo_t, zeros_q, selb_t], axis=0)
    q_win = jnp.concatenate([zeros_q, qo_t, jnp.zeros((N_SELB, NSA_ROWS), bf16)], axis=0)
    qpos_r = q0 + (lax.broadcasted_iota(jnp.int32, (1, NSA_ROWS), 1) & (Q_BLK - 1))

    def v_tiles(first, count):
        return jnp.concatenate([vvt_ref[0, 0, first + j] for j in range(count)], axis=1)

    m_ref[...] = jnp.full(m_ref.shape, MASKED, f32)
    l_ref[...] = jnp.zeros(l_ref.shape, f32)
    acc_ref[...] = jnp.zeros(acc_ref.shape, f32)

    def sel_tile(k0, kt, causal):
        s = _dot(kk_ref[0, 0, pl.ds(k0, kt), :], q_sel)
        if causal:
            kpos = k0 + lax.broadcasted_iota(jnp.int32, (kt, NSA_ROWS), 0)
            s = jnp.where(kpos <= qpos_r, s, MASKED)
        m_old = m_ref[...]
        m_new = jnp.maximum(m_old, jnp.max(s, axis=0, keepdims=True))
        alpha = jnp.exp(m_old - m_new)
        p = jnp.exp(s - m_new)
        l_ref[...] = alpha * l_ref[...] + jnp.sum(p, axis=0, keepdims=True)
        vt = v_tiles(k0 // Q_BLK, kt // Q_BLK)
        acc_ref[...] = alpha * acc_ref[...] + _dot(vt, p.astype(bf16))
        m_ref[...] = m_new

    n_full = q0 // SEL_KT

    def full_step(t, c):
        sel_tile(pl.multiple_of(t * SEL_KT, SEL_KT), SEL_KT, False)
        return c

    lax.fori_loop(0, n_full, full_step, 0)
    d0 = n_full * SEL_KT

    def diag_step(t, c):
        sel_tile(pl.multiple_of(d0 + t * Q_BLK, Q_BLK), Q_BLK, False)
        return c

    lax.fori_loop(0, (q0 - d0) // Q_BLK, diag_step, 0)
    sel_tile(pl.multiple_of(q0, Q_BLK), Q_BLK, True)
    o_st = acc_ref[0:HEAD_DIM, :] / l_ref[...]

    w0 = pl.multiple_of(jnp.maximum(q0 - WINDOW, 0), Q_BLK)
    s_w = _dot(kk_ref[0, 0, pl.ds(w0, WIN_KEYS), :], q_win)
    kpos_w = w0 + lax.broadcasted_iota(jnp.int32, (WIN_KEYS, NSA_ROWS), 0)
    s_w = jnp.where((kpos_w <= qpos_r) & (kpos_w > qpos_r - WINDOW), s_w, MASKED)
    p_w = jnp.exp(s_w - jnp.max(s_w, axis=0, keepdims=True))
    l_w = jnp.sum(p_w, axis=0, keepdims=True)
    acc_w = _dot(v_tiles(w0 // Q_BLK, WIN_KEYS // Q_BLK), p_w.astype(bf16))
    o_wt = acc_w[HEAD_DIM:2 * HEAD_DIM, :] / l_w

    g = g_ref[0, 0, 0]
    out_t = g[0:1, :] * o_ct + g[1:2, :] * o_st + g[2:3, :] * o_wt
    o_ref[0] = jnp.concatenate([out_t[:, g_ * Q_BLK:(g_ + 1) * Q_BLK] for g_ in range(NSA_GROUP)], axis=0).T


def _nsa_prompt(qr, qo, gt, kc_p, vct, kk, vvt):
    bsz, _, nqb = qr.shape[:3]
    t_ = nqb * Q_BLK
    n_cmp = kc_p.shape[2]
    per_blk = lambda b, h, i: (b, h, i, 0, 0)
    per_head = lambda b, h, i: (b, h, 0, 0)
    return pl.pallas_call(
        _nsa_prompt_body,
        grid=(bsz, NSA_KV_HEADS, nqb),
        in_specs=[pl.BlockSpec((1, 1, 1, HEAD_DIM, NSA_ROWS), per_blk),
                  pl.BlockSpec((1, 1, 1, HEAD_DIM, NSA_ROWS), per_blk),
                  pl.BlockSpec((1, 1, n_cmp, HEAD_DIM), per_head),
                  pl.BlockSpec((1, 1, HEAD_DIM, n_cmp), per_head),
                  pl.BlockSpec((1, 1, t_, KK_W), per_head),
                  pl.BlockSpec((1, 1, nqb, 2 * HEAD_DIM, Q_BLK), lambda b, h, i: (b, h, 0, 0, 0)),
                  pl.BlockSpec((1, 1, 1, 3, NSA_ROWS), per_blk)],
        out_specs=pl.BlockSpec((1, Q_BLK, NSA_GROUP * HEAD_DIM), lambda b, h, i: (b, i, h)),
        out_shape=jax.ShapeDtypeStruct((bsz, t_, NSA_HEADS * HEAD_DIM), jnp.float32),
        scratch_shapes=[pltpu.VMEM((8 + n_cmp, Q_BLK), jnp.float32),
                        pltpu.VMEM((1, NSA_ROWS), jnp.float32),
                        pltpu.VMEM((1, NSA_ROWS), jnp.float32),
                        pltpu.VMEM((2 * HEAD_DIM, NSA_ROWS), jnp.float32)],
        compiler_params=pltpu.CompilerParams(
            dimension_semantics=("arbitrary", "arbitrary", "arbitrary"),
            vmem_limit_bytes=48 * 1024 * 1024),
        name="nsa_prompt",
    )(qr, qo, kc_p, vct, kk, vvt, gt)


GLA_SUB = 16
GLA_QK = GLA_HEADS * GLA_DK
GLA_V = GLA_HEADS * GLA_DV


def _dot_tn(a, b):
    return lax.dot_general(a, b, (((0,), (0,)), ((), ())), preferred_element_type=jnp.float32)


def _gla_body(q_ref, k_ref, v_ref, gr_ref, glr_ref, wg_ref, bg_ref, ng_ref, s0_ref, exp_ref, bd_ref,
              o_ref, sfin_ref, st_ref, b_ref, qd_ref, *, t_valid):
    f32, bf16 = jnp.float32, jnp.bfloat16
    tt = q_ref.shape[1]
    ti = pl.program_id(1)

    @pl.when(ti == 0)
    def _():
        st_ref[...] = s0_ref[0]

    row = lax.broadcasted_iota(jnp.int32, (tt, 1), 0)
    z = _dot(glr_ref[0][:, :GLA_RANK].astype(bf16), wg_ref[...]) + bg_ref[...]
    la = (jnp.minimum(z, 0.0) - jnp.log1p(jnp.exp(-jnp.abs(z)))) * (1.0 / GLA_TAU)
    la = jnp.where(ti * tt + row < t_valid, la, 0.0)
    seg = row & (GLA_SUB - 1)
    b = la
    for s in (1, 2, 4, 8):
        b = b + jnp.where(seg >= s, pltpu.roll(b, s, axis=0), 0.0)
    q = q_ref[0] * (GLA_DK ** -0.5)
    k = k_ref[0]
    v = v_ref[0]
    o = _dot((q * k).astype(bf16), exp_ref[...]) * v
    for d in range(1, GLA_SUB):
        decay = jnp.exp(jnp.minimum(b - pltpu.roll(b, d, axis=0), 0.0))
        w = jnp.where(seg >= d, q * pltpu.roll(k, d, axis=0) * decay, 0.0)
        o = o + _dot(w.astype(bf16), exp_ref[...]) * pltpu.roll(v, d, axis=0)
    o_ref[0] = o
    b_ref[...] = b
    qd_ref[...] = (q * jnp.exp(b)).astype(bf16)

    def block_step(c, carry):
        rows = pl.ds(pl.multiple_of(c * GLA_SUB, GLA_SUB), GLA_SUB)
        st = st_ref[...]
        o_ref[0, rows, :] += _dot_nt(qd_ref[rows, :], st.astype(bf16))
        bc = b_ref[rows, :]
        bl = bc[GLA_SUB - 1:GLA_SUB, :]
        kc = (k_ref[0, rows, :] * jnp.exp(bl - bc)).astype(bf16)
        upd = _dot_tn(v_ref[0, rows, :].astype(bf16), kc)
        st_ref[...] = st * jnp.exp(bl) + upd * bd_ref[...]
        return carry

    lax.fori_loop(0, tt // GLA_SUB, block_step, 0)
    sfin_ref[0] = st_ref[...]
    gr = gr_ref[0]
    gate = gr * jax.nn.sigmoid(gr)
    for h in range(GLA_HEADS):
        cols = slice(h * GLA_DV, (h + 1) * GLA_DV)
        oh = o_ref[0, :, cols]
        ms = jnp.mean(oh * oh, axis=-1, keepdims=True)
        o_ref[0, :, cols] = oh * lax.rsqrt(ms + LN_EPS) * ng_ref[...] * gate[:, cols]


def _gla(h, w_gla_gate, b_gla_gate, gla_norm_g, gla_state):
    bsz, t_, n_in = h.shape
    tp = -(-t_ // GLA_SUB) * GLA_SUB
    if tp != t_:
        h = jnp.pad(h, ((0, 0), (0, tp - t_), (0, 0)))
    tt = min(tp, 256)
    heads = np.arange(GLA_HEADS)
    expand = np.repeat(np.repeat(np.eye(GLA_HEADS, dtype=np.float32), GLA_DK, 0), GLA_DV, 1)
    bdmask = jnp.asarray(expand.T)
    if gla_state is None:
        s0 = jnp.zeros((bsz, GLA_V, GLA_QK), jnp.float32)
    else:
        s0 = jnp.zeros((bsz, GLA_HEADS, GLA_DV, GLA_HEADS, GLA_DK), jnp.float32)
        s0 = s0.at[:, heads, :, heads, :].set(gla_state.transpose(1, 0, 3, 2)).reshape(bsz, GLA_V, GLA_QK)
    tile = lambda width, blk: pl.BlockSpec((1, tt, width), lambda b, i: (b, i, blk))
    fixed2 = lambda shape: pl.BlockSpec(shape, lambda b, i: (0, 0))
    per_b = pl.BlockSpec((1, GLA_V, GLA_QK), lambda b, i: (b, 0, 0))
    o, s_t = pl.pallas_call(
        functools.partial(_gla_body, t_valid=t_),
        grid=(bsz, tp // tt),
        in_specs=[tile(GLA_QK, 0), tile(GLA_QK, 1), tile(GLA_V, 1), tile(GLA_V, 2),
                  tile(LANE, (2 * GLA_QK + 2 * GLA_V + NSA_SIZES[0] + NSA_SIZES[1]) // LANE),
                  fixed2((GLA_RANK, GLA_QK)), fixed2((1, GLA_QK)), fixed2((1, GLA_DV)), per_b,
                  fixed2((GLA_QK, GLA_V)), fixed2((GLA_V, GLA_QK))],
        out_specs=[pl.BlockSpec((1, tt, GLA_V), lambda b, i: (b, i, 0)), per_b],
        out_shape=[jax.ShapeDtypeStruct((bsz, tp, GLA_V), jnp.float32),
                   jax.ShapeDtypeStruct((bsz, GLA_V, GLA_QK), jnp.float32)],
        scratch_shapes=[pltpu.VMEM((GLA_V, GLA_QK), jnp.float32), pltpu.VMEM((tt, GLA_QK), jnp.float32),
                        pltpu.VMEM((tt, GLA_QK), jnp.bfloat16)],
        compiler_params=pltpu.CompilerParams(dimension_semantics=("arbitrary", "arbitrary"),
                                             vmem_limit_bytes=48 * 1024 * 1024),
        name="gla",
    )(h, h, h, h, h, w_gla_gate.astype(jnp.bfloat16), b_gla_gate.reshape(1, GLA_QK),
      gla_norm_g.reshape(1, GLA_DV), s0, jnp.asarray(expand, jnp.bfloat16), bdmask)
    s_new = s_t.reshape(bsz, GLA_HEADS, GLA_DV, GLA_HEADS, GLA_DK)[:, heads, :, heads, :]
    return o[:, :t_], s_new.transpose(1, 0, 3, 2)


COL_NQ = 2 * GLA_QK + 2 * GLA_V
COL_NKV = COL_NQ + NSA_SIZES[0]
COL_TAIL = COL_NKV + NSA_SIZES[1]
TAIL_GATE = GLA_RANK
_ORIG = np.cumsum((0,) + GLA_SIZES + NSA_SIZES)
IN_AB_PERM = np.concatenate([np.arange(_ORIG[0], _ORIG[4]), np.arange(_ORIG[5], _ORIG[7]),
                             np.arange(_ORIG[4], _ORIG[5]), np.arange(_ORIG[7], _ORIG[8])])
SUBS = Q_BLK // CMP_STRIDE


def _nsa_prep_body(nq_ref, kv0_ref, kv1_ref, kv2_ref, tail_ref, rc_ref, ru_ref, rd_ref, pool_ref,
                   rows_ref, win_ref, kk_ref, vvt_ref, qr_ref, qo_ref, g_ref, pooled_ref):
    bf16 = jnp.bfloat16
    q0 = pl.program_id(1) * Q_BLK
    kv_w = NSA_KV_HEADS * HEAD_DIM

    def rope(x):
        reps = x.shape[1] // LANE
        wide = lambda r: jnp.concatenate([r[...]] * reps, axis=1) if reps > 1 else r[...]
        half = ROPE_DIM // 2
        return (x * wide(rc_ref) + pltpu.roll(x, half, axis=1) * wide(ru_ref)
                + pltpu.roll(x, x.shape[1] - half, axis=1) * wide(rd_ref))

    kv0, kv1, kv2 = kv0_ref[0], kv1_ref[0], kv2_ref[0]
    k_sel, v_sel = rope(kv1[:, :kv_w]), kv1[:, kv_w:]
    k_win, v_win = rope(kv2[:, :kv_w]), kv2[:, kv_w:]
    rows_ref[0] = jnp.concatenate([kv0, k_sel, v_sel], axis=1)
    win_ref[0] = jnp.concatenate([k_win, v_win], axis=1)
    blk_id = lax.shift_right_logical(q0 + lax.broadcasted_iota(jnp.int32, (Q_BLK, N_SELB), 0),
                                     int(math.log2(SEL_BLK)))
    onehot = jnp.where(lax.broadcasted_iota(jnp.int32, (Q_BLK, N_SELB), 1) == blk_id, 1.0, 0.0).astype(bf16)
    q = nq_ref[0] * (HEAD_DIM ** -0.5)
    q_rot = rope(q)
    gates_t = jax.nn.sigmoid(tail_ref[0]).T
    for h in range(NSA_KV_HEADS):
        hs = slice(h * HEAD_DIM, (h + 1) * HEAD_DIM)
        kk_ref[0, h] = jnp.concatenate([k_sel[:, hs].astype(bf16), k_win[:, hs].astype(bf16), onehot], axis=1)
        vvt_ref[0, h, 0] = jnp.concatenate([v_sel[:, hs], v_win[:, hs]], axis=1).T.astype(bf16)
        gw = NSA_GROUP * HEAD_DIM
        for src, dst in ((q, qr_ref), (q_rot, qo_ref)):
            t = src[:, h * gw:(h + 1) * gw].T
            dst[0, h, 0] = jnp.concatenate([t[g * HEAD_DIM:(g + 1) * HEAD_DIM] for g in range(NSA_GROUP)],
                                           axis=1).astype(bf16)
        base = TAIL_GATE + h * NSA_GROUP * 3
        g_ref[0, h, 0] = jnp.concatenate(
            [jnp.concatenate([gates_t[base + 3 * g + c:base + 3 * g + c + 1] for g in range(NSA_GROUP)], axis=1)
             for c in range(3)], axis=0)
    kc_in, vc_in = kv0[:, :kv_w].astype(bf16), kv0[:, kv_w:].astype(bf16)
    pooled_ref[0] = jnp.concatenate([_dot(pool_ref[0], kc_in), _dot(pool_ref[1], kc_in),
                                     _dot(pool_ref[2], vc_in), _dot(pool_ref[3], vc_in)], axis=1)


def _nsa_prep(h, pos, w_cmp_pool):
    bsz, t_, _ = h.shape
    nqb = t_ // Q_BLK
    bf16 = jnp.bfloat16
    half = ROPE_DIM // 2
    inv_freq = jnp.power(ROPE_THETA, -jnp.arange(half, dtype=jnp.float32) / half)
    ang = pos.astype(jnp.float32)[:, None] * inv_freq
    cos, sin = jnp.cos(ang), jnp.sin(ang)
    rest = HEAD_DIM - ROPE_DIM
    z8, zr = jnp.zeros((t_, half), jnp.float32), jnp.zeros((t_, rest), jnp.float32)
    two = lambda a: jnp.concatenate([a, a], axis=1)
    rc = two(jnp.concatenate([cos, cos, jnp.ones((t_, rest), jnp.float32)], axis=1))
    ru = two(jnp.concatenate([z8, sin, zr], axis=1))
    rd = two(jnp.concatenate([-sin, z8, zr], axis=1))
    pool = _pool_matrices(w_cmp_pool)
    kv_w = NSA_KV_HEADS * HEAD_DIM
    col = lambda width, off: pl.BlockSpec((1, Q_BLK, width), lambda b, i: (b, i, off // width))
    rows_t = pl.BlockSpec((Q_BLK, LANE), lambda b, i: (i, 0))
    head4 = lambda r, c: pl.BlockSpec((1, NSA_KV_HEADS, 1, r, c), lambda b, i: (b, 0, i, 0, 0))
    return pl.pallas_call(
        _nsa_prep_body,
        grid=(bsz, nqb),
        in_specs=[col(NSA_SIZES[0], COL_NQ), col(2 * kv_w, COL_NKV), col(2 * kv_w, COL_NKV + 2 * kv_w),
                  col(2 * kv_w, COL_NKV + 4 * kv_w), col(LANE, COL_TAIL), rows_t, rows_t, rows_t,
                  pl.BlockSpec((4, SUBS, Q_BLK), lambda b, i: (0, 0, 0))],
        out_specs=[pl.BlockSpec((1, Q_BLK, 4 * kv_w), lambda b, i: (b, i, 0)),
                   pl.BlockSpec((1, Q_BLK, 2 * kv_w), lambda b, i: (b, i, 0)),
                   pl.BlockSpec((1, NSA_KV_HEADS, Q_BLK, KK_W), lambda b, i: (b, 0, i, 0)),
                   head4(2 * HEAD_DIM, Q_BLK), head4(HEAD_DIM, NSA_ROWS), head4(HEAD_DIM, NSA_ROWS),
                   head4(3, NSA_ROWS),
                   pl.BlockSpec((1, SUBS, 4 * kv_w), lambda b, i: (b, i, 0))],
        out_shape=[jax.ShapeDtypeStruct((bsz, t_, 4 * kv_w), jnp.float32),
                   jax.ShapeDtypeStruct((bsz, t_, 2 * kv_w), jnp.float32),
                   jax.ShapeDtypeStruct((bsz, NSA_KV_HEADS, t_, KK_W), bf16),
                   jax.ShapeDtypeStruct((bsz, NSA_KV_HEADS, nqb, 2 * HEAD_DIM, Q_BLK), bf16),
                   jax.ShapeDtypeStruct((bsz, NSA_KV_HEADS, nqb, HEAD_DIM, NSA_ROWS), bf16),
                   jax.ShapeDtypeStruct((bsz, NSA_KV_HEADS, nqb, HEAD_DIM, NSA_ROWS), bf16),
                   jax.ShapeDtypeStruct((bsz, NSA_KV_HEADS, nqb, 3, NSA_ROWS), jnp.float32),
                   jax.ShapeDtypeStruct((bsz, t_ // CMP_STRIDE, 4 * kv_w), jnp.float32)],
        compiler_params=pltpu.CompilerParams(dimension_semantics=("arbitrary", "arbitrary")),
        name="nsa_prep",
    )(h, h, h, h, h, rc, ru, rd, pool)


PAGE_GROUP = 8
DEC_KEYS = PAGE_GROUP * PAGE_SIZE
NEW_PAD = 8
KV_W = NSA_KV_HEADS * HEAD_DIM


def _dec_pool_body(pt_ref, *refs):
    page_refs, pool_ref, out_ref = refs[:PAGE_GROUP], refs[PAGE_GROUP], refs[PAGE_GROUP + 1]
    bf16 = jnp.bfloat16
    parts = []
    for pr in page_refs:
        kv0 = pr[0]
        kc_in, vc_in = kv0[:, :KV_W].astype(bf16), kv0[:, KV_W:].astype(bf16)
        parts.append(jnp.concatenate([_dot(pool_ref[0], kc_in), _dot(pool_ref[1], kc_in),
                                      _dot(pool_ref[2], vc_in), _dot(pool_ref[3], vc_in)], axis=1))
    out_ref[0] = jnp.concatenate(parts, axis=0)


def _page_specs(n_pages, col_blk):
    def spec(i):
        return pl.BlockSpec((1, PAGE_SIZE, 2 * KV_W),
                            lambda b, j, pt: (pt[b * n_pages + j * PAGE_GROUP + i], 0, col_blk))
    return [spec(i) for i in range(PAGE_GROUP)]


def _dec_pool(cache, page_table, pool):
    bsz, n_pages = page_table.shape
    grid_spec = pltpu.PrefetchScalarGridSpec(
        num_scalar_prefetch=1, grid=(bsz, n_pages // PAGE_GROUP),
        in_specs=_page_specs(n_pages, 0) + [pl.BlockSpec((4, SUBS, Q_BLK), lambda b, j, pt: (0, 0, 0))],
        out_specs=pl.BlockSpec((1, PAGE_GROUP * SUBS, 4 * KV_W), lambda b, j, pt: (b, j, 0)))
    return pl.pallas_call(
        _dec_pool_body, grid_spec=grid_spec,
        out_shape=jax.ShapeDtypeStruct((bsz, n_pages * SUBS, 4 * KV_W), jnp.float32),
        compiler_params=pltpu.CompilerParams(dimension_semantics=("arbitrary", "arbitrary")),
        name="nsa_dec_pool",
    )(page_table.reshape(-1), *([cache] * PAGE_GROUP), pool)


def _dec_select_body(qr_ref, kct_ref, vc_ref, band_ref, oc_ref, selb_ref, *, qpos0, n_q, n_pick, n_blk):
    f32, bf16 = jnp.float32, jnp.bfloat16
    n_cmp = kct_ref.shape[3]
    rows = NSA_GROUP * n_q
    for h in range(NSA_KV_HEADS):
        s_c = _dot(qr_ref[0, h], kct_ref[0, h])
        n_idx = lax.broadcasted_iota(jnp.int32, (rows, n_cmp), 1)
        qpos = qpos0 + (lax.broadcasted_iota(jnp.int32, (rows, n_cmp), 0) % n_q)
        cmask = (n_idx * CMP_STRIDE + (CMP_BLK - 1)) <= qpos
        s_c = jnp.where(cmask, s_c, MASKED)
        p_c = jnp.where(cmask, jnp.exp(s_c - jnp.max(s_c, axis=1, keepdims=True)), 0.0)
        p_c = p_c / jnp.maximum(jnp.sum(p_c, axis=1, keepdims=True), 1e-30)
        oc_ref[0, h] = _dot(p_c.astype(bf16), vc_ref[0, h])
        imp = p_c[0:n_q]
        for g in range(1, NSA_GROUP):
            imp = imp + p_c[g * n_q:(g + 1) * n_q]
        imp_s = jnp.zeros((n_q, N_SELB), f32)
        rem = imp
        for _ in range(3):
            part = rem.astype(bf16)
            imp_s = imp_s + _dot(part, band_ref[...])
            rem = rem - part.astype(f32)
        blk = lax.broadcasted_iota(jnp.int32, (n_q, N_SELB), 1)
        qpos_s = qpos0 + lax.broadcasted_iota(jnp.int32, (n_q, N_SELB), 0)
        cur = lax.shift_right_logical(qpos_s, int(math.log2(SEL_BLK)))
        valid = (blk * SEL_BLK <= qpos_s) & (blk < n_blk)
        forced = (blk == 0) | (blk == cur) | (blk == cur - 1)
        score = jnp.where(valid, imp_s + jnp.where(forced, FORCE_BONUS, 0.0), -1e30)
        picked = jnp.zeros((n_q, N_SELB), f32)
        for _ in range(n_pick):
            best = jnp.max(score, axis=1, keepdims=True)
            first = jnp.min(jnp.where(score == best, blk, N_SELB), axis=1, keepdims=True)
            hit = blk == first
            picked = jnp.where(hit, 1.0, picked)
            score = jnp.where(hit, -3e38, score)
        selb_ref[0, h] = (jnp.where(valid, picked, 0.0) - 1.0) * (-MASKED)


def _dec_select(qr, kct, vc, n_q, qpos0, n_pick, n_blk):
    bsz = qr.shape[0]
    rows = NSA_GROUP * n_q
    n_cmp = kct.shape[3]
    ratio = SEL_BLK // CMP_STRIDE
    c_idx, j_idx = np.arange(n_cmp)[:, None], np.arange(N_SELB)[None, :]
    band = jnp.asarray(((c_idx >= ratio * j_idx - 1) & (c_idx <= ratio * j_idx + ratio - 1)), jnp.bfloat16)
    per_b = lambda *tail: pl.BlockSpec((1, NSA_KV_HEADS) + tail, lambda b: (b, 0, 0, 0))
    return pl.pallas_call(
        functools.partial(_dec_select_body, qpos0=qpos0, n_q=n_q, n_pick=n_pick, n_blk=n_blk),
        grid=(bsz,),
        in_specs=[per_b(rows, HEAD_DIM), per_b(HEAD_DIM, n_cmp), per_b(n_cmp, HEAD_DIM),
                  pl.BlockSpec((n_cmp, N_SELB), lambda b: (0, 0))],
        out_specs=[per_b(rows, HEAD_DIM), per_b(n_q, N_SELB)],
        out_shape=[jax.ShapeDtypeStruct((bsz, NSA_KV_HEADS, rows, HEAD_DIM), jnp.float32),
                   jax.ShapeDtypeStruct((bsz, NSA_KV_HEADS, n_q, N_SELB), jnp.float32)],
        compiler_params=pltpu.CompilerParams(dimension_semantics=("arbitrary",)),
        name="nsa_dec_select",
    )(qr, kct, vc, band)


def _dec_attend_body(pt_ref, *refs, qpos0, n_q, past):
    page_refs = refs[:PAGE_GROUP]
    (qs_ref, qw_ref, knew_ref, vnew_ref, wbuf_ref, wnew_ref, oc_ref, g_ref,
     o_ref, m_ref, l_ref, acc_ref) = refs[PAGE_GROUP:]
    f32, bf16 = jnp.float32, jnp.bfloat16
    j = pl.program_id(1)
    n_rows = qs_ref.shape[1]

    @pl.when(j == 0)
    def _():
        m_ref[...] = jnp.full(m_ref.shape, MASKED, f32)
        l_ref[...] = jnp.zeros(l_ref.shape, f32)
        acc_ref[...] = jnp.zeros(acc_ref.shape, f32)

    def online(s, v):
        m_old = m_ref[...]
        m_new = jnp.maximum(m_old, jnp.max(s, axis=1, keepdims=True))
        alpha = jnp.exp(m_old - m_new)
        p = jnp.exp(s - m_new)
        l_ref[...] = alpha * l_ref[...] + jnp.sum(p, axis=1, keepdims=True)
        acc_ref[...] = alpha * acc_ref[...] + _dot(p.astype(bf16), v)
        m_ref[...] = m_new

    qs = qs_ref[0]
    pages = [pr[0] for pr in page_refs]
    keys = jnp.concatenate([p[:, :KV_W] for p in pages], axis=0).astype(bf16)
    vals = jnp.concatenate([p[:, KV_W:] for p in pages], axis=0).astype(bf16)
    blk_id = j * (DEC_KEYS // SEL_BLK) + lax.shift_right_logical(
        lax.broadcasted_iota(jnp.int32, (DEC_KEYS, N_SELB), 0), int(math.log2(SEL_BLK)))
    onehot = jnp.where(lax.broadcasted_iota(jnp.int32, (DEC_KEYS, N_SELB), 1) == blk_id, 1.0, 0.0).astype(bf16)
    online(_dot_nt(qs, jnp.concatenate([keys, onehot], axis=1)), vals)

    @pl.when(j == pl.num_programs(1) - 1)
    def _():
        row_q = qpos0 + (lax.broadcasted_iota(jnp.int32, (n_rows, 1), 0) % n_q)
        qh = qw_ref[0]
        new_pos = past + lax.broadcasted_iota(jnp.int32, (n_rows, NEW_PAD), 1)
        new_ok = (new_pos <= row_q) & (new_pos < past + n_q)
        s_new = jnp.where(new_ok, _dot_nt(qh, knew_ref[0]), MASKED)
        online(s_new, vnew_ref[0])
        o_s = acc_ref[...] / l_ref[...]
        wbuf = wbuf_ref[0]
        wnew = wnew_ref[0]
        n_buf = wbuf.shape[0]
        s_b = _dot_nt(qh, wbuf[:, :KV_W].astype(bf16))
        pos_b = (past - n_buf) + lax.broadcasted_iota(jnp.int32, (n_rows, n_buf), 1)
        s_b = jnp.where((pos_b > row_q - WINDOW) & (pos_b >= 0), s_b, MASKED)
        s_n = jnp.where(new_ok, _dot_nt(qh, wnew[:, :KV_W].astype(bf16)), MASKED)
        m_w = jnp.maximum(jnp.max(s_b, axis=1, keepdims=True), jnp.max(s_n, axis=1, keepdims=True))
        p_b, p_n = jnp.exp(s_b - m_w), jnp.exp(s_n - m_w)
        l_w = jnp.sum(p_b, axis=1, keepdims=True) + jnp.sum(p_n, axis=1, keepdims=True)
        o_w = (_dot(p_b.astype(bf16), wbuf[:, KV_W:].astype(bf16))
               + _dot(p_n.astype(bf16), wnew[:, KV_W:].astype(bf16))) / l_w
        half = n_rows // NSA_KV_HEADS
        own = lambda a: jnp.concatenate([a[h * half:(h + 1) * half, h * HEAD_DIM:(h + 1) * HEAD_DIM]
                                         for h in range(NSA_KV_HEADS)], axis=0)
        g = g_ref[0]
        o_ref[0] = g[:, 0:1] * oc_ref[0] + g[:, 1:2] * own(o_s) + g[:, 2:3] * own(o_w)


def _dec_attend(cache, page_table, qs, qw, knew, vnew, wbuf, wnew, o_c, gates, n_q, qpos0):
    bsz, n_pages = page_table.shape
    n_rows = qs.shape[1]
    per_b = lambda *tail: pl.BlockSpec((1,) + tail, lambda b, j, pt: (b, 0, 0))
    grid_spec = pltpu.PrefetchScalarGridSpec(
        num_scalar_prefetch=1, grid=(bsz, n_pages // PAGE_GROUP),
        in_specs=_page_specs(n_pages, 1) + [
            per_b(n_rows, KV_W + N_SELB), per_b(n_rows, KV_W), per_b(NEW_PAD, KV_W), per_b(NEW_PAD, KV_W),
            per_b(wbuf.shape[1], 2 * KV_W), per_b(NEW_PAD, 2 * KV_W), per_b(n_rows, HEAD_DIM), per_b(n_rows, 3)],
        out_specs=per_b(n_rows, HEAD_DIM),
        scratch_shapes=[pltpu.VMEM((n_rows, 1), jnp.float32), pltpu.VMEM((n_rows, 1), jnp.float32),
                        pltpu.VMEM((n_rows, KV_W), jnp.float32)])
    return pl.pallas_call(
        functools.partial(_dec_attend_body, qpos0=qpos0, n_q=n_q, past=n_pages * PAGE_SIZE),
        grid_spec=grid_spec,
        out_shape=jax.ShapeDtypeStruct((bsz, n_rows, HEAD_DIM), jnp.float32),
        compiler_params=pltpu.CompilerParams(dimension_semantics=("arbitrary", "arbitrary")),
        name="nsa_dec_attend",
    )(page_table.reshape(-1), *([cache] * PAGE_GROUP), qs, qw, knew, vnew, wbuf, wnew, o_c, gates)


def _pool_matrices(w_cmp_pool):
    sub = np.arange(Q_BLK) // CMP_STRIDE == np.arange(SUBS)[:, None]
    w_rep = jnp.tile(w_cmp_pool.reshape(2, 2, CMP_STRIDE), (1, 1, SUBS))
    return jnp.where(sub[None, None], w_rep[:, :, None, :], 0.0).reshape(4, SUBS, Q_BLK).astype(jnp.bfloat16)


def _compressed_from_pooled(pooled):
    bsz, n_sub, _ = pooled.shape
    pooled = pooled.reshape(bsz, n_sub, 4, NSA_KV_HEADS, HEAD_DIM)
    kc = pooled[:, :-1, 0] + pooled[:, 1:, 1]
    vc = pooled[:, :-1, 2] + pooled[:, 1:, 3]
    pad = lambda a: jnp.pad(a, ((0, 0), (0, 1), (0, 0), (0, 0))).transpose(0, 2, 1, 3)
    return pad(kc), pad(vc)


def _nsa_decode(q_raw, q_rot, gates, rows_full, rows_win, cache, page_table, win_buf, w_cmp_pool, past):
    bsz, n_q = q_raw.shape[:2]
    bf16 = jnp.bfloat16
    n_blk = past // SEL_BLK
    assert past % DEC_KEYS == 0 and n_blk <= N_SELB and n_q <= NEW_PAD
    scale = HEAD_DIM ** -0.5
    cache2 = cache.reshape(cache.shape[0], PAGE_SIZE, 4 * KV_W)
    pooled = _dec_pool(cache2, page_table, _pool_matrices(w_cmp_pool))
    kc_p, vc_p = _compressed_from_pooled(pooled)
    rows_of = lambda a: a.transpose(0, 2, 3, 1, 4).reshape(bsz, NSA_KV_HEADS, NSA_GROUP * n_q, a.shape[-1])
    qr = rows_of((q_raw * scale).astype(bf16))
    n_pick = min(SEL_TOPN, n_blk + 1) - 1
    o_c, selb = _dec_select(qr, kc_p.transpose(0, 1, 3, 2).astype(bf16), vc_p.astype(bf16), n_q, past, n_pick, n_blk)
    qo = rows_of((q_rot * scale).astype(bf16))
    zero = jnp.zeros_like(qo[:, 0])
    qw = jnp.concatenate([jnp.concatenate([qo[:, 0], zero], -1), jnp.concatenate([zero, qo[:, 1]], -1)], axis=1)
    bias = jnp.tile(selb, (1, 1, NSA_GROUP, 1)).reshape(bsz, -1, N_SELB).astype(bf16)
    qs = jnp.concatenate([qw, bias], axis=-1)
    pad_new = lambda a: jnp.pad(a.reshape(bsz, n_q, -1), ((0, 0), (0, NEW_PAD - n_q), (0, 0)))
    knew = pad_new(rows_full[:, :, 2]).astype(bf16)
    vnew = pad_new(rows_full[:, :, 3]).astype(bf16)
    wnew = pad_new(rows_win)
    wbuf = win_buf.reshape(bsz, win_buf.shape[1], 2 * KV_W)
    gt = rows_of(gates).reshape(bsz, -1, 3)
    o = _dec_attend(cache2, page_table, qs, qw, knew, vnew, wbuf, wnew,
                    o_c.reshape(bsz, -1, HEAD_DIM), gt, n_q, past)
    o = o.reshape(bsz, NSA_KV_HEADS, NSA_GROUP, n_q, HEAD_DIM).transpose(0, 3, 1, 2, 4)
    return o.reshape(bsz, n_q, NSA_HEADS * HEAD_DIM)


def _ab_mixer(x, pos, w_in, w_gla_gate, b_gla_gate, gla_norm_g, w_cmp_pool, w_out,
              gla_state, nsa_cache, page_table, win_buf):
    bsz, t_, _ = x.shape
    h_in = _mm(x.reshape(bsz * t_, -1), w_in[:, IN_AB_PERM], keep_pad=True).reshape(bsz, t_, -1)
    o_a, s_a = _gla(h_in, w_gla_gate, b_gla_gate, gla_norm_g, gla_state)
    kv_w = NSA_KV_HEADS * HEAD_DIM
    if nsa_cache is None:
        rows2, win2, kk, vvt, qr, qo, gt, pooled = _nsa_prep(h_in, pos, w_cmp_pool)
        pooled = pooled.reshape(bsz, t_ // CMP_STRIDE, 4, NSA_KV_HEADS, HEAD_DIM)
        kc = pooled[:, :-1, 0] + pooled[:, 1:, 1]
        vc = pooled[:, :-1, 2] + pooled[:, 1:, 3]
        kc_p = jnp.pad(kc, ((0, 0), (0, 1), (0, 0), (0, 0))).transpose(0, 2, 1, 3).astype(jnp.bfloat16)
        vct = jnp.pad(vc, ((0, 0), (0, 1), (0, 0), (0, 0))).transpose(0, 2, 3, 1).astype(jnp.bfloat16)
        o_b = _nsa_prompt(qr, qo, gt, kc_p, vct, kk, vvt)
        rows_full = rows2.reshape(bsz, t_, 4, NSA_KV_HEADS, HEAD_DIM)
        new_win = win2[:, -min(WINDOW, t_):].reshape(bsz, -1, 2, NSA_KV_HEADS, HEAD_DIM)
    else:
        nq = h_in[..., COL_NQ:COL_NKV]
        nkv = h_in[..., COL_NKV:COL_TAIL]
        ngate = h_in[..., COL_TAIL + TAIL_GATE:COL_TAIL + TAIL_GATE + NSA_SIZES[2]]
        q_raw = nq.reshape(bsz, t_, NSA_KV_HEADS, NSA_GROUP, HEAD_DIM)
        q_rot = _partial_rope(q_raw, pos)
        kv = nkv.reshape(bsz, t_, 6, NSA_KV_HEADS, HEAD_DIM)
        k_sel = _partial_rope(kv[:, :, 2], pos)
        k_win = _partial_rope(kv[:, :, 4], pos)
        rows_full = jnp.stack([kv[:, :, 0], kv[:, :, 1], k_sel, kv[:, :, 3]], axis=2)
        rows_win = jnp.stack([k_win, kv[:, :, 5]], axis=2)
        gates = jax.nn.sigmoid(ngate).reshape(bsz, t_, NSA_KV_HEADS, NSA_GROUP, 3)
        past_len = page_table.shape[1] * PAGE_SIZE
        o_b = _nsa_decode(q_raw, q_rot, gates, rows_full, rows_win, nsa_cache, page_table, win_buf,
                          w_cmp_pool, past_len)
        w_buf = win_buf.shape[1]
        kw = jnp.concatenate([win_buf, rows_win], axis=1)
        new_win = kw[:, -w_buf:]
    y = _mm3(jnp.concatenate([o_a, o_b], axis=-1), w_out)
    return y, s_a, rows_full, new_win


def _conv_module(x, conv_buf, w_pw1, b_pw1, w_dw, b_dw, ln_g, ln_b, w_pw2, b_pw2):
    bsz = x.shape[0]
    a, g = jnp.split(_mm3(x, w_pw1) + b_pw1, 2, axis=-1)
    u = a * jax.nn.sigmoid(g)
    if conv_buf is None:
        conv_buf = jnp.zeros((bsz, CONV_W - 1, D_CONV), u.dtype)
    ext = jnp.concatenate([conv_buf, u], axis=1)
    c = lax.conv_general_dilated(ext, w_dw[:, None, :], (1,), 'VALID',
                                 dimension_numbers=('NWC', 'WIO', 'NWC'),
                                 feature_group_count=D_CONV) + b_dw
    c = jax.nn.silu(_layer_norm(c, ln_g, ln_b))
    return _mm3(c, w_pw2) + b_pw2, ext[:, -(CONV_W - 1):]


PACK_W = 256
SC_WINDOW = 128
SC_TILES = 32


def _pack_rows(y):
    out = []
    for h in range(2):
        lo = lax.bitcast_convert_type(y[:, 2 * h * PACK_W:(2 * h + 1) * PACK_W].astype(jnp.bfloat16)
                                      .astype(jnp.float32), jnp.uint32)
        hi = lax.bitcast_convert_type(y[:, (2 * h + 1) * PACK_W:(2 * h + 2) * PACK_W].astype(jnp.bfloat16)
                                      .astype(jnp.float32), jnp.uint32)
        out.append(lax.bitcast_convert_type((lo >> 16) | hi, jnp.int32))
    return out


def _unpack_words(w):
    u = lax.bitcast_convert_type(w, jnp.uint32)
    lo = lax.bitcast_convert_type(u << 16, jnp.float32)
    hi = lax.bitcast_convert_type(u & jnp.uint32(0xFFFF0000), jnp.float32)
    return lo, hi


def _gather_rows(src, idx):
    n = idx.shape[0]
    if n % (SC_WINDOW * SC_TILES) != 0:
        return jnp.take(src, idx, axis=0)
    mesh = plsc.VectorSubcoreMesh(core_axis_name="core", subcore_axis_name="subcore")

    @pl.kernel(out_type=jax.ShapeDtypeStruct((n, src.shape[1]), src.dtype), mesh=mesh)
    def gather_kernel(src_hbm, idx_hbm, out_hbm):
        def step(idx_vmem, out_vmem):
            pltpu.sync_copy(src_hbm.at[idx_vmem.at[0]], out_vmem)

        pltpu.emit_pipeline(
            step, grid=(n // SC_WINDOW,),
            in_specs=[pl.BlockSpec((1, SC_WINDOW), index_map=lambda i: (0, i))],
            out_specs=[pl.BlockSpec((SC_WINDOW, src.shape[1]), index_map=lambda i: (i, 0))],
            core_axis_name=("core", "subcore"),
            dimension_semantics=(pltpu.PARALLEL,),
        )(idx_hbm, out_hbm)

    return gather_kernel(src, idx.reshape(1, n))


PER_GROUP = N_EXPERTS // N_GROUPS
PICKED = -3e38


def _ln_rows(v, g, b):
    mu = jnp.mean(v, axis=-1, keepdims=True)
    c = v - mu
    var = jnp.mean(c * c, axis=-1, keepdims=True)
    return c * lax.rsqrt(var + LN_EPS) * g + b


def _first_max(v, ids, axes, sentinel):
    best = v
    for a in axes:
        best = jnp.max(best, axis=a, keepdims=True)
    first = jnp.where(v == best, ids, sentinel)
    for a in axes:
        first = jnp.min(first, axis=a, keepdims=True)
    return best, first


def _sum_axes(v, axes):
    for a in axes:
        v = jnp.sum(v, axis=a, keepdims=True)
    return v


def _moe_pre_body(x_ref, mix_ref, g_ref, b_ref, wr_ref, br_ref, wgu_ref, wdn_ref,
                  x1_ref, xp_ref, sh_ref, eidx_ref, gate_ref, rank_ref, cnt_ref, run_ref):
    f32, bf16 = jnp.float32, jnp.bfloat16
    tm = x_ref.shape[0]

    @pl.when(pl.program_id(0) == 0)
    def _():
        run_ref[...] = jnp.zeros(run_ref.shape, f32)

    x1 = _ln_rows(ALPHA * x_ref[...] + mix_ref[...], g_ref[...], b_ref[...])
    x1_ref[...] = x1
    x1b = x1.astype(bf16)
    xp_ref[0], xp_ref[1] = _pack_rows(x1)

    h = _dot(x1b, wgu_ref[...])
    d_sh = h.shape[1] // 2
    act = (jax.nn.silu(h[:, :d_sh]) * h[:, d_sh:]).astype(bf16)
    sh_ref[...] = _dot(act, wdn_ref[...])

    s = jax.nn.sigmoid(_dot_nt(wr_ref[...], x1b)).reshape(N_GROUPS, PER_GROUP, tm)
    sb = s + br_ref[...].reshape(N_GROUPS, PER_GROUP, 1)
    shape3 = (N_GROUPS, PER_GROUP, tm)
    pid = lax.broadcasted_iota(jnp.int32, shape3, 1)
    gid = lax.broadcasted_iota(jnp.int32, (N_GROUPS, 1, tm), 0)
    eid = lax.broadcasted_iota(jnp.int32, shape3, 0) * PER_GROUP + pid
    top1, i1 = _first_max(sb, pid, (1,), PER_GROUP)
    top2 = jnp.max(jnp.where(pid == i1, PICKED, sb), axis=1, keepdims=True)
    gscore = top1 + top2
    gsel = jnp.zeros((N_GROUPS, 1, tm), f32)
    for _ in range(TOPK_GROUPS):
        _, first = _first_max(gscore, gid, (0,), N_GROUPS)
        hit = gid == first
        gsel = jnp.where(hit, 1.0, gsel)
        gscore = jnp.where(hit, PICKED, gscore)
    cand = jnp.where(gsel > 0.0, sb, -1e30)
    firsts, gates = [], []
    picked = jnp.zeros(shape3, f32)
    for _ in range(TOP_K):
        _, first = _first_max(cand, eid, (0, 1), N_EXPERTS)
        hit = eid == first
        firsts.append(first)
        gates.append(_sum_axes(jnp.where(hit, s, 0.0), (0, 1)))
        picked = jnp.where(hit, 1.0, picked)
        cand = jnp.where(hit, PICKED, cand)
    gsum = gates[0]
    for gk in gates[1:]:
        gsum = gsum + gk
    earlier = (lax.broadcasted_iota(jnp.int32, (tm, tm), 0) < lax.broadcasted_iota(jnp.int32, (tm, tm), 1))
    picked2 = picked.reshape(N_EXPERTS, tm)
    rank = run_ref[...] + _dot(picked2.astype(bf16), jnp.where(earlier, 1.0, 0.0).astype(bf16))
    run_new = run_ref[...] + jnp.sum(picked2, axis=1, keepdims=True)
    run_ref[...] = run_new
    cnt_ref[...] = jnp.broadcast_to(run_new, cnt_ref.shape)
    rank3 = rank.reshape(shape3)
    for k in range(TOP_K):
        hit = eid == firsts[k]
        eidx_ref[k:k + 1, :] = firsts[k].reshape(1, tm)
        gate_ref[k:k + 1, :] = (gates[k] / gsum * ROUTE_SCALE).reshape(1, tm)
        rank_ref[k:k + 1, :] = _sum_axes(jnp.where(hit, rank3, 0.0), (0, 1)).reshape(1, tm).astype(jnp.int32)


def _moe_pre(x, mix, g, b, w_router, b_router, w_sh_gu, w_sh_down):
    m, d = x.shape
    bf16 = jnp.bfloat16
    tm = min(m, 512)
    row = lambda i: (i, 0)
    col = lambda i: (0, i)
    fixed = lambda i: (0, 0)
    d_sh2 = w_sh_gu.shape[1]
    return pl.pallas_call(
        _moe_pre_body,
        grid=(m // tm,),
        in_specs=[pl.BlockSpec((tm, d), row), pl.BlockSpec((tm, d), row),
                  pl.BlockSpec((1, d), fixed), pl.BlockSpec((1, d), fixed),
                  pl.BlockSpec((N_EXPERTS, d), fixed), pl.BlockSpec((N_EXPERTS, 1), fixed),
                  pl.BlockSpec((d, d_sh2), fixed), pl.BlockSpec((d_sh2 // 2, d), fixed)],
        out_specs=[pl.BlockSpec((tm, d), row), pl.BlockSpec((2, tm, PACK_W), lambda i: (0, i, 0)),
                   pl.BlockSpec((tm, d), row),
                   pl.BlockSpec((TOP_K, tm), col), pl.BlockSpec((TOP_K, tm), col), pl.BlockSpec((TOP_K, tm), col),
                   pl.BlockSpec((N_EXPERTS, LANE), fixed)],
        out_shape=[jax.ShapeDtypeStruct((m, d), jnp.float32), jax.ShapeDtypeStruct((2, m, PACK_W), jnp.int32),
                   jax.ShapeDtypeStruct((m, d), jnp.float32),
                   jax.ShapeDtypeStruct((TOP_K, m), jnp.int32), jax.ShapeDtypeStruct((TOP_K, m), jnp.float32),
                   jax.ShapeDtypeStruct((TOP_K, m), jnp.int32),
                   jax.ShapeDtypeStruct((N_EXPERTS, LANE), jnp.float32)],
        scratch_shapes=[pltpu.VMEM((N_EXPERTS, 1), jnp.float32)],
        compiler_params=pltpu.CompilerParams(dimension_semantics=("arbitrary",),
                                             vmem_limit_bytes=48 * 1024 * 1024),
        name="moe_pre",
    )(x, mix, g.reshape(1, d), b.reshape(1, d), w_router.T.astype(bf16), b_router.reshape(N_EXPERTS, 1),
      w_sh_gu.astype(bf16), w_sh_down.astype(bf16))


def _moe_expert_body(exp_ref, first_ref, active_ref, xs_ref, wgu_ref, wdn_ref, y_ref, wgu_bf, wdn_bf):
    i = pl.program_id(0)
    bf16 = jnp.bfloat16

    @pl.when(first_ref[i] == 1)
    def _():
        wgu_bf[...] = wgu_ref[0].astype(bf16)
        wdn_bf[...] = wdn_ref[0].astype(bf16)

    @pl.when(active_ref[i] == 1)
    def _():
        h = None
        for hw in range(2):
            for q, xq in enumerate(_unpack_words(xs_ref[hw])):
                r0 = (2 * hw + q) * PACK_W
                part = _dot(xq.astype(bf16), wgu_bf[r0:r0 + PACK_W, :])
                h = part if h is None else h + part
        d_e = h.shape[1] // 2
        act = (jax.nn.silu(h[:, :d_e]) * h[:, d_e:]).astype(bf16)
        y_ref[0], y_ref[1] = _pack_rows(_dot(act, wdn_bf[...]))

    @pl.when(active_ref[i] == 0)
    def _():
        y_ref[...] = jnp.zeros(y_ref.shape, y_ref.dtype)


def _moe_experts(xs, blk_exp, blk_first, blk_active, w_exp_gu, w_exp_down, bm):
    n_slots = xs.shape[1]
    d = w_exp_gu.shape[1]
    n_blk = n_slots // bm
    d_e2 = w_exp_gu.shape[2]
    words = lambda i, e, f, a: (0, i, 0)
    grid_spec = pltpu.PrefetchScalarGridSpec(
        num_scalar_prefetch=3,
        grid=(n_blk,),
        in_specs=[pl.BlockSpec((2, bm, PACK_W), words),
                  pl.BlockSpec((1, d, d_e2), lambda i, e, f, a: (e[i], 0, 0)),
                  pl.BlockSpec((1, d_e2 // 2, d), lambda i, e, f, a: (e[i], 0, 0))],
        out_specs=pl.BlockSpec((2, bm, PACK_W), words),
        scratch_shapes=[pltpu.VMEM((d, d_e2), jnp.bfloat16), pltpu.VMEM((d_e2 // 2, d), jnp.bfloat16)])
    return pl.pallas_call(
        _moe_expert_body,
        grid_spec=grid_spec,
        out_shape=jax.ShapeDtypeStruct((2, n_slots, PACK_W), jnp.int32),
        compiler_params=pltpu.CompilerParams(dimension_semantics=("arbitrary",),
                                             vmem_limit_bytes=48 * 1024 * 1024),
        name="moe_experts",
    )(blk_exp, blk_first, blk_active, xs, w_exp_gu, w_exp_down)


def _combine_ln_body(x_ref, yg_ref, gt_ref, sh_ref, g_ref, b_ref, o_ref):
    gt = gt_ref[...]
    parts = []
    for hw in range(2):
        lo_acc = hi_acc = None
        for k in range(TOP_K):
            lo, hi = _unpack_words(yg_ref[hw, k])
            gk = gt[:, k:k + 1]
            lo_acc = lo * gk if lo_acc is None else lo_acc + lo * gk
            hi_acc = hi * gk if hi_acc is None else hi_acc + hi * gk
        parts += [lo_acc, hi_acc]
    routed = jnp.concatenate(parts, axis=1)
    o_ref[...] = _ln_rows(ALPHA * x_ref[...] + (routed + sh_ref[...]), g_ref[...], b_ref[...])


def _combine_ln(x, yg, gate_t, shared, g, b):
    m, d = x.shape
    tm = min(m, 256)
    row = lambda i: (i, 0)
    fixed = lambda i: (0, 0)
    return pl.pallas_call(
        _combine_ln_body,
        grid=(m // tm,),
        in_specs=[pl.BlockSpec((tm, d), row), pl.BlockSpec((2, TOP_K, tm, PACK_W), lambda i: (0, 0, i, 0)),
                  pl.BlockSpec((tm, TOP_K), row), pl.BlockSpec((tm, d), row),
                  pl.BlockSpec((1, d), fixed), pl.BlockSpec((1, d), fixed)],
        out_specs=pl.BlockSpec((tm, d), row),
        out_shape=jax.ShapeDtypeStruct((m, d), jnp.float32),
        compiler_params=pltpu.CompilerParams(dimension_semantics=("arbitrary",)),
        name="combine_ln",
    )(x, yg, gate_t, shared, g.reshape(1, d), b.reshape(1, d))


def _moe_layer(x, mix, ln1_g, ln1_b, ln2_g, ln2_b, w_router, b_router, w_exp_gu, w_exp_down, w_sh_gu, w_sh_down):
    m, d = x.shape
    x1, xp, shared, eidx, gate8, rank8, counts = _moe_pre(x, mix, ln1_g, ln1_b, w_router, b_router,
                                                           w_sh_gu, w_sh_down)
    bm = 512 if m * TOP_K >= 512 * N_EXPERTS else MOE_BLK
    n_blk = (m * TOP_K) // bm + N_EXPERTS
    counts = counts[:, 0].astype(jnp.int32)
    padded = (counts + bm - 1) // bm * bm
    pad_end = jnp.cumsum(padded)
    pad_start = pad_end - padded
    start_of = jnp.sum(jnp.where(eidx[:, :, None] == jnp.arange(N_EXPERTS), pad_start, 0), axis=-1)
    dest = (start_of + rank8).reshape(-1)
    tok = jnp.tile(jnp.arange(m, dtype=jnp.int32), TOP_K)
    slot_tok = (jnp.arange(n_blk * bm, dtype=jnp.int32) % m).at[dest].set(tok)
    blk_start = jnp.arange(n_blk, dtype=jnp.int32) * bm
    blk_exp = jnp.minimum(jnp.sum(pad_end[None, :] <= blk_start[:, None], axis=1), N_EXPERTS - 1).astype(jnp.int32)
    blk_active = (blk_start < pad_end[-1]).astype(jnp.int32)
    blk_first = jnp.concatenate([jnp.ones((1,), jnp.int32), (blk_exp[1:] != blk_exp[:-1]).astype(jnp.int32)])
    n_slots = n_blk * bm
    xs = _gather_rows(xp.reshape(2 * m, PACK_W), jnp.concatenate([slot_tok, slot_tok + m]))
    y = _moe_experts(xs.reshape(2, n_slots, PACK_W), blk_exp, blk_first, blk_active, w_exp_gu, w_exp_down, bm)
    yg = _gather_rows(y.reshape(2 * n_slots, PACK_W), jnp.concatenate([dest, dest + n_slots]))
    return _combine_ln(x1, yg.reshape(2, TOP_K, m, PACK_W), gate8.T, shared, ln2_g, ln2_b)


def _trunk(x, pos, gla_state, nsa_cache, page_table, win_buf, conv_buf,
           w_in_ab, w_gla_gate, b_gla_gate, gla_norm_g, w_cmp_pool, w_out_ab,
           w_pw1, b_pw1, w_dw, b_dw, conv_ln_g, conv_ln_b, w_pw2, b_pw2,
           ln_g, ln_b, w_router, b_router, w_exp_gu, w_exp_down, w_sh_gu, w_sh_down):
    new_gla, new_rows, new_win, new_conv = [], [], [], []
    for layer in range(DEPTH):
        i = layer // 2
        if layer % 2 == 0:
            mix, s_a, rows, win = _ab_mixer(
                x, pos, w_in_ab[i], w_gla_gate[i], b_gla_gate[i], gla_norm_g[i], w_cmp_pool[i], w_out_ab[i],
                None if gla_state is None else gla_state[i],
                None if nsa_cache is None else nsa_cache[i], page_table,
                None if win_buf is None else win_buf[i])
            new_gla.append(s_a)
            new_rows.append(rows)
            new_win.append(win)
        else:
            mix, cb = _conv_module(x, None if conv_buf is None else conv_buf[i], w_pw1[i], b_pw1[i],
                                   w_dw[i], b_dw[i], conv_ln_g[i], conv_ln_b[i], w_pw2[i], b_pw2[i])
            new_conv.append(cb)
        bsz, t_, d = x.shape
        x = _moe_layer(x.reshape(-1, d), mix.reshape(-1, d), ln_g[layer, 0], ln_b[layer, 0],
                       ln_g[layer, 1], ln_b[layer, 1], w_router[layer], b_router[layer],
                       w_exp_gu[layer], w_exp_down[layer], w_sh_gu[layer], w_sh_down[layer]).reshape(bsz, t_, d)
    return x, jnp.stack(new_gla), jnp.stack(new_rows), jnp.stack(new_win), jnp.stack(new_conv)


def kernel(x_prompt, x_sample, state_gla, cache_nsa_kv, state_nsa_win, state_conv, page_table,
           w_in_ab, w_gla_gate, b_gla_gate, gla_norm_g, w_cmp_pool, w_out_ab,
           w_pw1, b_pw1, w_dw, b_dw, conv_ln_g, conv_ln_b, w_pw2, b_pw2,
           ln_g, ln_b, w_router, b_router, w_exp_gu, w_exp_down, w_sh_gu, w_sh_down):
    weights = (w_in_ab, w_gla_gate, b_gla_gate, gla_norm_g, w_cmp_pool, w_out_ab,
               w_pw1, b_pw1, w_dw, b_dw, conv_ln_g, conv_ln_b, w_pw2, b_pw2,
               ln_g, ln_b, w_router, b_router, w_exp_gu, w_exp_down, w_sh_gu, w_sh_down)
    past_len = page_table.shape[1] * PAGE_SIZE
    pos_p = jnp.arange(x_prompt.shape[1])
    pos_s = past_len + jnp.arange(x_sample.shape[1])
    y_prompt, gla_p, rows_p, win_p, conv_p = _trunk(x_prompt, pos_p, None, None, None, None, None, *weights)
    y_sample, gla_s, rows_s, win_s, conv_s = _trunk(x_sample, pos_s, state_gla, cache_nsa_kv, page_table,
                                                    state_nsa_win, state_conv, *weights)
    return (y_prompt, y_sample, gla_p, gla_s, rows_p, rows_s, win_p, win_s, conv_p, conv_s)
```

```python
import functools
import math

import jax
import jax.numpy as jnp
import numpy as np
from jax import lax
from jax.experimental import pallas as pl
from jax.experimental.pallas import tpu as pltpu
from jax.experimental.pallas import tpu_sc as plsc

D_MODEL = 1024
DEPTH = 2
PAGE_SIZE = 128

GLA_HEADS = 4
GLA_DV = D_MODEL // 2 // GLA_HEADS
GLA_DK = GLA_DV // 2
GLA_RANK = 16
GLA_TAU = 16.0
GLA_CHUNK = 64

NSA_HEADS = 8
NSA_KV_HEADS = 2
NSA_GROUP = NSA_HEADS // NSA_KV_HEADS
HEAD_DIM = D_MODEL // 2 // NSA_HEADS
CMP_BLK = 32
CMP_STRIDE = 16
SEL_BLK = 64
SEL_TOPN = 16
WINDOW = 512
Q_BLK = 128
FORCE_BONUS = 100.0
ROPE_DIM = HEAD_DIM // 4
ROPE_THETA = 500000.0

GLA_SIZES = (GLA_HEADS * GLA_DK, GLA_HEADS * GLA_DK, GLA_HEADS * GLA_DV, GLA_HEADS * GLA_DV, GLA_RANK)
NSA_SIZES = (NSA_HEADS * HEAD_DIM, 6 * NSA_KV_HEADS * HEAD_DIM, 3 * NSA_HEADS)

CONV_W = 31
D_CONV = D_MODEL

N_EXPERTS = 64
N_GROUPS = 8
TOPK_GROUPS = 4
TOP_K = 8
D_EXPERT = 256
ROUTE_SCALE = 2.5
MOE_BLK = 128

ALPHA = (2 * DEPTH) ** 0.25
LN_EPS = 1e-5

LANE = 128


def _dot(a, b):
    return jnp.dot(a, b, preferred_element_type=jnp.float32)


def _dot_nt(a, b):
    return lax.dot_general(a, b, (((1,), (1,)), ((), ())), preferred_element_type=jnp.float32)


def _mm_body(x_ref, w_ref, o_ref):
    o_ref[...] = _dot(x_ref[...].astype(jnp.bfloat16), w_ref[...].astype(jnp.bfloat16))


def _mm(x, w, keep_pad=False):
    m, k = x.shape
    n = w.shape[1]
    n_pad = -(-n // LANE) * LANE
    w = w.astype(jnp.bfloat16)
    if n_pad != n:
        w = jnp.pad(w, ((0, 0), (0, n_pad - n)))
    tm = min(m, 512)
    out = pl.pallas_call(
        _mm_body,
        grid=(m // tm,),
        in_specs=[pl.BlockSpec((tm, k), lambda i: (i, 0)),
                  pl.BlockSpec((k, n_pad), lambda i: (0, 0))],
        out_specs=pl.BlockSpec((tm, n_pad), lambda i: (i, 0)),
        out_shape=jax.ShapeDtypeStruct((m, n_pad), jnp.float32),
        compiler_params=pltpu.CompilerParams(dimension_semantics=("arbitrary",),
                                             vmem_limit_bytes=48 * 1024 * 1024),
        name="mm",
    )(x, w)
    return out if keep_pad or n_pad == n else out[:, :n]


def _mm3(x, w):
    b, t, d = x.shape
    return _mm(x.reshape(b * t, d), w).reshape(b, t, -1)


def _split_cols(h, sizes):
    return jnp.split(h, np.cumsum(sizes)[:-1].tolist(), axis=-1)


def _layer_norm(x, g, b):
    mu = x.mean(-1, keepdims=True)
    var = jnp.square(x - mu).mean(-1, keepdims=True)
    return (x - mu) * lax.rsqrt(var + LN_EPS) * g + b


def _rms_norm(x, g):
    return x * lax.rsqrt(jnp.mean(x * x, -1, keepdims=True) + LN_EPS) * g


def _partial_rope(x, pos):
    half = ROPE_DIM // 2
    inv_freq = jnp.power(ROPE_THETA, -jnp.arange(half, dtype=jnp.float32) / half)
    ang = pos.astype(jnp.float32)[:, None] * inv_freq
    ang = ang.reshape(ang.shape[0], *([1] * (x.ndim - 3)), half)
    cos, sin = jnp.cos(ang), jnp.sin(ang)
    x1 = x[..., :half]
    x2 = x[..., half:ROPE_DIM]
    rot = jnp.concatenate([x1 * cos - x2 * sin, x2 * cos + x1 * sin], -1)
    return jnp.concatenate([rot, x[..., ROPE_DIM:]], -1)


def _masked_softmax(s, mask):
    s = jnp.where(mask, s, -jnp.inf)
    m = jnp.max(s, axis=-1, keepdims=True)
    m = jnp.where(jnp.isfinite(m), m, 0.0)
    p = jnp.exp(s - m)
    return p / jnp.maximum(p.sum(-1, keepdims=True), 1e-30)


def _gla_recurrence(q, k, v, log_a, s0):
    bsz, t_, nh, _ = q.shape
    c = math.gcd(t_, GLA_CHUNK)
    n = t_ // c

    def chunks(a):
        return jnp.moveaxis(a.reshape(bsz, n, c, *a.shape[2:]), 1, 0)

    causal = jnp.tril(jnp.ones((c, c), dtype=bool))[None, :, :, None, None]

    def step(s, inp):
        qc, kc, vc, lc = inp
        bc = jnp.cumsum(lc, axis=1)
        decay = jnp.exp(jnp.where(causal, bc[:, :, None] - bc[:, None, :], -jnp.inf))
        attn = jnp.einsum('bijhd,bjhd->bhij', qc[:, :, None] * decay, kc)
        o = jnp.einsum('bhij,bjhe->bihe', attn, vc) + jnp.einsum('bihd,bhde->bihe', qc * jnp.exp(bc), s)
        bl = bc[:, -1]
        s = jnp.exp(bl)[..., None] * s + jnp.einsum('bjhd,bjhe->bhde', kc * jnp.exp(bl[:, None] - bc), vc)
        return s, o

    s_fin, o = lax.scan(step, s0, (chunks(q), chunks(k), chunks(v), chunks(log_a)))
    return jnp.moveaxis(o, 0, 1).reshape(bsz, t_, nh, -1), s_fin


def _compress(k, v, w_pool):
    bsz, length = k.shape[:2]
    n_sub = length // CMP_STRIDE

    def pool(a, w):
        sub = a[:, :n_sub * CMP_STRIDE].reshape(bsz, n_sub, CMP_STRIDE, *a.shape[2:])
        first = jnp.einsum('bnjhd,j->bnhd', sub, w[:CMP_STRIDE])
        second = jnp.einsum('bnjhd,j->bnhd', sub, w[CMP_STRIDE:])
        return first[:, :-1] + second[:, 1:]

    cend = jnp.arange(n_sub - 1) * CMP_STRIDE + CMP_BLK - 1
    return pool(k, w_pool[0]), pool(v, w_pool[1]), cend


def _to_sel_blocks(a, n_sel):
    bsz, length = a.shape[:2]
    a = jnp.pad(a, ((0, 0), (0, n_sel * SEL_BLK - length), (0, 0), (0, 0)))
    return a.reshape(bsz, n_sel, SEL_BLK, NSA_KV_HEADS, HEAD_DIM).transpose(0, 3, 1, 2, 4)


def _nsa_attend(q_raw, q_rot, qpos, gates, kc, vc, cend, ksb, vsb, kw, vw, kwpos):
    scale = HEAD_DIM ** -0.5
    bsz, tq = q_raw.shape[:2]
    n_cmp, n_sel = kc.shape[1], ksb.shape[2]
    s_c = jnp.einsum('bqhgd,bnhd->bhgqn', q_raw, kc) * scale
    p_c = _masked_softmax(s_c, cend[None, :] <= qpos[:, None])
    o_c = jnp.einsum('bhgqn,bnhd->bqhgd', p_c, vc)
    ratio = SEL_BLK // CMP_STRIDE
    imp = p_c.sum(axis=2)
    imp = jnp.pad(imp, ((0, 0), (0, 0), (0, 0), (1, ratio * (n_sel + 1) - 1 - n_cmp)))
    imp = imp.reshape(bsz, NSA_KV_HEADS, tq, n_sel + 1, ratio)
    imp_s = imp[..., :n_sel, :].sum(-1) + imp[..., 1:, 0]
    blk = jnp.arange(n_sel)[None, :]
    cur = (qpos // SEL_BLK)[:, None]
    valid = blk * SEL_BLK <= qpos[:, None]
    forced = (blk == 0) | (blk == cur) | (blk == cur - 1)
    score = jnp.where(valid, imp_s + jnp.where(forced, FORCE_BONUS, 0.0), -jnp.inf)
    k_top = min(SEL_TOPN, n_sel)
    _, sel = lax.top_k(score, k_top)
    take = jax.vmap(jax.vmap(lambda blocks, idx: blocks[idx]))
    ks = take(ksb, sel).reshape(bsz, NSA_KV_HEADS, tq, k_top * SEL_BLK, HEAD_DIM)
    vs = take(vsb, sel).reshape(bsz, NSA_KV_HEADS, tq, k_top * SEL_BLK, HEAD_DIM)
    kpos = (sel[..., None] * SEL_BLK + jnp.arange(SEL_BLK)).reshape(bsz, NSA_KV_HEADS, tq, k_top * SEL_BLK)
    s_s = jnp.einsum('bqhgd,bhqkd->bhgqk', q_rot, ks) * scale
    p_s = _masked_softmax(s_s, (kpos <= qpos[:, None])[:, :, None])
    o_s = jnp.einsum('bhgqk,bhqkd->bqhgd', p_s, vs)
    s_w = jnp.einsum('bqhgd,bkhd->bhgqk', q_rot, kw) * scale
    kp, qp = kwpos[None, :], qpos[:, None]
    p_w = _masked_softmax(s_w, (kp <= qp) & (kp > qp - WINDOW) & (kp >= 0))
    o_w = jnp.einsum('bhgqk,bkhd->bqhgd', p_w, vw)
    return gates[..., 0:1] * o_c + gates[..., 1:2] * o_s + gates[..., 2:3] * o_w


NSA_ROWS = NSA_GROUP * Q_BLK
SEL_KT = 1024
N_SELB = 128
MASKED = -1e9
WIN_KEYS = WINDOW + Q_BLK
KK_W = 2 * HEAD_DIM + N_SELB


def _nsa_prompt_body(qr_ref, qo_ref, kc_ref, vct_ref, kk_ref, vvt_ref, g_ref, o_ref,
                     imp_ref, m_ref, l_ref, acc_ref):
    f32, bf16 = jnp.float32, jnp.bfloat16
    qb = pl.program_id(2)
    q0 = qb * Q_BLK
    qr_t = qr_ref[0, 0, 0]
    qo_t = qo_ref[0, 0, 0]
    n_cmp = kc_ref.shape[2]

    s_c = _dot(kc_ref[0, 0], qr_t)
    n_idx = lax.broadcasted_iota(jnp.int32, (n_cmp, NSA_ROWS), 0)
    qpos_c = q0 + (lax.broadcasted_iota(jnp.int32, (n_cmp, NSA_ROWS), 1) & (Q_BLK - 1))
    cmask = (n_idx * CMP_STRIDE + (CMP_BLK - 1)) <= qpos_c
    s_c = jnp.where(cmask, s_c, MASKED)
    m_c = jnp.max(s_c, axis=0, keepdims=True)
    p_c = jnp.where(cmask, jnp.exp(s_c - m_c), 0.0)
    p_c = p_c / jnp.maximum(jnp.sum(p_c, axis=0, keepdims=True), 1e-30)
    o_ct = _dot(vct_ref[0, 0], p_c.astype(bf16))

    imp = (p_c[:, 0:Q_BLK] + p_c[:, Q_BLK:2 * Q_BLK]) + p_c[:, 2 * Q_BLK:3 * Q_BLK] + p_c[:, 3 * Q_BLK:]
    imp_ref[0:8, :] = jnp.zeros((8, Q_BLK), f32)
    imp_ref[8:8 + n_cmp, :] = imp
    ratio = SEL_BLK // CMP_STRIDE
    n_selb = n_cmp // ratio
    imp_s = imp_ref[pl.ds(7, n_selb, stride=ratio), :]
    for r in range(ratio):
        imp_s = imp_s + imp_ref[pl.ds(8 + r, n_selb, stride=ratio), :]
    blk = lax.broadcasted_iota(jnp.int32, (n_selb, Q_BLK), 0)
    qpos_s = q0 + lax.broadcasted_iota(jnp.int32, (n_selb, Q_BLK), 1)
    cur = lax.shift_right_logical(qpos_s, int(math.log2(SEL_BLK)))
    valid = blk * SEL_BLK <= qpos_s
    forced = (blk == 0) | (blk == cur) | (blk == cur - 1)
    score = jnp.where(valid, imp_s + jnp.where(forced, FORCE_BONUS, 0.0), -1e30)
    picked = jnp.zeros((n_selb, Q_BLK), f32)
    for _ in range(SEL_TOPN):
        best = jnp.max(score, axis=0, keepdims=True)
        first = jnp.min(jnp.where(score == best, blk, n_selb), axis=0, keepdims=True)
        hit = blk == first
        picked = jnp.where(hit, 1.0, picked)
        score = jnp.where(hit, -3e38, score)
    selb_t = jnp.where(valid, picked, 0.0)
    if n_selb < N_SELB:
        selb_t = jnp.concatenate([selb_t, jnp.zeros((N_SELB - n_selb, Q_BLK), f32)], axis=0)
    selb_t = ((selb_t - 1.0) * (-MASKED)).astype(bf16)
    selb_t = jnp.concatenate([selb_t] * NSA_GROUP, axis=1)

    zeros_q = jnp.zeros((HEAD_DIM, NSA_ROWS), bf16)
    q_sel = jnp.concatenate([qo_t, zeros_q, selb_t], axis=0)
    q_win = jnp.concatenate([zeros_q, qo_t, jnp.zeros((N_SELB, NSA_ROWS), bf16)], axis=0)
    qpos_r = q0 + (lax.broadcasted_iota(jnp.int32, (1, NSA_ROWS), 1) & (Q_BLK - 1))

    def v_tiles(first, count):
        return jnp.concatenate([vvt_ref[0, 0, first + j] for j in range(count)], axis=1)

    m_ref[...] = jnp.full(m_ref.shape, MASKED, f32)
    l_ref[...] = jnp.zeros(l_ref.shape, f32)
    acc_ref[...] = jnp.zeros(acc_ref.shape, f32)

    def sel_tile(k0, kt, causal):
        s = _dot(kk_ref[0, 0, pl.ds(k0, kt), :], q_sel)
        if causal:
            kpos = k0 + lax.broadcasted_iota(jnp.int32, (kt, NSA_ROWS), 0)
            s = jnp.where(kpos <= qpos_r, s, MASKED)
        m_old = m_ref[...]
        m_new = jnp.maximum(m_old, jnp.max(s, axis=0, keepdims=True))
        alpha = jnp.exp(m_old - m_new)
        p = jnp.exp(s - m_new)
        l_ref[...] = alpha * l_ref[...] + jnp.sum(p, axis=0, keepdims=True)
        vt = v_tiles(k0 // Q_BLK, kt // Q_BLK)
        acc_ref[...] = alpha * acc_ref[...] + _dot(vt, p.astype(bf16))
        m_ref[...] = m_new

    n_full = q0 // SEL_KT

    def full_step(t, c):
        sel_tile(pl.multiple_of(t * SEL_KT, SEL_KT), SEL_KT, False)
        return c

    lax.fori_loop(0, n_full, full_step, 0)
    sel_tile(pl.multiple_of(n_full * SEL_KT, SEL_KT), SEL_KT, True)
    o_st = acc_ref[0:HEAD_DIM, :] / l_ref[...]

    w0 = pl.multiple_of(jnp.maximum(q0 - WINDOW, 0), Q_BLK)
    s_w = _dot(kk_ref[0, 0, pl.ds(w0, WIN_KEYS), :], q_win)
    kpos_w = w0 + lax.broadcasted_iota(jnp.int32, (WIN_KEYS, NSA_ROWS), 0)
    s_w = jnp.where((kpos_w <= qpos_r) & (kpos_w > qpos_r - WINDOW), s_w, MASKED)
    p_w = jnp.exp(s_w - jnp.max(s_w, axis=0, keepdims=True))
    l_w = jnp.sum(p_w, axis=0, keepdims=True)
    acc_w = _dot(v_tiles(w0 // Q_BLK, WIN_KEYS // Q_BLK), p_w.astype(bf16))
    o_wt = acc_w[HEAD_DIM:2 * HEAD_DIM, :] / l_w

    g = g_ref[0, 0, 0]
    out_t = g[0:1, :] * o_ct + g[1:2, :] * o_st + g[2:3, :] * o_wt
    o_ref[0] = jnp.concatenate([out_t[:, g_ * Q_BLK:(g_ + 1) * Q_BLK] for g_ in range(NSA_GROUP)], axis=0).T


def _nsa_prompt(qr, qo, gt, kc_p, vct, kk, vvt):
    bsz, _, nqb = qr.shape[:3]
    t_ = nqb * Q_BLK
    n_cmp = kc_p.shape[2]
    per_blk = lambda b, h, i: (b, h, i, 0, 0)
    per_head = lambda b, h, i: (b, h, 0, 0)
    return pl.pallas_call(
        _nsa_prompt_body,
        grid=(bsz, NSA_KV_HEADS, nqb),
        in_specs=[pl.BlockSpec((1, 1, 1, HEAD_DIM, NSA_ROWS), per_blk),
                  pl.BlockSpec((1, 1, 1, HEAD_DIM, NSA_ROWS), per_blk),
                  pl.BlockSpec((1, 1, n_cmp, HEAD_DIM), per_head),
                  pl.BlockSpec((1, 1, HEAD_DIM, n_cmp), per_head),
                  pl.BlockSpec((1, 1, t_, KK_W), per_head),
                  pl.BlockSpec((1, 1, nqb, 2 * HEAD_DIM, Q_BLK), lambda b, h, i: (b, h, 0, 0, 0)),
                  pl.BlockSpec((1, 1, 1, 3, NSA_ROWS), per_blk)],
        out_specs=pl.BlockSpec((1, Q_BLK, NSA_GROUP * HEAD_DIM), lambda b, h, i: (b, i, h)),
        out_shape=jax.ShapeDtypeStruct((bsz, t_, NSA_HEADS * HEAD_DIM), jnp.float32),
        scratch_shapes=[pltpu.VMEM((8 + n_cmp, Q_BLK), jnp.float32),
                        pltpu.VMEM((1, NSA_ROWS), jnp.float32),
                        pltpu.VMEM((1, NSA_ROWS), jnp.float32),
                        pltpu.VMEM((2 * HEAD_DIM, NSA_ROWS), jnp.float32)],
        compiler_params=pltpu.CompilerParams(
            dimension_semantics=("arbitrary", "arbitrary", "arbitrary"),
            vmem_limit_bytes=48 * 1024 * 1024),
        name="nsa_prompt",
    )(qr, qo, kc_p, vct, kk, vvt, gt)


GLA_SUB = 16
GLA_QK = GLA_HEADS * GLA_DK
GLA_V = GLA_HEADS * GLA_DV


def _dot_tn(a, b):
    return lax.dot_general(a, b, (((0,), (0,)), ((), ())), preferred_element_type=jnp.float32)


def _gla_body(q_ref, k_ref, v_ref, gr_ref, glr_ref, wg_ref, bg_ref, ng_ref, s0_ref, exp_ref, bd_ref,
              o_ref, sfin_ref, st_ref, b_ref, qd_ref, *, t_valid):
    f32, bf16 = jnp.float32, jnp.bfloat16
    tt = q_ref.shape[1]
    ti = pl.program_id(1)

    @pl.when(ti == 0)
    def _():
        st_ref[...] = s0_ref[0]

    row = lax.broadcasted_iota(jnp.int32, (tt, 1), 0)
    z = _dot(glr_ref[0][:, :GLA_RANK].astype(bf16), wg_ref[...]) + bg_ref[...]
    la = (jnp.minimum(z, 0.0) - jnp.log1p(jnp.exp(-jnp.abs(z)))) * (1.0 / GLA_TAU)
    la = jnp.where(ti * tt + row < t_valid, la, 0.0)
    seg = row & (GLA_SUB - 1)
    b = la
    for s in (1, 2, 4, 8):
        b = b + jnp.where(seg >= s, pltpu.roll(b, s, axis=0), 0.0)
    q = q_ref[0] * (GLA_DK ** -0.5)
    k = k_ref[0]
    v = v_ref[0]
    o = _dot((q * k).astype(bf16), exp_ref[...]) * v
    for d in range(1, GLA_SUB):
        decay = jnp.exp(jnp.minimum(b - pltpu.roll(b, d, axis=0), 0.0))
        w = jnp.where(seg >= d, q * pltpu.roll(k, d, axis=0) * decay, 0.0)
        o = o + _dot(w.astype(bf16), exp_ref[...]) * pltpu.roll(v, d, axis=0)
    o_ref[0] = o
    b_ref[...] = b
    qd_ref[...] = (q * jnp.exp(b)).astype(bf16)

    def block_step(c, carry):
        rows = pl.ds(pl.multiple_of(c * GLA_SUB, GLA_SUB), GLA_SUB)
        st = st_ref[...]
        o_ref[0, rows, :] += _dot_nt(qd_ref[rows, :], st.astype(bf16))
        bc = b_ref[rows, :]
        bl = bc[GLA_SUB - 1:GLA_SUB, :]
        kc = (k_ref[0, rows, :] * jnp.exp(bl - bc)).astype(bf16)
        upd = _dot_tn(v_ref[0, rows, :].astype(bf16), kc)
        st_ref[...] = st * jnp.exp(bl) + upd * bd_ref[...]
        return carry

    lax.fori_loop(0, tt // GLA_SUB, block_step, 0)
    sfin_ref[0] = st_ref[...]
    gr = gr_ref[0]
    gate = gr * jax.nn.sigmoid(gr)
    for h in range(GLA_HEADS):
        cols = slice(h * GLA_DV, (h + 1) * GLA_DV)
        oh = o_ref[0, :, cols]
        ms = jnp.mean(oh * oh, axis=-1, keepdims=True)
        o_ref[0, :, cols] = oh * lax.rsqrt(ms + LN_EPS) * ng_ref[...] * gate[:, cols]


def _gla(h, w_gla_gate, b_gla_gate, gla_norm_g, gla_state):
    bsz, t_, n_in = h.shape
    tp = -(-t_ // GLA_SUB) * GLA_SUB
    if tp != t_:
        h = jnp.pad(h, ((0, 0), (0, tp - t_), (0, 0)))
    tt = min(tp, 256)
    heads = np.arange(GLA_HEADS)
    expand = np.repeat(np.repeat(np.eye(GLA_HEADS, dtype=np.float32), GLA_DK, 0), GLA_DV, 1)
    bdmask = jnp.asarray(expand.T)
    if gla_state is None:
        s0 = jnp.zeros((bsz, GLA_V, GLA_QK), jnp.float32)
    else:
        s0 = jnp.zeros((bsz, GLA_HEADS, GLA_DV, GLA_HEADS, GLA_DK), jnp.float32)
        s0 = s0.at[:, heads, :, heads, :].set(gla_state.transpose(1, 0, 3, 2)).reshape(bsz, GLA_V, GLA_QK)
    tile = lambda width, blk: pl.BlockSpec((1, tt, width), lambda b, i: (b, i, blk))
    fixed2 = lambda shape: pl.BlockSpec(shape, lambda b, i: (0, 0))
    per_b = pl.BlockSpec((1, GLA_V, GLA_QK), lambda b, i: (b, 0, 0))
    o, s_t = pl.pallas_call(
        functools.partial(_gla_body, t_valid=t_),
        grid=(bsz, tp // tt),
        in_specs=[tile(GLA_QK, 0), tile(GLA_QK, 1), tile(GLA_V, 1), tile(GLA_V, 2),
                  tile(LANE, (2 * GLA_QK + 2 * GLA_V + NSA_SIZES[0] + NSA_SIZES[1]) // LANE),
                  fixed2((GLA_RANK, GLA_QK)), fixed2((1, GLA_QK)), fixed2((1, GLA_DV)), per_b,
                  fixed2((GLA_QK, GLA_V)), fixed2((GLA_V, GLA_QK))],
        out_specs=[pl.BlockSpec((1, tt, GLA_V), lambda b, i: (b, i, 0)), per_b],
        out_shape=[jax.ShapeDtypeStruct((bsz, tp, GLA_V), jnp.float32),
                   jax.ShapeDtypeStruct((bsz, GLA_V, GLA_QK), jnp.float32)],
        scratch_shapes=[pltpu.VMEM((GLA_V, GLA_QK), jnp.float32), pltpu.VMEM((tt, GLA_QK), jnp.float32),
                        pltpu.VMEM((tt, GLA_QK), jnp.bfloat16)],
        compiler_params=pltpu.CompilerParams(dimension_semantics=("arbitrary", "arbitrary"),
                                             vmem_limit_bytes=48 * 1024 * 1024),
        name="gla",
    )(h, h, h, h, h, w_gla_gate.astype(jnp.bfloat16), b_gla_gate.reshape(1, GLA_QK),
      gla_norm_g.reshape(1, GLA_DV), s0, jnp.asarray(expand, jnp.bfloat16), bdmask)
    s_new = s_t.reshape(bsz, GLA_HEADS, GLA_DV, GLA_HEADS, GLA_DK)[:, heads, :, heads, :]
    return o[:, :t_], s_new.transpose(1, 0, 3, 2)


COL_NQ = 2 * GLA_QK + 2 * GLA_V
COL_NKV = COL_NQ + NSA_SIZES[0]
COL_TAIL = COL_NKV + NSA_SIZES[1]
TAIL_GATE = GLA_RANK
_ORIG = np.cumsum((0,) + GLA_SIZES + NSA_SIZES)
IN_AB_PERM = np.concatenate([np.arange(_ORIG[0], _ORIG[4]), np.arange(_ORIG[5], _ORIG[7]),
                             np.arange(_ORIG[4], _ORIG[5]), np.arange(_ORIG[7], _ORIG[8])])
SUBS = Q_BLK // CMP_STRIDE


def _nsa_prep_body(nq_ref, kv0_ref, kv1_ref, kv2_ref, tail_ref, rc_ref, ru_ref, rd_ref, pool_ref,
                   rows_ref, win_ref, kk_ref, vvt_ref, qr_ref, qo_ref, g_ref, pooled_ref):
    bf16 = jnp.bfloat16
    q0 = pl.program_id(1) * Q_BLK
    kv_w = NSA_KV_HEADS * HEAD_DIM

    def rope(x):
        reps = x.shape[1] // LANE
        wide = lambda r: jnp.concatenate([r[...]] * reps, axis=1) if reps > 1 else r[...]
        half = ROPE_DIM // 2
        return (x * wide(rc_ref) + pltpu.roll(x, half, axis=1) * wide(ru_ref)
                + pltpu.roll(x, x.shape[1] - half, axis=1) * wide(rd_ref))

    kv0, kv1, kv2 = kv0_ref[0], kv1_ref[0], kv2_ref[0]
    k_sel, v_sel = rope(kv1[:, :kv_w]), kv1[:, kv_w:]
    k_win, v_win = rope(kv2[:, :kv_w]), kv2[:, kv_w:]
    rows_ref[0] = jnp.concatenate([kv0, k_sel, v_sel], axis=1)
    win_ref[0] = jnp.concatenate([k_win, v_win], axis=1)
    blk_id = lax.shift_right_logical(q0 + lax.broadcasted_iota(jnp.int32, (Q_BLK, N_SELB), 0),
                                     int(math.log2(SEL_BLK)))
    onehot = jnp.where(lax.broadcasted_iota(jnp.int32, (Q_BLK, N_SELB), 1) == blk_id, 1.0, 0.0).astype(bf16)
    q = nq_ref[0] * (HEAD_DIM ** -0.5)
    q_rot = rope(q)
    gates_t = jax.nn.sigmoid(tail_ref[0]).T
    for h in range(NSA_KV_HEADS):
        hs = slice(h * HEAD_DIM, (h + 1) * HEAD_DIM)
        kk_ref[0, h] = jnp.concatenate([k_sel[:, hs].astype(bf16), k_win[:, hs].astype(bf16), onehot], axis=1)
        vvt_ref[0, h, 0] = jnp.concatenate([v_sel[:, hs], v_win[:, hs]], axis=1).T.astype(bf16)
        gw = NSA_GROUP * HEAD_DIM
        for src, dst in ((q, qr_ref), (q_rot, qo_ref)):
            t = src[:, h * gw:(h + 1) * gw].T
            dst[0, h, 0] = jnp.concatenate([t[g * HEAD_DIM:(g + 1) * HEAD_DIM] for g in range(NSA_GROUP)],
                                           axis=1).astype(bf16)
        base = TAIL_GATE + h * NSA_GROUP * 3
        g_ref[0, h, 0] = jnp.concatenate(
            [jnp.concatenate([gates_t[base + 3 * g + c:base + 3 * g + c + 1] for g in range(NSA_GROUP)], axis=1)
             for c in range(3)], axis=0)
    kc_in, vc_in = kv0[:, :kv_w].astype(bf16), kv0[:, kv_w:].astype(bf16)
    pooled_ref[0] = jnp.concatenate([_dot(pool_ref[0], kc_in), _dot(pool_ref[1], kc_in),
                                     _dot(pool_ref[2], vc_in), _dot(pool_ref[3], vc_in)], axis=1)


def _nsa_prep(h, pos, w_cmp_pool):
    bsz, t_, _ = h.shape
    nqb = t_ // Q_BLK
    bf16 = jnp.bfloat16
    half = ROPE_DIM // 2
    inv_freq = jnp.power(ROPE_THETA, -jnp.arange(half, dtype=jnp.float32) / half)
    ang = pos.astype(jnp.float32)[:, None] * inv_freq
    cos, sin = jnp.cos(ang), jnp.sin(ang)
    rest = HEAD_DIM - ROPE_DIM
    z8, zr = jnp.zeros((t_, half), jnp.float32), jnp.zeros((t_, rest), jnp.float32)
    two = lambda a: jnp.concatenate([a, a], axis=1)
    rc = two(jnp.concatenate([cos, cos, jnp.ones((t_, rest), jnp.float32)], axis=1))
    ru = two(jnp.concatenate([z8, sin, zr], axis=1))
    rd = two(jnp.concatenate([-sin, z8, zr], axis=1))
    pool = _pool_matrices(w_cmp_pool)
    kv_w = NSA_KV_HEADS * HEAD_DIM
    col = lambda width, off: pl.BlockSpec((1, Q_BLK, width), lambda b, i: (b, i, off // width))
    rows_t = pl.BlockSpec((Q_BLK, LANE), lambda b, i: (i, 0))
    head4 = lambda r, c: pl.BlockSpec((1, NSA_KV_HEADS, 1, r, c), lambda b, i: (b, 0, i, 0, 0))
    return pl.pallas_call(
        _nsa_prep_body,
        grid=(bsz, nqb),
        in_specs=[col(NSA_SIZES[0], COL_NQ), col(2 * kv_w, COL_NKV), col(2 * kv_w, COL_NKV + 2 * kv_w),
                  col(2 * kv_w, COL_NKV + 4 * kv_w), col(LANE, COL_TAIL), rows_t, rows_t, rows_t,
                  pl.BlockSpec((4, SUBS, Q_BLK), lambda b, i: (0, 0, 0))],
        out_specs=[pl.BlockSpec((1, Q_BLK, 4 * kv_w), lambda b, i: (b, i, 0)),
                   pl.BlockSpec((1, Q_BLK, 2 * kv_w), lambda b, i: (b, i, 0)),
                   pl.BlockSpec((1, NSA_KV_HEADS, Q_BLK, KK_W), lambda b, i: (b, 0, i, 0)),
                   head4(2 * HEAD_DIM, Q_BLK), head4(HEAD_DIM, NSA_ROWS), head4(HEAD_DIM, NSA_ROWS),
                   head4(3, NSA_ROWS),
                   pl.BlockSpec((1, SUBS, 4 * kv_w), lambda b, i: (b, i, 0))],
        out_shape=[jax.ShapeDtypeStruct((bsz, t_, 4 * kv_w), jnp.float32),
                   jax.ShapeDtypeStruct((bsz, t_, 2 * kv_w), jnp.float32),
                   jax.ShapeDtypeStruct((bsz, NSA_KV_HEADS, t_, KK_W), bf16),
                   jax.ShapeDtypeStruct((bsz, NSA_KV_HEADS, nqb, 2 * HEAD_DIM, Q_BLK), bf16),
                   jax.ShapeDtypeStruct((bsz, NSA_KV_HEADS, nqb, HEAD_DIM, NSA_ROWS), bf16),
                   jax.ShapeDtypeStruct((bsz, NSA_KV_HEADS, nqb, HEAD_DIM, NSA_ROWS), bf16),
                   jax.ShapeDtypeStruct((bsz, NSA_KV_HEADS, nqb, 3, NSA_ROWS), jnp.float32),
                   jax.ShapeDtypeStruct((bsz, t_ // CMP_STRIDE, 4 * kv_w), jnp.float32)],
        compiler_params=pltpu.CompilerParams(dimension_semantics=("arbitrary", "arbitrary")),
        name="nsa_prep",
    )(h, h, h, h, h, rc, ru, rd, pool)


PAGE_GROUP = 8
DEC_KEYS = PAGE_GROUP * PAGE_SIZE
NEW_PAD = 8
KV_W = NSA_KV_HEADS * HEAD_DIM


def _dec_pool_body(pt_ref, *refs):
    page_refs, pool_ref, out_ref = refs[:PAGE_GROUP], refs[PAGE_GROUP], refs[PAGE_GROUP + 1]
    bf16 = jnp.bfloat16
    parts = []
    for pr in page_refs:
        kv0 = pr[0]
        kc_in, vc_in = kv0[:, :KV_W].astype(bf16), kv0[:, KV_W:].astype(bf16)
        parts.append(jnp.concatenate([_dot(pool_ref[0], kc_in), _dot(pool_ref[1], kc_in),
                                      _dot(pool_ref[2], vc_in), _dot(pool_ref[3], vc_in)], axis=1))
    out_ref[0] = jnp.concatenate(parts, axis=0)


def _page_specs(n_pages, col_blk):
    def spec(i):
        return pl.BlockSpec((1, PAGE_SIZE, 2 * KV_W),
                            lambda b, j, pt: (pt[b * n_pages + j * PAGE_GROUP + i], 0, col_blk))
    return [spec(i) for i in range(PAGE_GROUP)]


def _dec_pool(cache, page_table, pool):
    bsz, n_pages = page_table.shape
    grid_spec = pltpu.PrefetchScalarGridSpec(
        num_scalar_prefetch=1, grid=(bsz, n_pages // PAGE_GROUP),
        in_specs=_page_specs(n_pages, 0) + [pl.BlockSpec((4, SUBS, Q_BLK), lambda b, j, pt: (0, 0, 0))],
        out_specs=pl.BlockSpec((1, PAGE_GROUP * SUBS, 4 * KV_W), lambda b, j, pt: (b, j, 0)))
    return pl.pallas_call(
        _dec_pool_body, grid_spec=grid_spec,
        out_shape=jax.ShapeDtypeStruct((bsz, n_pages * SUBS, 4 * KV_W), jnp.float32),
        compiler_params=pltpu.CompilerParams(dimension_semantics=("arbitrary", "arbitrary")),
        name="nsa_dec_pool",
    )(page_table.reshape(-1), *([cache] * PAGE_GROUP), pool)


def _dec_select_body(qr_ref, kct_ref, vc_ref, band_ref, oc_ref, selb_ref, *, qpos0, n_q, n_pick, n_blk):
    f32, bf16 = jnp.float32, jnp.bfloat16
    n_cmp = kct_ref.shape[3]
    rows = NSA_GROUP * n_q
    for h in range(NSA_KV_HEADS):
        s_c = _dot(qr_ref[0, h], kct_ref[0, h])
        n_idx = lax.broadcasted_iota(jnp.int32, (rows, n_cmp), 1)
        qpos = qpos0 + (lax.broadcasted_iota(jnp.int32, (rows, n_cmp), 0) % n_q)
        cmask = (n_idx * CMP_STRIDE + (CMP_BLK - 1)) <= qpos
        s_c = jnp.where(cmask, s_c, MASKED)
        p_c = jnp.where(cmask, jnp.exp(s_c - jnp.max(s_c, axis=1, keepdims=True)), 0.0)
        p_c = p_c / jnp.maximum(jnp.sum(p_c, axis=1, keepdims=True), 1e-30)
        oc_ref[0, h] = _dot(p_c.astype(bf16), vc_ref[0, h])
        imp = p_c[0:n_q]
        for g in range(1, NSA_GROUP):
            imp = imp + p_c[g * n_q:(g + 1) * n_q]
        imp_s = jnp.zeros((n_q, N_SELB), f32)
        rem = imp
        for _ in range(3):
            part = rem.astype(bf16)
            imp_s = imp_s + _dot(part, band_ref[...])
            rem = rem - part.astype(f32)
        blk = lax.broadcasted_iota(jnp.int32, (n_q, N_SELB), 1)
        qpos_s = qpos0 + lax.broadcasted_iota(jnp.int32, (n_q, N_SELB), 0)
        cur = lax.shift_right_logical(qpos_s, int(math.log2(SEL_BLK)))
        valid = (blk * SEL_BLK <= qpos_s) & (blk < n_blk)
        forced = (blk == 0) | (blk == cur) | (blk == cur - 1)
        score = jnp.where(valid, imp_s + jnp.where(forced, FORCE_BONUS, 0.0), -1e30)
        picked = jnp.zeros((n_q, N_SELB), f32)
        for _ in range(n_pick):
            best = jnp.max(score, axis=1, keepdims=True)
            first = jnp.min(jnp.where(score == best, blk, N_SELB), axis=1, keepdims=True)
            hit = blk == first
            picked = jnp.where(hit, 1.0, picked)
            score = jnp.where(hit, -3e38, score)
        selb_ref[0, h] = (jnp.where(valid, picked, 0.0) - 1.0) * (-MASKED)


def _dec_select(qr, kct, vc, n_q, qpos0, n_pick, n_blk):
    bsz = qr.shape[0]
    rows = NSA_GROUP * n_q
    n_cmp = kct.shape[3]
    ratio = SEL_BLK // CMP_STRIDE
    c_idx, j_idx = np.arange(n_cmp)[:, None], np.arange(N_SELB)[None, :]
    band = jnp.asarray(((c_idx >= ratio * j_idx - 1) & (c_idx <= ratio * j_idx + ratio - 1)), jnp.bfloat16)
    per_b = lambda *tail: pl.BlockSpec((1, NSA_KV_HEADS) + tail, lambda b: (b, 0, 0, 0))
    return pl.pallas_call(
        functools.partial(_dec_select_body, qpos0=qpos0, n_q=n_q, n_pick=n_pick, n_blk=n_blk),
        grid=(bsz,),
        in_specs=[per_b(rows, HEAD_DIM), per_b(HEAD_DIM, n_cmp), per_b(n_cmp, HEAD_DIM),
                  pl.BlockSpec((n_cmp, N_SELB), lambda b: (0, 0))],
        out_specs=[per_b(rows, HEAD_DIM), per_b(n_q, N_SELB)],
        out_shape=[jax.ShapeDtypeStruct((bsz, NSA_KV_HEADS, rows, HEAD_DIM), jnp.float32),
                   jax.ShapeDtypeStruct((bsz, NSA_KV_HEADS, n_q, N_SELB), jnp.float32)],
        compiler_params=pltpu.CompilerParams(dimension_semantics=("arbitrary",)),
        name="nsa_dec_select",
    )(qr, kct, vc, band)


def _dec_attend_body(pt_ref, *refs, qpos0, n_q, past):
    page_refs = refs[:PAGE_GROUP]
    (qs_ref, qw_ref, knew_ref, vnew_ref, wbuf_ref, wnew_ref, oc_ref, g_ref,
     o_ref, m_ref, l_ref, acc_ref) = refs[PAGE_GROUP:]
    f32, bf16 = jnp.float32, jnp.bfloat16
    j = pl.program_id(1)
    n_rows = qs_ref.shape[1]

    @pl.when(j == 0)
    def _():
        m_ref[...] = jnp.full(m_ref.shape, MASKED, f32)
        l_ref[...] = jnp.zeros(l_ref.shape, f32)
        acc_ref[...] = jnp.zeros(acc_ref.shape, f32)

    def online(s, v):
        m_old = m_ref[...]
        m_new = jnp.maximum(m_old, jnp.max(s, axis=1, keepdims=True))
        alpha = jnp.exp(m_old - m_new)
        p = jnp.exp(s - m_new)
        l_ref[...] = alpha * l_ref[...] + jnp.sum(p, axis=1, keepdims=True)
        acc_ref[...] = alpha * acc_ref[...] + _dot(p.astype(bf16), v)
        m_ref[...] = m_new

    qs = qs_ref[0]
    pages = [pr[0] for pr in page_refs]
    keys = jnp.concatenate([p[:, :KV_W] for p in pages], axis=0).astype(bf16)
    vals = jnp.concatenate([p[:, KV_W:] for p in pages], axis=0).astype(bf16)
    blk_id = j * (DEC_KEYS // SEL_BLK) + lax.shift_right_logical(
        lax.broadcasted_iota(jnp.int32, (DEC_KEYS, N_SELB), 0), int(math.log2(SEL_BLK)))
    onehot = jnp.where(lax.broadcasted_iota(jnp.int32, (DEC_KEYS, N_SELB), 1) == blk_id, 1.0, 0.0).astype(bf16)
    online(_dot_nt(qs, jnp.concatenate([keys, onehot], axis=1)), vals)

    @pl.when(j == pl.num_programs(1) - 1)
    def _():
        row_q = qpos0 + (lax.broadcasted_iota(jnp.int32, (n_rows, 1), 0) % n_q)
        qh = qw_ref[0]
        new_pos = past + lax.broadcasted_iota(jnp.int32, (n_rows, NEW_PAD), 1)
        new_ok = (new_pos <= row_q) & (new_pos < past + n_q)
        s_new = jnp.where(new_ok, _dot_nt(qh, knew_ref[0]), MASKED)
        online(s_new, vnew_ref[0])
        o_s = acc_ref[...] / l_ref[...]
        wbuf = wbuf_ref[0]
        wnew = wnew_ref[0]
        n_buf = wbuf.shape[0]
        s_b = _dot_nt(qh, wbuf[:, :KV_W].astype(bf16))
        pos_b = (past - n_buf) + lax.broadcasted_iota(jnp.int32, (n_rows, n_buf), 1)
        s_b = jnp.where((pos_b > row_q - WINDOW) & (pos_b >= 0), s_b, MASKED)
        s_n = jnp.where(new_ok, _dot_nt(qh, wnew[:, :KV_W].astype(bf16)), MASKED)
        m_w = jnp.maximum(jnp.max(s_b, axis=1, keepdims=True), jnp.max(s_n, axis=1, keepdims=True))
        p_b, p_n = jnp.exp(s_b - m_w), jnp.exp(s_n - m_w)
        l_w = jnp.sum(p_b, axis=1, keepdims=True) + jnp.sum(p_n, axis=1, keepdims=True)
        o_w = (_dot(p_b.astype(bf16), wbuf[:, KV_W:].astype(bf16))
               + _dot(p_n.astype(bf16), wnew[:, KV_W:].astype(bf16))) / l_w
        half = n_rows // NSA_KV_HEADS
        own = lambda a: jnp.concatenate([a[h * half:(h + 1) * half, h * HEAD_DIM:(h + 1) * HEAD_DIM]
                                         for h in range(NSA_KV_HEADS)], axis=0)
        g = g_ref[0]
        o_ref[0] = g[:, 0:1] * oc_ref[0] + g[:, 1:2] * own(o_s) + g[:, 2:3] * own(o_w)


def _dec_attend(cache, page_table, qs, qw, knew, vnew, wbuf, wnew, o_c, gates, n_q, qpos0):
    bsz, n_pages = page_table.shape
    n_rows = qs.shape[1]
    per_b = lambda *tail: pl.BlockSpec((1,) + tail, lambda b, j, pt: (b, 0, 0))
    grid_spec = pltpu.PrefetchScalarGridSpec(
        num_scalar_prefetch=1, grid=(bsz, n_pages // PAGE_GROUP),
        in_specs=_page_specs(n_pages, 1) + [
            per_b(n_rows, KV_W + N_SELB), per_b(n_rows, KV_W), per_b(NEW_PAD, KV_W), per_b(NEW_PAD, KV_W),
            per_b(wbuf.shape[1], 2 * KV_W), per_b(NEW_PAD, 2 * KV_W), per_b(n_rows, HEAD_DIM), per_b(n_rows, 3)],
        out_specs=per_b(n_rows, HEAD_DIM),
        scratch_shapes=[pltpu.VMEM((n_rows, 1), jnp.float32), pltpu.VMEM((n_rows, 1), jnp.float32),
                        pltpu.VMEM((n_rows, KV_W), jnp.float32)])
    return pl.pallas_call(
        functools.partial(_dec_attend_body, qpos0=qpos0, n_q=n_q, past=n_pages * PAGE_SIZE),
        grid_spec=grid_spec,
        out_shape=jax.ShapeDtypeStruct((bsz, n_rows, HEAD_DIM), jnp.float32),
        compiler_params=pltpu.CompilerParams(dimension_semantics=("arbitrary", "arbitrary")),
        name="nsa_dec_attend",
    )(page_table.reshape(-1), *([cache] * PAGE_GROUP), qs, qw, knew, vnew, wbuf, wnew, o_c, gates)


def _pool_matrices(w_cmp_pool):
    sub = np.arange(Q_BLK) // CMP_STRIDE == np.arange(SUBS)[:, None]
    w_rep = jnp.tile(w_cmp_pool.reshape(2, 2, CMP_STRIDE), (1, 1, SUBS))
    return jnp.where(sub[None, None], w_rep[:, :, None, :], 0.0).reshape(4, SUBS, Q_BLK).astype(jnp.bfloat16)


def _compressed_from_pooled(pooled):
    bsz, n_sub, _ = pooled.shape
    pooled = pooled.reshape(bsz, n_sub, 4, NSA_KV_HEADS, HEAD_DIM)
    kc = pooled[:, :-1, 0] + pooled[:, 1:, 1]
    vc = pooled[:, :-1, 2] + pooled[:, 1:, 3]
    pad = lambda a: jnp.pad(a, ((0, 0), (0, 1), (0, 0), (0, 0))).transpose(0, 2, 1, 3)
    return pad(kc), pad(vc)


def _nsa_decode(q_raw, q_rot, gates, rows_full, rows_win, cache, page_table, win_buf, w_cmp_pool, past):
    bsz, n_q = q_raw.shape[:2]
    bf16 = jnp.bfloat16
    n_blk = past // SEL_BLK
    assert past % DEC_KEYS == 0 and n_blk <= N_SELB and n_q <= NEW_PAD
    scale = HEAD_DIM ** -0.5
    cache2 = cache.reshape(cache.shape[0], PAGE_SIZE, 4 * KV_W)
    pooled = _dec_pool(cache2, page_table, _pool_matrices(w_cmp_pool))
    kc_p, vc_p = _compressed_from_pooled(pooled)
    rows_of = lambda a: a.transpose(0, 2, 3, 1, 4).reshape(bsz, NSA_KV_HEADS, NSA_GROUP * n_q, a.shape[-1])
    qr = rows_of((q_raw * scale).astype(bf16))
    n_pick = min(SEL_TOPN, n_blk + 1) - 1
    o_c, selb = _dec_select(qr, kc_p.transpose(0, 1, 3, 2).astype(bf16), vc_p.astype(bf16), n_q, past, n_pick, n_blk)
    qo = rows_of((q_rot * scale).astype(bf16))
    zero = jnp.zeros_like(qo[:, 0])
    qw = jnp.concatenate([jnp.concatenate([qo[:, 0], zero], -1), jnp.concatenate([zero, qo[:, 1]], -1)], axis=1)
    bias = jnp.tile(selb, (1, 1, NSA_GROUP, 1)).reshape(bsz, -1, N_SELB).astype(bf16)
    qs = jnp.concatenate([qw, bias], axis=-1)
    pad_new = lambda a: jnp.pad(a.reshape(bsz, n_q, -1), ((0, 0), (0, NEW_PAD - n_q), (0, 0)))
    knew = pad_new(rows_full[:, :, 2]).astype(bf16)
    vnew = pad_new(rows_full[:, :, 3]).astype(bf16)
    wnew = pad_new(rows_win)
    wbuf = win_buf.reshape(bsz, win_buf.shape[1], 2 * KV_W)
    gt = rows_of(gates).reshape(bsz, -1, 3)
    o = _dec_attend(cache2, page_table, qs, qw, knew, vnew, wbuf, wnew,
                    o_c.reshape(bsz, -1, HEAD_DIM), gt, n_q, past)
    o = o.reshape(bsz, NSA_KV_HEADS, NSA_GROUP, n_q, HEAD_DIM).transpose(0, 3, 1, 2, 4)
    return o.reshape(bsz, n_q, NSA_HEADS * HEAD_DIM)


def _ab_mixer(x, pos, w_in, w_gla_gate, b_gla_gate, gla_norm_g, w_cmp_pool, w_out,
              gla_state, nsa_cache, page_table, win_buf):
    bsz, t_, _ = x.shape
    h_in = _mm(x.reshape(bsz * t_, -1), w_in[:, IN_AB_PERM], keep_pad=True).reshape(bsz, t_, -1)
    o_a, s_a = _gla(h_in, w_gla_gate, b_gla_gate, gla_norm_g, gla_state)
    kv_w = NSA_KV_HEADS * HEAD_DIM
    if nsa_cache is None:
        rows2, win2, kk, vvt, qr, qo, gt, pooled = _nsa_prep(h_in, pos, w_cmp_pool)
        pooled = pooled.reshape(bsz, t_ // CMP_STRIDE, 4, NSA_KV_HEADS, HEAD_DIM)
        kc = pooled[:, :-1, 0] + pooled[:, 1:, 1]
        vc = pooled[:, :-1, 2] + pooled[:, 1:, 3]
        kc_p = jnp.pad(kc, ((0, 0), (0, 1), (0, 0), (0, 0))).transpose(0, 2, 1, 3).astype(jnp.bfloat16)
        vct = jnp.pad(vc, ((0, 0), (0, 1), (0, 0), (0, 0))).transpose(0, 2, 3, 1).astype(jnp.bfloat16)
        o_b = _nsa_prompt(qr, qo, gt, kc_p, vct, kk, vvt)
        rows_full = rows2.reshape(bsz, t_, 4, NSA_KV_HEADS, HEAD_DIM)
        new_win = win2[:, -min(WINDOW, t_):].reshape(bsz, -1, 2, NSA_KV_HEADS, HEAD_DIM)
    else:
        nq = h_in[..., COL_NQ:COL_NKV]
        nkv = h_in[..., COL_NKV:COL_TAIL]
        ngate = h_in[..., COL_TAIL + TAIL_GATE:COL_TAIL + TAIL_GATE + NSA_SIZES[2]]
        q_raw = nq.reshape(bsz, t_, NSA_KV_HEADS, NSA_GROUP, HEAD_DIM)
        q_rot = _partial_rope(q_raw, pos)
        kv = nkv.reshape(bsz, t_, 6, NSA_KV_HEADS, HEAD_DIM)
        k_sel = _partial_rope(kv[:, :, 2], pos)
        k_win = _partial_rope(kv[:, :, 4], pos)
        rows_full = jnp.stack([kv[:, :, 0], kv[:, :, 1], k_sel, kv[:, :, 3]], axis=2)
        rows_win = jnp.stack([k_win, kv[:, :, 5]], axis=2)
        gates = jax.nn.sigmoid(ngate).reshape(bsz, t_, NSA_KV_HEADS, NSA_GROUP, 3)
        past_len = page_table.shape[1] * PAGE_SIZE
        o_b = _nsa_decode(q_raw, q_rot, gates, rows_full, rows_win, nsa_cache, page_table, win_buf,
                          w_cmp_pool, past_len)
        w_buf = win_buf.shape[1]
        kw = jnp.concatenate([win_buf, rows_win], axis=1)
        new_win = kw[:, -w_buf:]
    y = _mm3(jnp.concatenate([o_a, o_b], axis=-1), w_out)
    return y, s_a, rows_full, new_win


CONV_HALO = 32
CONV_LEAD = CONV_HALO - (CONV_W - 1)


def _conv_body(x_ref, buf0_ref, w1_ref, b1_ref, wdw_ref, bdw_ref, g_ref, b_ref, w2_ref, b2_ref,
               o_ref, tail_ref, ext_ref, *, t_last):
    bf16 = jnp.bfloat16
    tt = x_ref.shape[1]
    i = pl.program_id(1)

    @pl.when(i == 0)
    def _():
        ext_ref[0:CONV_HALO, :] = buf0_ref[0]

    h = _dot(x_ref[0].astype(bf16), w1_ref[...]) + b1_ref[...]
    ext_ref[CONV_HALO:CONV_HALO + tt, :] = h[:, :D_CONV] * jax.nn.sigmoid(h[:, D_CONV:])
    c = jnp.zeros((tt, D_CONV), jnp.float32) + bdw_ref[...]
    for k in range(CONV_W):
        c = c + ext_ref[pl.ds(CONV_LEAD + k, tt), :] * wdw_ref[k:k + 1, :]
    c = _ln_rows(c, g_ref[...], b_ref[...])
    c = c * jax.nn.sigmoid(c)
    o_ref[0] = _dot(c.astype(bf16), w2_ref[...]) + b2_ref[...]
    tail_ref[0] = ext_ref[t_last:t_last + CONV_HALO, :]
    ext_ref[0:CONV_HALO, :] = ext_ref[tt:tt + CONV_HALO, :]


def _conv_module(x, conv_buf, w_pw1, b_pw1, w_dw, b_dw, ln_g, ln_b, w_pw2, b_pw2):
    bsz, t_, d = x.shape
    bf16 = jnp.bfloat16
    tp = -(-t_ // 8) * 8
    tt = min(tp, 256)
    n_t = tp // tt
    if tp != t_:
        x = jnp.pad(x, ((0, 0), (0, tp - t_), (0, 0)))
    if conv_buf is None:
        buf0 = jnp.zeros((bsz, CONV_HALO, D_CONV), jnp.float32)
    else:
        buf0 = jnp.pad(conv_buf, ((0, 0), (CONV_LEAD, 0), (0, 0)))
    fixed = lambda shape: pl.BlockSpec(shape, lambda b, i: (0,) * len(shape))
    per_b = pl.BlockSpec((1, CONV_HALO, D_CONV), lambda b, i: (b, 0, 0))
    out, tail = pl.pallas_call(
        functools.partial(_conv_body, t_last=t_ - (n_t - 1) * tt),
        grid=(bsz, n_t),
        in_specs=[pl.BlockSpec((1, tt, d), lambda b, i: (b, i, 0)), per_b,
                  fixed((d, 2 * D_CONV)), fixed((1, 2 * D_CONV)), fixed((CONV_HALO, D_CONV)), fixed((1, D_CONV)),
                  fixed((1, D_CONV)), fixed((1, D_CONV)), fixed((D_CONV, d)), fixed((1, d))],
        out_specs=[pl.BlockSpec((1, tt, d), lambda b, i: (b, i, 0)), per_b],
        out_shape=[jax.ShapeDtypeStruct((bsz, tp, d), jnp.float32),
                   jax.ShapeDtypeStruct((bsz, CONV_HALO, D_CONV), jnp.float32)],
        scratch_shapes=[pltpu.VMEM((CONV_HALO + tt, D_CONV), jnp.float32)],
        compiler_params=pltpu.CompilerParams(dimension_semantics=("arbitrary", "arbitrary"),
                                             vmem_limit_bytes=48 * 1024 * 1024),
        name="conv_module",
    )(x, buf0, w_pw1.astype(bf16), b_pw1.reshape(1, -1), jnp.pad(w_dw, ((0, CONV_HALO - CONV_W), (0, 0))),
      b_dw.reshape(1, -1), ln_g.reshape(1, -1), ln_b.reshape(1, -1), w_pw2.astype(bf16), b_pw2.reshape(1, -1))
    return out[:, :t_], tail[:, CONV_LEAD:]


PACK_W = 256
SC_WINDOW = 128
SC_TILES = 32


def _pack_rows(y):
    out = []
    for h in range(2):
        lo = lax.bitcast_convert_type(y[:, 2 * h * PACK_W:(2 * h + 1) * PACK_W].astype(jnp.bfloat16)
                                      .astype(jnp.float32), jnp.uint32)
        hi = lax.bitcast_convert_type(y[:, (2 * h + 1) * PACK_W:(2 * h + 2) * PACK_W].astype(jnp.bfloat16)
                                      .astype(jnp.float32), jnp.uint32)
        out.append(lax.bitcast_convert_type((lo >> 16) | hi, jnp.int32))
    return out


def _unpack_words(w):
    u = lax.bitcast_convert_type(w, jnp.uint32)
    lo = lax.bitcast_convert_type(u << 16, jnp.float32)
    hi = lax.bitcast_convert_type(u & jnp.uint32(0xFFFF0000), jnp.float32)
    return lo, hi


def _gather_rows(src, idx):
    n = idx.shape[0]
    if n % (SC_WINDOW * SC_TILES) != 0:
        return jnp.take(src, idx, axis=0)
    mesh = plsc.VectorSubcoreMesh(core_axis_name="core", subcore_axis_name="subcore")

    @pl.kernel(out_type=jax.ShapeDtypeStruct((n, src.shape[1]), src.dtype), mesh=mesh)
    def gather_kernel(src_hbm, idx_hbm, out_hbm):
        def step(idx_vmem, out_vmem):
            pltpu.sync_copy(src_hbm.at[idx_vmem.at[0]], out_vmem)

        pltpu.emit_pipeline(
            step, grid=(n // SC_WINDOW,),
            in_specs=[pl.BlockSpec((1, SC_WINDOW), index_map=lambda i: (0, i))],
            out_specs=[pl.BlockSpec((SC_WINDOW, src.shape[1]), index_map=lambda i: (i, 0))],
            core_axis_name=("core", "subcore"),
            dimension_semantics=(pltpu.PARALLEL,),
        )(idx_hbm, out_hbm)

    return gather_kernel(src, idx.reshape(1, n))


PER_GROUP = N_EXPERTS // N_GROUPS
PICKED = -3e38


def _ln_rows(v, g, b):
    mu = jnp.mean(v, axis=-1, keepdims=True)
    c = v - mu
    var = jnp.mean(c * c, axis=-1, keepdims=True)
    return c * lax.rsqrt(var + LN_EPS) * g + b


def _first_max(v, ids, axes, sentinel):
    best = v
    for a in axes:
        best = jnp.max(best, axis=a, keepdims=True)
    first = jnp.where(v == best, ids, sentinel)
    for a in axes:
        first = jnp.min(first, axis=a, keepdims=True)
    return best, first


def _sum_axes(v, axes):
    for a in axes:
        v = jnp.sum(v, axis=a, keepdims=True)
    return v


def _moe_pre_body(x_ref, mix_ref, g_ref, b_ref, wr_ref, br_ref, wgu_ref, wdn_ref,
                  x1_ref, xp_ref, sh_ref, eidx_ref, gate_ref, rank_ref, cnt_ref, run_ref):
    f32, bf16 = jnp.float32, jnp.bfloat16
    tm = x_ref.shape[0]

    @pl.when(pl.program_id(0) == 0)
    def _():
        run_ref[...] = jnp.zeros(run_ref.shape, f32)

    x1 = _ln_rows(ALPHA * x_ref[...] + mix_ref[...], g_ref[...], b_ref[...])
    x1_ref[...] = x1
    x1b = x1.astype(bf16)
    xp_ref[0], xp_ref[1] = _pack_rows(x1)

    h = _dot(x1b, wgu_ref[...])
    d_sh = h.shape[1] // 2
    act = (jax.nn.silu(h[:, :d_sh]) * h[:, d_sh:]).astype(bf16)
    sh_ref[...] = _dot(act, wdn_ref[...])

    s = jax.nn.sigmoid(_dot_nt(wr_ref[...], x1b)).reshape(N_GROUPS, PER_GROUP, tm)
    sb = s + br_ref[...].reshape(N_GROUPS, PER_GROUP, 1)
    shape3 = (N_GROUPS, PER_GROUP, tm)
    pid = lax.broadcasted_iota(jnp.int32, shape3, 1)
    gid = lax.broadcasted_iota(jnp.int32, (N_GROUPS, 1, tm), 0)
    eid = lax.broadcasted_iota(jnp.int32, shape3, 0) * PER_GROUP + pid
    top1, i1 = _first_max(sb, pid, (1,), PER_GROUP)
    top2 = jnp.max(jnp.where(pid == i1, PICKED, sb), axis=1, keepdims=True)
    gscore = top1 + top2
    gsel = jnp.zeros((N_GROUPS, 1, tm), f32)
    for _ in range(TOPK_GROUPS):
        _, first = _first_max(gscore, gid, (0,), N_GROUPS)
        hit = gid == first
        gsel = jnp.where(hit, 1.0, gsel)
        gscore = jnp.where(hit, PICKED, gscore)
    cand = jnp.where(gsel > 0.0, sb, -1e30)
    firsts, gates = [], []
    picked = jnp.zeros(shape3, f32)
    for _ in range(TOP_K):
        _, first = _first_max(cand, eid, (0, 1), N_EXPERTS)
        hit = eid == first
        firsts.append(first)
        gates.append(_sum_axes(jnp.where(hit, s, 0.0), (0, 1)))
        picked = jnp.where(hit, 1.0, picked)
        cand = jnp.where(hit, PICKED, cand)
    gsum = gates[0]
    for gk in gates[1:]:
        gsum = gsum + gk
    earlier = (lax.broadcasted_iota(jnp.int32, (tm, tm), 0) < lax.broadcasted_iota(jnp.int32, (tm, tm), 1))
    picked2 = picked.reshape(N_EXPERTS, tm)
    rank = run_ref[...] + _dot(picked2.astype(bf16), jnp.where(earlier, 1.0, 0.0).astype(bf16))
    run_new = run_ref[...] + jnp.sum(picked2, axis=1, keepdims=True)
    run_ref[...] = run_new
    cnt_ref[...] = jnp.broadcast_to(run_new, cnt_ref.shape)
    rank3 = rank.reshape(shape3)
    for k in range(TOP_K):
        hit = eid == firsts[k]
        eidx_ref[k:k + 1, :] = firsts[k].reshape(1, tm)
        gate_ref[k:k + 1, :] = (gates[k] / gsum * ROUTE_SCALE).reshape(1, tm)
        rank_ref[k:k + 1, :] = _sum_axes(jnp.where(hit, rank3, 0.0), (0, 1)).reshape(1, tm).astype(jnp.int32)


def _moe_pre(x, mix, g, b, w_router, b_router, w_sh_gu, w_sh_down):
    m, d = x.shape
    bf16 = jnp.bfloat16
    tm = min(m, 512)
    row = lambda i: (i, 0)
    col = lambda i: (0, i)
    fixed = lambda i: (0, 0)
    d_sh2 = w_sh_gu.shape[1]
    return pl.pallas_call(
        _moe_pre_body,
        grid=(m // tm,),
        in_specs=[pl.BlockSpec((tm, d), row), pl.BlockSpec((tm, d), row),
                  pl.BlockSpec((1, d), fixed), pl.BlockSpec((1, d), fixed),
                  pl.BlockSpec((N_EXPERTS, d), fixed), pl.BlockSpec((N_EXPERTS, 1), fixed),
                  pl.BlockSpec((d, d_sh2), fixed), pl.BlockSpec((d_sh2 // 2, d), fixed)],
        out_specs=[pl.BlockSpec((tm, d), row), pl.BlockSpec((2, tm, PACK_W), lambda i: (0, i, 0)),
                   pl.BlockSpec((tm, d), row),
                   pl.BlockSpec((TOP_K, tm), col), pl.BlockSpec((TOP_K, tm), col), pl.BlockSpec((TOP_K, tm), col),
                   pl.BlockSpec((N_EXPERTS, LANE), fixed)],
        out_shape=[jax.ShapeDtypeStruct((m, d), jnp.float32), jax.ShapeDtypeStruct((2, m, PACK_W), jnp.int32),
                   jax.ShapeDtypeStruct((m, d), jnp.float32),
                   jax.ShapeDtypeStruct((TOP_K, m), jnp.int32), jax.ShapeDtypeStruct((TOP_K, m), jnp.float32),
                   jax.ShapeDtypeStruct((TOP_K, m), jnp.int32),
                   jax.ShapeDtypeStruct((N_EXPERTS, LANE), jnp.float32)],
        scratch_shapes=[pltpu.VMEM((N_EXPERTS, 1), jnp.float32)],
        compiler_params=pltpu.CompilerParams(dimension_semantics=("arbitrary",),
                                             vmem_limit_bytes=48 * 1024 * 1024),
        name="moe_pre",
    )(x, mix, g.reshape(1, d), b.reshape(1, d), w_router.T.astype(bf16), b_router.reshape(N_EXPERTS, 1),
      w_sh_gu.astype(bf16), w_sh_down.astype(bf16))


def _moe_expert_body(exp_ref, first_ref, active_ref, xs_ref, wgu_ref, wdn_ref, y_ref, wgu_bf, wdn_bf):
    i = pl.program_id(0)
    bf16 = jnp.bfloat16

    @pl.when(first_ref[i] == 1)
    def _():
        wgu_bf[...] = wgu_ref[0].astype(bf16)
        wdn_bf[...] = wdn_ref[0].astype(bf16)

    @pl.when(active_ref[i] == 1)
    def _():
        h = None
        for hw in range(2):
            for q, xq in enumerate(_unpack_words(xs_ref[hw])):
                r0 = (2 * hw + q) * PACK_W
                part = _dot(xq.astype(bf16), wgu_bf[r0:r0 + PACK_W, :])
                h = part if h is None else h + part
        d_e = h.shape[1] // 2
        act = (jax.nn.silu(h[:, :d_e]) * h[:, d_e:]).astype(bf16)
        y_ref[0], y_ref[1] = _pack_rows(_dot(act, wdn_bf[...]))

    @pl.when(active_ref[i] == 0)
    def _():
        y_ref[...] = jnp.zeros(y_ref.shape, y_ref.dtype)


def _moe_experts(xs, blk_exp, blk_first, blk_active, w_exp_gu, w_exp_down, bm):
    n_slots = xs.shape[1]
    d = w_exp_gu.shape[1]
    n_blk = n_slots // bm
    d_e2 = w_exp_gu.shape[2]
    words = lambda i, e, f, a: (0, i, 0)
    grid_spec = pltpu.PrefetchScalarGridSpec(
        num_scalar_prefetch=3,
        grid=(n_blk,),
        in_specs=[pl.BlockSpec((2, bm, PACK_W), words),
                  pl.BlockSpec((1, d, d_e2), lambda i, e, f, a: (e[i], 0, 0)),
                  pl.BlockSpec((1, d_e2 // 2, d), lambda i, e, f, a: (e[i], 0, 0))],
        out_specs=pl.BlockSpec((2, bm, PACK_W), words),
        scratch_shapes=[pltpu.VMEM((d, d_e2), jnp.bfloat16), pltpu.VMEM((d_e2 // 2, d), jnp.bfloat16)])
    return pl.pallas_call(
        _moe_expert_body,
        grid_spec=grid_spec,
        out_shape=jax.ShapeDtypeStruct((2, n_slots, PACK_W), jnp.int32),
        compiler_params=pltpu.CompilerParams(dimension_semantics=("arbitrary",),
                                             vmem_limit_bytes=48 * 1024 * 1024),
        name="moe_experts",
    )(blk_exp, blk_first, blk_active, xs, w_exp_gu, w_exp_down)


def _combine_ln_body(x_ref, yg_ref, gt_ref, sh_ref, g_ref, b_ref, o_ref):
    gt = gt_ref[...]
    parts = []
    for hw in range(2):
        lo_acc = hi_acc = None
        for k in range(TOP_K):
            lo, hi = _unpack_words(yg_ref[hw, k])
            gk = gt[:, k:k + 1]
            lo_acc = lo * gk if lo_acc is None else lo_acc + lo * gk
            hi_acc = hi * gk if hi_acc is None else hi_acc + hi * gk
        parts += [lo_acc, hi_acc]
    routed = jnp.concatenate(parts, axis=1)
    o_ref[...] = _ln_rows(ALPHA * x_ref[...] + (routed + sh_ref[...]), g_ref[...], b_ref[...])


def _combine_ln(x, yg, gate_t, shared, g, b):
    m, d = x.shape
    tm = min(m, 256)
    row = lambda i: (i, 0)
    fixed = lambda i: (0, 0)
    return pl.pallas_call(
        _combine_ln_body,
        grid=(m // tm,),
        in_specs=[pl.BlockSpec((tm, d), row), pl.BlockSpec((2, TOP_K, tm, PACK_W), lambda i: (0, 0, i, 0)),
                  pl.BlockSpec((tm, TOP_K), row), pl.BlockSpec((tm, d), row),
                  pl.BlockSpec((1, d), fixed), pl.BlockSpec((1, d), fixed)],
        out_specs=pl.BlockSpec((tm, d), row),
        out_shape=jax.ShapeDtypeStruct((m, d), jnp.float32),
        compiler_params=pltpu.CompilerParams(dimension_semantics=("arbitrary",)),
        name="combine_ln",
    )(x, yg, gate_t, shared, g.reshape(1, d), b.reshape(1, d))


def _moe_layer(x, mix, ln1_g, ln1_b, ln2_g, ln2_b, w_router, b_router, w_exp_gu, w_exp_down, w_sh_gu, w_sh_down):
    m, d = x.shape
    x1, xp, shared, eidx, gate8, rank8, counts = _moe_pre(x, mix, ln1_g, ln1_b, w_router, b_router,
                                                           w_sh_gu, w_sh_down)
    bm = 512 if m * TOP_K >= 512 * N_EXPERTS else MOE_BLK
    n_blk = (m * TOP_K) // bm + N_EXPERTS
    counts = counts[:, 0].astype(jnp.int32)
    padded = (counts + bm - 1) // bm * bm
    pad_end = jnp.cumsum(padded)
    pad_start = pad_end - padded
    start_of = jnp.sum(jnp.where(eidx[:, :, None] == jnp.arange(N_EXPERTS), pad_start, 0), axis=-1)
    dest = (start_of + rank8).reshape(-1)
    tok = jnp.tile(jnp.arange(m, dtype=jnp.int32), TOP_K)
    slot_tok = (jnp.arange(n_blk * bm, dtype=jnp.int32) % m).at[dest].set(tok)
    blk_start = jnp.arange(n_blk, dtype=jnp.int32) * bm
    blk_exp = jnp.minimum(jnp.sum(pad_end[None, :] <= blk_start[:, None], axis=1), N_EXPERTS - 1).astype(jnp.int32)
    blk_active = (blk_start < pad_end[-1]).astype(jnp.int32)
    blk_first = jnp.concatenate([jnp.ones((1,), jnp.int32), (blk_exp[1:] != blk_exp[:-1]).astype(jnp.int32)])
    n_slots = n_blk * bm
    xs = _gather_rows(xp.reshape(2 * m, PACK_W), jnp.concatenate([slot_tok, slot_tok + m]))
    y = _moe_experts(xs.reshape(2, n_slots, PACK_W), blk_exp, blk_first, blk_active, w_exp_gu, w_exp_down, bm)
    yg = _gather_rows(y.reshape(2 * n_slots, PACK_W), jnp.concatenate([dest, dest + n_slots]))
    return _combine_ln(x1, yg.reshape(2, TOP_K, m, PACK_W), gate8.T, shared, ln2_g, ln2_b)


def _trunk(x, pos, gla_state, nsa_cache, page_table, win_buf, conv_buf,
           w_in_ab, w_gla_gate, b_gla_gate, gla_norm_g, w_cmp_pool, w_out_ab,
           w_pw1, b_pw1, w_dw, b_dw, conv_ln_g, conv_ln_b, w_pw2, b_pw2,
           ln_g, ln_b, w_router, b_router, w_exp_gu, w_exp_down, w_sh_gu, w_sh_down):
    new_gla, new_rows, new_win, new_conv = [], [], [], []
    for layer in range(DEPTH):
        i = layer // 2
        if layer % 2 == 0:
            mix, s_a, rows, win = _ab_mixer(
                x, pos, w_in_ab[i], w_gla_gate[i], b_gla_gate[i], gla_norm_g[i], w_cmp_pool[i], w_out_ab[i],
                None if gla_state is None else gla_state[i],
                None if nsa_cache is None else nsa_cache[i], page_table,
                None if win_buf is None else win_buf[i])
            new_gla.append(s_a)
            new_rows.append(rows)
            new_win.append(win)
        else:
            mix, cb = _conv_module(x, None if conv_buf is None else conv_buf[i], w_pw1[i], b_pw1[i],
                                   w_dw[i], b_dw[i], conv_ln_g[i], conv_ln_b[i], w_pw2[i], b_pw2[i])
            new_conv.append(cb)
        bsz, t_, d = x.shape
        x = _moe_layer(x.reshape(-1, d), mix.reshape(-1, d), ln_g[layer, 0], ln_b[layer, 0],
                       ln_g[layer, 1], ln_b[layer, 1], w_router[layer], b_router[layer],
                       w_exp_gu[layer], w_exp_down[layer], w_sh_gu[layer], w_sh_down[layer]).reshape(bsz, t_, d)
    return x, jnp.stack(new_gla), jnp.stack(new_rows), jnp.stack(new_win), jnp.stack(new_conv)


def kernel(x_prompt, x_sample, state_gla, cache_nsa_kv, state_nsa_win, state_conv, page_table,
           w_in_ab, w_gla_gate, b_gla_gate, gla_norm_g, w_cmp_pool, w_out_ab,
           w_pw1, b_pw1, w_dw, b_dw, conv_ln_g, conv_ln_b, w_pw2, b_pw2,
           ln_g, ln_b, w_router, b_router, w_exp_gu, w_exp_down, w_sh_gu, w_sh_down):
    weights = (w_in_ab, w_gla_gate, b_gla_gate, gla_norm_g, w_cmp_pool, w_out_ab,
               w_pw1, b_pw1, w_dw, b_dw, conv_ln_g, conv_ln_b, w_pw2, b_pw2,
               ln_g, ln_b, w_router, b_router, w_exp_gu, w_exp_down, w_sh_gu, w_sh_down)
    past_len = page_table.shape[1] * PAGE_SIZE
    pos_p = jnp.arange(x_prompt.shape[1])
    pos_s = past_len + jnp.arange(x_sample.shape[1])
    y_prompt, gla_p, rows_p, win_p, conv_p = _trunk(x_prompt, pos_p, None, None, None, None, None, *weights)
    y_sample, gla_s, rows_s, win_s, conv_s = _trunk(x_sample, pos_s, state_gla, cache_nsa_kv, page_table,
                                                    state_nsa_win, state_conv, *weights)
    return (y_prompt, y_sample, gla_p, gla_s, rows_p, rows_s, win_p, win_s, conv_p, conv_s)
```

```python
import functools
import math

import jax
import jax.numpy as jnp
import numpy as np
from jax import lax
from jax.experimental import pallas as pl
from jax.experimental.pallas import tpu as pltpu
from jax.experimental.pallas import tpu_sc as plsc

D_MODEL = 1024
DEPTH = 2
PAGE_SIZE = 128

GLA_HEADS = 4
GLA_DV = D_MODEL // 2 // GLA_HEADS
GLA_DK = GLA_DV // 2
GLA_RANK = 16
GLA_TAU = 16.0
GLA_CHUNK = 64

NSA_HEADS = 8
NSA_KV_HEADS = 2
NSA_GROUP = NSA_HEADS // NSA_KV_HEADS
HEAD_DIM = D_MODEL // 2 // NSA_HEADS
CMP_BLK = 32
CMP_STRIDE = 16
SEL_BLK = 64
SEL_TOPN = 16
WINDOW = 512
Q_BLK = 128
FORCE_BONUS = 100.0
ROPE_DIM = HEAD_DIM // 4
ROPE_THETA = 500000.0

GLA_SIZES = (GLA_HEADS * GLA_DK, GLA_HEADS * GLA_DK, GLA_HEADS * GLA_DV, GLA_HEADS * GLA_DV, GLA_RANK)
NSA_SIZES = (NSA_HEADS * HEAD_DIM, 6 * NSA_KV_HEADS * HEAD_DIM, 3 * NSA_HEADS)

CONV_W = 31
D_CONV = D_MODEL

N_EXPERTS = 64
N_GROUPS = 8
TOPK_GROUPS = 4
TOP_K = 8
D_EXPERT = 256
ROUTE_SCALE = 2.5
MOE_BLK = 128

ALPHA = (2 * DEPTH) ** 0.25
LN_EPS = 1e-5

LANE = 128


def _dot(a, b):
    return jnp.dot(a, b, preferred_element_type=jnp.float32)


def _dot_nt(a, b):
    return lax.dot_general(a, b, (((1,), (1,)), ((), ())), preferred_element_type=jnp.float32)


def _mm_body(x_ref, w_ref, o_ref):
    o_ref[...] = _dot(x_ref[...].astype(jnp.bfloat16), w_ref[...].astype(jnp.bfloat16))


def _mm(x, w, keep_pad=False):
    m, k = x.shape
    n = w.shape[1]
    n_pad = -(-n // LANE) * LANE
    w = w.astype(jnp.bfloat16)
    if n_pad != n:
        w = jnp.pad(w, ((0, 0), (0, n_pad - n)))
    tm = min(m, 512)
    out = pl.pallas_call(
        _mm_body,
        grid=(m // tm,),
        in_specs=[pl.BlockSpec((tm, k), lambda i: (i, 0)),
                  pl.BlockSpec((k, n_pad), lambda i: (0, 0))],
        out_specs=pl.BlockSpec((tm, n_pad), lambda i: (i, 0)),
        out_shape=jax.ShapeDtypeStruct((m, n_pad), jnp.float32),
        compiler_params=pltpu.CompilerParams(dimension_semantics=("arbitrary",),
                                             vmem_limit_bytes=48 * 1024 * 1024),
        name="mm",
    )(x, w)
    return out if keep_pad or n_pad == n else out[:, :n]


def _mm3(x, w):
    b, t, d = x.shape
    return _mm(x.reshape(b * t, d), w).reshape(b, t, -1)


def _split_cols(h, sizes):
    return jnp.split(h, np.cumsum(sizes)[:-1].tolist(), axis=-1)


def _layer_norm(x, g, b):
    mu = x.mean(-1, keepdims=True)
    var = jnp.square(x - mu).mean(-1, keepdims=True)
    return (x - mu) * lax.rsqrt(var + LN_EPS) * g + b


def _rms_norm(x, g):
    return x * lax.rsqrt(jnp.mean(x * x, -1, keepdims=True) + LN_EPS) * g


def _partial_rope(x, pos):
    half = ROPE_DIM // 2
    inv_freq = jnp.power(ROPE_THETA, -jnp.arange(half, dtype=jnp.float32) / half)
    ang = pos.astype(jnp.float32)[:, None] * inv_freq
    ang = ang.reshape(ang.shape[0], *([1] * (x.ndim - 3)), half)
    cos, sin = jnp.cos(ang), jnp.sin(ang)
    x1 = x[..., :half]
    x2 = x[..., half:ROPE_DIM]
    rot = jnp.concatenate([x1 * cos - x2 * sin, x2 * cos + x1 * sin], -1)
    return jnp.concatenate([rot, x[..., ROPE_DIM:]], -1)


def _masked_softmax(s, mask):
    s = jnp.where(mask, s, -jnp.inf)
    m = jnp.max(s, axis=-1, keepdims=True)
    m = jnp.where(jnp.isfinite(m), m, 0.0)
    p = jnp.exp(s - m)
    return p / jnp.maximum(p.sum(-1, keepdims=True), 1e-30)


def _gla_recurrence(q, k, v, log_a, s0):
    bsz, t_, nh, _ = q.shape
    c = math.gcd(t_, GLA_CHUNK)
    n = t_ // c

    def chunks(a):
        return jnp.moveaxis(a.reshape(bsz, n, c, *a.shape[2:]), 1, 0)

    causal = jnp.tril(jnp.ones((c, c), dtype=bool))[None, :, :, None, None]

    def step(s, inp):
        qc, kc, vc, lc = inp
        bc = jnp.cumsum(lc, axis=1)
        decay = jnp.exp(jnp.where(causal, bc[:, :, None] - bc[:, None, :], -jnp.inf))
        attn = jnp.einsum('bijhd,bjhd->bhij', qc[:, :, None] * decay, kc)
        o = jnp.einsum('bhij,bjhe->bihe', attn, vc) + jnp.einsum('bihd,bhde->bihe', qc * jnp.exp(bc), s)
        bl = bc[:, -1]
        s = jnp.exp(bl)[..., None] * s + jnp.einsum('bjhd,bjhe->bhde', kc * jnp.exp(bl[:, None] - bc), vc)
        return s, o

    s_fin, o = lax.scan(step, s0, (chunks(q), chunks(k), chunks(v), chunks(log_a)))
    return jnp.moveaxis(o, 0, 1).reshape(bsz, t_, nh, -1), s_fin


def _compress(k, v, w_pool):
    bsz, length = k.shape[:2]
    n_sub = length // CMP_STRIDE

    def pool(a, w):
        sub = a[:, :n_sub * CMP_STRIDE].reshape(bsz, n_sub, CMP_STRIDE, *a.shape[2:])
        first = jnp.einsum('bnjhd,j->bnhd', sub, w[:CMP_STRIDE])
        second = jnp.einsum('bnjhd,j->bnhd', sub, w[CMP_STRIDE:])
        return first[:, :-1] + second[:, 1:]

    cend = jnp.arange(n_sub - 1) * CMP_STRIDE + CMP_BLK - 1
    return pool(k, w_pool[0]), pool(v, w_pool[1]), cend


def _to_sel_blocks(a, n_sel):
    bsz, length = a.shape[:2]
    a = jnp.pad(a, ((0, 0), (0, n_sel * SEL_BLK - length), (0, 0), (0, 0)))
    return a.reshape(bsz, n_sel, SEL_BLK, NSA_KV_HEADS, HEAD_DIM).transpose(0, 3, 1, 2, 4)


def _nsa_attend(q_raw, q_rot, qpos, gates, kc, vc, cend, ksb, vsb, kw, vw, kwpos):
    scale = HEAD_DIM ** -0.5
    bsz, tq = q_raw.shape[:2]
    n_cmp, n_sel = kc.shape[1], ksb.shape[2]
    s_c = jnp.einsum('bqhgd,bnhd->bhgqn', q_raw, kc) * scale
    p_c = _masked_softmax(s_c, cend[None, :] <= qpos[:, None])
    o_c = jnp.einsum('bhgqn,bnhd->bqhgd', p_c, vc)
    ratio = SEL_BLK // CMP_STRIDE
    imp = p_c.sum(axis=2)
    imp = jnp.pad(imp, ((0, 0), (0, 0), (0, 0), (1, ratio * (n_sel + 1) - 1 - n_cmp)))
    imp = imp.reshape(bsz, NSA_KV_HEADS, tq, n_sel + 1, ratio)
    imp_s = imp[..., :n_sel, :].sum(-1) + imp[..., 1:, 0]
    blk = jnp.arange(n_sel)[None, :]
    cur = (qpos // SEL_BLK)[:, None]
    valid = blk * SEL_BLK <= qpos[:, None]
    forced = (blk == 0) | (blk == cur) | (blk == cur - 1)
    score = jnp.where(valid, imp_s + jnp.where(forced, FORCE_BONUS, 0.0), -jnp.inf)
    k_top = min(SEL_TOPN, n_sel)
    _, sel = lax.top_k(score, k_top)
    take = jax.vmap(jax.vmap(lambda blocks, idx: blocks[idx]))
    ks = take(ksb, sel).reshape(bsz, NSA_KV_HEADS, tq, k_top * SEL_BLK, HEAD_DIM)
    vs = take(vsb, sel).reshape(bsz, NSA_KV_HEADS, tq, k_top * SEL_BLK, HEAD_DIM)
    kpos = (sel[..., None] * SEL_BLK + jnp.arange(SEL_BLK)).reshape(bsz, NSA_KV_HEADS, tq, k_top * SEL_BLK)
    s_s = jnp.einsum('bqhgd,bhqkd->bhgqk', q_rot, ks) * scale
    p_s = _masked_softmax(s_s, (kpos <= qpos[:, None])[:, :, None])
    o_s = jnp.einsum('bhgqk,bhqkd->bqhgd', p_s, vs)
    s_w = jnp.einsum('bqhgd,bkhd->bhgqk', q_rot, kw) * scale
    kp, qp = kwpos[None, :], qpos[:, None]
    p_w = _masked_softmax(s_w, (kp <= qp) & (kp > qp - WINDOW) & (kp >= 0))
    o_w = jnp.einsum('bhgqk,bkhd->bqhgd', p_w, vw)
    return gates[..., 0:1] * o_c + gates[..., 1:2] * o_s + gates[..., 2:3] * o_w


NSA_ROWS = NSA_GROUP * Q_BLK
SEL_KT = 1024
N_SELB = 128
MASKED = -1e9
WIN_KEYS = WINDOW + Q_BLK
KK_W = 2 * HEAD_DIM + N_SELB


def _nsa_prompt_body(qr_ref, qo_ref, kc_ref, vct_ref, kk_ref, vvt_ref, g_ref, o_ref,
                     imp_ref, m_ref, l_ref, acc_ref):
    f32, bf16 = jnp.float32, jnp.bfloat16
    qb = pl.program_id(2)
    q0 = qb * Q_BLK
    qr_t = qr_ref[0, 0, 0]
    qo_t = qo_ref[0, 0, 0]
    n_cmp = kc_ref.shape[2]

    s_c = _dot(kc_ref[0, 0], qr_t)
    n_idx = lax.broadcasted_iota(jnp.int32, (n_cmp, NSA_ROWS), 0)
    qpos_c = q0 + (lax.broadcasted_iota(jnp.int32, (n_cmp, NSA_ROWS), 1) & (Q_BLK - 1))
    cmask = (n_idx * CMP_STRIDE + (CMP_BLK - 1)) <= qpos_c
    s_c = jnp.where(cmask, s_c, MASKED)
    m_c = jnp.max(s_c, axis=0, keepdims=True)
    p_c = jnp.where(cmask, jnp.exp(s_c - m_c), 0.0)
    p_c = p_c / jnp.maximum(jnp.sum(p_c, axis=0, keepdims=True), 1e-30)
    o_ct = _dot(vct_ref[0, 0], p_c.astype(bf16))

    imp = (p_c[:, 0:Q_BLK] + p_c[:, Q_BLK:2 * Q_BLK]) + p_c[:, 2 * Q_BLK:3 * Q_BLK] + p_c[:, 3 * Q_BLK:]
    imp_ref[0:8, :] = jnp.zeros((8, Q_BLK), f32)
    imp_ref[8:8 + n_cmp, :] = imp
    ratio = SEL_BLK // CMP_STRIDE
    n_selb = n_cmp // ratio
    imp_s = imp_ref[pl.ds(7, n_selb, stride=ratio), :]
    for r in range(ratio):
        imp_s = imp_s + imp_ref[pl.ds(8 + r, n_selb, stride=ratio), :]
    blk = lax.broadcasted_iota(jnp.int32, (n_selb, Q_BLK), 0)
    qpos_s = q0 + lax.broadcasted_iota(jnp.int32, (n_selb, Q_BLK), 1)
    cur = lax.shift_right_logical(qpos_s, int(math.log2(SEL_BLK)))
    valid = blk * SEL_BLK <= qpos_s
    forced = (blk == 0) | (blk == cur) | (blk == cur - 1)
    score = jnp.where(valid, imp_s + jnp.where(forced, FORCE_BONUS, 0.0), -1e30)
    picked = jnp.zeros((n_selb, Q_BLK), f32)
    for _ in range(SEL_TOPN):
        best = jnp.max(score, axis=0, keepdims=True)
        first = jnp.min(jnp.where(score == best, blk, n_selb), axis=0, keepdims=True)
        hit = blk == first
        picked = jnp.where(hit, 1.0, picked)
        score = jnp.where(hit, -3e38, score)
    selb_t = jnp.where(valid, picked, 0.0)
    if n_selb < N_SELB:
        selb_t = jnp.concatenate([selb_t, jnp.zeros((N_SELB - n_selb, Q_BLK), f32)], axis=0)
    selb_t = ((selb_t - 1.0) * (-MASKED)).astype(bf16)
    selb_t = jnp.concatenate([selb_t] * NSA_GROUP, axis=1)

    zeros_q = jnp.zeros((HEAD_DIM, NSA_ROWS), bf16)
    q_sel = jnp.concatenate([qo_t, zeros_q, selb_t], axis=0)
    q_win = jnp.concatenate([zeros_q, qo_t, jnp.zeros((N_SELB, NSA_ROWS), bf16)], axis=0)
    qpos_r = q0 + (lax.broadcasted_iota(jnp.int32, (1, NSA_ROWS), 1) & (Q_BLK - 1))

    def v_tiles(first, count):
        return jnp.concatenate([vvt_ref[0, 0, first + j] for j in range(count)], axis=1)

    m_ref[...] = jnp.full(m_ref.shape, MASKED, f32)
    l_ref[...] = jnp.zeros(l_ref.shape, f32)
    acc_ref[...] = jnp.zeros(acc_ref.shape, f32)

    def sel_tile(k0, kt, causal):
        s = _dot(kk_ref[0, 0, pl.ds(k0, kt), :], q_sel)
        if causal:
            kpos = k0 + lax.broadcasted_iota(jnp.int32, (kt, NSA_ROWS), 0)
            s = jnp.where(kpos <= qpos_r, s, MASKED)
        m_old = m_ref[...]
        m_new = jnp.maximum(m_old, jnp.max(s, axis=0, keepdims=True))
        alpha = jnp.exp(m_old - m_new)
        p = jnp.exp(s - m_new)
        l_ref[...] = alpha * l_ref[...] + jnp.sum(p, axis=0, keepdims=True)
        vt = v_tiles(k0 // Q_BLK, kt // Q_BLK)
        acc_ref[...] = alpha * acc_ref[...] + _dot(vt, p.astype(bf16))
        m_ref[...] = m_new

    n_full = q0 // SEL_KT

    def full_step(t, c):
        sel_tile(pl.multiple_of(t * SEL_KT, SEL_KT), SEL_KT, False)
        return c

    lax.fori_loop(0, n_full, full_step, 0)
    sel_tile(pl.multiple_of(n_full * SEL_KT, SEL_KT), SEL_KT, True)
    o_st = acc_ref[0:HEAD_DIM, :] / l_ref[...]

    w0 = pl.multiple_of(jnp.maximum(q0 - WINDOW, 0), Q_BLK)
    s_w = _dot(kk_ref[0, 0, pl.ds(w0, WIN_KEYS), :], q_win)
    kpos_w = w0 + lax.broadcasted_iota(jnp.int32, (WIN_KEYS, NSA_ROWS), 0)
    s_w = jnp.where((kpos_w <= qpos_r) & (kpos_w > qpos_r - WINDOW), s_w, MASKED)
    p_w = jnp.exp(s_w - jnp.max(s_w, axis=0, keepdims=True))
    l_w = jnp.sum(p_w, axis=0, keepdims=True)
    acc_w = _dot(v_tiles(w0 // Q_BLK, WIN_KEYS // Q_BLK), p_w.astype(bf16))
    o_wt = acc_w[HEAD_DIM:2 * HEAD_DIM, :] / l_w

    g = g_ref[0, 0, 0]
    out_t = g[0:1, :] * o_ct + g[1:2, :] * o_st + g[2:3, :] * o_wt
    o_ref[0] = jnp.concatenate([out_t[:, g_ * Q_BLK:(g_ + 1) * Q_BLK] for g_ in range(NSA_GROUP)], axis=0).T


def _nsa_prompt(qr, qo, gt, kc_p, vct, kk, vvt):
    bsz, _, nqb = qr.shape[:3]
    t_ = nqb * Q_BLK
    n_cmp = kc_p.shape[2]
    per_blk = lambda b, h, i: (b, h, i, 0, 0)
    per_head = lambda b, h, i: (b, h, 0, 0)
    return pl.pallas_call(
        _nsa_prompt_body,
        grid=(bsz, NSA_KV_HEADS, nqb),
        in_specs=[pl.BlockSpec((1, 1, 1, HEAD_DIM, NSA_ROWS), per_blk),
                  pl.BlockSpec((1, 1, 1, HEAD_DIM, NSA_ROWS), per_blk),
                  pl.BlockSpec((1, 1, n_cmp, HEAD_DIM), per_head),
                  pl.BlockSpec((1, 1, HEAD_DIM, n_cmp), per_head),
                  pl.BlockSpec((1, 1, t_, KK_W), per_head),
                  pl.BlockSpec((1, 1, nqb, 2 * HEAD_DIM, Q_BLK), lambda b, h, i: (b, h, 0, 0, 0)),
                  pl.BlockSpec((1, 1, 1, 3, NSA_ROWS), per_blk)],
        out_specs=pl.BlockSpec((1, Q_BLK, NSA_GROUP * HEAD_DIM), lambda b, h, i: (b, i, h)),
        out_shape=jax.ShapeDtypeStruct((bsz, t_, NSA_HEADS * HEAD_DIM), jnp.float32),
        scratch_shapes=[pltpu.VMEM((8 + n_cmp, Q_BLK), jnp.float32),
                        pltpu.VMEM((1, NSA_ROWS), jnp.float32),
                        pltpu.VMEM((1, NSA_ROWS), jnp.float32),
                        pltpu.VMEM((2 * HEAD_DIM, NSA_ROWS), jnp.float32)],
        compiler_params=pltpu.CompilerParams(
            dimension_semantics=("arbitrary", "arbitrary", "arbitrary"),
            vmem_limit_bytes=48 * 1024 * 1024),
        name="nsa_prompt",
    )(qr, qo, kc_p, vct, kk, vvt, gt)


GLA_SUB = 16
GLA_QK = GLA_HEADS * GLA_DK
GLA_V = GLA_HEADS * GLA_DV


def _dot_tn(a, b):
    return lax.dot_general(a, b, (((0,), (0,)), ((), ())), preferred_element_type=jnp.float32)


def _gla_body(q_ref, k_ref, v_ref, gr_ref, glr_ref, wg_ref, bg_ref, ng_ref, s0_ref, exp_ref, bd_ref,
              o_ref, sfin_ref, st_ref, b_ref, qd_ref, *, t_valid):
    f32, bf16 = jnp.float32, jnp.bfloat16
    tt = q_ref.shape[1]
    ti = pl.program_id(1)

    @pl.when(ti == 0)
    def _():
        st_ref[...] = s0_ref[0]

    row = lax.broadcasted_iota(jnp.int32, (tt, 1), 0)
    z = _dot(glr_ref[0][:, :GLA_RANK].astype(bf16), wg_ref[...]) + bg_ref[...]
    la = (jnp.minimum(z, 0.0) - jnp.log1p(jnp.exp(-jnp.abs(z)))) * (1.0 / GLA_TAU)
    la = jnp.where(ti * tt + row < t_valid, la, 0.0)
    seg = row & (GLA_SUB - 1)
    b = la
    for s in (1, 2, 4, 8):
        b = b + jnp.where(seg >= s, pltpu.roll(b, s, axis=0), 0.0)
    q = q_ref[0] * (GLA_DK ** -0.5)
    k = k_ref[0]
    v = v_ref[0]
    o = _dot((q * k).astype(bf16), exp_ref[...]) * v
    for d in range(1, GLA_SUB):
        decay = jnp.exp(jnp.minimum(b - pltpu.roll(b, d, axis=0), 0.0))
        w = jnp.where(seg >= d, q * pltpu.roll(k, d, axis=0) * decay, 0.0)
        o = o + _dot(w.astype(bf16), exp_ref[...]) * pltpu.roll(v, d, axis=0)
    o_ref[0] = o
    b_ref[...] = b
    qd_ref[...] = (q * jnp.exp(b)).astype(bf16)

    def block_step(c, carry):
        rows = pl.ds(pl.multiple_of(c * GLA_SUB, GLA_SUB), GLA_SUB)
        st = st_ref[...]
        o_ref[0, rows, :] += _dot_nt(qd_ref[rows, :], st.astype(bf16))
        bc = b_ref[rows, :]
        bl = bc[GLA_SUB - 1:GLA_SUB, :]
        kc = (k_ref[0, rows, :] * jnp.exp(bl - bc)).astype(bf16)
        upd = _dot_tn(v_ref[0, rows, :].astype(bf16), kc)
        st_ref[...] = st * jnp.exp(bl) + upd * bd_ref[...]
        return carry

    lax.fori_loop(0, tt // GLA_SUB, block_step, 0)
    sfin_ref[0] = st_ref[...]
    gr = gr_ref[0]
    gate = gr * jax.nn.sigmoid(gr)
    for h in range(GLA_HEADS):
        cols = slice(h * GLA_DV, (h + 1) * GLA_DV)
        oh = o_ref[0, :, cols]
        ms = jnp.mean(oh * oh, axis=-1, keepdims=True)
        o_ref[0, :, cols] = oh * lax.rsqrt(ms + LN_EPS) * ng_ref[...] * gate[:, cols]


def _gla(h, w_gla_gate, b_gla_gate, gla_norm_g, gla_state):
    bsz, t_, n_in = h.shape
    tp = -(-t_ // GLA_SUB) * GLA_SUB
    if tp != t_:
        h = jnp.pad(h, ((0, 0), (0, tp - t_), (0, 0)))
    tt = min(tp, 256)
    heads = np.arange(GLA_HEADS)
    expand = np.repeat(np.repeat(np.eye(GLA_HEADS, dtype=np.float32), GLA_DK, 0), GLA_DV, 1)
    bdmask = jnp.asarray(expand.T)
    if gla_state is None:
        s0 = jnp.zeros((bsz, GLA_V, GLA_QK), jnp.float32)
    else:
        s0 = jnp.zeros((bsz, GLA_HEADS, GLA_DV, GLA_HEADS, GLA_DK), jnp.float32)
        s0 = s0.at[:, heads, :, heads, :].set(gla_state.transpose(1, 0, 3, 2)).reshape(bsz, GLA_V, GLA_QK)
    tile = lambda width, blk: pl.BlockSpec((1, tt, width), lambda b, i: (b, i, blk))
    fixed2 = lambda shape: pl.BlockSpec(shape, lambda b, i: (0, 0))
    per_b = pl.BlockSpec((1, GLA_V, GLA_QK), lambda b, i: (b, 0, 0))
    o, s_t = pl.pallas_call(
        functools.partial(_gla_body, t_valid=t_),
        grid=(bsz, tp // tt),
        in_specs=[tile(GLA_QK, 0), tile(GLA_QK, 1), tile(GLA_V, 1), tile(GLA_V, 2),
                  tile(LANE, (2 * GLA_QK + 2 * GLA_V + NSA_SIZES[0] + NSA_SIZES[1]) // LANE),
                  fixed2((GLA_RANK, GLA_QK)), fixed2((1, GLA_QK)), fixed2((1, GLA_DV)), per_b,
                  fixed2((GLA_QK, GLA_V)), fixed2((GLA_V, GLA_QK))],
        out_specs=[pl.BlockSpec((1, tt, GLA_V), lambda b, i: (b, i, 0)), per_b],
        out_shape=[jax.ShapeDtypeStruct((bsz, tp, GLA_V), jnp.float32),
                   jax.ShapeDtypeStruct((bsz, GLA_V, GLA_QK), jnp.float32)],
        scratch_shapes=[pltpu.VMEM((GLA_V, GLA_QK), jnp.float32), pltpu.VMEM((tt, GLA_QK), jnp.float32),
                        pltpu.VMEM((tt, GLA_QK), jnp.bfloat16)],
        compiler_params=pltpu.CompilerParams(dimension_semantics=("arbitrary", "arbitrary"),
                                             vmem_limit_bytes=48 * 1024 * 1024),
        name="gla",
    )(h, h, h, h, h, w_gla_gate.astype(jnp.bfloat16), b_gla_gate.reshape(1, GLA_QK),
      gla_norm_g.reshape(1, GLA_DV), s0, jnp.asarray(expand, jnp.bfloat16), bdmask)
    s_new = s_t.reshape(bsz, GLA_HEADS, GLA_DV, GLA_HEADS, GLA_DK)[:, heads, :, heads, :]
    return o[:, :t_], s_new.transpose(1, 0, 3, 2)


COL_NQ = 2 * GLA_QK + 2 * GLA_V
COL_NKV = COL_NQ + NSA_SIZES[0]
COL_TAIL = COL_NKV + NSA_SIZES[1]
TAIL_GATE = GLA_RANK
_ORIG = np.cumsum((0,) + GLA_SIZES + NSA_SIZES)
IN_AB_PERM = np.concatenate([np.arange(_ORIG[0], _ORIG[4]), np.arange(_ORIG[5], _ORIG[7]),
                             np.arange(_ORIG[4], _ORIG[5]), np.arange(_ORIG[7], _ORIG[8])])
SUBS = Q_BLK // CMP_STRIDE


def _nsa_prep_body(nq_ref, kv0_ref, kv1_ref, kv2_ref, tail_ref, rc_ref, ru_ref, rd_ref, pool_ref,
                   rows_ref, win_ref, kk_ref, vvt_ref, qr_ref, qo_ref, g_ref, pooled_ref):
    bf16 = jnp.bfloat16
    q0 = pl.program_id(1) * Q_BLK
    kv_w = NSA_KV_HEADS * HEAD_DIM

    def rope(x):
        reps = x.shape[1] // LANE
        wide = lambda r: jnp.concatenate([r[...]] * reps, axis=1) if reps > 1 else r[...]
        half = ROPE_DIM // 2
        return (x * wide(rc_ref) + pltpu.roll(x, half, axis=1) * wide(ru_ref)
                + pltpu.roll(x, x.shape[1] - half, axis=1) * wide(rd_ref))

    kv0, kv1, kv2 = kv0_ref[0], kv1_ref[0], kv2_ref[0]
    k_sel, v_sel = rope(kv1[:, :kv_w]), kv1[:, kv_w:]
    k_win, v_win = rope(kv2[:, :kv_w]), kv2[:, kv_w:]
    rows_ref[0] = jnp.concatenate([kv0, k_sel, v_sel], axis=1)
    win_ref[0] = jnp.concatenate([k_win, v_win], axis=1)
    blk_id = lax.shift_right_logical(q0 + lax.broadcasted_iota(jnp.int32, (Q_BLK, N_SELB), 0),
                                     int(math.log2(SEL_BLK)))
    onehot = jnp.where(lax.broadcasted_iota(jnp.int32, (Q_BLK, N_SELB), 1) == blk_id, 1.0, 0.0).astype(bf16)
    q = nq_ref[0] * (HEAD_DIM ** -0.5)
    q_rot = rope(q)
    gates_t = jax.nn.sigmoid(tail_ref[0]).T
    for h in range(NSA_KV_HEADS):
        hs = slice(h * HEAD_DIM, (h + 1) * HEAD_DIM)
        kk_ref[0, h] = jnp.concatenate([k_sel[:, hs].astype(bf16), k_win[:, hs].astype(bf16), onehot], axis=1)
        vvt_ref[0, h, 0] = jnp.concatenate([v_sel[:, hs], v_win[:, hs]], axis=1).T.astype(bf16)
        gw = NSA_GROUP * HEAD_DIM
        for src, dst in ((q, qr_ref), (q_rot, qo_ref)):
            t = src[:, h * gw:(h + 1) * gw].T
            dst[0, h, 0] = jnp.concatenate([t[g * HEAD_DIM:(g + 1) * HEAD_DIM] for g in range(NSA_GROUP)],
                                           axis=1).astype(bf16)
        base = TAIL_GATE + h * NSA_GROUP * 3
        g_ref[0, h, 0] = jnp.concatenate(
            [jnp.concatenate([gates_t[base + 3 * g + c:base + 3 * g + c + 1] for g in range(NSA_GROUP)], axis=1)
             for c in range(3)], axis=0)
    kc_in, vc_in = kv0[:, :kv_w].astype(bf16), kv0[:, kv_w:].astype(bf16)
    pooled_ref[0] = jnp.concatenate([_dot(pool_ref[0], kc_in), _dot(pool_ref[1], kc_in),
                                     _dot(pool_ref[2], vc_in), _dot(pool_ref[3], vc_in)], axis=1)


def _nsa_prep(h, pos, w_cmp_pool):
    bsz, t_, _ = h.shape
    nqb = t_ // Q_BLK
    bf16 = jnp.bfloat16
    half = ROPE_DIM // 2
    inv_freq = jnp.power(ROPE_THETA, -jnp.arange(half, dtype=jnp.float32) / half)
    ang = pos.astype(jnp.float32)[:, None] * inv_freq
    cos, sin = jnp.cos(ang), jnp.sin(ang)
    rest = HEAD_DIM - ROPE_DIM
    z8, zr = jnp.zeros((t_, half), jnp.float32), jnp.zeros((t_, rest), jnp.float32)
    two = lambda a: jnp.concatenate([a, a], axis=1)
    rc = two(jnp.concatenate([cos, cos, jnp.ones((t_, rest), jnp.float32)], axis=1))
    ru = two(jnp.concatenate([z8, sin, zr], axis=1))
    rd = two(jnp.concatenate([-sin, z8, zr], axis=1))
    pool = _pool_matrices(w_cmp_pool)
    kv_w = NSA_KV_HEADS * HEAD_DIM
    col = lambda width, off: pl.BlockSpec((1, Q_BLK, width), lambda b, i: (b, i, off // width))
    rows_t = pl.BlockSpec((Q_BLK, LANE), lambda b, i: (i, 0))
    head4 = lambda r, c: pl.BlockSpec((1, NSA_KV_HEADS, 1, r, c), lambda b, i: (b, 0, i, 0, 0))
    return pl.pallas_call(
        _nsa_prep_body,
        grid=(bsz, nqb),
        in_specs=[col(NSA_SIZES[0], COL_NQ), col(2 * kv_w, COL_NKV), col(2 * kv_w, COL_NKV + 2 * kv_w),
                  col(2 * kv_w, COL_NKV + 4 * kv_w), col(LANE, COL_TAIL), rows_t, rows_t, rows_t,
                  pl.BlockSpec((4, SUBS, Q_BLK), lambda b, i: (0, 0, 0))],
        out_specs=[pl.BlockSpec((1, Q_BLK, 4 * kv_w), lambda b, i: (b, i, 0)),
                   pl.BlockSpec((1, Q_BLK, 2 * kv_w), lambda b, i: (b, i, 0)),
                   pl.BlockSpec((1, NSA_KV_HEADS, Q_BLK, KK_W), lambda b, i: (b, 0, i, 0)),
                   head4(2 * HEAD_DIM, Q_BLK), head4(HEAD_DIM, NSA_ROWS), head4(HEAD_DIM, NSA_ROWS),
                   head4(3, NSA_ROWS),
                   pl.BlockSpec((1, SUBS, 4 * kv_w), lambda b, i: (b, i, 0))],
        out_shape=[jax.ShapeDtypeStruct((bsz, t_, 4 * kv_w), jnp.float32),
                   jax.ShapeDtypeStruct((bsz, t_, 2 * kv_w), jnp.float32),
                   jax.ShapeDtypeStruct((bsz, NSA_KV_HEADS, t_, KK_W), bf16),
                   jax.ShapeDtypeStruct((bsz, NSA_KV_HEADS, nqb, 2 * HEAD_DIM, Q_BLK), bf16),
                   jax.ShapeDtypeStruct((bsz, NSA_KV_HEADS, nqb, HEAD_DIM, NSA_ROWS), bf16),
                   jax.ShapeDtypeStruct((bsz, NSA_KV_HEADS, nqb, HEAD_DIM, NSA_ROWS), bf16),
                   jax.ShapeDtypeStruct((bsz, NSA_KV_HEADS, nqb, 3, NSA_ROWS), jnp.float32),
                   jax.ShapeDtypeStruct((bsz, t_ // CMP_STRIDE, 4 * kv_w), jnp.float32)],
        compiler_params=pltpu.CompilerParams(dimension_semantics=("arbitrary", "arbitrary")),
        name="nsa_prep",
    )(h, h, h, h, h, rc, ru, rd, pool)


PAGE_GROUP = 8
DEC_KEYS = PAGE_GROUP * PAGE_SIZE
NEW_PAD = 8
KV_W = NSA_KV_HEADS * HEAD_DIM


def _dec_pool_body(pt_ref, *refs):
    page_refs, pool_ref, out_ref = refs[:PAGE_GROUP], refs[PAGE_GROUP], refs[PAGE_GROUP + 1]
    bf16 = jnp.bfloat16
    parts = []
    for pr in page_refs:
        kv0 = pr[0]
        kc_in, vc_in = kv0[:, :KV_W].astype(bf16), kv0[:, KV_W:].astype(bf16)
        parts.append(jnp.concatenate([_dot(pool_ref[0], kc_in), _dot(pool_ref[1], kc_in),
                                      _dot(pool_ref[2], vc_in), _dot(pool_ref[3], vc_in)], axis=1))
    out_ref[0] = jnp.concatenate(parts, axis=0)


def _page_specs(n_pages, col_blk):
    def spec(i):
        return pl.BlockSpec((1, PAGE_SIZE, 2 * KV_W),
                            lambda b, j, pt: (pt[b * n_pages + j * PAGE_GROUP + i], 0, col_blk))
    return [spec(i) for i in range(PAGE_GROUP)]


def _dec_pool(cache, page_table, pool):
    bsz, n_pages = page_table.shape
    grid_spec = pltpu.PrefetchScalarGridSpec(
        num_scalar_prefetch=1, grid=(bsz, n_pages // PAGE_GROUP),
        in_specs=_page_specs(n_pages, 0) + [pl.BlockSpec((4, SUBS, Q_BLK), lambda b, j, pt: (0, 0, 0))],
        out_specs=pl.BlockSpec((1, PAGE_GROUP * SUBS, 4 * KV_W), lambda b, j, pt: (b, j, 0)))
    return pl.pallas_call(
        _dec_pool_body, grid_spec=grid_spec,
        out_shape=jax.ShapeDtypeStruct((bsz, n_pages * SUBS, 4 * KV_W), jnp.float32),
        compiler_params=pltpu.CompilerParams(dimension_semantics=("arbitrary", "arbitrary")),
        name="nsa_dec_pool",
    )(page_table.reshape(-1), *([cache] * PAGE_GROUP), pool)


def _dec_select_body(qr_ref, kct_ref, vc_ref, band_ref, oc_ref, selb_ref, *, qpos0, n_q, n_pick, n_blk):
    f32, bf16 = jnp.float32, jnp.bfloat16
    n_cmp = kct_ref.shape[3]
    rows = NSA_GROUP * n_q
    for h in range(NSA_KV_HEADS):
        s_c = _dot(qr_ref[0, h], kct_ref[0, h])
        n_idx = lax.broadcasted_iota(jnp.int32, (rows, n_cmp), 1)
        qpos = qpos0 + (lax.broadcasted_iota(jnp.int32, (rows, n_cmp), 0) % n_q)
        cmask = (n_idx * CMP_STRIDE + (CMP_BLK - 1)) <= qpos
        s_c = jnp.where(cmask, s_c, MASKED)
        p_c = jnp.where(cmask, jnp.exp(s_c - jnp.max(s_c, axis=1, keepdims=True)), 0.0)
        p_c = p_c / jnp.maximum(jnp.sum(p_c, axis=1, keepdims=True), 1e-30)
        oc_ref[0, h] = _dot(p_c.astype(bf16), vc_ref[0, h])
        imp = p_c[0:n_q]
        for g in range(1, NSA_GROUP):
            imp = imp + p_c[g * n_q:(g + 1) * n_q]
        imp_s = jnp.zeros((n_q, N_SELB), f32)
        rem = imp
        for _ in range(3):
            part = rem.astype(bf16)
            imp_s = imp_s + _dot(part, band_ref[...])
            rem = rem - part.astype(f32)
        blk = lax.broadcasted_iota(jnp.int32, (n_q, N_SELB), 1)
        qpos_s = qpos0 + lax.broadcasted_iota(jnp.int32, (n_q, N_SELB), 0)
        cur = lax.shift_right_logical(qpos_s, int(math.log2(SEL_BLK)))
        valid = (blk * SEL_BLK <= qpos_s) & (blk < n_blk)
        forced = (blk == 0) | (blk == cur) | (blk == cur - 1)
        score = jnp.where(valid, imp_s + jnp.where(forced, FORCE_BONUS, 0.0), -1e30)
        picked = jnp.zeros((n_q, N_SELB), f32)
        for _ in range(n_pick):
            best = jnp.max(score, axis=1, keepdims=True)
            first = jnp.min(jnp.where(score == best, blk, N_SELB), axis=1, keepdims=True)
            hit = blk == first
            picked = jnp.where(hit, 1.0, picked)
            score = jnp.where(hit, -3e38, score)
        selb_ref[0, h] = (jnp.where(valid, picked, 0.0) - 1.0) * (-MASKED)


def _dec_select(qr, kct, vc, n_q, qpos0, n_pick, n_blk):
    bsz = qr.shape[0]
    rows = NSA_GROUP * n_q
    n_cmp = kct.shape[3]
    ratio = SEL_BLK // CMP_STRIDE
    c_idx, j_idx = np.arange(n_cmp)[:, None], np.arange(N_SELB)[None, :]
    band = jnp.asarray(((c_idx >= ratio * j_idx - 1) & (c_idx <= ratio * j_idx + ratio - 1)), jnp.bfloat16)
    per_b = lambda *tail: pl.BlockSpec((1, NSA_KV_HEADS) + tail, lambda b: (b, 0, 0, 0))
    return pl.pallas_call(
        functools.partial(_dec_select_body, qpos0=qpos0, n_q=n_q, n_pick=n_pick, n_blk=n_blk),
        grid=(bsz,),
        in_specs=[per_b(rows, HEAD_DIM), per_b(HEAD_DIM, n_cmp), per_b(n_cmp, HEAD_DIM),
                  pl.BlockSpec((n_cmp, N_SELB), lambda b: (0, 0))],
        out_specs=[per_b(rows, HEAD_DIM), per_b(n_q, N_SELB)],
        out_shape=[jax.ShapeDtypeStruct((bsz, NSA_KV_HEADS, rows, HEAD_DIM), jnp.float32),
                   jax.ShapeDtypeStruct((bsz, NSA_KV_HEADS, n_q, N_SELB), jnp.float32)],
        compiler_params=pltpu.CompilerParams(dimension_semantics=("arbitrary",)),
        name="nsa_dec_select",
    )(qr, kct, vc, band)


def _dec_attend_body(pt_ref, *refs, qpos0, n_q, past):
    page_refs = refs[:PAGE_GROUP]
    (qs_ref, qw_ref, knew_ref, vnew_ref, wbuf_ref, wnew_ref, oc_ref, g_ref,
     o_ref, m_ref, l_ref, acc_ref) = refs[PAGE_GROUP:]
    f32, bf16 = jnp.float32, jnp.bfloat16
    j = pl.program_id(1)
    n_rows = qs_ref.shape[1]

    @pl.when(j == 0)
    def _():
        m_ref[...] = jnp.full(m_ref.shape, MASKED, f32)
        l_ref[...] = jnp.zeros(l_ref.shape, f32)
        acc_ref[...] = jnp.zeros(acc_ref.shape, f32)

    def online(s, v):
        m_old = m_ref[...]
        m_new = jnp.maximum(m_old, jnp.max(s, axis=1, keepdims=True))
        alpha = jnp.exp(m_old - m_new)
        p = jnp.exp(s - m_new)
        l_ref[...] = alpha * l_ref[...] + jnp.sum(p, axis=1, keepdims=True)
        acc_ref[...] = alpha * acc_ref[...] + _dot(p.astype(bf16), v)
        m_ref[...] = m_new

    qs = qs_ref[0]
    pages = [pr[0] for pr in page_refs]
    keys = jnp.concatenate([p[:, :KV_W] for p in pages], axis=0).astype(bf16)
    vals = jnp.concatenate([p[:, KV_W:] for p in pages], axis=0).astype(bf16)
    blk_id = j * (DEC_KEYS // SEL_BLK) + lax.shift_right_logical(
        lax.broadcasted_iota(jnp.int32, (DEC_KEYS, N_SELB), 0), int(math.log2(SEL_BLK)))
    onehot = jnp.where(lax.broadcasted_iota(jnp.int32, (DEC_KEYS, N_SELB), 1) == blk_id, 1.0, 0.0).astype(bf16)
    online(_dot_nt(qs, jnp.concatenate([keys, onehot], axis=1)), vals)

    @pl.when(j == pl.num_programs(1) - 1)
    def _():
        row_q = qpos0 + (lax.broadcasted_iota(jnp.int32, (n_rows, 1), 0) % n_q)
        qh = qw_ref[0]
        new_pos = past + lax.broadcasted_iota(jnp.int32, (n_rows, NEW_PAD), 1)
        new_ok = (new_pos <= row_q) & (new_pos < past + n_q)
        s_new = jnp.where(new_ok, _dot_nt(qh, knew_ref[0]), MASKED)
        online(s_new, vnew_ref[0])
        o_s = acc_ref[...] / l_ref[...]
        wbuf = wbuf_ref[0]
        wnew = wnew_ref[0]
        n_buf = wbuf.shape[0]
        s_b = _dot_nt(qh, wbuf[:, :KV_W].astype(bf16))
        pos_b = (past - n_buf) + lax.broadcasted_iota(jnp.int32, (n_rows, n_buf), 1)
        s_b = jnp.where((pos_b > row_q - WINDOW) & (pos_b >= 0), s_b, MASKED)
        s_n = jnp.where(new_ok, _dot_nt(qh, wnew[:, :KV_W].astype(bf16)), MASKED)
        m_w = jnp.maximum(jnp.max(s_b, axis=1, keepdims=True), jnp.max(s_n, axis=1, keepdims=True))
        p_b, p_n = jnp.exp(s_b - m_w), jnp.exp(s_n - m_w)
        l_w = jnp.sum(p_b, axis=1, keepdims=True) + jnp.sum(p_n, axis=1, keepdims=True)
        o_w = (_dot(p_b.astype(bf16), wbuf[:, KV_W:].astype(bf16))
               + _dot(p_n.astype(bf16), wnew[:, KV_W:].astype(bf16))) / l_w
        half = n_rows // NSA_KV_HEADS
        own = lambda a: jnp.concatenate([a[h * half:(h + 1) * half, h * HEAD_DIM:(h + 1) * HEAD_DIM]
                                         for h in range(NSA_KV_HEADS)], axis=0)
        g = g_ref[0]
        o_ref[0] = g[:, 0:1] * oc_ref[0] + g[:, 1:2] * own(o_s) + g[:, 2:3] * own(o_w)


def _dec_attend(cache, page_table, qs, qw, knew, vnew, wbuf, wnew, o_c, gates, n_q, qpos0):
    bsz, n_pages = page_table.shape
    n_rows = qs.shape[1]
    per_b = lambda *tail: pl.BlockSpec((1,) + tail, lambda b, j, pt: (b, 0, 0))
    grid_spec = pltpu.PrefetchScalarGridSpec(
        num_scalar_prefetch=1, grid=(bsz, n_pages // PAGE_GROUP),
        in_specs=_page_specs(n_pages, 1) + [
            per_b(n_rows, KV_W + N_SELB), per_b(n_rows, KV_W), per_b(NEW_PAD, KV_W), per_b(NEW_PAD, KV_W),
            per_b(wbuf.shape[1], 2 * KV_W), per_b(NEW_PAD, 2 * KV_W), per_b(n_rows, HEAD_DIM), per_b(n_rows, 3)],
        out_specs=per_b(n_rows, HEAD_DIM),
        scratch_shapes=[pltpu.VMEM((n_rows, 1), jnp.float32), pltpu.VMEM((n_rows, 1), jnp.float32),
                        pltpu.VMEM((n_rows, KV_W), jnp.float32)])
    return pl.pallas_call(
        functools.partial(_dec_attend_body, qpos0=qpos0, n_q=n_q, past=n_pages * PAGE_SIZE),
        grid_spec=grid_spec,
        out_shape=jax.ShapeDtypeStruct((bsz, n_rows, HEAD_DIM), jnp.float32),
        compiler_params=pltpu.CompilerParams(dimension_semantics=("arbitrary", "arbitrary")),
        name="nsa_dec_attend",
    )(page_table.reshape(-1), *([cache] * PAGE_GROUP), qs, qw, knew, vnew, wbuf, wnew, o_c, gates)


def _pool_matrices(w_cmp_pool):
    sub = np.arange(Q_BLK) // CMP_STRIDE == np.arange(SUBS)[:, None]
    w_rep = jnp.tile(w_cmp_pool.reshape(2, 2, CMP_STRIDE), (1, 1, SUBS))
    return jnp.where(sub[None, None], w_rep[:, :, None, :], 0.0).reshape(4, SUBS, Q_BLK).astype(jnp.bfloat16)


def _compressed_from_pooled(pooled):
    bsz, n_sub, _ = pooled.shape
    pooled = pooled.reshape(bsz, n_sub, 4, NSA_KV_HEADS, HEAD_DIM)
    kc = pooled[:, :-1, 0] + pooled[:, 1:, 1]
    vc = pooled[:, :-1, 2] + pooled[:, 1:, 3]
    pad = lambda a: jnp.pad(a, ((0, 0), (0, 1), (0, 0), (0, 0))).transpose(0, 2, 1, 3)
    return pad(kc), pad(vc)


def _nsa_decode(q_raw, q_rot, gates, rows_full, rows_win, cache, page_table, win_buf, w_cmp_pool, past):
    bsz, n_q = q_raw.shape[:2]
    bf16 = jnp.bfloat16
    n_blk = past // SEL_BLK
    assert past % DEC_KEYS == 0 and n_blk <= N_SELB and n_q <= NEW_PAD
    scale = HEAD_DIM ** -0.5
    cache2 = cache.reshape(cache.shape[0], PAGE_SIZE, 4 * KV_W)
    pooled = _dec_pool(cache2, page_table, _pool_matrices(w_cmp_pool))
    kc_p, vc_p = _compressed_from_pooled(pooled)
    rows_of = lambda a: a.transpose(0, 2, 3, 1, 4).reshape(bsz, NSA_KV_HEADS, NSA_GROUP * n_q, a.shape[-1])
    qr = rows_of((q_raw * scale).astype(bf16))
    n_pick = min(SEL_TOPN, n_blk + 1) - 1
    o_c, selb = _dec_select(qr, kc_p.transpose(0, 1, 3, 2).astype(bf16), vc_p.astype(bf16), n_q, past, n_pick, n_blk)
    qo = rows_of((q_rot * scale).astype(bf16))
    zero = jnp.zeros_like(qo[:, 0])
    qw = jnp.concatenate([jnp.concatenate([qo[:, 0], zero], -1), jnp.concatenate([zero, qo[:, 1]], -1)], axis=1)
    bias = jnp.tile(selb, (1, 1, NSA_GROUP, 1)).reshape(bsz, -1, N_SELB).astype(bf16)
    qs = jnp.concatenate([qw, bias], axis=-1)
    pad_new = lambda a: jnp.pad(a.reshape(bsz, n_q, -1), ((0, 0), (0, NEW_PAD - n_q), (0, 0)))
    knew = pad_new(rows_full[:, :, 2]).astype(bf16)
    vnew = pad_new(rows_full[:, :, 3]).astype(bf16)
    wnew = pad_new(rows_win)
    wbuf = win_buf.reshape(bsz, win_buf.shape[1], 2 * KV_W)
    gt = rows_of(gates).reshape(bsz, -1, 3)
    o = _dec_attend(cache2, page_table, qs, qw, knew, vnew, wbuf, wnew,
                    o_c.reshape(bsz, -1, HEAD_DIM), gt, n_q, past)
    o = o.reshape(bsz, NSA_KV_HEADS, NSA_GROUP, n_q, HEAD_DIM).transpose(0, 3, 1, 2, 4)
    return o.reshape(bsz, n_q, NSA_HEADS * HEAD_DIM)


def _ab_mixer(x, pos, w_in, w_gla_gate, b_gla_gate, gla_norm_g, w_cmp_pool, w_out,
              gla_state, nsa_cache, page_table, win_buf):
    bsz, t_, _ = x.shape
    h_in = _mm(x.reshape(bsz * t_, -1), w_in[:, IN_AB_PERM], keep_pad=True).reshape(bsz, t_, -1)
    o_a, s_a = _gla(h_in, w_gla_gate, b_gla_gate, gla_norm_g, gla_state)
    kv_w = NSA_KV_HEADS * HEAD_DIM
    if nsa_cache is None:
        rows2, win2, kk, vvt, qr, qo, gt, pooled = _nsa_prep(h_in, pos, w_cmp_pool)
        pooled = pooled.reshape(bsz, t_ // CMP_STRIDE, 4, NSA_KV_HEADS, HEAD_DIM)
        kc = pooled[:, :-1, 0] + pooled[:, 1:, 1]
        vc = pooled[:, :-1, 2] + pooled[:, 1:, 3]
        kc_p = jnp.pad(kc, ((0, 0), (0, 1), (0, 0), (0, 0))).transpose(0, 2, 1, 3).astype(jnp.bfloat16)
        vct = jnp.pad(vc, ((0, 0), (0, 1), (0, 0), (0, 0))).transpose(0, 2, 3, 1).astype(jnp.bfloat16)
        o_b = _nsa_prompt(qr, qo, gt, kc_p, vct, kk, vvt)
        rows_full = rows2.reshape(bsz, t_, 4, NSA_KV_HEADS, HEAD_DIM)
        new_win = win2[:, -min(WINDOW, t_):].reshape(bsz, -1, 2, NSA_KV_HEADS, HEAD_DIM)
    else:
        nq = h_in[..., COL_NQ:COL_NKV]
        nkv = h_in[..., COL_NKV:COL_TAIL]
        ngate = h_in[..., COL_TAIL + TAIL_GATE:COL_TAIL + TAIL_GATE + NSA_SIZES[2]]
        q_raw = nq.reshape(bsz, t_, NSA_KV_HEADS, NSA_GROUP, HEAD_DIM)
        q_rot = _partial_rope(q_raw, pos)
        kv = nkv.reshape(bsz, t_, 6, NSA_KV_HEADS, HEAD_DIM)
        k_sel = _partial_rope(kv[:, :, 2], pos)
        k_win = _partial_rope(kv[:, :, 4], pos)
        rows_full = jnp.stack([kv[:, :, 0], kv[:, :, 1], k_sel, kv[:, :, 3]], axis=2)
        rows_win = jnp.stack([k_win, kv[:, :, 5]], axis=2)
        gates = jax.nn.sigmoid(ngate).reshape(bsz, t_, NSA_KV_HEADS, NSA_GROUP, 3)
        past_len = page_table.shape[1] * PAGE_SIZE
        o_b = _nsa_decode(q_raw, q_rot, gates, rows_full, rows_win, nsa_cache, page_table, win_buf,
                          w_cmp_pool, past_len)
        w_buf = win_buf.shape[1]
        kw = jnp.concatenate([win_buf, rows_win], axis=1)
        new_win = kw[:, -w_buf:]
    y = _mm3(jnp.concatenate([o_a, o_b], axis=-1), w_out)
    return y, s_a, rows_full, new_win


CONV_HALO = 32
CONV_LEAD = CONV_HALO - (CONV_W - 1)


def _conv_body(x_ref, buf0_ref, w1_ref, b1_ref, wdw_ref, bdw_ref, g_ref, b_ref, w2_ref, b2_ref,
               o_ref, tail_ref, ext_ref, *, t_last):
    bf16 = jnp.bfloat16
    tt = x_ref.shape[1]
    i = pl.program_id(1)

    @pl.when(i == 0)
    def _():
        ext_ref[0:CONV_HALO, :] = buf0_ref[0]

    h = _dot(x_ref[0].astype(bf16), w1_ref[...]) + b1_ref[...]
    ext_ref[CONV_HALO:CONV_HALO + tt, :] = h[:, :D_CONV] * jax.nn.sigmoid(h[:, D_CONV:])
    c = jnp.zeros((tt, D_CONV), jnp.float32) + bdw_ref[...]
    for k in range(CONV_W):
        c = c + ext_ref[pl.ds(CONV_LEAD + k, tt), :] * wdw_ref[k:k + 1, :]
    c = _ln_rows(c, g_ref[...], b_ref[...])
    c = c * jax.nn.sigmoid(c)
    o_ref[0] = _dot(c.astype(bf16), w2_ref[...]) + b2_ref[...]
    tail_ref[0] = ext_ref[t_last:t_last + CONV_HALO, :]
    ext_ref[0:CONV_HALO, :] = ext_ref[tt:tt + CONV_HALO, :]


def _conv_module(x, conv_buf, w_pw1, b_pw1, w_dw, b_dw, ln_g, ln_b, w_pw2, b_pw2):
    bsz, t_, d = x.shape
    bf16 = jnp.bfloat16
    tp = -(-t_ // 8) * 8
    tt = min(tp, 256)
    n_t = tp // tt
    if tp != t_:
        x = jnp.pad(x, ((0, 0), (0, tp - t_), (0, 0)))
    if conv_buf is None:
        buf0 = jnp.zeros((bsz, CONV_HALO, D_CONV), jnp.float32)
    else:
        buf0 = jnp.pad(conv_buf, ((0, 0), (CONV_LEAD, 0), (0, 0)))
    fixed = lambda shape: pl.BlockSpec(shape, lambda b, i: (0,) * len(shape))
    per_b = pl.BlockSpec((1, CONV_HALO, D_CONV), lambda b, i: (b, 0, 0))
    out, tail = pl.pallas_call(
        functools.partial(_conv_body, t_last=t_ - (n_t - 1) * tt),
        grid=(bsz, n_t),
        in_specs=[pl.BlockSpec((1, tt, d), lambda b, i: (b, i, 0)), per_b,
                  fixed((d, 2 * D_CONV)), fixed((1, 2 * D_CONV)), fixed((CONV_HALO, D_CONV)), fixed((1, D_CONV)),
                  fixed((1, D_CONV)), fixed((1, D_CONV)), fixed((D_CONV, d)), fixed((1, d))],
        out_specs=[pl.BlockSpec((1, tt, d), lambda b, i: (b, i, 0)), per_b],
        out_shape=[jax.ShapeDtypeStruct((bsz, tp, d), jnp.float32),
                   jax.ShapeDtypeStruct((bsz, CONV_HALO, D_CONV), jnp.float32)],
        scratch_shapes=[pltpu.VMEM((CONV_HALO + tt, D_CONV), jnp.float32)],
        compiler_params=pltpu.CompilerParams(dimension_semantics=("arbitrary", "arbitrary"),
                                             vmem_limit_bytes=48 * 1024 * 1024),
        name="conv_module",
    )(x, buf0, w_pw1.astype(bf16), b_pw1.reshape(1, -1), jnp.pad(w_dw, ((0, CONV_HALO - CONV_W), (0, 0))),
      b_dw.reshape(1, -1), ln_g.reshape(1, -1), ln_b.reshape(1, -1), w_pw2.astype(bf16), b_pw2.reshape(1, -1))
    return out[:, :t_], tail[:, CONV_LEAD:]


PACK_W = 256
SC_WINDOW = 128
SC_TILES = 32


def _pack_rows(y):
    out = []
    for h in range(2):
        lo = lax.bitcast_convert_type(y[:, 2 * h * PACK_W:(2 * h + 1) * PACK_W].astype(jnp.bfloat16)
                                      .astype(jnp.float32), jnp.uint32)
        hi = lax.bitcast_convert_type(y[:, (2 * h + 1) * PACK_W:(2 * h + 2) * PACK_W].astype(jnp.bfloat16)
                                      .astype(jnp.float32), jnp.uint32)
        out.append(lax.bitcast_convert_type((lo >> 16) | hi, jnp.int32))
    return out


def _unpack_words(w):
    u = lax.bitcast_convert_type(w, jnp.uint32)
    lo = lax.bitcast_convert_type(u << 16, jnp.float32)
    hi = lax.bitcast_convert_type(u & jnp.uint32(0xFFFF0000), jnp.float32)
    return lo, hi


def _gather_rows(src, idx):
    n = idx.shape[0]
    if n % (SC_WINDOW * SC_TILES) != 0:
        return jnp.take(src, idx, axis=0)
    mesh = plsc.VectorSubcoreMesh(core_axis_name="core", subcore_axis_name="subcore")

    @pl.kernel(out_type=jax.ShapeDtypeStruct((n, src.shape[1]), src.dtype), mesh=mesh)
    def gather_kernel(src_hbm, idx_hbm, out_hbm):
        def step(idx_vmem, out_vmem):
            pltpu.sync_copy(src_hbm.at[idx_vmem.at[0]], out_vmem)

        pltpu.emit_pipeline(
            step, grid=(n // SC_WINDOW,),
            in_specs=[pl.BlockSpec((1, SC_WINDOW), index_map=lambda i: (0, i))],
            out_specs=[pl.BlockSpec((SC_WINDOW, src.shape[1]), index_map=lambda i: (i, 0))],
            core_axis_name=("core", "subcore"),
            dimension_semantics=(pltpu.PARALLEL,),
        )(idx_hbm, out_hbm)

    return gather_kernel(src, idx.reshape(1, n))


def _scatter_rows(src, idx, n_out):
    n = idx.shape[0]
    m = src.shape[0] // 2
    reps = n // (2 * m)
    if n % (SC_WINDOW * SC_TILES) != 0 or m % SC_WINDOW != 0:
        rows = jnp.arange(n, dtype=jnp.int32)
        src_row = (rows // (reps * m)) * m + rows % m
        return jnp.zeros((n_out, src.shape[1]), src.dtype).at[idx].set(jnp.take(src, src_row, axis=0))
    tiles = m // SC_WINDOW
    mesh = plsc.VectorSubcoreMesh(core_axis_name="core", subcore_axis_name="subcore")

    @pl.kernel(out_type=jax.ShapeDtypeStruct((n_out, src.shape[1]), src.dtype), mesh=mesh, scratch_types=[])
    def scatter_kernel(src_hbm, idx_hbm, out_hbm):
        def step(src_vmem, idx_vmem):
            pltpu.sync_copy(src_vmem, out_hbm.at[idx_vmem.at[0]])

        pltpu.emit_pipeline(
            step, grid=(n // SC_WINDOW,),
            in_specs=[pl.BlockSpec((SC_WINDOW, src.shape[1]),
                                   index_map=lambda i: ((i // (reps * tiles)) * tiles + i % tiles, 0)),
                      pl.BlockSpec((1, SC_WINDOW), index_map=lambda i: (0, i))],
            out_specs=[],
            core_axis_name=("core", "subcore"),
            dimension_semantics=(pltpu.PARALLEL,),
        )(src_hbm, idx_hbm)

    return scatter_kernel(src, idx.reshape(1, n))


PER_GROUP = N_EXPERTS // N_GROUPS
PICKED = -3e38


def _ln_rows(v, g, b):
    mu = jnp.mean(v, axis=-1, keepdims=True)
    c = v - mu
    var = jnp.mean(c * c, axis=-1, keepdims=True)
    return c * lax.rsqrt(var + LN_EPS) * g + b


def _first_max(v, ids, axes, sentinel):
    best = v
    for a in axes:
        best = jnp.max(best, axis=a, keepdims=True)
    first = jnp.where(v == best, ids, sentinel)
    for a in axes:
        first = jnp.min(first, axis=a, keepdims=True)
    return best, first


def _sum_axes(v, axes):
    for a in axes:
        v = jnp.sum(v, axis=a, keepdims=True)
    return v


def _moe_pre_body(x_ref, mix_ref, g_ref, b_ref, wr_ref, br_ref, wgu_ref, wdn_ref,
                  x1_ref, xp_ref, sh_ref, eidx_ref, gate_ref, rank_ref, cnt_ref, run_ref):
    f32, bf16 = jnp.float32, jnp.bfloat16
    tm = x_ref.shape[0]

    @pl.when(pl.program_id(0) == 0)
    def _():
        run_ref[...] = jnp.zeros(run_ref.shape, f32)

    x1 = _ln_rows(ALPHA * x_ref[...] + mix_ref[...], g_ref[...], b_ref[...])
    x1_ref[...] = x1
    x1b = x1.astype(bf16)
    xp_ref[0], xp_ref[1] = _pack_rows(x1)

    h = _dot(x1b, wgu_ref[...])
    d_sh = h.shape[1] // 2
    act = (jax.nn.silu(h[:, :d_sh]) * h[:, d_sh:]).astype(bf16)
    sh_ref[...] = _dot(act, wdn_ref[...])

    s = jax.nn.sigmoid(_dot_nt(wr_ref[...], x1b)).reshape(N_GROUPS, PER_GROUP, tm)
    sb = s + br_ref[...].reshape(N_GROUPS, PER_GROUP, 1)
    shape3 = (N_GROUPS, PER_GROUP, tm)
    pid = lax.broadcasted_iota(jnp.int32, shape3, 1)
    gid = lax.broadcasted_iota(jnp.int32, (N_GROUPS, 1, tm), 0)
    eid = lax.broadcasted_iota(jnp.int32, shape3, 0) * PER_GROUP + pid
    top1, i1 = _first_max(sb, pid, (1,), PER_GROUP)
    top2 = jnp.max(jnp.where(pid == i1, PICKED, sb), axis=1, keepdims=True)
    gscore = top1 + top2
    gsel = jnp.zeros((N_GROUPS, 1, tm), f32)
    for _ in range(TOPK_GROUPS):
        _, first = _first_max(gscore, gid, (0,), N_GROUPS)
        hit = gid == first
        gsel = jnp.where(hit, 1.0, gsel)
        gscore = jnp.where(hit, PICKED, gscore)
    cand = jnp.where(gsel > 0.0, sb, -1e30)
    firsts, gates = [], []
    picked = jnp.zeros(shape3, f32)
    for _ in range(TOP_K):
        _, first = _first_max(cand, eid, (0, 1), N_EXPERTS)
        hit = eid == first
        firsts.append(first)
        gates.append(_sum_axes(jnp.where(hit, s, 0.0), (0, 1)))
        picked = jnp.where(hit, 1.0, picked)
        cand = jnp.where(hit, PICKED, cand)
    gsum = gates[0]
    for gk in gates[1:]:
        gsum = gsum + gk
    earlier = (lax.broadcasted_iota(jnp.int32, (tm, tm), 0) < lax.broadcasted_iota(jnp.int32, (tm, tm), 1))
    picked2 = picked.reshape(N_EXPERTS, tm)
    rank = run_ref[...] + _dot(picked2.astype(bf16), jnp.where(earlier, 1.0, 0.0).astype(bf16))
    run_new = run_ref[...] + jnp.sum(picked2, axis=1, keepdims=True)
    run_ref[...] = run_new
    cnt_ref[...] = jnp.broadcast_to(run_new, cnt_ref.shape)
    rank3 = rank.reshape(shape3)
    for k in range(TOP_K):
        hit = eid == firsts[k]
        eidx_ref[k:k + 1, :] = firsts[k].reshape(1, tm)
        gate_ref[k:k + 1, :] = (gates[k] / gsum * ROUTE_SCALE).reshape(1, tm)
        rank_ref[k:k + 1, :] = _sum_axes(jnp.where(hit, rank3, 0.0), (0, 1)).reshape(1, tm).astype(jnp.int32)


def _moe_pre(x, mix, g, b, w_router, b_router, w_sh_gu, w_sh_down):
    m, d = x.shape
    bf16 = jnp.bfloat16
    tm = min(m, 512)
    row = lambda i: (i, 0)
    col = lambda i: (0, i)
    fixed = lambda i: (0, 0)
    d_sh2 = w_sh_gu.shape[1]
    return pl.pallas_call(
        _moe_pre_body,
        grid=(m // tm,),
        in_specs=[pl.BlockSpec((tm, d), row), pl.BlockSpec((tm, d), row),
                  pl.BlockSpec((1, d), fixed), pl.BlockSpec((1, d), fixed),
                  pl.BlockSpec((N_EXPERTS, d), fixed), pl.BlockSpec((N_EXPERTS, 1), fixed),
                  pl.BlockSpec((d, d_sh2), fixed), pl.BlockSpec((d_sh2 // 2, d), fixed)],
        out_specs=[pl.BlockSpec((tm, d), row), pl.BlockSpec((2, tm, PACK_W), lambda i: (0, i, 0)),
                   pl.BlockSpec((tm, d), row),
                   pl.BlockSpec((TOP_K, tm), col), pl.BlockSpec((TOP_K, tm), col), pl.BlockSpec((TOP_K, tm), col),
                   pl.BlockSpec((N_EXPERTS, LANE), fixed)],
        out_shape=[jax.ShapeDtypeStruct((m, d), jnp.float32), jax.ShapeDtypeStruct((2, m, PACK_W), jnp.int32),
                   jax.ShapeDtypeStruct((m, d), jnp.float32),
                   jax.ShapeDtypeStruct((TOP_K, m), jnp.int32), jax.ShapeDtypeStruct((TOP_K, m), jnp.float32),
                   jax.ShapeDtypeStruct((TOP_K, m), jnp.int32),
                   jax.ShapeDtypeStruct((N_EXPERTS, LANE), jnp.float32)],
        scratch_shapes=[pltpu.VMEM((N_EXPERTS, 1), jnp.float32)],
        compiler_params=pltpu.CompilerParams(dimension_semantics=("arbitrary",),
                                             vmem_limit_bytes=48 * 1024 * 1024),
        name="moe_pre",
    )(x, mix, g.reshape(1, d), b.reshape(1, d), w_router.T.astype(bf16), b_router.reshape(N_EXPERTS, 1),
      w_sh_gu.astype(bf16), w_sh_down.astype(bf16))


def _moe_expert_body(exp_ref, first_ref, rows_ref, xs_ref, wgu_ref, wdn_ref, y_ref, wgu_bf, wdn_bf):
    i = pl.program_id(0)
    bf16 = jnp.bfloat16

    @pl.when(first_ref[i] == 1)
    def _():
        wgu_bf[...] = wgu_ref[0].astype(bf16)
        wdn_bf[...] = wdn_ref[0].astype(bf16)

    @pl.when(rows_ref[i] > 0)
    def _():
        live = lax.broadcasted_iota(jnp.int32, (xs_ref.shape[1], 1), 0) < rows_ref[i]
        h = None
        for hw in range(2):
            for q, xq in enumerate(_unpack_words(xs_ref[hw])):
                r0 = (2 * hw + q) * PACK_W
                part = _dot(jnp.where(live, xq, 0.0).astype(bf16), wgu_bf[r0:r0 + PACK_W, :])
                h = part if h is None else h + part
        d_e = h.shape[1] // 2
        act = (jax.nn.silu(h[:, :d_e]) * h[:, d_e:]).astype(bf16)
        y_ref[0], y_ref[1] = _pack_rows(_dot(act, wdn_bf[...]))

    @pl.when(rows_ref[i] == 0)
    def _():
        y_ref[...] = jnp.zeros(y_ref.shape, y_ref.dtype)


def _moe_experts(xs, blk_exp, blk_first, blk_rows, w_exp_gu, w_exp_down, bm):
    n_slots = xs.shape[1]
    d = w_exp_gu.shape[1]
    n_blk = n_slots // bm
    d_e2 = w_exp_gu.shape[2]
    words = lambda i, e, f, a: (0, i, 0)
    grid_spec = pltpu.PrefetchScalarGridSpec(
        num_scalar_prefetch=3,
        grid=(n_blk,),
        in_specs=[pl.BlockSpec((2, bm, PACK_W), words),
                  pl.BlockSpec((1, d, d_e2), lambda i, e, f, a: (e[i], 0, 0)),
                  pl.BlockSpec((1, d_e2 // 2, d), lambda i, e, f, a: (e[i], 0, 0))],
        out_specs=pl.BlockSpec((2, bm, PACK_W), words),
        scratch_shapes=[pltpu.VMEM((d, d_e2), jnp.bfloat16), pltpu.VMEM((d_e2 // 2, d), jnp.bfloat16)])
    return pl.pallas_call(
        _moe_expert_body,
        grid_spec=grid_spec,
        out_shape=jax.ShapeDtypeStruct((2, n_slots, PACK_W), jnp.int32),
        compiler_params=pltpu.CompilerParams(dimension_semantics=("arbitrary",),
                                             vmem_limit_bytes=48 * 1024 * 1024),
        name="moe_experts",
    )(blk_exp, blk_first, blk_rows, xs, w_exp_gu, w_exp_down)


def _combine_ln_body(x_ref, yg_ref, gt_ref, sh_ref, g_ref, b_ref, o_ref):
    gt = gt_ref[...]
    parts = []
    for hw in range(2):
        lo_acc = hi_acc = None
        for k in range(TOP_K):
            lo, hi = _unpack_words(yg_ref[hw, k])
            gk = gt[:, k:k + 1]
            lo_acc = lo * gk if lo_acc is None else lo_acc + lo * gk
            hi_acc = hi * gk if hi_acc is None else hi_acc + hi * gk
        parts += [lo_acc, hi_acc]
    routed = jnp.concatenate(parts, axis=1)
    o_ref[...] = _ln_rows(ALPHA * x_ref[...] + (routed + sh_ref[...]), g_ref[...], b_ref[...])


def _combine_ln(x, yg, gate_t, shared, g, b):
    m, d = x.shape
    tm = min(m, 256)
    row = lambda i: (i, 0)
    fixed = lambda i: (0, 0)
    return pl.pallas_call(
        _combine_ln_body,
        grid=(m // tm,),
        in_specs=[pl.BlockSpec((tm, d), row), pl.BlockSpec((2, TOP_K, tm, PACK_W), lambda i: (0, 0, i, 0)),
                  pl.BlockSpec((tm, TOP_K), row), pl.BlockSpec((tm, d), row),
                  pl.BlockSpec((1, d), fixed), pl.BlockSpec((1, d), fixed)],
        out_specs=pl.BlockSpec((tm, d), row),
        out_shape=jax.ShapeDtypeStruct((m, d), jnp.float32),
        compiler_params=pltpu.CompilerParams(dimension_semantics=("arbitrary",)),
        name="combine_ln",
    )(x, yg, gate_t, shared, g.reshape(1, d), b.reshape(1, d))


def _moe_layer(x, mix, ln1_g, ln1_b, ln2_g, ln2_b, w_router, b_router, w_exp_gu, w_exp_down, w_sh_gu, w_sh_down):
    m, d = x.shape
    x1, xp, shared, eidx, gate8, rank8, counts = _moe_pre(x, mix, ln1_g, ln1_b, w_router, b_router,
                                                           w_sh_gu, w_sh_down)
    bm = 512 if m * TOP_K >= 512 * N_EXPERTS else MOE_BLK
    n_blk = (m * TOP_K) // bm + N_EXPERTS
    counts = counts[:, 0].astype(jnp.int32)
    padded = (counts + bm - 1) // bm * bm
    pad_end = jnp.cumsum(padded)
    pad_start = pad_end - padded
    start_of = jnp.sum(jnp.where(eidx[:, :, None] == jnp.arange(N_EXPERTS), pad_start, 0), axis=-1)
    dest = (start_of + rank8).reshape(-1)
    blk_start = jnp.arange(n_blk, dtype=jnp.int32) * bm
    blk_exp = jnp.minimum(jnp.sum(pad_end[None, :] <= blk_start[:, None], axis=1), N_EXPERTS - 1).astype(jnp.int32)
    blk_rows = jnp.clip(counts[blk_exp] - (blk_start - pad_start[blk_exp]), 0, bm).astype(jnp.int32)
    blk_first = jnp.concatenate([jnp.ones((1,), jnp.int32), (blk_exp[1:] != blk_exp[:-1]).astype(jnp.int32)])
    n_slots = n_blk * bm
    xs = _scatter_rows(xp.reshape(2 * m, PACK_W), jnp.concatenate([dest, dest + n_slots]), 2 * n_slots)
    y = _moe_experts(xs.reshape(2, n_slots, PACK_W), blk_exp, blk_first, blk_rows, w_exp_gu, w_exp_down, bm)
    yg = _gather_rows(y.reshape(2 * n_slots, PACK_W), jnp.concatenate([dest, dest + n_slots]))
    return _combine_ln(x1, yg.reshape(2, TOP_K, m, PACK_W), gate8.T, shared, ln2_g, ln2_b)


def _trunk(x, pos, gla_state, nsa_cache, page_table, win_buf, conv_buf,
           w_in_ab, w_gla_gate, b_gla_gate, gla_norm_g, w_cmp_pool, w_out_ab,
           w_pw1, b_pw1, w_dw, b_dw, conv_ln_g, conv_ln_b, w_pw2, b_pw2,
           ln_g, ln_b, w_router, b_router, w_exp_gu, w_exp_down, w_sh_gu, w_sh_down):
    new_gla, new_rows, new_win, new_conv = [], [], [], []
    for layer in range(DEPTH):
        i = layer // 2
        if layer % 2 == 0:
            mix, s_a, rows, win = _ab_mixer(
                x, pos, w_in_ab[i], w_gla_gate[i], b_gla_gate[i], gla_norm_g[i], w_cmp_pool[i], w_out_ab[i],
                None if gla_state is None else gla_state[i],
                None if nsa_cache is None else nsa_cache[i], page_table,
                None if win_buf is None else win_buf[i])
            new_gla.append(s_a)
            new_rows.append(rows)
            new_win.append(win)
        else:
            mix, cb = _conv_module(x, None if conv_buf is None else conv_buf[i], w_pw1[i], b_pw1[i],
                                   w_dw[i], b_dw[i], conv_ln_g[i], conv_ln_b[i], w_pw2[i], b_pw2[i])
            new_conv.append(cb)
        bsz, t_, d = x.shape
        x = _moe_layer(x.reshape(-1, d), mix.reshape(-1, d), ln_g[layer, 0], ln_b[layer, 0],
                       ln_g[layer, 1], ln_b[layer, 1], w_router[layer], b_router[layer],
                       w_exp_gu[layer], w_exp_down[layer], w_sh_gu[layer], w_sh_down[layer]).reshape(bsz, t_, d)
    return x, jnp.stack(new_gla), jnp.stack(new_rows), jnp.stack(new_win), jnp.stack(new_conv)


def kernel(x_prompt, x_sample, state_gla, cache_nsa_kv, state_nsa_win, state_conv, page_table,
           w_in_ab, w_gla_gate, b_gla_gate, gla_norm_g, w_cmp_pool, w_out_ab,
           w_pw1, b_pw1, w_dw, b_dw, conv_ln_g, conv_ln_b, w_pw2, b_pw2,
           ln_g, ln_b, w_router, b_router, w_exp_gu, w_exp_down, w_sh_gu, w_sh_down):
    weights = (w_in_ab, w_gla_gate, b_gla_gate, gla_norm_g, w_cmp_pool, w_out_ab,
               w_pw1, b_pw1, w_dw, b_dw, conv_ln_g, conv_ln_b, w_pw2, b_pw2,
               ln_g, ln_b, w_router, b_router, w_exp_gu, w_exp_down, w_sh_gu, w_sh_down)
    past_len = page_table.shape[1] * PAGE_SIZE
    pos_p = jnp.arange(x_prompt.shape[1])
    pos_s = past_len + jnp.arange(x_sample.shape[1])
    y_prompt, gla_p, rows_p, win_p, conv_p = _trunk(x_prompt, pos_p, None, None, None, None, None, *weights)
    y_sample, gla_s, rows_s, win_s, conv_s = _trunk(x_sample, pos_s, state_gla, cache_nsa_kv, page_table,
                                                    state_nsa_win, state_conv, *weights)
    return (y_prompt, y_sample, gla_p, gla_s, rows_p, rows_s, win_p, win_s, conv_p, conv_s)
```

```python
import functools
import math

import jax
import jax.numpy as jnp
import numpy as np
from jax import lax
from jax.experimental import pallas as pl
from jax.experimental.pallas import tpu as pltpu
from jax.experimental.pallas import tpu_sc as plsc

D_MODEL = 1024
DEPTH = 2
PAGE_SIZE = 128

GLA_HEADS = 4
GLA_DV = D_MODEL // 2 // GLA_HEADS
GLA_DK = GLA_DV // 2
GLA_RANK = 16
GLA_TAU = 16.0
GLA_CHUNK = 64

NSA_HEADS = 8
NSA_KV_HEADS = 2
NSA_GROUP = NSA_HEADS // NSA_KV_HEADS
HEAD_DIM = D_MODEL // 2 // NSA_HEADS
CMP_BLK = 32
CMP_STRIDE = 16
SEL_BLK = 64
SEL_TOPN = 16
WINDOW = 512
Q_BLK = 128
FORCE_BONUS = 100.0
ROPE_DIM = HEAD_DIM // 4
ROPE_THETA = 500000.0

GLA_SIZES = (GLA_HEADS * GLA_DK, GLA_HEADS * GLA_DK, GLA_HEADS * GLA_DV, GLA_HEADS * GLA_DV, GLA_RANK)
NSA_SIZES = (NSA_HEADS * HEAD_DIM, 6 * NSA_KV_HEADS * HEAD_DIM, 3 * NSA_HEADS)

CONV_W = 31
D_CONV = D_MODEL

N_EXPERTS = 64
N_GROUPS = 8
TOPK_GROUPS = 4
TOP_K = 8
D_EXPERT = 256
ROUTE_SCALE = 2.5
MOE_BLK = 128

ALPHA = (2 * DEPTH) ** 0.25
LN_EPS = 1e-5

LANE = 128


def _dot(a, b):
    return jnp.dot(a, b, preferred_element_type=jnp.float32)


def _dot_nt(a, b):
    return lax.dot_general(a, b, (((1,), (1,)), ((), ())), preferred_element_type=jnp.float32)


def _mm_body(x_ref, w_ref, o_ref):
    o_ref[...] = _dot(x_ref[...].astype(jnp.bfloat16), w_ref[...].astype(jnp.bfloat16))


def _mm(x, w, keep_pad=False):
    m, k = x.shape
    n = w.shape[1]
    n_pad = -(-n // LANE) * LANE
    w = w.astype(jnp.bfloat16)
    if n_pad != n:
        w = jnp.pad(w, ((0, 0), (0, n_pad - n)))
    tm = min(m, 512)
    out = pl.pallas_call(
        _mm_body,
        grid=(m // tm,),
        in_specs=[pl.BlockSpec((tm, k), lambda i: (i, 0)),
                  pl.BlockSpec((k, n_pad), lambda i: (0, 0))],
        out_specs=pl.BlockSpec((tm, n_pad), lambda i: (i, 0)),
        out_shape=jax.ShapeDtypeStruct((m, n_pad), jnp.float32),
        compiler_params=pltpu.CompilerParams(dimension_semantics=("arbitrary",),
                                             vmem_limit_bytes=48 * 1024 * 1024),
        name="mm",
    )(x, w)
    return out if keep_pad or n_pad == n else out[:, :n]


def _mm3(x, w):
    b, t, d = x.shape
    return _mm(x.reshape(b * t, d), w).reshape(b, t, -1)


def _split_cols(h, sizes):
    return jnp.split(h, np.cumsum(sizes)[:-1].tolist(), axis=-1)


def _layer_norm(x, g, b):
    mu = x.mean(-1, keepdims=True)
    var = jnp.square(x - mu).mean(-1, keepdims=True)
    return (x - mu) * lax.rsqrt(var + LN_EPS) * g + b


def _rms_norm(x, g):
    return x * lax.rsqrt(jnp.mean(x * x, -1, keepdims=True) + LN_EPS) * g


def _partial_rope(x, pos):
    half = ROPE_DIM // 2
    inv_freq = jnp.power(ROPE_THETA, -jnp.arange(half, dtype=jnp.float32) / half)
    ang = pos.astype(jnp.float32)[:, None] * inv_freq
    ang = ang.reshape(ang.shape[0], *([1] * (x.ndim - 3)), half)
    cos, sin = jnp.cos(ang), jnp.sin(ang)
    x1 = x[..., :half]
    x2 = x[..., half:ROPE_DIM]
    rot = jnp.concatenate([x1 * cos - x2 * sin, x2 * cos + x1 * sin], -1)
    return jnp.concatenate([rot, x[..., ROPE_DIM:]], -1)


def _masked_softmax(s, mask):
    s = jnp.where(mask, s, -jnp.inf)
    m = jnp.max(s, axis=-1, keepdims=True)
    m = jnp.where(jnp.isfinite(m), m, 0.0)
    p = jnp.exp(s - m)
    return p / jnp.maximum(p.sum(-1, keepdims=True), 1e-30)


def _gla_recurrence(q, k, v, log_a, s0):
    bsz, t_, nh, _ = q.shape
    c = math.gcd(t_, GLA_CHUNK)
    n = t_ // c

    def chunks(a):
        return jnp.moveaxis(a.reshape(bsz, n, c, *a.shape[2:]), 1, 0)

    causal = jnp.tril(jnp.ones((c, c), dtype=bool))[None, :, :, None, None]

    def step(s, inp):
        qc, kc, vc, lc = inp
        bc = jnp.cumsum(lc, axis=1)
        decay = jnp.exp(jnp.where(causal, bc[:, :, None] - bc[:, None, :], -jnp.inf))
        attn = jnp.einsum('bijhd,bjhd->bhij', qc[:, :, None] * decay, kc)
        o = jnp.einsum('bhij,bjhe->bihe', attn, vc) + jnp.einsum('bihd,bhde->bihe', qc * jnp.exp(bc), s)
        bl = bc[:, -1]
        s = jnp.exp(bl)[..., None] * s + jnp.einsum('bjhd,bjhe->bhde', kc * jnp.exp(bl[:, None] - bc), vc)
        return s, o

    s_fin, o = lax.scan(step, s0, (chunks(q), chunks(k), chunks(v), chunks(log_a)))
    return jnp.moveaxis(o, 0, 1).reshape(bsz, t_, nh, -1), s_fin


def _compress(k, v, w_pool):
    bsz, length = k.shape[:2]
    n_sub = length // CMP_STRIDE

    def pool(a, w):
        sub = a[:, :n_sub * CMP_STRIDE].reshape(bsz, n_sub, CMP_STRIDE, *a.shape[2:])
        first = jnp.einsum('bnjhd,j->bnhd', sub, w[:CMP_STRIDE])
        second = jnp.einsum('bnjhd,j->bnhd', sub, w[CMP_STRIDE:])
        return first[:, :-1] + second[:, 1:]

    cend = jnp.arange(n_sub - 1) * CMP_STRIDE + CMP_BLK - 1
    return pool(k, w_pool[0]), pool(v, w_pool[1]), cend


def _to_sel_blocks(a, n_sel):
    bsz, length = a.shape[:2]
    a = jnp.pad(a, ((0, 0), (0, n_sel * SEL_BLK - length), (0, 0), (0, 0)))
    return a.reshape(bsz, n_sel, SEL_BLK, NSA_KV_HEADS, HEAD_DIM).transpose(0, 3, 1, 2, 4)


def _nsa_attend(q_raw, q_rot, qpos, gates, kc, vc, cend, ksb, vsb, kw, vw, kwpos):
    scale = HEAD_DIM ** -0.5
    bsz, tq = q_raw.shape[:2]
    n_cmp, n_sel = kc.shape[1], ksb.shape[2]
    s_c = jnp.einsum('bqhgd,bnhd->bhgqn', q_raw, kc) * scale
    p_c = _masked_softmax(s_c, cend[None, :] <= qpos[:, None])
    o_c = jnp.einsum('bhgqn,bnhd->bqhgd', p_c, vc)
    ratio = SEL_BLK // CMP_STRIDE
    imp = p_c.sum(axis=2)
    imp = jnp.pad(imp, ((0, 0), (0, 0), (0, 0), (1, ratio * (n_sel + 1) - 1 - n_cmp)))
    imp = imp.reshape(bsz, NSA_KV_HEADS, tq, n_sel + 1, ratio)
    imp_s = imp[..., :n_sel, :].sum(-1) + imp[..., 1:, 0]
    blk = jnp.arange(n_sel)[None, :]
    cur = (qpos // SEL_BLK)[:, None]
    valid = blk * SEL_BLK <= qpos[:, None]
    forced = (blk == 0) | (blk == cur) | (blk == cur - 1)
    score = jnp.where(valid, imp_s + jnp.where(forced, FORCE_BONUS, 0.0), -jnp.inf)
    k_top = min(SEL_TOPN, n_sel)
    _, sel = lax.top_k(score, k_top)
    take = jax.vmap(jax.vmap(lambda blocks, idx: blocks[idx]))
    ks = take(ksb, sel).reshape(bsz, NSA_KV_HEADS, tq, k_top * SEL_BLK, HEAD_DIM)
    vs = take(vsb, sel).reshape(bsz, NSA_KV_HEADS, tq, k_top * SEL_BLK, HEAD_DIM)
    kpos = (sel[..., None] * SEL_BLK + jnp.arange(SEL_BLK)).reshape(bsz, NSA_KV_HEADS, tq, k_top * SEL_BLK)
    s_s = jnp.einsum('bqhgd,bhqkd->bhgqk', q_rot, ks) * scale
    p_s = _masked_softmax(s_s, (kpos <= qpos[:, None])[:, :, None])
    o_s = jnp.einsum('bhgqk,bhqkd->bqhgd', p_s, vs)
    s_w = jnp.einsum('bqhgd,bkhd->bhgqk', q_rot, kw) * scale
    kp, qp = kwpos[None, :], qpos[:, None]
    p_w = _masked_softmax(s_w, (kp <= qp) & (kp > qp - WINDOW) & (kp >= 0))
    o_w = jnp.einsum('bhgqk,bkhd->bqhgd', p_w, vw)
    return gates[..., 0:1] * o_c + gates[..., 1:2] * o_s + gates[..., 2:3] * o_w


NSA_ROWS = NSA_GROUP * Q_BLK
SEL_KT = 1024
N_SELB = 128
MASKED = -1e9
WIN_KEYS = WINDOW + Q_BLK
KK_W = 2 * HEAD_DIM + N_SELB


def _nsa_prompt_body(qr_ref, qo_ref, kc_ref, vct_ref, kk_ref, vvt_ref, g_ref, o_ref,
                     imp_ref, m_ref, l_ref, acc_ref):
    f32, bf16 = jnp.float32, jnp.bfloat16
    qb = pl.program_id(2)
    q0 = qb * Q_BLK
    qr_t = qr_ref[0, 0, 0]
    qo_t = qo_ref[0, 0, 0]
    n_cmp = kc_ref.shape[2]

    s_c = _dot(kc_ref[0, 0], qr_t)
    n_idx = lax.broadcasted_iota(jnp.int32, (n_cmp, NSA_ROWS), 0)
    qpos_c = q0 + (lax.broadcasted_iota(jnp.int32, (n_cmp, NSA_ROWS), 1) & (Q_BLK - 1))
    cmask = (n_idx * CMP_STRIDE + (CMP_BLK - 1)) <= qpos_c
    s_c = jnp.where(cmask, s_c, MASKED)
    m_c = jnp.max(s_c, axis=0, keepdims=True)
    p_c = jnp.where(cmask, jnp.exp(s_c - m_c), 0.0)
    p_c = p_c / jnp.maximum(jnp.sum(p_c, axis=0, keepdims=True), 1e-30)
    o_ct = _dot(vct_ref[0, 0], p_c.astype(bf16))

    imp = (p_c[:, 0:Q_BLK] + p_c[:, Q_BLK:2 * Q_BLK]) + p_c[:, 2 * Q_BLK:3 * Q_BLK] + p_c[:, 3 * Q_BLK:]
    imp_ref[0:8, :] = jnp.zeros((8, Q_BLK), f32)
    imp_ref[8:8 + n_cmp, :] = imp
    ratio = SEL_BLK // CMP_STRIDE
    n_selb = n_cmp // ratio
    imp_s = imp_ref[pl.ds(7, n_selb, stride=ratio), :]
    for r in range(ratio):
        imp_s = imp_s + imp_ref[pl.ds(8 + r, n_selb, stride=ratio), :]
    blk = lax.broadcasted_iota(jnp.int32, (n_selb, Q_BLK), 0)
    qpos_s = q0 + lax.broadcasted_iota(jnp.int32, (n_selb, Q_BLK), 1)
    cur = lax.shift_right_logical(qpos_s, int(math.log2(SEL_BLK)))
    valid = blk * SEL_BLK <= qpos_s
    forced = (blk == 0) | (blk == cur) | (blk == cur - 1)
    score = jnp.where(valid, imp_s + jnp.where(forced, FORCE_BONUS, 0.0), -1e30)
    picked = jnp.zeros((n_selb, Q_BLK), f32)
    for _ in range(SEL_TOPN):
        best = jnp.max(score, axis=0, keepdims=True)
        first = jnp.min(jnp.where(score == best, blk, n_selb), axis=0, keepdims=True)
        hit = blk == first
        picked = jnp.where(hit, 1.0, picked)
        score = jnp.where(hit, -3e38, score)
    selb_t = jnp.where(valid, picked, 0.0)
    if n_selb < N_SELB:
        selb_t = jnp.concatenate([selb_t, jnp.zeros((N_SELB - n_selb, Q_BLK), f32)], axis=0)
    selb_t = ((selb_t - 1.0) * (-MASKED)).astype(bf16)
    selb_t = jnp.concatenate([selb_t] * NSA_GROUP, axis=1)

    zeros_q = jnp.zeros((HEAD_DIM, NSA_ROWS), bf16)
    q_sel = jnp.concatenate([qo_t, zeros_q, selb_t], axis=0)
    q_win = jnp.concatenate([zeros_q, qo_t, jnp.zeros((N_SELB, NSA_ROWS), bf16)], axis=0)
    qpos_r = q0 + (lax.broadcasted_iota(jnp.int32, (1, NSA_ROWS), 1) & (Q_BLK - 1))

    def v_tiles(first, count):
        return jnp.concatenate([vvt_ref[0, 0, first + j] for j in range(count)], axis=1)

    m_ref[...] = jnp.full(m_ref.shape, MASKED, f32)
    l_ref[...] = jnp.zeros(l_ref.shape, f32)
    acc_ref[...] = jnp.zeros(acc_ref.shape, f32)

    def sel_tile(k0, kt, causal):
        s = _dot(kk_ref[0, 0, pl.ds(k0, kt), :], q_sel)
        if causal:
            kpos = k0 + lax.broadcasted_iota(jnp.int32, (kt, NSA_ROWS), 0)
            s = jnp.where(kpos <= qpos_r, s, MASKED)
        m_old = m_ref[...]
        m_new = jnp.maximum(m_old, jnp.max(s, axis=0, keepdims=True))
        alpha = jnp.exp(m_old - m_new)
        p = jnp.exp(s - m_new)
        l_ref[...] = alpha * l_ref[...] + jnp.sum(p, axis=0, keepdims=True)
        vt = v_tiles(k0 // Q_BLK, kt // Q_BLK)
        acc_ref[...] = alpha * acc_ref[...] + _dot(vt, p.astype(bf16))
        m_ref[...] = m_new

    n_full = q0 // SEL_KT

    def full_step(t, c):
        sel_tile(pl.multiple_of(t * SEL_KT, SEL_KT), SEL_KT, False)
        return c

    lax.fori_loop(0, n_full, full_step, 0)
    sel_tile(pl.multiple_of(n_full * SEL_KT, SEL_KT), SEL_KT, True)
    o_st = acc_ref[0:HEAD_DIM, :] / l_ref[...]

    w0 = pl.multiple_of(jnp.maximum(q0 - WINDOW, 0), Q_BLK)
    s_w = _dot(kk_ref[0, 0, pl.ds(w0, WIN_KEYS), :], q_win)
    kpos_w = w0 + lax.broadcasted_iota(jnp.int32, (WIN_KEYS, NSA_ROWS), 0)
    s_w = jnp.where((kpos_w <= qpos_r) & (kpos_w > qpos_r - WINDOW), s_w, MASKED)
    p_w = jnp.exp(s_w - jnp.max(s_w, axis=0, keepdims=True))
    l_w = jnp.sum(p_w, axis=0, keepdims=True)
    acc_w = _dot(v_tiles(w0 // Q_BLK, WIN_KEYS // Q_BLK), p_w.astype(bf16))
    o_wt = acc_w[HEAD_DIM:2 * HEAD_DIM, :] / l_w

    g = g_ref[0, 0, 0]
    out_t = g[0:1, :] * o_ct + g[1:2, :] * o_st + g[2:3, :] * o_wt
    o_ref[0] = jnp.concatenate([out_t[:, g_ * Q_BLK:(g_ + 1) * Q_BLK] for g_ in range(NSA_GROUP)], axis=0).T


def _nsa_prompt(qr, qo, gt, kc_p, vct, kk, vvt):
    bsz, _, nqb = qr.shape[:3]
    t_ = nqb * Q_BLK
    n_cmp = kc_p.shape[2]
    per_blk = lambda b, h, i: (b, h, i, 0, 0)
    per_head = lambda b, h, i: (b, h, 0, 0)
    return pl.pallas_call(
        _nsa_prompt_body,
        grid=(bsz, NSA_KV_HEADS, nqb),
        in_specs=[pl.BlockSpec((1, 1, 1, HEAD_DIM, NSA_ROWS), per_blk),
                  pl.BlockSpec((1, 1, 1, HEAD_DIM, NSA_ROWS), per_blk),
                  pl.BlockSpec((1, 1, n_cmp, HEAD_DIM), per_head),
                  pl.BlockSpec((1, 1, HEAD_DIM, n_cmp), per_head),
                  pl.BlockSpec((1, 1, t_, KK_W), per_head),
                  pl.BlockSpec((1, 1, nqb, 2 * HEAD_DIM, Q_BLK), lambda b, h, i: (b, h, 0, 0, 0)),
                  pl.BlockSpec((1, 1, 1, 3, NSA_ROWS), per_blk)],
        out_specs=pl.BlockSpec((1, Q_BLK, NSA_GROUP * HEAD_DIM), lambda b, h, i: (b, i, h)),
        out_shape=jax.ShapeDtypeStruct((bsz, t_, NSA_HEADS * HEAD_DIM), jnp.float32),
        scratch_shapes=[pltpu.VMEM((8 + n_cmp, Q_BLK), jnp.float32),
                        pltpu.VMEM((1, NSA_ROWS), jnp.float32),
                        pltpu.VMEM((1, NSA_ROWS), jnp.float32),
                        pltpu.VMEM((2 * HEAD_DIM, NSA_ROWS), jnp.float32)],
        compiler_params=pltpu.CompilerParams(
            dimension_semantics=("arbitrary", "arbitrary", "arbitrary"),
            vmem_limit_bytes=48 * 1024 * 1024),
        name="nsa_prompt",
    )(qr, qo, kc_p, vct, kk, vvt, gt)


GLA_SUB = 16
GLA_QK = GLA_HEADS * GLA_DK
GLA_V = GLA_HEADS * GLA_DV


def _dot_tn(a, b):
    return lax.dot_general(a, b, (((0,), (0,)), ((), ())), preferred_element_type=jnp.float32)


def _gla_body(q_ref, k_ref, v_ref, gr_ref, glr_ref, wg_ref, bg_ref, ng_ref, s0_ref, exp_ref, bd_ref,
              o_ref, sfin_ref, st_ref, b_ref, qd_ref, *, t_valid):
    f32, bf16 = jnp.float32, jnp.bfloat16
    tt = q_ref.shape[1]
    ti = pl.program_id(1)

    @pl.when(ti == 0)
    def _():
        st_ref[...] = s0_ref[0]

    row = lax.broadcasted_iota(jnp.int32, (tt, 1), 0)
    z = _dot(glr_ref[0][:, :GLA_RANK].astype(bf16), wg_ref[...]) + bg_ref[...]
    la = (jnp.minimum(z, 0.0) - jnp.log1p(jnp.exp(-jnp.abs(z)))) * (1.0 / GLA_TAU)
    la = jnp.where(ti * tt + row < t_valid, la, 0.0)
    seg = row & (GLA_SUB - 1)
    b = la
    for s in (1, 2, 4, 8):
        b = b + jnp.where(seg >= s, pltpu.roll(b, s, axis=0), 0.0)
    q = q_ref[0] * (GLA_DK ** -0.5)
    k = k_ref[0]
    v = v_ref[0]
    o = _dot((q * k).astype(bf16), exp_ref[...]) * v
    for d in range(1, GLA_SUB):
        decay = jnp.exp(jnp.minimum(b - pltpu.roll(b, d, axis=0), 0.0))
        w = jnp.where(seg >= d, q * pltpu.roll(k, d, axis=0) * decay, 0.0)
        o = o + _dot(w.astype(bf16), exp_ref[...]) * pltpu.roll(v, d, axis=0)
    o_ref[0] = o
    b_ref[...] = b
    qd_ref[...] = (q * jnp.exp(b)).astype(bf16)

    def block_step(c, carry):
        rows = pl.ds(pl.multiple_of(c * GLA_SUB, GLA_SUB), GLA_SUB)
        st = st_ref[...]
        o_ref[0, rows, :] += _dot_nt(qd_ref[rows, :], st.astype(bf16))
        bc = b_ref[rows, :]
        bl = bc[GLA_SUB - 1:GLA_SUB, :]
        kc = (k_ref[0, rows, :] * jnp.exp(bl - bc)).astype(bf16)
        upd = _dot_tn(v_ref[0, rows, :].astype(bf16), kc)
        st_ref[...] = st * jnp.exp(bl) + upd * bd_ref[...]
        return carry

    lax.fori_loop(0, tt // GLA_SUB, block_step, 0)
    sfin_ref[0] = st_ref[...]
    gr = gr_ref[0]
    gate = gr * jax.nn.sigmoid(gr)
    for h in range(GLA_HEADS):
        cols = slice(h * GLA_DV, (h + 1) * GLA_DV)
        oh = o_ref[0, :, cols]
        ms = jnp.mean(oh * oh, axis=-1, keepdims=True)
        o_ref[0, :, cols] = oh * lax.rsqrt(ms + LN_EPS) * ng_ref[...] * gate[:, cols]


def _gla(h, w_gla_gate, b_gla_gate, gla_norm_g, gla_state):
    bsz, t_, n_in = h.shape
    tp = -(-t_ // GLA_SUB) * GLA_SUB
    if tp != t_:
        h = jnp.pad(h, ((0, 0), (0, tp - t_), (0, 0)))
    tt = min(tp, 256)
    heads = np.arange(GLA_HEADS)
    expand = np.repeat(np.repeat(np.eye(GLA_HEADS, dtype=np.float32), GLA_DK, 0), GLA_DV, 1)
    bdmask = jnp.asarray(expand.T)
    if gla_state is None:
        s0 = jnp.zeros((bsz, GLA_V, GLA_QK), jnp.float32)
    else:
        s0 = jnp.zeros((bsz, GLA_HEADS, GLA_DV, GLA_HEADS, GLA_DK), jnp.float32)
        s0 = s0.at[:, heads, :, heads, :].set(gla_state.transpose(1, 0, 3, 2)).reshape(bsz, GLA_V, GLA_QK)
    tile = lambda width, blk: pl.BlockSpec((1, tt, width), lambda b, i: (b, i, blk))
    fixed2 = lambda shape: pl.BlockSpec(shape, lambda b, i: (0, 0))
    per_b = pl.BlockSpec((1, GLA_V, GLA_QK), lambda b, i: (b, 0, 0))
    o, s_t = pl.pallas_call(
        functools.partial(_gla_body, t_valid=t_),
        grid=(bsz, tp // tt),
        in_specs=[tile(GLA_QK, 0), tile(GLA_QK, 1), tile(GLA_V, 1), tile(GLA_V, 2),
                  tile(LANE, (2 * GLA_QK + 2 * GLA_V + NSA_SIZES[0] + NSA_SIZES[1]) // LANE),
                  fixed2((GLA_RANK, GLA_QK)), fixed2((1, GLA_QK)), fixed2((1, GLA_DV)), per_b,
                  fixed2((GLA_QK, GLA_V)), fixed2((GLA_V, GLA_QK))],
        out_specs=[pl.BlockSpec((1, tt, GLA_V), lambda b, i: (b, i, 0)), per_b],
        out_shape=[jax.ShapeDtypeStruct((bsz, tp, GLA_V), jnp.float32),
                   jax.ShapeDtypeStruct((bsz, GLA_V, GLA_QK), jnp.float32)],
        scratch_shapes=[pltpu.VMEM((GLA_V, GLA_QK), jnp.float32), pltpu.VMEM((tt, GLA_QK), jnp.float32),
                        pltpu.VMEM((tt, GLA_QK), jnp.bfloat16)],
        compiler_params=pltpu.CompilerParams(dimension_semantics=("arbitrary", "arbitrary"),
                                             vmem_limit_bytes=48 * 1024 * 1024),
        name="gla",
    )(h, h, h, h, h, w_gla_gate.astype(jnp.bfloat16), b_gla_gate.reshape(1, GLA_QK),
      gla_norm_g.reshape(1, GLA_DV), s0, jnp.asarray(expand, jnp.bfloat16), bdmask)
    s_new = s_t.reshape(bsz, GLA_HEADS, GLA_DV, GLA_HEADS, GLA_DK)[:, heads, :, heads, :]
    return o[:, :t_], s_new.transpose(1, 0, 3, 2)


COL_NQ = 2 * GLA_QK + 2 * GLA_V
COL_NKV = COL_NQ + NSA_SIZES[0]
COL_TAIL = COL_NKV + NSA_SIZES[1]
TAIL_GATE = GLA_RANK
_ORIG = np.cumsum((0,) + GLA_SIZES + NSA_SIZES)
IN_AB_PERM = np.concatenate([np.arange(_ORIG[0], _ORIG[4]), np.arange(_ORIG[5], _ORIG[7]),
                             np.arange(_ORIG[4], _ORIG[5]), np.arange(_ORIG[7], _ORIG[8])])
SUBS = Q_BLK // CMP_STRIDE


def _nsa_prep_body(nq_ref, kv0_ref, kv1_ref, kv2_ref, tail_ref, rc_ref, ru_ref, rd_ref, pool_ref,
                   rows_ref, win_ref, kk_ref, vvt_ref, qr_ref, qo_ref, g_ref, pooled_ref):
    bf16 = jnp.bfloat16
    q0 = pl.program_id(1) * Q_BLK
    kv_w = NSA_KV_HEADS * HEAD_DIM

    def rope(x):
        reps = x.shape[1] // LANE
        wide = lambda r: jnp.concatenate([r[...]] * reps, axis=1) if reps > 1 else r[...]
        half = ROPE_DIM // 2
        return (x * wide(rc_ref) + pltpu.roll(x, half, axis=1) * wide(ru_ref)
                + pltpu.roll(x, x.shape[1] - half, axis=1) * wide(rd_ref))

    kv0, kv1, kv2 = kv0_ref[0], kv1_ref[0], kv2_ref[0]
    k_sel, v_sel = rope(kv1[:, :kv_w]), kv1[:, kv_w:]
    k_win, v_win = rope(kv2[:, :kv_w]), kv2[:, kv_w:]
    rows_ref[0] = jnp.concatenate([kv0, k_sel, v_sel], axis=1)
    win_ref[0] = jnp.concatenate([k_win, v_win], axis=1)
    blk_id = lax.shift_right_logical(q0 + lax.broadcasted_iota(jnp.int32, (Q_BLK, N_SELB), 0),
                                     int(math.log2(SEL_BLK)))
    onehot = jnp.where(lax.broadcasted_iota(jnp.int32, (Q_BLK, N_SELB), 1) == blk_id, 1.0, 0.0).astype(bf16)
    q = nq_ref[0] * (HEAD_DIM ** -0.5)
    q_rot = rope(q)
    gates_t = jax.nn.sigmoid(tail_ref[0]).T
    for h in range(NSA_KV_HEADS):
        hs = slice(h * HEAD_DIM, (h + 1) * HEAD_DIM)
        kk_ref[0, h] = jnp.concatenate([k_sel[:, hs].astype(bf16), k_win[:, hs].astype(bf16), onehot], axis=1)
        vvt_ref[0, h, 0] = jnp.concatenate([v_sel[:, hs], v_win[:, hs]], axis=1).T.astype(bf16)
        gw = NSA_GROUP * HEAD_DIM
        for src, dst in ((q, qr_ref), (q_rot, qo_ref)):
            t = src[:, h * gw:(h + 1) * gw].T
            dst[0, h, 0] = jnp.concatenate([t[g * HEAD_DIM:(g + 1) * HEAD_DIM] for g in range(NSA_GROUP)],
                                           axis=1).astype(bf16)
        base = TAIL_GATE + h * NSA_GROUP * 3
        g_ref[0, h, 0] = jnp.concatenate(
            [jnp.concatenate([gates_t[base + 3 * g + c:base + 3 * g + c + 1] for g in range(NSA_GROUP)], axis=1)
             for c in range(3)], axis=0)
    kc_in, vc_in = kv0[:, :kv_w].astype(bf16), kv0[:, kv_w:].astype(bf16)
    pooled_ref[0] = jnp.concatenate([_dot(pool_ref[0], kc_in), _dot(pool_ref[1], kc_in),
                                     _dot(pool_ref[2], vc_in), _dot(pool_ref[3], vc_in)], axis=1)


def _nsa_prep(h, pos, w_cmp_pool):
    bsz, t_, _ = h.shape
    nqb = t_ // Q_BLK
    bf16 = jnp.bfloat16
    half = ROPE_DIM // 2
    inv_freq = jnp.power(ROPE_THETA, -jnp.arange(half, dtype=jnp.float32) / half)
    ang = pos.astype(jnp.float32)[:, None] * inv_freq
    cos, sin = jnp.cos(ang), jnp.sin(ang)
    rest = HEAD_DIM - ROPE_DIM
    z8, zr = jnp.zeros((t_, half), jnp.float32), jnp.zeros((t_, rest), jnp.float32)
    two = lambda a: jnp.concatenate([a, a], axis=1)
    rc = two(jnp.concatenate([cos, cos, jnp.ones((t_, rest), jnp.float32)], axis=1))
    ru = two(jnp.concatenate([z8, sin, zr], axis=1))
    rd = two(jnp.concatenate([-sin, z8, zr], axis=1))
    pool = _pool_matrices(w_cmp_pool)
    kv_w = NSA_KV_HEADS * HEAD_DIM
    col = lambda width, off: pl.BlockSpec((1, Q_BLK, width), lambda b, i: (b, i, off // width))
    rows_t = pl.BlockSpec((Q_BLK, LANE), lambda b, i: (i, 0))
    head4 = lambda r, c: pl.BlockSpec((1, NSA_KV_HEADS, 1, r, c), lambda b, i: (b, 0, i, 0, 0))
    return pl.pallas_call(
        _nsa_prep_body,
        grid=(bsz, nqb),
        in_specs=[col(NSA_SIZES[0], COL_NQ), col(2 * kv_w, COL_NKV), col(2 * kv_w, COL_NKV + 2 * kv_w),
                  col(2 * kv_w, COL_NKV + 4 * kv_w), col(LANE, COL_TAIL), rows_t, rows_t, rows_t,
                  pl.BlockSpec((4, SUBS, Q_BLK), lambda b, i: (0, 0, 0))],
        out_specs=[pl.BlockSpec((1, Q_BLK, 4 * kv_w), lambda b, i: (b, i, 0)),
                   pl.BlockSpec((1, Q_BLK, 2 * kv_w), lambda b, i: (b, i, 0)),
                   pl.BlockSpec((1, NSA_KV_HEADS, Q_BLK, KK_W), lambda b, i: (b, 0, i, 0)),
                   head4(2 * HEAD_DIM, Q_BLK), head4(HEAD_DIM, NSA_ROWS), head4(HEAD_DIM, NSA_ROWS),
                   head4(3, NSA_ROWS),
                   pl.BlockSpec((1, SUBS, 4 * kv_w), lambda b, i: (b, i, 0))],
        out_shape=[jax.ShapeDtypeStruct((bsz, t_, 4 * kv_w), jnp.float32),
                   jax.ShapeDtypeStruct((bsz, t_, 2 * kv_w), jnp.float32),
                   jax.ShapeDtypeStruct((bsz, NSA_KV_HEADS, t_, KK_W), bf16),
                   jax.ShapeDtypeStruct((bsz, NSA_KV_HEADS, nqb, 2 * HEAD_DIM, Q_BLK), bf16),
                   jax.ShapeDtypeStruct((bsz, NSA_KV_HEADS, nqb, HEAD_DIM, NSA_ROWS), bf16),
                   jax.ShapeDtypeStruct((bsz, NSA_KV_HEADS, nqb, HEAD_DIM, NSA_ROWS), bf16),
                   jax.ShapeDtypeStruct((bsz, NSA_KV_HEADS, nqb, 3, NSA_ROWS), jnp.float32),
                   jax.ShapeDtypeStruct((bsz, t_ // CMP_STRIDE, 4 * kv_w), jnp.float32)],
        compiler_params=pltpu.CompilerParams(dimension_semantics=("arbitrary", "arbitrary")),
        name="nsa_prep",
    )(h, h, h, h, h, rc, ru, rd, pool)


PAGE_GROUP = 8
DEC_KEYS = PAGE_GROUP * PAGE_SIZE
NEW_PAD = 8
KV_W = NSA_KV_HEADS * HEAD_DIM


def _dec_pool_body(pt_ref, *refs):
    page_refs, pool_ref, out_ref = refs[:PAGE_GROUP], refs[PAGE_GROUP], refs[PAGE_GROUP + 1]
    bf16 = jnp.bfloat16
    parts = []
    for pr in page_refs:
        kv0 = pr[0]
        kc_t, vc_t = kv0[:KV_W].astype(bf16), kv0[KV_W:].astype(bf16)
        parts.append(jnp.concatenate([_dot_nt(pool_ref[0], kc_t), _dot_nt(pool_ref[1], kc_t),
                                      _dot_nt(pool_ref[2], vc_t), _dot_nt(pool_ref[3], vc_t)], axis=1))
    out_ref[0] = jnp.concatenate(parts, axis=0)


def _page_specs(n_pages, col_blk):
    def spec(i):
        return pl.BlockSpec((1, 2 * KV_W, PAGE_SIZE),
                            lambda b, j, pt: (pt[b * n_pages + j * PAGE_GROUP + i], col_blk, 0))
    return [spec(i) for i in range(PAGE_GROUP)]


def _dec_pool(cache, page_table, pool):
    bsz, n_pages = page_table.shape
    grid_spec = pltpu.PrefetchScalarGridSpec(
        num_scalar_prefetch=1, grid=(bsz, n_pages // PAGE_GROUP),
        in_specs=_page_specs(n_pages, 0) + [pl.BlockSpec((4, SUBS, Q_BLK), lambda b, j, pt: (0, 0, 0))],
        out_specs=pl.BlockSpec((1, PAGE_GROUP * SUBS, 4 * KV_W), lambda b, j, pt: (b, j, 0)))
    return pl.pallas_call(
        _dec_pool_body, grid_spec=grid_spec,
        out_shape=jax.ShapeDtypeStruct((bsz, n_pages * SUBS, 4 * KV_W), jnp.float32),
        compiler_params=pltpu.CompilerParams(dimension_semantics=("arbitrary", "arbitrary")),
        name="nsa_dec_pool",
    )(page_table.reshape(-1), *([cache] * PAGE_GROUP), pool)


def _dec_select_body(qr_ref, kct_ref, vc_ref, band_ref, oc_ref, selb_ref, *, qpos0, n_q, n_pick, n_blk):
    f32, bf16 = jnp.float32, jnp.bfloat16
    n_cmp = kct_ref.shape[3]
    rows = NSA_GROUP * n_q
    for h in range(NSA_KV_HEADS):
        s_c = _dot(qr_ref[0, h], kct_ref[0, h])
        n_idx = lax.broadcasted_iota(jnp.int32, (rows, n_cmp), 1)
        qpos = qpos0 + (lax.broadcasted_iota(jnp.int32, (rows, n_cmp), 0) % n_q)
        cmask = (n_idx * CMP_STRIDE + (CMP_BLK - 1)) <= qpos
        s_c = jnp.where(cmask, s_c, MASKED)
        p_c = jnp.where(cmask, jnp.exp(s_c - jnp.max(s_c, axis=1, keepdims=True)), 0.0)
        p_c = p_c / jnp.maximum(jnp.sum(p_c, axis=1, keepdims=True), 1e-30)
        oc_ref[0, h] = _dot(p_c.astype(bf16), vc_ref[0, h])
        imp = p_c[0:n_q]
        for g in range(1, NSA_GROUP):
            imp = imp + p_c[g * n_q:(g + 1) * n_q]
        imp_s = jnp.zeros((n_q, N_SELB), f32)
        rem = imp
        for _ in range(3):
            part = rem.astype(bf16)
            imp_s = imp_s + _dot(part, band_ref[...])
            rem = rem - part.astype(f32)
        blk = lax.broadcasted_iota(jnp.int32, (n_q, N_SELB), 1)
        qpos_s = qpos0 + lax.broadcasted_iota(jnp.int32, (n_q, N_SELB), 0)
        cur = lax.shift_right_logical(qpos_s, int(math.log2(SEL_BLK)))
        valid = (blk * SEL_BLK <= qpos_s) & (blk < n_blk)
        forced = (blk == 0) | (blk == cur) | (blk == cur - 1)
        score = jnp.where(valid, imp_s + jnp.where(forced, FORCE_BONUS, 0.0), -1e30)
        picked = jnp.zeros((n_q, N_SELB), f32)
        for _ in range(n_pick):
            best = jnp.max(score, axis=1, keepdims=True)
            first = jnp.min(jnp.where(score == best, blk, N_SELB), axis=1, keepdims=True)
            hit = blk == first
            picked = jnp.where(hit, 1.0, picked)
            score = jnp.where(hit, -3e38, score)
        selb_ref[0, h] = (jnp.where(valid, picked, 0.0) - 1.0) * (-MASKED)


def _dec_select(qr, kct, vc, n_q, qpos0, n_pick, n_blk):
    bsz = qr.shape[0]
    rows = NSA_GROUP * n_q
    n_cmp = kct.shape[3]
    ratio = SEL_BLK // CMP_STRIDE
    c_idx, j_idx = np.arange(n_cmp)[:, None], np.arange(N_SELB)[None, :]
    band = jnp.asarray(((c_idx >= ratio * j_idx - 1) & (c_idx <= ratio * j_idx + ratio - 1)), jnp.bfloat16)
    per_b = lambda *tail: pl.BlockSpec((1, NSA_KV_HEADS) + tail, lambda b: (b, 0, 0, 0))
    return pl.pallas_call(
        functools.partial(_dec_select_body, qpos0=qpos0, n_q=n_q, n_pick=n_pick, n_blk=n_blk),
        grid=(bsz,),
        in_specs=[per_b(rows, HEAD_DIM), per_b(HEAD_DIM, n_cmp), per_b(n_cmp, HEAD_DIM),
                  pl.BlockSpec((n_cmp, N_SELB), lambda b: (0, 0))],
        out_specs=[per_b(rows, HEAD_DIM), per_b(n_q, N_SELB)],
        out_shape=[jax.ShapeDtypeStruct((bsz, NSA_KV_HEADS, rows, HEAD_DIM), jnp.float32),
                   jax.ShapeDtypeStruct((bsz, NSA_KV_HEADS, n_q, N_SELB), jnp.float32)],
        compiler_params=pltpu.CompilerParams(dimension_semantics=("arbitrary",)),
        name="nsa_dec_select",
    )(qr, kct, vc, band)


def _dec_attend_body(pt_ref, *refs, qpos0, n_q, past):
    page_refs = refs[:PAGE_GROUP]
    (qs_ref, qw_ref, knew_ref, vnew_ref, wbuf_ref, wnew_ref, oc_ref, g_ref,
     o_ref, m_ref, l_ref, acc_ref) = refs[PAGE_GROUP:]
    f32, bf16 = jnp.float32, jnp.bfloat16
    j = pl.program_id(1)
    n_rows = qs_ref.shape[1]

    @pl.when(j == 0)
    def _():
        m_ref[...] = jnp.full(m_ref.shape, MASKED, f32)
        l_ref[...] = jnp.zeros(l_ref.shape, f32)
        acc_ref[...] = jnp.zeros(acc_ref.shape, f32)

    def online(s, weigh):
        m_old = m_ref[...]
        m_new = jnp.maximum(m_old, jnp.max(s, axis=1, keepdims=True))
        alpha = jnp.exp(m_old - m_new)
        p = jnp.exp(s - m_new)
        l_ref[...] = alpha * l_ref[...] + jnp.sum(p, axis=1, keepdims=True)
        acc_ref[...] = alpha * acc_ref[...] + weigh(p.astype(bf16))
        m_ref[...] = m_new

    qs = qs_ref[0]
    pages = [pr[0] for pr in page_refs]
    keys_t = jnp.concatenate([p[:KV_W] for p in pages], axis=1).astype(bf16)
    vals_t = jnp.concatenate([p[KV_W:] for p in pages], axis=1).astype(bf16)
    blk_id = j * (DEC_KEYS // SEL_BLK) + lax.shift_right_logical(
        lax.broadcasted_iota(jnp.int32, (N_SELB, DEC_KEYS), 1), int(math.log2(SEL_BLK)))
    onehot_t = jnp.where(lax.broadcasted_iota(jnp.int32, (N_SELB, DEC_KEYS), 0) == blk_id, 1.0, 0.0).astype(bf16)
    online(_dot(qs, jnp.concatenate([keys_t, onehot_t], axis=0)), lambda p: _dot_nt(p, vals_t))

    @pl.when(j == pl.num_programs(1) - 1)
    def _():
        row_q = qpos0 + (lax.broadcasted_iota(jnp.int32, (n_rows, 1), 0) % n_q)
        qh = qw_ref[0]
        new_pos = past + lax.broadcasted_iota(jnp.int32, (n_rows, NEW_PAD), 1)
        new_ok = (new_pos <= row_q) & (new_pos < past + n_q)
        s_new = jnp.where(new_ok, _dot_nt(qh, knew_ref[0]), MASKED)
        online(s_new, lambda p: _dot(p, vnew_ref[0]))
        o_s = acc_ref[...] / l_ref[...]
        wbuf_t = wbuf_ref[0]
        wnew = wnew_ref[0]
        n_buf = wbuf_t.shape[1]
        s_b = _dot(qh, wbuf_t[:KV_W].astype(bf16))
        pos_b = (past - n_buf) + lax.broadcasted_iota(jnp.int32, (n_rows, n_buf), 1)
        s_b = jnp.where((pos_b > row_q - WINDOW) & (pos_b >= 0), s_b, MASKED)
        s_n = jnp.where(new_ok, _dot_nt(qh, wnew[:, :KV_W].astype(bf16)), MASKED)
        m_w = jnp.maximum(jnp.max(s_b, axis=1, keepdims=True), jnp.max(s_n, axis=1, keepdims=True))
        p_b, p_n = jnp.exp(s_b - m_w), jnp.exp(s_n - m_w)
        l_w = jnp.sum(p_b, axis=1, keepdims=True) + jnp.sum(p_n, axis=1, keepdims=True)
        o_w = (_dot_nt(p_b.astype(bf16), wbuf_t[KV_W:].astype(bf16))
               + _dot(p_n.astype(bf16), wnew[:, KV_W:].astype(bf16))) / l_w
        half = n_rows // NSA_KV_HEADS
        own = lambda a: jnp.concatenate([a[h * half:(h + 1) * half, h * HEAD_DIM:(h + 1) * HEAD_DIM]
                                         for h in range(NSA_KV_HEADS)], axis=0)
        g = g_ref[0]
        o_ref[0] = g[:, 0:1] * oc_ref[0] + g[:, 1:2] * own(o_s) + g[:, 2:3] * own(o_w)


def _dec_attend(cache, page_table, qs, qw, knew, vnew, wbuf, wnew, o_c, gates, n_q, qpos0):
    bsz, n_pages = page_table.shape
    n_rows = qs.shape[1]
    per_b = lambda *tail: pl.BlockSpec((1,) + tail, lambda b, j, pt: (b, 0, 0))
    grid_spec = pltpu.PrefetchScalarGridSpec(
        num_scalar_prefetch=1, grid=(bsz, n_pages // PAGE_GROUP),
        in_specs=_page_specs(n_pages, 1) + [
            per_b(n_rows, KV_W + N_SELB), per_b(n_rows, KV_W), per_b(NEW_PAD, KV_W), per_b(NEW_PAD, KV_W),
            per_b(2 * KV_W, wbuf.shape[2]), per_b(NEW_PAD, 2 * KV_W), per_b(n_rows, HEAD_DIM), per_b(n_rows, 3)],
        out_specs=per_b(n_rows, HEAD_DIM),
        scratch_shapes=[pltpu.VMEM((n_rows, 1), jnp.float32), pltpu.VMEM((n_rows, 1), jnp.float32),
                        pltpu.VMEM((n_rows, KV_W), jnp.float32)])
    return pl.pallas_call(
        functools.partial(_dec_attend_body, qpos0=qpos0, n_q=n_q, past=n_pages * PAGE_SIZE),
        grid_spec=grid_spec,
        out_shape=jax.ShapeDtypeStruct((bsz, n_rows, HEAD_DIM), jnp.float32),
        compiler_params=pltpu.CompilerParams(dimension_semantics=("arbitrary", "arbitrary")),
        name="nsa_dec_attend",
    )(page_table.reshape(-1), *([cache] * PAGE_GROUP), qs, qw, knew, vnew, wbuf, wnew, o_c, gates)


def _pool_matrices(w_cmp_pool):
    sub = np.arange(Q_BLK) // CMP_STRIDE == np.arange(SUBS)[:, None]
    w_rep = jnp.tile(w_cmp_pool.reshape(2, 2, CMP_STRIDE), (1, 1, SUBS))
    return jnp.where(sub[None, None], w_rep[:, :, None, :], 0.0).reshape(4, SUBS, Q_BLK).astype(jnp.bfloat16)


def _compressed_from_pooled(pooled):
    bsz, n_sub, _ = pooled.shape
    pooled = pooled.reshape(bsz, n_sub, 4, NSA_KV_HEADS, HEAD_DIM)
    kc = pooled[:, :-1, 0] + pooled[:, 1:, 1]
    vc = pooled[:, :-1, 2] + pooled[:, 1:, 3]
    pad = lambda a: jnp.pad(a, ((0, 0), (0, 1), (0, 0), (0, 0))).transpose(0, 2, 1, 3)
    return pad(kc), pad(vc)


def _nsa_decode(q_raw, q_rot, gates, rows_full, rows_win, cache, page_table, win_buf, w_cmp_pool, past):
    bsz, n_q = q_raw.shape[:2]
    bf16 = jnp.bfloat16
    n_blk = past // SEL_BLK
    assert past % DEC_KEYS == 0 and n_blk <= N_SELB and n_q <= NEW_PAD
    scale = HEAD_DIM ** -0.5
    cache2 = cache.transpose(0, 2, 3, 4, 1).reshape(cache.shape[0], 4 * KV_W, PAGE_SIZE)
    pooled = _dec_pool(cache2, page_table, _pool_matrices(w_cmp_pool))
    kc_p, vc_p = _compressed_from_pooled(pooled)
    rows_of = lambda a: a.transpose(0, 2, 3, 1, 4).reshape(bsz, NSA_KV_HEADS, NSA_GROUP * n_q, a.shape[-1])
    qr = rows_of((q_raw * scale).astype(bf16))
    n_pick = min(SEL_TOPN, n_blk + 1) - 1
    o_c, selb = _dec_select(qr, kc_p.transpose(0, 1, 3, 2).astype(bf16), vc_p.astype(bf16), n_q, past, n_pick, n_blk)
    qo = rows_of((q_rot * scale).astype(bf16))
    zero = jnp.zeros_like(qo[:, 0])
    qw = jnp.concatenate([jnp.concatenate([qo[:, 0], zero], -1), jnp.concatenate([zero, qo[:, 1]], -1)], axis=1)
    bias = jnp.tile(selb, (1, 1, NSA_GROUP, 1)).reshape(bsz, -1, N_SELB).astype(bf16)
    qs = jnp.concatenate([qw, bias], axis=-1)
    pad_new = lambda a: jnp.pad(a.reshape(bsz, n_q, -1), ((0, 0), (0, NEW_PAD - n_q), (0, 0)))
    knew = pad_new(rows_full[:, :, 2]).astype(bf16)
    vnew = pad_new(rows_full[:, :, 3]).astype(bf16)
    wnew = pad_new(rows_win)
    wbuf = win_buf.transpose(0, 2, 3, 4, 1).reshape(bsz, 2 * KV_W, win_buf.shape[1])
    gt = rows_of(gates).reshape(bsz, -1, 3)
    o = _dec_attend(cache2, page_table, qs, qw, knew, vnew, wbuf, wnew,
                    o_c.reshape(bsz, -1, HEAD_DIM), gt, n_q, past)
    o = o.reshape(bsz, NSA_KV_HEADS, NSA_GROUP, n_q, HEAD_DIM).transpose(0, 3, 1, 2, 4)
    return o.reshape(bsz, n_q, NSA_HEADS * HEAD_DIM)


def _ab_mixer(x, pos, w_in, w_gla_gate, b_gla_gate, gla_norm_g, w_cmp_pool, w_out,
              gla_state, nsa_cache, page_table, win_buf):
    bsz, t_, _ = x.shape
    h_in = _mm(x.reshape(bsz * t_, -1), w_in[:, IN_AB_PERM], keep_pad=True).reshape(bsz, t_, -1)
    o_a, s_a = _gla(h_in, w_gla_gate, b_gla_gate, gla_norm_g, gla_state)
    kv_w = NSA_KV_HEADS * HEAD_DIM
    if nsa_cache is None:
        rows2, win2, kk, vvt, qr, qo, gt, pooled = _nsa_prep(h_in, pos, w_cmp_pool)
        pooled = pooled.reshape(bsz, t_ // CMP_STRIDE, 4, NSA_KV_HEADS, HEAD_DIM)
        kc = pooled[:, :-1, 0] + pooled[:, 1:, 1]
        vc = pooled[:, :-1, 2] + pooled[:, 1:, 3]
        kc_p = jnp.pad(kc, ((0, 0), (0, 1), (0, 0), (0, 0))).transpose(0, 2, 1, 3).astype(jnp.bfloat16)
        vct = jnp.pad(vc, ((0, 0), (0, 1), (0, 0), (0, 0))).transpose(0, 2, 3, 1).astype(jnp.bfloat16)
        o_b = _nsa_prompt(qr, qo, gt, kc_p, vct, kk, vvt)
        rows_full = rows2.reshape(bsz, t_, 4, NSA_KV_HEADS, HEAD_DIM)
        new_win = win2[:, -min(WINDOW, t_):].reshape(bsz, -1, 2, NSA_KV_HEADS, HEAD_DIM)
    else:
        nq = h_in[..., COL_NQ:COL_NKV]
        nkv = h_in[..., COL_NKV:COL_TAIL]
        ngate = h_in[..., COL_TAIL + TAIL_GATE:COL_TAIL + TAIL_GATE + NSA_SIZES[2]]
        q_raw = nq.reshape(bsz, t_, NSA_KV_HEADS, NSA_GROUP, HEAD_DIM)
        q_rot = _partial_rope(q_raw, pos)
        kv = nkv.reshape(bsz, t_, 6, NSA_KV_HEADS, HEAD_DIM)
        k_sel = _partial_rope(kv[:, :, 2], pos)
        k_win = _partial_rope(kv[:, :, 4], pos)
        rows_full = jnp.stack([kv[:, :, 0], kv[:, :, 1], k_sel, kv[:, :, 3]], axis=2)
        rows_win = jnp.stack([k_win, kv[:, :, 5]], axis=2)
        gates = jax.nn.sigmoid(ngate).reshape(bsz, t_, NSA_KV_HEADS, NSA_GROUP, 3)
        past_len = page_table.shape[1] * PAGE_SIZE
        o_b = _nsa_decode(q_raw, q_rot, gates, rows_full, rows_win, nsa_cache, page_table, win_buf,
                          w_cmp_pool, past_len)
        w_buf = win_buf.shape[1]
        kw = jnp.concatenate([win_buf, rows_win], axis=1)
        new_win = kw[:, -w_buf:]
    y = _mm3(jnp.concatenate([o_a, o_b], axis=-1), w_out)
    return y, s_a, rows_full, new_win


CONV_HALO = 32
CONV_LEAD = CONV_HALO - (CONV_W - 1)


def _conv_body(x_ref, buf0_ref, w1_ref, b1_ref, wdw_ref, bdw_ref, g_ref, b_ref, w2_ref, b2_ref,
               o_ref, tail_ref, ext_ref, *, t_last):
    bf16 = jnp.bfloat16
    tt = x_ref.shape[1]
    i = pl.program_id(1)

    @pl.when(i == 0)
    def _():
        ext_ref[0:CONV_HALO, :] = buf0_ref[0]

    h = _dot(x_ref[0].astype(bf16), w1_ref[...]) + b1_ref[...]
    ext_ref[CONV_HALO:CONV_HALO + tt, :] = h[:, :D_CONV] * jax.nn.sigmoid(h[:, D_CONV:])
    c = jnp.zeros((tt, D_CONV), jnp.float32) + bdw_ref[...]
    for k in range(CONV_W):
        c = c + ext_ref[pl.ds(CONV_LEAD + k, tt), :] * wdw_ref[k:k + 1, :]
    c = _ln_rows(c, g_ref[...], b_ref[...])
    c = c * jax.nn.sigmoid(c)
    o_ref[0] = _dot(c.astype(bf16), w2_ref[...]) + b2_ref[...]
    tail_ref[0] = ext_ref[t_last:t_last + CONV_HALO, :]
    ext_ref[0:CONV_HALO, :] = ext_ref[tt:tt + CONV_HALO, :]


def _conv_module(x, conv_buf, w_pw1, b_pw1, w_dw, b_dw, ln_g, ln_b, w_pw2, b_pw2):
    bsz, t_, d = x.shape
    bf16 = jnp.bfloat16
    tp = -(-t_ // 8) * 8
    tt = min(tp, 256)
    n_t = tp // tt
    if tp != t_:
        x = jnp.pad(x, ((0, 0), (0, tp - t_), (0, 0)))
    if conv_buf is None:
        buf0 = jnp.zeros((bsz, CONV_HALO, D_CONV), jnp.float32)
    else:
        buf0 = jnp.pad(conv_buf, ((0, 0), (CONV_LEAD, 0), (0, 0)))
    fixed = lambda shape: pl.BlockSpec(shape, lambda b, i: (0,) * len(shape))
    per_b = pl.BlockSpec((1, CONV_HALO, D_CONV), lambda b, i: (b, 0, 0))
    out, tail = pl.pallas_call(
        functools.partial(_conv_body, t_last=t_ - (n_t - 1) * tt),
        grid=(bsz, n_t),
        in_specs=[pl.BlockSpec((1, tt, d), lambda b, i: (b, i, 0)), per_b,
                  fixed((d, 2 * D_CONV)), fixed((1, 2 * D_CONV)), fixed((CONV_HALO, D_CONV)), fixed((1, D_CONV)),
                  fixed((1, D_CONV)), fixed((1, D_CONV)), fixed((D_CONV, d)), fixed((1, d))],
        out_specs=[pl.BlockSpec((1, tt, d), lambda b, i: (b, i, 0)), per_b],
        out_shape=[jax.ShapeDtypeStruct((bsz, tp, d), jnp.float32),
                   jax.ShapeDtypeStruct((bsz, CONV_HALO, D_CONV), jnp.float32)],
        scratch_shapes=[pltpu.VMEM((CONV_HALO + tt, D_CONV), jnp.float32)],
        compiler_params=pltpu.CompilerParams(dimension_semantics=("arbitrary", "arbitrary"),
                                             vmem_limit_bytes=48 * 1024 * 1024),
        name="conv_module",
    )(x, buf0, w_pw1.astype(bf16), b_pw1.reshape(1, -1), jnp.pad(w_dw, ((0, CONV_HALO - CONV_W), (0, 0))),
      b_dw.reshape(1, -1), ln_g.reshape(1, -1), ln_b.reshape(1, -1), w_pw2.astype(bf16), b_pw2.reshape(1, -1))
    return out[:, :t_], tail[:, CONV_LEAD:]


PACK_W = 256
SC_WINDOW = 128
SC_TILES = 32


def _pack_rows(y):
    out = []
    for h in range(2):
        lo = lax.bitcast_convert_type(y[:, 2 * h * PACK_W:(2 * h + 1) * PACK_W].astype(jnp.bfloat16)
                                      .astype(jnp.float32), jnp.uint32)
        hi = lax.bitcast_convert_type(y[:, (2 * h + 1) * PACK_W:(2 * h + 2) * PACK_W].astype(jnp.bfloat16)
                                      .astype(jnp.float32), jnp.uint32)
        out.append(lax.bitcast_convert_type((lo >> 16) | hi, jnp.int32))
    return out


def _unpack_words(w):
    u = lax.bitcast_convert_type(w, jnp.uint32)
    lo = lax.bitcast_convert_type(u << 16, jnp.float32)
    hi = lax.bitcast_convert_type(u & jnp.uint32(0xFFFF0000), jnp.float32)
    return lo, hi


def _gather_rows(src, idx):
    n = idx.shape[0]
    if n % (SC_WINDOW * SC_TILES) != 0:
        return jnp.take(src, idx, axis=0)
    mesh = plsc.VectorSubcoreMesh(core_axis_name="core", subcore_axis_name="subcore")

    @pl.kernel(out_type=jax.ShapeDtypeStruct((n, src.shape[1]), src.dtype), mesh=mesh)
    def gather_kernel(src_hbm, idx_hbm, out_hbm):
        def step(idx_vmem, out_vmem):
            pltpu.sync_copy(src_hbm.at[idx_vmem.at[0]], out_vmem)

        pltpu.emit_pipeline(
            step, grid=(n // SC_WINDOW,),
            in_specs=[pl.BlockSpec((1, SC_WINDOW), index_map=lambda i: (0, i))],
            out_specs=[pl.BlockSpec((SC_WINDOW, src.shape[1]), index_map=lambda i: (i, 0))],
            core_axis_name=("core", "subcore"),
            dimension_semantics=(pltpu.PARALLEL,),
        )(idx_hbm, out_hbm)

    return gather_kernel(src, idx.reshape(1, n))


def _scatter_rows(src, idx, n_out):
    n = idx.shape[0]
    m = src.shape[0] // 2
    reps = n // (2 * m)
    if n % (SC_WINDOW * SC_TILES) != 0 or m % SC_WINDOW != 0:
        rows = jnp.arange(n, dtype=jnp.int32)
        src_row = (rows // (reps * m)) * m + rows % m
        return jnp.zeros((n_out, src.shape[1]), src.dtype).at[idx].set(jnp.take(src, src_row, axis=0))
    tiles = m // SC_WINDOW
    mesh = plsc.VectorSubcoreMesh(core_axis_name="core", subcore_axis_name="subcore")

    @pl.kernel(out_type=jax.ShapeDtypeStruct((n_out, src.shape[1]), src.dtype), mesh=mesh, scratch_types=[])
    def scatter_kernel(src_hbm, idx_hbm, out_hbm):
        def step(src_vmem, idx_vmem):
            pltpu.sync_copy(src_vmem, out_hbm.at[idx_vmem.at[0]])

        pltpu.emit_pipeline(
            step, grid=(n // SC_WINDOW,),
            in_specs=[pl.BlockSpec((SC_WINDOW, src.shape[1]),
                                   index_map=lambda i: ((i // (reps * tiles)) * tiles + i % tiles, 0)),
                      pl.BlockSpec((1, SC_WINDOW), index_map=lambda i: (0, i))],
            out_specs=[],
            core_axis_name=("core", "subcore"),
            dimension_semantics=(pltpu.PARALLEL,),
        )(src_hbm, idx_hbm)

    return scatter_kernel(src, idx.reshape(1, n))


PER_GROUP = N_EXPERTS // N_GROUPS
PICKED = -3e38


def _ln_rows(v, g, b):
    mu = jnp.mean(v, axis=-1, keepdims=True)
    c = v - mu
    var = jnp.mean(c * c, axis=-1, keepdims=True)
    return c * lax.rsqrt(var + LN_EPS) * g + b


def _first_max(v, ids, axes, sentinel):
    best = v
    for a in axes:
        best = jnp.max(best, axis=a, keepdims=True)
    first = jnp.where(v == best, ids, sentinel)
    for a in axes:
        first = jnp.min(first, axis=a, keepdims=True)
    return best, first


def _sum_axes(v, axes):
    for a in axes:
        v = jnp.sum(v, axis=a, keepdims=True)
    return v


def _moe_pre_body(x_ref, mix_ref, g_ref, b_ref, wr_ref, br_ref, wgu_ref, wdn_ref,
                  x1_ref, xp_ref, sh_ref, eidx_ref, gate_ref, rank_ref, cnt_ref, run_ref):
    f32, bf16 = jnp.float32, jnp.bfloat16
    tm = x_ref.shape[0]

    @pl.when(pl.program_id(0) == 0)
    def _():
        run_ref[...] = jnp.zeros(run_ref.shape, f32)

    x1 = _ln_rows(ALPHA * x_ref[...] + mix_ref[...], g_ref[...], b_ref[...])
    x1_ref[...] = x1
    x1b = x1.astype(bf16)
    xp_ref[0], xp_ref[1] = _pack_rows(x1)

    h = _dot(x1b, wgu_ref[...])
    d_sh = h.shape[1] // 2
    act = (jax.nn.silu(h[:, :d_sh]) * h[:, d_sh:]).astype(bf16)
    sh_ref[...] = _dot(act, wdn_ref[...])

    s = jax.nn.sigmoid(_dot_nt(wr_ref[...], x1b)).reshape(N_GROUPS, PER_GROUP, tm)
    sb = s + br_ref[...].reshape(N_GROUPS, PER_GROUP, 1)
    shape3 = (N_GROUPS, PER_GROUP, tm)
    pid = lax.broadcasted_iota(jnp.int32, shape3, 1)
    gid = lax.broadcasted_iota(jnp.int32, (N_GROUPS, 1, tm), 0)
    eid = lax.broadcasted_iota(jnp.int32, shape3, 0) * PER_GROUP + pid
    top1, i1 = _first_max(sb, pid, (1,), PER_GROUP)
    top2 = jnp.max(jnp.where(pid == i1, PICKED, sb), axis=1, keepdims=True)
    gscore = top1 + top2
    gsel = jnp.zeros((N_GROUPS, 1, tm), f32)
    for _ in range(TOPK_GROUPS):
        _, first = _first_max(gscore, gid, (0,), N_GROUPS)
        hit = gid == first
        gsel = jnp.where(hit, 1.0, gsel)
        gscore = jnp.where(hit, PICKED, gscore)
    cand = jnp.where(gsel > 0.0, sb, -1e30)
    firsts, gates = [], []
    picked = jnp.zeros(shape3, f32)
    for _ in range(TOP_K):
        _, first = _first_max(cand, eid, (0, 1), N_EXPERTS)
        hit = eid == first
        firsts.append(first)
        gates.append(_sum_axes(jnp.where(hit, s, 0.0), (0, 1)))
        picked = jnp.where(hit, 1.0, picked)
        cand = jnp.where(hit, PICKED, cand)
    gsum = gates[0]
    for gk in gates[1:]:
        gsum = gsum + gk
    earlier = (lax.broadcasted_iota(jnp.int32, (tm, tm), 0) < lax.broadcasted_iota(jnp.int32, (tm, tm), 1))
    picked2 = picked.reshape(N_EXPERTS, tm)
    rank = run_ref[...] + _dot(picked2.astype(bf16), jnp.where(earlier, 1.0, 0.0).astype(bf16))
    run_new = run_ref[...] + jnp.sum(picked2, axis=1, keepdims=True)
    run_ref[...] = run_new
    cnt_ref[...] = jnp.broadcast_to(run_new, cnt_ref.shape)
    rank3 = rank.reshape(shape3)
    for k in range(TOP_K):
        hit = eid == firsts[k]
        eidx_ref[k:k + 1, :] = firsts[k].reshape(1, tm)
        gate_ref[k:k + 1, :] = (gates[k] / gsum * ROUTE_SCALE).reshape(1, tm)
        rank_ref[k:k + 1, :] = _sum_axes(jnp.where(hit, rank3, 0.0), (0, 1)).reshape(1, tm).astype(jnp.int32)


def _moe_pre(x, mix, g, b, w_router, b_router, w_sh_gu, w_sh_down):
    m, d = x.shape
    bf16 = jnp.bfloat16
    tm = min(m, 512)
    row = lambda i: (i, 0)
    col = lambda i: (0, i)
    fixed = lambda i: (0, 0)
    d_sh2 = w_sh_gu.shape[1]
    return pl.pallas_call(
        _moe_pre_body,
        grid=(m // tm,),
        in_specs=[pl.BlockSpec((tm, d), row), pl.BlockSpec((tm, d), row),
                  pl.BlockSpec((1, d), fixed), pl.BlockSpec((1, d), fixed),
                  pl.BlockSpec((N_EXPERTS, d), fixed), pl.BlockSpec((N_EXPERTS, 1), fixed),
                  pl.BlockSpec((d, d_sh2), fixed), pl.BlockSpec((d_sh2 // 2, d), fixed)],
        out_specs=[pl.BlockSpec((tm, d), row), pl.BlockSpec((2, tm, PACK_W), lambda i: (0, i, 0)),
                   pl.BlockSpec((tm, d), row),
                   pl.BlockSpec((TOP_K, tm), col), pl.BlockSpec((TOP_K, tm), col), pl.BlockSpec((TOP_K, tm), col),
                   pl.BlockSpec((N_EXPERTS, LANE), fixed)],
        out_shape=[jax.ShapeDtypeStruct((m, d), jnp.float32), jax.ShapeDtypeStruct((2, m, PACK_W), jnp.int32),
                   jax.ShapeDtypeStruct((m, d), jnp.float32),
                   jax.ShapeDtypeStruct((TOP_K, m), jnp.int32), jax.ShapeDtypeStruct((TOP_K, m), jnp.float32),
                   jax.ShapeDtypeStruct((TOP_K, m), jnp.int32),
                   jax.ShapeDtypeStruct((N_EXPERTS, LANE), jnp.float32)],
        scratch_shapes=[pltpu.VMEM((N_EXPERTS, 1), jnp.float32)],
        compiler_params=pltpu.CompilerParams(dimension_semantics=("arbitrary",),
                                             vmem_limit_bytes=48 * 1024 * 1024),
        name="moe_pre",
    )(x, mix, g.reshape(1, d), b.reshape(1, d), w_router.T.astype(bf16), b_router.reshape(N_EXPERTS, 1),
      w_sh_gu.astype(bf16), w_sh_down.astype(bf16))


def _moe_expert_body(exp_ref, first_ref, rows_ref, xs_ref, wgu_ref, wdn_ref, y_ref, wgu_bf, wdn_bf):
    i = pl.program_id(0)
    bf16 = jnp.bfloat16

    @pl.when(first_ref[i] == 1)
    def _():
        wgu_bf[...] = wgu_ref[0].astype(bf16)
        wdn_bf[...] = wdn_ref[0].astype(bf16)

    @pl.when(rows_ref[i] > 0)
    def _():
        live = lax.broadcasted_iota(jnp.int32, (xs_ref.shape[1], 1), 0) < rows_ref[i]
        h = None
        for hw in range(2):
            for q, xq in enumerate(_unpack_words(xs_ref[hw])):
                r0 = (2 * hw + q) * PACK_W
                part = _dot(jnp.where(live, xq, 0.0).astype(bf16), wgu_bf[r0:r0 + PACK_W, :])
                h = part if h is None else h + part
        d_e = h.shape[1] // 2
        act = (jax.nn.silu(h[:, :d_e]) * h[:, d_e:]).astype(bf16)
        y_ref[0], y_ref[1] = _pack_rows(_dot(act, wdn_bf[...]))

    @pl.when(rows_ref[i] == 0)
    def _():
        y_ref[...] = jnp.zeros(y_ref.shape, y_ref.dtype)


def _moe_experts(xs, blk_exp, blk_first, blk_rows, w_exp_gu, w_exp_down, bm):
    n_slots = xs.shape[1]
    d = w_exp_gu.shape[1]
    n_blk = n_slots // bm
    d_e2 = w_exp_gu.shape[2]
    words = lambda i, e, f, a: (0, i, 0)
    grid_spec = pltpu.PrefetchScalarGridSpec(
        num_scalar_prefetch=3,
        grid=(n_blk,),
        in_specs=[pl.BlockSpec((2, bm, PACK_W), words),
                  pl.BlockSpec((1, d, d_e2), lambda i, e, f, a: (e[i], 0, 0)),
                  pl.BlockSpec((1, d_e2 // 2, d), lambda i, e, f, a: (e[i], 0, 0))],
        out_specs=pl.BlockSpec((2, bm, PACK_W), words),
        scratch_shapes=[pltpu.VMEM((d, d_e2), jnp.bfloat16), pltpu.VMEM((d_e2 // 2, d), jnp.bfloat16)])
    return pl.pallas_call(
        _moe_expert_body,
        grid_spec=grid_spec,
        out_shape=jax.ShapeDtypeStruct((2, n_slots, PACK_W), jnp.int32),
        compiler_params=pltpu.CompilerParams(dimension_semantics=("arbitrary",),
                                             vmem_limit_bytes=48 * 1024 * 1024),
        name="moe_experts",
    )(blk_exp, blk_first, blk_rows, xs, w_exp_gu, w_exp_down)


def _combine_ln_body(x_ref, yg_ref, gt_ref, sh_ref, g_ref, b_ref, o_ref):
    gt = gt_ref[...]
    parts = []
    for hw in range(2):
        lo_acc = hi_acc = None
        for k in range(TOP_K):
            lo, hi = _unpack_words(yg_ref[hw, k])
            gk = gt[:, k:k + 1]
            lo_acc = lo * gk if lo_acc is None else lo_acc + lo * gk
            hi_acc = hi * gk if hi_acc is None else hi_acc + hi * gk
        parts += [lo_acc, hi_acc]
    routed = jnp.concatenate(parts, axis=1)
    o_ref[...] = _ln_rows(ALPHA * x_ref[...] + (routed + sh_ref[...]), g_ref[...], b_ref[...])


def _combine_ln(x, yg, gate_t, shared, g, b):
    m, d = x.shape
    tm = min(m, 256)
    row = lambda i: (i, 0)
    fixed = lambda i: (0, 0)
    return pl.pallas_call(
        _combine_ln_body,
        grid=(m // tm,),
        in_specs=[pl.BlockSpec((tm, d), row), pl.BlockSpec((2, TOP_K, tm, PACK_W), lambda i: (0, 0, i, 0)),
                  pl.BlockSpec((tm, TOP_K), row), pl.BlockSpec((tm, d), row),
                  pl.BlockSpec((1, d), fixed), pl.BlockSpec((1, d), fixed)],
        out_specs=pl.BlockSpec((tm, d), row),
        out_shape=jax.ShapeDtypeStruct((m, d), jnp.float32),
        compiler_params=pltpu.CompilerParams(dimension_semantics=("arbitrary",)),
        name="combine_ln",
    )(x, yg, gate_t, shared, g.reshape(1, d), b.reshape(1, d))


def _moe_layer(x, mix, ln1_g, ln1_b, ln2_g, ln2_b, w_router, b_router, w_exp_gu, w_exp_down, w_sh_gu, w_sh_down):
    m, d = x.shape
    x1, xp, shared, eidx, gate8, rank8, counts = _moe_pre(x, mix, ln1_g, ln1_b, w_router, b_router,
                                                           w_sh_gu, w_sh_down)
    bm = 512 if m * TOP_K >= 512 * N_EXPERTS else MOE_BLK
    n_blk = (m * TOP_K) // bm + N_EXPERTS
    counts = counts[:, 0].astype(jnp.int32)
    padded = (counts + bm - 1) // bm * bm
    pad_end = jnp.cumsum(padded)
    pad_start = pad_end - padded
    start_of = jnp.sum(jnp.where(eidx[:, :, None] == jnp.arange(N_EXPERTS), pad_start, 0), axis=-1)
    dest = (start_of + rank8).reshape(-1)
    blk_start = jnp.arange(n_blk, dtype=jnp.int32) * bm
    blk_exp = jnp.minimum(jnp.sum(pad_end[None, :] <= blk_start[:, None], axis=1), N_EXPERTS - 1).astype(jnp.int32)
    blk_rows = jnp.clip(counts[blk_exp] - (blk_start - pad_start[blk_exp]), 0, bm).astype(jnp.int32)
    blk_first = jnp.concatenate([jnp.ones((1,), jnp.int32), (blk_exp[1:] != blk_exp[:-1]).astype(jnp.int32)])
    n_slots = n_blk * bm
    xs = _scatter_rows(xp.reshape(2 * m, PACK_W), jnp.concatenate([dest, dest + n_slots]), 2 * n_slots)
    y = _moe_experts(xs.reshape(2, n_slots, PACK_W), blk_exp, blk_first, blk_rows, w_exp_gu, w_exp_down, bm)
    yg = _gather_rows(y.reshape(2 * n_slots, PACK_W), jnp.concatenate([dest, dest + n_slots]))
    return _combine_ln(x1, yg.reshape(2, TOP_K, m, PACK_W), gate8.T, shared, ln2_g, ln2_b)


def _trunk(x, pos, gla_state, nsa_cache, page_table, win_buf, conv_buf,
           w_in_ab, w_gla_gate, b_gla_gate, gla_norm_g, w_cmp_pool, w_out_ab,
           w_pw1, b_pw1, w_dw, b_dw, conv_ln_g, conv_ln_b, w_pw2, b_pw2,
           ln_g, ln_b, w_router, b_router, w_exp_gu, w_exp_down, w_sh_gu, w_sh_down):
    new_gla, new_rows, new_win, new_conv = [], [], [], []
    for layer in range(DEPTH):
        i = layer // 2
        if layer % 2 == 0:
            mix, s_a, rows, win = _ab_mixer(
                x, pos, w_in_ab[i], w_gla_gate[i], b_gla_gate[i], gla_norm_g[i], w_cmp_pool[i], w_out_ab[i],
                None if gla_state is None else gla_state[i],
                None if nsa_cache is None else nsa_cache[i], page_table,
                None if win_buf is None else win_buf[i])
            new_gla.append(s_a)
            new_rows.append(rows)
            new_win.append(win)
        else:
            mix, cb = _conv_module(x, None if conv_buf is None else conv_buf[i], w_pw1[i], b_pw1[i],
                                   w_dw[i], b_dw[i], conv_ln_g[i], conv_ln_b[i], w_pw2[i], b_pw2[i])
            new_conv.append(cb)
        bsz, t_, d = x.shape
        x = _moe_layer(x.reshape(-1, d), mix.reshape(-1, d), ln_g[layer, 0], ln_b[layer, 0],
                       ln_g[layer, 1], ln_b[layer, 1], w_router[layer], b_router[layer],
                       w_exp_gu[layer], w_exp_down[layer], w_sh_gu[layer], w_sh_down[layer]).reshape(bsz, t_, d)
    return x, jnp.stack(new_gla), jnp.stack(new_rows), jnp.stack(new_win), jnp.stack(new_conv)


def kernel(x_prompt, x_sample, state_gla, cache_nsa_kv, state_nsa_win, state_conv, page_table,
           w_in_ab, w_gla_gate, b_gla_gate, gla_norm_g, w_cmp_pool, w_out_ab,
           w_pw1, b_pw1, w_dw, b_dw, conv_ln_g, conv_ln_b, w_pw2, b_pw2,
           ln_g, ln_b, w_router, b_router, w_exp_gu, w_exp_down, w_sh_gu, w_sh_down):
    weights = (w_in_ab, w_gla_gate, b_gla_gate, gla_norm_g, w_cmp_pool, w_out_ab,
               w_pw1, b_pw1, w_dw, b_dw, conv_ln_g, conv_ln_b, w_pw2, b_pw2,
               ln_g, ln_b, w_router, b_router, w_exp_gu, w_exp_down, w_sh_gu, w_sh_down)
    past_len = page_table.shape[1] * PAGE_SIZE
    pos_p = jnp.arange(x_prompt.shape[1])
    pos_s = past_len + jnp.arange(x_sample.shape[1])
    y_prompt, gla_p, rows_p, win_p, conv_p = _trunk(x_prompt, pos_p, None, None, None, None, None, *weights)
    y_sample, gla_s, rows_s, win_s, conv_s = _trunk(x_sample, pos_s, state_gla, cache_nsa_kv, page_table,
                                                    state_nsa_win, state_conv, *weights)
    return (y_prompt, y_sample, gla_p, gla_s, rows_p, rows_s, win_p, win_s, conv_p, conv_s)
```

```python
import functools
import math

import jax
import jax.numpy as jnp
import numpy as np
from jax import lax
from jax.experimental import pallas as pl
from jax.experimental.pallas import tpu as pltpu
from jax.experimental.pallas import tpu_sc as plsc

D_MODEL = 1024
DEPTH = 2
PAGE_SIZE = 128

GLA_HEADS = 4
GLA_DV = D_MODEL // 2 // GLA_HEADS
GLA_DK = GLA_DV // 2
GLA_RANK = 16
GLA_TAU = 16.0

NSA_HEADS = 8
NSA_KV_HEADS = 2
NSA_GROUP = NSA_HEADS // NSA_KV_HEADS
HEAD_DIM = D_MODEL // 2 // NSA_HEADS
CMP_BLK = 32
CMP_STRIDE = 16
SEL_BLK = 64
SEL_TOPN = 16
WINDOW = 512
Q_BLK = 128
FORCE_BONUS = 100.0
ROPE_DIM = HEAD_DIM // 4
ROPE_THETA = 500000.0

GLA_SIZES = (GLA_HEADS * GLA_DK, GLA_HEADS * GLA_DK, GLA_HEADS * GLA_DV, GLA_HEADS * GLA_DV, GLA_RANK)
NSA_SIZES = (NSA_HEADS * HEAD_DIM, 6 * NSA_KV_HEADS * HEAD_DIM, 3 * NSA_HEADS)

CONV_W = 31
D_CONV = D_MODEL

N_EXPERTS = 64
N_GROUPS = 8
TOPK_GROUPS = 4
TOP_K = 8
D_EXPERT = 256
ROUTE_SCALE = 2.5
MOE_BLK = 128

ALPHA = (2 * DEPTH) ** 0.25
LN_EPS = 1e-5

LANE = 128
V7X_VMEM_BYTES = 64 * 1024 * 1024
VMEM_LIMIT = V7X_VMEM_BYTES * 3 // 4


def _dot(a, b):
    return jnp.dot(a, b, preferred_element_type=jnp.float32)


def _dot_nt(a, b):
    return lax.dot_general(a, b, (((1,), (1,)), ((), ())), preferred_element_type=jnp.float32)


def _mm_body(x_ref, w_ref, o_ref):
    o_ref[...] = _dot(x_ref[...].astype(jnp.bfloat16), w_ref[...].astype(jnp.bfloat16))


def _mm(x, w, keep_pad=False):
    m, k = x.shape
    n = w.shape[1]
    n_pad = -(-n // LANE) * LANE
    w = w.astype(jnp.bfloat16)
    if n_pad != n:
        w = jnp.pad(w, ((0, 0), (0, n_pad - n)))
    tm = min(m, 512)
    out = pl.pallas_call(
        _mm_body,
        grid=(m // tm,),
        in_specs=[pl.BlockSpec((tm, k), lambda i: (i, 0)),
                  pl.BlockSpec((k, n_pad), lambda i: (0, 0))],
        out_specs=pl.BlockSpec((tm, n_pad), lambda i: (i, 0)),
        out_shape=jax.ShapeDtypeStruct((m, n_pad), jnp.float32),
        compiler_params=pltpu.CompilerParams(dimension_semantics=("arbitrary",),
                                             vmem_limit_bytes=VMEM_LIMIT),
        name="mm",
    )(x, w)
    return out if keep_pad or n_pad == n else out[:, :n]


def _mm_pair_body(a_ref, b_ref, w_ref, o_ref):
    ka = a_ref.shape[1]
    o_ref[...] = (_dot(a_ref[...].astype(jnp.bfloat16), w_ref[0:ka, :])
                  + _dot(b_ref[...].astype(jnp.bfloat16), w_ref[ka:, :]))


def _mm_pair(a, b, w):
    m, ka = a.shape
    kb = b.shape[1]
    n = w.shape[1]
    tm = min(m, 512)
    return pl.pallas_call(
        _mm_pair_body,
        grid=(m // tm,),
        in_specs=[pl.BlockSpec((tm, ka), lambda i: (i, 0)), pl.BlockSpec((tm, kb), lambda i: (i, 0)),
                  pl.BlockSpec((ka + kb, n), lambda i: (0, 0))],
        out_specs=pl.BlockSpec((tm, n), lambda i: (i, 0)),
        out_shape=jax.ShapeDtypeStruct((m, n), jnp.float32),
        compiler_params=pltpu.CompilerParams(dimension_semantics=("arbitrary",)),
        name="mm_pair",
    )(a, b, w.astype(jnp.bfloat16))


def _partial_rope(x, pos):
    half = ROPE_DIM // 2
    inv_freq = jnp.power(ROPE_THETA, -jnp.arange(half, dtype=jnp.float32) / half)
    ang = pos.astype(jnp.float32)[:, None] * inv_freq
    ang = ang.reshape(ang.shape[0], *([1] * (x.ndim - 3)), half)
    cos, sin = jnp.cos(ang), jnp.sin(ang)
    x1 = x[..., :half]
    x2 = x[..., half:ROPE_DIM]
    rot = jnp.concatenate([x1 * cos - x2 * sin, x2 * cos + x1 * sin], -1)
    return jnp.concatenate([rot, x[..., ROPE_DIM:]], -1)


NSA_ROWS = NSA_GROUP * Q_BLK
SEL_KT = 1024
N_SELB = 128
MASKED = -1e9
WIN_KEYS = WINDOW + Q_BLK
KK_W = 2 * HEAD_DIM + N_SELB


def _nsa_prompt_body(qr_ref, qo_ref, kc_ref, vct_ref, kk_ref, vvt_ref, g_ref, o_ref,
                     imp_ref, m_ref, l_ref, acc_ref):
    f32, bf16 = jnp.float32, jnp.bfloat16
    qb = pl.program_id(2)
    q0 = qb * Q_BLK
    qr_t = qr_ref[0, 0, 0]
    qo_t = qo_ref[0, 0, 0]
    n_cmp = kc_ref.shape[2]

    s_c = _dot(kc_ref[0, 0], qr_t)
    n_idx = lax.broadcasted_iota(jnp.int32, (n_cmp, NSA_ROWS), 0)
    qpos_c = q0 + (lax.broadcasted_iota(jnp.int32, (n_cmp, NSA_ROWS), 1) & (Q_BLK - 1))
    cmask = (n_idx * CMP_STRIDE + (CMP_BLK - 1)) <= qpos_c
    s_c = jnp.where(cmask, s_c, MASKED)
    m_c = jnp.max(s_c, axis=0, keepdims=True)
    p_c = jnp.where(cmask, jnp.exp(s_c - m_c), 0.0)
    p_c = p_c / jnp.maximum(jnp.sum(p_c, axis=0, keepdims=True), 1e-30)
    o_ct = _dot(vct_ref[0, 0], p_c.astype(bf16))

    imp = (p_c[:, 0:Q_BLK] + p_c[:, Q_BLK:2 * Q_BLK]) + p_c[:, 2 * Q_BLK:3 * Q_BLK] + p_c[:, 3 * Q_BLK:]
    imp_ref[0:8, :] = jnp.zeros((8, Q_BLK), f32)
    imp_ref[8:8 + n_cmp, :] = imp
    ratio = SEL_BLK // CMP_STRIDE
    n_selb = n_cmp // ratio
    imp_s = imp_ref[pl.ds(7, n_selb, stride=ratio), :]
    for r in range(ratio):
        imp_s = imp_s + imp_ref[pl.ds(8 + r, n_selb, stride=ratio), :]
    blk = lax.broadcasted_iota(jnp.int32, (n_selb, Q_BLK), 0)
    qpos_s = q0 + lax.broadcasted_iota(jnp.int32, (n_selb, Q_BLK), 1)
    cur = lax.shift_right_logical(qpos_s, int(math.log2(SEL_BLK)))
    valid = blk * SEL_BLK <= qpos_s
    forced = (blk == 0) | (blk == cur) | (blk == cur - 1)
    score = jnp.where(valid, imp_s + jnp.where(forced, FORCE_BONUS, 0.0), -1e30)
    picked = jnp.zeros((n_selb, Q_BLK), f32)
    for _ in range(SEL_TOPN):
        best = jnp.max(score, axis=0, keepdims=True)
        first = jnp.min(jnp.where(score == best, blk, n_selb), axis=0, keepdims=True)
        hit = blk == first
        picked = jnp.where(hit, 1.0, picked)
        score = jnp.where(hit, -3e38, score)
    selb_t = jnp.where(valid, picked, 0.0)
    if n_selb < N_SELB:
        selb_t = jnp.concatenate([selb_t, jnp.zeros((N_SELB - n_selb, Q_BLK), f32)], axis=0)
    selb_t = ((selb_t - 1.0) * (-MASKED)).astype(bf16)
    selb_t = jnp.concatenate([selb_t] * NSA_GROUP, axis=1)

    zeros_q = jnp.zeros((HEAD_DIM, NSA_ROWS), bf16)
    q_sel = jnp.concatenate([qo_t, zeros_q, selb_t], axis=0)
    q_win = jnp.concatenate([zeros_q, qo_t, jnp.zeros((N_SELB, NSA_ROWS), bf16)], axis=0)
    qpos_r = q0 + (lax.broadcasted_iota(jnp.int32, (1, NSA_ROWS), 1) & (Q_BLK - 1))

    def v_tiles(first, count):
        return jnp.concatenate([vvt_ref[0, 0, first + j] for j in range(count)], axis=1)

    m_ref[...] = jnp.full(m_ref.shape, MASKED, f32)
    l_ref[...] = jnp.zeros(l_ref.shape, f32)
    acc_ref[...] = jnp.zeros(acc_ref.shape, f32)

    def sel_tile(k0, kt, causal):
        s = _dot(kk_ref[0, 0, pl.ds(k0, kt), :], q_sel)
        if causal:
            kpos = k0 + lax.broadcasted_iota(jnp.int32, (kt, NSA_ROWS), 0)
            s = jnp.where(kpos <= qpos_r, s, MASKED)
        m_old = m_ref[...]
        m_new = jnp.maximum(m_old, jnp.max(s, axis=0, keepdims=True))
        alpha = jnp.exp(m_old - m_new)
        p = jnp.exp(s - m_new)
        l_ref[...] = alpha * l_ref[...] + jnp.sum(p, axis=0, keepdims=True)
        vt = v_tiles(k0 // Q_BLK, kt // Q_BLK)
        acc_ref[...] = alpha * acc_ref[...] + _dot(vt, p.astype(bf16))
        m_ref[...] = m_new

    n_full = q0 // SEL_KT

    def full_step(t, c):
        sel_tile(pl.multiple_of(t * SEL_KT, SEL_KT), SEL_KT, False)
        return c

    lax.fori_loop(0, n_full, full_step, 0)
    sel_tile(pl.multiple_of(n_full * SEL_KT, SEL_KT), SEL_KT, True)
    o_st = acc_ref[0:HEAD_DIM, :] / l_ref[...]

    w0 = pl.multiple_of(jnp.maximum(q0 - WINDOW, 0), Q_BLK)
    s_w = _dot(kk_ref[0, 0, pl.ds(w0, WIN_KEYS), :], q_win)
    kpos_w = w0 + lax.broadcasted_iota(jnp.int32, (WIN_KEYS, NSA_ROWS), 0)
    s_w = jnp.where((kpos_w <= qpos_r) & (kpos_w > qpos_r - WINDOW), s_w, MASKED)
    p_w = jnp.exp(s_w - jnp.max(s_w, axis=0, keepdims=True))
    l_w = jnp.sum(p_w, axis=0, keepdims=True)
    acc_w = _dot(v_tiles(w0 // Q_BLK, WIN_KEYS // Q_BLK), p_w.astype(bf16))
    o_wt = acc_w[HEAD_DIM:2 * HEAD_DIM, :] / l_w

    g = g_ref[0, 0, 0]
    out_t = g[0:1, :] * o_ct + g[1:2, :] * o_st + g[2:3, :] * o_wt
    o_ref[0] = jnp.concatenate([out_t[:, g_ * Q_BLK:(g_ + 1) * Q_BLK] for g_ in range(NSA_GROUP)], axis=0).T


def _nsa_prompt(qr, qo, gt, kc_p, vct, kk, vvt):
    bsz, _, nqb = qr.shape[:3]
    t_ = nqb * Q_BLK
    n_cmp = kc_p.shape[2]
    per_blk = lambda b, h, i: (b, h, i, 0, 0)
    per_head = lambda b, h, i: (b, h, 0, 0)
    return pl.pallas_call(
        _nsa_prompt_body,
        grid=(bsz, NSA_KV_HEADS, nqb),
        in_specs=[pl.BlockSpec((1, 1, 1, HEAD_DIM, NSA_ROWS), per_blk),
                  pl.BlockSpec((1, 1, 1, HEAD_DIM, NSA_ROWS), per_blk),
                  pl.BlockSpec((1, 1, n_cmp, HEAD_DIM), per_head),
                  pl.BlockSpec((1, 1, HEAD_DIM, n_cmp), per_head),
                  pl.BlockSpec((1, 1, t_, KK_W), per_head),
                  pl.BlockSpec((1, 1, nqb, 2 * HEAD_DIM, Q_BLK), lambda b, h, i: (b, h, 0, 0, 0)),
                  pl.BlockSpec((1, 1, 1, 3, NSA_ROWS), per_blk)],
        out_specs=pl.BlockSpec((1, Q_BLK, NSA_GROUP * HEAD_DIM), lambda b, h, i: (b, i, h)),
        out_shape=jax.ShapeDtypeStruct((bsz, t_, NSA_HEADS * HEAD_DIM), jnp.float32),
        scratch_shapes=[pltpu.VMEM((8 + n_cmp, Q_BLK), jnp.float32),
                        pltpu.VMEM((1, NSA_ROWS), jnp.float32),
                        pltpu.VMEM((1, NSA_ROWS), jnp.float32),
                        pltpu.VMEM((2 * HEAD_DIM, NSA_ROWS), jnp.float32)],
        compiler_params=pltpu.CompilerParams(
            dimension_semantics=("arbitrary", "arbitrary", "arbitrary"),
            vmem_limit_bytes=VMEM_LIMIT),
        name="nsa_prompt",
    )(qr, qo, kc_p, vct, kk, vvt, gt)


GLA_SUB = 16
GLA_QK = GLA_HEADS * GLA_DK
GLA_V = GLA_HEADS * GLA_DV


def _dot_tn(a, b):
    return lax.dot_general(a, b, (((0,), (0,)), ((), ())), preferred_element_type=jnp.float32)


def _gla_body(q_ref, k_ref, v_ref, gr_ref, glr_ref, wg_ref, bg_ref, ng_ref, s0_ref, exp_ref, bd_ref,
              o_ref, sfin_ref, st_ref, b_ref, qd_ref, *, t_valid):
    f32, bf16 = jnp.float32, jnp.bfloat16
    tt = q_ref.shape[1]
    ti = pl.program_id(1)

    @pl.when(ti == 0)
    def _():
        st_ref[...] = s0_ref[0]

    row = lax.broadcasted_iota(jnp.int32, (tt, 1), 0)
    z = _dot(glr_ref[0][:, :GLA_RANK].astype(bf16), wg_ref[...]) + bg_ref[...]
    la = (jnp.minimum(z, 0.0) - jnp.log1p(jnp.exp(-jnp.abs(z)))) * (1.0 / GLA_TAU)
    la = jnp.where(ti * tt + row < t_valid, la, 0.0)
    seg = row & (GLA_SUB - 1)
    b = la
    for s in (1, 2, 4, 8):
        b = b + jnp.where(seg >= s, pltpu.roll(b, s, axis=0), 0.0)
    q = q_ref[0] * (GLA_DK ** -0.5)
    k = k_ref[0]
    v = v_ref[0]
    o = _dot((q * k).astype(bf16), exp_ref[...]) * v
    for d in range(1, GLA_SUB):
        decay = jnp.exp(jnp.minimum(b - pltpu.roll(b, d, axis=0), 0.0))
        w = jnp.where(seg >= d, q * pltpu.roll(k, d, axis=0) * decay, 0.0)
        o = o + _dot(w.astype(bf16), exp_ref[...]) * pltpu.roll(v, d, axis=0)
    o_ref[0] = o
    b_ref[...] = b
    qd_ref[...] = (q * jnp.exp(b)).astype(bf16)

    def block_step(c, carry):
        rows = pl.ds(pl.multiple_of(c * GLA_SUB, GLA_SUB), GLA_SUB)
        st = st_ref[...]
        o_ref[0, rows, :] += _dot_nt(qd_ref[rows, :], st.astype(bf16))
        bc = b_ref[rows, :]
        bl = bc[GLA_SUB - 1:GLA_SUB, :]
        kc = (k_ref[0, rows, :] * jnp.exp(bl - bc)).astype(bf16)
        upd = _dot_tn(v_ref[0, rows, :].astype(bf16), kc)
        st_ref[...] = st * jnp.exp(bl) + upd * bd_ref[...]
        return carry

    lax.fori_loop(0, tt // GLA_SUB, block_step, 0)
    sfin_ref[0] = st_ref[...]
    gr = gr_ref[0]
    gate = gr * jax.nn.sigmoid(gr)
    for h in range(GLA_HEADS):
        cols = slice(h * GLA_DV, (h + 1) * GLA_DV)
        oh = o_ref[0, :, cols]
        ms = jnp.mean(oh * oh, axis=-1, keepdims=True)
        o_ref[0, :, cols] = oh * lax.rsqrt(ms + LN_EPS) * ng_ref[...] * gate[:, cols]


def _gla(h, w_gla_gate, b_gla_gate, gla_norm_g, gla_state):
    bsz, t_, n_in = h.shape
    tp = -(-t_ // GLA_SUB) * GLA_SUB
    if tp != t_:
        h = jnp.pad(h, ((0, 0), (0, tp - t_), (0, 0)))
    tt = min(tp, 256)
    heads = np.arange(GLA_HEADS)
    expand = np.repeat(np.repeat(np.eye(GLA_HEADS, dtype=np.float32), GLA_DK, 0), GLA_DV, 1)
    bdmask = jnp.asarray(expand.T)
    if gla_state is None:
        s0 = jnp.zeros((bsz, GLA_V, GLA_QK), jnp.float32)
    else:
        s0 = jnp.zeros((bsz, GLA_HEADS, GLA_DV, GLA_HEADS, GLA_DK), jnp.float32)
        s0 = s0.at[:, heads, :, heads, :].set(gla_state.transpose(1, 0, 3, 2)).reshape(bsz, GLA_V, GLA_QK)
    tile = lambda width, blk: pl.BlockSpec((1, tt, width), lambda b, i: (b, i, blk))
    fixed2 = lambda shape: pl.BlockSpec(shape, lambda b, i: (0, 0))
    per_b = pl.BlockSpec((1, GLA_V, GLA_QK), lambda b, i: (b, 0, 0))
    o, s_t = pl.pallas_call(
        functools.partial(_gla_body, t_valid=t_),
        grid=(bsz, tp // tt),
        in_specs=[tile(GLA_QK, 0), tile(GLA_QK, 1), tile(GLA_V, 1), tile(GLA_V, 2),
                  tile(LANE, (2 * GLA_QK + 2 * GLA_V + NSA_SIZES[0] + NSA_SIZES[1]) // LANE),
                  fixed2((GLA_RANK, GLA_QK)), fixed2((1, GLA_QK)), fixed2((1, GLA_DV)), per_b,
                  fixed2((GLA_QK, GLA_V)), fixed2((GLA_V, GLA_QK))],
        out_specs=[pl.BlockSpec((1, tt, GLA_V), lambda b, i: (b, i, 0)), per_b],
        out_shape=[jax.ShapeDtypeStruct((bsz, tp, GLA_V), jnp.float32),
                   jax.ShapeDtypeStruct((bsz, GLA_V, GLA_QK), jnp.float32)],
        scratch_shapes=[pltpu.VMEM((GLA_V, GLA_QK), jnp.float32), pltpu.VMEM((tt, GLA_QK), jnp.float32),
                        pltpu.VMEM((tt, GLA_QK), jnp.bfloat16)],
        compiler_params=pltpu.CompilerParams(dimension_semantics=("arbitrary", "arbitrary"),
                                             vmem_limit_bytes=VMEM_LIMIT),
        name="gla",
    )(h, h, h, h, h, w_gla_gate.astype(jnp.bfloat16), b_gla_gate.reshape(1, GLA_QK),
      gla_norm_g.reshape(1, GLA_DV), s0, jnp.asarray(expand, jnp.bfloat16), bdmask)
    s_new = s_t.reshape(bsz, GLA_HEADS, GLA_DV, GLA_HEADS, GLA_DK)[:, heads, :, heads, :]
    return o[:, :t_], s_new.transpose(1, 0, 3, 2)


COL_NQ = 2 * GLA_QK + 2 * GLA_V
COL_NKV = COL_NQ + NSA_SIZES[0]
COL_TAIL = COL_NKV + NSA_SIZES[1]
TAIL_GATE = GLA_RANK
_ORIG = np.cumsum((0,) + GLA_SIZES + NSA_SIZES)
IN_AB_PERM = np.concatenate([np.arange(_ORIG[0], _ORIG[4]), np.arange(_ORIG[5], _ORIG[7]),
                             np.arange(_ORIG[4], _ORIG[5]), np.arange(_ORIG[7], _ORIG[8])])
SUBS = Q_BLK // CMP_STRIDE


def _nsa_prep_body(nq_ref, kv0_ref, kv1_ref, kv2_ref, tail_ref, rc_ref, ru_ref, rd_ref, pool_ref,
                   rows_ref, win_ref, kk_ref, vvt_ref, qr_ref, qo_ref, g_ref, pooled_ref):
    bf16 = jnp.bfloat16
    q0 = pl.program_id(1) * Q_BLK
    kv_w = NSA_KV_HEADS * HEAD_DIM

    def rope(x):
        reps = x.shape[1] // LANE
        wide = lambda r: jnp.concatenate([r[...]] * reps, axis=1) if reps > 1 else r[...]
        half = ROPE_DIM // 2
        return (x * wide(rc_ref) + pltpu.roll(x, half, axis=1) * wide(ru_ref)
                + pltpu.roll(x, x.shape[1] - half, axis=1) * wide(rd_ref))

    kv0, kv1, kv2 = kv0_ref[0], kv1_ref[0], kv2_ref[0]
    k_sel, v_sel = rope(kv1[:, :kv_w]), kv1[:, kv_w:]
    k_win, v_win = rope(kv2[:, :kv_w]), kv2[:, kv_w:]
    rows_ref[0] = jnp.concatenate([kv0, k_sel, v_sel], axis=1)
    win_ref[0] = jnp.concatenate([k_win, v_win], axis=1)
    blk_id = lax.shift_right_logical(q0 + lax.broadcasted_iota(jnp.int32, (Q_BLK, N_SELB), 0),
                                     int(math.log2(SEL_BLK)))
    onehot = jnp.where(lax.broadcasted_iota(jnp.int32, (Q_BLK, N_SELB), 1) == blk_id, 1.0, 0.0).astype(bf16)
    q = nq_ref[0] * (HEAD_DIM ** -0.5)
    q_rot = rope(q)
    gates_t = jax.nn.sigmoid(tail_ref[0]).T
    for h in range(NSA_KV_HEADS):
        hs = slice(h * HEAD_DIM, (h + 1) * HEAD_DIM)
        kk_ref[0, h] = jnp.concatenate([k_sel[:, hs].astype(bf16), k_win[:, hs].astype(bf16), onehot], axis=1)
        vvt_ref[0, h, 0] = jnp.concatenate([v_sel[:, hs], v_win[:, hs]], axis=1).T.astype(bf16)
        gw = NSA_GROUP * HEAD_DIM
        for src, dst in ((q, qr_ref), (q_rot, qo_ref)):
            t = src[:, h * gw:(h + 1) * gw].T
            dst[0, h, 0] = jnp.concatenate([t[g * HEAD_DIM:(g + 1) * HEAD_DIM] for g in range(NSA_GROUP)],
                                           axis=1).astype(bf16)
        base = TAIL_GATE + h * NSA_GROUP * 3
        g_ref[0, h, 0] = jnp.concatenate(
            [jnp.concatenate([gates_t[base + 3 * g + c:base + 3 * g + c + 1] for g in range(NSA_GROUP)], axis=1)
             for c in range(3)], axis=0)
    kc_in, vc_in = kv0[:, :kv_w].astype(bf16), kv0[:, kv_w:].astype(bf16)
    pooled_ref[0] = jnp.concatenate([_dot(pool_ref[0], kc_in), _dot(pool_ref[1], kc_in),
                                     _dot(pool_ref[2], vc_in), _dot(pool_ref[3], vc_in)], axis=1)


def _nsa_prep(h, pos, w_cmp_pool):
    bsz, t_, _ = h.shape
    nqb = t_ // Q_BLK
    bf16 = jnp.bfloat16
    half = ROPE_DIM // 2
    inv_freq = jnp.power(ROPE_THETA, -jnp.arange(half, dtype=jnp.float32) / half)
    ang = pos.astype(jnp.float32)[:, None] * inv_freq
    cos, sin = jnp.cos(ang), jnp.sin(ang)
    rest = HEAD_DIM - ROPE_DIM
    z8, zr = jnp.zeros((t_, half), jnp.float32), jnp.zeros((t_, rest), jnp.float32)
    two = lambda a: jnp.concatenate([a, a], axis=1)
    rc = two(jnp.concatenate([cos, cos, jnp.ones((t_, rest), jnp.float32)], axis=1))
    ru = two(jnp.concatenate([z8, sin, zr], axis=1))
    rd = two(jnp.concatenate([-sin, z8, zr], axis=1))
    pool = _pool_matrices(w_cmp_pool)
    kv_w = NSA_KV_HEADS * HEAD_DIM
    col = lambda width, off: pl.BlockSpec((1, Q_BLK, width), lambda b, i: (b, i, off // width))
    rows_t = pl.BlockSpec((Q_BLK, LANE), lambda b, i: (i, 0))
    head4 = lambda r, c: pl.BlockSpec((1, NSA_KV_HEADS, 1, r, c), lambda b, i: (b, 0, i, 0, 0))
    return pl.pallas_call(
        _nsa_prep_body,
        grid=(bsz, nqb),
        in_specs=[col(NSA_SIZES[0], COL_NQ), col(2 * kv_w, COL_NKV), col(2 * kv_w, COL_NKV + 2 * kv_w),
                  col(2 * kv_w, COL_NKV + 4 * kv_w), col(LANE, COL_TAIL), rows_t, rows_t, rows_t,
                  pl.BlockSpec((4, SUBS, Q_BLK), lambda b, i: (0, 0, 0))],
        out_specs=[pl.BlockSpec((1, Q_BLK, 4 * kv_w), lambda b, i: (b, i, 0)),
                   pl.BlockSpec((1, Q_BLK, 2 * kv_w), lambda b, i: (b, i, 0)),
                   pl.BlockSpec((1, NSA_KV_HEADS, Q_BLK, KK_W), lambda b, i: (b, 0, i, 0)),
                   head4(2 * HEAD_DIM, Q_BLK), head4(HEAD_DIM, NSA_ROWS), head4(HEAD_DIM, NSA_ROWS),
                   head4(3, NSA_ROWS),
                   pl.BlockSpec((1, SUBS, 4 * kv_w), lambda b, i: (b, i, 0))],
        out_shape=[jax.ShapeDtypeStruct((bsz, t_, 4 * kv_w), jnp.float32),
                   jax.ShapeDtypeStruct((bsz, t_, 2 * kv_w), jnp.float32),
                   jax.ShapeDtypeStruct((bsz, NSA_KV_HEADS, t_, KK_W), bf16),
                   jax.ShapeDtypeStruct((bsz, NSA_KV_HEADS, nqb, 2 * HEAD_DIM, Q_BLK), bf16),
                   jax.ShapeDtypeStruct((bsz, NSA_KV_HEADS, nqb, HEAD_DIM, NSA_ROWS), bf16),
                   jax.ShapeDtypeStruct((bsz, NSA_KV_HEADS, nqb, HEAD_DIM, NSA_ROWS), bf16),
                   jax.ShapeDtypeStruct((bsz, NSA_KV_HEADS, nqb, 3, NSA_ROWS), jnp.float32),
                   jax.ShapeDtypeStruct((bsz, t_ // CMP_STRIDE, 4 * kv_w), jnp.float32)],
        compiler_params=pltpu.CompilerParams(dimension_semantics=("arbitrary", "arbitrary")),
        name="nsa_prep",
    )(h, h, h, h, h, rc, ru, rd, pool)


PAGE_GROUP = 16
DEC_KEYS = PAGE_GROUP * PAGE_SIZE
NEW_PAD = 8
KV_W = NSA_KV_HEADS * HEAD_DIM


def _dec_pool_body(pt_ref, *refs):
    page_refs, pool_ref, out_ref = refs[:PAGE_GROUP], refs[PAGE_GROUP], refs[PAGE_GROUP + 1]
    bf16 = jnp.bfloat16
    parts = []
    for pr in page_refs:
        kv0 = pr[0]
        kc_t, vc_t = kv0[:KV_W].astype(bf16), kv0[KV_W:].astype(bf16)
        parts.append(jnp.concatenate([_dot_nt(pool_ref[0], kc_t), _dot_nt(pool_ref[1], kc_t),
                                      _dot_nt(pool_ref[2], vc_t), _dot_nt(pool_ref[3], vc_t)], axis=1))
    out_ref[0] = jnp.concatenate(parts, axis=0)


def _page_specs(n_pages, col_blk):
    def spec(i):
        return pl.BlockSpec((1, 2 * KV_W, PAGE_SIZE),
                            lambda b, j, pt: (pt[b * n_pages + j * PAGE_GROUP + i], col_blk, 0))
    return [spec(i) for i in range(PAGE_GROUP)]


def _dec_pool(cache, page_table, pool):
    bsz, n_pages = page_table.shape
    grid_spec = pltpu.PrefetchScalarGridSpec(
        num_scalar_prefetch=1, grid=(bsz, n_pages // PAGE_GROUP),
        in_specs=_page_specs(n_pages, 0) + [pl.BlockSpec((4, SUBS, Q_BLK), lambda b, j, pt: (0, 0, 0))],
        out_specs=pl.BlockSpec((1, PAGE_GROUP * SUBS, 4 * KV_W), lambda b, j, pt: (b, j, 0)))
    return pl.pallas_call(
        _dec_pool_body, grid_spec=grid_spec,
        out_shape=jax.ShapeDtypeStruct((bsz, n_pages * SUBS, 4 * KV_W), jnp.float32),
        compiler_params=pltpu.CompilerParams(dimension_semantics=("arbitrary", "arbitrary")),
        name="nsa_dec_pool",
    )(page_table.reshape(-1), *([cache] * PAGE_GROUP), pool)


def _dec_select_body(qr_ref, kct_ref, vc_ref, band_ref, oc_ref, selb_ref, *, qpos0, n_q, n_pick, n_blk):
    f32, bf16 = jnp.float32, jnp.bfloat16
    n_cmp = kct_ref.shape[3]
    rows = NSA_GROUP * n_q
    for h in range(NSA_KV_HEADS):
        s_c = _dot(qr_ref[0, h], kct_ref[0, h])
        n_idx = lax.broadcasted_iota(jnp.int32, (rows, n_cmp), 1)
        qpos = qpos0 + (lax.broadcasted_iota(jnp.int32, (rows, n_cmp), 0) % n_q)
        cmask = (n_idx * CMP_STRIDE + (CMP_BLK - 1)) <= qpos
        s_c = jnp.where(cmask, s_c, MASKED)
        p_c = jnp.where(cmask, jnp.exp(s_c - jnp.max(s_c, axis=1, keepdims=True)), 0.0)
        p_c = p_c / jnp.maximum(jnp.sum(p_c, axis=1, keepdims=True), 1e-30)
        oc_ref[0, h] = _dot(p_c.astype(bf16), vc_ref[0, h])
        imp = p_c[0:n_q]
        for g in range(1, NSA_GROUP):
            imp = imp + p_c[g * n_q:(g + 1) * n_q]
        imp_s = jnp.zeros((n_q, N_SELB), f32)
        rem = imp
        for _ in range(3):
            part = rem.astype(bf16)
            imp_s = imp_s + _dot(part, band_ref[...])
            rem = rem - part.astype(f32)
        blk = lax.broadcasted_iota(jnp.int32, (n_q, N_SELB), 1)
        qpos_s = qpos0 + lax.broadcasted_iota(jnp.int32, (n_q, N_SELB), 0)
        cur = lax.shift_right_logical(qpos_s, int(math.log2(SEL_BLK)))
        valid = (blk * SEL_BLK <= qpos_s) & (blk < n_blk)
        forced = (blk == 0) | (blk == cur) | (blk == cur - 1)
        score = jnp.where(valid, imp_s + jnp.where(forced, FORCE_BONUS, 0.0), -1e30)
        picked = jnp.zeros((n_q, N_SELB), f32)
        for _ in range(n_pick):
            best = jnp.max(score, axis=1, keepdims=True)
            first = jnp.min(jnp.where(score == best, blk, N_SELB), axis=1, keepdims=True)
            hit = blk == first
            picked = jnp.where(hit, 1.0, picked)
            score = jnp.where(hit, -3e38, score)
        selb_ref[0, h] = (jnp.where(valid, picked, 0.0) - 1.0) * (-MASKED)


def _dec_select(qr, kct, vc, n_q, qpos0, n_pick, n_blk):
    bsz = qr.shape[0]
    rows = NSA_GROUP * n_q
    n_cmp = kct.shape[3]
    ratio = SEL_BLK // CMP_STRIDE
    c_idx, j_idx = np.arange(n_cmp)[:, None], np.arange(N_SELB)[None, :]
    band = jnp.asarray(((c_idx >= ratio * j_idx - 1) & (c_idx <= ratio * j_idx + ratio - 1)), jnp.bfloat16)
    per_b = lambda *tail: pl.BlockSpec((1, NSA_KV_HEADS) + tail, lambda b: (b, 0, 0, 0))
    return pl.pallas_call(
        functools.partial(_dec_select_body, qpos0=qpos0, n_q=n_q, n_pick=n_pick, n_blk=n_blk),
        grid=(bsz,),
        in_specs=[per_b(rows, HEAD_DIM), per_b(HEAD_DIM, n_cmp), per_b(n_cmp, HEAD_DIM),
                  pl.BlockSpec((n_cmp, N_SELB), lambda b: (0, 0))],
        out_specs=[per_b(rows, HEAD_DIM), per_b(n_q, N_SELB)],
        out_shape=[jax.ShapeDtypeStruct((bsz, NSA_KV_HEADS, rows, HEAD_DIM), jnp.float32),
                   jax.ShapeDtypeStruct((bsz, NSA_KV_HEADS, n_q, N_SELB), jnp.float32)],
        compiler_params=pltpu.CompilerParams(dimension_semantics=("arbitrary",)),
        name="nsa_dec_select",
    )(qr, kct, vc, band)


def _dec_attend_body(pt_ref, *refs, qpos0, n_q, past):
    page_refs = refs[:PAGE_GROUP]
    (qs_ref, qw_ref, knew_ref, vnew_ref, wbuf_ref, wnew_ref, oc_ref, g_ref,
     o_ref, m_ref, l_ref, acc_ref) = refs[PAGE_GROUP:]
    f32, bf16 = jnp.float32, jnp.bfloat16
    j = pl.program_id(1)
    n_rows = qs_ref.shape[1]

    @pl.when(j == 0)
    def _():
        m_ref[...] = jnp.full(m_ref.shape, MASKED, f32)
        l_ref[...] = jnp.zeros(l_ref.shape, f32)
        acc_ref[...] = jnp.zeros(acc_ref.shape, f32)

    def online(s, weigh):
        m_old = m_ref[...]
        m_new = jnp.maximum(m_old, jnp.max(s, axis=1, keepdims=True))
        alpha = jnp.exp(m_old - m_new)
        p = jnp.exp(s - m_new)
        l_ref[...] = alpha * l_ref[...] + jnp.sum(p, axis=1, keepdims=True)
        acc_ref[...] = alpha * acc_ref[...] + weigh(p.astype(bf16))
        m_ref[...] = m_new

    qs = qs_ref[0]
    pages = [pr[0] for pr in page_refs]
    keys_t = jnp.concatenate([p[:KV_W] for p in pages], axis=1).astype(bf16)
    vals_t = jnp.concatenate([p[KV_W:] for p in pages], axis=1).astype(bf16)
    blk_id = j * (DEC_KEYS // SEL_BLK) + lax.shift_right_logical(
        lax.broadcasted_iota(jnp.int32, (N_SELB, DEC_KEYS), 1), int(math.log2(SEL_BLK)))
    onehot_t = jnp.where(lax.broadcasted_iota(jnp.int32, (N_SELB, DEC_KEYS), 0) == blk_id, 1.0, 0.0).astype(bf16)
    online(_dot(qs, jnp.concatenate([keys_t, onehot_t], axis=0)), lambda p: _dot_nt(p, vals_t))

    @pl.when(j == pl.num_programs(1) - 1)
    def _():
        row_q = qpos0 + (lax.broadcasted_iota(jnp.int32, (n_rows, 1), 0) % n_q)
        qh = qw_ref[0]
        new_pos = past + lax.broadcasted_iota(jnp.int32, (n_rows, NEW_PAD), 1)
        new_ok = (new_pos <= row_q) & (new_pos < past + n_q)
        s_new = jnp.where(new_ok, _dot_nt(qh, knew_ref[0]), MASKED)
        online(s_new, lambda p: _dot(p, vnew_ref[0]))
        o_s = acc_ref[...] / l_ref[...]
        wbuf_t = wbuf_ref[0]
        wnew = wnew_ref[0]
        n_buf = wbuf_t.shape[1]
        s_b = _dot(qh, wbuf_t[:KV_W].astype(bf16))
        pos_b = (past - n_buf) + lax.broadcasted_iota(jnp.int32, (n_rows, n_buf), 1)
        s_b = jnp.where((pos_b > row_q - WINDOW) & (pos_b >= 0), s_b, MASKED)
        s_n = jnp.where(new_ok, _dot_nt(qh, wnew[:, :KV_W].astype(bf16)), MASKED)
        m_w = jnp.maximum(jnp.max(s_b, axis=1, keepdims=True), jnp.max(s_n, axis=1, keepdims=True))
        p_b, p_n = jnp.exp(s_b - m_w), jnp.exp(s_n - m_w)
        l_w = jnp.sum(p_b, axis=1, keepdims=True) + jnp.sum(p_n, axis=1, keepdims=True)
        o_w = (_dot_nt(p_b.astype(bf16), wbuf_t[KV_W:].astype(bf16))
               + _dot(p_n.astype(bf16), wnew[:, KV_W:].astype(bf16))) / l_w
        half = n_rows // NSA_KV_HEADS
        own = lambda a: jnp.concatenate([a[h * half:(h + 1) * half, h * HEAD_DIM:(h + 1) * HEAD_DIM]
                                         for h in range(NSA_KV_HEADS)], axis=0)
        g = g_ref[0]
        o_ref[0] = g[:, 0:1] * oc_ref[0] + g[:, 1:2] * own(o_s) + g[:, 2:3] * own(o_w)


def _dec_attend(cache, page_table, qs, qw, knew, vnew, wbuf, wnew, o_c, gates, n_q, qpos0):
    bsz, n_pages = page_table.shape
    n_rows = qs.shape[1]
    per_b = lambda *tail: pl.BlockSpec((1,) + tail, lambda b, j, pt: (b, 0, 0))
    grid_spec = pltpu.PrefetchScalarGridSpec(
        num_scalar_prefetch=1, grid=(bsz, n_pages // PAGE_GROUP),
        in_specs=_page_specs(n_pages, 1) + [
            per_b(n_rows, KV_W + N_SELB), per_b(n_rows, KV_W), per_b(NEW_PAD, KV_W), per_b(NEW_PAD, KV_W),
            per_b(2 * KV_W, wbuf.shape[2]), per_b(NEW_PAD, 2 * KV_W), per_b(n_rows, HEAD_DIM), per_b(n_rows, 3)],
        out_specs=per_b(n_rows, HEAD_DIM),
        scratch_shapes=[pltpu.VMEM((n_rows, 1), jnp.float32), pltpu.VMEM((n_rows, 1), jnp.float32),
                        pltpu.VMEM((n_rows, KV_W), jnp.float32)])
    return pl.pallas_call(
        functools.partial(_dec_attend_body, qpos0=qpos0, n_q=n_q, past=n_pages * PAGE_SIZE),
        grid_spec=grid_spec,
        out_shape=jax.ShapeDtypeStruct((bsz, n_rows, HEAD_DIM), jnp.float32),
        compiler_params=pltpu.CompilerParams(dimension_semantics=("arbitrary", "arbitrary")),
        name="nsa_dec_attend",
    )(page_table.reshape(-1), *([cache] * PAGE_GROUP), qs, qw, knew, vnew, wbuf, wnew, o_c, gates)


def _pool_matrices(w_cmp_pool):
    sub = np.arange(Q_BLK) // CMP_STRIDE == np.arange(SUBS)[:, None]
    w_rep = jnp.tile(w_cmp_pool.reshape(2, 2, CMP_STRIDE), (1, 1, SUBS))
    return jnp.where(sub[None, None], w_rep[:, :, None, :], 0.0).reshape(4, SUBS, Q_BLK).astype(jnp.bfloat16)


def _compressed_from_pooled(pooled):
    bsz, n_sub, _ = pooled.shape
    pooled = pooled.reshape(bsz, n_sub, 4, NSA_KV_HEADS, HEAD_DIM)
    kc = pooled[:, :-1, 0] + pooled[:, 1:, 1]
    vc = pooled[:, :-1, 2] + pooled[:, 1:, 3]
    pad = lambda a: jnp.pad(a, ((0, 0), (0, 1), (0, 0), (0, 0))).transpose(0, 2, 1, 3)
    return pad(kc), pad(vc)


def _nsa_decode(q_raw, q_rot, gates, rows_full, rows_win, cache, page_table, win_buf, w_cmp_pool, past):
    bsz, n_q = q_raw.shape[:2]
    bf16 = jnp.bfloat16
    n_blk = past // SEL_BLK
    assert past % DEC_KEYS == 0 and n_blk <= N_SELB and n_q <= NEW_PAD
    scale = HEAD_DIM ** -0.5
    cache2 = cache.transpose(0, 2, 3, 4, 1).reshape(cache.shape[0], 4 * KV_W, PAGE_SIZE)
    pooled = _dec_pool(cache2, page_table, _pool_matrices(w_cmp_pool))
    kc_p, vc_p = _compressed_from_pooled(pooled)
    rows_of = lambda a: a.transpose(0, 2, 3, 1, 4).reshape(bsz, NSA_KV_HEADS, NSA_GROUP * n_q, a.shape[-1])
    qr = rows_of((q_raw * scale).astype(bf16))
    n_pick = min(SEL_TOPN, n_blk + 1) - 1
    o_c, selb = _dec_select(qr, kc_p.transpose(0, 1, 3, 2).astype(bf16), vc_p.astype(bf16), n_q, past, n_pick, n_blk)
    qo = rows_of((q_rot * scale).astype(bf16))
    zero = jnp.zeros_like(qo[:, 0])
    qw = jnp.concatenate([jnp.concatenate([qo[:, 0], zero], -1), jnp.concatenate([zero, qo[:, 1]], -1)], axis=1)
    bias = jnp.tile(selb, (1, 1, NSA_GROUP, 1)).reshape(bsz, -1, N_SELB).astype(bf16)
    qs = jnp.concatenate([qw, bias], axis=-1)
    pad_new = lambda a: jnp.pad(a.reshape(bsz, n_q, -1), ((0, 0), (0, NEW_PAD - n_q), (0, 0)))
    knew = pad_new(rows_full[:, :, 2]).astype(bf16)
    vnew = pad_new(rows_full[:, :, 3]).astype(bf16)
    wnew = pad_new(rows_win)
    wbuf = win_buf.transpose(0, 2, 3, 4, 1).reshape(bsz, 2 * KV_W, win_buf.shape[1])
    gt = rows_of(gates).reshape(bsz, -1, 3)
    o = _dec_attend(cache2, page_table, qs, qw, knew, vnew, wbuf, wnew,
                    o_c.reshape(bsz, -1, HEAD_DIM), gt, n_q, past)
    o = o.reshape(bsz, NSA_KV_HEADS, NSA_GROUP, n_q, HEAD_DIM).transpose(0, 3, 1, 2, 4)
    return o.reshape(bsz, n_q, NSA_HEADS * HEAD_DIM)


def _ab_mixer(x, pos, w_in, w_gla_gate, b_gla_gate, gla_norm_g, w_cmp_pool, w_out,
              gla_state, nsa_cache, page_table, win_buf):
    bsz, t_, _ = x.shape
    h_in = _mm(x.reshape(bsz * t_, -1), w_in[:, IN_AB_PERM], keep_pad=True).reshape(bsz, t_, -1)
    o_a, s_a = _gla(h_in, w_gla_gate, b_gla_gate, gla_norm_g, gla_state)
    kv_w = NSA_KV_HEADS * HEAD_DIM
    if nsa_cache is None:
        rows2, win2, kk, vvt, qr, qo, gt, pooled = _nsa_prep(h_in, pos, w_cmp_pool)
        pooled = pooled.reshape(bsz, t_ // CMP_STRIDE, 4, NSA_KV_HEADS, HEAD_DIM)
        kc = pooled[:, :-1, 0] + pooled[:, 1:, 1]
        vc = pooled[:, :-1, 2] + pooled[:, 1:, 3]
        kc_p = jnp.pad(kc, ((0, 0), (0, 1), (0, 0), (0, 0))).transpose(0, 2, 1, 3).astype(jnp.bfloat16)
        vct = jnp.pad(vc, ((0, 0), (0, 1), (0, 0), (0, 0))).transpose(0, 2, 3, 1).astype(jnp.bfloat16)
        o_b = _nsa_prompt(qr, qo, gt, kc_p, vct, kk, vvt)
        rows_full = rows2.reshape(bsz, t_, 4, NSA_KV_HEADS, HEAD_DIM)
        new_win = win2[:, -min(WINDOW, t_):].reshape(bsz, -1, 2, NSA_KV_HEADS, HEAD_DIM)
    else:
        nq = h_in[..., COL_NQ:COL_NKV]
        nkv = h_in[..., COL_NKV:COL_TAIL]
        ngate = h_in[..., COL_TAIL + TAIL_GATE:COL_TAIL + TAIL_GATE + NSA_SIZES[2]]
        q_raw = nq.reshape(bsz, t_, NSA_KV_HEADS, NSA_GROUP, HEAD_DIM)
        q_rot = _partial_rope(q_raw, pos)
        kv = nkv.reshape(bsz, t_, 6, NSA_KV_HEADS, HEAD_DIM)
        k_sel = _partial_rope(kv[:, :, 2], pos)
        k_win = _partial_rope(kv[:, :, 4], pos)
        rows_full = jnp.stack([kv[:, :, 0], kv[:, :, 1], k_sel, kv[:, :, 3]], axis=2)
        rows_win = jnp.stack([k_win, kv[:, :, 5]], axis=2)
        gates = jax.nn.sigmoid(ngate).reshape(bsz, t_, NSA_KV_HEADS, NSA_GROUP, 3)
        past_len = page_table.shape[1] * PAGE_SIZE
        o_b = _nsa_decode(q_raw, q_rot, gates, rows_full, rows_win, nsa_cache, page_table, win_buf,
                          w_cmp_pool, past_len)
        w_buf = win_buf.shape[1]
        kw = jnp.concatenate([win_buf, rows_win], axis=1)
        new_win = kw[:, -w_buf:]
    y = _mm_pair(o_a.reshape(bsz * t_, -1), o_b.reshape(bsz * t_, -1), w_out).reshape(bsz, t_, -1)
    return y, s_a, rows_full, new_win


CONV_HALO = 32
CONV_LEAD = CONV_HALO - (CONV_W - 1)


def _conv_body(x_ref, buf0_ref, w1_ref, b1_ref, wdw_ref, bdw_ref, g_ref, b_ref, w2_ref, b2_ref,
               o_ref, tail_ref, ext_ref, *, t_last):
    bf16 = jnp.bfloat16
    tt = x_ref.shape[1]
    i = pl.program_id(1)

    @pl.when(i == 0)
    def _():
        ext_ref[0:CONV_HALO, :] = buf0_ref[0]

    h = _dot(x_ref[0].astype(bf16), w1_ref[...]) + b1_ref[...]
    ext_ref[CONV_HALO:CONV_HALO + tt, :] = h[:, :D_CONV] * jax.nn.sigmoid(h[:, D_CONV:])
    c = jnp.zeros((tt, D_CONV), jnp.float32) + bdw_ref[...]
    for k in range(CONV_W):
        c = c + ext_ref[pl.ds(CONV_LEAD + k, tt), :] * wdw_ref[k:k + 1, :]
    c = _ln_rows(c, g_ref[...], b_ref[...])
    c = c * jax.nn.sigmoid(c)
    o_ref[0] = _dot(c.astype(bf16), w2_ref[...]) + b2_ref[...]
    tail_ref[0] = ext_ref[t_last:t_last + CONV_HALO, :]
    ext_ref[0:CONV_HALO, :] = ext_ref[tt:tt + CONV_HALO, :]


def _conv_module(x, conv_buf, w_pw1, b_pw1, w_dw, b_dw, ln_g, ln_b, w_pw2, b_pw2):
    bsz, t_, d = x.shape
    bf16 = jnp.bfloat16
    tp = -(-t_ // 8) * 8
    tt = min(tp, 256)
    n_t = tp // tt
    if tp != t_:
        x = jnp.pad(x, ((0, 0), (0, tp - t_), (0, 0)))
    if conv_buf is None:
        buf0 = jnp.zeros((bsz, CONV_HALO, D_CONV), jnp.float32)
    else:
        buf0 = jnp.pad(conv_buf, ((0, 0), (CONV_LEAD, 0), (0, 0)))
    fixed = lambda shape: pl.BlockSpec(shape, lambda b, i: (0,) * len(shape))
    per_b = pl.BlockSpec((1, CONV_HALO, D_CONV), lambda b, i: (b, 0, 0))
    out, tail = pl.pallas_call(
        functools.partial(_conv_body, t_last=t_ - (n_t - 1) * tt),
        grid=(bsz, n_t),
        in_specs=[pl.BlockSpec((1, tt, d), lambda b, i: (b, i, 0)), per_b,
                  fixed((d, 2 * D_CONV)), fixed((1, 2 * D_CONV)), fixed((CONV_HALO, D_CONV)), fixed((1, D_CONV)),
                  fixed((1, D_CONV)), fixed((1, D_CONV)), fixed((D_CONV, d)), fixed((1, d))],
        out_specs=[pl.BlockSpec((1, tt, d), lambda b, i: (b, i, 0)), per_b],
        out_shape=[jax.ShapeDtypeStruct((bsz, tp, d), jnp.float32),
                   jax.ShapeDtypeStruct((bsz, CONV_HALO, D_CONV), jnp.float32)],
        scratch_shapes=[pltpu.VMEM((CONV_HALO + tt, D_CONV), jnp.float32)],
        compiler_params=pltpu.CompilerParams(dimension_semantics=("arbitrary", "arbitrary"),
                                             vmem_limit_bytes=VMEM_LIMIT),
        name="conv_module",
    )(x, buf0, w_pw1.astype(bf16), b_pw1.reshape(1, -1), jnp.pad(w_dw, ((0, CONV_HALO - CONV_W), (0, 0))),
      b_dw.reshape(1, -1), ln_g.reshape(1, -1), ln_b.reshape(1, -1), w_pw2.astype(bf16), b_pw2.reshape(1, -1))
    return out[:, :t_], tail[:, CONV_LEAD:]


PACK_W = 256
SC_WINDOW = 128
SC_TILES = 32


def _pack_rows(y):
    out = []
    for h in range(2):
        lo = lax.bitcast_convert_type(y[:, 2 * h * PACK_W:(2 * h + 1) * PACK_W].astype(jnp.bfloat16)
                                      .astype(jnp.float32), jnp.uint32)
        hi = lax.bitcast_convert_type(y[:, (2 * h + 1) * PACK_W:(2 * h + 2) * PACK_W].astype(jnp.bfloat16)
                                      .astype(jnp.float32), jnp.uint32)
        out.append(lax.bitcast_convert_type((lo >> 16) | hi, jnp.int32))
    return out


def _unpack_words(w):
    u = lax.bitcast_convert_type(w, jnp.uint32)
    lo = lax.bitcast_convert_type(u << 16, jnp.float32)
    hi = lax.bitcast_convert_type(u & jnp.uint32(0xFFFF0000), jnp.float32)
    return lo, hi


def _gather_rows(src, idx):
    n = idx.shape[0]
    if n % (SC_WINDOW * SC_TILES) != 0:
        return jnp.take(src, idx, axis=0)
    mesh = plsc.VectorSubcoreMesh(core_axis_name="core", subcore_axis_name="subcore")

    @pl.kernel(out_type=jax.ShapeDtypeStruct((n, src.shape[1]), src.dtype), mesh=mesh)
    def gather_kernel(src_hbm, idx_hbm, out_hbm):
        def step(idx_vmem, out_vmem):
            pltpu.sync_copy(src_hbm.at[idx_vmem.at[0]], out_vmem)

        pltpu.emit_pipeline(
            step, grid=(n // SC_WINDOW,),
            in_specs=[pl.BlockSpec((1, SC_WINDOW), index_map=lambda i: (0, i))],
            out_specs=[pl.BlockSpec((SC_WINDOW, src.shape[1]), index_map=lambda i: (i, 0))],
            core_axis_name=("core", "subcore"),
            dimension_semantics=(pltpu.PARALLEL,),
        )(idx_hbm, out_hbm)

    return gather_kernel(src, idx.reshape(1, n))


def _scatter_rows(src, idx, n_out):
    n = idx.shape[0]
    m = src.shape[0] // 2
    reps = n // (2 * m)
    if n % (SC_WINDOW * SC_TILES) != 0 or m % SC_WINDOW != 0:
        rows = jnp.arange(n, dtype=jnp.int32)
        src_row = (rows // (reps * m)) * m + rows % m
        return jnp.zeros((n_out, src.shape[1]), src.dtype).at[idx].set(jnp.take(src, src_row, axis=0))
    tiles = m // SC_WINDOW
    mesh = plsc.VectorSubcoreMesh(core_axis_name="core", subcore_axis_name="subcore")

    @pl.kernel(out_type=jax.ShapeDtypeStruct((n_out, src.shape[1]), src.dtype), mesh=mesh, scratch_types=[])
    def scatter_kernel(src_hbm, idx_hbm, out_hbm):
        def step(src_vmem, idx_vmem):
            pltpu.sync_copy(src_vmem, out_hbm.at[idx_vmem.at[0]])

        pltpu.emit_pipeline(
            step, grid=(n // SC_WINDOW,),
            in_specs=[pl.BlockSpec((SC_WINDOW, src.shape[1]),
                                   index_map=lambda i: ((i // (reps * tiles)) * tiles + i % tiles, 0)),
                      pl.BlockSpec((1, SC_WINDOW), index_map=lambda i: (0, i))],
            out_specs=[],
            core_axis_name=("core", "subcore"),
            dimension_semantics=(pltpu.PARALLEL,),
        )(src_hbm, idx_hbm)

    return scatter_kernel(src, idx.reshape(1, n))


PER_GROUP = N_EXPERTS // N_GROUPS
PICKED = -3e38


def _ln_rows(v, g, b):
    mu = jnp.mean(v, axis=-1, keepdims=True)
    c = v - mu
    var = jnp.mean(c * c, axis=-1, keepdims=True)
    return c * lax.rsqrt(var + LN_EPS) * g + b


def _first_max(v, ids, axes, sentinel):
    best = v
    for a in axes:
        best = jnp.max(best, axis=a, keepdims=True)
    first = jnp.where(v == best, ids, sentinel)
    for a in axes:
        first = jnp.min(first, axis=a, keepdims=True)
    return best, first


def _sum_axes(v, axes):
    for a in axes:
        v = jnp.sum(v, axis=a, keepdims=True)
    return v


def _moe_pre_body(x_ref, mix_ref, g_ref, b_ref, wr_ref, br_ref, wgu_ref, wdn_ref,
                  x1_ref, xp_ref, sh_ref, eidx_ref, gate_ref, rank_ref, cnt_ref, run_ref):
    f32, bf16 = jnp.float32, jnp.bfloat16
    tm = x_ref.shape[0]

    @pl.when(pl.program_id(0) == 0)
    def _():
        run_ref[...] = jnp.zeros(run_ref.shape, f32)

    x1 = _ln_rows(ALPHA * x_ref[...] + mix_ref[...], g_ref[...], b_ref[...])
    x1_ref[...] = x1
    x1b = x1.astype(bf16)
    xp_ref[0], xp_ref[1] = _pack_rows(x1)

    h = _dot(x1b, wgu_ref[...])
    d_sh = h.shape[1] // 2
    act = (jax.nn.silu(h[:, :d_sh]) * h[:, d_sh:]).astype(bf16)
    sh_ref[...] = _dot(act, wdn_ref[...])

    s = jax.nn.sigmoid(_dot_nt(wr_ref[...], x1b)).reshape(N_GROUPS, PER_GROUP, tm)
    sb = s + br_ref[...].reshape(N_GROUPS, PER_GROUP, 1)
    shape3 = (N_GROUPS, PER_GROUP, tm)
    pid = lax.broadcasted_iota(jnp.int32, shape3, 1)
    gid = lax.broadcasted_iota(jnp.int32, (N_GROUPS, 1, tm), 0)
    eid = lax.broadcasted_iota(jnp.int32, shape3, 0) * PER_GROUP + pid
    top1, i1 = _first_max(sb, pid, (1,), PER_GROUP)
    top2 = jnp.max(jnp.where(pid == i1, PICKED, sb), axis=1, keepdims=True)
    gscore = top1 + top2
    gsel = jnp.zeros((N_GROUPS, 1, tm), f32)
    for _ in range(TOPK_GROUPS):
        _, first = _first_max(gscore, gid, (0,), N_GROUPS)
        hit = gid == first
        gsel = jnp.where(hit, 1.0, gsel)
        gscore = jnp.where(hit, PICKED, gscore)
    cand = jnp.where(gsel > 0.0, sb, -1e30)
    firsts, gates = [], []
    picked = jnp.zeros(shape3, f32)
    for _ in range(TOP_K):
        _, first = _first_max(cand, eid, (0, 1), N_EXPERTS)
        hit = eid == first
        firsts.append(first)
        gates.append(_sum_axes(jnp.where(hit, s, 0.0), (0, 1)))
        picked = jnp.where(hit, 1.0, picked)
        cand = jnp.where(hit, PICKED, cand)
    gsum = gates[0]
    for gk in gates[1:]:
        gsum = gsum + gk
    earlier = (lax.broadcasted_iota(jnp.int32, (tm, tm), 0) < lax.broadcasted_iota(jnp.int32, (tm, tm), 1))
    picked2 = picked.reshape(N_EXPERTS, tm)
    rank = run_ref[...] + _dot(picked2.astype(bf16), jnp.where(earlier, 1.0, 0.0).astype(bf16))
    run_new = run_ref[...] + jnp.sum(picked2, axis=1, keepdims=True)
    run_ref[...] = run_new
    cnt_ref[...] = jnp.broadcast_to(run_new, cnt_ref.shape)
    rank3 = rank.reshape(shape3)
    for k in range(TOP_K):
        hit = eid == firsts[k]
        eidx_ref[k:k + 1, :] = firsts[k].reshape(1, tm)
        gate_ref[k:k + 1, :] = (gates[k] / gsum * ROUTE_SCALE).reshape(1, tm)
        rank_ref[k:k + 1, :] = _sum_axes(jnp.where(hit, rank3, 0.0), (0, 1)).reshape(1, tm).astype(jnp.int32)


def _moe_pre(x, mix, g, b, w_router, b_router, w_sh_gu, w_sh_down):
    m, d = x.shape
    bf16 = jnp.bfloat16
    tm = min(m, 512)
    row = lambda i: (i, 0)
    col = lambda i: (0, i)
    fixed = lambda i: (0, 0)
    d_sh2 = w_sh_gu.shape[1]
    return pl.pallas_call(
        _moe_pre_body,
        grid=(m // tm,),
        in_specs=[pl.BlockSpec((tm, d), row), pl.BlockSpec((tm, d), row),
                  pl.BlockSpec((1, d), fixed), pl.BlockSpec((1, d), fixed),
                  pl.BlockSpec((N_EXPERTS, d), fixed), pl.BlockSpec((N_EXPERTS, 1), fixed),
                  pl.BlockSpec((d, d_sh2), fixed), pl.BlockSpec((d_sh2 // 2, d), fixed)],
        out_specs=[pl.BlockSpec((tm, d), row), pl.BlockSpec((2, tm, PACK_W), lambda i: (0, i, 0)),
                   pl.BlockSpec((tm, d), row),
                   pl.BlockSpec((TOP_K, tm), col), pl.BlockSpec((TOP_K, tm), col), pl.BlockSpec((TOP_K, tm), col),
                   pl.BlockSpec((N_EXPERTS, LANE), fixed)],
        out_shape=[jax.ShapeDtypeStruct((m, d), jnp.float32), jax.ShapeDtypeStruct((2, m, PACK_W), jnp.int32),
                   jax.ShapeDtypeStruct((m, d), jnp.float32),
                   jax.ShapeDtypeStruct((TOP_K, m), jnp.int32), jax.ShapeDtypeStruct((TOP_K, m), jnp.float32),
                   jax.ShapeDtypeStruct((TOP_K, m), jnp.int32),
                   jax.ShapeDtypeStruct((N_EXPERTS, LANE), jnp.float32)],
        scratch_shapes=[pltpu.VMEM((N_EXPERTS, 1), jnp.float32)],
        compiler_params=pltpu.CompilerParams(dimension_semantics=("arbitrary",),
                                             vmem_limit_bytes=VMEM_LIMIT),
        name="moe_pre",
    )(x, mix, g.reshape(1, d), b.reshape(1, d), w_router.T.astype(bf16), b_router.reshape(N_EXPERTS, 1),
      w_sh_gu.astype(bf16), w_sh_down.astype(bf16))


def _moe_expert_body(exp_ref, first_ref, rows_ref, xs_ref, wgu_ref, wdn_ref, y_ref, wgu_bf, wdn_bf):
    i = pl.program_id(0)
    bf16 = jnp.bfloat16

    @pl.when(first_ref[i] == 1)
    def _():
        wgu_bf[...] = wgu_ref[0, 0].astype(bf16)
        wdn_bf[...] = wdn_ref[0, 0].astype(bf16)

    @pl.when(rows_ref[i] > 0)
    def _():
        live = lax.broadcasted_iota(jnp.int32, (xs_ref.shape[1], 1), 0) < rows_ref[i]
        h = None
        for hw in range(2):
            for q, xq in enumerate(_unpack_words(xs_ref[hw])):
                r0 = (2 * hw + q) * PACK_W
                part = _dot(jnp.where(live, xq, 0.0).astype(bf16), wgu_bf[r0:r0 + PACK_W, :])
                h = part if h is None else h + part
        d_e = h.shape[1] // 2
        act = (jax.nn.silu(h[:, :d_e]) * h[:, d_e:]).astype(bf16)
        y_ref[0], y_ref[1] = _pack_rows(_dot(act, wdn_bf[...]))

    @pl.when(rows_ref[i] == 0)
    def _():
        y_ref[...] = jnp.zeros(y_ref.shape, y_ref.dtype)


def _moe_experts(xs, blk_exp, blk_first, blk_rows, w_exp_gu, w_exp_down, layer, bm):
    n_slots = xs.shape[1]
    d = w_exp_gu.shape[2]
    n_blk = n_slots // bm
    d_e2 = w_exp_gu.shape[3]
    words = lambda i, e, f, a: (0, i, 0)
    grid_spec = pltpu.PrefetchScalarGridSpec(
        num_scalar_prefetch=3,
        grid=(n_blk,),
        in_specs=[pl.BlockSpec((2, bm, PACK_W), words),
                  pl.BlockSpec((1, 1, d, d_e2), lambda i, e, f, a: (layer, e[i], 0, 0)),
                  pl.BlockSpec((1, 1, d_e2 // 2, d), lambda i, e, f, a: (layer, e[i], 0, 0))],
        out_specs=pl.BlockSpec((2, bm, PACK_W), words),
        scratch_shapes=[pltpu.VMEM((d, d_e2), jnp.bfloat16), pltpu.VMEM((d_e2 // 2, d), jnp.bfloat16)])
    return pl.pallas_call(
        _moe_expert_body,
        grid_spec=grid_spec,
        out_shape=jax.ShapeDtypeStruct((2, n_slots, PACK_W), jnp.int32),
        compiler_params=pltpu.CompilerParams(dimension_semantics=("arbitrary",),
                                             vmem_limit_bytes=VMEM_LIMIT),
        name="moe_experts",
    )(blk_exp, blk_first, blk_rows, xs, w_exp_gu, w_exp_down)


def _combine_ln_body(x_ref, yg_ref, gt_ref, sh_ref, g_ref, b_ref, o_ref):
    gt = gt_ref[...]
    parts = []
    for hw in range(2):
        lo_acc = hi_acc = None
        for k in range(TOP_K):
            lo, hi = _unpack_words(yg_ref[hw, k])
            gk = gt[:, k:k + 1]
            lo_acc = lo * gk if lo_acc is None else lo_acc + lo * gk
            hi_acc = hi * gk if hi_acc is None else hi_acc + hi * gk
        parts += [lo_acc, hi_acc]
    routed = jnp.concatenate(parts, axis=1)
    o_ref[...] = _ln_rows(ALPHA * x_ref[...] + (routed + sh_ref[...]), g_ref[...], b_ref[...])


def _combine_ln(x, yg, gate_t, shared, g, b):
    m, d = x.shape
    tm = min(m, 256)
    row = lambda i: (i, 0)
    fixed = lambda i: (0, 0)
    return pl.pallas_call(
        _combine_ln_body,
        grid=(m // tm,),
        in_specs=[pl.BlockSpec((tm, d), row), pl.BlockSpec((2, TOP_K, tm, PACK_W), lambda i: (0, 0, i, 0)),
                  pl.BlockSpec((tm, TOP_K), row), pl.BlockSpec((tm, d), row),
                  pl.BlockSpec((1, d), fixed), pl.BlockSpec((1, d), fixed)],
        out_specs=pl.BlockSpec((tm, d), row),
        out_shape=jax.ShapeDtypeStruct((m, d), jnp.float32),
        compiler_params=pltpu.CompilerParams(dimension_semantics=("arbitrary",)),
        name="combine_ln",
    )(x, yg, gate_t, shared, g.reshape(1, d), b.reshape(1, d))


def _moe_layer(x, mix, ln1_g, ln1_b, ln2_g, ln2_b, w_router, b_router, w_exp_gu, w_exp_down, layer,
               w_sh_gu, w_sh_down):
    m, d = x.shape
    x1, xp, shared, eidx, gate8, rank8, counts = _moe_pre(x, mix, ln1_g, ln1_b, w_router, b_router,
                                                           w_sh_gu, w_sh_down)
    bm = 512 if m * TOP_K >= 512 * N_EXPERTS else MOE_BLK
    n_blk = (m * TOP_K) // bm + N_EXPERTS
    counts = counts[:, 0].astype(jnp.int32)
    padded = (counts + bm - 1) // bm * bm
    pad_end = jnp.cumsum(padded)
    pad_start = pad_end - padded
    start_of = jnp.sum(jnp.where(eidx[:, :, None] == jnp.arange(N_EXPERTS), pad_start, 0), axis=-1)
    dest = (start_of + rank8).reshape(-1)
    blk_start = jnp.arange(n_blk, dtype=jnp.int32) * bm
    blk_exp = jnp.minimum(jnp.sum(pad_end[None, :] <= blk_start[:, None], axis=1), N_EXPERTS - 1).astype(jnp.int32)
    blk_rows = jnp.clip(counts[blk_exp] - (blk_start - pad_start[blk_exp]), 0, bm).astype(jnp.int32)
    blk_first = jnp.concatenate([jnp.ones((1,), jnp.int32), (blk_exp[1:] != blk_exp[:-1]).astype(jnp.int32)])
    n_slots = n_blk * bm
    xs = _scatter_rows(xp.reshape(2 * m, PACK_W), jnp.concatenate([dest, dest + n_slots]), 2 * n_slots)
    y = _moe_experts(xs.reshape(2, n_slots, PACK_W), blk_exp, blk_first, blk_rows, w_exp_gu, w_exp_down, layer, bm)
    yg = _gather_rows(y.reshape(2 * n_slots, PACK_W), jnp.concatenate([dest, dest + n_slots]))
    return _combine_ln(x1, yg.reshape(2, TOP_K, m, PACK_W), gate8.T, shared, ln2_g, ln2_b)


def _trunk(x, pos, gla_state, nsa_cache, page_table, win_buf, conv_buf,
           w_in_ab, w_gla_gate, b_gla_gate, gla_norm_g, w_cmp_pool, w_out_ab,
           w_pw1, b_pw1, w_dw, b_dw, conv_ln_g, conv_ln_b, w_pw2, b_pw2,
           ln_g, ln_b, w_router, b_router, w_exp_gu, w_exp_down, w_sh_gu, w_sh_down):
    new_gla, new_rows, new_win, new_conv = [], [], [], []
    for layer in range(DEPTH):
        i = layer // 2
        if layer % 2 == 0:
            mix, s_a, rows, win = _ab_mixer(
                x, pos, w_in_ab[i], w_gla_gate[i], b_gla_gate[i], gla_norm_g[i], w_cmp_pool[i], w_out_ab[i],
                None if gla_state is None else gla_state[i],
                None if nsa_cache is None else nsa_cache[i], page_table,
                None if win_buf is None else win_buf[i])
            new_gla.append(s_a)
            new_rows.append(rows)
            new_win.append(win)
        else:
            mix, cb = _conv_module(x, None if conv_buf is None else conv_buf[i], w_pw1[i], b_pw1[i],
                                   w_dw[i], b_dw[i], conv_ln_g[i], conv_ln_b[i], w_pw2[i], b_pw2[i])
            new_conv.append(cb)
        bsz, t_, d = x.shape
        x = _moe_layer(x.reshape(-1, d), mix.reshape(-1, d), ln_g[layer, 0], ln_b[layer, 0],
                       ln_g[layer, 1], ln_b[layer, 1], w_router[layer], b_router[layer],
                       w_exp_gu, w_exp_down, layer, w_sh_gu[layer], w_sh_down[layer]).reshape(bsz, t_, d)
    return x, jnp.stack(new_gla), jnp.stack(new_rows), jnp.stack(new_win), jnp.stack(new_conv)


def kernel(x_prompt, x_sample, state_gla, cache_nsa_kv, state_nsa_win, state_conv, page_table,
           w_in_ab, w_gla_gate, b_gla_gate, gla_norm_g, w_cmp_pool, w_out_ab,
           w_pw1, b_pw1, w_dw, b_dw, conv_ln_g, conv_ln_b, w_pw2, b_pw2,
           ln_g, ln_b, w_router, b_router, w_exp_gu, w_exp_down, w_sh_gu, w_sh_down):
    weights = (w_in_ab, w_gla_gate, b_gla_gate, gla_norm_g, w_cmp_pool, w_out_ab,
               w_pw1, b_pw1, w_dw, b_dw, conv_ln_g, conv_ln_b, w_pw2, b_pw2,
               ln_g, ln_b, w_router, b_router, w_exp_gu, w_exp_down, w_sh_gu, w_sh_down)
    past_len = page_table.shape[1] * PAGE_SIZE
    pos_p = jnp.arange(x_prompt.shape[1])
    pos_s = past_len + jnp.arange(x_sample.shape[1])
    y_prompt, gla_p, rows_p, win_p, conv_p = _trunk(x_prompt, pos_p, None, None, None, None, None, *weights)
    y_sample, gla_s, rows_s, win_s, conv_s = _trunk(x_sample, pos_s, state_gla, cache_nsa_kv, page_table,
                                                    state_nsa_win, state_conv, *weights)
    return (y_prompt, y_sample, gla_p, gla_s, rows_p, rows_s, win_p, win_s, conv_p, conv_s)
```

```python
import functools
import math

import jax
import jax.numpy as jnp
import numpy as np
from jax import lax
from jax.experimental import pallas as pl
from jax.experimental.pallas import tpu as pltpu
from jax.experimental.pallas import tpu_sc as plsc

D_MODEL = 1024
DEPTH = 2
PAGE_SIZE = 128

GLA_HEADS = 4
GLA_DV = D_MODEL // 2 // GLA_HEADS
GLA_DK = GLA_DV // 2
GLA_RANK = 16
GLA_TAU = 16.0

NSA_HEADS = 8
NSA_KV_HEADS = 2
NSA_GROUP = NSA_HEADS // NSA_KV_HEADS
HEAD_DIM = D_MODEL // 2 // NSA_HEADS
CMP_BLK = 32
CMP_STRIDE = 16
SEL_BLK = 64
SEL_TOPN = 16
WINDOW = 512
Q_BLK = 128
FORCE_BONUS = 100.0
ROPE_DIM = HEAD_DIM // 4
ROPE_THETA = 500000.0

GLA_SIZES = (GLA_HEADS * GLA_DK, GLA_HEADS * GLA_DK, GLA_HEADS * GLA_DV, GLA_HEADS * GLA_DV, GLA_RANK)
NSA_SIZES = (NSA_HEADS * HEAD_DIM, 6 * NSA_KV_HEADS * HEAD_DIM, 3 * NSA_HEADS)

CONV_W = 31
D_CONV = D_MODEL

N_EXPERTS = 64
N_GROUPS = 8
TOPK_GROUPS = 4
TOP_K = 8
D_EXPERT = 256
ROUTE_SCALE = 2.5
MOE_BLK = 128

ALPHA = (2 * DEPTH) ** 0.25
LN_EPS = 1e-5

LANE = 128
SUBLANES = 8
V7X_VMEM_BYTES = 64 * 1024 * 1024
VMEM_LIMIT = V7X_VMEM_BYTES * 3 // 4


def _dot(a, b):
    return jnp.dot(a, b, preferred_element_type=jnp.float32)


def _dot_nt(a, b):
    return lax.dot_general(a, b, (((1,), (1,)), ((), ())), preferred_element_type=jnp.float32)


def _mm_body(x_ref, w_ref, o_ref):
    o_ref[...] = _dot(x_ref[...].astype(jnp.bfloat16), w_ref[...].astype(jnp.bfloat16))


def _mm(x, w, keep_pad=False):
    m, k = x.shape
    n = w.shape[1]
    n_pad = -(-n // LANE) * LANE
    w = w.astype(jnp.bfloat16)
    if n_pad != n:
        w = jnp.pad(w, ((0, 0), (0, n_pad - n)))
    tm = min(m, 512)
    out = pl.pallas_call(
        _mm_body,
        grid=(m // tm,),
        in_specs=[pl.BlockSpec((tm, k), lambda i: (i, 0)),
                  pl.BlockSpec((k, n_pad), lambda i: (0, 0))],
        out_specs=pl.BlockSpec((tm, n_pad), lambda i: (i, 0)),
        out_shape=jax.ShapeDtypeStruct((m, n_pad), jnp.float32),
        compiler_params=pltpu.CompilerParams(dimension_semantics=("arbitrary",),
                                             vmem_limit_bytes=VMEM_LIMIT),
        name="mm",
    )(x, w)
    return out if keep_pad or n_pad == n else out[:, :n]


def _mm_pair_body(a_ref, b_ref, w_ref, o_ref):
    ka = a_ref.shape[1]
    o_ref[...] = (_dot(a_ref[...].astype(jnp.bfloat16), w_ref[0:ka, :])
                  + _dot(b_ref[...].astype(jnp.bfloat16), w_ref[ka:, :]))


def _mm_pair(a, b, w):
    m, ka = a.shape
    kb = b.shape[1]
    n = w.shape[1]
    tm = min(m, 512)
    return pl.pallas_call(
        _mm_pair_body,
        grid=(m // tm,),
        in_specs=[pl.BlockSpec((tm, ka), lambda i: (i, 0)), pl.BlockSpec((tm, kb), lambda i: (i, 0)),
                  pl.BlockSpec((ka + kb, n), lambda i: (0, 0))],
        out_specs=pl.BlockSpec((tm, n), lambda i: (i, 0)),
        out_shape=jax.ShapeDtypeStruct((m, n), jnp.float32),
        compiler_params=pltpu.CompilerParams(dimension_semantics=("arbitrary",)),
        name="mm_pair",
    )(a, b, w.astype(jnp.bfloat16))


def _partial_rope(x, pos):
    half = ROPE_DIM // 2
    inv_freq = jnp.power(ROPE_THETA, -jnp.arange(half, dtype=jnp.float32) / half)
    ang = pos.astype(jnp.float32)[:, None] * inv_freq
    ang = ang.reshape(ang.shape[0], *([1] * (x.ndim - 3)), half)
    cos, sin = jnp.cos(ang), jnp.sin(ang)
    x1 = x[..., :half]
    x2 = x[..., half:ROPE_DIM]
    rot = jnp.concatenate([x1 * cos - x2 * sin, x2 * cos + x1 * sin], -1)
    return jnp.concatenate([rot, x[..., ROPE_DIM:]], -1)


NSA_ROWS = NSA_GROUP * Q_BLK
SEL_KT = 1024
N_SELB = 128
MASKED = -1e9
WIN_KEYS = WINDOW + Q_BLK
KK_W = 2 * HEAD_DIM + N_SELB


def _nsa_prompt_body(qr_ref, qo_ref, kc_ref, vct_ref, kk_ref, vvt_ref, g_ref, o_ref,
                     imp_ref, m_ref, l_ref, acc_ref):
    f32, bf16 = jnp.float32, jnp.bfloat16
    qb = pl.program_id(2)
    q0 = qb * Q_BLK
    qr_t = qr_ref[0, 0, 0]
    qo_t = qo_ref[0, 0, 0]
    n_cmp = kc_ref.shape[2]

    s_c = _dot(kc_ref[0, 0], qr_t)
    n_idx = lax.broadcasted_iota(jnp.int32, (n_cmp, NSA_ROWS), 0)
    qpos_c = q0 + (lax.broadcasted_iota(jnp.int32, (n_cmp, NSA_ROWS), 1) & (Q_BLK - 1))
    cmask = (n_idx * CMP_STRIDE + (CMP_BLK - 1)) <= qpos_c
    s_c = jnp.where(cmask, s_c, MASKED)
    m_c = jnp.max(s_c, axis=0, keepdims=True)
    p_c = jnp.where(cmask, jnp.exp(s_c - m_c), 0.0)
    p_c = p_c / jnp.maximum(jnp.sum(p_c, axis=0, keepdims=True), 1e-30)
    o_ct = _dot(vct_ref[0, 0], p_c.astype(bf16))

    imp = (p_c[:, 0:Q_BLK] + p_c[:, Q_BLK:2 * Q_BLK]) + p_c[:, 2 * Q_BLK:3 * Q_BLK] + p_c[:, 3 * Q_BLK:]
    imp_ref[0:8, :] = jnp.zeros((8, Q_BLK), f32)
    imp_ref[8:8 + n_cmp, :] = imp
    ratio = SEL_BLK // CMP_STRIDE
    n_selb = n_cmp // ratio
    imp_s = imp_ref[pl.ds(7, n_selb, stride=ratio), :]
    for r in range(ratio):
        imp_s = imp_s + imp_ref[pl.ds(8 + r, n_selb, stride=ratio), :]
    blk = lax.broadcasted_iota(jnp.int32, (n_selb, Q_BLK), 0)
    qpos_s = q0 + lax.broadcasted_iota(jnp.int32, (n_selb, Q_BLK), 1)
    cur = lax.shift_right_logical(qpos_s, int(math.log2(SEL_BLK)))
    valid = blk * SEL_BLK <= qpos_s
    forced = (blk == 0) | (blk == cur) | (blk == cur - 1)
    score = jnp.where(valid, imp_s + jnp.where(forced, FORCE_BONUS, 0.0), -1e30)
    picked = jnp.zeros((n_selb, Q_BLK), f32)
    for _ in range(SEL_TOPN):
        best = jnp.max(score, axis=0, keepdims=True)
        first = jnp.min(jnp.where(score == best, blk, n_selb), axis=0, keepdims=True)
        hit = blk == first
        picked = jnp.where(hit, 1.0, picked)
        score = jnp.where(hit, -3e38, score)
    selb_t = jnp.where(valid, picked, 0.0)
    if n_selb < N_SELB:
        selb_t = jnp.concatenate([selb_t, jnp.zeros((N_SELB - n_selb, Q_BLK), f32)], axis=0)
    selb_t = ((selb_t - 1.0) * (-MASKED)).astype(bf16)
    selb_t = jnp.concatenate([selb_t] * NSA_GROUP, axis=1)

    zeros_q = jnp.zeros((HEAD_DIM, NSA_ROWS), bf16)
    q_sel = jnp.concatenate([qo_t, zeros_q, selb_t], axis=0)
    q_win = jnp.concatenate([zeros_q, qo_t, jnp.zeros((N_SELB, NSA_ROWS), bf16)], axis=0)
    qpos_r = q0 + (lax.broadcasted_iota(jnp.int32, (1, NSA_ROWS), 1) & (Q_BLK - 1))

    def v_tiles(first, count):
        return jnp.concatenate([vvt_ref[0, 0, first + j] for j in range(count)], axis=1)

    m_ref[...] = jnp.full(m_ref.shape, MASKED, f32)
    l_ref[...] = jnp.zeros(l_ref.shape, f32)
    acc_ref[...] = jnp.zeros(acc_ref.shape, f32)

    def sel_tile(k0, kt, causal):
        s = _dot(kk_ref[0, 0, pl.ds(k0, kt), :], q_sel)
        if causal:
            kpos = k0 + lax.broadcasted_iota(jnp.int32, (kt, NSA_ROWS), 0)
            s = jnp.where(kpos <= qpos_r, s, MASKED)
        m_old = m_ref[...]
        m_new = jnp.maximum(m_old, jnp.max(s, axis=0, keepdims=True))
        alpha = jnp.exp(m_old - m_new)
        p = jnp.exp(s - m_new)
        l_ref[...] = alpha * l_ref[...] + jnp.sum(p, axis=0, keepdims=True)
        vt = v_tiles(k0 // Q_BLK, kt // Q_BLK)
        acc_ref[...] = alpha * acc_ref[...] + _dot(vt, p.astype(bf16))
        m_ref[...] = m_new

    n_full = q0 // SEL_KT

    def full_step(t, c):
        sel_tile(pl.multiple_of(t * SEL_KT, SEL_KT), SEL_KT, False)
        return c

    lax.fori_loop(0, n_full, full_step, 0)
    sel_tile(pl.multiple_of(n_full * SEL_KT, SEL_KT), SEL_KT, True)
    o_st = acc_ref[0:HEAD_DIM, :] / l_ref[...]

    w0 = pl.multiple_of(jnp.maximum(q0 - WINDOW, 0), Q_BLK)
    s_w = _dot(kk_ref[0, 0, pl.ds(w0, WIN_KEYS), :], q_win)
    kpos_w = w0 + lax.broadcasted_iota(jnp.int32, (WIN_KEYS, NSA_ROWS), 0)
    s_w = jnp.where((kpos_w <= qpos_r) & (kpos_w > qpos_r - WINDOW), s_w, MASKED)
    p_w = jnp.exp(s_w - jnp.max(s_w, axis=0, keepdims=True))
    l_w = jnp.sum(p_w, axis=0, keepdims=True)
    acc_w = _dot(v_tiles(w0 // Q_BLK, WIN_KEYS // Q_BLK), p_w.astype(bf16))
    o_wt = acc_w[HEAD_DIM:2 * HEAD_DIM, :] / l_w

    g = g_ref[0, 0, 0]
    out_t = g[0:1, :] * o_ct + g[1:2, :] * o_st + g[2:3, :] * o_wt
    o_ref[0] = jnp.concatenate([out_t[:, g_ * Q_BLK:(g_ + 1) * Q_BLK] for g_ in range(NSA_GROUP)], axis=0).T


def _nsa_prompt(qr, qo, gt, kc_p, vct, kk, vvt):
    bsz, _, nqb = qr.shape[:3]
    t_ = nqb * Q_BLK
    n_cmp = kc_p.shape[2]
    per_blk = lambda b, h, i: (b, h, i, 0, 0)
    per_head = lambda b, h, i: (b, h, 0, 0)
    return pl.pallas_call(
        _nsa_prompt_body,
        grid=(bsz, NSA_KV_HEADS, nqb),
        in_specs=[pl.BlockSpec((1, 1, 1, HEAD_DIM, NSA_ROWS), per_blk),
                  pl.BlockSpec((1, 1, 1, HEAD_DIM, NSA_ROWS), per_blk),
                  pl.BlockSpec((1, 1, n_cmp, HEAD_DIM), per_head),
                  pl.BlockSpec((1, 1, HEAD_DIM, n_cmp), per_head),
                  pl.BlockSpec((1, 1, t_, KK_W), per_head),
                  pl.BlockSpec((1, 1, nqb, 2 * HEAD_DIM, Q_BLK), lambda b, h, i: (b, h, 0, 0, 0)),
                  pl.BlockSpec((1, 1, 1, 3, NSA_ROWS), per_blk)],
        out_specs=pl.BlockSpec((1, Q_BLK, NSA_GROUP * HEAD_DIM), lambda b, h, i: (b, i, h)),
        out_shape=jax.ShapeDtypeStruct((bsz, t_, NSA_HEADS * HEAD_DIM), jnp.float32),
        scratch_shapes=[pltpu.VMEM((8 + n_cmp, Q_BLK), jnp.float32),
                        pltpu.VMEM((1, NSA_ROWS), jnp.float32),
                        pltpu.VMEM((1, NSA_ROWS), jnp.float32),
                        pltpu.VMEM((2 * HEAD_DIM, NSA_ROWS), jnp.float32)],
        compiler_params=pltpu.CompilerParams(
            dimension_semantics=("arbitrary", "arbitrary", "arbitrary"),
            vmem_limit_bytes=VMEM_LIMIT),
        name="nsa_prompt",
    )(qr, qo, kc_p, vct, kk, vvt, gt)


GLA_SUB = 16
GLA_QK = GLA_HEADS * GLA_DK
GLA_V = GLA_HEADS * GLA_DV


def _dot_tn(a, b):
    return lax.dot_general(a, b, (((0,), (0,)), ((), ())), preferred_element_type=jnp.float32)


def _gla_body(q_ref, k_ref, v_ref, gr_ref, glr_ref, wg_ref, bg_ref, ng_ref, s0_ref, exp_ref, bd_ref,
              o_ref, sfin_ref, st_ref, b_ref, qd_ref, *, t_valid):
    f32, bf16 = jnp.float32, jnp.bfloat16
    tt = q_ref.shape[1]
    ti = pl.program_id(1)

    @pl.when(ti == 0)
    def _():
        st_ref[...] = s0_ref[0]

    row = lax.broadcasted_iota(jnp.int32, (tt, 1), 0)
    z = _dot(glr_ref[0][:, :GLA_RANK].astype(bf16), wg_ref[...]) + bg_ref[...]
    la = (jnp.minimum(z, 0.0) - jnp.log1p(jnp.exp(-jnp.abs(z)))) * (1.0 / GLA_TAU)
    la = jnp.where(ti * tt + row < t_valid, la, 0.0)
    seg = row & (GLA_SUB - 1)
    b = la
    for s in (1, 2, 4, 8):
        b = b + jnp.where(seg >= s, pltpu.roll(b, s, axis=0), 0.0)
    q = q_ref[0] * (GLA_DK ** -0.5)
    k = k_ref[0]
    v = v_ref[0]
    o = _dot((q * k).astype(bf16), exp_ref[...]) * v
    for d in range(1, GLA_SUB):
        decay = jnp.exp(jnp.minimum(b - pltpu.roll(b, d, axis=0), 0.0))
        w = jnp.where(seg >= d, q * pltpu.roll(k, d, axis=0) * decay, 0.0)
        o = o + _dot(w.astype(bf16), exp_ref[...]) * pltpu.roll(v, d, axis=0)
    o_ref[0] = o
    b_ref[...] = b
    qd_ref[...] = (q * jnp.exp(b)).astype(bf16)

    def block_step(c, carry):
        rows = pl.ds(pl.multiple_of(c * GLA_SUB, GLA_SUB), GLA_SUB)
        st = st_ref[...]
        o_ref[0, rows, :] += _dot_nt(qd_ref[rows, :], st.astype(bf16))
        bc = b_ref[rows, :]
        bl = bc[GLA_SUB - 1:GLA_SUB, :]
        kc = (k_ref[0, rows, :] * jnp.exp(bl - bc)).astype(bf16)
        upd = _dot_tn(v_ref[0, rows, :].astype(bf16), kc)
        st_ref[...] = st * jnp.exp(bl) + upd * bd_ref[...]
        return carry

    lax.fori_loop(0, tt // GLA_SUB, block_step, 0)
    sfin_ref[0] = st_ref[...]
    gr = gr_ref[0]
    gate = gr * jax.nn.sigmoid(gr)
    for h in range(GLA_HEADS):
        cols = slice(h * GLA_DV, (h + 1) * GLA_DV)
        oh = o_ref[0, :, cols]
        ms = jnp.mean(oh * oh, axis=-1, keepdims=True)
        o_ref[0, :, cols] = oh * lax.rsqrt(ms + LN_EPS) * ng_ref[...] * gate[:, cols]


def _gla(h, w_gla_gate, b_gla_gate, gla_norm_g, gla_state):
    bsz, t_, n_in = h.shape
    tp = -(-t_ // GLA_SUB) * GLA_SUB
    if tp != t_:
        h = jnp.pad(h, ((0, 0), (0, tp - t_), (0, 0)))
    tt = min(tp, 256)
    heads = np.arange(GLA_HEADS)
    expand = np.repeat(np.repeat(np.eye(GLA_HEADS, dtype=np.float32), GLA_DK, 0), GLA_DV, 1)
    bdmask = jnp.asarray(expand.T)
    if gla_state is None:
        s0 = jnp.zeros((bsz, GLA_V, GLA_QK), jnp.float32)
    else:
        s0 = jnp.zeros((bsz, GLA_HEADS, GLA_DV, GLA_HEADS, GLA_DK), jnp.float32)
        s0 = s0.at[:, heads, :, heads, :].set(gla_state.transpose(1, 0, 3, 2)).reshape(bsz, GLA_V, GLA_QK)
    tile = lambda width, blk: pl.BlockSpec((1, tt, width), lambda b, i: (b, i, blk))
    fixed2 = lambda shape: pl.BlockSpec(shape, lambda b, i: (0, 0))
    per_b = pl.BlockSpec((1, GLA_V, GLA_QK), lambda b, i: (b, 0, 0))
    o, s_t = pl.pallas_call(
        functools.partial(_gla_body, t_valid=t_),
        grid=(bsz, tp // tt),
        in_specs=[tile(GLA_QK, 0), tile(GLA_QK, 1), tile(GLA_V, 1), tile(GLA_V, 2),
                  tile(LANE, (2 * GLA_QK + 2 * GLA_V + NSA_SIZES[0] + NSA_SIZES[1]) // LANE),
                  fixed2((GLA_RANK, GLA_QK)), fixed2((1, GLA_QK)), fixed2((1, GLA_DV)), per_b,
                  fixed2((GLA_QK, GLA_V)), fixed2((GLA_V, GLA_QK))],
        out_specs=[pl.BlockSpec((1, tt, GLA_V), lambda b, i: (b, i, 0)), per_b],
        out_shape=[jax.ShapeDtypeStruct((bsz, tp, GLA_V), jnp.float32),
                   jax.ShapeDtypeStruct((bsz, GLA_V, GLA_QK), jnp.float32)],
        scratch_shapes=[pltpu.VMEM((GLA_V, GLA_QK), jnp.float32), pltpu.VMEM((tt, GLA_QK), jnp.float32),
                        pltpu.VMEM((tt, GLA_QK), jnp.bfloat16)],
        compiler_params=pltpu.CompilerParams(dimension_semantics=("arbitrary", "arbitrary"),
                                             vmem_limit_bytes=VMEM_LIMIT),
        name="gla",
    )(h, h, h, h, h, w_gla_gate.astype(jnp.bfloat16), b_gla_gate.reshape(1, GLA_QK),
      gla_norm_g.reshape(1, GLA_DV), s0, jnp.asarray(expand, jnp.bfloat16), bdmask)
    s_new = s_t.reshape(bsz, GLA_HEADS, GLA_DV, GLA_HEADS, GLA_DK)[:, heads, :, heads, :]
    return o[:, :t_], s_new.transpose(1, 0, 3, 2)


COL_NQ = 2 * GLA_QK + 2 * GLA_V
COL_NKV = COL_NQ + NSA_SIZES[0]
COL_TAIL = COL_NKV + NSA_SIZES[1]
TAIL_GATE = GLA_RANK
_ORIG = np.cumsum((0,) + GLA_SIZES + NSA_SIZES)
IN_AB_PERM = np.concatenate([np.arange(_ORIG[0], _ORIG[4]), np.arange(_ORIG[5], _ORIG[7]),
                             np.arange(_ORIG[4], _ORIG[5]), np.arange(_ORIG[7], _ORIG[8])])
SUBS = Q_BLK // CMP_STRIDE


def _nsa_prep_body(nq_ref, kv0_ref, kv1_ref, kv2_ref, tail_ref, rc_ref, ru_ref, rd_ref, pool_ref,
                   rows_ref, win_ref, kk_ref, vvt_ref, qr_ref, qo_ref, g_ref, pooled_ref):
    bf16 = jnp.bfloat16
    q0 = pl.program_id(1) * Q_BLK
    kv_w = NSA_KV_HEADS * HEAD_DIM

    def rope(x):
        reps = x.shape[1] // LANE
        wide = lambda r: jnp.concatenate([r[...]] * reps, axis=1) if reps > 1 else r[...]
        half = ROPE_DIM // 2
        return (x * wide(rc_ref) + pltpu.roll(x, half, axis=1) * wide(ru_ref)
                + pltpu.roll(x, x.shape[1] - half, axis=1) * wide(rd_ref))

    kv0, kv1, kv2 = kv0_ref[0], kv1_ref[0], kv2_ref[0]
    k_sel, v_sel = rope(kv1[:, :kv_w]), kv1[:, kv_w:]
    k_win, v_win = rope(kv2[:, :kv_w]), kv2[:, kv_w:]
    rows_ref[0] = jnp.concatenate([kv0, k_sel, v_sel], axis=1)
    win_ref[0] = jnp.concatenate([k_win, v_win], axis=1)
    blk_id = lax.shift_right_logical(q0 + lax.broadcasted_iota(jnp.int32, (Q_BLK, N_SELB), 0),
                                     int(math.log2(SEL_BLK)))
    onehot = jnp.where(lax.broadcasted_iota(jnp.int32, (Q_BLK, N_SELB), 1) == blk_id, 1.0, 0.0).astype(bf16)
    q = nq_ref[0] * (HEAD_DIM ** -0.5)
    q_rot = rope(q)
    gates_t = jax.nn.sigmoid(tail_ref[0]).T
    for h in range(NSA_KV_HEADS):
        hs = slice(h * HEAD_DIM, (h + 1) * HEAD_DIM)
        kk_ref[0, h] = jnp.concatenate([k_sel[:, hs].astype(bf16), k_win[:, hs].astype(bf16), onehot], axis=1)
        vvt_ref[0, h, 0] = jnp.concatenate([v_sel[:, hs], v_win[:, hs]], axis=1).T.astype(bf16)
        gw = NSA_GROUP * HEAD_DIM
        for src, dst in ((q, qr_ref), (q_rot, qo_ref)):
            t = src[:, h * gw:(h + 1) * gw].T
            dst[0, h, 0] = jnp.concatenate([t[g * HEAD_DIM:(g + 1) * HEAD_DIM] for g in range(NSA_GROUP)],
                                           axis=1).astype(bf16)
        base = TAIL_GATE + h * NSA_GROUP * 3
        g_ref[0, h, 0] = jnp.concatenate(
            [jnp.concatenate([gates_t[base + 3 * g + c:base + 3 * g + c + 1] for g in range(NSA_GROUP)], axis=1)
             for c in range(3)], axis=0)
    kc_in, vc_in = kv0[:, :kv_w].astype(bf16), kv0[:, kv_w:].astype(bf16)
    pooled_ref[0] = jnp.concatenate([_dot(pool_ref[0], kc_in), _dot(pool_ref[1], kc_in),
                                     _dot(pool_ref[2], vc_in), _dot(pool_ref[3], vc_in)], axis=1)


def _nsa_prep(h, pos, w_cmp_pool):
    bsz, t_, _ = h.shape
    nqb = t_ // Q_BLK
    bf16 = jnp.bfloat16
    half = ROPE_DIM // 2
    inv_freq = jnp.power(ROPE_THETA, -jnp.arange(half, dtype=jnp.float32) / half)
    ang = pos.astype(jnp.float32)[:, None] * inv_freq
    cos, sin = jnp.cos(ang), jnp.sin(ang)
    rest = HEAD_DIM - ROPE_DIM
    z8, zr = jnp.zeros((t_, half), jnp.float32), jnp.zeros((t_, rest), jnp.float32)
    two = lambda a: jnp.concatenate([a, a], axis=1)
    rc = two(jnp.concatenate([cos, cos, jnp.ones((t_, rest), jnp.float32)], axis=1))
    ru = two(jnp.concatenate([z8, sin, zr], axis=1))
    rd = two(jnp.concatenate([-sin, z8, zr], axis=1))
    pool = _pool_matrices(w_cmp_pool)
    kv_w = NSA_KV_HEADS * HEAD_DIM
    col = lambda width, off: pl.BlockSpec((1, Q_BLK, width), lambda b, i: (b, i, off // width))
    rows_t = pl.BlockSpec((Q_BLK, LANE), lambda b, i: (i, 0))
    head4 = lambda r, c: pl.BlockSpec((1, NSA_KV_HEADS, 1, r, c), lambda b, i: (b, 0, i, 0, 0))
    return pl.pallas_call(
        _nsa_prep_body,
        grid=(bsz, nqb),
        in_specs=[col(NSA_SIZES[0], COL_NQ), col(2 * kv_w, COL_NKV), col(2 * kv_w, COL_NKV + 2 * kv_w),
                  col(2 * kv_w, COL_NKV + 4 * kv_w), col(LANE, COL_TAIL), rows_t, rows_t, rows_t,
                  pl.BlockSpec((4, SUBS, Q_BLK), lambda b, i: (0, 0, 0))],
        out_specs=[pl.BlockSpec((1, Q_BLK, 4 * kv_w), lambda b, i: (b, i, 0)),
                   pl.BlockSpec((1, Q_BLK, 2 * kv_w), lambda b, i: (b, i, 0)),
                   pl.BlockSpec((1, NSA_KV_HEADS, Q_BLK, KK_W), lambda b, i: (b, 0, i, 0)),
                   head4(2 * HEAD_DIM, Q_BLK), head4(HEAD_DIM, NSA_ROWS), head4(HEAD_DIM, NSA_ROWS),
                   head4(3, NSA_ROWS),
                   pl.BlockSpec((1, SUBS, 4 * kv_w), lambda b, i: (b, i, 0))],
        out_shape=[jax.ShapeDtypeStruct((bsz, t_, 4 * kv_w), jnp.float32),
                   jax.ShapeDtypeStruct((bsz, t_, 2 * kv_w), jnp.float32),
                   jax.ShapeDtypeStruct((bsz, NSA_KV_HEADS, t_, KK_W), bf16),
                   jax.ShapeDtypeStruct((bsz, NSA_KV_HEADS, nqb, 2 * HEAD_DIM, Q_BLK), bf16),
                   jax.ShapeDtypeStruct((bsz, NSA_KV_HEADS, nqb, HEAD_DIM, NSA_ROWS), bf16),
                   jax.ShapeDtypeStruct((bsz, NSA_KV_HEADS, nqb, HEAD_DIM, NSA_ROWS), bf16),
                   jax.ShapeDtypeStruct((bsz, NSA_KV_HEADS, nqb, 3, NSA_ROWS), jnp.float32),
                   jax.ShapeDtypeStruct((bsz, t_ // CMP_STRIDE, 4 * kv_w), jnp.float32)],
        compiler_params=pltpu.CompilerParams(dimension_semantics=("arbitrary", "arbitrary")),
        name="nsa_prep",
    )(h, h, h, h, h, rc, ru, rd, pool)


PAGE_GROUP = 16
DEC_KEYS = PAGE_GROUP * PAGE_SIZE
NEW_PAD = 8
KV_W = NSA_KV_HEADS * HEAD_DIM


def _dec_pool_body(pt_ref, *refs):
    page_refs, pool_ref, out_ref = refs[:PAGE_GROUP], refs[PAGE_GROUP], refs[PAGE_GROUP + 1]
    bf16 = jnp.bfloat16
    pages = [pr[0] for pr in page_refs]
    kc_t = jnp.concatenate([p[:KV_W] for p in pages], axis=1).astype(bf16)
    vc_t = jnp.concatenate([p[KV_W:] for p in pages], axis=1).astype(bf16)
    out_ref[0] = jnp.concatenate([_dot_nt(pool_ref[0], kc_t), _dot_nt(pool_ref[1], kc_t),
                                  _dot_nt(pool_ref[2], vc_t), _dot_nt(pool_ref[3], vc_t)], axis=1)


def _page_specs(n_pages, col_blk):
    def spec(i):
        return pl.BlockSpec((1, 2 * KV_W, PAGE_SIZE),
                            lambda b, j, pt: (pt[b * n_pages + j * PAGE_GROUP + i], col_blk, 0))
    return [spec(i) for i in range(PAGE_GROUP)]


def _dec_pool(cache, page_table, pool):
    bsz, n_pages = page_table.shape
    grid_spec = pltpu.PrefetchScalarGridSpec(
        num_scalar_prefetch=1, grid=(bsz, n_pages // PAGE_GROUP),
        in_specs=_page_specs(n_pages, 0) + [pl.BlockSpec(pool.shape, lambda b, j, pt: (0, 0, 0))],
        out_specs=pl.BlockSpec((1, PAGE_GROUP * SUBS, 4 * KV_W), lambda b, j, pt: (b, j, 0)))
    return pl.pallas_call(
        _dec_pool_body, grid_spec=grid_spec,
        out_shape=jax.ShapeDtypeStruct((bsz, n_pages * SUBS, 4 * KV_W), jnp.float32),
        compiler_params=pltpu.CompilerParams(dimension_semantics=("arbitrary", "arbitrary")),
        name="nsa_dec_pool",
    )(page_table.reshape(-1), *([cache] * PAGE_GROUP), pool)


def _dec_select_body(qr_ref, kct_ref, vc_ref, band_ref, oc_ref, selb_ref, *, qpos0, n_q, n_pick, n_blk):
    f32, bf16 = jnp.float32, jnp.bfloat16
    n_cmp = kct_ref.shape[3]
    rows = NSA_GROUP * n_q
    for sq, h in [(a, b) for a in range(qr_ref.shape[0]) for b in range(NSA_KV_HEADS)]:
        s_c = _dot(qr_ref[sq, h], kct_ref[sq, h])
        n_idx = lax.broadcasted_iota(jnp.int32, (rows, n_cmp), 1)
        qpos = qpos0 + (lax.broadcasted_iota(jnp.int32, (rows, n_cmp), 0) % n_q)
        cmask = (n_idx * CMP_STRIDE + (CMP_BLK - 1)) <= qpos
        s_c = jnp.where(cmask, s_c, MASKED)
        p_c = jnp.where(cmask, jnp.exp(s_c - jnp.max(s_c, axis=1, keepdims=True)), 0.0)
        p_c = p_c / jnp.maximum(jnp.sum(p_c, axis=1, keepdims=True), 1e-30)
        oc_ref[sq, h] = _dot(p_c.astype(bf16), vc_ref[sq, h])
        imp = p_c[0:n_q]
        for g in range(1, NSA_GROUP):
            imp = imp + p_c[g * n_q:(g + 1) * n_q]
        imp_s = jnp.zeros((n_q, N_SELB), f32)
        rem = imp
        for _ in range(3):
            part = rem.astype(bf16)
            imp_s = imp_s + _dot(part, band_ref[...])
            rem = rem - part.astype(f32)
        blk = lax.broadcasted_iota(jnp.int32, (n_q, N_SELB), 1)
        qpos_s = qpos0 + lax.broadcasted_iota(jnp.int32, (n_q, N_SELB), 0)
        cur = lax.shift_right_logical(qpos_s, int(math.log2(SEL_BLK)))
        valid = (blk * SEL_BLK <= qpos_s) & (blk < n_blk)
        forced = (blk == 0) | (blk == cur) | (blk == cur - 1)
        score = jnp.where(valid, imp_s + jnp.where(forced, FORCE_BONUS, 0.0), -1e30)
        picked = jnp.zeros((n_q, N_SELB), f32)
        for _ in range(n_pick):
            best = jnp.max(score, axis=1, keepdims=True)
            first = jnp.min(jnp.where(score == best, blk, N_SELB), axis=1, keepdims=True)
            hit = blk == first
            picked = jnp.where(hit, 1.0, picked)
            score = jnp.where(hit, -3e38, score)
        selb_ref[sq, h] = (jnp.where(valid, picked, 0.0) - 1.0) * (-MASKED)


def _dec_select(qr, kct, vc, n_q, qpos0, n_pick, n_blk):
    bsz = qr.shape[0]
    rows = NSA_GROUP * n_q
    n_cmp = kct.shape[3]
    ratio = SEL_BLK // CMP_STRIDE
    c_idx, j_idx = np.arange(n_cmp)[:, None], np.arange(N_SELB)[None, :]
    band = jnp.asarray(((c_idx >= ratio * j_idx - 1) & (c_idx <= ratio * j_idx + ratio - 1)), jnp.bfloat16)
    per_step = next(c for c in (4, 2, 1) if bsz % c == 0)
    per_b = lambda *tail: pl.BlockSpec((per_step, NSA_KV_HEADS) + tail, lambda b: (b, 0, 0, 0))
    return pl.pallas_call(
        functools.partial(_dec_select_body, qpos0=qpos0, n_q=n_q, n_pick=n_pick, n_blk=n_blk),
        grid=(bsz // per_step,),
        in_specs=[per_b(rows, HEAD_DIM), per_b(HEAD_DIM, n_cmp), per_b(n_cmp, HEAD_DIM),
                  pl.BlockSpec((n_cmp, N_SELB), lambda b: (0, 0))],
        out_specs=[per_b(rows, HEAD_DIM), per_b(n_q, N_SELB)],
        out_shape=[jax.ShapeDtypeStruct((bsz, NSA_KV_HEADS, rows, HEAD_DIM), jnp.float32),
                   jax.ShapeDtypeStruct((bsz, NSA_KV_HEADS, n_q, N_SELB), jnp.float32)],
        compiler_params=pltpu.CompilerParams(dimension_semantics=("arbitrary",)),
        name="nsa_dec_select",
    )(qr, kct, vc, band)


def _dec_attend_body(pt_ref, *refs, qpos0, n_q, past):
    page_refs = refs[:PAGE_GROUP]
    (qs_ref, qw_ref, knew_ref, vnew_ref, wbuf_ref, wnew_ref, oc_ref, g_ref,
     o_ref, m_ref, l_ref, acc_ref) = refs[PAGE_GROUP:]
    f32, bf16 = jnp.float32, jnp.bfloat16
    j = pl.program_id(1)
    n_rows = qs_ref.shape[1]

    @pl.when(j == 0)
    def _():
        m_ref[...] = jnp.full(m_ref.shape, MASKED, f32)
        l_ref[...] = jnp.zeros(l_ref.shape, f32)
        acc_ref[...] = jnp.zeros(acc_ref.shape, f32)

    def online(s, weigh):
        m_old = m_ref[...]
        m_new = jnp.maximum(m_old, jnp.max(s, axis=1, keepdims=True))
        alpha = jnp.exp(m_old - m_new)
        p = jnp.exp(s - m_new)
        l_ref[...] = alpha * l_ref[...] + jnp.sum(p, axis=1, keepdims=True)
        acc_ref[...] = alpha * acc_ref[...] + weigh(p.astype(bf16))
        m_ref[...] = m_new

    qs = qs_ref[0]
    pages = [pr[0] for pr in page_refs]
    keys_t = jnp.concatenate([p[:KV_W] for p in pages], axis=1).astype(bf16)
    vals_t = jnp.concatenate([p[KV_W:] for p in pages], axis=1).astype(bf16)
    blk_id = j * (DEC_KEYS // SEL_BLK) + lax.shift_right_logical(
        lax.broadcasted_iota(jnp.int32, (N_SELB, DEC_KEYS), 1), int(math.log2(SEL_BLK)))
    onehot_t = jnp.where(lax.broadcasted_iota(jnp.int32, (N_SELB, DEC_KEYS), 0) == blk_id, 1.0, 0.0).astype(bf16)
    online(_dot(qs, jnp.concatenate([keys_t, onehot_t], axis=0)), lambda p: _dot_nt(p, vals_t))

    @pl.when(j == pl.num_programs(1) - 1)
    def _():
        row_q = qpos0 + (lax.broadcasted_iota(jnp.int32, (n_rows, 1), 0) % n_q)
        qh = qw_ref[0]
        new_pos = past + lax.broadcasted_iota(jnp.int32, (n_rows, NEW_PAD), 1)
        new_ok = (new_pos <= row_q) & (new_pos < past + n_q)
        s_new = jnp.where(new_ok, _dot_nt(qh, knew_ref[0]), MASKED)
        online(s_new, lambda p: _dot(p, vnew_ref[0]))
        o_s = acc_ref[...] / l_ref[...]
        wbuf_t = wbuf_ref[0]
        wnew = wnew_ref[0]
        n_buf = wbuf_t.shape[1]
        s_b = _dot(qh, wbuf_t[:KV_W].astype(bf16))
        pos_b = (past - n_buf) + lax.broadcasted_iota(jnp.int32, (n_rows, n_buf), 1)
        s_b = jnp.where((pos_b > row_q - WINDOW) & (pos_b >= 0), s_b, MASKED)
        s_n = jnp.where(new_ok, _dot_nt(qh, wnew[:, :KV_W].astype(bf16)), MASKED)
        m_w = jnp.maximum(jnp.max(s_b, axis=1, keepdims=True), jnp.max(s_n, axis=1, keepdims=True))
        p_b, p_n = jnp.exp(s_b - m_w), jnp.exp(s_n - m_w)
        l_w = jnp.sum(p_b, axis=1, keepdims=True) + jnp.sum(p_n, axis=1, keepdims=True)
        o_w = (_dot_nt(p_b.astype(bf16), wbuf_t[KV_W:].astype(bf16))
               + _dot(p_n.astype(bf16), wnew[:, KV_W:].astype(bf16))) / l_w
        half = n_rows // NSA_KV_HEADS
        own = lambda a: jnp.concatenate([a[h * half:(h + 1) * half, h * HEAD_DIM:(h + 1) * HEAD_DIM]
                                         for h in range(NSA_KV_HEADS)], axis=0)
        g = g_ref[0]
        o_ref[0] = g[:, 0:1] * oc_ref[0] + g[:, 1:2] * own(o_s) + g[:, 2:3] * own(o_w)


def _dec_attend(cache, page_table, qs, qw, knew, vnew, wbuf, wnew, o_c, gates, n_q, qpos0):
    bsz, n_pages = page_table.shape
    n_rows = qs.shape[1]
    per_b = lambda *tail: pl.BlockSpec((1,) + tail, lambda b, j, pt: (b, 0, 0))
    grid_spec = pltpu.PrefetchScalarGridSpec(
        num_scalar_prefetch=1, grid=(bsz, n_pages // PAGE_GROUP),
        in_specs=_page_specs(n_pages, 1) + [
            per_b(n_rows, KV_W + N_SELB), per_b(n_rows, KV_W), per_b(NEW_PAD, KV_W), per_b(NEW_PAD, KV_W),
            per_b(2 * KV_W, wbuf.shape[2]), per_b(NEW_PAD, 2 * KV_W), per_b(n_rows, HEAD_DIM), per_b(n_rows, 3)],
        out_specs=per_b(n_rows, HEAD_DIM),
        scratch_shapes=[pltpu.VMEM((n_rows, 1), jnp.float32), pltpu.VMEM((n_rows, 1), jnp.float32),
                        pltpu.VMEM((n_rows, KV_W), jnp.float32)])
    return pl.pallas_call(
        functools.partial(_dec_attend_body, qpos0=qpos0, n_q=n_q, past=n_pages * PAGE_SIZE),
        grid_spec=grid_spec,
        out_shape=jax.ShapeDtypeStruct((bsz, n_rows, HEAD_DIM), jnp.float32),
        compiler_params=pltpu.CompilerParams(dimension_semantics=("arbitrary", "arbitrary")),
        name="nsa_dec_attend",
    )(page_table.reshape(-1), *([cache] * PAGE_GROUP), qs, qw, knew, vnew, wbuf, wnew, o_c, gates)


def _pool_matrices(w_cmp_pool, rows=Q_BLK):
    subs = rows // CMP_STRIDE
    sub = np.arange(rows) // CMP_STRIDE == np.arange(subs)[:, None]
    w_rep = jnp.tile(w_cmp_pool.reshape(2, 2, CMP_STRIDE), (1, 1, subs))
    return jnp.where(sub[None, None], w_rep[:, :, None, :], 0.0).reshape(4, subs, rows).astype(jnp.bfloat16)


def _compressed_from_pooled(pooled):
    bsz, n_sub, _ = pooled.shape
    pooled = pooled.reshape(bsz, n_sub, 4, NSA_KV_HEADS, HEAD_DIM)
    kc = pooled[:, :-1, 0] + pooled[:, 1:, 1]
    vc = pooled[:, :-1, 2] + pooled[:, 1:, 3]
    pad = lambda a: jnp.pad(a, ((0, 0), (0, 1), (0, 0), (0, 0))).transpose(0, 2, 1, 3)
    return pad(kc), pad(vc)


def _nsa_decode(q_raw, q_rot, gates, rows_full, rows_win, cache, page_table, win_buf, w_cmp_pool, past):
    bsz, n_q = q_raw.shape[:2]
    bf16 = jnp.bfloat16
    n_blk = past // SEL_BLK
    assert past % DEC_KEYS == 0 and n_blk <= N_SELB and n_q <= NEW_PAD
    scale = HEAD_DIM ** -0.5
    cache2 = cache.transpose(0, 2, 3, 4, 1).reshape(cache.shape[0], 4 * KV_W, PAGE_SIZE)
    pooled = _dec_pool(cache2, page_table, _pool_matrices(w_cmp_pool, DEC_KEYS))
    kc_p, vc_p = _compressed_from_pooled(pooled)
    rows_of = lambda a: a.transpose(0, 2, 3, 1, 4).reshape(bsz, NSA_KV_HEADS, NSA_GROUP * n_q, a.shape[-1])
    qr = rows_of((q_raw * scale).astype(bf16))
    n_pick = min(SEL_TOPN, n_blk + 1) - 1
    o_c, selb = _dec_select(qr, kc_p.transpose(0, 1, 3, 2).astype(bf16), vc_p.astype(bf16), n_q, past, n_pick, n_blk)
    qo = rows_of((q_rot * scale).astype(bf16))
    zero = jnp.zeros_like(qo[:, 0])
    qw = jnp.concatenate([jnp.concatenate([qo[:, 0], zero], -1), jnp.concatenate([zero, qo[:, 1]], -1)], axis=1)
    bias = jnp.tile(selb, (1, 1, NSA_GROUP, 1)).reshape(bsz, -1, N_SELB).astype(bf16)
    qs = jnp.concatenate([qw, bias], axis=-1)
    pad_new = lambda a: jnp.pad(a.reshape(bsz, n_q, -1), ((0, 0), (0, NEW_PAD - n_q), (0, 0)))
    knew = pad_new(rows_full[:, :, 2]).astype(bf16)
    vnew = pad_new(rows_full[:, :, 3]).astype(bf16)
    wnew = pad_new(rows_win)
    wbuf = win_buf.transpose(0, 2, 3, 4, 1).reshape(bsz, 2 * KV_W, win_buf.shape[1])
    gt = rows_of(gates).reshape(bsz, -1, 3)
    o = _dec_attend(cache2, page_table, qs, qw, knew, vnew, wbuf, wnew,
                    o_c.reshape(bsz, -1, HEAD_DIM), gt, n_q, past)
    o = o.reshape(bsz, NSA_KV_HEADS, NSA_GROUP, n_q, HEAD_DIM).transpose(0, 3, 1, 2, 4)
    return o.reshape(bsz, n_q, NSA_HEADS * HEAD_DIM)


def _ab_mixer(x, pos, w_in, w_gla_gate, b_gla_gate, gla_norm_g, w_cmp_pool, w_out,
              gla_state, nsa_cache, page_table, win_buf):
    bsz, t_, _ = x.shape
    h_in = _mm(x.reshape(bsz * t_, -1), w_in[:, IN_AB_PERM], keep_pad=True).reshape(bsz, t_, -1)
    o_a, s_a = _gla(h_in, w_gla_gate, b_gla_gate, gla_norm_g, gla_state)
    kv_w = NSA_KV_HEADS * HEAD_DIM
    if nsa_cache is None:
        rows2, win2, kk, vvt, qr, qo, gt, pooled = _nsa_prep(h_in, pos, w_cmp_pool)
        pooled = pooled.reshape(bsz, t_ // CMP_STRIDE, 4, NSA_KV_HEADS, HEAD_DIM)
        kc = pooled[:, :-1, 0] + pooled[:, 1:, 1]
        vc = pooled[:, :-1, 2] + pooled[:, 1:, 3]
        kc_p = jnp.pad(kc, ((0, 0), (0, 1), (0, 0), (0, 0))).transpose(0, 2, 1, 3).astype(jnp.bfloat16)
        vct = jnp.pad(vc, ((0, 0), (0, 1), (0, 0), (0, 0))).transpose(0, 2, 3, 1).astype(jnp.bfloat16)
        o_b = _nsa_prompt(qr, qo, gt, kc_p, vct, kk, vvt)
        rows_full = rows2.reshape(bsz, t_, 4, NSA_KV_HEADS, HEAD_DIM)
        new_win = win2[:, -min(WINDOW, t_):].reshape(bsz, -1, 2, NSA_KV_HEADS, HEAD_DIM)
    else:
        nq = h_in[..., COL_NQ:COL_NKV]
        nkv = h_in[..., COL_NKV:COL_TAIL]
        ngate = h_in[..., COL_TAIL + TAIL_GATE:COL_TAIL + TAIL_GATE + NSA_SIZES[2]]
        q_raw = nq.reshape(bsz, t_, NSA_KV_HEADS, NSA_GROUP, HEAD_DIM)
        q_rot = _partial_rope(q_raw, pos)
        kv = nkv.reshape(bsz, t_, 6, NSA_KV_HEADS, HEAD_DIM)
        k_sel = _partial_rope(kv[:, :, 2], pos)
        k_win = _partial_rope(kv[:, :, 4], pos)
        rows_full = jnp.stack([kv[:, :, 0], kv[:, :, 1], k_sel, kv[:, :, 3]], axis=2)
        rows_win = jnp.stack([k_win, kv[:, :, 5]], axis=2)
        gates = jax.nn.sigmoid(ngate).reshape(bsz, t_, NSA_KV_HEADS, NSA_GROUP, 3)
        past_len = page_table.shape[1] * PAGE_SIZE
        o_b = _nsa_decode(q_raw, q_rot, gates, rows_full, rows_win, nsa_cache, page_table, win_buf,
                          w_cmp_pool, past_len)
        w_buf = win_buf.shape[1]
        kw = jnp.concatenate([win_buf, rows_win], axis=1)
        new_win = kw[:, -w_buf:]
    y = _mm_pair(o_a.reshape(bsz * t_, -1), o_b.reshape(bsz * t_, -1), w_out).reshape(bsz, t_, -1)
    return y, s_a, rows_full, new_win


CONV_HALO = 32
CONV_LEAD = CONV_HALO - (CONV_W - 1)


def _conv_body(x_ref, buf0_ref, w1_ref, b1_ref, wdw_ref, bdw_ref, g_ref, b_ref, w2_ref, b2_ref,
               o_ref, tail_ref, ext_ref, z_ref, *, t_last):
    bf16 = jnp.bfloat16
    tt = x_ref.shape[1]
    i = pl.program_id(1)

    @pl.when(i == 0)
    def _():
        ext_ref[0:CONV_HALO, :] = buf0_ref[0]
        ext_ref[CONV_HALO + tt:CONV_HALO + tt + SUBLANES, :] = jnp.zeros((SUBLANES, D_CONV), jnp.float32)

    h = _dot(x_ref[0].astype(bf16), w1_ref[...]) + b1_ref[...]
    ext_ref[CONV_HALO:CONV_HALO + tt, :] = h[:, :D_CONV] * jax.nn.sigmoid(h[:, D_CONV:])
    c = jnp.zeros((tt, D_CONV), jnp.float32) + bdw_ref[...]
    for r in range(SUBLANES):
        z = None
        for a in range(CONV_HALO // SUBLANES + 1):
            k = SUBLANES * a + r - CONV_LEAD
            if 0 <= k < CONV_W:
                term = ext_ref[SUBLANES * a:SUBLANES * a + tt + SUBLANES, :] * wdw_ref[k:k + 1, :]
                z = term if z is None else z + term
        if r == 0:
            c = c + z[:tt]
        else:
            z_ref[...] = z
            c = c + z_ref[pl.ds(r, tt), :]
    c = _ln_rows(c, g_ref[...], b_ref[...])
    c = c * jax.nn.sigmoid(c)
    o_ref[0] = _dot(c.astype(bf16), w2_ref[...]) + b2_ref[...]
    tail_ref[0] = ext_ref[t_last:t_last + CONV_HALO, :]
    ext_ref[0:CONV_HALO, :] = ext_ref[tt:tt + CONV_HALO, :]


def _conv_module(x, conv_buf, w_pw1, b_pw1, w_dw, b_dw, ln_g, ln_b, w_pw2, b_pw2):
    bsz, t_, d = x.shape
    bf16 = jnp.bfloat16
    tp = -(-t_ // 8) * 8
    tt = min(tp, 256)
    n_t = tp // tt
    if tp != t_:
        x = jnp.pad(x, ((0, 0), (0, tp - t_), (0, 0)))
    if conv_buf is None:
        buf0 = jnp.zeros((bsz, CONV_HALO, D_CONV), jnp.float32)
    else:
        buf0 = jnp.pad(conv_buf, ((0, 0), (CONV_LEAD, 0), (0, 0)))
    fixed = lambda shape: pl.BlockSpec(shape, lambda b, i: (0,) * len(shape))
    per_b = pl.BlockSpec((1, CONV_HALO, D_CONV), lambda b, i: (b, 0, 0))
    out, tail = pl.pallas_call(
        functools.partial(_conv_body, t_last=t_ - (n_t - 1) * tt),
        grid=(bsz, n_t),
        in_specs=[pl.BlockSpec((1, tt, d), lambda b, i: (b, i, 0)), per_b,
                  fixed((d, 2 * D_CONV)), fixed((1, 2 * D_CONV)), fixed((CONV_HALO, D_CONV)), fixed((1, D_CONV)),
                  fixed((1, D_CONV)), fixed((1, D_CONV)), fixed((D_CONV, d)), fixed((1, d))],
        out_specs=[pl.BlockSpec((1, tt, d), lambda b, i: (b, i, 0)), per_b],
        out_shape=[jax.ShapeDtypeStruct((bsz, tp, d), jnp.float32),
                   jax.ShapeDtypeStruct((bsz, CONV_HALO, D_CONV), jnp.float32)],
        scratch_shapes=[pltpu.VMEM((CONV_HALO + tt + SUBLANES, D_CONV), jnp.float32),
                        pltpu.VMEM((tt + SUBLANES, D_CONV), jnp.float32)],
        compiler_params=pltpu.CompilerParams(dimension_semantics=("arbitrary", "arbitrary"),
                                             vmem_limit_bytes=VMEM_LIMIT),
        name="conv_module",
    )(x, buf0, w_pw1.astype(bf16), b_pw1.reshape(1, -1), jnp.pad(w_dw, ((0, CONV_HALO - CONV_W), (0, 0))),
      b_dw.reshape(1, -1), ln_g.reshape(1, -1), ln_b.reshape(1, -1), w_pw2.astype(bf16), b_pw2.reshape(1, -1))
    return out[:, :t_], tail[:, CONV_LEAD:]


PACK_W = 256
SC_WINDOW = 128
SC_TILES = 32


def _pack_rows(y):
    out = []
    for h in range(2):
        lo = lax.bitcast_convert_type(y[:, 2 * h * PACK_W:(2 * h + 1) * PACK_W].astype(jnp.bfloat16)
                                      .astype(jnp.float32), jnp.uint32)
        hi = lax.bitcast_convert_type(y[:, (2 * h + 1) * PACK_W:(2 * h + 2) * PACK_W].astype(jnp.bfloat16)
                                      .astype(jnp.float32), jnp.uint32)
        out.append(lax.bitcast_convert_type((lo >> 16) | hi, jnp.int32))
    return out


def _unpack_words(w):
    u = lax.bitcast_convert_type(w, jnp.uint32)
    lo = lax.bitcast_convert_type(u << 16, jnp.float32)
    hi = lax.bitcast_convert_type(u & jnp.uint32(0xFFFF0000), jnp.float32)
    return lo, hi


def _gather_rows(src, idx):
    n = idx.shape[0]
    if n % (SC_WINDOW * SC_TILES) != 0:
        return jnp.take(src, idx, axis=0)
    mesh = plsc.VectorSubcoreMesh(core_axis_name="core", subcore_axis_name="subcore")

    @pl.kernel(out_type=jax.ShapeDtypeStruct((n, src.shape[1]), src.dtype), mesh=mesh)
    def gather_kernel(src_hbm, idx_hbm, out_hbm):
        def step(idx_vmem, out_vmem):
            pltpu.sync_copy(src_hbm.at[idx_vmem.at[0]], out_vmem)

        pltpu.emit_pipeline(
            step, grid=(n // SC_WINDOW,),
            in_specs=[pl.BlockSpec((1, SC_WINDOW), index_map=lambda i: (0, i))],
            out_specs=[pl.BlockSpec((SC_WINDOW, src.shape[1]), index_map=lambda i: (i, 0))],
            core_axis_name=("core", "subcore"),
            dimension_semantics=(pltpu.PARALLEL,),
        )(idx_hbm, out_hbm)

    return gather_kernel(src, idx.reshape(1, n))


def _scatter_rows(src, idx, n_out):
    n = idx.shape[0]
    m = src.shape[0] // 2
    reps = n // (2 * m)
    if n % (SC_WINDOW * SC_TILES) != 0 or m % SC_WINDOW != 0:
        rows = jnp.arange(n, dtype=jnp.int32)
        src_row = (rows // (reps * m)) * m + rows % m
        return jnp.zeros((n_out, src.shape[1]), src.dtype).at[idx].set(jnp.take(src, src_row, axis=0))
    tiles = m // SC_WINDOW
    mesh = plsc.VectorSubcoreMesh(core_axis_name="core", subcore_axis_name="subcore")

    @pl.kernel(out_type=jax.ShapeDtypeStruct((n_out, src.shape[1]), src.dtype), mesh=mesh, scratch_types=[])
    def scatter_kernel(src_hbm, idx_hbm, out_hbm):
        def step(src_vmem, idx_vmem):
            pltpu.sync_copy(src_vmem, out_hbm.at[idx_vmem.at[0]])

        pltpu.emit_pipeline(
            step, grid=(n // SC_WINDOW,),
            in_specs=[pl.BlockSpec((SC_WINDOW, src.shape[1]),
                                   index_map=lambda i: ((i // (reps * tiles)) * tiles + i % tiles, 0)),
                      pl.BlockSpec((1, SC_WINDOW), index_map=lambda i: (0, i))],
            out_specs=[],
            core_axis_name=("core", "subcore"),
            dimension_semantics=(pltpu.PARALLEL,),
        )(src_hbm, idx_hbm)

    return scatter_kernel(src, idx.reshape(1, n))


PER_GROUP = N_EXPERTS // N_GROUPS
PICKED = -3e38


def _ln_rows(v, g, b):
    mu = jnp.mean(v, axis=-1, keepdims=True)
    c = v - mu
    var = jnp.mean(c * c, axis=-1, keepdims=True)
    return c * lax.rsqrt(var + LN_EPS) * g + b


def _first_max(v, ids, axes, sentinel):
    best = v
    for a in axes:
        best = jnp.max(best, axis=a, keepdims=True)
    first = jnp.where(v == best, ids, sentinel)
    for a in axes:
        first = jnp.min(first, axis=a, keepdims=True)
    return best, first


def _sum_axes(v, axes):
    for a in axes:
        v = jnp.sum(v, axis=a, keepdims=True)
    return v


def _moe_pre_body(x_ref, mix_ref, g_ref, b_ref, wr_ref, br_ref, wgu_ref, wdn_ref,
                  x1_ref, xp_ref, sh_ref, eidx_ref, gate_ref, rank_ref, cnt_ref, run_ref):
    f32, bf16 = jnp.float32, jnp.bfloat16
    tm = x_ref.shape[0]

    @pl.when(pl.program_id(0) == 0)
    def _():
        run_ref[...] = jnp.zeros(run_ref.shape, f32)

    x1 = _ln_rows(ALPHA * x_ref[...] + mix_ref[...], g_ref[...], b_ref[...])
    x1_ref[...] = x1
    x1b = x1.astype(bf16)
    xp_ref[0], xp_ref[1] = _pack_rows(x1)

    h = _dot(x1b, wgu_ref[...])
    d_sh = h.shape[1] // 2
    act = (jax.nn.silu(h[:, :d_sh]) * h[:, d_sh:]).astype(bf16)
    sh_ref[...] = _dot(act, wdn_ref[...])

    s = jax.nn.sigmoid(_dot_nt(wr_ref[...], x1b)).reshape(N_GROUPS, PER_GROUP, tm)
    sb = s + br_ref[...].reshape(N_GROUPS, PER_GROUP, 1)
    shape3 = (N_GROUPS, PER_GROUP, tm)
    pid = lax.broadcasted_iota(jnp.int32, shape3, 1)
    gid = lax.broadcasted_iota(jnp.int32, (N_GROUPS, 1, tm), 0)
    eid = lax.broadcasted_iota(jnp.int32, shape3, 0) * PER_GROUP + pid
    top1, i1 = _first_max(sb, pid, (1,), PER_GROUP)
    top2 = jnp.max(jnp.where(pid == i1, PICKED, sb), axis=1, keepdims=True)
    gscore = top1 + top2
    gsel = jnp.zeros((N_GROUPS, 1, tm), f32)
    for _ in range(TOPK_GROUPS):
        _, first = _first_max(gscore, gid, (0,), N_GROUPS)
        hit = gid == first
        gsel = jnp.where(hit, 1.0, gsel)
        gscore = jnp.where(hit, PICKED, gscore)
    cand = jnp.where(gsel > 0.0, sb, -1e30)
    firsts, gates = [], []
    picked = jnp.zeros(shape3, f32)
    for _ in range(TOP_K):
        _, first = _first_max(cand, eid, (0, 1), N_EXPERTS)
        hit = eid == first
        firsts.append(first)
        gates.append(_sum_axes(jnp.where(hit, s, 0.0), (0, 1)))
        picked = jnp.where(hit, 1.0, picked)
        cand = jnp.where(hit, PICKED, cand)
    gsum = gates[0]
    for gk in gates[1:]:
        gsum = gsum + gk
    earlier = (lax.broadcasted_iota(jnp.int32, (tm, tm), 0) < lax.broadcasted_iota(jnp.int32, (tm, tm), 1))
    picked2 = picked.reshape(N_EXPERTS, tm)
    rank = run_ref[...] + _dot(picked2.astype(bf16), jnp.where(earlier, 1.0, 0.0).astype(bf16))
    run_new = run_ref[...] + jnp.sum(picked2, axis=1, keepdims=True)
    run_ref[...] = run_new
    cnt_ref[...] = jnp.broadcast_to(run_new, cnt_ref.shape)
    rank3 = rank.reshape(shape3)
    for k in range(TOP_K):
        hit = eid == firsts[k]
        eidx_ref[k:k + 1, :] = firsts[k].reshape(1, tm)
        gate_ref[k:k + 1, :] = (gates[k] / gsum * ROUTE_SCALE).reshape(1, tm)
        rank_ref[k:k + 1, :] = _sum_axes(jnp.where(hit, rank3, 0.0), (0, 1)).reshape(1, tm).astype(jnp.int32)


def _moe_pre(x, mix, g, b, w_router, b_router, w_sh_gu, w_sh_down):
    m, d = x.shape
    bf16 = jnp.bfloat16
    tm = min(m, 512)
    row = lambda i: (i, 0)
    col = lambda i: (0, i)
    fixed = lambda i: (0, 0)
    d_sh2 = w_sh_gu.shape[1]
    return pl.pallas_call(
        _moe_pre_body,
        grid=(m // tm,),
        in_specs=[pl.BlockSpec((tm, d), row), pl.BlockSpec((tm, d), row),
                  pl.BlockSpec((1, d), fixed), pl.BlockSpec((1, d), fixed),
                  pl.BlockSpec((N_EXPERTS, d), fixed), pl.BlockSpec((N_EXPERTS, 1), fixed),
                  pl.BlockSpec((d, d_sh2), fixed), pl.BlockSpec((d_sh2 // 2, d), fixed)],
        out_specs=[pl.BlockSpec((tm, d), row), pl.BlockSpec((2, tm, PACK_W), lambda i: (0, i, 0)),
                   pl.BlockSpec((tm, d), row),
                   pl.BlockSpec((TOP_K, tm), col), pl.BlockSpec((TOP_K, tm), col), pl.BlockSpec((TOP_K, tm), col),
                   pl.BlockSpec((N_EXPERTS, LANE), fixed)],
        out_shape=[jax.ShapeDtypeStruct((m, d), jnp.float32), jax.ShapeDtypeStruct((2, m, PACK_W), jnp.int32),
                   jax.ShapeDtypeStruct((m, d), jnp.float32),
                   jax.ShapeDtypeStruct((TOP_K, m), jnp.int32), jax.ShapeDtypeStruct((TOP_K, m), jnp.float32),
                   jax.ShapeDtypeStruct((TOP_K, m), jnp.int32),
                   jax.ShapeDtypeStruct((N_EXPERTS, LANE), jnp.float32)],
        scratch_shapes=[pltpu.VMEM((N_EXPERTS, 1), jnp.float32)],
        compiler_params=pltpu.CompilerParams(dimension_semantics=("arbitrary",),
                                             vmem_limit_bytes=VMEM_LIMIT),
        name="moe_pre",
    )(x, mix, g.reshape(1, d), b.reshape(1, d), w_router.T.astype(bf16), b_router.reshape(N_EXPERTS, 1),
      w_sh_gu.astype(bf16), w_sh_down.astype(bf16))


def _moe_expert_body(exp_ref, first_ref, rows_ref, xs_ref, wgu_ref, wdn_ref, y_ref, wgu_bf, wdn_bf):
    i = pl.program_id(0)
    bf16 = jnp.bfloat16

    @pl.when(first_ref[i] == 1)
    def _():
        wgu_bf[...] = wgu_ref[0, 0].astype(bf16)
        wdn_bf[...] = wdn_ref[0, 0].astype(bf16)

    @pl.when(rows_ref[i] > 0)
    def _():
        live = lax.broadcasted_iota(jnp.int32, (xs_ref.shape[1], 1), 0) < rows_ref[i]
        h = None
        for hw in range(2):
            for q, xq in enumerate(_unpack_words(xs_ref[hw])):
                r0 = (2 * hw + q) * PACK_W
                part = _dot(jnp.where(live, xq, 0.0).astype(bf16), wgu_bf[r0:r0 + PACK_W, :])
                h = part if h is None else h + part
        d_e = h.shape[1] // 2
        act = (jax.nn.silu(h[:, :d_e]) * h[:, d_e:]).astype(bf16)
        y_ref[0], y_ref[1] = _pack_rows(_dot(act, wdn_bf[...]))

    @pl.when(rows_ref[i] == 0)
    def _():
        y_ref[...] = jnp.zeros(y_ref.shape, y_ref.dtype)


def _moe_experts(xs, blk_exp, blk_first, blk_rows, w_exp_gu, w_exp_down, layer, bm):
    n_slots = xs.shape[1]
    d = w_exp_gu.shape[2]
    n_blk = n_slots // bm
    d_e2 = w_exp_gu.shape[3]
    words = lambda i, e, f, a: (0, i, 0)
    grid_spec = pltpu.PrefetchScalarGridSpec(
        num_scalar_prefetch=3,
        grid=(n_blk,),
        in_specs=[pl.BlockSpec((2, bm, PACK_W), words),
                  pl.BlockSpec((1, 1, d, d_e2), lambda i, e, f, a: (layer, e[i], 0, 0)),
                  pl.BlockSpec((1, 1, d_e2 // 2, d), lambda i, e, f, a: (layer, e[i], 0, 0))],
        out_specs=pl.BlockSpec((2, bm, PACK_W), words),
        scratch_shapes=[pltpu.VMEM((d, d_e2), jnp.bfloat16), pltpu.VMEM((d_e2 // 2, d), jnp.bfloat16)])
    return pl.pallas_call(
        _moe_expert_body,
        grid_spec=grid_spec,
        out_shape=jax.ShapeDtypeStruct((2, n_slots, PACK_W), jnp.int32),
        compiler_params=pltpu.CompilerParams(dimension_semantics=("arbitrary",),
                                             vmem_limit_bytes=VMEM_LIMIT),
        name="moe_experts",
    )(blk_exp, blk_first, blk_rows, xs, w_exp_gu, w_exp_down)


def _combine_ln_body(x_ref, yg_ref, gt_ref, sh_ref, g_ref, b_ref, o_ref):
    gt = gt_ref[...]
    parts = []
    for hw in range(2):
        lo_acc = hi_acc = None
        for k in range(TOP_K):
            lo, hi = _unpack_words(yg_ref[hw, k])
            gk = gt[:, k:k + 1]
            lo_acc = lo * gk if lo_acc is None else lo_acc + lo * gk
            hi_acc = hi * gk if hi_acc is None else hi_acc + hi * gk
        parts += [lo_acc, hi_acc]
    routed = jnp.concatenate(parts, axis=1)
    o_ref[...] = _ln_rows(ALPHA * x_ref[...] + (routed + sh_ref[...]), g_ref[...], b_ref[...])


def _combine_ln(x, yg, gate_t, shared, g, b):
    m, d = x.shape
    tm = min(m, 256)
    row = lambda i: (i, 0)
    fixed = lambda i: (0, 0)
    return pl.pallas_call(
        _combine_ln_body,
        grid=(m // tm,),
        in_specs=[pl.BlockSpec((tm, d), row), pl.BlockSpec((2, TOP_K, tm, PACK_W), lambda i: (0, 0, i, 0)),
                  pl.BlockSpec((tm, TOP_K), row), pl.BlockSpec((tm, d), row),
                  pl.BlockSpec((1, d), fixed), pl.BlockSpec((1, d), fixed)],
        out_specs=pl.BlockSpec((tm, d), row),
        out_shape=jax.ShapeDtypeStruct((m, d), jnp.float32),
        compiler_params=pltpu.CompilerParams(dimension_semantics=("arbitrary",)),
        name="combine_ln",
    )(x, yg, gate_t, shared, g.reshape(1, d), b.reshape(1, d))


def _moe_layer(x, mix, ln1_g, ln1_b, ln2_g, ln2_b, w_router, b_router, w_exp_gu, w_exp_down, layer,
               w_sh_gu, w_sh_down):
    m, d = x.shape
    x1, xp, shared, eidx, gate8, rank8, counts = _moe_pre(x, mix, ln1_g, ln1_b, w_router, b_router,
                                                           w_sh_gu, w_sh_down)
    bm = 512 if m * TOP_K >= 512 * N_EXPERTS else MOE_BLK
    n_blk = (m * TOP_K) // bm + N_EXPERTS
    counts = counts[:, 0].astype(jnp.int32)
    padded = (counts + bm - 1) // bm * bm
    pad_end = jnp.cumsum(padded)
    pad_start = pad_end - padded
    start_of = jnp.sum(jnp.where(eidx[:, :, None] == jnp.arange(N_EXPERTS), pad_start, 0), axis=-1)
    dest = (start_of + rank8).reshape(-1)
    blk_start = jnp.arange(n_blk, dtype=jnp.int32) * bm
    blk_exp = jnp.minimum(jnp.sum(pad_end[None, :] <= blk_start[:, None], axis=1), N_EXPERTS - 1).astype(jnp.int32)
    blk_rows = jnp.clip(counts[blk_exp] - (blk_start - pad_start[blk_exp]), 0, bm).astype(jnp.int32)
    blk_first = jnp.concatenate([jnp.ones((1,), jnp.int32), (blk_exp[1:] != blk_exp[:-1]).astype(jnp.int32)])
    n_slots = n_blk * bm
    xs = _scatter_rows(xp.reshape(2 * m, PACK_W), jnp.concatenate([dest, dest + n_slots]), 2 * n_slots)
    y = _moe_experts(xs.reshape(2, n_slots, PACK_W), blk_exp, blk_first, blk_rows, w_exp_gu, w_exp_down, layer, bm)
    yg = _gather_rows(y.reshape(2 * n_slots, PACK_W), jnp.concatenate([dest, dest + n_slots]))
    return _combine_ln(x1, yg.reshape(2, TOP_K, m, PACK_W), gate8.T, shared, ln2_g, ln2_b)


def _trunk(x, pos, gla_state, nsa_cache, page_table, win_buf, conv_buf,
           w_in_ab, w_gla_gate, b_gla_gate, gla_norm_g, w_cmp_pool, w_out_ab,
           w_pw1, b_pw1, w_dw, b_dw, conv_ln_g, conv_ln_b, w_pw2, b_pw2,
           ln_g, ln_b, w_router, b_router, w_exp_gu, w_exp_down, w_sh_gu, w_sh_down):
    new_gla, new_rows, new_win, new_conv = [], [], [], []
    for layer in range(DEPTH):
        i = layer // 2
        if layer % 2 == 0:
            mix, s_a, rows, win = _ab_mixer(
                x, pos, w_in_ab[i], w_gla_gate[i], b_gla_gate[i], gla_norm_g[i], w_cmp_pool[i], w_out_ab[i],
                None if gla_state is None else gla_state[i],
                None if nsa_cache is None else nsa_cache[i], page_table,
                None if win_buf is None else win_buf[i])
            new_gla.append(s_a)
            new_rows.append(rows)
            new_win.append(win)
        else:
            mix, cb = _conv_module(x, None if conv_buf is None else conv_buf[i], w_pw1[i], b_pw1[i],
                                   w_dw[i], b_dw[i], conv_ln_g[i], conv_ln_b[i], w_pw2[i], b_pw2[i])
            new_conv.append(cb)
        bsz, t_, d = x.shape
        x = _moe_layer(x.reshape(-1, d), mix.reshape(-1, d), ln_g[layer, 0], ln_b[layer, 0],
                       ln_g[layer, 1], ln_b[layer, 1], w_router[layer], b_router[layer],
                       w_exp_gu, w_exp_down, layer, w_sh_gu[layer], w_sh_down[layer]).reshape(bsz, t_, d)
    return x, jnp.stack(new_gla), jnp.stack(new_rows), jnp.stack(new_win), jnp.stack(new_conv)


def kernel(x_prompt, x_sample, state_gla, cache_nsa_kv, state_nsa_win, state_conv, page_table,
           w_in_ab, w_gla_gate, b_gla_gate, gla_norm_g, w_cmp_pool, w_out_ab,
           w_pw1, b_pw1, w_dw, b_dw, conv_ln_g, conv_ln_b, w_pw2, b_pw2,
           ln_g, ln_b, w_router, b_router, w_exp_gu, w_exp_down, w_sh_gu, w_sh_down):
    weights = (w_in_ab, w_gla_gate, b_gla_gate, gla_norm_g, w_cmp_pool, w_out_ab,
               w_pw1, b_pw1, w_dw, b_dw, conv_ln_g, conv_ln_b, w_pw2, b_pw2,
               ln_g, ln_b, w_router, b_router, w_exp_gu, w_exp_down, w_sh_gu, w_sh_down)
    past_len = page_table.shape[1] * PAGE_SIZE
    pos_p = jnp.arange(x_prompt.shape[1])
    pos_s = past_len + jnp.arange(x_sample.shape[1])
    y_prompt, gla_p, rows_p, win_p, conv_p = _trunk(x_prompt, pos_p, None, None, None, None, None, *weights)
    y_sample, gla_s, rows_s, win_s, conv_s = _trunk(x_sample, pos_s, state_gla, cache_nsa_kv, page_table,
                                                    state_nsa_win, state_conv, *weights)
    return (y_prompt, y_sample, gla_p, gla_s, rows_p, rows_s, win_p, win_s, conv_p, conv_s)
```

```python
import functools
import math

import jax
import jax.numpy as jnp
import numpy as np
from jax import lax
from jax.experimental import pallas as pl
from jax.experimental.pallas import tpu as pltpu
from jax.experimental.pallas import tpu_sc as plsc

D_MODEL = 1024
DEPTH = 2
PAGE_SIZE = 128

GLA_HEADS = 4
GLA_DV = D_MODEL // 2 // GLA_HEADS
GLA_DK = GLA_DV // 2
GLA_RANK = 16
GLA_TAU = 16.0

NSA_HEADS = 8
NSA_KV_HEADS = 2
NSA_GROUP = NSA_HEADS // NSA_KV_HEADS
HEAD_DIM = D_MODEL // 2 // NSA_HEADS
CMP_BLK = 32
CMP_STRIDE = 16
SEL_BLK = 64
SEL_TOPN = 16
WINDOW = 512
Q_BLK = 128
FORCE_BONUS = 100.0
ROPE_DIM = HEAD_DIM // 4
ROPE_THETA = 500000.0

GLA_SIZES = (GLA_HEADS * GLA_DK, GLA_HEADS * GLA_DK, GLA_HEADS * GLA_DV, GLA_HEADS * GLA_DV, GLA_RANK)
NSA_SIZES = (NSA_HEADS * HEAD_DIM, 6 * NSA_KV_HEADS * HEAD_DIM, 3 * NSA_HEADS)

CONV_W = 31
D_CONV = D_MODEL

N_EXPERTS = 64
N_GROUPS = 8
TOPK_GROUPS = 4
TOP_K = 8
D_EXPERT = 256
ROUTE_SCALE = 2.5
MOE_BLK = 128

ALPHA = (2 * DEPTH) ** 0.25
LN_EPS = 1e-5

LANE = 128
SUBLANES = 8
V7X_VMEM_BYTES = 64 * 1024 * 1024
VMEM_LIMIT = V7X_VMEM_BYTES * 3 // 4


def _dot(a, b):
    return jnp.dot(a, b, preferred_element_type=jnp.float32)


def _dot_nt(a, b):
    return lax.dot_general(a, b, (((1,), (1,)), ((), ())), preferred_element_type=jnp.float32)


def _mm_body(x_ref, w_ref, o_ref):
    o_ref[...] = _dot(x_ref[...].astype(jnp.bfloat16), w_ref[...].astype(jnp.bfloat16))


def _mm(x, w, keep_pad=False):
    m, k = x.shape
    n = w.shape[1]
    n_pad = -(-n // LANE) * LANE
    w = w.astype(jnp.bfloat16)
    if n_pad != n:
        w = jnp.pad(w, ((0, 0), (0, n_pad - n)))
    tm = min(m, 512)
    out = pl.pallas_call(
        _mm_body,
        grid=(m // tm,),
        in_specs=[pl.BlockSpec((tm, k), lambda i: (i, 0)),
                  pl.BlockSpec((k, n_pad), lambda i: (0, 0))],
        out_specs=pl.BlockSpec((tm, n_pad), lambda i: (i, 0)),
        out_shape=jax.ShapeDtypeStruct((m, n_pad), jnp.float32),
        compiler_params=pltpu.CompilerParams(dimension_semantics=("arbitrary",),
                                             vmem_limit_bytes=VMEM_LIMIT),
        name="mm",
    )(x, w)
    return out if keep_pad or n_pad == n else out[:, :n]


def _mm_pair_body(a_ref, b_ref, w_ref, o_ref):
    ka = a_ref.shape[1]
    o_ref[...] = (_dot(a_ref[...].astype(jnp.bfloat16), w_ref[0:ka, :])
                  + _dot(b_ref[...].astype(jnp.bfloat16), w_ref[ka:, :]))


def _mm_pair(a, b, w):
    m, ka = a.shape
    kb = b.shape[1]
    n = w.shape[1]
    tm = min(m, 512)
    return pl.pallas_call(
        _mm_pair_body,
        grid=(m // tm,),
        in_specs=[pl.BlockSpec((tm, ka), lambda i: (i, 0)), pl.BlockSpec((tm, kb), lambda i: (i, 0)),
                  pl.BlockSpec((ka + kb, n), lambda i: (0, 0))],
        out_specs=pl.BlockSpec((tm, n), lambda i: (i, 0)),
        out_shape=jax.ShapeDtypeStruct((m, n), jnp.float32),
        compiler_params=pltpu.CompilerParams(dimension_semantics=("arbitrary",)),
        name="mm_pair",
    )(a, b, w.astype(jnp.bfloat16))


def _partial_rope(x, pos):
    half = ROPE_DIM // 2
    inv_freq = jnp.power(ROPE_THETA, -jnp.arange(half, dtype=jnp.float32) / half)
    ang = pos.astype(jnp.float32)[:, None] * inv_freq
    ang = ang.reshape(ang.shape[0], *([1] * (x.ndim - 3)), half)
    cos, sin = jnp.cos(ang), jnp.sin(ang)
    x1 = x[..., :half]
    x2 = x[..., half:ROPE_DIM]
    rot = jnp.concatenate([x1 * cos - x2 * sin, x2 * cos + x1 * sin], -1)
    return jnp.concatenate([rot, x[..., ROPE_DIM:]], -1)


NSA_ROWS = NSA_GROUP * Q_BLK
SEL_KT = 1024
N_SELB = 128
MASKED = -1e9
WIN_KEYS = WINDOW + Q_BLK
KK_W = 2 * HEAD_DIM + N_SELB


def _nsa_prompt_body(qr_ref, qo_ref, kc_ref, vct_ref, kk_ref, vvt_ref, g_ref, o_ref,
                     imp_ref, m_ref, l_ref, acc_ref):
    f32, bf16 = jnp.float32, jnp.bfloat16
    qb = pl.program_id(2)
    q0 = qb * Q_BLK
    qr_t = qr_ref[0, 0, 0]
    qo_t = qo_ref[0, 0, 0]
    n_cmp = kc_ref.shape[2]

    s_c = _dot(kc_ref[0, 0], qr_t)
    n_idx = lax.broadcasted_iota(jnp.int32, (n_cmp, NSA_ROWS), 0)
    qpos_c = q0 + (lax.broadcasted_iota(jnp.int32, (n_cmp, NSA_ROWS), 1) & (Q_BLK - 1))
    cmask = (n_idx * CMP_STRIDE + (CMP_BLK - 1)) <= qpos_c
    s_c = jnp.where(cmask, s_c, MASKED)
    m_c = jnp.max(s_c, axis=0, keepdims=True)
    p_c = jnp.where(cmask, jnp.exp(s_c - m_c), 0.0)
    p_c = p_c / jnp.maximum(jnp.sum(p_c, axis=0, keepdims=True), 1e-30)
    o_ct = _dot(vct_ref[0, 0], p_c.astype(bf16))

    imp = (p_c[:, 0:Q_BLK] + p_c[:, Q_BLK:2 * Q_BLK]) + p_c[:, 2 * Q_BLK:3 * Q_BLK] + p_c[:, 3 * Q_BLK:]
    imp_ref[0:8, :] = jnp.zeros((8, Q_BLK), f32)
    imp_ref[8:8 + n_cmp, :] = imp
    ratio = SEL_BLK // CMP_STRIDE
    n_selb = n_cmp // ratio
    imp_s = imp_ref[pl.ds(7, n_selb, stride=ratio), :]
    for r in range(ratio):
        imp_s = imp_s + imp_ref[pl.ds(8 + r, n_selb, stride=ratio), :]
    blk = lax.broadcasted_iota(jnp.int32, (n_selb, Q_BLK), 0)
    qpos_s = q0 + lax.broadcasted_iota(jnp.int32, (n_selb, Q_BLK), 1)
    cur = lax.shift_right_logical(qpos_s, int(math.log2(SEL_BLK)))
    valid = blk * SEL_BLK <= qpos_s
    forced = (blk == 0) | (blk == cur) | (blk == cur - 1)
    score = jnp.where(valid, imp_s + jnp.where(forced, FORCE_BONUS, 0.0), -1e30)
    picked = jnp.zeros((n_selb, Q_BLK), f32)
    for _ in range(SEL_TOPN):
        best = jnp.max(score, axis=0, keepdims=True)
        first = jnp.min(jnp.where(score == best, blk, n_selb), axis=0, keepdims=True)
        hit = blk == first
        picked = jnp.where(hit, 1.0, picked)
        score = jnp.where(hit, -3e38, score)
    selb_t = jnp.where(valid, picked, 0.0)
    if n_selb < N_SELB:
        selb_t = jnp.concatenate([selb_t, jnp.zeros((N_SELB - n_selb, Q_BLK), f32)], axis=0)
    selb_t = ((selb_t - 1.0) * (-MASKED)).astype(bf16)
    selb_t = jnp.concatenate([selb_t] * NSA_GROUP, axis=1)

    zeros_q = jnp.zeros((HEAD_DIM, NSA_ROWS), bf16)
    q_sel = jnp.concatenate([qo_t, zeros_q, selb_t], axis=0)
    q_win = jnp.concatenate([zeros_q, qo_t, jnp.zeros((N_SELB, NSA_ROWS), bf16)], axis=0)
    qpos_r = q0 + (lax.broadcasted_iota(jnp.int32, (1, NSA_ROWS), 1) & (Q_BLK - 1))

    def v_tiles(first, count):
        return jnp.concatenate([vvt_ref[0, 0, first + j] for j in range(count)], axis=1)

    m_ref[...] = jnp.full(m_ref.shape, MASKED, f32)
    l_ref[...] = jnp.zeros(l_ref.shape, f32)
    acc_ref[...] = jnp.zeros(acc_ref.shape, f32)

    def sel_tile(k0, kt, causal):
        half = kt // 2
        scores = []
        for part in range(2):
            kp0 = k0 + part * half
            s = _dot(kk_ref[0, 0, pl.ds(kp0, half), :], q_sel)
            if causal:
                kpos = kp0 + lax.broadcasted_iota(jnp.int32, (half, NSA_ROWS), 0)
                s = jnp.where(kpos <= qpos_r, s, MASKED)
            scores.append(s)
        m_run, l_run, acc = m_ref[...], l_ref[...], acc_ref[...]
        for part, s in enumerate(scores):
            m_new = jnp.maximum(m_run, jnp.max(s, axis=0, keepdims=True))
            alpha = jnp.exp(m_run - m_new)
            p = jnp.exp(s - m_new)
            l_run = alpha * l_run + jnp.sum(p, axis=0, keepdims=True)
            vt = v_tiles((k0 + part * half) // Q_BLK, half // Q_BLK)
            acc = alpha * acc + _dot(vt, p.astype(bf16))
            m_run = m_new
        m_ref[...], l_ref[...], acc_ref[...] = m_run, l_run, acc

    n_full = q0 // SEL_KT

    def full_step(t, c):
        sel_tile(pl.multiple_of(t * SEL_KT, SEL_KT), SEL_KT, False)
        return c

    lax.fori_loop(0, n_full, full_step, 0)
    sel_tile(pl.multiple_of(n_full * SEL_KT, SEL_KT), SEL_KT, True)
    o_st = acc_ref[0:HEAD_DIM, :] / l_ref[...]

    w0 = pl.multiple_of(jnp.maximum(q0 - WINDOW, 0), Q_BLK)
    s_w = _dot(kk_ref[0, 0, pl.ds(w0, WIN_KEYS), :], q_win)
    kpos_w = w0 + lax.broadcasted_iota(jnp.int32, (WIN_KEYS, NSA_ROWS), 0)
    s_w = jnp.where((kpos_w <= qpos_r) & (kpos_w > qpos_r - WINDOW), s_w, MASKED)
    p_w = jnp.exp(s_w - jnp.max(s_w, axis=0, keepdims=True))
    l_w = jnp.sum(p_w, axis=0, keepdims=True)
    acc_w = _dot(v_tiles(w0 // Q_BLK, WIN_KEYS // Q_BLK), p_w.astype(bf16))
    o_wt = acc_w[HEAD_DIM:2 * HEAD_DIM, :] / l_w

    g = g_ref[0, 0, 0]
    out_t = g[0:1, :] * o_ct + g[1:2, :] * o_st + g[2:3, :] * o_wt
    o_ref[0] = jnp.concatenate([out_t[:, g_ * Q_BLK:(g_ + 1) * Q_BLK] for g_ in range(NSA_GROUP)], axis=0).T


def _nsa_prompt(qr, qo, gt, kc_p, vct, kk, vvt):
    bsz, _, nqb = qr.shape[:3]
    t_ = nqb * Q_BLK
    n_cmp = kc_p.shape[2]
    per_blk = lambda b, h, i: (b, h, i, 0, 0)
    per_head = lambda b, h, i: (b, h, 0, 0)
    return pl.pallas_call(
        _nsa_prompt_body,
        grid=(bsz, NSA_KV_HEADS, nqb),
        in_specs=[pl.BlockSpec((1, 1, 1, HEAD_DIM, NSA_ROWS), per_blk),
                  pl.BlockSpec((1, 1, 1, HEAD_DIM, NSA_ROWS), per_blk),
                  pl.BlockSpec((1, 1, n_cmp, HEAD_DIM), per_head),
                  pl.BlockSpec((1, 1, HEAD_DIM, n_cmp), per_head),
                  pl.BlockSpec((1, 1, t_, KK_W), per_head),
                  pl.BlockSpec((1, 1, nqb, 2 * HEAD_DIM, Q_BLK), lambda b, h, i: (b, h, 0, 0, 0)),
                  pl.BlockSpec((1, 1, 1, 3, NSA_ROWS), per_blk)],
        out_specs=pl.BlockSpec((1, Q_BLK, NSA_GROUP * HEAD_DIM), lambda b, h, i: (b, i, h)),
        out_shape=jax.ShapeDtypeStruct((bsz, t_, NSA_HEADS * HEAD_DIM), jnp.float32),
        scratch_shapes=[pltpu.VMEM((8 + n_cmp, Q_BLK), jnp.float32),
                        pltpu.VMEM((1, NSA_ROWS), jnp.float32),
                        pltpu.VMEM((1, NSA_ROWS), jnp.float32),
                        pltpu.VMEM((2 * HEAD_DIM, NSA_ROWS), jnp.float32)],
        compiler_params=pltpu.CompilerParams(
            dimension_semantics=("arbitrary", "arbitrary", "arbitrary"),
            vmem_limit_bytes=VMEM_LIMIT),
        name="nsa_prompt",
    )(qr, qo, kc_p, vct, kk, vvt, gt)


GLA_SUB = 16
GLA_UNROLL = 8
GLA_QK = GLA_HEADS * GLA_DK
GLA_V = GLA_HEADS * GLA_DV


def _dot_tn(a, b):
    return lax.dot_general(a, b, (((0,), (0,)), ((), ())), preferred_element_type=jnp.float32)


def _gla_body(q_ref, k_ref, v_ref, gr_ref, glr_ref, wg_ref, bg_ref, ng_ref, s0_ref, exp_ref,
              o_ref, sfin_ref, st_ref, b_ref, qd_ref, *, t_valid):
    f32, bf16 = jnp.float32, jnp.bfloat16
    tt = q_ref.shape[1]
    ti = pl.program_id(1)

    @pl.when(ti == 0)
    def _():
        st_ref[...] = s0_ref[0]

    row = lax.broadcasted_iota(jnp.int32, (tt, 1), 0)
    z = _dot(glr_ref[0][:, :GLA_RANK].astype(bf16), wg_ref[...]) + bg_ref[...]
    la = (jnp.minimum(z, 0.0) - jnp.log1p(jnp.exp(-jnp.abs(z)))) * (1.0 / GLA_TAU)
    la = jnp.where(ti * tt + row < t_valid, la, 0.0)
    seg = row & (GLA_SUB - 1)
    b = la
    for s in (1, 2, 4, 8):
        b = b + jnp.where(seg >= s, pltpu.roll(b, s, axis=0), 0.0)
    q = q_ref[0] * (GLA_DK ** -0.5)
    k = k_ref[0]
    v = v_ref[0]
    o = _dot((q * k).astype(bf16), exp_ref[...]) * v
    for d in range(1, GLA_SUB):
        decay = jnp.exp(jnp.minimum(b - pltpu.roll(b, d, axis=0), 0.0))
        w = jnp.where(seg >= d, q * pltpu.roll(k, d, axis=0) * decay, 0.0)
        o = o + _dot(w.astype(bf16), exp_ref[...]) * pltpu.roll(v, d, axis=0)
    o_ref[0] = o
    b_ref[...] = b
    qd_ref[...] = (q * jnp.exp(b)).astype(bf16)

    def block_step(c, carry):
        rows = pl.ds(pl.multiple_of(c * GLA_SUB, GLA_SUB), GLA_SUB)
        qd = qd_ref[rows, :]
        bc = b_ref[rows, :]
        bl = bc[GLA_SUB - 1:GLA_SUB, :]
        kc = (k_ref[0, rows, :] * jnp.exp(bl - bc)).astype(bf16)
        keep = jnp.exp(bl)
        vb = v_ref[0, rows, :].astype(bf16)
        outs = []
        for h in range(GLA_HEADS):
            dk = slice(h * GLA_DK, (h + 1) * GLA_DK)
            dv = slice(h * GLA_DV, (h + 1) * GLA_DV)
            st = st_ref[dv, :]
            outs.append(_dot_nt(qd[:, dk], st.astype(bf16)))
            st_ref[dv, :] = st * keep[:, dk] + _dot_tn(vb[:, dv], kc[:, dk])
        o_ref[0, rows, :] += jnp.concatenate(outs, axis=1)
        return carry

    lax.fori_loop(0, tt // GLA_SUB, block_step, 0, unroll=GLA_UNROLL)
    sfin_ref[0] = st_ref[...]
    gr = gr_ref[0]
    gate = gr * jax.nn.sigmoid(gr)
    for h in range(GLA_HEADS):
        cols = slice(h * GLA_DV, (h + 1) * GLA_DV)
        oh = o_ref[0, :, cols]
        ms = jnp.mean(oh * oh, axis=-1, keepdims=True)
        o_ref[0, :, cols] = oh * lax.rsqrt(ms + LN_EPS) * ng_ref[...] * gate[:, cols]


def _gla(h, w_gla_gate, b_gla_gate, gla_norm_g, gla_state):
    bsz, t_, n_in = h.shape
    tp = -(-t_ // GLA_SUB) * GLA_SUB
    if tp != t_:
        h = jnp.pad(h, ((0, 0), (0, tp - t_), (0, 0)))
    tt = min(tp, 256)
    expand = np.repeat(np.repeat(np.eye(GLA_HEADS, dtype=np.float32), GLA_DK, 0), GLA_DV, 1)
    if gla_state is None:
        s0 = jnp.zeros((bsz, GLA_V, GLA_DK), jnp.float32)
    else:
        s0 = gla_state.transpose(0, 1, 3, 2).reshape(bsz, GLA_V, GLA_DK)
    tile = lambda width, blk: pl.BlockSpec((1, tt, width), lambda b, i: (b, i, blk))
    fixed2 = lambda shape: pl.BlockSpec(shape, lambda b, i: (0, 0))
    per_b = pl.BlockSpec((1, GLA_V, GLA_DK), lambda b, i: (b, 0, 0))
    o, s_t = pl.pallas_call(
        functools.partial(_gla_body, t_valid=t_),
        grid=(bsz, tp // tt),
        in_specs=[tile(GLA_QK, 0), tile(GLA_QK, 1), tile(GLA_V, 1), tile(GLA_V, 2),
                  tile(LANE, (2 * GLA_QK + 2 * GLA_V + NSA_SIZES[0] + NSA_SIZES[1]) // LANE),
                  fixed2((GLA_RANK, GLA_QK)), fixed2((1, GLA_QK)), fixed2((1, GLA_DV)), per_b,
                  fixed2((GLA_QK, GLA_V))],
        out_specs=[pl.BlockSpec((1, tt, GLA_V), lambda b, i: (b, i, 0)), per_b],
        out_shape=[jax.ShapeDtypeStruct((bsz, tp, GLA_V), jnp.float32),
                   jax.ShapeDtypeStruct((bsz, GLA_V, GLA_DK), jnp.float32)],
        scratch_shapes=[pltpu.VMEM((GLA_V, GLA_DK), jnp.float32), pltpu.VMEM((tt, GLA_QK), jnp.float32),
                        pltpu.VMEM((tt, GLA_QK), jnp.bfloat16)],
        compiler_params=pltpu.CompilerParams(dimension_semantics=("arbitrary", "arbitrary"),
                                             vmem_limit_bytes=VMEM_LIMIT),
        name="gla",
    )(h, h, h, h, h, w_gla_gate.astype(jnp.bfloat16), b_gla_gate.reshape(1, GLA_QK),
      gla_norm_g.reshape(1, GLA_DV), s0, jnp.asarray(expand, jnp.bfloat16))
    return o[:, :t_], s_t.reshape(bsz, GLA_HEADS, GLA_DV, GLA_DK).transpose(0, 1, 3, 2)


COL_NQ = 2 * GLA_QK + 2 * GLA_V
COL_NKV = COL_NQ + NSA_SIZES[0]
COL_TAIL = COL_NKV + NSA_SIZES[1]
TAIL_GATE = GLA_RANK
_ORIG = np.cumsum((0,) + GLA_SIZES + NSA_SIZES)
IN_AB_PERM = np.concatenate([np.arange(_ORIG[0], _ORIG[4]), np.arange(_ORIG[5], _ORIG[7]),
                             np.arange(_ORIG[4], _ORIG[5]), np.arange(_ORIG[7], _ORIG[8])])
SUBS = Q_BLK // CMP_STRIDE


def _nsa_prep_body(nq_ref, kv0_ref, kv1_ref, kv2_ref, tail_ref, rc_ref, ru_ref, rd_ref, pool_ref,
                   rows_ref, win_ref, kk_ref, vvt_ref, qr_ref, qo_ref, g_ref, pooled_ref):
    bf16 = jnp.bfloat16
    q0 = pl.program_id(1) * Q_BLK
    kv_w = NSA_KV_HEADS * HEAD_DIM

    def rope(x):
        reps = x.shape[1] // LANE
        wide = lambda r: jnp.concatenate([r[...]] * reps, axis=1) if reps > 1 else r[...]
        half = ROPE_DIM // 2
        return (x * wide(rc_ref) + pltpu.roll(x, half, axis=1) * wide(ru_ref)
                + pltpu.roll(x, x.shape[1] - half, axis=1) * wide(rd_ref))

    kv0, kv1, kv2 = kv0_ref[0], kv1_ref[0], kv2_ref[0]
    k_sel, v_sel = rope(kv1[:, :kv_w]), kv1[:, kv_w:]
    k_win, v_win = rope(kv2[:, :kv_w]), kv2[:, kv_w:]
    rows_ref[0] = jnp.concatenate([kv0, k_sel, v_sel], axis=1)
    win_ref[0] = jnp.concatenate([k_win, v_win], axis=1)
    blk_id = lax.shift_right_logical(q0 + lax.broadcasted_iota(jnp.int32, (Q_BLK, N_SELB), 0),
                                     int(math.log2(SEL_BLK)))
    onehot = jnp.where(lax.broadcasted_iota(jnp.int32, (Q_BLK, N_SELB), 1) == blk_id, 1.0, 0.0).astype(bf16)
    q = nq_ref[0] * (HEAD_DIM ** -0.5)
    q_rot = rope(q)
    gates_t = jax.nn.sigmoid(tail_ref[0]).T
    for h in range(NSA_KV_HEADS):
        hs = slice(h * HEAD_DIM, (h + 1) * HEAD_DIM)
        kk_ref[0, h] = jnp.concatenate([k_sel[:, hs].astype(bf16), k_win[:, hs].astype(bf16), onehot], axis=1)
        vvt_ref[0, h, 0] = jnp.concatenate([v_sel[:, hs], v_win[:, hs]], axis=1).T.astype(bf16)
        gw = NSA_GROUP * HEAD_DIM
        for src, dst in ((q, qr_ref), (q_rot, qo_ref)):
            t = src[:, h * gw:(h + 1) * gw].T
            dst[0, h, 0] = jnp.concatenate([t[g * HEAD_DIM:(g + 1) * HEAD_DIM] for g in range(NSA_GROUP)],
                                           axis=1).astype(bf16)
        base = TAIL_GATE + h * NSA_GROUP * 3
        g_ref[0, h, 0] = jnp.concatenate(
            [jnp.concatenate([gates_t[base + 3 * g + c:base + 3 * g + c + 1] for g in range(NSA_GROUP)], axis=1)
             for c in range(3)], axis=0)
    kc_in, vc_in = kv0[:, :kv_w].astype(bf16), kv0[:, kv_w:].astype(bf16)
    pooled_ref[0] = jnp.concatenate([_dot(pool_ref[0], kc_in), _dot(pool_ref[1], kc_in),
                                     _dot(pool_ref[2], vc_in), _dot(pool_ref[3], vc_in)], axis=1)


def _nsa_prep(h, pos, w_cmp_pool):
    bsz, t_, _ = h.shape
    nqb = t_ // Q_BLK
    bf16 = jnp.bfloat16
    half = ROPE_DIM // 2
    inv_freq = jnp.power(ROPE_THETA, -jnp.arange(half, dtype=jnp.float32) / half)
    ang = pos.astype(jnp.float32)[:, None] * inv_freq
    cos, sin = jnp.cos(ang), jnp.sin(ang)
    rest = HEAD_DIM - ROPE_DIM
    z8, zr = jnp.zeros((t_, half), jnp.float32), jnp.zeros((t_, rest), jnp.float32)
    two = lambda a: jnp.concatenate([a, a], axis=1)
    rc = two(jnp.concatenate([cos, cos, jnp.ones((t_, rest), jnp.float32)], axis=1))
    ru = two(jnp.concatenate([z8, sin, zr], axis=1))
    rd = two(jnp.concatenate([-sin, z8, zr], axis=1))
    pool = _pool_matrices(w_cmp_pool)
    kv_w = NSA_KV_HEADS * HEAD_DIM
    col = lambda width, off: pl.BlockSpec((1, Q_BLK, width), lambda b, i: (b, i, off // width))
    rows_t = pl.BlockSpec((Q_BLK, LANE), lambda b, i: (i, 0))
    head4 = lambda r, c: pl.BlockSpec((1, NSA_KV_HEADS, 1, r, c), lambda b, i: (b, 0, i, 0, 0))
    return pl.pallas_call(
        _nsa_prep_body,
        grid=(bsz, nqb),
        in_specs=[col(NSA_SIZES[0], COL_NQ), col(2 * kv_w, COL_NKV), col(2 * kv_w, COL_NKV + 2 * kv_w),
                  col(2 * kv_w, COL_NKV + 4 * kv_w), col(LANE, COL_TAIL), rows_t, rows_t, rows_t,
                  pl.BlockSpec((4, SUBS, Q_BLK), lambda b, i: (0, 0, 0))],
        out_specs=[pl.BlockSpec((1, Q_BLK, 4 * kv_w), lambda b, i: (b, i, 0)),
                   pl.BlockSpec((1, Q_BLK, 2 * kv_w), lambda b, i: (b, i, 0)),
                   pl.BlockSpec((1, NSA_KV_HEADS, Q_BLK, KK_W), lambda b, i: (b, 0, i, 0)),
                   head4(2 * HEAD_DIM, Q_BLK), head4(HEAD_DIM, NSA_ROWS), head4(HEAD_DIM, NSA_ROWS),
                   head4(3, NSA_ROWS),
                   pl.BlockSpec((1, SUBS, 4 * kv_w), lambda b, i: (b, i, 0))],
        out_shape=[jax.ShapeDtypeStruct((bsz, t_, 4 * kv_w), jnp.float32),
                   jax.ShapeDtypeStruct((bsz, t_, 2 * kv_w), jnp.float32),
                   jax.ShapeDtypeStruct((bsz, NSA_KV_HEADS, t_, KK_W), bf16),
                   jax.ShapeDtypeStruct((bsz, NSA_KV_HEADS, nqb, 2 * HEAD_DIM, Q_BLK), bf16),
                   jax.ShapeDtypeStruct((bsz, NSA_KV_HEADS, nqb, HEAD_DIM, NSA_ROWS), bf16),
                   jax.ShapeDtypeStruct((bsz, NSA_KV_HEADS, nqb, HEAD_DIM, NSA_ROWS), bf16),
                   jax.ShapeDtypeStruct((bsz, NSA_KV_HEADS, nqb, 3, NSA_ROWS), jnp.float32),
                   jax.ShapeDtypeStruct((bsz, t_ // CMP_STRIDE, 4 * kv_w), jnp.float32)],
        compiler_params=pltpu.CompilerParams(dimension_semantics=("arbitrary", "arbitrary")),
        name="nsa_prep",
    )(h, h, h, h, h, rc, ru, rd, pool)


PAGE_GROUP = 16
DEC_KEYS = PAGE_GROUP * PAGE_SIZE
NEW_PAD = 8
KV_W = NSA_KV_HEADS * HEAD_DIM


def _dec_pool_body(pt_ref, *refs):
    page_refs, pool_ref, out_ref = refs[:PAGE_GROUP], refs[PAGE_GROUP], refs[PAGE_GROUP + 1]
    bf16 = jnp.bfloat16
    pages = [pr[0] for pr in page_refs]
    kc_t = jnp.concatenate([p[:KV_W] for p in pages], axis=1).astype(bf16)
    vc_t = jnp.concatenate([p[KV_W:] for p in pages], axis=1).astype(bf16)
    out_ref[0] = jnp.concatenate([_dot_nt(pool_ref[0], kc_t), _dot_nt(pool_ref[1], kc_t),
                                  _dot_nt(pool_ref[2], vc_t), _dot_nt(pool_ref[3], vc_t)], axis=1)


def _page_specs(n_pages, col_blk):
    def spec(i):
        return pl.BlockSpec((1, 2 * KV_W, PAGE_SIZE),
                            lambda b, j, pt: (pt[b * n_pages + j * PAGE_GROUP + i], col_blk, 0))
    return [spec(i) for i in range(PAGE_GROUP)]


def _dec_pool(cache, page_table, pool):
    bsz, n_pages = page_table.shape
    grid_spec = pltpu.PrefetchScalarGridSpec(
        num_scalar_prefetch=1, grid=(bsz, n_pages // PAGE_GROUP),
        in_specs=_page_specs(n_pages, 0) + [pl.BlockSpec(pool.shape, lambda b, j, pt: (0, 0, 0))],
        out_specs=pl.BlockSpec((1, PAGE_GROUP * SUBS, 4 * KV_W), lambda b, j, pt: (b, j, 0)))
    return pl.pallas_call(
        _dec_pool_body, grid_spec=grid_spec,
        out_shape=jax.ShapeDtypeStruct((bsz, n_pages * SUBS, 4 * KV_W), jnp.float32),
        compiler_params=pltpu.CompilerParams(dimension_semantics=("arbitrary", "arbitrary")),
        name="nsa_dec_pool",
    )(page_table.reshape(-1), *([cache] * PAGE_GROUP), pool)


def _dec_select_body(qr_ref, kct_ref, vc_ref, band_ref, oc_ref, selb_ref, *, qpos0, n_q, n_pick, n_blk):
    f32, bf16 = jnp.float32, jnp.bfloat16
    n_cmp = kct_ref.shape[3]
    rows = NSA_GROUP * n_q
    for sq, h in [(a, b) for a in range(qr_ref.shape[0]) for b in range(NSA_KV_HEADS)]:
        s_c = _dot(qr_ref[sq, h], kct_ref[sq, h])
        n_idx = lax.broadcasted_iota(jnp.int32, (rows, n_cmp), 1)
        qpos = qpos0 + (lax.broadcasted_iota(jnp.int32, (rows, n_cmp), 0) % n_q)
        cmask = (n_idx * CMP_STRIDE + (CMP_BLK - 1)) <= qpos
        s_c = jnp.where(cmask, s_c, MASKED)
        p_c = jnp.where(cmask, jnp.exp(s_c - jnp.max(s_c, axis=1, keepdims=True)), 0.0)
        p_c = p_c / jnp.maximum(jnp.sum(p_c, axis=1, keepdims=True), 1e-30)
        oc_ref[sq, h] = _dot(p_c.astype(bf16), vc_ref[sq, h])
        imp = p_c[0:n_q]
        for g in range(1, NSA_GROUP):
            imp = imp + p_c[g * n_q:(g + 1) * n_q]
        imp_s = jnp.zeros((n_q, N_SELB), f32)
        rem = imp
        for _ in range(3):
            part = rem.astype(bf16)
            imp_s = imp_s + _dot(part, band_ref[...])
            rem = rem - part.astype(f32)
        blk = lax.broadcasted_iota(jnp.int32, (n_q, N_SELB), 1)
        qpos_s = qpos0 + lax.broadcasted_iota(jnp.int32, (n_q, N_SELB), 0)
        cur = lax.shift_right_logical(qpos_s, int(math.log2(SEL_BLK)))
        valid = (blk * SEL_BLK <= qpos_s) & (blk < n_blk)
        forced = (blk == 0) | (blk == cur) | (blk == cur - 1)
        score = jnp.where(valid, imp_s + jnp.where(forced, FORCE_BONUS, 0.0), -1e30)
        picked = jnp.zeros((n_q, N_SELB), f32)
        for _ in range(n_pick):
            best = jnp.max(score, axis=1, keepdims=True)
            first = jnp.min(jnp.where(score == best, blk, N_SELB), axis=1, keepdims=True)
            hit = blk == first
            picked = jnp.where(hit, 1.0, picked)
            score = jnp.where(hit, -3e38, score)
        selb_ref[sq, h] = (jnp.where(valid, picked, 0.0) - 1.0) * (-MASKED)


def _dec_select(qr, kct, vc, n_q, qpos0, n_pick, n_blk):
    bsz = qr.shape[0]
    rows = NSA_GROUP * n_q
    n_cmp = kct.shape[3]
    ratio = SEL_BLK // CMP_STRIDE
    c_idx, j_idx = np.arange(n_cmp)[:, None], np.arange(N_SELB)[None, :]
    band = jnp.asarray(((c_idx >= ratio * j_idx - 1) & (c_idx <= ratio * j_idx + ratio - 1)), jnp.bfloat16)
    per_step = next(c for c in (4, 2, 1) if bsz % c == 0)
    per_b = lambda *tail: pl.BlockSpec((per_step, NSA_KV_HEADS) + tail, lambda b: (b, 0, 0, 0))
    return pl.pallas_call(
        functools.partial(_dec_select_body, qpos0=qpos0, n_q=n_q, n_pick=n_pick, n_blk=n_blk),
        grid=(bsz // per_step,),
        in_specs=[per_b(rows, HEAD_DIM), per_b(HEAD_DIM, n_cmp), per_b(n_cmp, HEAD_DIM),
                  pl.BlockSpec((n_cmp, N_SELB), lambda b: (0, 0))],
        out_specs=[per_b(rows, HEAD_DIM), per_b(n_q, N_SELB)],
        out_shape=[jax.ShapeDtypeStruct((bsz, NSA_KV_HEADS, rows, HEAD_DIM), jnp.float32),
                   jax.ShapeDtypeStruct((bsz, NSA_KV_HEADS, n_q, N_SELB), jnp.float32)],
        compiler_params=pltpu.CompilerParams(dimension_semantics=("arbitrary",)),
        name="nsa_dec_select",
    )(qr, kct, vc, band)


def _dec_attend_body(pt_ref, *refs, qpos0, n_q, past):
    page_refs = refs[:PAGE_GROUP]
    (qs_ref, qw_ref, knew_ref, vnew_ref, wbuf_ref, wnew_ref, oc_ref, g_ref,
     o_ref, m_ref, l_ref, acc_ref) = refs[PAGE_GROUP:]
    f32, bf16 = jnp.float32, jnp.bfloat16
    j = pl.program_id(1)
    n_rows = qs_ref.shape[1]

    @pl.when(j == 0)
    def _():
        m_ref[...] = jnp.full(m_ref.shape, MASKED, f32)
        l_ref[...] = jnp.zeros(l_ref.shape, f32)
        acc_ref[...] = jnp.zeros(acc_ref.shape, f32)

    def online(s, weigh):
        m_old = m_ref[...]
        m_new = jnp.maximum(m_old, jnp.max(s, axis=1, keepdims=True))
        alpha = jnp.exp(m_old - m_new)
        p = jnp.exp(s - m_new)
        l_ref[...] = alpha * l_ref[...] + jnp.sum(p, axis=1, keepdims=True)
        acc_ref[...] = alpha * acc_ref[...] + weigh(p.astype(bf16))
        m_ref[...] = m_new

    qs = qs_ref[0]
    pages = [pr[0] for pr in page_refs]
    keys_t = jnp.concatenate([p[:KV_W] for p in pages], axis=1).astype(bf16)
    vals_t = jnp.concatenate([p[KV_W:] for p in pages], axis=1).astype(bf16)
    blk_id = j * (DEC_KEYS // SEL_BLK) + lax.shift_right_logical(
        lax.broadcasted_iota(jnp.int32, (N_SELB, DEC_KEYS), 1), int(math.log2(SEL_BLK)))
    onehot_t = jnp.where(lax.broadcasted_iota(jnp.int32, (N_SELB, DEC_KEYS), 0) == blk_id, 1.0, 0.0).astype(bf16)
    online(_dot(qs, jnp.concatenate([keys_t, onehot_t], axis=0)), lambda p: _dot_nt(p, vals_t))

    @pl.when(j == pl.num_programs(1) - 1)
    def _():
        row_q = qpos0 + (lax.broadcasted_iota(jnp.int32, (n_rows, 1), 0) % n_q)
        qh = qw_ref[0]
        new_pos = past + lax.broadcasted_iota(jnp.int32, (n_rows, NEW_PAD), 1)
        new_ok = (new_pos <= row_q) & (new_pos < past + n_q)
        s_new = jnp.where(new_ok, _dot_nt(qh, knew_ref[0]), MASKED)
        online(s_new, lambda p: _dot(p, vnew_ref[0]))
        o_s = acc_ref[...] / l_ref[...]
        wbuf_t = wbuf_ref[0]
        wnew = wnew_ref[0]
        n_buf = wbuf_t.shape[1]
        s_b = _dot(qh, wbuf_t[:KV_W].astype(bf16))
        pos_b = (past - n_buf) + lax.broadcasted_iota(jnp.int32, (n_rows, n_buf), 1)
        s_b = jnp.where((pos_b > row_q - WINDOW) & (pos_b >= 0), s_b, MASKED)
        s_n = jnp.where(new_ok, _dot_nt(qh, wnew[:, :KV_W].astype(bf16)), MASKED)
        m_w = jnp.maximum(jnp.max(s_b, axis=1, keepdims=True), jnp.max(s_n, axis=1, keepdims=True))
        p_b, p_n = jnp.exp(s_b - m_w), jnp.exp(s_n - m_w)
        l_w = jnp.sum(p_b, axis=1, keepdims=True) + jnp.sum(p_n, axis=1, keepdims=True)
        o_w = (_dot_nt(p_b.astype(bf16), wbuf_t[KV_W:].astype(bf16))
               + _dot(p_n.astype(bf16), wnew[:, KV_W:].astype(bf16))) / l_w
        half = n_rows // NSA_KV_HEADS
        own = lambda a: jnp.concatenate([a[h * half:(h + 1) * half, h * HEAD_DIM:(h + 1) * HEAD_DIM]
                                         for h in range(NSA_KV_HEADS)], axis=0)
        g = g_ref[0]
        o_ref[0] = g[:, 0:1] * oc_ref[0] + g[:, 1:2] * own(o_s) + g[:, 2:3] * own(o_w)


def _dec_attend(cache, page_table, qs, qw, knew, vnew, wbuf, wnew, o_c, gates, n_q, qpos0):
    bsz, n_pages = page_table.shape
    n_rows = qs.shape[1]
    per_b = lambda *tail: pl.BlockSpec((1,) + tail, lambda b, j, pt: (b, 0, 0))
    grid_spec = pltpu.PrefetchScalarGridSpec(
        num_scalar_prefetch=1, grid=(bsz, n_pages // PAGE_GROUP),
        in_specs=_page_specs(n_pages, 1) + [
            per_b(n_rows, KV_W + N_SELB), per_b(n_rows, KV_W), per_b(NEW_PAD, KV_W), per_b(NEW_PAD, KV_W),
            per_b(2 * KV_W, wbuf.shape[2]), per_b(NEW_PAD, 2 * KV_W), per_b(n_rows, HEAD_DIM), per_b(n_rows, 3)],
        out_specs=per_b(n_rows, HEAD_DIM),
        scratch_shapes=[pltpu.VMEM((n_rows, 1), jnp.float32), pltpu.VMEM((n_rows, 1), jnp.float32),
                        pltpu.VMEM((n_rows, KV_W), jnp.float32)])
    return pl.pallas_call(
        functools.partial(_dec_attend_body, qpos0=qpos0, n_q=n_q, past=n_pages * PAGE_SIZE),
        grid_spec=grid_spec,
        out_shape=jax.ShapeDtypeStruct((bsz, n_rows, HEAD_DIM), jnp.float32),
        compiler_params=pltpu.CompilerParams(dimension_semantics=("arbitrary", "arbitrary")),
        name="nsa_dec_attend",
    )(page_table.reshape(-1), *([cache] * PAGE_GROUP), qs, qw, knew, vnew, wbuf, wnew, o_c, gates)


def _pool_matrices(w_cmp_pool, rows=Q_BLK):
    subs = rows // CMP_STRIDE
    sub = np.arange(rows) // CMP_STRIDE == np.arange(subs)[:, None]
    w_rep = jnp.tile(w_cmp_pool.reshape(2, 2, CMP_STRIDE), (1, 1, subs))
    return jnp.where(sub[None, None], w_rep[:, :, None, :], 0.0).reshape(4, subs, rows).astype(jnp.bfloat16)


def _compressed_from_pooled(pooled):
    bsz, n_sub, _ = pooled.shape
    pooled = pooled.reshape(bsz, n_sub, 4, NSA_KV_HEADS, HEAD_DIM)
    kc = pooled[:, :-1, 0] + pooled[:, 1:, 1]
    vc = pooled[:, :-1, 2] + pooled[:, 1:, 3]
    pad = lambda a: jnp.pad(a, ((0, 0), (0, 1), (0, 0), (0, 0))).transpose(0, 2, 1, 3)
    return pad(kc), pad(vc)


def _nsa_decode(q_raw, q_rot, gates, rows_full, rows_win, cache, page_table, win_buf, w_cmp_pool, past):
    bsz, n_q = q_raw.shape[:2]
    bf16 = jnp.bfloat16
    n_blk = past // SEL_BLK
    assert past % DEC_KEYS == 0 and n_blk <= N_SELB and n_q <= NEW_PAD
    scale = HEAD_DIM ** -0.5
    cache2 = cache.transpose(0, 2, 3, 4, 1).reshape(cache.shape[0], 4 * KV_W, PAGE_SIZE)
    pooled = _dec_pool(cache2, page_table, _pool_matrices(w_cmp_pool, DEC_KEYS))
    kc_p, vc_p = _compressed_from_pooled(pooled)
    rows_of = lambda a: a.transpose(0, 2, 3, 1, 4).reshape(bsz, NSA_KV_HEADS, NSA_GROUP * n_q, a.shape[-1])
    qr = rows_of((q_raw * scale).astype(bf16))
    n_pick = min(SEL_TOPN, n_blk + 1) - 1
    o_c, selb = _dec_select(qr, kc_p.transpose(0, 1, 3, 2).astype(bf16), vc_p.astype(bf16), n_q, past, n_pick, n_blk)
    qo = rows_of((q_rot * scale).astype(bf16))
    zero = jnp.zeros_like(qo[:, 0])
    qw = jnp.concatenate([jnp.concatenate([qo[:, 0], zero], -1), jnp.concatenate([zero, qo[:, 1]], -1)], axis=1)
    bias = jnp.tile(selb, (1, 1, NSA_GROUP, 1)).reshape(bsz, -1, N_SELB).astype(bf16)
    qs = jnp.concatenate([qw, bias], axis=-1)
    pad_new = lambda a: jnp.pad(a.reshape(bsz, n_q, -1), ((0, 0), (0, NEW_PAD - n_q), (0, 0)))
    knew = pad_new(rows_full[:, :, 2]).astype(bf16)
    vnew = pad_new(rows_full[:, :, 3]).astype(bf16)
    wnew = pad_new(rows_win)
    wbuf = win_buf.transpose(0, 2, 3, 4, 1).reshape(bsz, 2 * KV_W, win_buf.shape[1])
    gt = rows_of(gates).reshape(bsz, -1, 3)
    o = _dec_attend(cache2, page_table, qs, qw, knew, vnew, wbuf, wnew,
                    o_c.reshape(bsz, -1, HEAD_DIM), gt, n_q, past)
    o = o.reshape(bsz, NSA_KV_HEADS, NSA_GROUP, n_q, HEAD_DIM).transpose(0, 3, 1, 2, 4)
    return o.reshape(bsz, n_q, NSA_HEADS * HEAD_DIM)


def _ab_mixer(x, pos, w_in, w_gla_gate, b_gla_gate, gla_norm_g, w_cmp_pool, w_out,
              gla_state, nsa_cache, page_table, win_buf):
    bsz, t_, _ = x.shape
    h_in = _mm(x.reshape(bsz * t_, -1), w_in[:, IN_AB_PERM], keep_pad=True).reshape(bsz, t_, -1)
    o_a, s_a = _gla(h_in, w_gla_gate, b_gla_gate, gla_norm_g, gla_state)
    kv_w = NSA_KV_HEADS * HEAD_DIM
    if nsa_cache is None:
        rows2, win2, kk, vvt, qr, qo, gt, pooled = _nsa_prep(h_in, pos, w_cmp_pool)
        pooled = pooled.reshape(bsz, t_ // CMP_STRIDE, 4, NSA_KV_HEADS, HEAD_DIM)
        kc = pooled[:, :-1, 0] + pooled[:, 1:, 1]
        vc = pooled[:, :-1, 2] + pooled[:, 1:, 3]
        kc_p = jnp.pad(kc, ((0, 0), (0, 1), (0, 0), (0, 0))).transpose(0, 2, 1, 3).astype(jnp.bfloat16)
        vct = jnp.pad(vc, ((0, 0), (0, 1), (0, 0), (0, 0))).transpose(0, 2, 3, 1).astype(jnp.bfloat16)
        o_b = _nsa_prompt(qr, qo, gt, kc_p, vct, kk, vvt)
        rows_full = rows2.reshape(bsz, t_, 4, NSA_KV_HEADS, HEAD_DIM)
        new_win = win2[:, -min(WINDOW, t_):].reshape(bsz, -1, 2, NSA_KV_HEADS, HEAD_DIM)
    else:
        nq = h_in[..., COL_NQ:COL_NKV]
        nkv = h_in[..., COL_NKV:COL_TAIL]
        ngate = h_in[..., COL_TAIL + TAIL_GATE:COL_TAIL + TAIL_GATE + NSA_SIZES[2]]
        q_raw = nq.reshape(bsz, t_, NSA_KV_HEADS, NSA_GROUP, HEAD_DIM)
        q_rot = _partial_rope(q_raw, pos)
        kv = nkv.reshape(bsz, t_, 6, NSA_KV_HEADS, HEAD_DIM)
        k_sel = _partial_rope(kv[:, :, 2], pos)
        k_win = _partial_rope(kv[:, :, 4], pos)
        rows_full = jnp.stack([kv[:, :, 0], kv[:, :, 1], k_sel, kv[:, :, 3]], axis=2)
        rows_win = jnp.stack([k_win, kv[:, :, 5]], axis=2)
        gates = jax.nn.sigmoid(ngate).reshape(bsz, t_, NSA_KV_HEADS, NSA_GROUP, 3)
        past_len = page_table.shape[1] * PAGE_SIZE
        o_b = _nsa_decode(q_raw, q_rot, gates, rows_full, rows_win, nsa_cache, page_table, win_buf,
                          w_cmp_pool, past_len)
        w_buf = win_buf.shape[1]
        kw = jnp.concatenate([win_buf, rows_win], axis=1)
        new_win = kw[:, -w_buf:]
    y = _mm_pair(o_a.reshape(bsz * t_, -1), o_b.reshape(bsz * t_, -1), w_out).reshape(bsz, t_, -1)
    return y, s_a, rows_full, new_win


CONV_HALO = 32
CONV_LEAD = CONV_HALO - (CONV_W - 1)


def _conv_body(x_ref, buf0_ref, w1_ref, b1_ref, wdw_ref, bdw_ref, g_ref, b_ref, w2_ref, b2_ref,
               o_ref, tail_ref, ext_ref, z_ref, *, t_last):
    bf16 = jnp.bfloat16
    tt = x_ref.shape[1]
    i = pl.program_id(1)

    @pl.when(i == 0)
    def _():
        ext_ref[0:CONV_HALO, :] = buf0_ref[0]
        ext_ref[CONV_HALO + tt:CONV_HALO + tt + SUBLANES, :] = jnp.zeros((SUBLANES, D_CONV), jnp.float32)

    h = _dot(x_ref[0].astype(bf16), w1_ref[...]) + b1_ref[...]
    ext_ref[CONV_HALO:CONV_HALO + tt, :] = h[:, :D_CONV] * jax.nn.sigmoid(h[:, D_CONV:])
    c = jnp.zeros((tt, D_CONV), jnp.float32) + bdw_ref[...]
    for r in range(SUBLANES):
        z = None
        for a in range(CONV_HALO // SUBLANES + 1):
            k = SUBLANES * a + r - CONV_LEAD
            if 0 <= k < CONV_W:
                term = ext_ref[SUBLANES * a:SUBLANES * a + tt + SUBLANES, :] * wdw_ref[k:k + 1, :]
                z = term if z is None else z + term
        if r == 0:
            c = c + z[:tt]
        else:
            z_ref[...] = z
            c = c + z_ref[pl.ds(r, tt), :]
    c = _ln_rows(c, g_ref[...], b_ref[...])
    c = c * jax.nn.sigmoid(c)
    o_ref[0] = _dot(c.astype(bf16), w2_ref[...]) + b2_ref[...]
    tail_ref[0] = ext_ref[t_last:t_last + CONV_HALO, :]
    ext_ref[0:CONV_HALO, :] = ext_ref[tt:tt + CONV_HALO, :]


def _conv_module(x, conv_buf, w_pw1, b_pw1, w_dw, b_dw, ln_g, ln_b, w_pw2, b_pw2):
    bsz, t_, d = x.shape
    bf16 = jnp.bfloat16
    tp = -(-t_ // 8) * 8
    tt = min(tp, 256)
    n_t = tp // tt
    if tp != t_:
        x = jnp.pad(x, ((0, 0), (0, tp - t_), (0, 0)))
    if conv_buf is None:
        buf0 = jnp.zeros((bsz, CONV_HALO, D_CONV), jnp.float32)
    else:
        buf0 = jnp.pad(conv_buf, ((0, 0), (CONV_LEAD, 0), (0, 0)))
    fixed = lambda shape: pl.BlockSpec(shape, lambda b, i: (0,) * len(shape))
    per_b = pl.BlockSpec((1, CONV_HALO, D_CONV), lambda b, i: (b, 0, 0))
    out, tail = pl.pallas_call(
        functools.partial(_conv_body, t_last=t_ - (n_t - 1) * tt),
        grid=(bsz, n_t),
        in_specs=[pl.BlockSpec((1, tt, d), lambda b, i: (b, i, 0)), per_b,
                  fixed((d, 2 * D_CONV)), fixed((1, 2 * D_CONV)), fixed((CONV_HALO, D_CONV)), fixed((1, D_CONV)),
                  fixed((1, D_CONV)), fixed((1, D_CONV)), fixed((D_CONV, d)), fixed((1, d))],
        out_specs=[pl.BlockSpec((1, tt, d), lambda b, i: (b, i, 0)), per_b],
        out_shape=[jax.ShapeDtypeStruct((bsz, tp, d), jnp.float32),
                   jax.ShapeDtypeStruct((bsz, CONV_HALO, D_CONV), jnp.float32)],
        scratch_shapes=[pltpu.VMEM((CONV_HALO + tt + SUBLANES, D_CONV), jnp.float32),
                        pltpu.VMEM((tt + SUBLANES, D_CONV), jnp.float32)],
        compiler_params=pltpu.CompilerParams(dimension_semantics=("arbitrary", "arbitrary"),
                                             vmem_limit_bytes=VMEM_LIMIT),
        name="conv_module",
    )(x, buf0, w_pw1.astype(bf16), b_pw1.reshape(1, -1), jnp.pad(w_dw, ((0, CONV_HALO - CONV_W), (0, 0))),
      b_dw.reshape(1, -1), ln_g.reshape(1, -1), ln_b.reshape(1, -1), w_pw2.astype(bf16), b_pw2.reshape(1, -1))
    return out[:, :t_], tail[:, CONV_LEAD:]


PACK_W = 256
SC_WINDOW = 128
SC_TILES = 32


def _pack_rows(y):
    out = []
    for h in range(2):
        lo = lax.bitcast_convert_type(y[:, 2 * h * PACK_W:(2 * h + 1) * PACK_W].astype(jnp.bfloat16)
                                      .astype(jnp.float32), jnp.uint32)
        hi = lax.bitcast_convert_type(y[:, (2 * h + 1) * PACK_W:(2 * h + 2) * PACK_W].astype(jnp.bfloat16)
                                      .astype(jnp.float32), jnp.uint32)
        out.append(lax.bitcast_convert_type((lo >> 16) | hi, jnp.int32))
    return out


def _unpack_words(w):
    u = lax.bitcast_convert_type(w, jnp.uint32)
    lo = lax.bitcast_convert_type(u << 16, jnp.float32)
    hi = lax.bitcast_convert_type(u & jnp.uint32(0xFFFF0000), jnp.float32)
    return lo, hi


def _gather_rows(src, idx):
    n = idx.shape[0]
    if n % (SC_WINDOW * SC_TILES) != 0:
        return jnp.take(src, idx, axis=0)
    mesh = plsc.VectorSubcoreMesh(core_axis_name="core", subcore_axis_name="subcore")

    @pl.kernel(out_type=jax.ShapeDtypeStruct((n, src.shape[1]), src.dtype), mesh=mesh)
    def gather_kernel(src_hbm, idx_hbm, out_hbm):
        def step(idx_vmem, out_vmem):
            pltpu.sync_copy(src_hbm.at[idx_vmem.at[0]], out_vmem)

        pltpu.emit_pipeline(
            step, grid=(n // SC_WINDOW,),
            in_specs=[pl.BlockSpec((1, SC_WINDOW), index_map=lambda i: (0, i))],
            out_specs=[pl.BlockSpec((SC_WINDOW, src.shape[1]), index_map=lambda i: (i, 0))],
            core_axis_name=("core", "subcore"),
            dimension_semantics=(pltpu.PARALLEL,),
        )(idx_hbm, out_hbm)

    return gather_kernel(src, idx.reshape(1, n))


def _scatter_rows(src, idx, n_out):
    n = idx.shape[0]
    m = src.shape[0] // 2
    reps = n // (2 * m)
    if n % (SC_WINDOW * SC_TILES) != 0 or m % SC_WINDOW != 0:
        rows = jnp.arange(n, dtype=jnp.int32)
        src_row = (rows // (reps * m)) * m + rows % m
        return jnp.zeros((n_out, src.shape[1]), src.dtype).at[idx].set(jnp.take(src, src_row, axis=0))
    tiles = m // SC_WINDOW
    mesh = plsc.VectorSubcoreMesh(core_axis_name="core", subcore_axis_name="subcore")

    @pl.kernel(out_type=jax.ShapeDtypeStruct((n_out, src.shape[1]), src.dtype), mesh=mesh, scratch_types=[])
    def scatter_kernel(src_hbm, idx_hbm, out_hbm):
        def step(src_vmem, idx_vmem):
            pltpu.sync_copy(src_vmem, out_hbm.at[idx_vmem.at[0]])

        pltpu.emit_pipeline(
            step, grid=(n // SC_WINDOW,),
            in_specs=[pl.BlockSpec((SC_WINDOW, src.shape[1]),
                                   index_map=lambda i: ((i // (reps * tiles)) * tiles + i % tiles, 0)),
                      pl.BlockSpec((1, SC_WINDOW), index_map=lambda i: (0, i))],
            out_specs=[],
            core_axis_name=("core", "subcore"),
            dimension_semantics=(pltpu.PARALLEL,),
        )(src_hbm, idx_hbm)

    return scatter_kernel(src, idx.reshape(1, n))


PER_GROUP = N_EXPERTS // N_GROUPS
PICKED = -3e38


def _ln_rows(v, g, b):
    mu = jnp.mean(v, axis=-1, keepdims=True)
    c = v - mu
    var = jnp.mean(c * c, axis=-1, keepdims=True)
    return c * lax.rsqrt(var + LN_EPS) * g + b


def _first_max(v, ids, axes, sentinel):
    best = v
    for a in axes:
        best = jnp.max(best, axis=a, keepdims=True)
    first = jnp.where(v == best, ids, sentinel)
    for a in axes:
        first = jnp.min(first, axis=a, keepdims=True)
    return best, first


def _sum_axes(v, axes):
    for a in axes:
        v = jnp.sum(v, axis=a, keepdims=True)
    return v


def _moe_pre_body(x_ref, mix_ref, g_ref, b_ref, wr_ref, br_ref, wgu_ref, wdn_ref,
                  x1_ref, xp_ref, sh_ref, eidx_ref, gate_ref, rank_ref, cnt_ref, run_ref):
    f32, bf16 = jnp.float32, jnp.bfloat16
    tm = x_ref.shape[0]

    @pl.when(pl.program_id(0) == 0)
    def _():
        run_ref[...] = jnp.zeros(run_ref.shape, f32)

    x1 = _ln_rows(ALPHA * x_ref[...] + mix_ref[...], g_ref[...], b_ref[...])
    x1_ref[...] = x1
    x1b = x1.astype(bf16)
    xp_ref[0], xp_ref[1] = _pack_rows(x1)

    h = _dot(x1b, wgu_ref[...])
    d_sh = h.shape[1] // 2
    act = (jax.nn.silu(h[:, :d_sh]) * h[:, d_sh:]).astype(bf16)
    sh_ref[...] = _dot(act, wdn_ref[...])

    s = jax.nn.sigmoid(_dot_nt(wr_ref[...], x1b)).reshape(N_GROUPS, PER_GROUP, tm)
    sb = s + br_ref[...].reshape(N_GROUPS, PER_GROUP, 1)
    shape3 = (N_GROUPS, PER_GROUP, tm)
    pid = lax.broadcasted_iota(jnp.int32, shape3, 1)
    gid = lax.broadcasted_iota(jnp.int32, (N_GROUPS, 1, tm), 0)
    eid = lax.broadcasted_iota(jnp.int32, shape3, 0) * PER_GROUP + pid
    top1, i1 = _first_max(sb, pid, (1,), PER_GROUP)
    top2 = jnp.max(jnp.where(pid == i1, PICKED, sb), axis=1, keepdims=True)
    gscore = top1 + top2
    gsel = jnp.zeros((N_GROUPS, 1, tm), f32)
    for _ in range(TOPK_GROUPS):
        _, first = _first_max(gscore, gid, (0,), N_GROUPS)
        hit = gid == first
        gsel = jnp.where(hit, 1.0, gsel)
        gscore = jnp.where(hit, PICKED, gscore)
    cand = jnp.where(gsel > 0.0, sb, -1e30)
    firsts, gates = [], []
    picked = jnp.zeros(shape3, f32)
    for _ in range(TOP_K):
        _, first = _first_max(cand, eid, (0, 1), N_EXPERTS)
        hit = eid == first
        firsts.append(first)
        gates.append(_sum_axes(jnp.where(hit, s, 0.0), (0, 1)))
        picked = jnp.where(hit, 1.0, picked)
        cand = jnp.where(hit, PICKED, cand)
    gsum = gates[0]
    for gk in gates[1:]:
        gsum = gsum + gk
    earlier = (lax.broadcasted_iota(jnp.int32, (tm, tm), 0) < lax.broadcasted_iota(jnp.int32, (tm, tm), 1))
    picked2 = picked.reshape(N_EXPERTS, tm)
    rank = run_ref[...] + _dot(picked2.astype(bf16), jnp.where(earlier, 1.0, 0.0).astype(bf16))
    run_new = run_ref[...] + jnp.sum(picked2, axis=1, keepdims=True)
    run_ref[...] = run_new
    cnt_ref[...] = jnp.broadcast_to(run_new, cnt_ref.shape)
    rank3 = rank.reshape(shape3)
    for k in range(TOP_K):
        hit = eid == firsts[k]
        eidx_ref[k:k + 1, :] = firsts[k].reshape(1, tm)
        gate_ref[k:k + 1, :] = (gates[k] / gsum * ROUTE_SCALE).reshape(1, tm)
        rank_ref[k:k + 1, :] = _sum_axes(jnp.where(hit, rank3, 0.0), (0, 1)).reshape(1, tm).astype(jnp.int32)


def _moe_pre(x, mix, g, b, w_router, b_router, w_sh_gu, w_sh_down):
    m, d = x.shape
    bf16 = jnp.bfloat16
    tm = min(m, 512)
    row = lambda i: (i, 0)
    col = lambda i: (0, i)
    fixed = lambda i: (0, 0)
    d_sh2 = w_sh_gu.shape[1]
    return pl.pallas_call(
        _moe_pre_body,
        grid=(m // tm,),
        in_specs=[pl.BlockSpec((tm, d), row), pl.BlockSpec((tm, d), row),
                  pl.BlockSpec((1, d), fixed), pl.BlockSpec((1, d), fixed),
                  pl.BlockSpec((N_EXPERTS, d), fixed), pl.BlockSpec((N_EXPERTS, 1), fixed),
                  pl.BlockSpec((d, d_sh2), fixed), pl.BlockSpec((d_sh2 // 2, d), fixed)],
        out_specs=[pl.BlockSpec((tm, d), row), pl.BlockSpec((2, tm, PACK_W), lambda i: (0, i, 0)),
                   pl.BlockSpec((tm, d), row),
                   pl.BlockSpec((TOP_K, tm), col), pl.BlockSpec((TOP_K, tm), col), pl.BlockSpec((TOP_K, tm), col),
                   pl.BlockSpec((N_EXPERTS, LANE), fixed)],
        out_shape=[jax.ShapeDtypeStruct((m, d), jnp.float32), jax.ShapeDtypeStruct((2, m, PACK_W), jnp.int32),
                   jax.ShapeDtypeStruct((m, d), jnp.float32),
                   jax.ShapeDtypeStruct((TOP_K, m), jnp.int32), jax.ShapeDtypeStruct((TOP_K, m), jnp.float32),
                   jax.ShapeDtypeStruct((TOP_K, m), jnp.int32),
                   jax.ShapeDtypeStruct((N_EXPERTS, LANE), jnp.float32)],
        scratch_shapes=[pltpu.VMEM((N_EXPERTS, 1), jnp.float32)],
        compiler_params=pltpu.CompilerParams(dimension_semantics=("arbitrary",),
                                             vmem_limit_bytes=VMEM_LIMIT),
        name="moe_pre",
    )(x, mix, g.reshape(1, d), b.reshape(1, d), w_router.T.astype(bf16), b_router.reshape(N_EXPERTS, 1),
      w_sh_gu.astype(bf16), w_sh_down.astype(bf16))


def _moe_expert_body(exp_ref, first_ref, rows_ref, xs_ref, wgu_ref, wdn_ref, y_ref, wgu_bf, wdn_bf):
    i = pl.program_id(0)
    bf16 = jnp.bfloat16

    @pl.when(first_ref[i] == 1)
    def _():
        wgu_bf[...] = wgu_ref[0, 0].astype(bf16)
        wdn_bf[...] = wdn_ref[0, 0].astype(bf16)

    @pl.when(rows_ref[i] > 0)
    def _():
        live = lax.broadcasted_iota(jnp.int32, (xs_ref.shape[1], 1), 0) < rows_ref[i]
        h = None
        for hw in range(2):
            for q, xq in enumerate(_unpack_words(xs_ref[hw])):
                r0 = (2 * hw + q) * PACK_W
                part = _dot(jnp.where(live, xq, 0.0).astype(bf16), wgu_bf[r0:r0 + PACK_W, :])
                h = part if h is None else h + part
        d_e = h.shape[1] // 2
        act = (jax.nn.silu(h[:, :d_e]) * h[:, d_e:]).astype(bf16)
        y_ref[0], y_ref[1] = _pack_rows(_dot(act, wdn_bf[...]))

    @pl.when(rows_ref[i] == 0)
    def _():
        y_ref[...] = jnp.zeros(y_ref.shape, y_ref.dtype)


def _moe_experts(xs, blk_exp, blk_first, blk_rows, w_exp_gu, w_exp_down, layer, bm):
    n_slots = xs.shape[1]
    d = w_exp_gu.shape[2]
    n_blk = n_slots // bm
    d_e2 = w_exp_gu.shape[3]
    words = lambda i, e, f, a: (0, i, 0)
    grid_spec = pltpu.PrefetchScalarGridSpec(
        num_scalar_prefetch=3,
        grid=(n_blk,),
        in_specs=[pl.BlockSpec((2, bm, PACK_W), words),
                  pl.BlockSpec((1, 1, d, d_e2), lambda i, e, f, a: (layer, e[i], 0, 0)),
                  pl.BlockSpec((1, 1, d_e2 // 2, d), lambda i, e, f, a: (layer, e[i], 0, 0))],
        out_specs=pl.BlockSpec((2, bm, PACK_W), words),
        scratch_shapes=[pltpu.VMEM((d, d_e2), jnp.bfloat16), pltpu.VMEM((d_e2 // 2, d), jnp.bfloat16)])
    return pl.pallas_call(
        _moe_expert_body,
        grid_spec=grid_spec,
        out_shape=jax.ShapeDtypeStruct((2, n_slots, PACK_W), jnp.int32),
        compiler_params=pltpu.CompilerParams(dimension_semantics=("arbitrary",),
                                             vmem_limit_bytes=VMEM_LIMIT),
        name="moe_experts",
    )(blk_exp, blk_first, blk_rows, xs, w_exp_gu, w_exp_down)


def _combine_ln_body(x_ref, yg_ref, gt_ref, sh_ref, g_ref, b_ref, o_ref):
    gt = gt_ref[...]
    parts = []
    for hw in range(2):
        lo_acc = hi_acc = None
        for k in range(TOP_K):
            lo, hi = _unpack_words(yg_ref[hw, k])
            gk = gt[:, k:k + 1]
            lo_acc = lo * gk if lo_acc is None else lo_acc + lo * gk
            hi_acc = hi * gk if hi_acc is None else hi_acc + hi * gk
        parts += [lo_acc, hi_acc]
    routed = jnp.concatenate(parts, axis=1)
    o_ref[...] = _ln_rows(ALPHA * x_ref[...] + (routed + sh_ref[...]), g_ref[...], b_ref[...])


def _combine_ln(x, yg, gate_t, shared, g, b):
    m, d = x.shape
    tm = min(m, 256)
    row = lambda i: (i, 0)
    fixed = lambda i: (0, 0)
    return pl.pallas_call(
        _combine_ln_body,
        grid=(m // tm,),
        in_specs=[pl.BlockSpec((tm, d), row), pl.BlockSpec((2, TOP_K, tm, PACK_W), lambda i: (0, 0, i, 0)),
                  pl.BlockSpec((tm, TOP_K), row), pl.BlockSpec((tm, d), row),
                  pl.BlockSpec((1, d), fixed), pl.BlockSpec((1, d), fixed)],
        out_specs=pl.BlockSpec((tm, d), row),
        out_shape=jax.ShapeDtypeStruct((m, d), jnp.float32),
        compiler_params=pltpu.CompilerParams(dimension_semantics=("arbitrary",)),
        name="combine_ln",
    )(x, yg, gate_t, shared, g.reshape(1, d), b.reshape(1, d))


def _moe_layer(x, mix, ln1_g, ln1_b, ln2_g, ln2_b, w_router, b_router, w_exp_gu, w_exp_down, layer,
               w_sh_gu, w_sh_down):
    m, d = x.shape
    x1, xp, shared, eidx, gate8, rank8, counts = _moe_pre(x, mix, ln1_g, ln1_b, w_router, b_router,
                                                           w_sh_gu, w_sh_down)
    bm = 512 if m * TOP_K >= 512 * N_EXPERTS else MOE_BLK
    n_blk = (m * TOP_K) // bm + N_EXPERTS
    counts = counts[:, 0].astype(jnp.int32)
    padded = (counts + bm - 1) // bm * bm
    pad_end = jnp.cumsum(padded)
    pad_start = pad_end - padded
    start_of = jnp.sum(jnp.where(eidx[:, :, None] == jnp.arange(N_EXPERTS), pad_start, 0), axis=-1)
    dest = (start_of + rank8).reshape(-1)
    blk_start = jnp.arange(n_blk, dtype=jnp.int32) * bm
    blk_exp = jnp.minimum(jnp.sum(pad_end[None, :] <= blk_start[:, None], axis=1), N_EXPERTS - 1).astype(jnp.int32)
    blk_rows = jnp.clip(counts[blk_exp] - (blk_start - pad_start[blk_exp]), 0, bm).astype(jnp.int32)
    blk_first = jnp.concatenate([jnp.ones((1,), jnp.int32), (blk_exp[1:] != blk_exp[:-1]).astype(jnp.int32)])
    n_slots = n_blk * bm
    xs = _scatter_rows(xp.reshape(2 * m, PACK_W), jnp.concatenate([dest, dest + n_slots]), 2 * n_slots)
    y = _moe_experts(xs.reshape(2, n_slots, PACK_W), blk_exp, blk_first, blk_rows, w_exp_gu, w_exp_down, layer, bm)
    yg = _gather_rows(y.reshape(2 * n_slots, PACK_W), jnp.concatenate([dest, dest + n_slots]))
    return _combine_ln(x1, yg.reshape(2, TOP_K, m, PACK_W), gate8.T, shared, ln2_g, ln2_b)


def _trunk(x, pos, gla_state, nsa_cache, page_table, win_buf, conv_buf,
           w_in_ab, w_gla_gate, b_gla_gate, gla_norm_g, w_cmp_pool, w_out_ab,
           w_pw1, b_pw1, w_dw, b_dw, conv_ln_g, conv_ln_b, w_pw2, b_pw2,
           ln_g, ln_b, w_router, b_router, w_exp_gu, w_exp_down, w_sh_gu, w_sh_down):
    new_gla, new_rows, new_win, new_conv = [], [], [], []
    for layer in range(DEPTH):
        i = layer // 2
        if layer % 2 == 0:
            mix, s_a, rows, win = _ab_mixer(
                x, pos, w_in_ab[i], w_gla_gate[i], b_gla_gate[i], gla_norm_g[i], w_cmp_pool[i], w_out_ab[i],
                None if gla_state is None else gla_state[i],
                None if nsa_cache is None else nsa_cache[i], page_table,
                None if win_buf is None else win_buf[i])
            new_gla.append(s_a)
            new_rows.append(rows)
            new_win.append(win)
        else:
            mix, cb = _conv_module(x, None if conv_buf is None else conv_buf[i], w_pw1[i], b_pw1[i],
                                   w_dw[i], b_dw[i], conv_ln_g[i], conv_ln_b[i], w_pw2[i], b_pw2[i])
            new_conv.append(cb)
        bsz, t_, d = x.shape
        x = _moe_layer(x.reshape(-1, d), mix.reshape(-1, d), ln_g[layer, 0], ln_b[layer, 0],
                       ln_g[layer, 1], ln_b[layer, 1], w_router[layer], b_router[layer],
                       w_exp_gu, w_exp_down, layer, w_sh_gu[layer], w_sh_down[layer]).reshape(bsz, t_, d)
    return x, jnp.stack(new_gla), jnp.stack(new_rows), jnp.stack(new_win), jnp.stack(new_conv)


def kernel(x_prompt, x_sample, state_gla, cache_nsa_kv, state_nsa_win, state_conv, page_table,
           w_in_ab, w_gla_gate, b_gla_gate, gla_norm_g, w_cmp_pool, w_out_ab,
           w_pw1, b_pw1, w_dw, b_dw, conv_ln_g, conv_ln_b, w_pw2, b_pw2,
           ln_g, ln_b, w_router, b_router, w_exp_gu, w_exp_down, w_sh_gu, w_sh_down):
    weights = (w_in_ab, w_gla_gate, b_gla_gate, gla_norm_g, w_cmp_pool, w_out_ab,
               w_pw1, b_pw1, w_dw, b_dw, conv_ln_g, conv_ln_b, w_pw2, b_pw2,
               ln_g, ln_b, w_router, b_router, w_exp_gu, w_exp_down, w_sh_gu, w_sh_down)
    past_len = page_table.shape[1] * PAGE_SIZE
    pos_p = jnp.arange(x_prompt.shape[1])
    pos_s = past_len + jnp.arange(x_sample.shape[1])
    y_prompt, gla_p, rows_p, win_p, conv_p = _trunk(x_prompt, pos_p, None, None, None, None, None, *weights)
    y_sample, gla_s, rows_s, win_s, conv_s = _trunk(x_sample, pos_s, state_gla, cache_nsa_kv, page_table,
                                                    state_nsa_win, state_conv, *weights)
    return (y_prompt, y_sample, gla_p, gla_s, rows_p, rows_s, win_p, win_s, conv_p, conv_s)
```

```python
import functools
import math

import jax
import jax.numpy as jnp
import numpy as np
from jax import lax
from jax.experimental import pallas as pl
from jax.experimental.pallas import tpu as pltpu
from jax.experimental.pallas import tpu_sc as plsc

D_MODEL = 1024
DEPTH = 2
PAGE_SIZE = 128

GLA_HEADS = 4
GLA_DV = D_MODEL // 2 // GLA_HEADS
GLA_DK = GLA_DV // 2
GLA_RANK = 16
GLA_TAU = 16.0

NSA_HEADS = 8
NSA_KV_HEADS = 2
NSA_GROUP = NSA_HEADS // NSA_KV_HEADS
HEAD_DIM = D_MODEL // 2 // NSA_HEADS
CMP_BLK = 32
CMP_STRIDE = 16
SEL_BLK = 64
SEL_TOPN = 16
WINDOW = 512
Q_BLK = 128
FORCE_BONUS = 100.0
ROPE_DIM = HEAD_DIM // 4
ROPE_THETA = 500000.0

GLA_SIZES = (GLA_HEADS * GLA_DK, GLA_HEADS * GLA_DK, GLA_HEADS * GLA_DV, GLA_HEADS * GLA_DV, GLA_RANK)
NSA_SIZES = (NSA_HEADS * HEAD_DIM, 6 * NSA_KV_HEADS * HEAD_DIM, 3 * NSA_HEADS)

CONV_W = 31
D_CONV = D_MODEL

N_EXPERTS = 64
N_GROUPS = 8
TOPK_GROUPS = 4
TOP_K = 8
D_EXPERT = 256
ROUTE_SCALE = 2.5
MOE_BLK = 128

ALPHA = (2 * DEPTH) ** 0.25
LN_EPS = 1e-5

LANE = 128
SUBLANES = 8
V7X_VMEM_BYTES = 64 * 1024 * 1024
VMEM_LIMIT = V7X_VMEM_BYTES * 3 // 4


def _dot(a, b):
    return jnp.dot(a, b, preferred_element_type=jnp.float32)


def _dot_nt(a, b):
    return lax.dot_general(a, b, (((1,), (1,)), ((), ())), preferred_element_type=jnp.float32)


def _mm_body(x_ref, w_ref, o_ref):
    o_ref[...] = _dot(x_ref[...].astype(jnp.bfloat16), w_ref[...].astype(jnp.bfloat16))


def _mm(x, w, keep_pad=False):
    m, k = x.shape
    n = w.shape[1]
    n_pad = -(-n // LANE) * LANE
    w = w.astype(jnp.bfloat16)
    if n_pad != n:
        w = jnp.pad(w, ((0, 0), (0, n_pad - n)))
    tm = min(m, 512)
    out = pl.pallas_call(
        _mm_body,
        grid=(m // tm,),
        in_specs=[pl.BlockSpec((tm, k), lambda i: (i, 0)),
                  pl.BlockSpec((k, n_pad), lambda i: (0, 0))],
        out_specs=pl.BlockSpec((tm, n_pad), lambda i: (i, 0)),
        out_shape=jax.ShapeDtypeStruct((m, n_pad), jnp.float32),
        compiler_params=pltpu.CompilerParams(dimension_semantics=("arbitrary",),
                                             vmem_limit_bytes=VMEM_LIMIT),
        name="mm",
    )(x, w)
    return out if keep_pad or n_pad == n else out[:, :n]


def _mm_pair_body(a_ref, b_ref, w_ref, o_ref):
    ka = a_ref.shape[1]
    o_ref[...] = (_dot(a_ref[...].astype(jnp.bfloat16), w_ref[0:ka, :])
                  + _dot(b_ref[...].astype(jnp.bfloat16), w_ref[ka:, :]))


def _mm_pair(a, b, w):
    m, ka = a.shape
    kb = b.shape[1]
    n = w.shape[1]
    tm = min(m, 512)
    return pl.pallas_call(
        _mm_pair_body,
        grid=(m // tm,),
        in_specs=[pl.BlockSpec((tm, ka), lambda i: (i, 0)), pl.BlockSpec((tm, kb), lambda i: (i, 0)),
                  pl.BlockSpec((ka + kb, n), lambda i: (0, 0))],
        out_specs=pl.BlockSpec((tm, n), lambda i: (i, 0)),
        out_shape=jax.ShapeDtypeStruct((m, n), jnp.float32),
        compiler_params=pltpu.CompilerParams(dimension_semantics=("arbitrary",)),
        name="mm_pair",
    )(a, b, w.astype(jnp.bfloat16))


def _partial_rope(x, pos):
    half = ROPE_DIM // 2
    inv_freq = jnp.power(ROPE_THETA, -jnp.arange(half, dtype=jnp.float32) / half)
    ang = pos.astype(jnp.float32)[:, None] * inv_freq
    ang = ang.reshape(ang.shape[0], *([1] * (x.ndim - 3)), half)
    cos, sin = jnp.cos(ang), jnp.sin(ang)
    x1 = x[..., :half]
    x2 = x[..., half:ROPE_DIM]
    rot = jnp.concatenate([x1 * cos - x2 * sin, x2 * cos + x1 * sin], -1)
    return jnp.concatenate([rot, x[..., ROPE_DIM:]], -1)


NSA_ROWS = NSA_GROUP * Q_BLK
SEL_KT = 1024
N_SELB = 128
MASKED = -1e9
WIN_KEYS = WINDOW + Q_BLK
KK_W = 2 * HEAD_DIM + N_SELB


def _nsa_prompt_body(qr_ref, qo_ref, kc_ref, vct_ref, kk_ref, vvt_ref, g_ref, o_ref,
                     imp_ref, m_ref, l_ref, acc_ref):
    f32, bf16 = jnp.float32, jnp.bfloat16
    qb = pl.program_id(2)
    q0 = qb * Q_BLK
    qr_t = qr_ref[0, 0, 0]
    qo_t = qo_ref[0, 0, 0]
    n_cmp = kc_ref.shape[2]

    s_c = _dot(kc_ref[0, 0], qr_t)
    n_idx = lax.broadcasted_iota(jnp.int32, (n_cmp, NSA_ROWS), 0)
    qpos_c = q0 + (lax.broadcasted_iota(jnp.int32, (n_cmp, NSA_ROWS), 1) & (Q_BLK - 1))
    cmask = (n_idx * CMP_STRIDE + (CMP_BLK - 1)) <= qpos_c
    s_c = jnp.where(cmask, s_c, MASKED)
    m_c = jnp.max(s_c, axis=0, keepdims=True)
    p_c = jnp.where(cmask, jnp.exp(s_c - m_c), 0.0)
    p_c = p_c / jnp.maximum(jnp.sum(p_c, axis=0, keepdims=True), 1e-30)
    o_ct = _dot(vct_ref[0, 0], p_c.astype(bf16))

    imp = (p_c[:, 0:Q_BLK] + p_c[:, Q_BLK:2 * Q_BLK]) + p_c[:, 2 * Q_BLK:3 * Q_BLK] + p_c[:, 3 * Q_BLK:]
    imp_ref[0:8, :] = jnp.zeros((8, Q_BLK), f32)
    imp_ref[8:8 + n_cmp, :] = imp
    ratio = SEL_BLK // CMP_STRIDE
    n_selb = n_cmp // ratio
    imp_s = imp_ref[pl.ds(7, n_selb, stride=ratio), :]
    for r in range(ratio):
        imp_s = imp_s + imp_ref[pl.ds(8 + r, n_selb, stride=ratio), :]
    blk = lax.broadcasted_iota(jnp.int32, (n_selb, Q_BLK), 0)
    qpos_s = q0 + lax.broadcasted_iota(jnp.int32, (n_selb, Q_BLK), 1)
    cur = lax.shift_right_logical(qpos_s, int(math.log2(SEL_BLK)))
    valid = blk * SEL_BLK <= qpos_s
    forced = (blk == 0) | (blk == cur) | (blk == cur - 1)
    score = jnp.where(valid, imp_s + jnp.where(forced, FORCE_BONUS, 0.0), -1e30)
    picked = jnp.zeros((n_selb, Q_BLK), f32)
    for _ in range(SEL_TOPN):
        best = jnp.max(score, axis=0, keepdims=True)
        first = jnp.min(jnp.where(score == best, blk, n_selb), axis=0, keepdims=True)
        hit = blk == first
        picked = jnp.where(hit, 1.0, picked)
        score = jnp.where(hit, -3e38, score)
    selb_t = jnp.where(valid, picked, 0.0)
    if n_selb < N_SELB:
        selb_t = jnp.concatenate([selb_t, jnp.zeros((N_SELB - n_selb, Q_BLK), f32)], axis=0)
    selb_t = ((selb_t - 1.0) * (-MASKED)).astype(bf16)
    selb_t = jnp.concatenate([selb_t] * NSA_GROUP, axis=1)

    zeros_q = jnp.zeros((HEAD_DIM, NSA_ROWS), bf16)
    q_sel = jnp.concatenate([qo_t, zeros_q, selb_t], axis=0)
    q_win = jnp.concatenate([zeros_q, qo_t, jnp.zeros((N_SELB, NSA_ROWS), bf16)], axis=0)
    qpos_r = q0 + (lax.broadcasted_iota(jnp.int32, (1, NSA_ROWS), 1) & (Q_BLK - 1))

    def v_tiles(first, count):
        return jnp.concatenate([vvt_ref[0, 0, first + j] for j in range(count)], axis=1)

    m_ref[...] = jnp.full(m_ref.shape, MASKED, f32)
    l_ref[...] = jnp.zeros(l_ref.shape, f32)
    acc_ref[...] = jnp.zeros(acc_ref.shape, f32)

    def sel_tile(k0, kt, causal):
        s = _dot(kk_ref[0, 0, pl.ds(k0, kt), :], q_sel)
        if causal:
            kpos = k0 + lax.broadcasted_iota(jnp.int32, (kt, NSA_ROWS), 0)
            s = jnp.where(kpos <= qpos_r, s, MASKED)
        m_old = m_ref[...]
        m_new = jnp.maximum(m_old, jnp.max(s, axis=0, keepdims=True))
        alpha = jnp.exp(m_old - m_new)
        p = jnp.exp(s - m_new)
        l_ref[...] = alpha * l_ref[...] + jnp.sum(p, axis=0, keepdims=True)
        vt = v_tiles(k0 // Q_BLK, kt // Q_BLK)
        acc_ref[...] = alpha * acc_ref[...] + _dot(vt, p.astype(bf16))
        m_ref[...] = m_new

    n_full = q0 // SEL_KT

    def full_step(t, c):
        sel_tile(pl.multiple_of(t * SEL_KT, SEL_KT), SEL_KT, False)
        return c

    lax.fori_loop(0, n_full, full_step, 0)
    d0 = pl.multiple_of(n_full * SEL_KT, SEL_KT)
    short = q0 + Q_BLK - n_full * SEL_KT <= SEL_KT // 2

    @pl.when(short)
    def _():
        sel_tile(d0, SEL_KT // 2, True)

    @pl.when(jnp.logical_not(short))
    def _():
        sel_tile(d0, SEL_KT, True)
    o_st = acc_ref[0:HEAD_DIM, :] / l_ref[...]

    w0 = pl.multiple_of(jnp.maximum(q0 - WINDOW, 0), Q_BLK)
    s_w = _dot(kk_ref[0, 0, pl.ds(w0, WIN_KEYS), :], q_win)
    kpos_w = w0 + lax.broadcasted_iota(jnp.int32, (WIN_KEYS, NSA_ROWS), 0)
    s_w = jnp.where((kpos_w <= qpos_r) & (kpos_w > qpos_r - WINDOW), s_w, MASKED)
    p_w = jnp.exp(s_w - jnp.max(s_w, axis=0, keepdims=True))
    l_w = jnp.sum(p_w, axis=0, keepdims=True)
    acc_w = _dot(v_tiles(w0 // Q_BLK, WIN_KEYS // Q_BLK), p_w.astype(bf16))
    o_wt = acc_w[HEAD_DIM:2 * HEAD_DIM, :] / l_w

    g = g_ref[0, 0, 0]
    out_t = g[0:1, :] * o_ct + g[1:2, :] * o_st + g[2:3, :] * o_wt
    o_ref[0] = jnp.concatenate([out_t[:, g_ * Q_BLK:(g_ + 1) * Q_BLK] for g_ in range(NSA_GROUP)], axis=0).T


def _nsa_prompt(qr, qo, gt, kc_p, vct, kk, vvt):
    bsz, _, nqb = qr.shape[:3]
    t_ = nqb * Q_BLK
    n_cmp = kc_p.shape[2]
    per_blk = lambda b, h, i: (b, h, i, 0, 0)
    per_head = lambda b, h, i: (b, h, 0, 0)
    return pl.pallas_call(
        _nsa_prompt_body,
        grid=(bsz, NSA_KV_HEADS, nqb),
        in_specs=[pl.BlockSpec((1, 1, 1, HEAD_DIM, NSA_ROWS), per_blk),
                  pl.BlockSpec((1, 1, 1, HEAD_DIM, NSA_ROWS), per_blk),
                  pl.BlockSpec((1, 1, n_cmp, HEAD_DIM), per_head),
                  pl.BlockSpec((1, 1, HEAD_DIM, n_cmp), per_head),
                  pl.BlockSpec((1, 1, t_, KK_W), per_head),
                  pl.BlockSpec((1, 1, nqb, 2 * HEAD_DIM, Q_BLK), lambda b, h, i: (b, h, 0, 0, 0)),
                  pl.BlockSpec((1, 1, 1, 3, NSA_ROWS), per_blk)],
        out_specs=pl.BlockSpec((1, Q_BLK, NSA_GROUP * HEAD_DIM), lambda b, h, i: (b, i, h)),
        out_shape=jax.ShapeDtypeStruct((bsz, t_, NSA_HEADS * HEAD_DIM), jnp.float32),
        scratch_shapes=[pltpu.VMEM((8 + n_cmp, Q_BLK), jnp.float32),
                        pltpu.VMEM((1, NSA_ROWS), jnp.float32),
                        pltpu.VMEM((1, NSA_ROWS), jnp.float32),
                        pltpu.VMEM((2 * HEAD_DIM, NSA_ROWS), jnp.float32)],
        compiler_params=pltpu.CompilerParams(
            dimension_semantics=("arbitrary", "arbitrary", "arbitrary"),
            vmem_limit_bytes=VMEM_LIMIT),
        name="nsa_prompt",
    )(qr, qo, kc_p, vct, kk, vvt, gt)


GLA_SUB = 16
GLA_UNROLL = 8
GLA_TILE = 256
GLA_QK = GLA_HEADS * GLA_DK
GLA_V = GLA_HEADS * GLA_DV


def _dot_tn(a, b):
    return lax.dot_general(a, b, (((0,), (0,)), ((), ())), preferred_element_type=jnp.float32)


def _gla_body(q_ref, k_ref, v_ref, gr_ref, glr_ref, wg_ref, bg_ref, ng_ref, s0_ref, exp_ref,
              o_ref, sfin_ref, st_ref, b_ref, qd_ref, *, t_valid):
    f32, bf16 = jnp.float32, jnp.bfloat16
    tt = q_ref.shape[1]
    ti = pl.program_id(1)

    @pl.when(ti == 0)
    def _():
        st_ref[...] = s0_ref[0]

    row = lax.broadcasted_iota(jnp.int32, (tt, 1), 0)
    z = _dot(glr_ref[0][:, :GLA_RANK].astype(bf16), wg_ref[...]) + bg_ref[...]
    la = (jnp.minimum(z, 0.0) - jnp.log1p(jnp.exp(-jnp.abs(z)))) * (1.0 / GLA_TAU)
    la = jnp.where(ti * tt + row < t_valid, la, 0.0)
    seg = row & (GLA_SUB - 1)
    b = la
    for s in (1, 2, 4, 8):
        b = b + jnp.where(seg >= s, pltpu.roll(b, s, axis=0), 0.0)
    q = q_ref[0] * (GLA_DK ** -0.5)
    k = k_ref[0]
    v = v_ref[0]
    o = _dot((q * k).astype(bf16), exp_ref[...]) * v
    for d in range(1, GLA_SUB):
        decay = jnp.exp(jnp.minimum(b - pltpu.roll(b, d, axis=0), 0.0))
        w = jnp.where(seg >= d, q * pltpu.roll(k, d, axis=0) * decay, 0.0)
        o = o + _dot(w.astype(bf16), exp_ref[...]) * pltpu.roll(v, d, axis=0)
    o_ref[0] = o
    b_ref[...] = b
    qd_ref[...] = (q * jnp.exp(b)).astype(bf16)

    def block_step(c, carry):
        rows = pl.ds(pl.multiple_of(c * GLA_SUB, GLA_SUB), GLA_SUB)
        qd = qd_ref[rows, :]
        bc = b_ref[rows, :]
        bl = bc[GLA_SUB - 1:GLA_SUB, :]
        kc = (k_ref[0, rows, :] * jnp.exp(bl - bc)).astype(bf16)
        keep = jnp.exp(bl)
        vb = v_ref[0, rows, :].astype(bf16)
        outs = []
        for h in range(GLA_HEADS):
            dk = slice(h * GLA_DK, (h + 1) * GLA_DK)
            dv = slice(h * GLA_DV, (h + 1) * GLA_DV)
            st = st_ref[dv, :]
            outs.append(_dot_nt(qd[:, dk], st.astype(bf16)))
            st_ref[dv, :] = st * keep[:, dk] + _dot_tn(vb[:, dv], kc[:, dk])
        o_ref[0, rows, :] += jnp.concatenate(outs, axis=1)
        return carry

    lax.fori_loop(0, tt // GLA_SUB, block_step, 0, unroll=GLA_UNROLL)
    sfin_ref[0] = st_ref[...]
    gr = gr_ref[0]
    gate = gr * jax.nn.sigmoid(gr)
    for h in range(GLA_HEADS):
        cols = slice(h * GLA_DV, (h + 1) * GLA_DV)
        oh = o_ref[0, :, cols]
        ms = jnp.mean(oh * oh, axis=-1, keepdims=True)
        o_ref[0, :, cols] = oh * lax.rsqrt(ms + LN_EPS) * ng_ref[...] * gate[:, cols]


def _gla(h, w_gla_gate, b_gla_gate, gla_norm_g, gla_state):
    bsz, t_, n_in = h.shape
    tp = -(-t_ // GLA_SUB) * GLA_SUB
    if tp != t_:
        h = jnp.pad(h, ((0, 0), (0, tp - t_), (0, 0)))
    tt = min(tp, GLA_TILE)
    expand =np.repeat(np.repeat(np.eye(GLA_HEADS, dtype=np.float32), GLA_DK, 0), GLA_DV, 1)
    if gla_state is None:
        s0 = jnp.zeros((bsz, GLA_V, GLA_DK), jnp.float32)
    else:
        s0 = gla_state.transpose(0, 1, 3, 2).reshape(bsz, GLA_V, GLA_DK)
    tile = lambda width, blk: pl.BlockSpec((1, tt, width), lambda b, i: (b, i, blk))
    fixed2 = lambda shape: pl.BlockSpec(shape, lambda b, i: (0, 0))
    per_b = pl.BlockSpec((1, GLA_V, GLA_DK), lambda b, i: (b, 0, 0))
    o, s_t = pl.pallas_call(
        functools.partial(_gla_body, t_valid=t_),
        grid=(bsz, tp // tt),
        in_specs=[tile(GLA_QK, 0), tile(GLA_QK, 1), tile(GLA_V, 1), tile(GLA_V, 2),
                  tile(LANE, (2 * GLA_QK + 2 * GLA_V + NSA_SIZES[0] + NSA_SIZES[1]) // LANE),
                  fixed2((GLA_RANK, GLA_QK)), fixed2((1, GLA_QK)), fixed2((1, GLA_DV)), per_b,
                  fixed2((GLA_QK, GLA_V))],
        out_specs=[pl.BlockSpec((1, tt, GLA_V), lambda b, i: (b, i, 0)), per_b],
        out_shape=[jax.ShapeDtypeStruct((bsz, tp, GLA_V), jnp.float32),
                   jax.ShapeDtypeStruct((bsz, GLA_V, GLA_DK), jnp.float32)],
        scratch_shapes=[pltpu.VMEM((GLA_V, GLA_DK), jnp.float32), pltpu.VMEM((tt, GLA_QK), jnp.float32),
                        pltpu.VMEM((tt, GLA_QK), jnp.bfloat16)],
        compiler_params=pltpu.CompilerParams(dimension_semantics=("arbitrary", "arbitrary"),
                                             vmem_limit_bytes=VMEM_LIMIT),
        name="gla",
    )(h, h, h, h, h, w_gla_gate.astype(jnp.bfloat16), b_gla_gate.reshape(1, GLA_QK),
      gla_norm_g.reshape(1, GLA_DV), s0, jnp.asarray(expand, jnp.bfloat16))
    return o[:, :t_], s_t.reshape(bsz, GLA_HEADS, GLA_DV, GLA_DK).transpose(0, 1, 3, 2)


COL_NQ = 2 * GLA_QK + 2 * GLA_V
COL_NKV = COL_NQ + NSA_SIZES[0]
COL_TAIL = COL_NKV + NSA_SIZES[1]
TAIL_GATE = GLA_RANK
_ORIG = np.cumsum((0,) + GLA_SIZES + NSA_SIZES)
IN_AB_PERM = np.concatenate([np.arange(_ORIG[0], _ORIG[4]), np.arange(_ORIG[5], _ORIG[7]),
                             np.arange(_ORIG[4], _ORIG[5]), np.arange(_ORIG[7], _ORIG[8])])
SUBS = Q_BLK // CMP_STRIDE


def _nsa_prep_body(nq_ref, kv0_ref, kv1_ref, kv2_ref, tail_ref, rc_ref, ru_ref, rd_ref, pool_ref,
                   rows_ref, win_ref, kk_ref, vvt_ref, qr_ref, qo_ref, g_ref, pooled_ref):
    bf16 = jnp.bfloat16
    q0 = pl.program_id(1) * Q_BLK
    kv_w = NSA_KV_HEADS * HEAD_DIM

    def rope(x):
        reps = x.shape[1] // LANE
        wide = lambda r: jnp.concatenate([r[...]] * reps, axis=1) if reps > 1 else r[...]
        half = ROPE_DIM // 2
        return (x * wide(rc_ref) + pltpu.roll(x, half, axis=1) * wide(ru_ref)
                + pltpu.roll(x, x.shape[1] - half, axis=1) * wide(rd_ref))

    kv0, kv1, kv2 = kv0_ref[0], kv1_ref[0], kv2_ref[0]
    k_sel, v_sel = rope(kv1[:, :kv_w]), kv1[:, kv_w:]
    k_win, v_win = rope(kv2[:, :kv_w]), kv2[:, kv_w:]
    rows_ref[0] = jnp.concatenate([kv0, k_sel, v_sel], axis=1)
    win_ref[0] = jnp.concatenate([k_win, v_win], axis=1)
    blk_id = lax.shift_right_logical(q0 + lax.broadcasted_iota(jnp.int32, (Q_BLK, N_SELB), 0),
                                     int(math.log2(SEL_BLK)))
    onehot = jnp.where(lax.broadcasted_iota(jnp.int32, (Q_BLK, N_SELB), 1) == blk_id, 1.0, 0.0).astype(bf16)
    q = nq_ref[0] * (HEAD_DIM ** -0.5)
    q_rot = rope(q)
    gates_t = jax.nn.sigmoid(tail_ref[0]).T
    for h in range(NSA_KV_HEADS):
        hs = slice(h * HEAD_DIM, (h + 1) * HEAD_DIM)
        kk_ref[0, h] = jnp.concatenate([k_sel[:, hs].astype(bf16), k_win[:, hs].astype(bf16), onehot], axis=1)
        vvt_ref[0, h, 0] = jnp.concatenate([v_sel[:, hs], v_win[:, hs]], axis=1).T.astype(bf16)
        gw = NSA_GROUP * HEAD_DIM
        for src, dst in ((q, qr_ref), (q_rot, qo_ref)):
            t = src[:, h * gw:(h + 1) * gw].T
            dst[0, h, 0] = jnp.concatenate([t[g * HEAD_DIM:(g + 1) * HEAD_DIM] for g in range(NSA_GROUP)],
                                           axis=1).astype(bf16)
        base = TAIL_GATE + h * NSA_GROUP * 3
        g_ref[0, h, 0] = jnp.concatenate(
            [jnp.concatenate([gates_t[base + 3 * g + c:base + 3 * g + c + 1] for g in range(NSA_GROUP)], axis=1)
             for c in range(3)], axis=0)
    kc_in, vc_in = kv0[:, :kv_w].astype(bf16), kv0[:, kv_w:].astype(bf16)
    pooled_ref[0] = jnp.concatenate([_dot(pool_ref[0], kc_in), _dot(pool_ref[1], kc_in),
                                     _dot(pool_ref[2], vc_in), _dot(pool_ref[3], vc_in)], axis=1)


def _nsa_prep(h, pos, w_cmp_pool):
    bsz, t_, _ = h.shape
    nqb = t_ // Q_BLK
    bf16 = jnp.bfloat16
    half = ROPE_DIM // 2
    inv_freq = jnp.power(ROPE_THETA, -jnp.arange(half, dtype=jnp.float32) / half)
    ang = pos.astype(jnp.float32)[:, None] * inv_freq
    cos, sin = jnp.cos(ang), jnp.sin(ang)
    rest = HEAD_DIM - ROPE_DIM
    z8, zr = jnp.zeros((t_, half), jnp.float32), jnp.zeros((t_, rest), jnp.float32)
    two = lambda a: jnp.concatenate([a, a], axis=1)
    rc = two(jnp.concatenate([cos, cos, jnp.ones((t_, rest), jnp.float32)], axis=1))
    ru = two(jnp.concatenate([z8, sin, zr], axis=1))
    rd = two(jnp.concatenate([-sin, z8, zr], axis=1))
    pool = _pool_matrices(w_cmp_pool)
    kv_w = NSA_KV_HEADS * HEAD_DIM
    col = lambda width, off: pl.BlockSpec((1, Q_BLK, width), lambda b, i: (b, i, off // width))
    rows_t = pl.BlockSpec((Q_BLK, LANE), lambda b, i: (i, 0))
    head4 = lambda r, c: pl.BlockSpec((1, NSA_KV_HEADS, 1, r, c), lambda b, i: (b, 0, i, 0, 0))
    return pl.pallas_call(
        _nsa_prep_body,
        grid=(bsz, nqb),
        in_specs=[col(NSA_SIZES[0], COL_NQ), col(2 * kv_w, COL_NKV), col(2 * kv_w, COL_NKV + 2 * kv_w),
                  col(2 * kv_w, COL_NKV + 4 * kv_w), col(LANE, COL_TAIL), rows_t, rows_t, rows_t,
                  pl.BlockSpec((4, SUBS, Q_BLK), lambda b, i: (0, 0, 0))],
        out_specs=[pl.BlockSpec((1, Q_BLK, 4 * kv_w), lambda b, i: (b, i, 0)),
                   pl.BlockSpec((1, Q_BLK, 2 * kv_w), lambda b, i: (b, i, 0)),
                   pl.BlockSpec((1, NSA_KV_HEADS, Q_BLK, KK_W), lambda b, i: (b, 0, i, 0)),
                   head4(2 * HEAD_DIM, Q_BLK), head4(HEAD_DIM, NSA_ROWS), head4(HEAD_DIM, NSA_ROWS),
                   head4(3, NSA_ROWS),
                   pl.BlockSpec((1, SUBS, 4 * kv_w), lambda b, i: (b, i, 0))],
        out_shape=[jax.ShapeDtypeStruct((bsz, t_, 4 * kv_w), jnp.float32),
                   jax.ShapeDtypeStruct((bsz, t_, 2 * kv_w), jnp.float32),
                   jax.ShapeDtypeStruct((bsz, NSA_KV_HEADS, t_, KK_W), bf16),
                   jax.ShapeDtypeStruct((bsz, NSA_KV_HEADS, nqb, 2 * HEAD_DIM, Q_BLK), bf16),
                   jax.ShapeDtypeStruct((bsz, NSA_KV_HEADS, nqb, HEAD_DIM, NSA_ROWS), bf16),
                   jax.ShapeDtypeStruct((bsz, NSA_KV_HEADS, nqb, HEAD_DIM, NSA_ROWS), bf16),
                   jax.ShapeDtypeStruct((bsz, NSA_KV_HEADS, nqb, 3, NSA_ROWS), jnp.float32),
                   jax.ShapeDtypeStruct((bsz, t_ // CMP_STRIDE, 4 * kv_w), jnp.float32)],
        compiler_params=pltpu.CompilerParams(dimension_semantics=("arbitrary", "arbitrary")),
        name="nsa_prep",
    )(h, h, h, h, h, rc, ru, rd, pool)


PAGE_GROUP = 16
DEC_KEYS = PAGE_GROUP * PAGE_SIZE
NEW_PAD = 8
KV_W = NSA_KV_HEADS * HEAD_DIM


def _dec_pool_body(pt_ref, *refs):
    page_refs, pool_ref, out_ref = refs[:PAGE_GROUP], refs[PAGE_GROUP], refs[PAGE_GROUP + 1]
    bf16 = jnp.bfloat16
    pages = [pr[0] for pr in page_refs]
    kc_t = jnp.concatenate([p[:KV_W] for p in pages], axis=1).astype(bf16)
    vc_t = jnp.concatenate([p[KV_W:] for p in pages], axis=1).astype(bf16)
    out_ref[0] = jnp.concatenate([_dot(kc_t, pool_ref[0]), _dot(kc_t, pool_ref[1]),
                                  _dot(vc_t, pool_ref[2]), _dot(vc_t, pool_ref[3])], axis=0)


def _page_specs(n_pages, col_blk):
    def spec(i):
        return pl.BlockSpec((1, 2 * KV_W, PAGE_SIZE),
                            lambda b, j, pt: (pt[b * n_pages + j * PAGE_GROUP + i], col_blk, 0))
    return [spec(i) for i in range(PAGE_GROUP)]


def _dec_pool(cache, page_table, pool):
    bsz, n_pages = page_table.shape
    grid_spec = pltpu.PrefetchScalarGridSpec(
        num_scalar_prefetch=1, grid=(bsz, n_pages // PAGE_GROUP),
        in_specs=_page_specs(n_pages, 0) + [pl.BlockSpec(pool.shape, lambda b, j, pt: (0, 0, 0))],
        out_specs=pl.BlockSpec((1, 4 * KV_W, PAGE_GROUP * SUBS), lambda b, j, pt: (b, 0, j)))
    return pl.pallas_call(
        _dec_pool_body, grid_spec=grid_spec,
        out_shape=jax.ShapeDtypeStruct((bsz, 4 * KV_W, n_pages * SUBS), jnp.float32),
        compiler_params=pltpu.CompilerParams(dimension_semantics=("arbitrary", "arbitrary")),
        name="nsa_dec_pool",
    )(page_table.reshape(-1), *([cache] * PAGE_GROUP), pool)


def _dec_select_body(qr_ref, kct_ref, vc_ref, band_ref, oc_ref, selb_ref, *, qpos0, n_q, n_pick, n_blk):
    f32, bf16 = jnp.float32, jnp.bfloat16
    n_cmp = kct_ref.shape[3]
    rows = NSA_GROUP * n_q
    for sq, h in [(a, b) for a in range(qr_ref.shape[0]) for b in range(NSA_KV_HEADS)]:
        s_c = _dot(qr_ref[sq, h], kct_ref[sq, h])
        n_idx = lax.broadcasted_iota(jnp.int32, (rows, n_cmp), 1)
        qpos = qpos0 + (lax.broadcasted_iota(jnp.int32, (rows, n_cmp), 0) % n_q)
        cmask = (n_idx * CMP_STRIDE + (CMP_BLK - 1)) <= qpos
        s_c = jnp.where(cmask, s_c, MASKED)
        p_c = jnp.where(cmask, jnp.exp(s_c - jnp.max(s_c, axis=1, keepdims=True)), 0.0)
        p_c = p_c / jnp.maximum(jnp.sum(p_c, axis=1, keepdims=True), 1e-30)
        oc_ref[sq, h] = _dot(p_c.astype(bf16), vc_ref[sq, h])
        imp = p_c[0:n_q]
        for g in range(1, NSA_GROUP):
            imp = imp + p_c[g * n_q:(g + 1) * n_q]
        imp_s = jnp.zeros((n_q, N_SELB), f32)
        rem = imp
        for _ in range(3):
            part = rem.astype(bf16)
            imp_s = imp_s + _dot(part, band_ref[...])
            rem = rem - part.astype(f32)
        blk = lax.broadcasted_iota(jnp.int32, (n_q, N_SELB), 1)
        qpos_s = qpos0 + lax.broadcasted_iota(jnp.int32, (n_q, N_SELB), 0)
        cur = lax.shift_right_logical(qpos_s, int(math.log2(SEL_BLK)))
        valid = (blk * SEL_BLK <= qpos_s) & (blk < n_blk)
        forced = (blk == 0) | (blk == cur) | (blk == cur - 1)
        score = jnp.where(valid, imp_s + jnp.where(forced, FORCE_BONUS, 0.0), -1e30)
        picked = jnp.zeros((n_q, N_SELB), f32)
        for _ in range(n_pick):
            best = jnp.max(score, axis=1, keepdims=True)
            first = jnp.min(jnp.where(score == best, blk, N_SELB), axis=1, keepdims=True)
            hit = blk == first
            picked = jnp.where(hit, 1.0, picked)
            score = jnp.where(hit, -3e38, score)
        selb_ref[sq, h] = (jnp.where(valid, picked, 0.0) - 1.0) * (-MASKED)


def _dec_select(qr, kct, vc, n_q, qpos0, n_pick, n_blk):
    bsz = qr.shape[0]
    rows = NSA_GROUP * n_q
    n_cmp = kct.shape[3]
    ratio = SEL_BLK // CMP_STRIDE
    c_idx, j_idx = np.arange(n_cmp)[:, None], np.arange(N_SELB)[None, :]
    band = jnp.asarray(((c_idx >= ratio * j_idx - 1) & (c_idx <= ratio * j_idx + ratio - 1)), jnp.bfloat16)
    per_step = next(c for c in (4, 2, 1) if bsz % c == 0)
    per_b = lambda *tail: pl.BlockSpec((per_step, NSA_KV_HEADS) + tail, lambda b: (b, 0, 0, 0))
    return pl.pallas_call(
        functools.partial(_dec_select_body, qpos0=qpos0, n_q=n_q, n_pick=n_pick, n_blk=n_blk),
        grid=(bsz // per_step,),
        in_specs=[per_b(rows, HEAD_DIM), per_b(HEAD_DIM, n_cmp), per_b(n_cmp, HEAD_DIM),
                  pl.BlockSpec((n_cmp, N_SELB), lambda b: (0, 0))],
        out_specs=[per_b(rows, HEAD_DIM), per_b(n_q, N_SELB)],
        out_shape=[jax.ShapeDtypeStruct((bsz, NSA_KV_HEADS, rows, HEAD_DIM), jnp.float32),
                   jax.ShapeDtypeStruct((bsz, NSA_KV_HEADS, n_q, N_SELB), jnp.float32)],
        compiler_params=pltpu.CompilerParams(dimension_semantics=("arbitrary",)),
        name="nsa_dec_select",
    )(qr, kct, vc, band)


def _dec_attend_body(pt_ref, *refs, qpos0, n_q, past):
    page_refs = refs[:PAGE_GROUP]
    (qs_ref, qw_ref, knew_ref, vnew_ref, wbuf_ref, wnew_ref, oc_ref, g_ref,
     o_ref, m_ref, l_ref, acc_ref) = refs[PAGE_GROUP:]
    f32, bf16 = jnp.float32, jnp.bfloat16
    j = pl.program_id(1)
    n_rows = qs_ref.shape[1]

    @pl.when(j == 0)
    def _():
        m_ref[...] = jnp.full(m_ref.shape, MASKED, f32)
        l_ref[...] = jnp.zeros(l_ref.shape, f32)
        acc_ref[...] = jnp.zeros(acc_ref.shape, f32)

    def online(s, weigh):
        m_old = m_ref[...]
        m_new = jnp.maximum(m_old, jnp.max(s, axis=1, keepdims=True))
        alpha = jnp.exp(m_old - m_new)
        p = jnp.exp(s - m_new)
        l_ref[...] = alpha * l_ref[...] + jnp.sum(p, axis=1, keepdims=True)
        acc_ref[...] = alpha * acc_ref[...] + weigh(p.astype(bf16))
        m_ref[...] = m_new

    qs = qs_ref[0]
    pages = [pr[0] for pr in page_refs]
    keys_t = jnp.concatenate([p[:KV_W] for p in pages], axis=1).astype(bf16)
    vals_t = jnp.concatenate([p[KV_W:] for p in pages], axis=1).astype(bf16)
    blk_id = j * (DEC_KEYS // SEL_BLK) + lax.shift_right_logical(
        lax.broadcasted_iota(jnp.int32, (N_SELB, DEC_KEYS), 1), int(math.log2(SEL_BLK)))
    onehot_t = jnp.where(lax.broadcasted_iota(jnp.int32, (N_SELB, DEC_KEYS), 0) == blk_id, 1.0, 0.0).astype(bf16)
    online(_dot(qs, jnp.concatenate([keys_t, onehot_t], axis=0)), lambda p: _dot_nt(p, vals_t))

    @pl.when(j == pl.num_programs(1) - 1)
    def _():
        row_q = qpos0 + (lax.broadcasted_iota(jnp.int32, (n_rows, 1), 0) % n_q)
        qh = qw_ref[0]
        new_pos = past + lax.broadcasted_iota(jnp.int32, (n_rows, NEW_PAD), 1)
        new_ok = (new_pos <= row_q) & (new_pos < past + n_q)
        s_new = jnp.where(new_ok, _dot_nt(qh, knew_ref[0]), MASKED)
        online(s_new, lambda p: _dot(p, vnew_ref[0]))
        o_s = acc_ref[...] / l_ref[...]
        wbuf_t = wbuf_ref[0]
        wnew = wnew_ref[0]
        n_buf = wbuf_t.shape[1]
        s_b = _dot(qh, wbuf_t[:KV_W].astype(bf16))
        pos_b = (past - n_buf) + lax.broadcasted_iota(jnp.int32, (n_rows, n_buf), 1)
        s_b = jnp.where((pos_b > row_q - WINDOW) & (pos_b >= 0), s_b, MASKED)
        s_n = jnp.where(new_ok, _dot_nt(qh, wnew[:, :KV_W].astype(bf16)), MASKED)
        m_w = jnp.maximum(jnp.max(s_b, axis=1, keepdims=True), jnp.max(s_n, axis=1, keepdims=True))
        p_b, p_n = jnp.exp(s_b - m_w), jnp.exp(s_n - m_w)
        l_w = jnp.sum(p_b, axis=1, keepdims=True) + jnp.sum(p_n, axis=1, keepdims=True)
        o_w = (_dot_nt(p_b.astype(bf16), wbuf_t[KV_W:].astype(bf16))
               + _dot(p_n.astype(bf16), wnew[:, KV_W:].astype(bf16))) / l_w
        half = n_rows // NSA_KV_HEADS
        own = lambda a: jnp.concatenate([a[h * half:(h + 1) * half, h * HEAD_DIM:(h + 1) * HEAD_DIM]
                                         for h in range(NSA_KV_HEADS)], axis=0)
        g = g_ref[0]
        o_ref[0] = g[:, 0:1] * oc_ref[0] + g[:, 1:2] * own(o_s) + g[:, 2:3] * own(o_w)


def _dec_attend(cache, page_table, qs, qw, knew, vnew, wbuf, wnew, o_c, gates, n_q, qpos0):
    bsz, n_pages = page_table.shape
    n_rows = qs.shape[1]
    per_b = lambda *tail: pl.BlockSpec((1,) + tail, lambda b, j, pt: (b, 0, 0))
    grid_spec = pltpu.PrefetchScalarGridSpec(
        num_scalar_prefetch=1, grid=(bsz, n_pages // PAGE_GROUP),
        in_specs=_page_specs(n_pages, 1) + [
            per_b(n_rows, KV_W + N_SELB), per_b(n_rows, KV_W), per_b(NEW_PAD, KV_W), per_b(NEW_PAD, KV_W),
            per_b(2 * KV_W, wbuf.shape[2]), per_b(NEW_PAD, 2 * KV_W), per_b(n_rows, HEAD_DIM), per_b(n_rows, 3)],
        out_specs=per_b(n_rows, HEAD_DIM),
        scratch_shapes=[pltpu.VMEM((n_rows, 1), jnp.float32), pltpu.VMEM((n_rows, 1), jnp.float32),
                        pltpu.VMEM((n_rows, KV_W), jnp.float32)])
    return pl.pallas_call(
        functools.partial(_dec_attend_body, qpos0=qpos0, n_q=n_q, past=n_pages * PAGE_SIZE),
        grid_spec=grid_spec,
        out_shape=jax.ShapeDtypeStruct((bsz, n_rows, HEAD_DIM), jnp.float32),
        compiler_params=pltpu.CompilerParams(dimension_semantics=("arbitrary", "arbitrary")),
        name="nsa_dec_attend",
    )(page_table.reshape(-1), *([cache] * PAGE_GROUP), qs, qw, knew, vnew, wbuf, wnew, o_c, gates)


def _pool_matrices(w_cmp_pool, rows=Q_BLK):
    subs = rows // CMP_STRIDE
    sub = np.arange(rows) // CMP_STRIDE == np.arange(subs)[:, None]
    w_rep = jnp.tile(w_cmp_pool.reshape(2, 2, CMP_STRIDE), (1, 1, subs))
    return jnp.where(sub[None, None], w_rep[:, :, None, :], 0.0).reshape(4, subs, rows).astype(jnp.bfloat16)


def _nsa_decode(q_raw, q_rot, gates, rows_full, rows_win, cache, page_table, win_buf, w_cmp_pool, past):
    bsz, n_q = q_raw.shape[:2]
    bf16 = jnp.bfloat16
    n_blk = past // SEL_BLK
    assert past % DEC_KEYS == 0 and n_blk <= N_SELB and n_q <= NEW_PAD
    scale = HEAD_DIM ** -0.5
    cache2 = cache.transpose(0, 2, 3, 4, 1).reshape(cache.shape[0], 4 * KV_W, PAGE_SIZE)
    pooled_t = _dec_pool(cache2, page_table, _pool_matrices(w_cmp_pool, DEC_KEYS).transpose(0, 2, 1))
    pooled_t = pooled_t.reshape(bsz, 4, NSA_KV_HEADS, HEAD_DIM, -1)
    last = ((0, 0), (0, 0), (0, 0), (0, 1))
    kct = jnp.pad(pooled_t[:, 0, ..., :-1] + pooled_t[:, 1, ..., 1:], last)
    vc_p = jnp.pad(pooled_t[:, 2, ..., :-1] + pooled_t[:, 3, ..., 1:], last).transpose(0, 1, 3, 2)
    rows_of = lambda a: a.transpose(0, 2, 3, 1, 4).reshape(bsz, NSA_KV_HEADS, NSA_GROUP * n_q, a.shape[-1])
    qr = rows_of((q_raw * scale).astype(bf16))
    n_pick = min(SEL_TOPN, n_blk + 1) - 1
    o_c, selb = _dec_select(qr, kct.astype(bf16), vc_p.astype(bf16), n_q, past, n_pick, n_blk)
    qo = rows_of((q_rot * scale).astype(bf16))
    zero = jnp.zeros_like(qo[:, 0])
    qw = jnp.concatenate([jnp.concatenate([qo[:, 0], zero], -1), jnp.concatenate([zero, qo[:, 1]], -1)], axis=1)
    bias = jnp.tile(selb, (1, 1, NSA_GROUP, 1)).reshape(bsz, -1, N_SELB).astype(bf16)
    qs = jnp.concatenate([qw, bias], axis=-1)
    pad_new = lambda a: jnp.pad(a.reshape(bsz, n_q, -1), ((0, 0), (0, NEW_PAD - n_q), (0, 0)))
    knew = pad_new(rows_full[:, :, 2]).astype(bf16)
    vnew = pad_new(rows_full[:, :, 3]).astype(bf16)
    wnew = pad_new(rows_win)
    wbuf = win_buf.transpose(0, 2, 3, 4, 1).reshape(bsz, 2 * KV_W, win_buf.shape[1])
    gt = rows_of(gates).reshape(bsz, -1, 3)
    o = _dec_attend(cache2, page_table, qs, qw, knew, vnew, wbuf, wnew,
                    o_c.reshape(bsz, -1, HEAD_DIM), gt, n_q, past)
    o = o.reshape(bsz, NSA_KV_HEADS, NSA_GROUP, n_q, HEAD_DIM).transpose(0, 3, 1, 2, 4)
    return o.reshape(bsz, n_q, NSA_HEADS * HEAD_DIM)


def _ab_mixer(x, pos, w_in, w_gla_gate, b_gla_gate, gla_norm_g, w_cmp_pool, w_out,
              gla_state, nsa_cache, page_table, win_buf):
    bsz, t_, _ = x.shape
    h_in = _mm(x.reshape(bsz * t_, -1), w_in[:, IN_AB_PERM], keep_pad=True).reshape(bsz, t_, -1)
    o_a, s_a = _gla(h_in, w_gla_gate, b_gla_gate, gla_norm_g, gla_state)
    kv_w = NSA_KV_HEADS * HEAD_DIM
    if nsa_cache is None:
        rows2, win2, kk, vvt, qr, qo, gt, pooled = _nsa_prep(h_in, pos, w_cmp_pool)
        pooled = pooled.reshape(bsz, t_ // CMP_STRIDE, 4, NSA_KV_HEADS, HEAD_DIM)
        kc = pooled[:, :-1, 0] + pooled[:, 1:, 1]
        vc = pooled[:, :-1, 2] + pooled[:, 1:, 3]
        kc_p = jnp.pad(kc, ((0, 0), (0, 1), (0, 0), (0, 0))).transpose(0, 2, 1, 3).astype(jnp.bfloat16)
        vct = jnp.pad(vc, ((0, 0), (0, 1), (0, 0), (0, 0))).transpose(0, 2, 3, 1).astype(jnp.bfloat16)
        o_b = _nsa_prompt(qr, qo, gt, kc_p, vct, kk, vvt)
        rows_full = rows2.reshape(bsz, t_, 4, NSA_KV_HEADS, HEAD_DIM)
        new_win = win2[:, -min(WINDOW, t_):].reshape(bsz, -1, 2, NSA_KV_HEADS, HEAD_DIM)
    else:
        nq = h_in[..., COL_NQ:COL_NKV]
        nkv = h_in[..., COL_NKV:COL_TAIL]
        ngate = h_in[..., COL_TAIL + TAIL_GATE:COL_TAIL + TAIL_GATE + NSA_SIZES[2]]
        q_raw = nq.reshape(bsz, t_, NSA_KV_HEADS, NSA_GROUP, HEAD_DIM)
        q_rot = _partial_rope(q_raw, pos)
        kv = nkv.reshape(bsz, t_, 6, NSA_KV_HEADS, HEAD_DIM)
        k_sel = _partial_rope(kv[:, :, 2], pos)
        k_win = _partial_rope(kv[:, :, 4], pos)
        rows_full = jnp.stack([kv[:, :, 0], kv[:, :, 1], k_sel, kv[:, :, 3]], axis=2)
        rows_win = jnp.stack([k_win, kv[:, :, 5]], axis=2)
        gates = jax.nn.sigmoid(ngate).reshape(bsz, t_, NSA_KV_HEADS, NSA_GROUP, 3)
        past_len = page_table.shape[1] * PAGE_SIZE
        o_b = _nsa_decode(q_raw, q_rot, gates, rows_full, rows_win, nsa_cache, page_table, win_buf,
                          w_cmp_pool, past_len)
        w_buf = win_buf.shape[1]
        kw = jnp.concatenate([win_buf, rows_win], axis=1)
        new_win = kw[:, -w_buf:]
    y = _mm_pair(o_a.reshape(bsz * t_, -1), o_b.reshape(bsz * t_, -1), w_out).reshape(bsz, t_, -1)
    return y, s_a, rows_full, new_win


CONV_HALO = 32
CONV_LEAD = CONV_HALO - (CONV_W - 1)


def _conv_body(x_ref, buf0_ref, w1_ref, b1_ref, wdw_ref, bdw_ref, g_ref, b_ref, w2_ref, b2_ref,
               o_ref, tail_ref, ext_ref, z_ref, *, t_last):
    bf16 = jnp.bfloat16
    tt = x_ref.shape[1]
    i = pl.program_id(1)

    @pl.when(i == 0)
    def _():
        ext_ref[0:CONV_HALO, :] = buf0_ref[0]
        ext_ref[CONV_HALO + tt:CONV_HALO + tt + SUBLANES, :] = jnp.zeros((SUBLANES, D_CONV), jnp.float32)

    h = _dot(x_ref[0].astype(bf16), w1_ref[...]) + b1_ref[...]
    ext_ref[CONV_HALO:CONV_HALO + tt, :] = h[:, :D_CONV] * jax.nn.sigmoid(h[:, D_CONV:])
    c = jnp.zeros((tt, D_CONV), jnp.float32) + bdw_ref[...]
    for r in range(SUBLANES):
        z = None
        for a in range(CONV_HALO // SUBLANES + 1):
            k = SUBLANES * a + r - CONV_LEAD
            if 0 <= k < CONV_W:
                term = ext_ref[SUBLANES * a:SUBLANES * a + tt + SUBLANES, :] * wdw_ref[k:k + 1, :]
                z = term if z is None else z + term
        if r == 0:
            c = c + z[:tt]
        else:
            z_ref[...] = z
            c = c + z_ref[pl.ds(r, tt), :]
    c = _ln_rows(c, g_ref[...], b_ref[...])
    c = c * jax.nn.sigmoid(c)
    o_ref[0] = _dot(c.astype(bf16), w2_ref[...]) + b2_ref[...]
    tail_ref[0] = ext_ref[t_last:t_last + CONV_HALO, :]
    ext_ref[0:CONV_HALO, :] = ext_ref[tt:tt + CONV_HALO, :]


def _conv_module(x, conv_buf, w_pw1, b_pw1, w_dw, b_dw, ln_g, ln_b, w_pw2, b_pw2):
    bsz, t_, d = x.shape
    bf16 = jnp.bfloat16
    tp = -(-t_ // 8) * 8
    tt = min(tp, 256)
    n_t = tp // tt
    if tp != t_:
        x = jnp.pad(x, ((0, 0), (0, tp - t_), (0, 0)))
    if conv_buf is None:
        buf0 = jnp.zeros((bsz, CONV_HALO, D_CONV), jnp.float32)
    else:
        buf0 = jnp.pad(conv_buf, ((0, 0), (CONV_LEAD, 0), (0, 0)))
    fixed = lambda shape: pl.BlockSpec(shape, lambda b, i: (0,) * len(shape))
    per_b = pl.BlockSpec((1, CONV_HALO, D_CONV), lambda b, i: (b, 0, 0))
    out, tail = pl.pallas_call(
        functools.partial(_conv_body, t_last=t_ - (n_t - 1) * tt),
        grid=(bsz, n_t),
        in_specs=[pl.BlockSpec((1, tt, d), lambda b, i: (b, i, 0)), per_b,
                  fixed((d, 2 * D_CONV)), fixed((1, 2 * D_CONV)), fixed((CONV_HALO, D_CONV)), fixed((1, D_CONV)),
                  fixed((1, D_CONV)), fixed((1, D_CONV)), fixed((D_CONV, d)), fixed((1, d))],
        out_specs=[pl.BlockSpec((1, tt, d), lambda b, i: (b, i, 0)), per_b],
        out_shape=[jax.ShapeDtypeStruct((bsz, tp, d), jnp.float32),
                   jax.ShapeDtypeStruct((bsz, CONV_HALO, D_CONV), jnp.float32)],
        scratch_shapes=[pltpu.VMEM((CONV_HALO + tt + SUBLANES, D_CONV), jnp.float32),
                        pltpu.VMEM((tt + SUBLANES, D_CONV), jnp.float32)],
        compiler_params=pltpu.CompilerParams(dimension_semantics=("arbitrary", "arbitrary"),
                                             vmem_limit_bytes=VMEM_LIMIT),
        name="conv_module",
    )(x, buf0, w_pw1.astype(bf16), b_pw1.reshape(1, -1), jnp.pad(w_dw, ((0, CONV_HALO - CONV_W), (0, 0))),
      b_dw.reshape(1, -1), ln_g.reshape(1, -1), ln_b.reshape(1, -1), w_pw2.astype(bf16), b_pw2.reshape(1, -1))
    return out[:, :t_], tail[:, CONV_LEAD:]


PACK_W = 256
SC_WINDOW = 128
SC_TILES = 32


def _pack_rows(y):
    out = []
    for h in range(2):
        lo = lax.bitcast_convert_type(y[:, 2 * h * PACK_W:(2 * h + 1) * PACK_W].astype(jnp.bfloat16)
                                      .astype(jnp.float32), jnp.uint32)
        hi = lax.bitcast_convert_type(y[:, (2 * h + 1) * PACK_W:(2 * h + 2) * PACK_W].astype(jnp.bfloat16)
                                      .astype(jnp.float32), jnp.uint32)
        out.append(lax.bitcast_convert_type((lo >> 16) | hi, jnp.int32))
    return out


def _unpack_words(w):
    u = lax.bitcast_convert_type(w, jnp.uint32)
    lo = lax.bitcast_convert_type(u << 16, jnp.float32)
    hi = lax.bitcast_convert_type(u & jnp.uint32(0xFFFF0000), jnp.float32)
    return lo, hi


def _gather_rows(src, idx):
    n = idx.shape[0]
    if n % (SC_WINDOW * SC_TILES) != 0:
        return jnp.take(src, idx, axis=0)
    mesh = plsc.VectorSubcoreMesh(core_axis_name="core", subcore_axis_name="subcore")

    @pl.kernel(out_type=jax.ShapeDtypeStruct((n, src.shape[1]), src.dtype), mesh=mesh)
    def gather_kernel(src_hbm, idx_hbm, out_hbm):
        def step(idx_vmem, out_vmem):
            pltpu.sync_copy(src_hbm.at[idx_vmem.at[0]], out_vmem)

        pltpu.emit_pipeline(
            step, grid=(n // SC_WINDOW,),
            in_specs=[pl.BlockSpec((1, SC_WINDOW), index_map=lambda i: (0, i))],
            out_specs=[pl.BlockSpec((SC_WINDOW, src.shape[1]), index_map=lambda i: (i, 0))],
            core_axis_name=("core", "subcore"),
            dimension_semantics=(pltpu.PARALLEL,),
        )(idx_hbm, out_hbm)

    return gather_kernel(src, idx.reshape(1, n))


def _scatter_rows(src, idx, n_out):
    n = idx.shape[0]
    m = src.shape[0] // 2
    reps = n // (2 * m)
    if n % (SC_WINDOW * SC_TILES) != 0 or m % SC_WINDOW != 0:
        rows = jnp.arange(n, dtype=jnp.int32)
        src_row = (rows // (reps * m)) * m + rows % m
        return jnp.zeros((n_out, src.shape[1]), src.dtype).at[idx].set(jnp.take(src, src_row, axis=0))
    tiles = m // SC_WINDOW
    mesh = plsc.VectorSubcoreMesh(core_axis_name="core", subcore_axis_name="subcore")

    @pl.kernel(out_type=jax.ShapeDtypeStruct((n_out, src.shape[1]), src.dtype), mesh=mesh, scratch_types=[])
    def scatter_kernel(src_hbm, idx_hbm, out_hbm):
        def step(src_vmem, idx_vmem):
            pltpu.sync_copy(src_vmem, out_hbm.at[idx_vmem.at[0]])

        pltpu.emit_pipeline(
            step, grid=(n // SC_WINDOW,),
            in_specs=[pl.BlockSpec((SC_WINDOW, src.shape[1]),
                                   index_map=lambda i: ((i // (reps * tiles)) * tiles + i % tiles, 0)),
                      pl.BlockSpec((1, SC_WINDOW), index_map=lambda i: (0, i))],
            out_specs=[],
            core_axis_name=("core", "subcore"),
            dimension_semantics=(pltpu.PARALLEL,),
        )(src_hbm, idx_hbm)

    return scatter_kernel(src, idx.reshape(1, n))


PER_GROUP = N_EXPERTS // N_GROUPS
PICKED = -3e38


def _ln_rows(v, g, b):
    mu = jnp.mean(v, axis=-1, keepdims=True)
    c = v - mu
    var = jnp.mean(c * c, axis=-1, keepdims=True)
    return c * lax.rsqrt(var + LN_EPS) * g + b


def _first_max(v, ids, axes, sentinel):
    best = v
    for a in axes:
        best = jnp.max(best, axis=a, keepdims=True)
    first = jnp.where(v == best, ids, sentinel)
    for a in axes:
        first = jnp.min(first, axis=a, keepdims=True)
    return best, first


def _sum_axes(v, axes):
    for a in axes:
        v = jnp.sum(v, axis=a, keepdims=True)
    return v


def _moe_pre_body(x_ref, mix_ref, g_ref, b_ref, wr_ref, br_ref, wgu_ref, wdn_ref,
                  x1_ref, xp_ref, sh_ref, eidx_ref, gate_ref, rank_ref, cnt_ref, run_ref):
    f32, bf16 = jnp.float32, jnp.bfloat16
    tm = x_ref.shape[0]

    @pl.when(pl.program_id(0) == 0)
    def _():
        run_ref[...] = jnp.zeros(run_ref.shape, f32)

    x1 = _ln_rows(ALPHA * x_ref[...] + mix_ref[...], g_ref[...], b_ref[...])
    x1_ref[...] = x1
    x1b = x1.astype(bf16)
    xp_ref[0], xp_ref[1] = _pack_rows(x1)

    h = _dot(x1b, wgu_ref[...])
    d_sh = h.shape[1] // 2
    act = (jax.nn.silu(h[:, :d_sh]) * h[:, d_sh:]).astype(bf16)
    sh_ref[...] = _dot(act, wdn_ref[...])

    s = jax.nn.sigmoid(_dot_nt(wr_ref[...], x1b)).reshape(N_GROUPS, PER_GROUP, tm)
    sb = s + br_ref[...].reshape(N_GROUPS, PER_GROUP, 1)
    shape3 = (N_GROUPS, PER_GROUP, tm)
    pid = lax.broadcasted_iota(jnp.int32, shape3, 1)
    gid = lax.broadcasted_iota(jnp.int32, (N_GROUPS, 1, tm), 0)
    eid = lax.broadcasted_iota(jnp.int32, shape3, 0) * PER_GROUP + pid
    top1, i1 = _first_max(sb, pid, (1,), PER_GROUP)
    top2 = jnp.max(jnp.where(pid == i1, PICKED, sb), axis=1, keepdims=True)
    gscore = top1 + top2
    gsel = jnp.zeros((N_GROUPS, 1, tm), f32)
    for _ in range(TOPK_GROUPS):
        _, first = _first_max(gscore, gid, (0,), N_GROUPS)
        hit = gid == first
        gsel = jnp.where(hit, 1.0, gsel)
        gscore = jnp.where(hit, PICKED, gscore)
    cand = jnp.where(gsel > 0.0, sb, -1e30)
    firsts, gates = [], []
    picked = jnp.zeros(shape3, f32)
    for _ in range(TOP_K):
        _, first = _first_max(cand, eid, (0, 1), N_EXPERTS)
        hit = eid == first
        firsts.append(first)
        gates.append(_sum_axes(jnp.where(hit, s, 0.0), (0, 1)))
        picked = jnp.where(hit, 1.0, picked)
        cand = jnp.where(hit, PICKED, cand)
    gsum = gates[0]
    for gk in gates[1:]:
        gsum = gsum + gk
    earlier = (lax.broadcasted_iota(jnp.int32, (tm, tm), 0) < lax.broadcasted_iota(jnp.int32, (tm, tm), 1))
    picked2 = picked.reshape(N_EXPERTS, tm)
    rank = run_ref[...] + _dot(picked2.astype(bf16), jnp.where(earlier, 1.0, 0.0).astype(bf16))
    run_new = run_ref[...] + jnp.sum(picked2, axis=1, keepdims=True)
    run_ref[...] = run_new
    cnt_ref[...] = jnp.broadcast_to(run_new, cnt_ref.shape)
    rank3 = rank.reshape(shape3)
    for k in range(TOP_K):
        hit = eid == firsts[k]
        eidx_ref[k:k + 1, :] = firsts[k].reshape(1, tm)
        gate_ref[k:k + 1, :] = (gates[k] / gsum * ROUTE_SCALE).reshape(1, tm)
        rank_ref[k:k + 1, :] = _sum_axes(jnp.where(hit, rank3, 0.0), (0, 1)).reshape(1, tm).astype(jnp.int32)


def _moe_pre(x, mix, g, b, w_router, b_router, w_sh_gu, w_sh_down):
    m, d = x.shape
    bf16 = jnp.bfloat16
    tm = min(m, 512)
    row = lambda i: (i, 0)
    col = lambda i: (0, i)
    fixed = lambda i: (0, 0)
    d_sh2 = w_sh_gu.shape[1]
    return pl.pallas_call(
        _moe_pre_body,
        grid=(m // tm,),
        in_specs=[pl.BlockSpec((tm, d), row), pl.BlockSpec((tm, d), row),
                  pl.BlockSpec((1, d), fixed), pl.BlockSpec((1, d), fixed),
                  pl.BlockSpec((N_EXPERTS, d), fixed), pl.BlockSpec((N_EXPERTS, 1), fixed),
                  pl.BlockSpec((d, d_sh2), fixed), pl.BlockSpec((d_sh2 // 2, d), fixed)],
        out_specs=[pl.BlockSpec((tm, d), row), pl.BlockSpec((2, tm, PACK_W), lambda i: (0, i, 0)),
                   pl.BlockSpec((tm, d), row),
                   pl.BlockSpec((TOP_K, tm), col), pl.BlockSpec((TOP_K, tm), col), pl.BlockSpec((TOP_K, tm), col),
                   pl.BlockSpec((N_EXPERTS, LANE), fixed)],
        out_shape=[jax.ShapeDtypeStruct((m, d), jnp.float32), jax.ShapeDtypeStruct((2, m, PACK_W), jnp.int32),
                   jax.ShapeDtypeStruct((m, d), jnp.float32),
                   jax.ShapeDtypeStruct((TOP_K, m), jnp.int32), jax.ShapeDtypeStruct((TOP_K, m), jnp.float32),
                   jax.ShapeDtypeStruct((TOP_K, m), jnp.int32),
                   jax.ShapeDtypeStruct((N_EXPERTS, LANE), jnp.float32)],
        scratch_shapes=[pltpu.VMEM((N_EXPERTS, 1), jnp.float32)],
        compiler_params=pltpu.CompilerParams(dimension_semantics=("arbitrary",),
                                             vmem_limit_bytes=VMEM_LIMIT),
        name="moe_pre",
    )(x, mix, g.reshape(1, d), b.reshape(1, d), w_router.T.astype(bf16), b_router.reshape(N_EXPERTS, 1),
      w_sh_gu.astype(bf16), w_sh_down.astype(bf16))


def _moe_expert_body(exp_ref, first_ref, rows_ref, xs_ref, wgu_ref, wdn_ref, y_ref, wgu_bf, wdn_bf):
    i = pl.program_id(0)
    bf16 = jnp.bfloat16

    @pl.when(first_ref[i] == 1)
    def _():
        wgu_bf[...] = wgu_ref[0, 0].astype(bf16)
        wdn_bf[...] = wdn_ref[0, 0].astype(bf16)

    @pl.when(rows_ref[i] > 0)
    def _():
        live = lax.broadcasted_iota(jnp.int32, (xs_ref.shape[1], 1), 0) < rows_ref[i]
        h = None
        for hw in range(2):
            for q, xq in enumerate(_unpack_words(xs_ref[hw])):
                r0 = (2 * hw + q) * PACK_W
                part = _dot(jnp.where(live, xq, 0.0).astype(bf16), wgu_bf[r0:r0 + PACK_W, :])
                h = part if h is None else h + part
        d_e = h.shape[1] // 2
        act = (jax.nn.silu(h[:, :d_e]) * h[:, d_e:]).astype(bf16)
        y_ref[0], y_ref[1] = _pack_rows(_dot(act, wdn_bf[...]))

    @pl.when(rows_ref[i] == 0)
    def _():
        y_ref[...] = jnp.zeros(y_ref.shape, y_ref.dtype)


def _moe_experts(xs, blk_exp, blk_first, blk_rows, w_exp_gu, w_exp_down, layer, bm):
    n_slots = xs.shape[1]
    d = w_exp_gu.shape[2]
    n_blk = n_slots // bm
    d_e2 = w_exp_gu.shape[3]
    words = lambda i, e, f, a: (0, i, 0)
    grid_spec = pltpu.PrefetchScalarGridSpec(
        num_scalar_prefetch=3,
        grid=(n_blk,),
        in_specs=[pl.BlockSpec((2, bm, PACK_W), words),
                  pl.BlockSpec((1, 1, d, d_e2), lambda i, e, f, a: (layer, e[i], 0, 0)),
                  pl.BlockSpec((1, 1, d_e2 // 2, d), lambda i, e, f, a: (layer, e[i], 0, 0))],
        out_specs=pl.BlockSpec((2, bm, PACK_W), words),
        scratch_shapes=[pltpu.VMEM((d, d_e2), jnp.bfloat16), pltpu.VMEM((d_e2 // 2, d), jnp.bfloat16)])
    return pl.pallas_call(
        _moe_expert_body,
        grid_spec=grid_spec,
        out_shape=jax.ShapeDtypeStruct((2, n_slots, PACK_W), jnp.int32),
        compiler_params=pltpu.CompilerParams(dimension_semantics=("arbitrary",),
                                             vmem_limit_bytes=VMEM_LIMIT),
        name="moe_experts",
    )(blk_exp, blk_first, blk_rows, xs, w_exp_gu, w_exp_down)


def _combine_ln_body(x_ref, yg_ref, gt_ref, sh_ref, g_ref, b_ref, o_ref):
    gt = gt_ref[...]
    parts = []
    for hw in range(2):
        lo_acc = hi_acc = None
        for k in range(TOP_K):
            lo, hi = _unpack_words(yg_ref[hw, k])
            gk = gt[:, k:k + 1]
            lo_acc = lo * gk if lo_acc is None else lo_acc + lo * gk
            hi_acc = hi * gk if hi_acc is None else hi_acc + hi * gk
        parts += [lo_acc, hi_acc]
    routed = jnp.concatenate(parts, axis=1)
    o_ref[...] = _ln_rows(ALPHA * x_ref[...] + (routed + sh_ref[...]), g_ref[...], b_ref[...])


def _combine_ln(x, yg, gate_t, shared, g, b):
    m, d = x.shape
    tm = min(m, 256)
    row = lambda i: (i, 0)
    fixed = lambda i: (0, 0)
    return pl.pallas_call(
        _combine_ln_body,
        grid=(m // tm,),
        in_specs=[pl.BlockSpec((tm, d), row), pl.BlockSpec((2, TOP_K, tm, PACK_W), lambda i: (0, 0, i, 0)),
                  pl.BlockSpec((tm, TOP_K), row), pl.BlockSpec((tm, d), row),
                  pl.BlockSpec((1, d), fixed), pl.BlockSpec((1, d), fixed)],
        out_specs=pl.BlockSpec((tm, d), row),
        out_shape=jax.ShapeDtypeStruct((m, d), jnp.float32),
        compiler_params=pltpu.CompilerParams(dimension_semantics=("arbitrary",)),
        name="combine_ln",
    )(x, yg, gate_t, shared, g.reshape(1, d), b.reshape(1, d))


def _moe_layer(x, mix, ln1_g, ln1_b, ln2_g, ln2_b, w_router, b_router, w_exp_gu, w_exp_down, layer,
               w_sh_gu, w_sh_down):
    m, d = x.shape
    x1, xp, shared, eidx, gate8, rank8, counts = _moe_pre(x, mix, ln1_g, ln1_b, w_router, b_router,
                                                           w_sh_gu, w_sh_down)
    bm = 512 if m * TOP_K >= 512 * N_EXPERTS else MOE_BLK
    n_blk = (m * TOP_K) // bm + N_EXPERTS
    counts = counts[:, 0].astype(jnp.int32)
    padded = (counts + bm - 1) // bm * bm
    pad_end = jnp.cumsum(padded)
    pad_start = pad_end - padded
    start_of = jnp.sum(jnp.where(eidx[:, :, None] == jnp.arange(N_EXPERTS), pad_start, 0), axis=-1)
    dest = (start_of + rank8).reshape(-1)
    blk_start = jnp.arange(n_blk, dtype=jnp.int32) * bm
    blk_exp = jnp.minimum(jnp.sum(pad_end[None, :] <= blk_start[:, None], axis=1), N_EXPERTS - 1).astype(jnp.int32)
    blk_rows = jnp.clip(counts[blk_exp] - (blk_start - pad_start[blk_exp]), 0, bm).astype(jnp.int32)
    blk_first = jnp.concatenate([jnp.ones((1,), jnp.int32), (blk_exp[1:] != blk_exp[:-1]).astype(jnp.int32)])
    n_slots = n_blk * bm
    xs = _scatter_rows(xp.reshape(2 * m, PACK_W), jnp.concatenate([dest, dest + n_slots]), 2 * n_slots)
    y = _moe_experts(xs.reshape(2, n_slots, PACK_W), blk_exp, blk_first, blk_rows, w_exp_gu, w_exp_down, layer, bm)
    yg = _gather_rows(y.reshape(2 * n_slots, PACK_W), jnp.concatenate([dest, dest + n_slots]))
    return _combine_ln(x1, yg.reshape(2, TOP_K, m, PACK_W), gate8.T, shared, ln2_g, ln2_b)


def _trunk(x, pos, gla_state, nsa_cache, page_table, win_buf, conv_buf,
           w_in_ab, w_gla_gate, b_gla_gate, gla_norm_g, w_cmp_pool, w_out_ab,
           w_pw1, b_pw1, w_dw, b_dw, conv_ln_g, conv_ln_b, w_pw2, b_pw2,
           ln_g, ln_b, w_router, b_router, w_exp_gu, w_exp_down, w_sh_gu, w_sh_down):
    new_gla, new_rows, new_win, new_conv = [], [], [], []
    for layer in range(DEPTH):
        i = layer // 2
        if layer % 2 == 0:
            mix, s_a, rows, win = _ab_mixer(
                x, pos, w_in_ab[i], w_gla_gate[i], b_gla_gate[i], gla_norm_g[i], w_cmp_pool[i], w_out_ab[i],
                None if gla_state is None else gla_state[i],
                None if nsa_cache is None else nsa_cache[i], page_table,
                None if win_buf is None else win_buf[i])
            new_gla.append(s_a)
            new_rows.append(rows)
            new_win.append(win)
        else:
            mix, cb = _conv_module(x, None if conv_buf is None else conv_buf[i], w_pw1[i], b_pw1[i],
                                   w_dw[i], b_dw[i], conv_ln_g[i], conv_ln_b[i], w_pw2[i], b_pw2[i])
            new_conv.append(cb)
        bsz, t_, d = x.shape
        x = _moe_layer(x.reshape(-1, d), mix.reshape(-1, d), ln_g[layer, 0], ln_b[layer, 0],
                       ln_g[layer, 1], ln_b[layer, 1], w_router[layer], b_router[layer],
                       w_exp_gu, w_exp_down, layer, w_sh_gu[layer], w_sh_down[layer]).reshape(bsz, t_, d)
    return x, jnp.stack(new_gla), jnp.stack(new_rows), jnp.stack(new_win), jnp.stack(new_conv)


def kernel(x_prompt, x_sample, state_gla, cache_nsa_kv, state_nsa_win, state_conv, page_table,
           w_in_ab, w_gla_gate, b_gla_gate, gla_norm_g, w_cmp_pool, w_out_ab,
           w_pw1, b_pw1, w_dw, b_dw, conv_ln_g, conv_ln_b, w_pw2, b_pw2,
           ln_g, ln_b, w_router, b_router, w_exp_gu, w_exp_down, w_sh_gu, w_sh_down):
    weights = (w_in_ab, w_gla_gate, b_gla_gate, gla_norm_g, w_cmp_pool, w_out_ab,
               w_pw1, b_pw1, w_dw, b_dw, conv_ln_g, conv_ln_b, w_pw2, b_pw2,
               ln_g, ln_b, w_router, b_router, w_exp_gu, w_exp_down, w_sh_gu, w_sh_down)
    past_len = page_table.shape[1] * PAGE_SIZE
    pos_p = jnp.arange(x_prompt.shape[1])
    pos_s = past_len + jnp.arange(x_sample.shape[1])
    y_prompt, gla_p, rows_p, win_p, conv_p = _trunk(x_prompt, pos_p, None, None, None, None, None, *weights)
    y_sample, gla_s, rows_s, win_s, conv_s = _trunk(x_sample, pos_s, state_gla, cache_nsa_kv, page_table,
                                                    state_nsa_win, state_conv, *weights)
    return (y_prompt, y_sample, gla_p, gla_s, rows_p, rows_s, win_p, win_s, conv_p, conv_s)
```

```python
import functools
import math

import jax
import jax.numpy as jnp
import numpy as np
from jax import lax
from jax.experimental import pallas as pl
from jax.experimental.pallas import tpu as pltpu
from jax.experimental.pallas import tpu_sc as plsc

D_MODEL = 1024
DEPTH = 2
PAGE_SIZE = 128

GLA_HEADS = 4
GLA_DV = D_MODEL // 2 // GLA_HEADS
GLA_DK = GLA_DV // 2
GLA_RANK = 16
GLA_TAU = 16.0

NSA_HEADS = 8
NSA_KV_HEADS = 2
NSA_GROUP = NSA_HEADS // NSA_KV_HEADS
HEAD_DIM = D_MODEL // 2 // NSA_HEADS
CMP_BLK = 32
CMP_STRIDE = 16
SEL_BLK = 64
SEL_TOPN = 16
WINDOW = 512
Q_BLK = 128
FORCE_BONUS = 100.0
ROPE_DIM = HEAD_DIM // 4
ROPE_THETA = 500000.0

GLA_SIZES = (GLA_HEADS * GLA_DK, GLA_HEADS * GLA_DK, GLA_HEADS * GLA_DV, GLA_HEADS * GLA_DV, GLA_RANK)
NSA_SIZES = (NSA_HEADS * HEAD_DIM, 6 * NSA_KV_HEADS * HEAD_DIM, 3 * NSA_HEADS)

CONV_W = 31
D_CONV = D_MODEL

N_EXPERTS = 64
N_GROUPS = 8
TOPK_GROUPS = 4
TOP_K = 8
D_EXPERT = 256
ROUTE_SCALE = 2.5
MOE_BLK = 128

ALPHA = (2 * DEPTH) ** 0.25
LN_EPS = 1e-5

LANE = 128
SUBLANES = 8
V7X_VMEM_BYTES = 64 * 1024 * 1024
VMEM_LIMIT = V7X_VMEM_BYTES * 3 // 4


def _dot(a, b):
    return jnp.dot(a, b, preferred_element_type=jnp.float32)


def _dot_nt(a, b):
    return lax.dot_general(a, b, (((1,), (1,)), ((), ())), preferred_element_type=jnp.float32)


def _mm_body(x_ref, w_ref, o_ref):
    o_ref[...] = _dot(x_ref[...].astype(jnp.bfloat16), w_ref[...].astype(jnp.bfloat16))


def _mm(x, w, keep_pad=False):
    m, k = x.shape
    n = w.shape[1]
    n_pad = -(-n // LANE) * LANE
    w = w.astype(jnp.bfloat16)
    if n_pad != n:
        w = jnp.pad(w, ((0, 0), (0, n_pad - n)))
    tm = min(m, 512)
    out = pl.pallas_call(
        _mm_body,
        grid=(m // tm,),
        in_specs=[pl.BlockSpec((tm, k), lambda i: (i, 0)),
                  pl.BlockSpec((k, n_pad), lambda i: (0, 0))],
        out_specs=pl.BlockSpec((tm, n_pad), lambda i: (i, 0)),
        out_shape=jax.ShapeDtypeStruct((m, n_pad), jnp.float32),
        compiler_params=pltpu.CompilerParams(dimension_semantics=("arbitrary",),
                                             vmem_limit_bytes=VMEM_LIMIT),
        name="mm",
    )(x, w)
    return out if keep_pad or n_pad == n else out[:, :n]


def _mm_pair_body(a_ref, b_ref, w_ref, o_ref):
    ka = a_ref.shape[1]
    o_ref[...] = (_dot(a_ref[...].astype(jnp.bfloat16), w_ref[0:ka, :])
                  + _dot(b_ref[...].astype(jnp.bfloat16), w_ref[ka:, :]))


def _mm_pair(a, b, w):
    m, ka = a.shape
    kb = b.shape[1]
    n = w.shape[1]
    tm = min(m, 512)
    return pl.pallas_call(
        _mm_pair_body,
        grid=(m // tm,),
        in_specs=[pl.BlockSpec((tm, ka), lambda i: (i, 0)), pl.BlockSpec((tm, kb), lambda i: (i, 0)),
                  pl.BlockSpec((ka + kb, n), lambda i: (0, 0))],
        out_specs=pl.BlockSpec((tm, n), lambda i: (i, 0)),
        out_shape=jax.ShapeDtypeStruct((m, n), jnp.float32),
        compiler_params=pltpu.CompilerParams(dimension_semantics=("arbitrary",)),
        name="mm_pair",
    )(a, b, w.astype(jnp.bfloat16))


def _partial_rope(x, pos):
    half = ROPE_DIM // 2
    inv_freq = jnp.power(ROPE_THETA, -jnp.arange(half, dtype=jnp.float32) / half)
    ang = pos.astype(jnp.float32)[:, None] * inv_freq
    ang = ang.reshape(ang.shape[0], *([1] * (x.ndim - 3)), half)
    cos, sin = jnp.cos(ang), jnp.sin(ang)
    x1 = x[..., :half]
    x2 = x[..., half:ROPE_DIM]
    rot = jnp.concatenate([x1 * cos - x2 * sin, x2 * cos + x1 * sin], -1)
    return jnp.concatenate([rot, x[..., ROPE_DIM:]], -1)


NSA_ROWS = NSA_GROUP * Q_BLK
SEL_KT = 1024
N_SELB = 128
MASKED = -1e9
WIN_KEYS = WINDOW + Q_BLK
KK_W = 2 * HEAD_DIM + N_SELB


def _nsa_prompt_body(qr_ref, qo_ref, kc_ref, vct_ref, kk_ref, vvt_ref, g_ref, o_ref,
                     imp_ref, m_ref, l_ref, acc_ref, oct_ref, selb_ref):
    f32, bf16 = jnp.float32, jnp.bfloat16
    qb = pl.program_id(2)
    q0 = qb * Q_BLK
    qr_t = qr_ref[0, 0, 0]
    qo_t = qo_ref[0, 0, 0]
    n_cmp = kc_ref.shape[2]

    ratio = SEL_BLK // CMP_STRIDE
    chunk = min(Q_BLK, n_cmp)
    n_chunks = n_cmp // chunk

    def compressed_and_select(n_act):
        nc = n_act * chunk
        nb = nc // ratio
        s_c = _dot(kc_ref[0, 0, 0:nc, :], qr_t)
        n_idx = lax.broadcasted_iota(jnp.int32, (nc, NSA_ROWS), 0)
        qpos_c = q0 + (lax.broadcasted_iota(jnp.int32, (nc, NSA_ROWS), 1) & (Q_BLK - 1))
        cmask = (n_idx * CMP_STRIDE + (CMP_BLK - 1)) <= qpos_c
        s_c = jnp.where(cmask, s_c, MASKED)
        m_c = jnp.max(s_c, axis=0, keepdims=True)
        p_c = jnp.where(cmask, jnp.exp(s_c - m_c), 0.0)
        p_c = p_c / jnp.maximum(jnp.sum(p_c, axis=0, keepdims=True), 1e-30)
        oct_ref[...] = _dot(vct_ref[0, 0, :, 0:nc], p_c.astype(bf16))
        imp = (p_c[:, 0:Q_BLK] + p_c[:, Q_BLK:2 * Q_BLK]) + p_c[:, 2 * Q_BLK:3 * Q_BLK] + p_c[:, 3 * Q_BLK:]
        imp_ref[0:8, :] = jnp.zeros((8, Q_BLK), f32)
        imp_ref[8:8 + nc, :] = imp
        imp_s = imp_ref[pl.ds(7, nb, stride=ratio), :]
        for r in range(ratio):
            imp_s = imp_s + imp_ref[pl.ds(8 + r, nb, stride=ratio), :]
        blk = lax.broadcasted_iota(jnp.int32, (nb, Q_BLK), 0)
        qpos_s = q0 + lax.broadcasted_iota(jnp.int32, (nb, Q_BLK), 1)
        cur = lax.shift_right_logical(qpos_s, int(math.log2(SEL_BLK)))
        valid = blk * SEL_BLK <= qpos_s
        forced = (blk == 0) | (blk == cur) | (blk == cur - 1)
        score = jnp.where(valid, imp_s + jnp.where(forced, FORCE_BONUS, 0.0), -1e30)
        picked = jnp.zeros((nb, Q_BLK), f32)
        for _ in range(SEL_TOPN):
            best = jnp.max(score, axis=0, keepdims=True)
            first = jnp.min(jnp.where(score == best, blk, nb), axis=0, keepdims=True)
            hit = blk == first
            picked = jnp.where(hit, 1.0, picked)
            score = jnp.where(hit, -3e38, score)
        sel = jnp.where(valid, picked, 0.0)
        if nb < N_SELB:
            sel = jnp.concatenate([sel, jnp.zeros((N_SELB - nb, Q_BLK), f32)], axis=0)
        sel = ((sel - 1.0) * (-MASKED)).astype(bf16)
        selb_ref[...] = jnp.concatenate([sel] * NSA_GROUP, axis=1)

    need = jnp.minimum((q0 + Q_BLK - CMP_BLK) // (CMP_STRIDE * chunk) + 1, n_chunks)
    for n_act in range(1, n_chunks + 1):
        pl.when(need == n_act)(functools.partial(compressed_and_select, n_act))
    o_ct = oct_ref[...]
    selb_t = selb_ref[...]

    zeros_q = jnp.zeros((HEAD_DIM, NSA_ROWS), bf16)
    q_sel = jnp.concatenate([qo_t, zeros_q, selb_t], axis=0)
    q_win = jnp.concatenate([zeros_q, qo_t, jnp.zeros((N_SELB, NSA_ROWS), bf16)], axis=0)
    qpos_r = q0 + (lax.broadcasted_iota(jnp.int32, (1, NSA_ROWS), 1) & (Q_BLK - 1))

    def v_tiles(first, count):
        return jnp.concatenate([vvt_ref[0, 0, first + j] for j in range(count)], axis=1)

    m_ref[...] = jnp.full(m_ref.shape, MASKED, f32)
    l_ref[...] = jnp.zeros(l_ref.shape, f32)
    acc_ref[...] = jnp.zeros(acc_ref.shape, f32)

    def sel_tile(k0, kt, causal):
        s = _dot(kk_ref[0, 0, pl.ds(k0, kt), :], q_sel)
        if causal:
            kpos = k0 + lax.broadcasted_iota(jnp.int32, (kt, NSA_ROWS), 0)
            s = jnp.where(kpos <= qpos_r, s, MASKED)
        m_old = m_ref[...]
        m_new = jnp.maximum(m_old, jnp.max(s, axis=0, keepdims=True))
        alpha = jnp.exp(m_old - m_new)
        p = jnp.exp(s - m_new)
        l_ref[...] = alpha * l_ref[...] + jnp.sum(p, axis=0, keepdims=True)
        vt = v_tiles(k0 // Q_BLK, kt // Q_BLK)
        acc_ref[...] = alpha * acc_ref[...] + _dot(vt, p.astype(bf16))
        m_ref[...] = m_new

    n_full = q0 // SEL_KT

    def full_step(t, c):
        sel_tile(pl.multiple_of(t * SEL_KT, SEL_KT), SEL_KT, False)
        return c

    lax.fori_loop(0, n_full, full_step, 0)
    d0 = pl.multiple_of(n_full * SEL_KT, SEL_KT)
    short = q0 + Q_BLK - n_full * SEL_KT <= SEL_KT // 2

    @pl.when(short)
    def _():
        sel_tile(d0, SEL_KT // 2, True)

    @pl.when(jnp.logical_not(short))
    def _():
        sel_tile(d0, SEL_KT, True)
    o_st = acc_ref[0:HEAD_DIM, :] / l_ref[...]

    w0 = pl.multiple_of(jnp.maximum(q0 - WINDOW, 0), Q_BLK)
    s_w = _dot(kk_ref[0, 0, pl.ds(w0, WIN_KEYS), :], q_win)
    kpos_w = w0 + lax.broadcasted_iota(jnp.int32, (WIN_KEYS, NSA_ROWS), 0)
    s_w = jnp.where((kpos_w <= qpos_r) & (kpos_w > qpos_r - WINDOW), s_w, MASKED)
    p_w = jnp.exp(s_w - jnp.max(s_w, axis=0, keepdims=True))
    l_w = jnp.sum(p_w, axis=0, keepdims=True)
    acc_w = _dot(v_tiles(w0 // Q_BLK, WIN_KEYS // Q_BLK), p_w.astype(bf16))
    o_wt = acc_w[HEAD_DIM:2 * HEAD_DIM, :] / l_w

    g = g_ref[0, 0, 0]
    out_t = g[0:1, :] * o_ct + g[1:2, :] * o_st + g[2:3, :] * o_wt
    o_ref[0] = jnp.concatenate([out_t[:, g_ * Q_BLK:(g_ + 1) * Q_BLK] for g_ in range(NSA_GROUP)], axis=0).T


def _nsa_prompt(qr, qo, gt, kc_p, vct, kk, vvt):
    bsz, _, nqb = qr.shape[:3]
    t_ = nqb * Q_BLK
    n_cmp = kc_p.shape[2]
    per_blk = lambda b, h, i: (b, h, i, 0, 0)
    per_head = lambda b, h, i: (b, h, 0, 0)
    return pl.pallas_call(
        _nsa_prompt_body,
        grid=(bsz, NSA_KV_HEADS, nqb),
        in_specs=[pl.BlockSpec((1, 1, 1, HEAD_DIM, NSA_ROWS), per_blk),
                  pl.BlockSpec((1, 1, 1, HEAD_DIM, NSA_ROWS), per_blk),
                  pl.BlockSpec((1, 1, n_cmp, HEAD_DIM), per_head),
                  pl.BlockSpec((1, 1, HEAD_DIM, n_cmp), per_head),
                  pl.BlockSpec((1, 1, t_, KK_W), per_head),
                  pl.BlockSpec((1, 1, nqb, 2 * HEAD_DIM, Q_BLK), lambda b, h, i: (b, h, 0, 0, 0)),
                  pl.BlockSpec((1, 1, 1, 3, NSA_ROWS), per_blk)],
        out_specs=pl.BlockSpec((1, Q_BLK, NSA_GROUP * HEAD_DIM), lambda b, h, i: (b, i, h)),
        out_shape=jax.ShapeDtypeStruct((bsz, t_, NSA_HEADS * HEAD_DIM), jnp.float32),
        scratch_shapes=[pltpu.VMEM((8 + n_cmp, Q_BLK), jnp.float32),
                        pltpu.VMEM((1, NSA_ROWS), jnp.float32),
                        pltpu.VMEM((1, NSA_ROWS), jnp.float32),
                        pltpu.VMEM((2 * HEAD_DIM, NSA_ROWS), jnp.float32),
                        pltpu.VMEM((HEAD_DIM, NSA_ROWS), jnp.float32),
                        pltpu.VMEM((N_SELB, NSA_ROWS), jnp.bfloat16)],
        compiler_params=pltpu.CompilerParams(
            dimension_semantics=("arbitrary", "arbitrary", "arbitrary"),
            vmem_limit_bytes=VMEM_LIMIT),
        name="nsa_prompt",
    )(qr, qo, kc_p, vct, kk, vvt, gt)


GLA_SUB = 16
GLA_UNROLL = 8
GLA_TILE = 256
GLA_QK = GLA_HEADS * GLA_DK
GLA_V = GLA_HEADS * GLA_DV


def _dot_tn(a, b):
    return lax.dot_general(a, b, (((0,), (0,)), ((), ())), preferred_element_type=jnp.float32)


def _gla_body(q_ref, k_ref, v_ref, gr_ref, glr_ref, wg_ref, bg_ref, ng_ref, s0_ref, exp_ref,
              o_ref, sfin_ref, st_ref, b_ref, qd_ref, *, t_valid):
    f32, bf16 = jnp.float32, jnp.bfloat16
    tt = q_ref.shape[1]
    ti = pl.program_id(1)

    @pl.when(ti == 0)
    def _():
        st_ref[...] = s0_ref[0]

    row = lax.broadcasted_iota(jnp.int32, (tt, 1), 0)
    z = _dot(glr_ref[0][:, :GLA_RANK].astype(bf16), wg_ref[...]) + bg_ref[...]
    la = (jnp.minimum(z, 0.0) - jnp.log1p(jnp.exp(-jnp.abs(z)))) * (1.0 / GLA_TAU)
    la = jnp.where(ti * tt + row < t_valid, la, 0.0)
    seg = row & (GLA_SUB - 1)
    b = la
    for s in (1, 2, 4, 8):
        b = b + jnp.where(seg >= s, pltpu.roll(b, s, axis=0), 0.0)
    q = q_ref[0] * (GLA_DK ** -0.5)
    k = k_ref[0]
    v = v_ref[0]
    o = _dot((q * k).astype(bf16), exp_ref[...]) * v
    for d in range(1, GLA_SUB):
        decay = jnp.exp(jnp.minimum(b - pltpu.roll(b, d, axis=0), 0.0))
        w = jnp.where(seg >= d, q * pltpu.roll(k, d, axis=0) * decay, 0.0)
        o = o + _dot(w.astype(bf16), exp_ref[...]) * pltpu.roll(v, d, axis=0)
    o_ref[0] = o
    b_ref[...] = b
    qd_ref[...] = (q * jnp.exp(b)).astype(bf16)

    def block_step(c, carry):
        rows = pl.ds(pl.multiple_of(c * GLA_SUB, GLA_SUB), GLA_SUB)
        qd = qd_ref[rows, :]
        bc = b_ref[rows, :]
        bl = bc[GLA_SUB - 1:GLA_SUB, :]
        kc = (k_ref[0, rows, :] * jnp.exp(bl - bc)).astype(bf16)
        keep = jnp.exp(bl)
        vb = v_ref[0, rows, :].astype(bf16)
        outs = []
        for h in range(GLA_HEADS):
            dk = slice(h * GLA_DK, (h + 1) * GLA_DK)
            dv = slice(h * GLA_DV, (h + 1) * GLA_DV)
            st = st_ref[dv, :]
            outs.append(_dot_nt(qd[:, dk], st.astype(bf16)))
            st_ref[dv, :] = st * keep[:, dk] + _dot_tn(vb[:, dv], kc[:, dk])
        o_ref[0, rows, :] += jnp.concatenate(outs, axis=1)
        return carry

    lax.fori_loop(0, tt // GLA_SUB, block_step, 0, unroll=GLA_UNROLL)
    sfin_ref[0] = st_ref[...]
    gr = gr_ref[0]
    gate = gr * jax.nn.sigmoid(gr)
    for h in range(GLA_HEADS):
        cols = slice(h * GLA_DV, (h + 1) * GLA_DV)
        oh = o_ref[0, :, cols]
        ms = jnp.mean(oh * oh, axis=-1, keepdims=True)
        o_ref[0, :, cols] = oh * lax.rsqrt(ms + LN_EPS) * ng_ref[...] * gate[:, cols]


def _gla(h, w_gla_gate, b_gla_gate, gla_norm_g, gla_state):
    bsz, t_, n_in = h.shape
    tp = -(-t_ // GLA_SUB) * GLA_SUB
    if tp != t_:
        h = jnp.pad(h, ((0, 0), (0, tp - t_), (0, 0)))
    tt = min(tp, GLA_TILE)
    expand =np.repeat(np.repeat(np.eye(GLA_HEADS, dtype=np.float32), GLA_DK, 0), GLA_DV, 1)
    if gla_state is None:
        s0 = jnp.zeros((bsz, GLA_V, GLA_DK), jnp.float32)
    else:
        s0 = gla_state.transpose(0, 1, 3, 2).reshape(bsz, GLA_V, GLA_DK)
    tile = lambda width, blk: pl.BlockSpec((1, tt, width), lambda b, i: (b, i, blk))
    fixed2 = lambda shape: pl.BlockSpec(shape, lambda b, i: (0, 0))
    per_b = pl.BlockSpec((1, GLA_V, GLA_DK), lambda b, i: (b, 0, 0))
    o, s_t = pl.pallas_call(
        functools.partial(_gla_body, t_valid=t_),
        grid=(bsz, tp // tt),
        in_specs=[tile(GLA_QK, 0), tile(GLA_QK, 1), tile(GLA_V, 1), tile(GLA_V, 2),
                  tile(LANE, (2 * GLA_QK + 2 * GLA_V + NSA_SIZES[0] + NSA_SIZES[1]) // LANE),
                  fixed2((GLA_RANK, GLA_QK)), fixed2((1, GLA_QK)), fixed2((1, GLA_DV)), per_b,
                  fixed2((GLA_QK, GLA_V))],
        out_specs=[pl.BlockSpec((1, tt, GLA_V), lambda b, i: (b, i, 0)), per_b],
        out_shape=[jax.ShapeDtypeStruct((bsz, tp, GLA_V), jnp.float32),
                   jax.ShapeDtypeStruct((bsz, GLA_V, GLA_DK), jnp.float32)],
        scratch_shapes=[pltpu.VMEM((GLA_V, GLA_DK), jnp.float32), pltpu.VMEM((tt, GLA_QK), jnp.float32),
                        pltpu.VMEM((tt, GLA_QK), jnp.bfloat16)],
        compiler_params=pltpu.CompilerParams(dimension_semantics=("arbitrary", "arbitrary"),
                                             vmem_limit_bytes=VMEM_LIMIT),
        name="gla",
    )(h, h, h, h, h, w_gla_gate.astype(jnp.bfloat16), b_gla_gate.reshape(1, GLA_QK),
      gla_norm_g.reshape(1, GLA_DV), s0, jnp.asarray(expand, jnp.bfloat16))
    return o[:, :t_], s_t.reshape(bsz, GLA_HEADS, GLA_DV, GLA_DK).transpose(0, 1, 3, 2)


COL_NQ = 2 * GLA_QK + 2 * GLA_V
COL_NKV = COL_NQ + NSA_SIZES[0]
COL_TAIL = COL_NKV + NSA_SIZES[1]
TAIL_GATE = GLA_RANK
_ORIG = np.cumsum((0,) + GLA_SIZES + NSA_SIZES)
IN_AB_PERM = np.concatenate([np.arange(_ORIG[0], _ORIG[4]), np.arange(_ORIG[5], _ORIG[7]),
                             np.arange(_ORIG[4], _ORIG[5]), np.arange(_ORIG[7], _ORIG[8])])
SUBS = Q_BLK // CMP_STRIDE


def _nsa_prep_body(nq_ref, kv0_ref, kv1_ref, kv2_ref, tail_ref, rc_ref, ru_ref, rd_ref, pool_ref,
                   rows_ref, win_ref, kk_ref, vvt_ref, qr_ref, qo_ref, g_ref, pooled_ref):
    bf16 = jnp.bfloat16
    q0 = pl.program_id(1) * Q_BLK
    kv_w = NSA_KV_HEADS * HEAD_DIM

    def rope(x):
        reps = x.shape[1] // LANE
        wide = lambda r: jnp.concatenate([r[...]] * reps, axis=1) if reps > 1 else r[...]
        half = ROPE_DIM // 2
        return (x * wide(rc_ref) + pltpu.roll(x, half, axis=1) * wide(ru_ref)
                + pltpu.roll(x, x.shape[1] - half, axis=1) * wide(rd_ref))

    kv0, kv1, kv2 = kv0_ref[0], kv1_ref[0], kv2_ref[0]
    k_sel, v_sel = rope(kv1[:, :kv_w]), kv1[:, kv_w:]
    k_win, v_win = rope(kv2[:, :kv_w]), kv2[:, kv_w:]
    rows_ref[0] = jnp.concatenate([kv0, k_sel, v_sel], axis=1)
    win_ref[0] = jnp.concatenate([k_win, v_win], axis=1)
    blk_id = lax.shift_right_logical(q0 + lax.broadcasted_iota(jnp.int32, (Q_BLK, N_SELB), 0),
                                     int(math.log2(SEL_BLK)))
    onehot = jnp.where(lax.broadcasted_iota(jnp.int32, (Q_BLK, N_SELB), 1) == blk_id, 1.0, 0.0).astype(bf16)
    q = nq_ref[0] * (HEAD_DIM ** -0.5)
    q_rot = rope(q)
    gates_t = jax.nn.sigmoid(tail_ref[0]).T
    for h in range(NSA_KV_HEADS):
        hs = slice(h * HEAD_DIM, (h + 1) * HEAD_DIM)
        kk_ref[0, h] = jnp.concatenate([k_sel[:, hs].astype(bf16), k_win[:, hs].astype(bf16), onehot], axis=1)
        vvt_ref[0, h, 0] = jnp.concatenate([v_sel[:, hs], v_win[:, hs]], axis=1).T.astype(bf16)
        gw = NSA_GROUP * HEAD_DIM
        for src, dst in ((q, qr_ref), (q_rot, qo_ref)):
            t = src[:, h * gw:(h + 1) * gw].T
            dst[0, h, 0] = jnp.concatenate([t[g * HEAD_DIM:(g + 1) * HEAD_DIM] for g in range(NSA_GROUP)],
                                           axis=1).astype(bf16)
        base = TAIL_GATE + h * NSA_GROUP * 3
        g_ref[0, h, 0] = jnp.concatenate(
            [jnp.concatenate([gates_t[base + 3 * g + c:base + 3 * g + c + 1] for g in range(NSA_GROUP)], axis=1)
             for c in range(3)], axis=0)
    kc_in, vc_in = kv0[:, :kv_w].astype(bf16), kv0[:, kv_w:].astype(bf16)
    pooled_ref[0] = jnp.concatenate([_dot(pool_ref[0], kc_in), _dot(pool_ref[1], kc_in),
                                     _dot(pool_ref[2], vc_in), _dot(pool_ref[3], vc_in)], axis=1)


def _nsa_prep(h, pos, w_cmp_pool):
    bsz, t_, _ = h.shape
    nqb = t_ // Q_BLK
    bf16 = jnp.bfloat16
    half = ROPE_DIM // 2
    inv_freq = jnp.power(ROPE_THETA, -jnp.arange(half, dtype=jnp.float32) / half)
    ang = pos.astype(jnp.float32)[:, None] * inv_freq
    cos, sin = jnp.cos(ang), jnp.sin(ang)
    rest = HEAD_DIM - ROPE_DIM
    z8, zr = jnp.zeros((t_, half), jnp.float32), jnp.zeros((t_, rest), jnp.float32)
    two = lambda a: jnp.concatenate([a, a], axis=1)
    rc = two(jnp.concatenate([cos, cos, jnp.ones((t_, rest), jnp.float32)], axis=1))
    ru = two(jnp.concatenate([z8, sin, zr], axis=1))
    rd = two(jnp.concatenate([-sin, z8, zr], axis=1))
    pool = _pool_matrices(w_cmp_pool)
    kv_w = NSA_KV_HEADS * HEAD_DIM
    col = lambda width, off: pl.BlockSpec((1, Q_BLK, width), lambda b, i: (b, i, off // width))
    rows_t = pl.BlockSpec((Q_BLK, LANE), lambda b, i: (i, 0))
    head4 = lambda r, c: pl.BlockSpec((1, NSA_KV_HEADS, 1, r, c), lambda b, i: (b, 0, i, 0, 0))
    return pl.pallas_call(
        _nsa_prep_body,
        grid=(bsz, nqb),
        in_specs=[col(NSA_SIZES[0], COL_NQ), col(2 * kv_w, COL_NKV), col(2 * kv_w, COL_NKV + 2 * kv_w),
                  col(2 * kv_w, COL_NKV + 4 * kv_w), col(LANE, COL_TAIL), rows_t, rows_t, rows_t,
                  pl.BlockSpec((4, SUBS, Q_BLK), lambda b, i: (0, 0, 0))],
        out_specs=[pl.BlockSpec((1, Q_BLK, 4 * kv_w), lambda b, i: (b, i, 0)),
                   pl.BlockSpec((1, Q_BLK, 2 * kv_w), lambda b, i: (b, i, 0)),
                   pl.BlockSpec((1, NSA_KV_HEADS, Q_BLK, KK_W), lambda b, i: (b, 0, i, 0)),
                   head4(2 * HEAD_DIM, Q_BLK), head4(HEAD_DIM, NSA_ROWS), head4(HEAD_DIM, NSA_ROWS),
                   head4(3, NSA_ROWS),
                   pl.BlockSpec((1, SUBS, 4 * kv_w), lambda b, i: (b, i, 0))],
        out_shape=[jax.ShapeDtypeStruct((bsz, t_, 4 * kv_w), jnp.float32),
                   jax.ShapeDtypeStruct((bsz, t_, 2 * kv_w), jnp.float32),
                   jax.ShapeDtypeStruct((bsz, NSA_KV_HEADS, t_, KK_W), bf16),
                   jax.ShapeDtypeStruct((bsz, NSA_KV_HEADS, nqb, 2 * HEAD_DIM, Q_BLK), bf16),
                   jax.ShapeDtypeStruct((bsz, NSA_KV_HEADS, nqb, HEAD_DIM, NSA_ROWS), bf16),
                   jax.ShapeDtypeStruct((bsz, NSA_KV_HEADS, nqb, HEAD_DIM, NSA_ROWS), bf16),
                   jax.ShapeDtypeStruct((bsz, NSA_KV_HEADS, nqb, 3, NSA_ROWS), jnp.float32),
                   jax.ShapeDtypeStruct((bsz, t_ // CMP_STRIDE, 4 * kv_w), jnp.float32)],
        compiler_params=pltpu.CompilerParams(dimension_semantics=("arbitrary", "arbitrary")),
        name="nsa_prep",
    )(h, h, h, h, h, rc, ru, rd, pool)


PAGE_GROUP = 32
DEC_KEYS = PAGE_GROUP * PAGE_SIZE
POOL_ROWS = 2048
NEW_PAD = 8
KV_W = NSA_KV_HEADS * HEAD_DIM


def _dec_pool_body(pt_ref, *refs):
    page_refs, pool_ref, out_ref = refs[:PAGE_GROUP], refs[PAGE_GROUP], refs[PAGE_GROUP + 1]
    bf16 = jnp.bfloat16
    pages = [pr[0] for pr in page_refs]
    per = POOL_ROWS // PAGE_SIZE
    cols = []
    for g0 in range(0, PAGE_GROUP, per):
        kc_t = jnp.concatenate([p[:KV_W] for p in pages[g0:g0 + per]], axis=1).astype(bf16)
        vc_t = jnp.concatenate([p[KV_W:] for p in pages[g0:g0 + per]], axis=1).astype(bf16)
        cols.append(jnp.concatenate([_dot(kc_t, pool_ref[0]), _dot(kc_t, pool_ref[1]),
                                     _dot(vc_t, pool_ref[2]), _dot(vc_t, pool_ref[3])], axis=0))
    out_ref[0] = jnp.concatenate(cols, axis=1)


def _page_specs(n_pages, col_blk):
    def spec(i):
        return pl.BlockSpec((1, 2 * KV_W, PAGE_SIZE),
                            lambda b, j, pt: (pt[b * n_pages + j * PAGE_GROUP + i], col_blk, 0))
    return [spec(i) for i in range(PAGE_GROUP)]


def _dec_pool(cache, page_table, pool):
    bsz, n_pages = page_table.shape
    grid_spec = pltpu.PrefetchScalarGridSpec(
        num_scalar_prefetch=1, grid=(bsz, n_pages // PAGE_GROUP),
        in_specs=_page_specs(n_pages, 0) + [pl.BlockSpec(pool.shape, lambda b, j, pt: (0, 0, 0))],
        out_specs=pl.BlockSpec((1, 4 * KV_W, PAGE_GROUP * SUBS), lambda b, j, pt: (b, 0, j)))
    return pl.pallas_call(
        _dec_pool_body, grid_spec=grid_spec,
        out_shape=jax.ShapeDtypeStruct((bsz, 4 * KV_W, n_pages * SUBS), jnp.float32),
        compiler_params=pltpu.CompilerParams(dimension_semantics=("arbitrary", "arbitrary")),
        name="nsa_dec_pool",
    )(page_table.reshape(-1), *([cache] * PAGE_GROUP), pool)


def _dec_select_body(qr_ref, kct_ref, vc_ref, band_ref, oc_ref, selb_ref, *, qpos0, n_q, n_pick, n_blk):
    f32, bf16 = jnp.float32, jnp.bfloat16
    n_cmp = kct_ref.shape[3]
    rows = NSA_GROUP * n_q
    for sq, h in [(a, b) for a in range(qr_ref.shape[0]) for b in range(NSA_KV_HEADS)]:
        s_c = _dot(qr_ref[sq, h], kct_ref[sq, h])
        n_idx = lax.broadcasted_iota(jnp.int32, (rows, n_cmp), 1)
        qpos = qpos0 + (lax.broadcasted_iota(jnp.int32, (rows, n_cmp), 0) % n_q)
        cmask = (n_idx * CMP_STRIDE + (CMP_BLK - 1)) <= qpos
        s_c = jnp.where(cmask, s_c, MASKED)
        p_c = jnp.where(cmask, jnp.exp(s_c - jnp.max(s_c, axis=1, keepdims=True)), 0.0)
        p_c = p_c / jnp.maximum(jnp.sum(p_c, axis=1, keepdims=True), 1e-30)
        oc_ref[sq, h] = _dot(p_c.astype(bf16), vc_ref[sq, h])
        imp = p_c[0:n_q]
        for g in range(1, NSA_GROUP):
            imp = imp + p_c[g * n_q:(g + 1) * n_q]
        imp_s = jnp.zeros((n_q, N_SELB), f32)
        rem = imp
        for _ in range(3):
            part = rem.astype(bf16)
            imp_s = imp_s + _dot(part, band_ref[...])
            rem = rem - part.astype(f32)
        blk = lax.broadcasted_iota(jnp.int32, (n_q, N_SELB), 1)
        qpos_s = qpos0 + lax.broadcasted_iota(jnp.int32, (n_q, N_SELB), 0)
        cur = lax.shift_right_logical(qpos_s, int(math.log2(SEL_BLK)))
        valid = (blk * SEL_BLK <= qpos_s) & (blk < n_blk)
        forced = (blk == 0) | (blk == cur) | (blk == cur - 1)
        score = jnp.where(valid, imp_s + jnp.where(forced, FORCE_BONUS, 0.0), -1e30)
        picked = jnp.zeros((n_q, N_SELB), f32)
        for _ in range(n_pick):
            best = jnp.max(score, axis=1, keepdims=True)
            first = jnp.min(jnp.where(score == best, blk, N_SELB), axis=1, keepdims=True)
            hit = blk == first
            picked = jnp.where(hit, 1.0, picked)
            score = jnp.where(hit, -3e38, score)
        selb_ref[sq, h] = (jnp.where(valid, picked, 0.0) - 1.0) * (-MASKED)


def _dec_select(qr, kct, vc, n_q, qpos0, n_pick, n_blk):
    bsz = qr.shape[0]
    rows = NSA_GROUP * n_q
    n_cmp = kct.shape[3]
    ratio = SEL_BLK // CMP_STRIDE
    c_idx, j_idx = np.arange(n_cmp)[:, None], np.arange(N_SELB)[None, :]
    band = jnp.asarray(((c_idx >= ratio * j_idx - 1) & (c_idx <= ratio * j_idx + ratio - 1)), jnp.bfloat16)
    per_step = next(c for c in (4, 2, 1) if bsz % c == 0)
    per_b = lambda *tail: pl.BlockSpec((per_step, NSA_KV_HEADS) + tail, lambda b: (b, 0, 0, 0))
    return pl.pallas_call(
        functools.partial(_dec_select_body, qpos0=qpos0, n_q=n_q, n_pick=n_pick, n_blk=n_blk),
        grid=(bsz // per_step,),
        in_specs=[per_b(rows, HEAD_DIM), per_b(HEAD_DIM, n_cmp), per_b(n_cmp, HEAD_DIM),
                  pl.BlockSpec((n_cmp, N_SELB), lambda b: (0, 0))],
        out_specs=[per_b(rows, HEAD_DIM), per_b(n_q, N_SELB)],
        out_shape=[jax.ShapeDtypeStruct((bsz, NSA_KV_HEADS, rows, HEAD_DIM), jnp.float32),
                   jax.ShapeDtypeStruct((bsz, NSA_KV_HEADS, n_q, N_SELB), jnp.float32)],
        compiler_params=pltpu.CompilerParams(dimension_semantics=("arbitrary",)),
        name="nsa_dec_select",
    )(qr, kct, vc, band)


def _dec_attend_body(pt_ref, *refs, qpos0, n_q, past):
    page_refs = refs[:PAGE_GROUP]
    (qs_ref, qw_ref, knew_ref, vnew_ref, wbuf_ref, wnew_ref, oc_ref, g_ref,
     o_ref, m_ref, l_ref, acc_ref) = refs[PAGE_GROUP:]
    f32, bf16 = jnp.float32, jnp.bfloat16
    j = pl.program_id(1)
    n_rows = qs_ref.shape[1]

    @pl.when(j == 0)
    def _():
        m_ref[...] = jnp.full(m_ref.shape, MASKED, f32)
        l_ref[...] = jnp.zeros(l_ref.shape, f32)
        acc_ref[...] = jnp.zeros(acc_ref.shape, f32)

    def online(s, weigh):
        m_old = m_ref[...]
        m_new = jnp.maximum(m_old, jnp.max(s, axis=1, keepdims=True))
        alpha = jnp.exp(m_old - m_new)
        p = jnp.exp(s - m_new)
        l_ref[...] = alpha * l_ref[...] + jnp.sum(p, axis=1, keepdims=True)
        acc_ref[...] = alpha * acc_ref[...] + weigh(p.astype(bf16))
        m_ref[...] = m_new

    qs = qs_ref[0]
    pages = [pr[0] for pr in page_refs]
    keys_t = jnp.concatenate([p[:KV_W] for p in pages], axis=1).astype(bf16)
    vals_t = jnp.concatenate([p[KV_W:] for p in pages], axis=1).astype(bf16)
    blk_id = j * (DEC_KEYS // SEL_BLK) + lax.shift_right_logical(
        lax.broadcasted_iota(jnp.int32, (N_SELB, DEC_KEYS), 1), int(math.log2(SEL_BLK)))
    onehot_t = jnp.where(lax.broadcasted_iota(jnp.int32, (N_SELB, DEC_KEYS), 0) == blk_id, 1.0, 0.0).astype(bf16)
    online(_dot(qs, jnp.concatenate([keys_t, onehot_t], axis=0)), lambda p: _dot_nt(p, vals_t))

    @pl.when(j == pl.num_programs(1) - 1)
    def _():
        row_q = qpos0 + (lax.broadcasted_iota(jnp.int32, (n_rows, 1), 0) % n_q)
        qh = qw_ref[0]
        new_pos = past + lax.broadcasted_iota(jnp.int32, (n_rows, NEW_PAD), 1)
        new_ok = (new_pos <= row_q) & (new_pos < past + n_q)
        s_new = jnp.where(new_ok, _dot_nt(qh, knew_ref[0]), MASKED)
        online(s_new, lambda p: _dot(p, vnew_ref[0]))
        o_s = acc_ref[...] / l_ref[...]
        wbuf_t = wbuf_ref[0]
        wnew = wnew_ref[0]
        n_buf = wbuf_t.shape[1]
        s_b = _dot(qh, wbuf_t[:KV_W].astype(bf16))
        pos_b = (past - n_buf) + lax.broadcasted_iota(jnp.int32, (n_rows, n_buf), 1)
        s_b = jnp.where((pos_b > row_q - WINDOW) & (pos_b >= 0), s_b, MASKED)
        s_n = jnp.where(new_ok, _dot_nt(qh, wnew[:, :KV_W].astype(bf16)), MASKED)
        m_w = jnp.maximum(jnp.max(s_b, axis=1, keepdims=True), jnp.max(s_n, axis=1, keepdims=True))
        p_b, p_n = jnp.exp(s_b - m_w), jnp.exp(s_n - m_w)
        l_w = jnp.sum(p_b, axis=1, keepdims=True) + jnp.sum(p_n, axis=1, keepdims=True)
        o_w = (_dot_nt(p_b.astype(bf16), wbuf_t[KV_W:].astype(bf16))
               + _dot(p_n.astype(bf16), wnew[:, KV_W:].astype(bf16))) / l_w
        half = n_rows // NSA_KV_HEADS
        own = lambda a: jnp.concatenate([a[h * half:(h + 1) * half, h * HEAD_DIM:(h + 1) * HEAD_DIM]
                                         for h in range(NSA_KV_HEADS)], axis=0)
        g = g_ref[0]
        o_ref[0] = g[:, 0:1] * oc_ref[0] + g[:, 1:2] * own(o_s) + g[:, 2:3] * own(o_w)


def _dec_attend(cache, page_table, qs, qw, knew, vnew, wbuf, wnew, o_c, gates, n_q, qpos0):
    bsz, n_pages = page_table.shape
    n_rows = qs.shape[1]
    per_b = lambda *tail: pl.BlockSpec((1,) + tail, lambda b, j, pt: (b, 0, 0))
    grid_spec = pltpu.PrefetchScalarGridSpec(
        num_scalar_prefetch=1, grid=(bsz, n_pages // PAGE_GROUP),
        in_specs=_page_specs(n_pages, 1) + [
            per_b(n_rows, KV_W + N_SELB), per_b(n_rows, KV_W), per_b(NEW_PAD, KV_W), per_b(NEW_PAD, KV_W),
            per_b(2 * KV_W, wbuf.shape[2]), per_b(NEW_PAD, 2 * KV_W), per_b(n_rows, HEAD_DIM), per_b(n_rows, 3)],
        out_specs=per_b(n_rows, HEAD_DIM),
        scratch_shapes=[pltpu.VMEM((n_rows, 1), jnp.float32), pltpu.VMEM((n_rows, 1), jnp.float32),
                        pltpu.VMEM((n_rows, KV_W), jnp.float32)])
    return pl.pallas_call(
        functools.partial(_dec_attend_body, qpos0=qpos0, n_q=n_q, past=n_pages * PAGE_SIZE),
        grid_spec=grid_spec,
        out_shape=jax.ShapeDtypeStruct((bsz, n_rows, HEAD_DIM), jnp.float32),
        compiler_params=pltpu.CompilerParams(dimension_semantics=("arbitrary", "arbitrary")),
        name="nsa_dec_attend",
    )(page_table.reshape(-1), *([cache] * PAGE_GROUP), qs, qw, knew, vnew, wbuf, wnew, o_c, gates)


def _pool_matrices(w_cmp_pool, rows=Q_BLK):
    subs = rows // CMP_STRIDE
    sub = np.arange(rows) // CMP_STRIDE == np.arange(subs)[:, None]
    w_rep = jnp.tile(w_cmp_pool.reshape(2, 2, CMP_STRIDE), (1, 1, subs))
    return jnp.where(sub[None, None], w_rep[:, :, None, :], 0.0).reshape(4, subs, rows).astype(jnp.bfloat16)


def _nsa_decode(q_raw, q_rot, gates, rows_full, rows_win, cache, page_table, win_buf, w_cmp_pool, past):
    bsz, n_q = q_raw.shape[:2]
    bf16 = jnp.bfloat16
    n_blk = past // SEL_BLK
    assert past % DEC_KEYS == 0 and n_blk <= N_SELB and n_q <= NEW_PAD
    scale = HEAD_DIM ** -0.5
    cache2 = cache.transpose(0, 2, 3, 4, 1).reshape(cache.shape[0], 4 * KV_W, PAGE_SIZE)
    pooled_t = _dec_pool(cache2, page_table, _pool_matrices(w_cmp_pool, POOL_ROWS).transpose(0, 2, 1))
    pooled_t = pooled_t.reshape(bsz, 4, NSA_KV_HEADS, HEAD_DIM, -1)
    last = ((0, 0), (0, 0), (0, 0), (0, 1))
    kct = jnp.pad(pooled_t[:, 0, ..., :-1] + pooled_t[:, 1, ..., 1:], last)
    vc_p = jnp.pad(pooled_t[:, 2, ..., :-1] + pooled_t[:, 3, ..., 1:], last).transpose(0, 1, 3, 2)
    rows_of = lambda a: a.transpose(0, 2, 3, 1, 4).reshape(bsz, NSA_KV_HEADS, NSA_GROUP * n_q, a.shape[-1])
    qr = rows_of((q_raw * scale).astype(bf16))
    n_pick = min(SEL_TOPN, n_blk + 1) - 1
    o_c, selb = _dec_select(qr, kct.astype(bf16), vc_p.astype(bf16), n_q, past, n_pick, n_blk)
    qo = rows_of((q_rot * scale).astype(bf16))
    zero = jnp.zeros_like(qo[:, 0])
    qw = jnp.concatenate([jnp.concatenate([qo[:, 0], zero], -1), jnp.concatenate([zero, qo[:, 1]], -1)], axis=1)
    bias = jnp.tile(selb, (1, 1, NSA_GROUP, 1)).reshape(bsz, -1, N_SELB).astype(bf16)
    qs = jnp.concatenate([qw, bias], axis=-1)
    pad_new = lambda a: jnp.pad(a.reshape(bsz, n_q, -1), ((0, 0), (0, NEW_PAD - n_q), (0, 0)))
    knew = pad_new(rows_full[:, :, 2]).astype(bf16)
    vnew = pad_new(rows_full[:, :, 3]).astype(bf16)
    wnew = pad_new(rows_win)
    wbuf = win_buf.transpose(0, 2, 3, 4, 1).reshape(bsz, 2 * KV_W, win_buf.shape[1])
    gt = rows_of(gates).reshape(bsz, -1, 3)
    o = _dec_attend(cache2, page_table, qs, qw, knew, vnew, wbuf, wnew,
                    o_c.reshape(bsz, -1, HEAD_DIM), gt, n_q, past)
    o = o.reshape(bsz, NSA_KV_HEADS, NSA_GROUP, n_q, HEAD_DIM).transpose(0, 3, 1, 2, 4)
    return o.reshape(bsz, n_q, NSA_HEADS * HEAD_DIM)


def _ab_mixer(x, pos, w_in, w_gla_gate, b_gla_gate, gla_norm_g, w_cmp_pool, w_out,
              gla_state, nsa_cache, page_table, win_buf):
    bsz, t_, _ = x.shape
    h_in = _mm(x.reshape(bsz * t_, -1), w_in[:, IN_AB_PERM], keep_pad=True).reshape(bsz, t_, -1)
    o_a, s_a = _gla(h_in, w_gla_gate, b_gla_gate, gla_norm_g, gla_state)
    kv_w = NSA_KV_HEADS * HEAD_DIM
    if nsa_cache is None:
        rows2, win2, kk, vvt, qr, qo, gt, pooled = _nsa_prep(h_in, pos, w_cmp_pool)
        pooled = pooled.reshape(bsz, t_ // CMP_STRIDE, 4, NSA_KV_HEADS, HEAD_DIM)
        kc = pooled[:, :-1, 0] + pooled[:, 1:, 1]
        vc = pooled[:, :-1, 2] + pooled[:, 1:, 3]
        kc_p = jnp.pad(kc, ((0, 0), (0, 1), (0, 0), (0, 0))).transpose(0, 2, 1, 3).astype(jnp.bfloat16)
        vct = jnp.pad(vc, ((0, 0), (0, 1), (0, 0), (0, 0))).transpose(0, 2, 3, 1).astype(jnp.bfloat16)
        o_b = _nsa_prompt(qr, qo, gt, kc_p, vct, kk, vvt)
        rows_full = rows2.reshape(bsz, t_, 4, NSA_KV_HEADS, HEAD_DIM)
        new_win = win2[:, -min(WINDOW, t_):].reshape(bsz, -1, 2, NSA_KV_HEADS, HEAD_DIM)
    else:
        nq = h_in[..., COL_NQ:COL_NKV]
        nkv = h_in[..., COL_NKV:COL_TAIL]
        ngate = h_in[..., COL_TAIL + TAIL_GATE:COL_TAIL + TAIL_GATE + NSA_SIZES[2]]
        q_raw = nq.reshape(bsz, t_, NSA_KV_HEADS, NSA_GROUP, HEAD_DIM)
        q_rot = _partial_rope(q_raw, pos)
        kv = nkv.reshape(bsz, t_, 6, NSA_KV_HEADS, HEAD_DIM)
        k_sel = _partial_rope(kv[:, :, 2], pos)
        k_win = _partial_rope(kv[:, :, 4], pos)
        rows_full = jnp.stack([kv[:, :, 0], kv[:, :, 1], k_sel, kv[:, :, 3]], axis=2)
        rows_win = jnp.stack([k_win, kv[:, :, 5]], axis=2)
        gates = jax.nn.sigmoid(ngate).reshape(bsz, t_, NSA_KV_HEADS, NSA_GROUP, 3)
        past_len = page_table.shape[1] * PAGE_SIZE
        o_b = _nsa_decode(q_raw, q_rot, gates, rows_full, rows_win, nsa_cache, page_table, win_buf,
                          w_cmp_pool, past_len)
        w_buf = win_buf.shape[1]
        kw = jnp.concatenate([win_buf, rows_win], axis=1)
        new_win = kw[:, -w_buf:]
    y = _mm_pair(o_a.reshape(bsz * t_, -1), o_b.reshape(bsz * t_, -1), w_out).reshape(bsz, t_, -1)
    return y, s_a, rows_full, new_win


CONV_HALO = 32
CONV_LEAD = CONV_HALO - (CONV_W - 1)


def _conv_body(x_ref, buf0_ref, w1_ref, b1_ref, wdw_ref, bdw_ref, g_ref, b_ref, w2_ref, b2_ref,
               o_ref, tail_ref, ext_ref, z_ref, *, t_last):
    bf16 = jnp.bfloat16
    tt = x_ref.shape[1]
    i = pl.program_id(1)

    @pl.when(i == 0)
    def _():
        ext_ref[0:CONV_HALO, :] = buf0_ref[0]
        ext_ref[CONV_HALO + tt:CONV_HALO + tt + SUBLANES, :] = jnp.zeros((SUBLANES, D_CONV), jnp.float32)

    h = _dot(x_ref[0].astype(bf16), w1_ref[...]) + b1_ref[...]
    ext_ref[CONV_HALO:CONV_HALO + tt, :] = h[:, :D_CONV] * jax.nn.sigmoid(h[:, D_CONV:])
    c = jnp.zeros((tt, D_CONV), jnp.float32) + bdw_ref[...]
    for r in range(SUBLANES):
        z = None
        for a in range(CONV_HALO // SUBLANES + 1):
            k = SUBLANES * a + r - CONV_LEAD
            if 0 <= k < CONV_W:
                term = ext_ref[SUBLANES * a:SUBLANES * a + tt + SUBLANES, :] * wdw_ref[k:k + 1, :]
                z = term if z is None else z + term
        if r == 0:
            c = c + z[:tt]
        else:
            z_ref[...] = z
            c = c + z_ref[pl.ds(r, tt), :]
    c = _ln_rows(c, g_ref[...], b_ref[...])
    c = c * jax.nn.sigmoid(c)
    o_ref[0] = _dot(c.astype(bf16), w2_ref[...]) + b2_ref[...]
    tail_ref[0] = ext_ref[t_last:t_last + CONV_HALO, :]
    ext_ref[0:CONV_HALO, :] = ext_ref[tt:tt + CONV_HALO, :]


def _conv_module(x, conv_buf, w_pw1, b_pw1, w_dw, b_dw, ln_g, ln_b, w_pw2, b_pw2):
    bsz, t_, d = x.shape
    bf16 = jnp.bfloat16
    tp = -(-t_ // 8) * 8
    tt = min(tp, 256)
    n_t = tp // tt
    if tp != t_:
        x = jnp.pad(x, ((0, 0), (0, tp - t_), (0, 0)))
    if conv_buf is None:
        buf0 = jnp.zeros((bsz, CONV_HALO, D_CONV), jnp.float32)
    else:
        buf0 = jnp.pad(conv_buf, ((0, 0), (CONV_LEAD, 0), (0, 0)))
    fixed = lambda shape: pl.BlockSpec(shape, lambda b, i: (0,) * len(shape))
    per_b = pl.BlockSpec((1, CONV_HALO, D_CONV), lambda b, i: (b, 0, 0))
    out, tail = pl.pallas_call(
        functools.partial(_conv_body, t_last=t_ - (n_t - 1) * tt),
        grid=(bsz, n_t),
        in_specs=[pl.BlockSpec((1, tt, d), lambda b, i: (b, i, 0)), per_b,
                  fixed((d, 2 * D_CONV)), fixed((1, 2 * D_CONV)), fixed((CONV_HALO, D_CONV)), fixed((1, D_CONV)),
                  fixed((1, D_CONV)), fixed((1, D_CONV)), fixed((D_CONV, d)), fixed((1, d))],
        out_specs=[pl.BlockSpec((1, tt, d), lambda b, i: (b, i, 0)), per_b],
        out_shape=[jax.ShapeDtypeStruct((bsz, tp, d), jnp.float32),
                   jax.ShapeDtypeStruct((bsz, CONV_HALO, D_CONV), jnp.float32)],
        scratch_shapes=[pltpu.VMEM((CONV_HALO + tt + SUBLANES, D_CONV), jnp.float32),
                        pltpu.VMEM((tt + SUBLANES, D_CONV), jnp.float32)],
        compiler_params=pltpu.CompilerParams(dimension_semantics=("arbitrary", "arbitrary"),
                                             vmem_limit_bytes=VMEM_LIMIT),
        name="conv_module",
    )(x, buf0, w_pw1.astype(bf16), b_pw1.reshape(1, -1), jnp.pad(w_dw, ((0, CONV_HALO - CONV_W), (0, 0))),
      b_dw.reshape(1, -1), ln_g.reshape(1, -1), ln_b.reshape(1, -1), w_pw2.astype(bf16), b_pw2.reshape(1, -1))
    return out[:, :t_], tail[:, CONV_LEAD:]


PACK_W = 256
SC_WINDOW = 128
SC_TILES = 32


def _pack_rows(y):
    out = []
    for h in range(2):
        lo = lax.bitcast_convert_type(y[:, 2 * h * PACK_W:(2 * h + 1) * PACK_W].astype(jnp.bfloat16)
                                      .astype(jnp.float32), jnp.uint32)
        hi = lax.bitcast_convert_type(y[:, (2 * h + 1) * PACK_W:(2 * h + 2) * PACK_W].astype(jnp.bfloat16)
                                      .astype(jnp.float32), jnp.uint32)
        out.append(lax.bitcast_convert_type((lo >> 16) | hi, jnp.int32))
    return out


def _unpack_words(w):
    u = lax.bitcast_convert_type(w, jnp.uint32)
    lo = lax.bitcast_convert_type(u << 16, jnp.float32)
    hi = lax.bitcast_convert_type(u & jnp.uint32(0xFFFF0000), jnp.float32)
    return lo, hi


def _gather_rows(src, idx):
    n = idx.shape[0]
    if n % (SC_WINDOW * SC_TILES) != 0:
        return jnp.take(src, idx, axis=0)
    mesh = plsc.VectorSubcoreMesh(core_axis_name="core", subcore_axis_name="subcore")

    @pl.kernel(out_type=jax.ShapeDtypeStruct((n, src.shape[1]), src.dtype), mesh=mesh)
    def gather_kernel(src_hbm, idx_hbm, out_hbm):
        def step(idx_vmem, out_vmem):
            pltpu.sync_copy(src_hbm.at[idx_vmem.at[0]], out_vmem)

        pltpu.emit_pipeline(
            step, grid=(n // SC_WINDOW,),
            in_specs=[pl.BlockSpec((1, SC_WINDOW), index_map=lambda i: (0, i))],
            out_specs=[pl.BlockSpec((SC_WINDOW, src.shape[1]), index_map=lambda i: (i, 0))],
            core_axis_name=("core", "subcore"),
            dimension_semantics=(pltpu.PARALLEL,),
        )(idx_hbm, out_hbm)

    return gather_kernel(src, idx.reshape(1, n))


def _scatter_rows(src, idx, n_out):
    n = idx.shape[0]
    m = src.shape[0] // 2
    reps = n // (2 * m)
    if n % (SC_WINDOW * SC_TILES) != 0 or m % SC_WINDOW != 0:
        rows = jnp.arange(n, dtype=jnp.int32)
        src_row = (rows // (reps * m)) * m + rows % m
        return jnp.zeros((n_out, src.shape[1]), src.dtype).at[idx].set(jnp.take(src, src_row, axis=0))
    tiles = m // SC_WINDOW
    mesh = plsc.VectorSubcoreMesh(core_axis_name="core", subcore_axis_name="subcore")

    @pl.kernel(out_type=jax.ShapeDtypeStruct((n_out, src.shape[1]), src.dtype), mesh=mesh, scratch_types=[])
    def scatter_kernel(src_hbm, idx_hbm, out_hbm):
        def step(src_vmem, idx_vmem):
            pltpu.sync_copy(src_vmem, out_hbm.at[idx_vmem.at[0]])

        pltpu.emit_pipeline(
            step, grid=(n // SC_WINDOW,),
            in_specs=[pl.BlockSpec((SC_WINDOW, src.shape[1]),
                                   index_map=lambda i: ((i // (reps * tiles)) * tiles + i % tiles, 0)),
                      pl.BlockSpec((1, SC_WINDOW), index_map=lambda i: (0, i))],
            out_specs=[],
            core_axis_name=("core", "subcore"),
            dimension_semantics=(pltpu.PARALLEL,),
        )(src_hbm, idx_hbm)

    return scatter_kernel(src, idx.reshape(1, n))


PER_GROUP = N_EXPERTS // N_GROUPS
PICKED = -3e38


def _ln_rows(v, g, b):
    mu = jnp.mean(v, axis=-1, keepdims=True)
    c = v - mu
    var = jnp.mean(c * c, axis=-1, keepdims=True)
    return c * lax.rsqrt(var + LN_EPS) * g + b


def _first_max(v, ids, axes, sentinel):
    best = v
    for a in axes:
        best = jnp.max(best, axis=a, keepdims=True)
    first = jnp.where(v == best, ids, sentinel)
    for a in axes:
        first = jnp.min(first, axis=a, keepdims=True)
    return best, first


def _sum_axes(v, axes):
    for a in axes:
        v = jnp.sum(v, axis=a, keepdims=True)
    return v


def _moe_pre_body(x_ref, mix_ref, g_ref, b_ref, wr_ref, br_ref, wgu_ref, wdn_ref,
                  x1_ref, xp_ref, sh_ref, eidx_ref, gate_ref, rank_ref, cnt_ref, run_ref):
    f32, bf16 = jnp.float32, jnp.bfloat16
    tm = x_ref.shape[0]

    @pl.when(pl.program_id(0) == 0)
    def _():
        run_ref[...] = jnp.zeros(run_ref.shape, f32)

    x1 = _ln_rows(ALPHA * x_ref[...] + mix_ref[...], g_ref[...], b_ref[...])
    x1_ref[...] = x1
    x1b = x1.astype(bf16)
    xp_ref[0], xp_ref[1] = _pack_rows(x1)

    h = _dot(x1b, wgu_ref[...])
    d_sh = h.shape[1] // 2
    act = (jax.nn.silu(h[:, :d_sh]) * h[:, d_sh:]).astype(bf16)
    sh_ref[...] = _dot(act, wdn_ref[...])

    s = jax.nn.sigmoid(_dot_nt(wr_ref[...], x1b)).reshape(N_GROUPS, PER_GROUP, tm)
    sb = s + br_ref[...].reshape(N_GROUPS, PER_GROUP, 1)
    shape3 = (N_GROUPS, PER_GROUP, tm)
    pid = lax.broadcasted_iota(jnp.int32, shape3, 1)
    gid = lax.broadcasted_iota(jnp.int32, (N_GROUPS, 1, tm), 0)
    eid = lax.broadcasted_iota(jnp.int32, shape3, 0) * PER_GROUP + pid
    top1, i1 = _first_max(sb, pid, (1,), PER_GROUP)
    top2 = jnp.max(jnp.where(pid == i1, PICKED, sb), axis=1, keepdims=True)
    gscore = top1 + top2
    gsel = jnp.zeros((N_GROUPS, 1, tm), f32)
    for _ in range(TOPK_GROUPS):
        _, first = _first_max(gscore, gid, (0,), N_GROUPS)
        hit = gid == first
        gsel = jnp.where(hit, 1.0, gsel)
        gscore = jnp.where(hit, PICKED, gscore)
    cand = jnp.where(gsel > 0.0, sb, -1e30)
    firsts, gates = [], []
    picked = jnp.zeros(shape3, f32)
    for _ in range(TOP_K):
        _, first = _first_max(cand, eid, (0, 1), N_EXPERTS)
        hit = eid == first
        firsts.append(first)
        gates.append(_sum_axes(jnp.where(hit, s, 0.0), (0, 1)))
        picked = jnp.where(hit, 1.0, picked)
        cand = jnp.where(hit, PICKED, cand)
    gsum = gates[0]
    for gk in gates[1:]:
        gsum = gsum + gk
    earlier = (lax.broadcasted_iota(jnp.int32, (tm, tm), 0) < lax.broadcasted_iota(jnp.int32, (tm, tm), 1))
    picked2 = picked.reshape(N_EXPERTS, tm)
    rank = run_ref[...] + _dot(picked2.astype(bf16), jnp.where(earlier, 1.0, 0.0).astype(bf16))
    run_new = run_ref[...] + jnp.sum(picked2, axis=1, keepdims=True)
    run_ref[...] = run_new
    cnt_ref[...] = jnp.broadcast_to(run_new, cnt_ref.shape)
    rank3 = rank.reshape(shape3)
    for k in range(TOP_K):
        hit = eid == firsts[k]
        eidx_ref[k:k + 1, :] = firsts[k].reshape(1, tm)
        gate_ref[k:k + 1, :] = (gates[k] / gsum * ROUTE_SCALE).reshape(1, tm)
        rank_ref[k:k + 1, :] = _sum_axes(jnp.where(hit, rank3, 0.0), (0, 1)).reshape(1, tm).astype(jnp.int32)


def _moe_pre(x, mix, g, b, w_router, b_router, w_sh_gu, w_sh_down):
    m, d = x.shape
    bf16 = jnp.bfloat16
    tm = min(m, 512)
    row = lambda i: (i, 0)
    col = lambda i: (0, i)
    fixed = lambda i: (0, 0)
    d_sh2 = w_sh_gu.shape[1]
    return pl.pallas_call(
        _moe_pre_body,
        grid=(m // tm,),
        in_specs=[pl.BlockSpec((tm, d), row), pl.BlockSpec((tm, d), row),
                  pl.BlockSpec((1, d), fixed), pl.BlockSpec((1, d), fixed),
                  pl.BlockSpec((N_EXPERTS, d), fixed), pl.BlockSpec((N_EXPERTS, 1), fixed),
                  pl.BlockSpec((d, d_sh2), fixed), pl.BlockSpec((d_sh2 // 2, d), fixed)],
        out_specs=[pl.BlockSpec((tm, d), row), pl.BlockSpec((2, tm, PACK_W), lambda i: (0, i, 0)),
                   pl.BlockSpec((tm, d), row),
                   pl.BlockSpec((TOP_K, tm), col), pl.BlockSpec((TOP_K, tm), col), pl.BlockSpec((TOP_K, tm), col),
                   pl.BlockSpec((N_EXPERTS, LANE), fixed)],
        out_shape=[jax.ShapeDtypeStruct((m, d), jnp.float32), jax.ShapeDtypeStruct((2, m, PACK_W), jnp.int32),
                   jax.ShapeDtypeStruct((m, d), jnp.float32),
                   jax.ShapeDtypeStruct((TOP_K, m), jnp.int32), jax.ShapeDtypeStruct((TOP_K, m), jnp.float32),
                   jax.ShapeDtypeStruct((TOP_K, m), jnp.int32),
                   jax.ShapeDtypeStruct((N_EXPERTS, LANE), jnp.float32)],
        scratch_shapes=[pltpu.VMEM((N_EXPERTS, 1), jnp.float32)],
        compiler_params=pltpu.CompilerParams(dimension_semantics=("arbitrary",),
                                             vmem_limit_bytes=VMEM_LIMIT),
        name="moe_pre",
    )(x, mix, g.reshape(1, d), b.reshape(1, d), w_router.T.astype(bf16), b_router.reshape(N_EXPERTS, 1),
      w_sh_gu.astype(bf16), w_sh_down.astype(bf16))


def _moe_expert_body(exp_ref, first_ref, rows_ref, xs_ref, wgu_ref, wdn_ref, y_ref, wgu_bf, wdn_bf):
    i = pl.program_id(0)
    bf16 = jnp.bfloat16

    @pl.when(first_ref[i] == 1)
    def _():
        wgu_bf[...] = wgu_ref[0, 0].astype(bf16)
        wdn_bf[...] = wdn_ref[0, 0].astype(bf16)

    @pl.when(rows_ref[i] > 0)
    def _():
        live = lax.broadcasted_iota(jnp.int32, (xs_ref.shape[1], 1), 0) < rows_ref[i]
        h = None
        for hw in range(2):
            for q, xq in enumerate(_unpack_words(xs_ref[hw])):
                r0 = (2 * hw + q) * PACK_W
                part = _dot(jnp.where(live, xq, 0.0).astype(bf16), wgu_bf[r0:r0 + PACK_W, :])
                h = part if h is None else h + part
        d_e = h.shape[1] // 2
        act = (jax.nn.silu(h[:, :d_e]) * h[:, d_e:]).astype(bf16)
        y_ref[0], y_ref[1] = _pack_rows(_dot(act, wdn_bf[...]))

    @pl.when(rows_ref[i] == 0)
    def _():
        y_ref[...] = jnp.zeros(y_ref.shape, y_ref.dtype)


def _moe_experts(xs, blk_exp, blk_first, blk_rows, w_exp_gu, w_exp_down, layer, bm):
    n_slots = xs.shape[1]
    d = w_exp_gu.shape[2]
    n_blk = n_slots // bm
    d_e2 = w_exp_gu.shape[3]
    words = lambda i, e, f, a: (0, i, 0)
    grid_spec = pltpu.PrefetchScalarGridSpec(
        num_scalar_prefetch=3,
        grid=(n_blk,),
        in_specs=[pl.BlockSpec((2, bm, PACK_W), words),
                  pl.BlockSpec((1, 1, d, d_e2), lambda i, e, f, a: (layer, e[i], 0, 0)),
                  pl.BlockSpec((1, 1, d_e2 // 2, d), lambda i, e, f, a: (layer, e[i], 0, 0))],
        out_specs=pl.BlockSpec((2, bm, PACK_W), words),
        scratch_shapes=[pltpu.VMEM((d, d_e2), jnp.bfloat16), pltpu.VMEM((d_e2 // 2, d), jnp.bfloat16)])
    return pl.pallas_call(
        _moe_expert_body,
        grid_spec=grid_spec,
        out_shape=jax.ShapeDtypeStruct((2, n_slots, PACK_W), jnp.int32),
        compiler_params=pltpu.CompilerParams(dimension_semantics=("arbitrary",),
                                             vmem_limit_bytes=VMEM_LIMIT),
        name="moe_experts",
    )(blk_exp, blk_first, blk_rows, xs, w_exp_gu, w_exp_down)


def _combine_ln_body(x_ref, yg_ref, gt_ref, sh_ref, g_ref, b_ref, o_ref):
    gt = gt_ref[...]
    parts = []
    for hw in range(2):
        lo_acc = hi_acc = None
        for k in range(TOP_K):
            lo, hi = _unpack_words(yg_ref[hw, k])
            gk = gt[:, k:k + 1]
            lo_acc = lo * gk if lo_acc is None else lo_acc + lo * gk
            hi_acc = hi * gk if hi_acc is None else hi_acc + hi * gk
        parts += [lo_acc, hi_acc]
    routed = jnp.concatenate(parts, axis=1)
    o_ref[...] = _ln_rows(ALPHA * x_ref[...] + (routed + sh_ref[...]), g_ref[...], b_ref[...])


def _combine_ln(x, yg, gate_t, shared, g, b):
    m, d = x.shape
    tm = min(m, 256)
    row = lambda i: (i, 0)
    fixed = lambda i: (0, 0)
    return pl.pallas_call(
        _combine_ln_body,
        grid=(m // tm,),
        in_specs=[pl.BlockSpec((tm, d), row), pl.BlockSpec((2, TOP_K, tm, PACK_W), lambda i: (0, 0, i, 0)),
                  pl.BlockSpec((tm, TOP_K), row), pl.BlockSpec((tm, d), row),
                  pl.BlockSpec((1, d), fixed), pl.BlockSpec((1, d), fixed)],
        out_specs=pl.BlockSpec((tm, d), row),
        out_shape=jax.ShapeDtypeStruct((m, d), jnp.float32),
        compiler_params=pltpu.CompilerParams(dimension_semantics=("arbitrary",)),
        name="combine_ln",
    )(x, yg, gate_t, shared, g.reshape(1, d), b.reshape(1, d))


def _moe_layer(x, mix, ln1_g, ln1_b, ln2_g, ln2_b, w_router, b_router, w_exp_gu, w_exp_down, layer,
               w_sh_gu, w_sh_down):
    m, d = x.shape
    x1, xp, shared, eidx, gate8, rank8, counts = _moe_pre(x, mix, ln1_g, ln1_b, w_router, b_router,
                                                           w_sh_gu, w_sh_down)
    bm = 512 if m * TOP_K >= 512 * N_EXPERTS else MOE_BLK
    n_blk = (m * TOP_K) // bm + N_EXPERTS
    counts = counts[:, 0].astype(jnp.int32)
    padded = (counts + bm - 1) // bm * bm
    pad_end = jnp.cumsum(padded)
    pad_start = pad_end - padded
    start_of = jnp.sum(jnp.where(eidx[:, :, None] == jnp.arange(N_EXPERTS), pad_start, 0), axis=-1)
    dest = (start_of + rank8).reshape(-1)
    blk_start = jnp.arange(n_blk, dtype=jnp.int32) * bm
    blk_exp = jnp.minimum(jnp.sum(pad_end[None, :] <= blk_start[:, None], axis=1), N_EXPERTS - 1).astype(jnp.int32)
    blk_rows = jnp.clip(counts[blk_exp] - (blk_start - pad_start[blk_exp]), 0, bm).astype(jnp.int32)
    blk_first = jnp.concatenate([jnp.ones((1,), jnp.int32), (blk_exp[1:] != blk_exp[:-1]).astype(jnp.int32)])
    n_slots = n_blk * bm
    xs = _scatter_rows(xp.reshape(2 * m, PACK_W), jnp.concatenate([dest, dest + n_slots]), 2 * n_slots)
    y = _moe_experts(xs.reshape(2, n_slots, PACK_W), blk_exp, blk_first, blk_rows, w_exp_gu, w_exp_down, layer, bm)
    yg = _gather_rows(y.reshape(2 * n_slots, PACK_W), jnp.concatenate([dest, dest + n_slots]))
    return _combine_ln(x1, yg.reshape(2, TOP_K, m, PACK_W), gate8.T, shared, ln2_g, ln2_b)


def _trunk(x, pos, gla_state, nsa_cache, page_table, win_buf, conv_buf,
           w_in_ab, w_gla_gate, b_gla_gate, gla_norm_g, w_cmp_pool, w_out_ab,
           w_pw1, b_pw1, w_dw, b_dw, conv_ln_g, conv_ln_b, w_pw2, b_pw2,
           ln_g, ln_b, w_router, b_router, w_exp_gu, w_exp_down, w_sh_gu, w_sh_down):
    new_gla, new_rows, new_win, new_conv = [], [], [], []
    for layer in range(DEPTH):
        i = layer // 2
        if layer % 2 == 0:
            mix, s_a, rows, win = _ab_mixer(
                x, pos, w_in_ab[i], w_gla_gate[i], b_gla_gate[i], gla_norm_g[i], w_cmp_pool[i], w_out_ab[i],
                None if gla_state is None else gla_state[i],
                None if nsa_cache is None else nsa_cache[i], page_table,
                None if win_buf is None else win_buf[i])
            new_gla.append(s_a)
            new_rows.append(rows)
            new_win.append(win)
        else:
            mix, cb = _conv_module(x, None if conv_buf is None else conv_buf[i], w_pw1[i], b_pw1[i],
                                   w_dw[i], b_dw[i], conv_ln_g[i], conv_ln_b[i], w_pw2[i], b_pw2[i])
            new_conv.append(cb)
        bsz, t_, d = x.shape
        x = _moe_layer(x.reshape(-1, d), mix.reshape(-1, d), ln_g[layer, 0], ln_b[layer, 0],
                       ln_g[layer, 1], ln_b[layer, 1], w_router[layer], b_router[layer],
                       w_exp_gu, w_exp_down, layer, w_sh_gu[layer], w_sh_down[layer]).reshape(bsz, t_, d)
    return x, jnp.stack(new_gla), jnp.stack(new_rows), jnp.stack(new_win), jnp.stack(new_conv)


def kernel(x_prompt, x_sample, state_gla, cache_nsa_kv, state_nsa_win, state_conv, page_table,
           w_in_ab, w_gla_gate, b_gla_gate, gla_norm_g, w_cmp_pool, w_out_ab,
           w_pw1, b_pw1, w_dw, b_dw, conv_ln_g, conv_ln_b, w_pw2, b_pw2,
           ln_g, ln_b, w_router, b_router, w_exp_gu, w_exp_down, w_sh_gu, w_sh_down):
    weights = (w_in_ab, w_gla_gate, b_gla_gate, gla_norm_g, w_cmp_pool, w_out_ab,
               w_pw1, b_pw1, w_dw, b_dw, conv_ln_g, conv_ln_b, w_pw2, b_pw2,
               ln_g, ln_b, w_router, b_router, w_exp_gu, w_exp_down, w_sh_gu, w_sh_down)
    past_len = page_table.shape[1] * PAGE_SIZE
    pos_p = jnp.arange(x_prompt.shape[1])
    pos_s = past_len + jnp.arange(x_sample.shape[1])
    y_prompt, gla_p, rows_p, win_p, conv_p = _trunk(x_prompt, pos_p, None, None, None, None, None, *weights)
    y_sample, gla_s, rows_s, win_s, conv_s = _trunk(x_sample, pos_s, state_gla, cache_nsa_kv, page_table,
                                                    state_nsa_win, state_conv, *weights)
    return (y_prompt, y_sample, gla_p, gla_s, rows_p, rows_s, win_p, win_s, conv_p, conv_s)
```

```python
import functools
import math

import jax
import jax.numpy as jnp
import numpy as np
from jax import lax
from jax.experimental import pallas as pl
from jax.experimental.pallas import tpu as pltpu
from jax.experimental.pallas import tpu_sc as plsc

D_MODEL = 1024
DEPTH = 2
PAGE_SIZE = 128

GLA_HEADS = 4
GLA_DV = D_MODEL // 2 // GLA_HEADS
GLA_DK = GLA_DV // 2
GLA_RANK = 16
GLA_TAU = 16.0

NSA_HEADS = 8
NSA_KV_HEADS = 2
NSA_GROUP = NSA_HEADS // NSA_KV_HEADS
HEAD_DIM = D_MODEL // 2 // NSA_HEADS
CMP_BLK = 32
CMP_STRIDE = 16
SEL_BLK = 64
SEL_TOPN = 16
WINDOW = 512
Q_BLK = 128
FORCE_BONUS = 100.0
ROPE_DIM = HEAD_DIM // 4
ROPE_THETA = 500000.0

GLA_SIZES = (GLA_HEADS * GLA_DK, GLA_HEADS * GLA_DK, GLA_HEADS * GLA_DV, GLA_HEADS * GLA_DV, GLA_RANK)
NSA_SIZES = (NSA_HEADS * HEAD_DIM, 6 * NSA_KV_HEADS * HEAD_DIM, 3 * NSA_HEADS)

CONV_W = 31
D_CONV = D_MODEL

N_EXPERTS = 64
N_GROUPS = 8
TOPK_GROUPS = 4
TOP_K = 8
D_EXPERT = 256
ROUTE_SCALE = 2.5
MOE_BLK = 128

ALPHA = (2 * DEPTH) ** 0.25
LN_EPS = 1e-5

LANE = 128
SUBLANES = 8
V7X_VMEM_BYTES = 64 * 1024 * 1024
VMEM_LIMIT = V7X_VMEM_BYTES * 3 // 4


def _dot(a, b):
    return jnp.dot(a, b, preferred_element_type=jnp.float32)


def _dot_nt(a, b):
    return lax.dot_general(a, b, (((1,), (1,)), ((), ())), preferred_element_type=jnp.float32)


def _mm_body(x_ref, w_ref, o_ref):
    o_ref[...] = _dot(x_ref[...].astype(jnp.bfloat16), w_ref[...].astype(jnp.bfloat16))


def _mm(x, w, keep_pad=False):
    m, k = x.shape
    n = w.shape[1]
    n_pad = -(-n // LANE) * LANE
    w = w.astype(jnp.bfloat16)
    if n_pad != n:
        w = jnp.pad(w, ((0, 0), (0, n_pad - n)))
    tm = min(m, 512)
    out = pl.pallas_call(
        _mm_body,
        grid=(m // tm,),
        in_specs=[pl.BlockSpec((tm, k), lambda i: (i, 0)),
                  pl.BlockSpec((k, n_pad), lambda i: (0, 0))],
        out_specs=pl.BlockSpec((tm, n_pad), lambda i: (i, 0)),
        out_shape=jax.ShapeDtypeStruct((m, n_pad), jnp.float32),
        compiler_params=pltpu.CompilerParams(dimension_semantics=("arbitrary",),
                                             vmem_limit_bytes=VMEM_LIMIT),
        name="mm",
    )(x, w)
    return out if keep_pad or n_pad == n else out[:, :n]


def _mm_pair_body(a_ref, b_ref, w_ref, o_ref):
    ka = a_ref.shape[1]
    o_ref[...] = (_dot(a_ref[...].astype(jnp.bfloat16), w_ref[0:ka, :])
                  + _dot(b_ref[...].astype(jnp.bfloat16), w_ref[ka:, :]))


def _mm_pair(a, b, w):
    m, ka = a.shape
    kb = b.shape[1]
    n = w.shape[1]
    tm = min(m, 512)
    return pl.pallas_call(
        _mm_pair_body,
        grid=(m // tm,),
        in_specs=[pl.BlockSpec((tm, ka), lambda i: (i, 0)), pl.BlockSpec((tm, kb), lambda i: (i, 0)),
                  pl.BlockSpec((ka + kb, n), lambda i: (0, 0))],
        out_specs=pl.BlockSpec((tm, n), lambda i: (i, 0)),
        out_shape=jax.ShapeDtypeStruct((m, n), jnp.float32),
        compiler_params=pltpu.CompilerParams(dimension_semantics=("arbitrary",)),
        name="mm_pair",
    )(a, b, w.astype(jnp.bfloat16))


def _partial_rope(x, pos):
    half = ROPE_DIM // 2
    inv_freq = jnp.power(ROPE_THETA, -jnp.arange(half, dtype=jnp.float32) / half)
    ang = pos.astype(jnp.float32)[:, None] * inv_freq
    ang = ang.reshape(ang.shape[0], *([1] * (x.ndim - 3)), half)
    cos, sin = jnp.cos(ang), jnp.sin(ang)
    x1 = x[..., :half]
    x2 = x[..., half:ROPE_DIM]
    rot = jnp.concatenate([x1 * cos - x2 * sin, x2 * cos + x1 * sin], -1)
    return jnp.concatenate([rot, x[..., ROPE_DIM:]], -1)


NSA_ROWS = NSA_GROUP * Q_BLK
SEL_KT = 1024
N_SELB = 128
MASKED = -1e9
WIN_KEYS = WINDOW + Q_BLK
KK_W = 2 * HEAD_DIM + N_SELB


def _nsa_prompt_body(qr_ref, qo_ref, kc_ref, vct_ref, kk_ref, vvt_ref, g_ref, o_ref,
                     imp_ref, m_ref, l_ref, acc_ref, oct_ref, selb_ref):
    f32, bf16 = jnp.float32, jnp.bfloat16
    qb = pl.program_id(2)
    q0 = qb * Q_BLK
    qr_t = qr_ref[0, 0, 0]
    qo_t = qo_ref[0, 0, 0]
    n_cmp = kc_ref.shape[2]

    ratio = SEL_BLK // CMP_STRIDE
    chunk = min(Q_BLK, n_cmp)
    n_chunks = n_cmp // chunk

    def compressed_and_select(n_act):
        nc = n_act * chunk
        nb = nc // ratio
        s_c = _dot(kc_ref[0, 0, 0:nc, :], qr_t)
        n_idx = lax.broadcasted_iota(jnp.int32, (nc, NSA_ROWS), 0)
        qpos_c = q0 + (lax.broadcasted_iota(jnp.int32, (nc, NSA_ROWS), 1) & (Q_BLK - 1))
        cmask = (n_idx * CMP_STRIDE + (CMP_BLK - 1)) <= qpos_c
        s_c = jnp.where(cmask, s_c, MASKED)
        m_c = jnp.max(s_c, axis=0, keepdims=True)
        p_c = jnp.where(cmask, jnp.exp(s_c - m_c), 0.0)
        p_c = p_c / jnp.maximum(jnp.sum(p_c, axis=0, keepdims=True), 1e-30)
        oct_ref[...] = _dot(vct_ref[0, 0, :, 0:nc], p_c.astype(bf16))
        imp = (p_c[:, 0:Q_BLK] + p_c[:, Q_BLK:2 * Q_BLK]) + p_c[:, 2 * Q_BLK:3 * Q_BLK] + p_c[:, 3 * Q_BLK:]
        imp_ref[0:8, :] = jnp.zeros((8, Q_BLK), f32)
        imp_ref[8:8 + nc, :] = imp
        imp_s = imp_ref[pl.ds(7, nb, stride=ratio), :]
        for r in range(ratio):
            imp_s = imp_s + imp_ref[pl.ds(8 + r, nb, stride=ratio), :]
        blk = lax.broadcasted_iota(jnp.int32, (nb, Q_BLK), 0)
        qpos_s = q0 + lax.broadcasted_iota(jnp.int32, (nb, Q_BLK), 1)
        cur = lax.shift_right_logical(qpos_s, int(math.log2(SEL_BLK)))
        valid = blk * SEL_BLK <= qpos_s
        forced = (blk == 0) | (blk == cur) | (blk == cur - 1)
        score = jnp.where(valid, imp_s + jnp.where(forced, FORCE_BONUS, 0.0), -1e30)
        picked = jnp.zeros((nb, Q_BLK), f32)
        for _ in range(SEL_TOPN):
            best = jnp.max(score, axis=0, keepdims=True)
            first = jnp.min(jnp.where(score == best, blk, nb), axis=0, keepdims=True)
            hit = blk == first
            picked = jnp.where(hit, 1.0, picked)
            score = jnp.where(hit, -3e38, score)
        sel = jnp.where(valid, picked, 0.0)
        if nb < N_SELB:
            sel = jnp.concatenate([sel, jnp.zeros((N_SELB - nb, Q_BLK), f32)], axis=0)
        sel = ((sel - 1.0) * (-MASKED)).astype(bf16)
        selb_ref[...] = jnp.concatenate([sel] * NSA_GROUP, axis=1)

    need = jnp.minimum((q0 + Q_BLK - CMP_BLK) // (CMP_STRIDE * chunk) + 1, n_chunks)
    for n_act in range(1, n_chunks + 1):
        pl.when(need == n_act)(functools.partial(compressed_and_select, n_act))
    o_ct = oct_ref[...]
    selb_t = selb_ref[...]

    zeros_q = jnp.zeros((HEAD_DIM, NSA_ROWS), bf16)
    q_sel = jnp.concatenate([qo_t, zeros_q, selb_t], axis=0)
    q_win = jnp.concatenate([zeros_q, qo_t, jnp.zeros((N_SELB, NSA_ROWS), bf16)], axis=0)
    qpos_r = q0 + (lax.broadcasted_iota(jnp.int32, (1, NSA_ROWS), 1) & (Q_BLK - 1))

    def v_tiles(first, count):
        return jnp.concatenate([vvt_ref[0, 0, first + j] for j in range(count)], axis=1)

    m_ref[...] = jnp.full(m_ref.shape, MASKED, f32)
    l_ref[...] = jnp.zeros(l_ref.shape, f32)
    acc_ref[...] = jnp.zeros(acc_ref.shape, f32)

    def sel_tile(k0, kt, causal):
        s = _dot(kk_ref[0, 0, pl.ds(k0, kt), :], q_sel)
        if causal:
            kpos = k0 + lax.broadcasted_iota(jnp.int32, (kt, NSA_ROWS), 0)
            s = jnp.where(kpos <= qpos_r, s, MASKED)
        m_old = m_ref[...]
        m_new = jnp.maximum(m_old, jnp.max(s, axis=0, keepdims=True))
        alpha = jnp.exp(m_old - m_new)
        p = jnp.exp(s - m_new)
        l_ref[...] = alpha * l_ref[...] + jnp.sum(p, axis=0, keepdims=True)
        vt = v_tiles(k0 // Q_BLK, kt // Q_BLK)
        acc_ref[...] = alpha * acc_ref[...] + _dot(vt, p.astype(bf16))
        m_ref[...] = m_new

    n_full = q0 // SEL_KT

    def full_step(t, c):
        sel_tile(pl.multiple_of(t * SEL_KT, SEL_KT), SEL_KT, False)
        return c

    lax.fori_loop(0, n_full, full_step, 0)
    d0 = pl.multiple_of(n_full * SEL_KT, SEL_KT)
    short = q0 + Q_BLK - n_full * SEL_KT <= SEL_KT // 2

    @pl.when(short)
    def _():
        sel_tile(d0, SEL_KT // 2, True)

    @pl.when(jnp.logical_not(short))
    def _():
        sel_tile(d0, SEL_KT, True)
    o_st = acc_ref[0:HEAD_DIM, :] / l_ref[...]

    w0 = pl.multiple_of(jnp.maximum(q0 - WINDOW, 0), Q_BLK)
    s_w = _dot(kk_ref[0, 0, pl.ds(w0, WIN_KEYS), :], q_win)
    kpos_w = w0 + lax.broadcasted_iota(jnp.int32, (WIN_KEYS, NSA_ROWS), 0)
    s_w = jnp.where((kpos_w <= qpos_r) & (kpos_w > qpos_r - WINDOW), s_w, MASKED)
    p_w = jnp.exp(s_w - jnp.max(s_w, axis=0, keepdims=True))
    l_w = jnp.sum(p_w, axis=0, keepdims=True)
    acc_w = _dot(v_tiles(w0 // Q_BLK, WIN_KEYS // Q_BLK), p_w.astype(bf16))
    o_wt = acc_w[HEAD_DIM:2 * HEAD_DIM, :] / l_w

    g = g_ref[0, 0, 0]
    out_t = g[0:1, :] * o_ct + g[1:2, :] * o_st + g[2:3, :] * o_wt
    o_ref[0] = jnp.concatenate([out_t[:, g_ * Q_BLK:(g_ + 1) * Q_BLK] for g_ in range(NSA_GROUP)], axis=0).T


def _nsa_prompt(qr, qo, gt, kc_p, vct, kk, vvt):
    bsz, _, nqb = qr.shape[:3]
    t_ = nqb * Q_BLK
    n_cmp = kc_p.shape[2]
    per_blk = lambda b, h, i: (b, h, i, 0, 0)
    per_head = lambda b, h, i: (b, h, 0, 0)
    return pl.pallas_call(
        _nsa_prompt_body,
        grid=(bsz, NSA_KV_HEADS, nqb),
        in_specs=[pl.BlockSpec((1, 1, 1, HEAD_DIM, NSA_ROWS), per_blk),
                  pl.BlockSpec((1, 1, 1, HEAD_DIM, NSA_ROWS), per_blk),
                  pl.BlockSpec((1, 1, n_cmp, HEAD_DIM), per_head),
                  pl.BlockSpec((1, 1, HEAD_DIM, n_cmp), per_head),
                  pl.BlockSpec((1, 1, t_, KK_W), per_head),
                  pl.BlockSpec((1, 1, nqb, 2 * HEAD_DIM, Q_BLK), lambda b, h, i: (b, h, 0, 0, 0)),
                  pl.BlockSpec((1, 1, 1, 3, NSA_ROWS), per_blk)],
        out_specs=pl.BlockSpec((1, Q_BLK, NSA_GROUP * HEAD_DIM), lambda b, h, i: (b, i, h)),
        out_shape=jax.ShapeDtypeStruct((bsz, t_, NSA_HEADS * HEAD_DIM), jnp.float32),
        scratch_shapes=[pltpu.VMEM((8 + n_cmp, Q_BLK), jnp.float32),
                        pltpu.VMEM((1, NSA_ROWS), jnp.float32),
                        pltpu.VMEM((1, NSA_ROWS), jnp.float32),
                        pltpu.VMEM((2 * HEAD_DIM, NSA_ROWS), jnp.float32),
                        pltpu.VMEM((HEAD_DIM, NSA_ROWS), jnp.float32),
                        pltpu.VMEM((N_SELB, NSA_ROWS), jnp.bfloat16)],
        compiler_params=pltpu.CompilerParams(
            dimension_semantics=("arbitrary", "arbitrary", "arbitrary"),
            vmem_limit_bytes=VMEM_LIMIT),
        name="nsa_prompt",
    )(qr, qo, kc_p, vct, kk, vvt, gt)


GLA_SUB = 16
GLA_UNROLL = 8
GLA_TILE = 256
GLA_QK = GLA_HEADS * GLA_DK
GLA_V = GLA_HEADS * GLA_DV


def _dot_tn(a, b):
    return lax.dot_general(a, b, (((0,), (0,)), ((), ())), preferred_element_type=jnp.float32)


def _gla_body(q_ref, k_ref, v_ref, gr_ref, glr_ref, wg_ref, bg_ref, ng_ref, s0_ref, exp_ref,
              o_ref, sfin_ref, st_ref, b_ref, qd_ref, *, t_valid):
    f32, bf16 = jnp.float32, jnp.bfloat16
    tt = q_ref.shape[1]
    ti = pl.program_id(1)

    @pl.when(ti == 0)
    def _():
        st_ref[...] = s0_ref[0]

    row = lax.broadcasted_iota(jnp.int32, (tt, 1), 0)
    z = _dot(glr_ref[0][:, :GLA_RANK].astype(bf16), wg_ref[...]) + bg_ref[...]
    la = (jnp.minimum(z, 0.0) - jnp.log1p(jnp.exp(-jnp.abs(z)))) * (1.0 / GLA_TAU)
    la = jnp.where(ti * tt + row < t_valid, la, 0.0)
    seg = row & (GLA_SUB - 1)
    b = la
    for s in (1, 2, 4, 8):
        b = b + jnp.where(seg >= s, pltpu.roll(b, s, axis=0), 0.0)
    q = q_ref[0] * (GLA_DK ** -0.5)
    k = k_ref[0]
    v = v_ref[0]
    o = _dot((q * k).astype(bf16), exp_ref[...]) * v
    for d in range(1, GLA_SUB):
        decay = jnp.exp(jnp.minimum(b - pltpu.roll(b, d, axis=0), 0.0))
        w = jnp.where(seg >= d, q * pltpu.roll(k, d, axis=0) * decay, 0.0)
        o = o + _dot(w.astype(bf16), exp_ref[...]) * pltpu.roll(v, d, axis=0)
    o_ref[0] = o
    b_ref[...] = b
    qd_ref[...] = (q * jnp.exp(b)).astype(bf16)

    def block_step(c, carry):
        rows = pl.ds(pl.multiple_of(c * GLA_SUB, GLA_SUB), GLA_SUB)
        qd = qd_ref[rows, :]
        bc = b_ref[rows, :]
        bl = bc[GLA_SUB - 1:GLA_SUB, :]
        kc = (k_ref[0, rows, :] * jnp.exp(bl - bc)).astype(bf16)
        keep = jnp.exp(bl)
        vb = v_ref[0, rows, :].astype(bf16)
        outs = []
        for h in range(GLA_HEADS):
            dk = slice(h * GLA_DK, (h + 1) * GLA_DK)
            dv = slice(h * GLA_DV, (h + 1) * GLA_DV)
            st = st_ref[dv, :]
            outs.append(_dot_nt(qd[:, dk], st.astype(bf16)))
            st_ref[dv, :] = st * keep[:, dk] + _dot_tn(vb[:, dv], kc[:, dk])
        o_ref[0, rows, :] += jnp.concatenate(outs, axis=1)
        return carry

    lax.fori_loop(0, tt // GLA_SUB, block_step, 0, unroll=GLA_UNROLL)
    sfin_ref[0] = st_ref[...]
    gr = gr_ref[0]
    gate = gr * jax.nn.sigmoid(gr)
    for h in range(GLA_HEADS):
        cols = slice(h * GLA_DV, (h + 1) * GLA_DV)
        oh = o_ref[0, :, cols]
        ms = jnp.mean(oh * oh, axis=-1, keepdims=True)
        o_ref[0, :, cols] = oh * lax.rsqrt(ms + LN_EPS) * ng_ref[...] * gate[:, cols]


def _gla(h, w_gla_gate, b_gla_gate, gla_norm_g, gla_state):
    bsz, t_, n_in = h.shape
    tp = -(-t_ // GLA_SUB) * GLA_SUB
    if tp != t_:
        h = jnp.pad(h, ((0, 0), (0, tp - t_), (0, 0)))
    tt = min(tp, GLA_TILE)
    expand =np.repeat(np.repeat(np.eye(GLA_HEADS, dtype=np.float32), GLA_DK, 0), GLA_DV, 1)
    if gla_state is None:
        s0 = jnp.zeros((bsz, GLA_V, GLA_DK), jnp.float32)
    else:
        s0 = gla_state.transpose(0, 1, 3, 2).reshape(bsz, GLA_V, GLA_DK)
    tile = lambda width, blk: pl.BlockSpec((1, tt, width), lambda b, i: (b, i, blk))
    fixed2 = lambda shape: pl.BlockSpec(shape, lambda b, i: (0, 0))
    per_b = pl.BlockSpec((1, GLA_V, GLA_DK), lambda b, i: (b, 0, 0))
    o, s_t = pl.pallas_call(
        functools.partial(_gla_body, t_valid=t_),
        grid=(bsz, tp // tt),
        in_specs=[tile(GLA_QK, 0), tile(GLA_QK, 1), tile(GLA_V, 1), tile(GLA_V, 2),
                  tile(LANE, (2 * GLA_QK + 2 * GLA_V + NSA_SIZES[0] + NSA_SIZES[1]) // LANE),
                  fixed2((GLA_RANK, GLA_QK)), fixed2((1, GLA_QK)), fixed2((1, GLA_DV)), per_b,
                  fixed2((GLA_QK, GLA_V))],
        out_specs=[pl.BlockSpec((1, tt, GLA_V), lambda b, i: (b, i, 0)), per_b],
        out_shape=[jax.ShapeDtypeStruct((bsz, tp, GLA_V), jnp.float32),
                   jax.ShapeDtypeStruct((bsz, GLA_V, GLA_DK), jnp.float32)],
        scratch_shapes=[pltpu.VMEM((GLA_V, GLA_DK), jnp.float32), pltpu.VMEM((tt, GLA_QK), jnp.float32),
                        pltpu.VMEM((tt, GLA_QK), jnp.bfloat16)],
        compiler_params=pltpu.CompilerParams(dimension_semantics=("arbitrary", "arbitrary"),
                                             vmem_limit_bytes=VMEM_LIMIT),
        name="gla",
    )(h, h, h, h, h, w_gla_gate.astype(jnp.bfloat16), b_gla_gate.reshape(1, GLA_QK),
      gla_norm_g.reshape(1, GLA_DV), s0, jnp.asarray(expand, jnp.bfloat16))
    return o[:, :t_], s_t.reshape(bsz, GLA_HEADS, GLA_DV, GLA_DK).transpose(0, 1, 3, 2)


COL_NQ = 2 * GLA_QK + 2 * GLA_V
COL_NKV = COL_NQ + NSA_SIZES[0]
COL_TAIL = COL_NKV + NSA_SIZES[1]
TAIL_GATE = GLA_RANK
_ORIG = np.cumsum((0,) + GLA_SIZES + NSA_SIZES)
IN_AB_PERM = np.concatenate([np.arange(_ORIG[0], _ORIG[4]), np.arange(_ORIG[5], _ORIG[7]),
                             np.arange(_ORIG[4], _ORIG[5]), np.arange(_ORIG[7], _ORIG[8])])
SUBS = Q_BLK // CMP_STRIDE


def _nsa_prep_body(nq_ref, kv0_ref, kv1_ref, kv2_ref, tail_ref, rc_ref, ru_ref, rd_ref, pool_ref,
                   rows_ref, win_ref, kk_ref, vvt_ref, qr_ref, qo_ref, g_ref, pooled_ref):
    bf16 = jnp.bfloat16
    q0 = pl.program_id(1) * Q_BLK
    kv_w = NSA_KV_HEADS * HEAD_DIM

    def rope(x):
        reps = x.shape[1] // LANE
        wide = lambda r: jnp.concatenate([r[...]] * reps, axis=1) if reps > 1 else r[...]
        half = ROPE_DIM // 2
        return (x * wide(rc_ref) + pltpu.roll(x, half, axis=1) * wide(ru_ref)
                + pltpu.roll(x, x.shape[1] - half, axis=1) * wide(rd_ref))

    kv0, kv1, kv2 = kv0_ref[0], kv1_ref[0], kv2_ref[0]
    k_sel, v_sel = rope(kv1[:, :kv_w]), kv1[:, kv_w:]
    k_win, v_win = rope(kv2[:, :kv_w]), kv2[:, kv_w:]
    rows_ref[0] = jnp.concatenate([kv0, k_sel, v_sel], axis=1)
    win_ref[0] = jnp.concatenate([k_win, v_win], axis=1)
    blk_id = lax.shift_right_logical(q0 + lax.broadcasted_iota(jnp.int32, (Q_BLK, N_SELB), 0),
                                     int(math.log2(SEL_BLK)))
    onehot = jnp.where(lax.broadcasted_iota(jnp.int32, (Q_BLK, N_SELB), 1) == blk_id, 1.0, 0.0).astype(bf16)
    q = nq_ref[0] * (HEAD_DIM ** -0.5)
    q_rot = rope(q)
    gates_t = jax.nn.sigmoid(tail_ref[0]).T
    for h in range(NSA_KV_HEADS):
        hs = slice(h * HEAD_DIM, (h + 1) * HEAD_DIM)
        kk_ref[0, h] = jnp.concatenate([k_sel[:, hs].astype(bf16), k_win[:, hs].astype(bf16), onehot], axis=1)
        vvt_ref[0, h, 0] = jnp.concatenate([v_sel[:, hs], v_win[:, hs]], axis=1).T.astype(bf16)
        gw = NSA_GROUP * HEAD_DIM
        for src, dst in ((q, qr_ref), (q_rot, qo_ref)):
            t = src[:, h * gw:(h + 1) * gw].T
            dst[0, h, 0] = jnp.concatenate([t[g * HEAD_DIM:(g + 1) * HEAD_DIM] for g in range(NSA_GROUP)],
                                           axis=1).astype(bf16)
        base = TAIL_GATE + h * NSA_GROUP * 3
        g_ref[0, h, 0] = jnp.concatenate(
            [jnp.concatenate([gates_t[base + 3 * g + c:base + 3 * g + c + 1] for g in range(NSA_GROUP)], axis=1)
             for c in range(3)], axis=0)
    kc_in, vc_in = kv0[:, :kv_w].astype(bf16), kv0[:, kv_w:].astype(bf16)
    pooled_ref[0] = jnp.concatenate([_dot(pool_ref[0], kc_in), _dot(pool_ref[1], kc_in),
                                     _dot(pool_ref[2], vc_in), _dot(pool_ref[3], vc_in)], axis=1)


def _nsa_prep(h, pos, w_cmp_pool):
    bsz, t_, _ = h.shape
    nqb = t_ // Q_BLK
    bf16 = jnp.bfloat16
    half = ROPE_DIM // 2
    inv_freq = jnp.power(ROPE_THETA, -jnp.arange(half, dtype=jnp.float32) / half)
    ang = pos.astype(jnp.float32)[:, None] * inv_freq
    cos, sin = jnp.cos(ang), jnp.sin(ang)
    rest = HEAD_DIM - ROPE_DIM
    z8, zr = jnp.zeros((t_, half), jnp.float32), jnp.zeros((t_, rest), jnp.float32)
    two = lambda a: jnp.concatenate([a, a], axis=1)
    rc = two(jnp.concatenate([cos, cos, jnp.ones((t_, rest), jnp.float32)], axis=1))
    ru = two(jnp.concatenate([z8, sin, zr], axis=1))
    rd = two(jnp.concatenate([-sin, z8, zr], axis=1))
    pool = _pool_matrices(w_cmp_pool)
    kv_w = NSA_KV_HEADS * HEAD_DIM
    col = lambda width, off: pl.BlockSpec((1, Q_BLK, width), lambda b, i: (b, i, off // width))
    rows_t = pl.BlockSpec((Q_BLK, LANE), lambda b, i: (i, 0))
    head4 = lambda r, c: pl.BlockSpec((1, NSA_KV_HEADS, 1, r, c), lambda b, i: (b, 0, i, 0, 0))
    return pl.pallas_call(
        _nsa_prep_body,
        grid=(bsz, nqb),
        in_specs=[col(NSA_SIZES[0], COL_NQ), col(2 * kv_w, COL_NKV), col(2 * kv_w, COL_NKV + 2 * kv_w),
                  col(2 * kv_w, COL_NKV + 4 * kv_w), col(LANE, COL_TAIL), rows_t, rows_t, rows_t,
                  pl.BlockSpec((4, SUBS, Q_BLK), lambda b, i: (0, 0, 0))],
        out_specs=[pl.BlockSpec((1, Q_BLK, 4 * kv_w), lambda b, i: (b, i, 0)),
                   pl.BlockSpec((1, Q_BLK, 2 * kv_w), lambda b, i: (b, i, 0)),
                   pl.BlockSpec((1, NSA_KV_HEADS, Q_BLK, KK_W), lambda b, i: (b, 0, i, 0)),
                   head4(2 * HEAD_DIM, Q_BLK), head4(HEAD_DIM, NSA_ROWS), head4(HEAD_DIM, NSA_ROWS),
                   head4(3, NSA_ROWS),
                   pl.BlockSpec((1, SUBS, 4 * kv_w), lambda b, i: (b, i, 0))],
        out_shape=[jax.ShapeDtypeStruct((bsz, t_, 4 * kv_w), jnp.float32),
                   jax.ShapeDtypeStruct((bsz, t_, 2 * kv_w), jnp.float32),
                   jax.ShapeDtypeStruct((bsz, NSA_KV_HEADS, t_, KK_W), bf16),
                   jax.ShapeDtypeStruct((bsz, NSA_KV_HEADS, nqb, 2 * HEAD_DIM, Q_BLK), bf16),
                   jax.ShapeDtypeStruct((bsz, NSA_KV_HEADS, nqb, HEAD_DIM, NSA_ROWS), bf16),
                   jax.ShapeDtypeStruct((bsz, NSA_KV_HEADS, nqb, HEAD_DIM, NSA_ROWS), bf16),
                   jax.ShapeDtypeStruct((bsz, NSA_KV_HEADS, nqb, 3, NSA_ROWS), jnp.float32),
                   jax.ShapeDtypeStruct((bsz, t_ // CMP_STRIDE, 4 * kv_w), jnp.float32)],
        compiler_params=pltpu.CompilerParams(dimension_semantics=("arbitrary", "arbitrary")),
        name="nsa_prep",
    )(h, h, h, h, h, rc, ru, rd, pool)


PAGE_GROUP = 32
DEC_KEYS = PAGE_GROUP * PAGE_SIZE
NEW_PAD = 8
KV_W = NSA_KV_HEADS * HEAD_DIM


def _dec_pool_body(pt_ref, *refs):
    page_refs, pool_ref, out_ref = refs[:PAGE_GROUP], refs[PAGE_GROUP], refs[PAGE_GROUP + 1]
    bf16 = jnp.bfloat16
    pages = [pr[0] for pr in page_refs]
    kc_t = jnp.concatenate([p[:KV_W] for p in pages], axis=0).astype(bf16)
    vc_t = jnp.concatenate([p[KV_W:] for p in pages], axis=0).astype(bf16)
    res = jnp.concatenate([_dot(kc_t, pool_ref[0]), _dot(vc_t, pool_ref[1])], axis=1)
    out_ref[0] = res.reshape(PAGE_GROUP, KV_W, 4 * SUBS)


def _page_specs(n_pages, col_blk):
    def spec(i):
        return pl.BlockSpec((1, 2 * KV_W, PAGE_SIZE),
                            lambda b, j, pt: (pt[b * n_pages + j * PAGE_GROUP + i], col_blk, 0))
    return [spec(i) for i in range(PAGE_GROUP)]


def _dec_pool(cache, page_table, pool):
    bsz, n_pages = page_table.shape
    grid_spec = pltpu.PrefetchScalarGridSpec(
        num_scalar_prefetch=1, grid=(bsz, n_pages // PAGE_GROUP),
        in_specs=_page_specs(n_pages, 0) + [pl.BlockSpec(pool.shape, lambda b, j, pt: (0, 0, 0))],
        out_specs=pl.BlockSpec((1, PAGE_GROUP, KV_W, 4 * SUBS), lambda b, j, pt: (b, j, 0, 0)))
    return pl.pallas_call(
        _dec_pool_body, grid_spec=grid_spec,
        out_shape=jax.ShapeDtypeStruct((bsz, n_pages, KV_W, 4 * SUBS), jnp.float32),
        compiler_params=pltpu.CompilerParams(dimension_semantics=("arbitrary", "arbitrary")),
        name="nsa_dec_pool",
    )(page_table.reshape(-1), *([cache] * PAGE_GROUP), pool)


def _dec_select_body(qr_ref, kct_ref, vc_ref, band_ref, oc_ref, selb_ref, *, qpos0, n_q, n_pick, n_blk):
    f32, bf16 = jnp.float32, jnp.bfloat16
    n_cmp = kct_ref.shape[3]
    rows = NSA_GROUP * n_q
    for sq, h in [(a, b) for a in range(qr_ref.shape[0]) for b in range(NSA_KV_HEADS)]:
        s_c = _dot(qr_ref[sq, h], kct_ref[sq, h])
        n_idx = lax.broadcasted_iota(jnp.int32, (rows, n_cmp), 1)
        qpos = qpos0 + (lax.broadcasted_iota(jnp.int32, (rows, n_cmp), 0) % n_q)
        cmask = (n_idx * CMP_STRIDE + (CMP_BLK - 1)) <= qpos
        s_c = jnp.where(cmask, s_c, MASKED)
        p_c = jnp.where(cmask, jnp.exp(s_c - jnp.max(s_c, axis=1, keepdims=True)), 0.0)
        p_c = p_c / jnp.maximum(jnp.sum(p_c, axis=1, keepdims=True), 1e-30)
        oc_ref[sq, h] = _dot(p_c.astype(bf16), vc_ref[sq, h])
        imp = p_c[0:n_q]
        for g in range(1, NSA_GROUP):
            imp = imp + p_c[g * n_q:(g + 1) * n_q]
        imp_s = jnp.zeros((n_q, N_SELB), f32)
        rem = imp
        for _ in range(3):
            part = rem.astype(bf16)
            imp_s = imp_s + _dot(part, band_ref[...])
            rem = rem - part.astype(f32)
        blk = lax.broadcasted_iota(jnp.int32, (n_q, N_SELB), 1)
        qpos_s = qpos0 + lax.broadcasted_iota(jnp.int32, (n_q, N_SELB), 0)
        cur = lax.shift_right_logical(qpos_s, int(math.log2(SEL_BLK)))
        valid = (blk * SEL_BLK <= qpos_s) & (blk < n_blk)
        forced = (blk == 0) | (blk == cur) | (blk == cur - 1)
        score = jnp.where(valid, imp_s + jnp.where(forced, FORCE_BONUS, 0.0), -1e30)
        picked = jnp.zeros((n_q, N_SELB), f32)
        for _ in range(n_pick):
            best = jnp.max(score, axis=1, keepdims=True)
            first = jnp.min(jnp.where(score == best, blk, N_SELB), axis=1, keepdims=True)
            hit = blk == first
            picked = jnp.where(hit, 1.0, picked)
            score = jnp.where(hit, -3e38, score)
        selb_ref[sq, h] = (jnp.where(valid, picked, 0.0) - 1.0) * (-MASKED)


def _dec_select(qr, kct, vc, n_q, qpos0, n_pick, n_blk):
    bsz = qr.shape[0]
    rows = NSA_GROUP * n_q
    n_cmp = kct.shape[3]
    ratio = SEL_BLK // CMP_STRIDE
    c_idx, j_idx = np.arange(n_cmp)[:, None], np.arange(N_SELB)[None, :]
    band = jnp.asarray(((c_idx >= ratio * j_idx - 1) & (c_idx <= ratio * j_idx + ratio - 1)), jnp.bfloat16)
    per_step = next(c for c in (4, 2, 1) if bsz % c == 0)
    per_b = lambda *tail: pl.BlockSpec((per_step, NSA_KV_HEADS) + tail, lambda b: (b, 0, 0, 0))
    return pl.pallas_call(
        functools.partial(_dec_select_body, qpos0=qpos0, n_q=n_q, n_pick=n_pick, n_blk=n_blk),
        grid=(bsz // per_step,),
        in_specs=[per_b(rows, HEAD_DIM), per_b(HEAD_DIM, n_cmp), per_b(n_cmp, HEAD_DIM),
                  pl.BlockSpec((n_cmp, N_SELB), lambda b: (0, 0))],
        out_specs=[per_b(rows, HEAD_DIM), per_b(n_q, N_SELB)],
        out_shape=[jax.ShapeDtypeStruct((bsz, NSA_KV_HEADS, rows, HEAD_DIM), jnp.float32),
                   jax.ShapeDtypeStruct((bsz, NSA_KV_HEADS, n_q, N_SELB), jnp.float32)],
        compiler_params=pltpu.CompilerParams(dimension_semantics=("arbitrary",)),
        name="nsa_dec_select",
    )(qr, kct, vc, band)


def _dec_attend_body(pt_ref, *refs, qpos0, n_q, past):
    page_refs = refs[:PAGE_GROUP]
    (qs_ref, qw_ref, knew_ref, vnew_ref, wbuf_ref, wnew_ref, oc_ref, g_ref,
     o_ref, m_ref, l_ref, acc_ref) = refs[PAGE_GROUP:]
    f32, bf16 = jnp.float32, jnp.bfloat16
    j = pl.program_id(1)
    n_rows = qs_ref.shape[1]

    @pl.when(j == 0)
    def _():
        m_ref[...] = jnp.full(m_ref.shape, MASKED, f32)
        l_ref[...] = jnp.zeros(l_ref.shape, f32)
        acc_ref[...] = jnp.zeros(acc_ref.shape, f32)

    def online(s, weigh):
        m_old = m_ref[...]
        m_new = jnp.maximum(m_old, jnp.max(s, axis=1, keepdims=True))
        alpha = jnp.exp(m_old - m_new)
        p = jnp.exp(s - m_new)
        l_ref[...] = alpha * l_ref[...] + jnp.sum(p, axis=1, keepdims=True)
        acc_ref[...] = alpha * acc_ref[...] + weigh(p.astype(bf16))
        m_ref[...] = m_new

    qs = qs_ref[0]
    pages = [pr[0] for pr in page_refs]
    keys_t = jnp.concatenate([p[:KV_W] for p in pages], axis=1).astype(bf16)
    vals_t = jnp.concatenate([p[KV_W:] for p in pages], axis=1).astype(bf16)
    blk_id = j * (DEC_KEYS // SEL_BLK) + lax.shift_right_logical(
        lax.broadcasted_iota(jnp.int32, (N_SELB, DEC_KEYS), 1), int(math.log2(SEL_BLK)))
    onehot_t = jnp.where(lax.broadcasted_iota(jnp.int32, (N_SELB, DEC_KEYS), 0) == blk_id, 1.0, 0.0).astype(bf16)
    online(_dot(qs, jnp.concatenate([keys_t, onehot_t], axis=0)), lambda p: _dot_nt(p, vals_t))

    @pl.when(j == pl.num_programs(1) - 1)
    def _():
        row_q = qpos0 + (lax.broadcasted_iota(jnp.int32, (n_rows, 1), 0) % n_q)
        qh = qw_ref[0]
        new_pos = past + lax.broadcasted_iota(jnp.int32, (n_rows, NEW_PAD), 1)
        new_ok = (new_pos <= row_q) & (new_pos < past + n_q)
        s_new = jnp.where(new_ok, _dot_nt(qh, knew_ref[0]), MASKED)
        online(s_new, lambda p: _dot(p, vnew_ref[0]))
        o_s = acc_ref[...] / l_ref[...]
        wbuf_t = wbuf_ref[0]
        wnew = wnew_ref[0]
        n_buf = wbuf_t.shape[1]
        s_b = _dot(qh, wbuf_t[:KV_W].astype(bf16))
        pos_b = (past - n_buf) + lax.broadcasted_iota(jnp.int32, (n_rows, n_buf), 1)
        s_b = jnp.where((pos_b > row_q - WINDOW) & (pos_b >= 0), s_b, MASKED)
        s_n = jnp.where(new_ok, _dot_nt(qh, wnew[:, :KV_W].astype(bf16)), MASKED)
        m_w = jnp.maximum(jnp.max(s_b, axis=1, keepdims=True), jnp.max(s_n, axis=1, keepdims=True))
        p_b, p_n = jnp.exp(s_b - m_w), jnp.exp(s_n - m_w)
        l_w = jnp.sum(p_b, axis=1, keepdims=True) + jnp.sum(p_n, axis=1, keepdims=True)
        o_w = (_dot_nt(p_b.astype(bf16), wbuf_t[KV_W:].astype(bf16))
               + _dot(p_n.astype(bf16), wnew[:, KV_W:].astype(bf16))) / l_w
        half = n_rows // NSA_KV_HEADS
        own = lambda a: jnp.concatenate([a[h * half:(h + 1) * half, h * HEAD_DIM:(h + 1) * HEAD_DIM]
                                         for h in range(NSA_KV_HEADS)], axis=0)
        g = g_ref[0]
        o_ref[0] = g[:, 0:1] * oc_ref[0] + g[:, 1:2] * own(o_s) + g[:, 2:3] * own(o_w)


def _dec_attend(cache, page_table, qs, qw, knew, vnew, wbuf, wnew, o_c, gates, n_q, qpos0):
    bsz, n_pages = page_table.shape
    n_rows = qs.shape[1]
    per_b = lambda *tail: pl.BlockSpec((1,) + tail, lambda b, j, pt: (b, 0, 0))
    grid_spec = pltpu.PrefetchScalarGridSpec(
        num_scalar_prefetch=1, grid=(bsz, n_pages // PAGE_GROUP),
        in_specs=_page_specs(n_pages, 1) + [
            per_b(n_rows, KV_W + N_SELB), per_b(n_rows, KV_W), per_b(NEW_PAD, KV_W), per_b(NEW_PAD, KV_W),
            per_b(2 * KV_W, wbuf.shape[2]), per_b(NEW_PAD, 2 * KV_W), per_b(n_rows, HEAD_DIM), per_b(n_rows, 3)],
        out_specs=per_b(n_rows, HEAD_DIM),
        scratch_shapes=[pltpu.VMEM((n_rows, 1), jnp.float32), pltpu.VMEM((n_rows, 1), jnp.float32),
                        pltpu.VMEM((n_rows, KV_W), jnp.float32)])
    return pl.pallas_call(
        functools.partial(_dec_attend_body, qpos0=qpos0, n_q=n_q, past=n_pages * PAGE_SIZE),
        grid_spec=grid_spec,
        out_shape=jax.ShapeDtypeStruct((bsz, n_rows, HEAD_DIM), jnp.float32),
        compiler_params=pltpu.CompilerParams(dimension_semantics=("arbitrary", "arbitrary")),
        name="nsa_dec_attend",
    )(page_table.reshape(-1), *([cache] * PAGE_GROUP), qs, qw, knew, vnew, wbuf, wnew, o_c, gates)


def _pool_matrices(w_cmp_pool, rows=Q_BLK):
    subs = rows // CMP_STRIDE
    sub = np.arange(rows) // CMP_STRIDE == np.arange(subs)[:, None]
    w_rep = jnp.tile(w_cmp_pool.reshape(2, 2, CMP_STRIDE), (1, 1, subs))
    return jnp.where(sub[None, None], w_rep[:, :, None, :], 0.0).reshape(4, subs, rows).astype(jnp.bfloat16)


def _nsa_decode(q_raw, q_rot, gates, rows_full, rows_win, cache, page_table, win_buf, w_cmp_pool, past):
    bsz, n_q = q_raw.shape[:2]
    bf16 = jnp.bfloat16
    n_blk = past // SEL_BLK
    assert past % DEC_KEYS == 0 and n_blk <= N_SELB and n_q <= NEW_PAD
    scale = HEAD_DIM ** -0.5
    cache2 = cache.transpose(0, 2, 3, 4, 1).reshape(cache.shape[0], 4 * KV_W, PAGE_SIZE)
    page_pool = _pool_matrices(w_cmp_pool).transpose(0, 2, 1)
    page_pool = jnp.concatenate([page_pool[0::2], page_pool[1::2]], axis=2)
    pooled = _dec_pool(cache2, page_table, page_pool)
    pooled_t = pooled.reshape(bsz, -1, NSA_KV_HEADS, HEAD_DIM, 4, SUBS).transpose(0, 4, 2, 3, 1, 5)
    pooled_t = pooled_t.reshape(bsz, 4, NSA_KV_HEADS, HEAD_DIM, -1)
    last = ((0, 0), (0, 0), (0, 0), (0, 1))
    kct = jnp.pad(pooled_t[:, 0, ..., :-1] + pooled_t[:, 1, ..., 1:], last)
    vc_p = jnp.pad(pooled_t[:, 2, ..., :-1] + pooled_t[:, 3, ..., 1:], last).transpose(0, 1, 3, 2)
    rows_of = lambda a: a.transpose(0, 2, 3, 1, 4).reshape(bsz, NSA_KV_HEADS, NSA_GROUP * n_q, a.shape[-1])
    qr = rows_of((q_raw * scale).astype(bf16))
    n_pick = min(SEL_TOPN, n_blk + 1) - 1
    o_c, selb = _dec_select(qr, kct.astype(bf16), vc_p.astype(bf16), n_q, past, n_pick, n_blk)
    qo = rows_of((q_rot * scale).astype(bf16))
    zero = jnp.zeros_like(qo[:, 0])
    qw = jnp.concatenate([jnp.concatenate([qo[:, 0], zero], -1), jnp.concatenate([zero, qo[:, 1]], -1)], axis=1)
    bias = jnp.tile(selb, (1, 1, NSA_GROUP, 1)).reshape(bsz, -1, N_SELB).astype(bf16)
    qs = jnp.concatenate([qw, bias], axis=-1)
    pad_new = lambda a: jnp.pad(a.reshape(bsz, n_q, -1), ((0, 0), (0, NEW_PAD - n_q), (0, 0)))
    knew = pad_new(rows_full[:, :, 2]).astype(bf16)
    vnew = pad_new(rows_full[:, :, 3]).astype(bf16)
    wnew = pad_new(rows_win)
    wbuf = win_buf.transpose(0, 2, 3, 4, 1).reshape(bsz, 2 * KV_W, win_buf.shape[1])
    gt = rows_of(gates).reshape(bsz, -1, 3)
    o = _dec_attend(cache2, page_table, qs, qw, knew, vnew, wbuf, wnew,
                    o_c.reshape(bsz, -1, HEAD_DIM), gt, n_q, past)
    o = o.reshape(bsz, NSA_KV_HEADS, NSA_GROUP, n_q, HEAD_DIM).transpose(0, 3, 1, 2, 4)
    return o.reshape(bsz, n_q, NSA_HEADS * HEAD_DIM)


def _ab_mixer(x, pos, w_in, w_gla_gate, b_gla_gate, gla_norm_g, w_cmp_pool, w_out,
              gla_state, nsa_cache, page_table, win_buf):
    bsz, t_, _ = x.shape
    h_in = _mm(x.reshape(bsz * t_, -1), w_in[:, IN_AB_PERM], keep_pad=True).reshape(bsz, t_, -1)
    o_a, s_a = _gla(h_in, w_gla_gate, b_gla_gate, gla_norm_g, gla_state)
    kv_w = NSA_KV_HEADS * HEAD_DIM
    if nsa_cache is None:
        rows2, win2, kk, vvt, qr, qo, gt, pooled = _nsa_prep(h_in, pos, w_cmp_pool)
        pooled = pooled.reshape(bsz, t_ // CMP_STRIDE, 4, NSA_KV_HEADS, HEAD_DIM)
        kc = pooled[:, :-1, 0] + pooled[:, 1:, 1]
        vc = pooled[:, :-1, 2] + pooled[:, 1:, 3]
        kc_p = jnp.pad(kc, ((0, 0), (0, 1), (0, 0), (0, 0))).transpose(0, 2, 1, 3).astype(jnp.bfloat16)
        vct = jnp.pad(vc, ((0, 0), (0, 1), (0, 0), (0, 0))).transpose(0, 2, 3, 1).astype(jnp.bfloat16)
        o_b = _nsa_prompt(qr, qo, gt, kc_p, vct, kk, vvt)
        rows_full = rows2.reshape(bsz, t_, 4, NSA_KV_HEADS, HEAD_DIM)
        new_win = win2[:, -min(WINDOW, t_):].reshape(bsz, -1, 2, NSA_KV_HEADS, HEAD_DIM)
    else:
        nq = h_in[..., COL_NQ:COL_NKV]
        nkv = h_in[..., COL_NKV:COL_TAIL]
        ngate = h_in[..., COL_TAIL + TAIL_GATE:COL_TAIL + TAIL_GATE + NSA_SIZES[2]]
        q_raw = nq.reshape(bsz, t_, NSA_KV_HEADS, NSA_GROUP, HEAD_DIM)
        q_rot = _partial_rope(q_raw, pos)
        kv = nkv.reshape(bsz, t_, 6, NSA_KV_HEADS, HEAD_DIM)
        k_sel = _partial_rope(kv[:, :, 2], pos)
        k_win = _partial_rope(kv[:, :, 4], pos)
        rows_full = jnp.stack([kv[:, :, 0], kv[:, :, 1], k_sel, kv[:, :, 3]], axis=2)
        rows_win = jnp.stack([k_win, kv[:, :, 5]], axis=2)
        gates = jax.nn.sigmoid(ngate).reshape(bsz, t_, NSA_KV_HEADS, NSA_GROUP, 3)
        past_len = page_table.shape[1] * PAGE_SIZE
        o_b = _nsa_decode(q_raw, q_rot, gates, rows_full, rows_win, nsa_cache, page_table, win_buf,
                          w_cmp_pool, past_len)
        w_buf = win_buf.shape[1]
        kw = jnp.concatenate([win_buf, rows_win], axis=1)
        new_win = kw[:, -w_buf:]
    y = _mm_pair(o_a.reshape(bsz * t_, -1), o_b.reshape(bsz * t_, -1), w_out).reshape(bsz, t_, -1)
    return y, s_a, rows_full, new_win


CONV_HALO = 32
CONV_LEAD = CONV_HALO - (CONV_W - 1)


def _conv_body(x_ref, buf0_ref, w1_ref, b1_ref, wdw_ref, bdw_ref, g_ref, b_ref, w2_ref, b2_ref,
               o_ref, tail_ref, ext_ref, z_ref, *, t_last):
    bf16 = jnp.bfloat16
    tt = x_ref.shape[1]
    i = pl.program_id(1)

    @pl.when(i == 0)
    def _():
        ext_ref[0:CONV_HALO, :] = buf0_ref[0]
        ext_ref[CONV_HALO + tt:CONV_HALO + tt + SUBLANES, :] = jnp.zeros((SUBLANES, D_CONV), jnp.float32)

    h = _dot(x_ref[0].astype(bf16), w1_ref[...]) + b1_ref[...]
    ext_ref[CONV_HALO:CONV_HALO + tt, :] = h[:, :D_CONV] * jax.nn.sigmoid(h[:, D_CONV:])
    c = jnp.zeros((tt, D_CONV), jnp.float32) + bdw_ref[...]
    for r in range(SUBLANES):
        z = None
        for a in range(CONV_HALO // SUBLANES + 1):
            k = SUBLANES * a + r - CONV_LEAD
            if 0 <= k < CONV_W:
                term = ext_ref[SUBLANES * a:SUBLANES * a + tt + SUBLANES, :] * wdw_ref[k:k + 1, :]
                z = term if z is None else z + term
        if r == 0:
            c = c + z[:tt]
        else:
            z_ref[...] = z
            c = c + z_ref[pl.ds(r, tt), :]
    c = _ln_rows(c, g_ref[...], b_ref[...])
    c = c * jax.nn.sigmoid(c)
    o_ref[0] = _dot(c.astype(bf16), w2_ref[...]) + b2_ref[...]
    tail_ref[0] = ext_ref[t_last:t_last + CONV_HALO, :]
    ext_ref[0:CONV_HALO, :] = ext_ref[tt:tt + CONV_HALO, :]


def _conv_module(x, conv_buf, w_pw1, b_pw1, w_dw, b_dw, ln_g, ln_b, w_pw2, b_pw2):
    bsz, t_, d = x.shape
    bf16 = jnp.bfloat16
    tp = -(-t_ // 8) * 8
    tt = min(tp, 256)
    n_t = tp // tt
    if tp != t_:
        x = jnp.pad(x, ((0, 0), (0, tp - t_), (0, 0)))
    if conv_buf is None:
        buf0 = jnp.zeros((bsz, CONV_HALO, D_CONV), jnp.float32)
    else:
        buf0 = jnp.pad(conv_buf, ((0, 0), (CONV_LEAD, 0), (0, 0)))
    fixed = lambda shape: pl.BlockSpec(shape, lambda b, i: (0,) * len(shape))
    per_b = pl.BlockSpec((1, CONV_HALO, D_CONV), lambda b, i: (b, 0, 0))
    out, tail = pl.pallas_call(
        functools.partial(_conv_body, t_last=t_ - (n_t - 1) * tt),
        grid=(bsz, n_t),
        in_specs=[pl.BlockSpec((1, tt, d), lambda b, i: (b, i, 0)), per_b,
                  fixed((d, 2 * D_CONV)), fixed((1, 2 * D_CONV)), fixed((CONV_HALO, D_CONV)), fixed((1, D_CONV)),
                  fixed((1, D_CONV)), fixed((1, D_CONV)), fixed((D_CONV, d)), fixed((1, d))],
        out_specs=[pl.BlockSpec((1, tt, d), lambda b, i: (b, i, 0)), per_b],
        out_shape=[jax.ShapeDtypeStruct((bsz, tp, d), jnp.float32),
                   jax.ShapeDtypeStruct((bsz, CONV_HALO, D_CONV), jnp.float32)],
        scratch_shapes=[pltpu.VMEM((CONV_HALO + tt + SUBLANES, D_CONV), jnp.float32),
                        pltpu.VMEM((tt + SUBLANES, D_CONV), jnp.float32)],
        compiler_params=pltpu.CompilerParams(dimension_semantics=("arbitrary", "arbitrary"),
                                             vmem_limit_bytes=VMEM_LIMIT),
        name="conv_module",
    )(x, buf0, w_pw1.astype(bf16), b_pw1.reshape(1, -1), jnp.pad(w_dw, ((0, CONV_HALO - CONV_W), (0, 0))),
      b_dw.reshape(1, -1), ln_g.reshape(1, -1), ln_b.reshape(1, -1), w_pw2.astype(bf16), b_pw2.reshape(1, -1))
    return out[:, :t_], tail[:, CONV_LEAD:]


PACK_W = 256
SC_WINDOW = 128
SC_TILES = 32


def _pack_rows(y):
    out = []
    for h in range(2):
        lo = lax.bitcast_convert_type(y[:, 2 * h * PACK_W:(2 * h + 1) * PACK_W].astype(jnp.bfloat16)
                                      .astype(jnp.float32), jnp.uint32)
        hi = lax.bitcast_convert_type(y[:, (2 * h + 1) * PACK_W:(2 * h + 2) * PACK_W].astype(jnp.bfloat16)
                                      .astype(jnp.float32), jnp.uint32)
        out.append(lax.bitcast_convert_type((lo >> 16) | hi, jnp.int32))
    return out


def _unpack_words(w):
    u = lax.bitcast_convert_type(w, jnp.uint32)
    lo = lax.bitcast_convert_type(u << 16, jnp.float32)
    hi = lax.bitcast_convert_type(u & jnp.uint32(0xFFFF0000), jnp.float32)
    return lo, hi


def _gather_rows(src, idx):
    n = idx.shape[0]
    if n % (SC_WINDOW * SC_TILES) != 0:
        return jnp.take(src, idx, axis=0)
    mesh = plsc.VectorSubcoreMesh(core_axis_name="core", subcore_axis_name="subcore")

    @pl.kernel(out_type=jax.ShapeDtypeStruct((n, src.shape[1]), src.dtype), mesh=mesh)
    def gather_kernel(src_hbm, idx_hbm, out_hbm):
        def step(idx_vmem, out_vmem):
            pltpu.sync_copy(src_hbm.at[idx_vmem.at[0]], out_vmem)

        pltpu.emit_pipeline(
            step, grid=(n // SC_WINDOW,),
            in_specs=[pl.BlockSpec((1, SC_WINDOW), index_map=lambda i: (0, i))],
            out_specs=[pl.BlockSpec((SC_WINDOW, src.shape[1]), index_map=lambda i: (i, 0))],
            core_axis_name=("core", "subcore"),
            dimension_semantics=(pltpu.PARALLEL,),
        )(idx_hbm, out_hbm)

    return gather_kernel(src, idx.reshape(1, n))


def _scatter_rows(src, idx, n_out):
    n = idx.shape[0]
    m = src.shape[0] // 2
    reps = n // (2 * m)
    if n % (SC_WINDOW * SC_TILES) != 0 or m % SC_WINDOW != 0:
        rows = jnp.arange(n, dtype=jnp.int32)
        src_row = (rows // (reps * m)) * m + rows % m
        return jnp.zeros((n_out, src.shape[1]), src.dtype).at[idx].set(jnp.take(src, src_row, axis=0))
    tiles = m // SC_WINDOW
    mesh = plsc.VectorSubcoreMesh(core_axis_name="core", subcore_axis_name="subcore")

    @pl.kernel(out_type=jax.ShapeDtypeStruct((n_out, src.shape[1]), src.dtype), mesh=mesh, scratch_types=[])
    def scatter_kernel(src_hbm, idx_hbm, out_hbm):
        def step(src_vmem, idx_vmem):
            pltpu.sync_copy(src_vmem, out_hbm.at[idx_vmem.at[0]])

        pltpu.emit_pipeline(
            step, grid=(n // SC_WINDOW,),
            in_specs=[pl.BlockSpec((SC_WINDOW, src.shape[1]),
                                   index_map=lambda i: ((i // (reps * tiles)) * tiles + i % tiles, 0)),
                      pl.BlockSpec((1, SC_WINDOW), index_map=lambda i: (0, i))],
            out_specs=[],
            core_axis_name=("core", "subcore"),
            dimension_semantics=(pltpu.PARALLEL,),
        )(src_hbm, idx_hbm)

    return scatter_kernel(src, idx.reshape(1, n))


PER_GROUP = N_EXPERTS // N_GROUPS
PICKED = -3e38


def _ln_rows(v, g, b):
    mu = jnp.mean(v, axis=-1, keepdims=True)
    c = v - mu
    var = jnp.mean(c * c, axis=-1, keepdims=True)
    return c * lax.rsqrt(var + LN_EPS) * g + b


def _first_max(v, ids, axes, sentinel):
    best = v
    for a in axes:
        best = jnp.max(best, axis=a, keepdims=True)
    first = jnp.where(v == best, ids, sentinel)
    for a in axes:
        first = jnp.min(first, axis=a, keepdims=True)
    return best, first


def _sum_axes(v, axes):
    for a in axes:
        v = jnp.sum(v, axis=a, keepdims=True)
    return v


def _moe_pre_body(x_ref, mix_ref, g_ref, b_ref, wr_ref, br_ref, wgu_ref, wdn_ref,
                  x1_ref, xp_ref, sh_ref, eidx_ref, gate_ref, rank_ref, cnt_ref, run_ref):
    f32, bf16 = jnp.float32, jnp.bfloat16
    tm = x_ref.shape[0]

    @pl.when(pl.program_id(0) == 0)
    def _():
        run_ref[...] = jnp.zeros(run_ref.shape, f32)

    x1 = _ln_rows(ALPHA * x_ref[...] + mix_ref[...], g_ref[...], b_ref[...])
    x1_ref[...] = x1
    x1b = x1.astype(bf16)
    xp_ref[0], xp_ref[1] = _pack_rows(x1)

    h = _dot(x1b, wgu_ref[...])
    d_sh = h.shape[1] // 2
    act = (jax.nn.silu(h[:, :d_sh]) * h[:, d_sh:]).astype(bf16)
    sh_ref[...] = _dot(act, wdn_ref[...])

    s = jax.nn.sigmoid(_dot_nt(wr_ref[...], x1b)).reshape(N_GROUPS, PER_GROUP, tm)
    sb = s + br_ref[...].reshape(N_GROUPS, PER_GROUP, 1)
    shape3 = (N_GROUPS, PER_GROUP, tm)
    pid = lax.broadcasted_iota(jnp.int32, shape3, 1)
    gid = lax.broadcasted_iota(jnp.int32, (N_GROUPS, 1, tm), 0)
    eid = lax.broadcasted_iota(jnp.int32, shape3, 0) * PER_GROUP + pid
    top1, i1 = _first_max(sb, pid, (1,), PER_GROUP)
    top2 = jnp.max(jnp.where(pid == i1, PICKED, sb), axis=1, keepdims=True)
    gscore = top1 + top2
    gsel = jnp.zeros((N_GROUPS, 1, tm), f32)
    for _ in range(TOPK_GROUPS):
        _, first = _first_max(gscore, gid, (0,), N_GROUPS)
        hit = gid == first
        gsel = jnp.where(hit, 1.0, gsel)
        gscore = jnp.where(hit, PICKED, gscore)
    cand = jnp.where(gsel > 0.0, sb, -1e30)
    firsts, gates = [], []
    picked = jnp.zeros(shape3, f32)
    for _ in range(TOP_K):
        _, first = _first_max(cand, eid, (0, 1), N_EXPERTS)
        hit = eid == first
        firsts.append(first)
        gates.append(_sum_axes(jnp.where(hit, s, 0.0), (0, 1)))
        picked = jnp.where(hit, 1.0, picked)
        cand = jnp.where(hit, PICKED, cand)
    gsum = gates[0]
    for gk in gates[1:]:
        gsum = gsum + gk
    earlier = (lax.broadcasted_iota(jnp.int32, (tm, tm), 0) < lax.broadcasted_iota(jnp.int32, (tm, tm), 1))
    picked2 = picked.reshape(N_EXPERTS, tm)
    rank = run_ref[...] + _dot(picked2.astype(bf16), jnp.where(earlier, 1.0, 0.0).astype(bf16))
    run_new = run_ref[...] + jnp.sum(picked2, axis=1, keepdims=True)
    run_ref[...] = run_new
    cnt_ref[...] = jnp.broadcast_to(run_new, cnt_ref.shape)
    rank3 = rank.reshape(shape3)
    for k in range(TOP_K):
        hit = eid == firsts[k]
        eidx_ref[k:k + 1, :] = firsts[k].reshape(1, tm)
        gate_ref[k:k + 1, :] = (gates[k] / gsum * ROUTE_SCALE).reshape(1, tm)
        rank_ref[k:k + 1, :] = _sum_axes(jnp.where(hit, rank3, 0.0), (0, 1)).reshape(1, tm).astype(jnp.int32)


def _moe_pre(x, mix, g, b, w_router, b_router, w_sh_gu, w_sh_down):
    m, d = x.shape
    bf16 = jnp.bfloat16
    tm = min(m, 512)
    row = lambda i: (i, 0)
    col = lambda i: (0, i)
    fixed = lambda i: (0, 0)
    d_sh2 = w_sh_gu.shape[1]
    return pl.pallas_call(
        _moe_pre_body,
        grid=(m // tm,),
        in_specs=[pl.BlockSpec((tm, d), row), pl.BlockSpec((tm, d), row),
                  pl.BlockSpec((1, d), fixed), pl.BlockSpec((1, d), fixed),
                  pl.BlockSpec((N_EXPERTS, d), fixed), pl.BlockSpec((N_EXPERTS, 1), fixed),
                  pl.BlockSpec((d, d_sh2), fixed), pl.BlockSpec((d_sh2 // 2, d), fixed)],
        out_specs=[pl.BlockSpec((tm, d), row), pl.BlockSpec((2, tm, PACK_W), lambda i: (0, i, 0)),
                   pl.BlockSpec((tm, d), row),
                   pl.BlockSpec((TOP_K, tm), col), pl.BlockSpec((TOP_K, tm), col), pl.BlockSpec((TOP_K, tm), col),
                   pl.BlockSpec((N_EXPERTS, LANE), fixed)],
        out_shape=[jax.ShapeDtypeStruct((m, d), jnp.float32), jax.ShapeDtypeStruct((2, m, PACK_W), jnp.int32),
                   jax.ShapeDtypeStruct((m, d), jnp.float32),
                   jax.ShapeDtypeStruct((TOP_K, m), jnp.int32), jax.ShapeDtypeStruct((TOP_K, m), jnp.float32),
                   jax.ShapeDtypeStruct((TOP_K, m), jnp.int32),
                   jax.ShapeDtypeStruct((N_EXPERTS, LANE), jnp.float32)],
        scratch_shapes=[pltpu.VMEM((N_EXPERTS, 1), jnp.float32)],
        compiler_params=pltpu.CompilerParams(dimension_semantics=("arbitrary",),
                                             vmem_limit_bytes=VMEM_LIMIT),
        name="moe_pre",
    )(x, mix, g.reshape(1, d), b.reshape(1, d), w_router.T.astype(bf16), b_router.reshape(N_EXPERTS, 1),
      w_sh_gu.astype(bf16), w_sh_down.astype(bf16))


def _moe_expert_body(exp_ref, first_ref, rows_ref, xs_ref, wgu_ref, wdn_ref, y_ref, wgu_bf, wdn_bf):
    i = pl.program_id(0)
    bf16 = jnp.bfloat16

    @pl.when(first_ref[i] == 1)
    def _():
        wgu_bf[...] = wgu_ref[0, 0].astype(bf16)
        wdn_bf[...] = wdn_ref[0, 0].astype(bf16)

    @pl.when(rows_ref[i] > 0)
    def _():
        live = lax.broadcasted_iota(jnp.int32, (xs_ref.shape[1], 1), 0) < rows_ref[i]
        h = None
        for hw in range(2):
            for q, xq in enumerate(_unpack_words(xs_ref[hw])):
                r0 = (2 * hw + q) * PACK_W
                part = _dot(jnp.where(live, xq, 0.0).astype(bf16), wgu_bf[r0:r0 + PACK_W, :])
                h = part if h is None else h + part
        d_e = h.shape[1] // 2
        act = (jax.nn.silu(h[:, :d_e]) * h[:, d_e:]).astype(bf16)
        y_ref[0], y_ref[1] = _pack_rows(_dot(act, wdn_bf[...]))

    @pl.when(rows_ref[i] == 0)
    def _():
        y_ref[...] = jnp.zeros(y_ref.shape, y_ref.dtype)


def _moe_experts(xs, blk_exp, blk_first, blk_rows, w_exp_gu, w_exp_down, layer, bm):
    n_slots = xs.shape[1]
    d = w_exp_gu.shape[2]
    n_blk = n_slots // bm
    d_e2 = w_exp_gu.shape[3]
    words = lambda i, e, f, a: (0, i, 0)
    grid_spec = pltpu.PrefetchScalarGridSpec(
        num_scalar_prefetch=3,
        grid=(n_blk,),
        in_specs=[pl.BlockSpec((2, bm, PACK_W), words),
                  pl.BlockSpec((1, 1, d, d_e2), lambda i, e, f, a: (layer, e[i], 0, 0)),
                  pl.BlockSpec((1, 1, d_e2 // 2, d), lambda i, e, f, a: (layer, e[i], 0, 0))],
        out_specs=pl.BlockSpec((2, bm, PACK_W), words),
        scratch_shapes=[pltpu.VMEM((d, d_e2), jnp.bfloat16), pltpu.VMEM((d_e2 // 2, d), jnp.bfloat16)])
    return pl.pallas_call(
        _moe_expert_body,
        grid_spec=grid_spec,
        out_shape=jax.ShapeDtypeStruct((2, n_slots, PACK_W), jnp.int32),
        compiler_params=pltpu.CompilerParams(dimension_semantics=("arbitrary",),
                                             vmem_limit_bytes=VMEM_LIMIT),
        name="moe_experts",
    )(blk_exp, blk_first, blk_rows, xs, w_exp_gu, w_exp_down)


def _combine_ln_body(x_ref, yg_ref, gt_ref, sh_ref, g_ref, b_ref, o_ref):
    gt = gt_ref[...]
    parts = []
    for hw in range(2):
        lo_acc = hi_acc = None
        for k in range(TOP_K):
            lo, hi = _unpack_words(yg_ref[hw, k])
            gk = gt[:, k:k + 1]
            lo_acc = lo * gk if lo_acc is None else lo_acc + lo * gk
            hi_acc = hi * gk if hi_acc is None else hi_acc + hi * gk
        parts += [lo_acc, hi_acc]
    routed = jnp.concatenate(parts, axis=1)
    o_ref[...] = _ln_rows(ALPHA * x_ref[...] + (routed + sh_ref[...]), g_ref[...], b_ref[...])


def _combine_ln(x, yg, gate_t, shared, g, b):
    m, d = x.shape
    tm = min(m, 256)
    row = lambda i: (i, 0)
    fixed = lambda i: (0, 0)
    return pl.pallas_call(
        _combine_ln_body,
        grid=(m // tm,),
        in_specs=[pl.BlockSpec((tm, d), row), pl.BlockSpec((2, TOP_K, tm, PACK_W), lambda i: (0, 0, i, 0)),
                  pl.BlockSpec((tm, TOP_K), row), pl.BlockSpec((tm, d), row),
                  pl.BlockSpec((1, d), fixed), pl.BlockSpec((1, d), fixed)],
        out_specs=pl.BlockSpec((tm, d), row),
        out_shape=jax.ShapeDtypeStruct((m, d), jnp.float32),
        compiler_params=pltpu.CompilerParams(dimension_semantics=("arbitrary",)),
        name="combine_ln",
    )(x, yg, gate_t, shared, g.reshape(1, d), b.reshape(1, d))


def _moe_layer(x, mix, ln1_g, ln1_b, ln2_g, ln2_b, w_router, b_router, w_exp_gu, w_exp_down, layer,
               w_sh_gu, w_sh_down):
    m, d = x.shape
    x1, xp, shared, eidx, gate8, rank8, counts = _moe_pre(x, mix, ln1_g, ln1_b, w_router, b_router,
                                                           w_sh_gu, w_sh_down)
    bm = 512 if m * TOP_K >= 512 * N_EXPERTS else MOE_BLK
    n_blk = (m * TOP_K) // bm + N_EXPERTS
    counts = counts[:, 0].astype(jnp.int32)
    padded = (counts + bm - 1) // bm * bm
    pad_end = jnp.cumsum(padded)
    pad_start = pad_end - padded
    start_of = jnp.sum(jnp.where(eidx[:, :, None] == jnp.arange(N_EXPERTS), pad_start, 0), axis=-1)
    dest = (start_of + rank8).reshape(-1)
    blk_start = jnp.arange(n_blk, dtype=jnp.int32) * bm
    blk_exp = jnp.minimum(jnp.sum(pad_end[None, :] <= blk_start[:, None], axis=1), N_EXPERTS - 1).astype(jnp.int32)
    blk_rows = jnp.clip(counts[blk_exp] - (blk_start - pad_start[blk_exp]), 0, bm).astype(jnp.int32)
    blk_first = jnp.concatenate([jnp.ones((1,), jnp.int32), (blk_exp[1:] != blk_exp[:-1]).astype(jnp.int32)])
    n_slots = n_blk * bm
    xs = _scatter_rows(xp.reshape(2 * m, PACK_W), jnp.concatenate([dest, dest + n_slots]), 2 * n_slots)
    y = _moe_experts(xs.reshape(2, n_slots, PACK_W), blk_exp, blk_first, blk_rows, w_exp_gu, w_exp_down, layer, bm)
    yg = _gather_rows(y.reshape(2 * n_slots, PACK_W), jnp.concatenate([dest, dest + n_slots]))
    return _combine_ln(x1, yg.reshape(2, TOP_K, m, PACK_W), gate8.T, shared, ln2_g, ln2_b)


def _trunk(x, pos, gla_state, nsa_cache, page_table, win_buf, conv_buf,
           w_in_ab, w_gla_gate, b_gla_gate, gla_norm_g, w_cmp_pool, w_out_ab,
           w_pw1, b_pw1, w_dw, b_dw, conv_ln_g, conv_ln_b, w_pw2, b_pw2,
           ln_g, ln_b, w_router, b_router, w_exp_gu, w_exp_down, w_sh_gu, w_sh_down):
    new_gla, new_rows, new_win, new_conv = [], [], [], []
    for layer in range(DEPTH):
        i = layer // 2
        if layer % 2 == 0:
            mix, s_a, rows, win = _ab_mixer(
                x, pos, w_in_ab[i], w_gla_gate[i], b_gla_gate[i], gla_norm_g[i], w_cmp_pool[i], w_out_ab[i],
                None if gla_state is None else gla_state[i],
                None if nsa_cache is None else nsa_cache[i], page_table,
                None if win_buf is None else win_buf[i])
            new_gla.append(s_a)
            new_rows.append(rows)
            new_win.append(win)
        else:
            mix, cb = _conv_module(x, None if conv_buf is None else conv_buf[i], w_pw1[i], b_pw1[i],
                                   w_dw[i], b_dw[i], conv_ln_g[i], conv_ln_b[i], w_pw2[i], b_pw2[i])
            new_conv.append(cb)
        bsz, t_, d = x.shape
        x = _moe_layer(x.reshape(-1, d), mix.reshape(-1, d), ln_g[layer, 0], ln_b[layer, 0],
                       ln_g[layer, 1], ln_b[layer, 1], w_router[layer], b_router[layer],
                       w_exp_gu, w_exp_down, layer, w_sh_gu[layer], w_sh_down[layer]).reshape(bsz, t_, d)
    return x, jnp.stack(new_gla), jnp.stack(new_rows), jnp.stack(new_win), jnp.stack(new_conv)


def kernel(x_prompt, x_sample, state_gla, cache_nsa_kv, state_nsa_win, state_conv, page_table,
           w_in_ab, w_gla_gate, b_gla_gate, gla_norm_g, w_cmp_pool, w_out_ab,
           w_pw1, b_pw1, w_dw, b_dw, conv_ln_g, conv_ln_b, w_pw2, b_pw2,
           ln_g, ln_b, w_router, b_router, w_exp_gu, w_exp_down, w_sh_gu, w_sh_down):
    weights = (w_in_ab, w_gla_gate, b_gla_gate, gla_norm_g, w_cmp_pool, w_out_ab,
               w_pw1, b_pw1, w_dw, b_dw, conv_ln_g, conv_ln_b, w_pw2, b_pw2,
               ln_g, ln_b, w_router, b_router, w_exp_gu, w_exp_down, w_sh_gu, w_sh_down)
    past_len = page_table.shape[1] * PAGE_SIZE
    pos_p = jnp.arange(x_prompt.shape[1])
    pos_s = past_len + jnp.arange(x_sample.shape[1])
    y_prompt, gla_p, rows_p, win_p, conv_p = _trunk(x_prompt, pos_p, None, None, None, None, None, *weights)
    y_sample, gla_s, rows_s, win_s, conv_s = _trunk(x_sample, pos_s, state_gla, cache_nsa_kv, page_table,
                                                    state_nsa_win, state_conv, *weights)
    return (y_prompt, y_sample, gla_p, gla_s, rows_p, rows_s, win_p, win_s, conv_p, conv_s)
```

```python
import functools
import math

import jax
import jax.numpy as jnp
import numpy as np
from jax import lax
from jax.experimental import pallas as pl
from jax.experimental.pallas import tpu as pltpu
from jax.experimental.pallas import tpu_sc as plsc

D_MODEL = 1024
DEPTH = 2
PAGE_SIZE = 128

GLA_HEADS = 4
GLA_DV = D_MODEL // 2 // GLA_HEADS
GLA_DK = GLA_DV // 2
GLA_RANK = 16
GLA_TAU = 16.0

NSA_HEADS = 8
NSA_KV_HEADS = 2
NSA_GROUP = NSA_HEADS // NSA_KV_HEADS
HEAD_DIM = D_MODEL // 2 // NSA_HEADS
CMP_BLK = 32
CMP_STRIDE = 16
SEL_BLK = 64
SEL_TOPN = 16
WINDOW = 512
Q_BLK = 128
FORCE_BONUS = 100.0
ROPE_DIM = HEAD_DIM // 4
ROPE_THETA = 500000.0

GLA_SIZES = (GLA_HEADS * GLA_DK, GLA_HEADS * GLA_DK, GLA_HEADS * GLA_DV, GLA_HEADS * GLA_DV, GLA_RANK)
NSA_SIZES = (NSA_HEADS * HEAD_DIM, 6 * NSA_KV_HEADS * HEAD_DIM, 3 * NSA_HEADS)

CONV_W = 31
D_CONV = D_MODEL

N_EXPERTS = 64
N_GROUPS = 8
TOPK_GROUPS = 4
TOP_K = 8
D_EXPERT = 256
ROUTE_SCALE = 2.5
MOE_BLK = 128

ALPHA = (2 * DEPTH) ** 0.25
LN_EPS = 1e-5

LANE = 128
SUBLANES = 8
V7X_VMEM_BYTES = 64 * 1024 * 1024
VMEM_LIMIT = V7X_VMEM_BYTES * 3 // 4


def _dot(a, b):
    return jnp.dot(a, b, preferred_element_type=jnp.float32)


def _dot_nt(a, b):
    return lax.dot_general(a, b, (((1,), (1,)), ((), ())), preferred_element_type=jnp.float32)


def _mm_body(x_ref, w_ref, o_ref):
    o_ref[...] = _dot(x_ref[...].astype(jnp.bfloat16), w_ref[...].astype(jnp.bfloat16))


def _mm(x, w, keep_pad=False):
    m, k = x.shape
    n = w.shape[1]
    n_pad = -(-n // LANE) * LANE
    w = w.astype(jnp.bfloat16)
    if n_pad != n:
        w = jnp.pad(w, ((0, 0), (0, n_pad - n)))
    tm = min(m, 512)
    out = pl.pallas_call(
        _mm_body,
        grid=(m // tm,),
        in_specs=[pl.BlockSpec((tm, k), lambda i: (i, 0)),
                  pl.BlockSpec((k, n_pad), lambda i: (0, 0))],
        out_specs=pl.BlockSpec((tm, n_pad), lambda i: (i, 0)),
        out_shape=jax.ShapeDtypeStruct((m, n_pad), jnp.float32),
        compiler_params=pltpu.CompilerParams(dimension_semantics=("arbitrary",),
                                             vmem_limit_bytes=VMEM_LIMIT),
        name="mm",
    )(x, w)
    return out if keep_pad or n_pad == n else out[:, :n]


def _mm_pair_body(a_ref, b_ref, w_ref, o_ref):
    ka = a_ref.shape[1]
    o_ref[...] = (_dot(a_ref[...].astype(jnp.bfloat16), w_ref[0:ka, :])
                  + _dot(b_ref[...].astype(jnp.bfloat16), w_ref[ka:, :]))


def _mm_pair(a, b, w):
    m, ka = a.shape
    kb = b.shape[1]
    n = w.shape[1]
    tm = min(m, 512)
    return pl.pallas_call(
        _mm_pair_body,
        grid=(m // tm,),
        in_specs=[pl.BlockSpec((tm, ka), lambda i: (i, 0)), pl.BlockSpec((tm, kb), lambda i: (i, 0)),
                  pl.BlockSpec((ka + kb, n), lambda i: (0, 0))],
        out_specs=pl.BlockSpec((tm, n), lambda i: (i, 0)),
        out_shape=jax.ShapeDtypeStruct((m, n), jnp.float32),
        compiler_params=pltpu.CompilerParams(dimension_semantics=("arbitrary",)),
        name="mm_pair",
    )(a, b, w.astype(jnp.bfloat16))


def _partial_rope(x, pos):
    half = ROPE_DIM // 2
    inv_freq = jnp.power(ROPE_THETA, -jnp.arange(half, dtype=jnp.float32) / half)
    ang = pos.astype(jnp.float32)[:, None] * inv_freq
    ang = ang.reshape(ang.shape[0], *([1] * (x.ndim - 3)), half)
    cos, sin = jnp.cos(ang), jnp.sin(ang)
    x1 = x[..., :half]
    x2 = x[..., half:ROPE_DIM]
    rot = jnp.concatenate([x1 * cos - x2 * sin, x2 * cos + x1 * sin], -1)
    return jnp.concatenate([rot, x[..., ROPE_DIM:]], -1)


NSA_ROWS = NSA_GROUP * Q_BLK
SEL_KT = 2048
N_SELB = 128
MASKED = -1e9
WIN_KEYS = WINDOW + Q_BLK
KK_W = 2 * HEAD_DIM + N_SELB


def _nsa_prompt_body(qr_ref, qo_ref, kc_ref, vct_ref, kk_ref, vvt_ref, g_ref, o_ref,
                     imp_ref, m_ref, l_ref, acc_ref, oct_ref, selb_ref):
    f32, bf16 = jnp.float32, jnp.bfloat16
    qb = pl.program_id(2)
    q0 = qb * Q_BLK
    qr_t = qr_ref[0, 0, 0]
    qo_t = qo_ref[0, 0, 0]
    n_cmp = kc_ref.shape[2]

    ratio = SEL_BLK // CMP_STRIDE
    chunk = min(Q_BLK, n_cmp)
    n_chunks = n_cmp // chunk

    def compressed_and_select(n_act):
        nc = n_act * chunk
        nb = nc // ratio
        s_c = _dot(kc_ref[0, 0, 0:nc, :], qr_t)
        n_idx = lax.broadcasted_iota(jnp.int32, (nc, NSA_ROWS), 0)
        qpos_c = q0 + (lax.broadcasted_iota(jnp.int32, (nc, NSA_ROWS), 1) & (Q_BLK - 1))
        cmask = (n_idx * CMP_STRIDE + (CMP_BLK - 1)) <= qpos_c
        s_c = jnp.where(cmask, s_c, MASKED)
        m_c = jnp.max(s_c, axis=0, keepdims=True)
        p_c = jnp.where(cmask, jnp.exp(s_c - m_c), 0.0)
        p_c = p_c / jnp.maximum(jnp.sum(p_c, axis=0, keepdims=True), 1e-30)
        oct_ref[...] = _dot(vct_ref[0, 0, :, 0:nc], p_c.astype(bf16))
        imp = (p_c[:, 0:Q_BLK] + p_c[:, Q_BLK:2 * Q_BLK]) + p_c[:, 2 * Q_BLK:3 * Q_BLK] + p_c[:, 3 * Q_BLK:]
        imp_ref[0:8, :] = jnp.zeros((8, Q_BLK), f32)
        imp_ref[8:8 + nc, :] = imp
        imp_s = imp_ref[pl.ds(7, nb, stride=ratio), :]
        for r in range(ratio):
            imp_s = imp_s + imp_ref[pl.ds(8 + r, nb, stride=ratio), :]
        blk = lax.broadcasted_iota(jnp.int32, (nb, Q_BLK), 0)
        qpos_s = q0 + lax.broadcasted_iota(jnp.int32, (nb, Q_BLK), 1)
        cur = lax.shift_right_logical(qpos_s, int(math.log2(SEL_BLK)))
        valid = blk * SEL_BLK <= qpos_s
        forced = (blk == 0) | (blk == cur) | (blk == cur - 1)
        score = jnp.where(valid, imp_s + jnp.where(forced, FORCE_BONUS, 0.0), -1e30)
        picked = jnp.zeros((nb, Q_BLK), f32)
        for _ in range(SEL_TOPN):
            best = jnp.max(score, axis=0, keepdims=True)
            first = jnp.min(jnp.where(score == best, blk, nb), axis=0, keepdims=True)
            hit = blk == first
            picked = jnp.where(hit, 1.0, picked)
            score = jnp.where(hit, -3e38, score)
        sel = jnp.where(valid, picked, 0.0)
        if nb < N_SELB:
            sel = jnp.concatenate([sel, jnp.zeros((N_SELB - nb, Q_BLK), f32)], axis=0)
        sel = ((sel - 1.0) * (-MASKED)).astype(bf16)
        selb_ref[...] = jnp.concatenate([sel] * NSA_GROUP, axis=1)

    need = jnp.minimum((q0 + Q_BLK - CMP_BLK) // (CMP_STRIDE * chunk) + 1, n_chunks)
    for n_act in range(1, n_chunks + 1):
        pl.when(need == n_act)(functools.partial(compressed_and_select, n_act))
    o_ct = oct_ref[...]
    selb_t = selb_ref[...]

    zeros_q = jnp.zeros((HEAD_DIM, NSA_ROWS), bf16)
    q_sel = jnp.concatenate([qo_t, zeros_q, selb_t], axis=0)
    q_win = jnp.concatenate([zeros_q, qo_t, jnp.zeros((N_SELB, NSA_ROWS), bf16)], axis=0)
    qpos_r = q0 + (lax.broadcasted_iota(jnp.int32, (1, NSA_ROWS), 1) & (Q_BLK - 1))

    def v_tiles(first, count):
        return jnp.concatenate([vvt_ref[0, 0, first + j] for j in range(count)], axis=1)

    m_ref[...] = jnp.full(m_ref.shape, MASKED, f32)
    l_ref[...] = jnp.zeros(l_ref.shape, f32)
    acc_ref[...] = jnp.zeros(acc_ref.shape, f32)

    def sel_tile(k0, kt, causal):
        s = _dot(kk_ref[0, 0, pl.ds(k0, kt), :], q_sel)
        if causal:
            kpos = k0 + lax.broadcasted_iota(jnp.int32, (kt, NSA_ROWS), 0)
            s = jnp.where(kpos <= qpos_r, s, MASKED)
        m_old = m_ref[...]
        m_new = jnp.maximum(m_old, jnp.max(s, axis=0, keepdims=True))
        alpha = jnp.exp(m_old - m_new)
        p = jnp.exp(s - m_new)
        l_ref[...] = alpha * l_ref[...] + jnp.sum(p, axis=0, keepdims=True)
        vt = v_tiles(k0 // Q_BLK, kt // Q_BLK)
        acc_ref[...] = alpha * acc_ref[...] + _dot(vt, p.astype(bf16))
        m_ref[...] = m_new

    n_full = q0 // SEL_KT

    def full_step(t, c):
        sel_tile(pl.multiple_of(t * SEL_KT, SEL_KT), SEL_KT, False)
        return c

    lax.fori_loop(0, n_full, full_step, 0)
    d0 = pl.multiple_of(n_full * SEL_KT, SEL_KT)
    short = q0 + Q_BLK - n_full * SEL_KT <= SEL_KT // 2

    @pl.when(short)
    def _():
        sel_tile(d0, SEL_KT // 2, True)

    @pl.when(jnp.logical_not(short))
    def _():
        sel_tile(d0, SEL_KT, True)
    o_st = acc_ref[0:HEAD_DIM, :] / l_ref[...]

    w0 = pl.multiple_of(jnp.maximum(q0 - WINDOW, 0), Q_BLK)
    s_w = _dot(kk_ref[0, 0, pl.ds(w0, WIN_KEYS), :], q_win)
    kpos_w = w0 + lax.broadcasted_iota(jnp.int32, (WIN_KEYS, NSA_ROWS), 0)
    s_w = jnp.where((kpos_w <= qpos_r) & (kpos_w > qpos_r - WINDOW), s_w, MASKED)
    p_w = jnp.exp(s_w - jnp.max(s_w, axis=0, keepdims=True))
    l_w = jnp.sum(p_w, axis=0, keepdims=True)
    acc_w = _dot(v_tiles(w0 // Q_BLK, WIN_KEYS // Q_BLK), p_w.astype(bf16))
    o_wt = acc_w[HEAD_DIM:2 * HEAD_DIM, :] / l_w

    g = g_ref[0, 0, 0]
    out_t = g[0:1, :] * o_ct + g[1:2, :] * o_st + g[2:3, :] * o_wt
    o_ref[0] = jnp.concatenate([out_t[:, g_ * Q_BLK:(g_ + 1) * Q_BLK] for g_ in range(NSA_GROUP)], axis=0).T


def _nsa_prompt(qr, qo, gt, kc_p, vct, kk, vvt):
    bsz, _, nqb = qr.shape[:3]
    t_ = nqb * Q_BLK
    n_cmp = kc_p.shape[2]
    per_blk = lambda b, h, i: (b, h, i, 0, 0)
    per_head = lambda b, h, i: (b, h, 0, 0)
    return pl.pallas_call(
        _nsa_prompt_body,
        grid=(bsz, NSA_KV_HEADS, nqb),
        in_specs=[pl.BlockSpec((1, 1, 1, HEAD_DIM, NSA_ROWS), per_blk),
                  pl.BlockSpec((1, 1, 1, HEAD_DIM, NSA_ROWS), per_blk),
                  pl.BlockSpec((1, 1, n_cmp, HEAD_DIM), per_head),
                  pl.BlockSpec((1, 1, HEAD_DIM, n_cmp), per_head),
                  pl.BlockSpec((1, 1, t_, KK_W), per_head),
                  pl.BlockSpec((1, 1, nqb, 2 * HEAD_DIM, Q_BLK), lambda b, h, i: (b, h, 0, 0, 0)),
                  pl.BlockSpec((1, 1, 1, 3, NSA_ROWS), per_blk)],
        out_specs=pl.BlockSpec((1, Q_BLK, NSA_GROUP * HEAD_DIM), lambda b, h, i: (b, i, h)),
        out_shape=jax.ShapeDtypeStruct((bsz, t_, NSA_HEADS * HEAD_DIM), jnp.float32),
        scratch_shapes=[pltpu.VMEM((8 + n_cmp, Q_BLK), jnp.float32),
                        pltpu.VMEM((1, NSA_ROWS), jnp.float32),
                        pltpu.VMEM((1, NSA_ROWS), jnp.float32),
                        pltpu.VMEM((2 * HEAD_DIM, NSA_ROWS), jnp.float32),
                        pltpu.VMEM((HEAD_DIM, NSA_ROWS), jnp.float32),
                        pltpu.VMEM((N_SELB, NSA_ROWS), jnp.bfloat16)],
        compiler_params=pltpu.CompilerParams(
            dimension_semantics=("arbitrary", "arbitrary", "arbitrary"),
            vmem_limit_bytes=VMEM_LIMIT),
        name="nsa_prompt",
    )(qr, qo, kc_p, vct, kk, vvt, gt)


GLA_SUB = 16
GLA_UNROLL = 8
GLA_TILE = 256
GLA_QK = GLA_HEADS * GLA_DK
GLA_V = GLA_HEADS * GLA_DV


def _dot_tn(a, b):
    return lax.dot_general(a, b, (((0,), (0,)), ((), ())), preferred_element_type=jnp.float32)


def _gla_body(q_ref, k_ref, v_ref, gr_ref, glr_ref, wg_ref, bg_ref, ng_ref, s0_ref, exp_ref,
              o_ref, sfin_ref, st_ref, b_ref, qd_ref, *, t_valid):
    f32, bf16 = jnp.float32, jnp.bfloat16
    tt = q_ref.shape[1]
    ti = pl.program_id(1)

    @pl.when(ti == 0)
    def _():
        st_ref[...] = s0_ref[0]

    row = lax.broadcasted_iota(jnp.int32, (tt, 1), 0)
    z = _dot(glr_ref[0][:, :GLA_RANK].astype(bf16), wg_ref[...]) + bg_ref[...]
    la = (jnp.minimum(z, 0.0) - jnp.log1p(jnp.exp(-jnp.abs(z)))) * (1.0 / GLA_TAU)
    la = jnp.where(ti * tt + row < t_valid, la, 0.0)
    seg = row & (GLA_SUB - 1)
    b = la
    for s in (1, 2, 4, 8):
        b = b + jnp.where(seg >= s, pltpu.roll(b, s, axis=0), 0.0)
    q = q_ref[0] * (GLA_DK ** -0.5)
    k = k_ref[0]
    v = v_ref[0]
    o = _dot((q * k).astype(bf16), exp_ref[...]) * v
    for d in range(1, GLA_SUB):
        decay = jnp.exp(jnp.minimum(b - pltpu.roll(b, d, axis=0), 0.0))
        w = jnp.where(seg >= d, q * pltpu.roll(k, d, axis=0) * decay, 0.0)
        o = o + _dot(w.astype(bf16), exp_ref[...]) * pltpu.roll(v, d, axis=0)
    o_ref[0] = o
    b_ref[...] = b
    qd_ref[...] = (q * jnp.exp(b)).astype(bf16)

    def block_step(c, carry):
        rows = pl.ds(pl.multiple_of(c * GLA_SUB, GLA_SUB), GLA_SUB)
        qd = qd_ref[rows, :]
        bc = b_ref[rows, :]
        bl = bc[GLA_SUB - 1:GLA_SUB, :]
        kc = (k_ref[0, rows, :] * jnp.exp(bl - bc)).astype(bf16)
        keep = jnp.exp(bl)
        vb = v_ref[0, rows, :].astype(bf16)
        outs = []
        for h in range(GLA_HEADS):
            dk = slice(h * GLA_DK, (h + 1) * GLA_DK)
            dv = slice(h * GLA_DV, (h + 1) * GLA_DV)
            st = st_ref[dv, :]
            outs.append(_dot_nt(qd[:, dk], st.astype(bf16)))
            st_ref[dv, :] = st * keep[:, dk] + _dot_tn(vb[:, dv], kc[:, dk])
        o_ref[0, rows, :] += jnp.concatenate(outs, axis=1)
        return carry

    lax.fori_loop(0, tt // GLA_SUB, block_step, 0, unroll=GLA_UNROLL)
    sfin_ref[0] = st_ref[...]
    gr = gr_ref[0]
    gate = gr * jax.nn.sigmoid(gr)
    for h in range(GLA_HEADS):
        cols = slice(h * GLA_DV, (h + 1) * GLA_DV)
        oh = o_ref[0, :, cols]
        ms = jnp.mean(oh * oh, axis=-1, keepdims=True)
        o_ref[0, :, cols] = oh * lax.rsqrt(ms + LN_EPS) * ng_ref[...] * gate[:, cols]


def _gla(h, w_gla_gate, b_gla_gate, gla_norm_g, gla_state):
    bsz, t_, n_in = h.shape
    tp = -(-t_ // GLA_SUB) * GLA_SUB
    if tp != t_:
        h = jnp.pad(h, ((0, 0), (0, tp - t_), (0, 0)))
    tt = min(tp, GLA_TILE)
    expand =np.repeat(np.repeat(np.eye(GLA_HEADS, dtype=np.float32), GLA_DK, 0), GLA_DV, 1)
    if gla_state is None:
        s0 = jnp.zeros((bsz, GLA_V, GLA_DK), jnp.float32)
    else:
        s0 = gla_state.transpose(0, 1, 3, 2).reshape(bsz, GLA_V, GLA_DK)
    tile = lambda width, blk: pl.BlockSpec((1, tt, width), lambda b, i: (b, i, blk))
    fixed2 = lambda shape: pl.BlockSpec(shape, lambda b, i: (0, 0))
    per_b = pl.BlockSpec((1, GLA_V, GLA_DK), lambda b, i: (b, 0, 0))
    o, s_t = pl.pallas_call(
        functools.partial(_gla_body, t_valid=t_),
        grid=(bsz, tp // tt),
        in_specs=[tile(GLA_QK, 0), tile(GLA_QK, 1), tile(GLA_V, 1), tile(GLA_V, 2),
                  tile(LANE, (2 * GLA_QK + 2 * GLA_V + NSA_SIZES[0] + NSA_SIZES[1]) // LANE),
                  fixed2((GLA_RANK, GLA_QK)), fixed2((1, GLA_QK)), fixed2((1, GLA_DV)), per_b,
                  fixed2((GLA_QK, GLA_V))],
        out_specs=[pl.BlockSpec((1, tt, GLA_V), lambda b, i: (b, i, 0)), per_b],
        out_shape=[jax.ShapeDtypeStruct((bsz, tp, GLA_V), jnp.float32),
                   jax.ShapeDtypeStruct((bsz, GLA_V, GLA_DK), jnp.float32)],
        scratch_shapes=[pltpu.VMEM((GLA_V, GLA_DK), jnp.float32), pltpu.VMEM((tt, GLA_QK), jnp.float32),
                        pltpu.VMEM((tt, GLA_QK), jnp.bfloat16)],
        compiler_params=pltpu.CompilerParams(dimension_semantics=("arbitrary", "arbitrary"),
                                             vmem_limit_bytes=VMEM_LIMIT),
        name="gla",
    )(h, h, h, h, h, w_gla_gate.astype(jnp.bfloat16), b_gla_gate.reshape(1, GLA_QK),
      gla_norm_g.reshape(1, GLA_DV), s0, jnp.asarray(expand, jnp.bfloat16))
    return o[:, :t_], s_t.reshape(bsz, GLA_HEADS, GLA_DV, GLA_DK).transpose(0, 1, 3, 2)


COL_NQ = 2 * GLA_QK + 2 * GLA_V
COL_NKV = COL_NQ + NSA_SIZES[0]
COL_TAIL = COL_NKV + NSA_SIZES[1]
TAIL_GATE = GLA_RANK
_ORIG = np.cumsum((0,) + GLA_SIZES + NSA_SIZES)
IN_AB_PERM = np.concatenate([np.arange(_ORIG[0], _ORIG[4]), np.arange(_ORIG[5], _ORIG[7]),
                             np.arange(_ORIG[4], _ORIG[5]), np.arange(_ORIG[7], _ORIG[8])])
SUBS = Q_BLK // CMP_STRIDE


def _nsa_prep_body(nq_ref, kv0_ref, kv1_ref, kv2_ref, tail_ref, rc_ref, ru_ref, rd_ref, pool_ref,
                   rows_ref, win_ref, kk_ref, vvt_ref, qr_ref, qo_ref, g_ref, pooled_ref):
    bf16 = jnp.bfloat16
    q0 = pl.program_id(1) * Q_BLK
    kv_w = NSA_KV_HEADS * HEAD_DIM

    def rope(x):
        reps = x.shape[1] // LANE
        wide = lambda r: jnp.concatenate([r[...]] * reps, axis=1) if reps > 1 else r[...]
        half = ROPE_DIM // 2
        return (x * wide(rc_ref) + pltpu.roll(x, half, axis=1) * wide(ru_ref)
                + pltpu.roll(x, x.shape[1] - half, axis=1) * wide(rd_ref))

    kv0, kv1, kv2 = kv0_ref[0], kv1_ref[0], kv2_ref[0]
    k_sel, v_sel = rope(kv1[:, :kv_w]), kv1[:, kv_w:]
    k_win, v_win = rope(kv2[:, :kv_w]), kv2[:, kv_w:]
    rows_ref[0] = jnp.concatenate([kv0, k_sel, v_sel], axis=1)
    win_ref[0] = jnp.concatenate([k_win, v_win], axis=1)
    blk_id = lax.shift_right_logical(q0 + lax.broadcasted_iota(jnp.int32, (Q_BLK, N_SELB), 0),
                                     int(math.log2(SEL_BLK)))
    onehot = jnp.where(lax.broadcasted_iota(jnp.int32, (Q_BLK, N_SELB), 1) == blk_id, 1.0, 0.0).astype(bf16)
    q = nq_ref[0] * (HEAD_DIM ** -0.5)
    q_rot = rope(q)
    gates_t = jax.nn.sigmoid(tail_ref[0]).T
    for h in range(NSA_KV_HEADS):
        hs = slice(h * HEAD_DIM, (h + 1) * HEAD_DIM)
        kk_ref[0, h] = jnp.concatenate([k_sel[:, hs].astype(bf16), k_win[:, hs].astype(bf16), onehot], axis=1)
        vvt_ref[0, h, 0] = jnp.concatenate([v_sel[:, hs], v_win[:, hs]], axis=1).T.astype(bf16)
        gw = NSA_GROUP * HEAD_DIM
        for src, dst in ((q, qr_ref), (q_rot, qo_ref)):
            t = src[:, h * gw:(h + 1) * gw].T
            dst[0, h, 0] = jnp.concatenate([t[g * HEAD_DIM:(g + 1) * HEAD_DIM] for g in range(NSA_GROUP)],
                                           axis=1).astype(bf16)
        base = TAIL_GATE + h * NSA_GROUP * 3
        g_ref[0, h, 0] = jnp.concatenate(
            [jnp.concatenate([gates_t[base + 3 * g + c:base + 3 * g + c + 1] for g in range(NSA_GROUP)], axis=1)
             for c in range(3)], axis=0)
    kc_in, vc_in = kv0[:, :kv_w].astype(bf16), kv0[:, kv_w:].astype(bf16)
    pooled_ref[0] = jnp.concatenate([_dot(pool_ref[0], kc_in), _dot(pool_ref[1], kc_in),
                                     _dot(pool_ref[2], vc_in), _dot(pool_ref[3], vc_in)], axis=1)


def _nsa_prep(h, pos, w_cmp_pool):
    bsz, t_, _ = h.shape
    nqb = t_ // Q_BLK
    bf16 = jnp.bfloat16
    half = ROPE_DIM // 2
    inv_freq = jnp.power(ROPE_THETA, -jnp.arange(half, dtype=jnp.float32) / half)
    ang = pos.astype(jnp.float32)[:, None] * inv_freq
    cos, sin = jnp.cos(ang), jnp.sin(ang)
    rest = HEAD_DIM - ROPE_DIM
    z8, zr = jnp.zeros((t_, half), jnp.float32), jnp.zeros((t_, rest), jnp.float32)
    two = lambda a: jnp.concatenate([a, a], axis=1)
    rc = two(jnp.concatenate([cos, cos, jnp.ones((t_, rest), jnp.float32)], axis=1))
    ru = two(jnp.concatenate([z8, sin, zr], axis=1))
    rd = two(jnp.concatenate([-sin, z8, zr], axis=1))
    pool = _pool_matrices(w_cmp_pool)
    kv_w = NSA_KV_HEADS * HEAD_DIM
    col = lambda width, off: pl.BlockSpec((1, Q_BLK, width), lambda b, i: (b, i, off // width))
    rows_t = pl.BlockSpec((Q_BLK, LANE), lambda b, i: (i, 0))
    head4 = lambda r, c: pl.BlockSpec((1, NSA_KV_HEADS, 1, r, c), lambda b, i: (b, 0, i, 0, 0))
    return pl.pallas_call(
        _nsa_prep_body,
        grid=(bsz, nqb),
        in_specs=[col(NSA_SIZES[0], COL_NQ), col(2 * kv_w, COL_NKV), col(2 * kv_w, COL_NKV + 2 * kv_w),
                  col(2 * kv_w, COL_NKV + 4 * kv_w), col(LANE, COL_TAIL), rows_t, rows_t, rows_t,
                  pl.BlockSpec((4, SUBS, Q_BLK), lambda b, i: (0, 0, 0))],
        out_specs=[pl.BlockSpec((1, Q_BLK, 4 * kv_w), lambda b, i: (b, i, 0)),
                   pl.BlockSpec((1, Q_BLK, 2 * kv_w), lambda b, i: (b, i, 0)),
                   pl.BlockSpec((1, NSA_KV_HEADS, Q_BLK, KK_W), lambda b, i: (b, 0, i, 0)),
                   head4(2 * HEAD_DIM, Q_BLK), head4(HEAD_DIM, NSA_ROWS), head4(HEAD_DIM, NSA_ROWS),
                   head4(3, NSA_ROWS),
                   pl.BlockSpec((1, SUBS, 4 * kv_w), lambda b, i: (b, i, 0))],
        out_shape=[jax.ShapeDtypeStruct((bsz, t_, 4 * kv_w), jnp.float32),
                   jax.ShapeDtypeStruct((bsz, t_, 2 * kv_w), jnp.float32),
                   jax.ShapeDtypeStruct((bsz, NSA_KV_HEADS, t_, KK_W), bf16),
                   jax.ShapeDtypeStruct((bsz, NSA_KV_HEADS, nqb, 2 * HEAD_DIM, Q_BLK), bf16),
                   jax.ShapeDtypeStruct((bsz, NSA_KV_HEADS, nqb, HEAD_DIM, NSA_ROWS), bf16),
                   jax.ShapeDtypeStruct((bsz, NSA_KV_HEADS, nqb, HEAD_DIM, NSA_ROWS), bf16),
                   jax.ShapeDtypeStruct((bsz, NSA_KV_HEADS, nqb, 3, NSA_ROWS), jnp.float32),
                   jax.ShapeDtypeStruct((bsz, t_ // CMP_STRIDE, 4 * kv_w), jnp.float32)],
        compiler_params=pltpu.CompilerParams(dimension_semantics=("arbitrary", "arbitrary")),
        name="nsa_prep",
    )(h, h, h, h, h, rc, ru, rd, pool)


PAGE_GROUP = 32
DEC_KEYS = PAGE_GROUP * PAGE_SIZE
POOL_ROWS = 2048
NEW_PAD = 8
KV_W = NSA_KV_HEADS * HEAD_DIM


def _dec_pool_body(pt_ref, *refs):
    page_refs, pool_ref, out_ref = refs[:PAGE_GROUP], refs[PAGE_GROUP], refs[PAGE_GROUP + 1]
    bf16 = jnp.bfloat16
    pages = [pr[0] for pr in page_refs]
    per = POOL_ROWS // PAGE_SIZE
    cols = []
    for g0 in range(0, PAGE_GROUP, per):
        kc_t = jnp.concatenate([p[:KV_W] for p in pages[g0:g0 + per]], axis=1).astype(bf16)
        vc_t = jnp.concatenate([p[KV_W:] for p in pages[g0:g0 + per]], axis=1).astype(bf16)
        cols.append(jnp.concatenate([_dot(kc_t, pool_ref[0]), _dot(kc_t, pool_ref[1]),
                                     _dot(vc_t, pool_ref[2]), _dot(vc_t, pool_ref[3])], axis=0))
    out_ref[0] = jnp.concatenate(cols, axis=1)


def _page_specs(n_pages, col_blk):
    def spec(i):
        return pl.BlockSpec((1, 2 * KV_W, PAGE_SIZE),
                            lambda b, j, pt: (pt[b * n_pages + j * PAGE_GROUP + i], col_blk, 0))
    return [spec(i) for i in range(PAGE_GROUP)]


def _dec_pool(cache, page_table, pool):
    bsz, n_pages = page_table.shape
    grid_spec = pltpu.PrefetchScalarGridSpec(
        num_scalar_prefetch=1, grid=(bsz, n_pages // PAGE_GROUP),
        in_specs=_page_specs(n_pages, 0) + [pl.BlockSpec(pool.shape, lambda b, j, pt: (0, 0, 0))],
        out_specs=pl.BlockSpec((1, 4 * KV_W, PAGE_GROUP * SUBS), lambda b, j, pt: (b, 0, j)))
    return pl.pallas_call(
        _dec_pool_body, grid_spec=grid_spec,
        out_shape=jax.ShapeDtypeStruct((bsz, 4 * KV_W, n_pages * SUBS), jnp.float32),
        compiler_params=pltpu.CompilerParams(dimension_semantics=("arbitrary", "arbitrary")),
        name="nsa_dec_pool",
    )(page_table.reshape(-1), *([cache] * PAGE_GROUP), pool)


def _dec_select_body(qr_ref, kct_ref, vc_ref, band_ref, oc_ref, selb_ref, *, qpos0, n_q, n_pick, n_blk):
    f32, bf16 = jnp.float32, jnp.bfloat16
    n_cmp = kct_ref.shape[3]
    rows = NSA_GROUP * n_q
    for sq, h in [(a, b) for a in range(qr_ref.shape[0]) for b in range(NSA_KV_HEADS)]:
        s_c = _dot(qr_ref[sq, h], kct_ref[sq, h])
        n_idx = lax.broadcasted_iota(jnp.int32, (rows, n_cmp), 1)
        qpos = qpos0 + (lax.broadcasted_iota(jnp.int32, (rows, n_cmp), 0) % n_q)
        cmask = (n_idx * CMP_STRIDE + (CMP_BLK - 1)) <= qpos
        s_c = jnp.where(cmask, s_c, MASKED)
        p_c = jnp.where(cmask, jnp.exp(s_c - jnp.max(s_c, axis=1, keepdims=True)), 0.0)
        p_c = p_c / jnp.maximum(jnp.sum(p_c, axis=1, keepdims=True), 1e-30)
        oc_ref[sq, h] = _dot(p_c.astype(bf16), vc_ref[sq, h])
        imp = p_c[0:n_q]
        for g in range(1, NSA_GROUP):
            imp = imp + p_c[g * n_q:(g + 1) * n_q]
        imp_s = jnp.zeros((n_q, N_SELB), f32)
        rem = imp
        for _ in range(3):
            part = rem.astype(bf16)
            imp_s = imp_s + _dot(part, band_ref[...])
            rem = rem - part.astype(f32)
        blk = lax.broadcasted_iota(jnp.int32, (n_q, N_SELB), 1)
        qpos_s = qpos0 + lax.broadcasted_iota(jnp.int32, (n_q, N_SELB), 0)
        cur = lax.shift_right_logical(qpos_s, int(math.log2(SEL_BLK)))
        valid = (blk * SEL_BLK <= qpos_s) & (blk < n_blk)
        forced = (blk == 0) | (blk == cur) | (blk == cur - 1)
        score = jnp.where(valid, imp_s + jnp.where(forced, FORCE_BONUS, 0.0), -1e30)
        picked = jnp.zeros((n_q, N_SELB), f32)
        for _ in range(n_pick):
            best = jnp.max(score, axis=1, keepdims=True)
            first = jnp.min(jnp.where(score == best, blk, N_SELB), axis=1, keepdims=True)
            hit = blk == first
            picked = jnp.where(hit, 1.0, picked)
            score = jnp.where(hit, -3e38, score)
        selb_ref[sq, h] = (jnp.where(valid, picked, 0.0) - 1.0) * (-MASKED)


def _dec_select(qr, kct, vc, n_q, qpos0, n_pick, n_blk):
    bsz = qr.shape[0]
    rows = NSA_GROUP * n_q
    n_cmp = kct.shape[3]
    ratio = SEL_BLK // CMP_STRIDE
    c_idx, j_idx = np.arange(n_cmp)[:, None], np.arange(N_SELB)[None, :]
    band = jnp.asarray(((c_idx >= ratio * j_idx - 1) & (c_idx <= ratio * j_idx + ratio - 1)), jnp.bfloat16)
    per_step = next(c for c in (4, 2, 1) if bsz % c == 0)
    per_b = lambda *tail: pl.BlockSpec((per_step, NSA_KV_HEADS) + tail, lambda b: (b, 0, 0, 0))
    return pl.pallas_call(
        functools.partial(_dec_select_body, qpos0=qpos0, n_q=n_q, n_pick=n_pick, n_blk=n_blk),
        grid=(bsz // per_step,),
        in_specs=[per_b(rows, HEAD_DIM), per_b(HEAD_DIM, n_cmp), per_b(n_cmp, HEAD_DIM),
                  pl.BlockSpec((n_cmp, N_SELB), lambda b: (0, 0))],
        out_specs=[per_b(rows, HEAD_DIM), per_b(n_q, N_SELB)],
        out_shape=[jax.ShapeDtypeStruct((bsz, NSA_KV_HEADS, rows, HEAD_DIM), jnp.float32),
                   jax.ShapeDtypeStruct((bsz, NSA_KV_HEADS, n_q, N_SELB), jnp.float32)],
        compiler_params=pltpu.CompilerParams(dimension_semantics=("arbitrary",)),
        name="nsa_dec_select",
    )(qr, kct, vc, band)


def _dec_attend_body(pt_ref, *refs, qpos0, n_q, past):
    page_refs = refs[:PAGE_GROUP]
    (qs_ref, qw_ref, knew_ref, vnew_ref, wbuf_ref, wnew_ref, oc_ref, g_ref,
     o_ref, m_ref, l_ref, acc_ref) = refs[PAGE_GROUP:]
    f32, bf16 = jnp.float32, jnp.bfloat16
    j = pl.program_id(1)
    n_rows = qs_ref.shape[1]

    @pl.when(j == 0)
    def _():
        m_ref[...] = jnp.full(m_ref.shape, MASKED, f32)
        l_ref[...] = jnp.zeros(l_ref.shape, f32)
        acc_ref[...] = jnp.zeros(acc_ref.shape, f32)

    def online(s, weigh):
        m_old = m_ref[...]
        m_new = jnp.maximum(m_old, jnp.max(s, axis=1, keepdims=True))
        alpha = jnp.exp(m_old - m_new)
        p = jnp.exp(s - m_new)
        l_ref[...] = alpha * l_ref[...] + jnp.sum(p, axis=1, keepdims=True)
        acc_ref[...] = alpha * acc_ref[...] + weigh(p.astype(bf16))
        m_ref[...] = m_new

    qs = qs_ref[0]
    pages = [pr[0] for pr in page_refs]
    keys_t = jnp.concatenate([p[:KV_W] for p in pages], axis=1).astype(bf16)
    vals_t = jnp.concatenate([p[KV_W:] for p in pages], axis=1).astype(bf16)
    blk_id = j * (DEC_KEYS // SEL_BLK) + lax.shift_right_logical(
        lax.broadcasted_iota(jnp.int32, (N_SELB, DEC_KEYS), 1), int(math.log2(SEL_BLK)))
    onehot_t = jnp.where(lax.broadcasted_iota(jnp.int32, (N_SELB, DEC_KEYS), 0) == blk_id, 1.0, 0.0).astype(bf16)
    online(_dot(qs, jnp.concatenate([keys_t, onehot_t], axis=0)), lambda p: _dot_nt(p, vals_t))

    @pl.when(j == pl.num_programs(1) - 1)
    def _():
        row_q = qpos0 + (lax.broadcasted_iota(jnp.int32, (n_rows, 1), 0) % n_q)
        qh = qw_ref[0]
        new_pos = past + lax.broadcasted_iota(jnp.int32, (n_rows, NEW_PAD), 1)
        new_ok = (new_pos <= row_q) & (new_pos < past + n_q)
        s_new = jnp.where(new_ok, _dot_nt(qh, knew_ref[0]), MASKED)
        online(s_new, lambda p: _dot(p, vnew_ref[0]))
        o_s = acc_ref[...] / l_ref[...]
        wbuf_t = wbuf_ref[0]
        wnew = wnew_ref[0]
        n_buf = wbuf_t.shape[1]
        s_b = _dot(qh, wbuf_t[:KV_W].astype(bf16))
        pos_b = (past - n_buf) + lax.broadcasted_iota(jnp.int32, (n_rows, n_buf), 1)
        s_b = jnp.where((pos_b > row_q - WINDOW) & (pos_b >= 0), s_b, MASKED)
        s_n = jnp.where(new_ok, _dot_nt(qh, wnew[:, :KV_W].astype(bf16)), MASKED)
        m_w = jnp.maximum(jnp.max(s_b, axis=1, keepdims=True), jnp.max(s_n, axis=1, keepdims=True))
        p_b, p_n = jnp.exp(s_b - m_w), jnp.exp(s_n - m_w)
        l_w = jnp.sum(p_b, axis=1, keepdims=True) + jnp.sum(p_n, axis=1, keepdims=True)
        o_w = (_dot_nt(p_b.astype(bf16), wbuf_t[KV_W:].astype(bf16))
               + _dot(p_n.astype(bf16), wnew[:, KV_W:].astype(bf16))) / l_w
        half = n_rows // NSA_KV_HEADS
        own = lambda a: jnp.concatenate([a[h * half:(h + 1) * half, h * HEAD_DIM:(h + 1) * HEAD_DIM]
                                         for h in range(NSA_KV_HEADS)], axis=0)
        g = g_ref[0]
        o_ref[0] = g[:, 0:1] * oc_ref[0] + g[:, 1:2] * own(o_s) + g[:, 2:3] * own(o_w)


def _dec_attend(cache, page_table, qs, qw, knew, vnew, wbuf, wnew, o_c, gates, n_q, qpos0):
    bsz, n_pages = page_table.shape
    n_rows = qs.shape[1]
    per_b = lambda *tail: pl.BlockSpec((1,) + tail, lambda b, j, pt: (b, 0, 0))
    grid_spec = pltpu.PrefetchScalarGridSpec(
        num_scalar_prefetch=1, grid=(bsz, n_pages // PAGE_GROUP),
        in_specs=_page_specs(n_pages, 1) + [
            per_b(n_rows, KV_W + N_SELB), per_b(n_rows, KV_W), per_b(NEW_PAD, KV_W), per_b(NEW_PAD, KV_W),
            per_b(2 * KV_W, wbuf.shape[2]), per_b(NEW_PAD, 2 * KV_W), per_b(n_rows, HEAD_DIM), per_b(n_rows, 3)],
        out_specs=per_b(n_rows, HEAD_DIM),
        scratch_shapes=[pltpu.VMEM((n_rows, 1), jnp.float32), pltpu.VMEM((n_rows, 1), jnp.float32),
                        pltpu.VMEM((n_rows, KV_W), jnp.float32)])
    return pl.pallas_call(
        functools.partial(_dec_attend_body, qpos0=qpos0, n_q=n_q, past=n_pages * PAGE_SIZE),
        grid_spec=grid_spec,
        out_shape=jax.ShapeDtypeStruct((bsz, n_rows, HEAD_DIM), jnp.float32),
        compiler_params=pltpu.CompilerParams(dimension_semantics=("arbitrary", "arbitrary")),
        name="nsa_dec_attend",
    )(page_table.reshape(-1), *([cache] * PAGE_GROUP), qs, qw, knew, vnew, wbuf, wnew, o_c, gates)


def _pool_matrices(w_cmp_pool, rows=Q_BLK):
    subs = rows // CMP_STRIDE
    sub = np.arange(rows) // CMP_STRIDE == np.arange(subs)[:, None]
    w_rep = jnp.tile(w_cmp_pool.reshape(2, 2, CMP_STRIDE), (1, 1, subs))
    return jnp.where(sub[None, None], w_rep[:, :, None, :], 0.0).reshape(4, subs, rows).astype(jnp.bfloat16)


def _nsa_decode(q_raw, q_rot, gates, rows_full, rows_win, cache, page_table, win_buf, w_cmp_pool, past):
    bsz, n_q = q_raw.shape[:2]
    bf16 = jnp.bfloat16
    n_blk = past // SEL_BLK
    assert past % DEC_KEYS == 0 and n_blk <= N_SELB and n_q <= NEW_PAD
    scale = HEAD_DIM ** -0.5
    cache2 = cache.transpose(0, 2, 3, 4, 1).reshape(cache.shape[0], 4 * KV_W, PAGE_SIZE)
    pooled_t = _dec_pool(cache2, page_table, _pool_matrices(w_cmp_pool, POOL_ROWS).transpose(0, 2, 1))
    pooled_t = pooled_t.reshape(bsz, 4, NSA_KV_HEADS, HEAD_DIM, -1)
    last = ((0, 0), (0, 0), (0, 0), (0, 1))
    kct = jnp.pad(pooled_t[:, 0, ..., :-1] + pooled_t[:, 1, ..., 1:], last)
    vc_p = jnp.pad(pooled_t[:, 2, ..., :-1] + pooled_t[:, 3, ..., 1:], last).transpose(0, 1, 3, 2)
    rows_of = lambda a: a.transpose(0, 2, 3, 1, 4).reshape(bsz, NSA_KV_HEADS, NSA_GROUP * n_q, a.shape[-1])
    qr = rows_of((q_raw * scale).astype(bf16))
    n_pick = min(SEL_TOPN, n_blk + 1) - 1
    o_c, selb = _dec_select(qr, kct.astype(bf16), vc_p.astype(bf16), n_q, past, n_pick, n_blk)
    qo = rows_of((q_rot * scale).astype(bf16))
    zero = jnp.zeros_like(qo[:, 0])
    qw = jnp.concatenate([jnp.concatenate([qo[:, 0], zero], -1), jnp.concatenate([zero, qo[:, 1]], -1)], axis=1)
    bias = jnp.tile(selb, (1, 1, NSA_GROUP, 1)).reshape(bsz, -1, N_SELB).astype(bf16)
    qs = jnp.concatenate([qw, bias], axis=-1)
    pad_new = lambda a: jnp.pad(a.reshape(bsz, n_q, -1), ((0, 0), (0, NEW_PAD - n_q), (0, 0)))
    knew = pad_new(rows_full[:, :, 2]).astype(bf16)
    vnew = pad_new(rows_full[:, :, 3]).astype(bf16)
    wnew = pad_new(rows_win)
    wbuf = win_buf.transpose(0, 2, 3, 4, 1).reshape(bsz, 2 * KV_W, win_buf.shape[1])
    gt = rows_of(gates).reshape(bsz, -1, 3)
    o = _dec_attend(cache2, page_table, qs, qw, knew, vnew, wbuf, wnew,
                    o_c.reshape(bsz, -1, HEAD_DIM), gt, n_q, past)
    o = o.reshape(bsz, NSA_KV_HEADS, NSA_GROUP, n_q, HEAD_DIM).transpose(0, 3, 1, 2, 4)
    return o.reshape(bsz, n_q, NSA_HEADS * HEAD_DIM)


def _ab_mixer(x, pos, w_in, w_gla_gate, b_gla_gate, gla_norm_g, w_cmp_pool, w_out,
              gla_state, nsa_cache, page_table, win_buf):
    bsz, t_, _ = x.shape
    h_in = _mm(x.reshape(bsz * t_, -1), w_in[:, IN_AB_PERM], keep_pad=True).reshape(bsz, t_, -1)
    o_a, s_a = _gla(h_in, w_gla_gate, b_gla_gate, gla_norm_g, gla_state)
    kv_w = NSA_KV_HEADS * HEAD_DIM
    if nsa_cache is None:
        rows2, win2, kk, vvt, qr, qo, gt, pooled = _nsa_prep(h_in, pos, w_cmp_pool)
        pooled = pooled.reshape(bsz, t_ // CMP_STRIDE, 4, NSA_KV_HEADS, HEAD_DIM)
        kc = pooled[:, :-1, 0] + pooled[:, 1:, 1]
        vc = pooled[:, :-1, 2] + pooled[:, 1:, 3]
        kc_p = jnp.pad(kc, ((0, 0), (0, 1), (0, 0), (0, 0))).transpose(0, 2, 1, 3).astype(jnp.bfloat16)
        vct = jnp.pad(vc, ((0, 0), (0, 1), (0, 0), (0, 0))).transpose(0, 2, 3, 1).astype(jnp.bfloat16)
        o_b = _nsa_prompt(qr, qo, gt, kc_p, vct, kk, vvt)
        rows_full = rows2.reshape(bsz, t_, 4, NSA_KV_HEADS, HEAD_DIM)
        new_win = win2[:, -min(WINDOW, t_):].reshape(bsz, -1, 2, NSA_KV_HEADS, HEAD_DIM)
    else:
        nq = h_in[..., COL_NQ:COL_NKV]
        nkv = h_in[..., COL_NKV:COL_TAIL]
        ngate = h_in[..., COL_TAIL + TAIL_GATE:COL_TAIL + TAIL_GATE + NSA_SIZES[2]]
        q_raw = nq.reshape(bsz, t_, NSA_KV_HEADS, NSA_GROUP, HEAD_DIM)
        q_rot = _partial_rope(q_raw, pos)
        kv = nkv.reshape(bsz, t_, 6, NSA_KV_HEADS, HEAD_DIM)
        k_sel = _partial_rope(kv[:, :, 2], pos)
        k_win = _partial_rope(kv[:, :, 4], pos)
        rows_full = jnp.stack([kv[:, :, 0], kv[:, :, 1], k_sel, kv[:, :, 3]], axis=2)
        rows_win = jnp.stack([k_win, kv[:, :, 5]], axis=2)
        gates = jax.nn.sigmoid(ngate).reshape(bsz, t_, NSA_KV_HEADS, NSA_GROUP, 3)
        past_len = page_table.shape[1] * PAGE_SIZE
        o_b = _nsa_decode(q_raw, q_rot, gates, rows_full, rows_win, nsa_cache, page_table, win_buf,
                          w_cmp_pool, past_len)
        w_buf = win_buf.shape[1]
        kw = jnp.concatenate([win_buf, rows_win], axis=1)
        new_win = kw[:, -w_buf:]
    y = _mm_pair(o_a.reshape(bsz * t_, -1), o_b.reshape(bsz * t_, -1), w_out).reshape(bsz, t_, -1)
    return y, s_a, rows_full, new_win


CONV_HALO = 32
CONV_LEAD = CONV_HALO - (CONV_W - 1)


def _conv_body(x_ref, buf0_ref, w1_ref, b1_ref, wdw_ref, bdw_ref, g_ref, b_ref, w2_ref, b2_ref,
               o_ref, tail_ref, ext_ref, z_ref, *, t_last):
    bf16 = jnp.bfloat16
    tt = x_ref.shape[1]
    i = pl.program_id(1)

    @pl.when(i == 0)
    def _():
        ext_ref[0:CONV_HALO, :] = buf0_ref[0]
        ext_ref[CONV_HALO + tt:CONV_HALO + tt + SUBLANES, :] = jnp.zeros((SUBLANES, D_CONV), jnp.float32)

    h = _dot(x_ref[0].astype(bf16), w1_ref[...]) + b1_ref[...]
    ext_ref[CONV_HALO:CONV_HALO + tt, :] = h[:, :D_CONV] * jax.nn.sigmoid(h[:, D_CONV:])
    c = jnp.zeros((tt, D_CONV), jnp.float32) + bdw_ref[...]
    for r in range(SUBLANES):
        z = None
        for a in range(CONV_HALO // SUBLANES + 1):
            k = SUBLANES * a + r - CONV_LEAD
            if 0 <= k < CONV_W:
                term = ext_ref[SUBLANES * a:SUBLANES * a + tt + SUBLANES, :] * wdw_ref[k:k + 1, :]
                z = term if z is None else z + term
        if r == 0:
            c = c + z[:tt]
        else:
            z_ref[...] = z
            c = c + z_ref[pl.ds(r, tt), :]
    c = _ln_rows(c, g_ref[...], b_ref[...])
    c = c * jax.nn.sigmoid(c)
    o_ref[0] = _dot(c.astype(bf16), w2_ref[...]) + b2_ref[...]
    tail_ref[0] = ext_ref[t_last:t_last + CONV_HALO, :]
    ext_ref[0:CONV_HALO, :] = ext_ref[tt:tt + CONV_HALO, :]


def _conv_module(x, conv_buf, w_pw1, b_pw1, w_dw, b_dw, ln_g, ln_b, w_pw2, b_pw2):
    bsz, t_, d = x.shape
    bf16 = jnp.bfloat16
    tp = -(-t_ // 8) * 8
    tt = min(tp, 256)
    n_t = tp // tt
    if tp != t_:
        x = jnp.pad(x, ((0, 0), (0, tp - t_), (0, 0)))
    if conv_buf is None:
        buf0 = jnp.zeros((bsz, CONV_HALO, D_CONV), jnp.float32)
    else:
        buf0 = jnp.pad(conv_buf, ((0, 0), (CONV_LEAD, 0), (0, 0)))
    fixed = lambda shape: pl.BlockSpec(shape, lambda b, i: (0,) * len(shape))
    per_b = pl.BlockSpec((1, CONV_HALO, D_CONV), lambda b, i: (b, 0, 0))
    out, tail = pl.pallas_call(
        functools.partial(_conv_body, t_last=t_ - (n_t - 1) * tt),
        grid=(bsz, n_t),
        in_specs=[pl.BlockSpec((1, tt, d), lambda b, i: (b, i, 0)), per_b,
                  fixed((d, 2 * D_CONV)), fixed((1, 2 * D_CONV)), fixed((CONV_HALO, D_CONV)), fixed((1, D_CONV)),
                  fixed((1, D_CONV)), fixed((1, D_CONV)), fixed((D_CONV, d)), fixed((1, d))],
        out_specs=[pl.BlockSpec((1, tt, d), lambda b, i: (b, i, 0)), per_b],
        out_shape=[jax.ShapeDtypeStruct((bsz, tp, d), jnp.float32),
                   jax.ShapeDtypeStruct((bsz, CONV_HALO, D_CONV), jnp.float32)],
        scratch_shapes=[pltpu.VMEM((CONV_HALO + tt + SUBLANES, D_CONV), jnp.float32),
                        pltpu.VMEM((tt + SUBLANES, D_CONV), jnp.float32)],
        compiler_params=pltpu.CompilerParams(dimension_semantics=("arbitrary", "arbitrary"),
                                             vmem_limit_bytes=VMEM_LIMIT),
        name="conv_module",
    )(x, buf0, w_pw1.astype(bf16), b_pw1.reshape(1, -1), jnp.pad(w_dw, ((0, CONV_HALO - CONV_W), (0, 0))),
      b_dw.reshape(1, -1), ln_g.reshape(1, -1), ln_b.reshape(1, -1), w_pw2.astype(bf16), b_pw2.reshape(1, -1))
    return out[:, :t_], tail[:, CONV_LEAD:]


PACK_W = 256
SC_WINDOW = 128
SC_TILES = 32


def _pack_rows(y):
    out = []
    for h in range(2):
        lo = lax.bitcast_convert_type(y[:, 2 * h * PACK_W:(2 * h + 1) * PACK_W].astype(jnp.bfloat16)
                                      .astype(jnp.float32), jnp.uint32)
        hi = lax.bitcast_convert_type(y[:, (2 * h + 1) * PACK_W:(2 * h + 2) * PACK_W].astype(jnp.bfloat16)
                                      .astype(jnp.float32), jnp.uint32)
        out.append(lax.bitcast_convert_type((lo >> 16) | hi, jnp.int32))
    return out


def _unpack_words(w):
    u = lax.bitcast_convert_type(w, jnp.uint32)
    lo = lax.bitcast_convert_type(u << 16, jnp.float32)
    hi = lax.bitcast_convert_type(u & jnp.uint32(0xFFFF0000), jnp.float32)
    return lo, hi


def _gather_rows(src, idx):
    n = idx.shape[0]
    if n % (SC_WINDOW * SC_TILES) != 0:
        return jnp.take(src, idx, axis=0)
    mesh = plsc.VectorSubcoreMesh(core_axis_name="core", subcore_axis_name="subcore")

    @pl.kernel(out_type=jax.ShapeDtypeStruct((n, src.shape[1]), src.dtype), mesh=mesh)
    def gather_kernel(src_hbm, idx_hbm, out_hbm):
        def step(idx_vmem, out_vmem):
            pltpu.sync_copy(src_hbm.at[idx_vmem.at[0]], out_vmem)

        pltpu.emit_pipeline(
            step, grid=(n // SC_WINDOW,),
            in_specs=[pl.BlockSpec((1, SC_WINDOW), index_map=lambda i: (0, i))],
            out_specs=[pl.BlockSpec((SC_WINDOW, src.shape[1]), index_map=lambda i: (i, 0))],
            core_axis_name=("core", "subcore"),
            dimension_semantics=(pltpu.PARALLEL,),
        )(idx_hbm, out_hbm)

    return gather_kernel(src, idx.reshape(1, n))


def _scatter_rows(src, idx, n_out):
    n = idx.shape[0]
    m = src.shape[0] // 2
    reps = n // (2 * m)
    if n % (SC_WINDOW * SC_TILES) != 0 or m % SC_WINDOW != 0:
        rows = jnp.arange(n, dtype=jnp.int32)
        src_row = (rows // (reps * m)) * m + rows % m
        return jnp.zeros((n_out, src.shape[1]), src.dtype).at[idx].set(jnp.take(src, src_row, axis=0))
    tiles = m // SC_WINDOW
    mesh = plsc.VectorSubcoreMesh(core_axis_name="core", subcore_axis_name="subcore")

    @pl.kernel(out_type=jax.ShapeDtypeStruct((n_out, src.shape[1]), src.dtype), mesh=mesh, scratch_types=[])
    def scatter_kernel(src_hbm, idx_hbm, out_hbm):
        def step(src_vmem, idx_vmem):
            pltpu.sync_copy(src_vmem, out_hbm.at[idx_vmem.at[0]])

        pltpu.emit_pipeline(
            step, grid=(n // SC_WINDOW,),
            in_specs=[pl.BlockSpec((SC_WINDOW, src.shape[1]),
                                   index_map=lambda i: ((i // (reps * tiles)) * tiles + i % tiles, 0)),
                      pl.BlockSpec((1, SC_WINDOW), index_map=lambda i: (0, i))],
            out_specs=[],
            core_axis_name=("core", "subcore"),
            dimension_semantics=(pltpu.PARALLEL,),
        )(src_hbm, idx_hbm)

    return scatter_kernel(src, idx.reshape(1, n))


PER_GROUP = N_EXPERTS // N_GROUPS
PICKED = -3e38


def _ln_rows(v, g, b):
    mu = jnp.mean(v, axis=-1, keepdims=True)
    c = v - mu
    var = jnp.mean(c * c, axis=-1, keepdims=True)
    return c * lax.rsqrt(var + LN_EPS) * g + b


def _first_max(v, ids, axes, sentinel):
    best = v
    for a in axes:
        best = jnp.max(best, axis=a, keepdims=True)
    first = jnp.where(v == best, ids, sentinel)
    for a in axes:
        first = jnp.min(first, axis=a, keepdims=True)
    return best, first


def _sum_axes(v, axes):
    for a in axes:
        v = jnp.sum(v, axis=a, keepdims=True)
    return v


def _moe_pre_body(x_ref, mix_ref, g_ref, b_ref, wr_ref, br_ref, wgu_ref, wdn_ref,
                  x1_ref, xp_ref, sh_ref, eidx_ref, gate_ref, rank_ref, cnt_ref, run_ref):
    f32, bf16 = jnp.float32, jnp.bfloat16
    tm = x_ref.shape[0]

    @pl.when(pl.program_id(0) == 0)
    def _():
        run_ref[...] = jnp.zeros(run_ref.shape, f32)

    x1 = _ln_rows(ALPHA * x_ref[...] + mix_ref[...], g_ref[...], b_ref[...])
    x1_ref[...] = x1
    x1b = x1.astype(bf16)
    xp_ref[0], xp_ref[1] = _pack_rows(x1)

    h = _dot(x1b, wgu_ref[...])
    d_sh = h.shape[1] // 2
    act = (jax.nn.silu(h[:, :d_sh]) * h[:, d_sh:]).astype(bf16)
    sh_ref[...] = _dot(act, wdn_ref[...])

    s = jax.nn.sigmoid(_dot_nt(wr_ref[...], x1b)).reshape(N_GROUPS, PER_GROUP, tm)
    sb = s + br_ref[...].reshape(N_GROUPS, PER_GROUP, 1)
    shape3 = (N_GROUPS, PER_GROUP, tm)
    pid = lax.broadcasted_iota(jnp.int32, shape3, 1)
    gid = lax.broadcasted_iota(jnp.int32, (N_GROUPS, 1, tm), 0)
    eid = lax.broadcasted_iota(jnp.int32, shape3, 0) * PER_GROUP + pid
    top1, i1 = _first_max(sb, pid, (1,), PER_GROUP)
    top2 = jnp.max(jnp.where(pid == i1, PICKED, sb), axis=1, keepdims=True)
    gscore = top1 + top2
    gsel = jnp.zeros((N_GROUPS, 1, tm), f32)
    for _ in range(TOPK_GROUPS):
        _, first = _first_max(gscore, gid, (0,), N_GROUPS)
        hit = gid == first
        gsel = jnp.where(hit, 1.0, gsel)
        gscore = jnp.where(hit, PICKED, gscore)
    cand = jnp.where(gsel > 0.0, sb, -1e30)
    firsts, gates = [], []
    picked = jnp.zeros(shape3, f32)
    for _ in range(TOP_K):
        _, first = _first_max(cand, eid, (0, 1), N_EXPERTS)
        hit = eid == first
        firsts.append(first)
        gates.append(_sum_axes(jnp.where(hit, s, 0.0), (0, 1)))
        picked = jnp.where(hit, 1.0, picked)
        cand = jnp.where(hit, PICKED, cand)
    gsum = gates[0]
    for gk in gates[1:]:
        gsum = gsum + gk
    earlier = (lax.broadcasted_iota(jnp.int32, (tm, tm), 0) < lax.broadcasted_iota(jnp.int32, (tm, tm), 1))
    picked2 = picked.reshape(N_EXPERTS, tm)
    rank = run_ref[...] + _dot(picked2.astype(bf16), jnp.where(earlier, 1.0, 0.0).astype(bf16))
    run_new = run_ref[...] + jnp.sum(picked2, axis=1, keepdims=True)
    run_ref[...] = run_new
    cnt_ref[...] = jnp.broadcast_to(run_new, cnt_ref.shape)
    rank3 = rank.reshape(shape3)
    for k in range(TOP_K):
        hit = eid == firsts[k]
        eidx_ref[k:k + 1, :] = firsts[k].reshape(1, tm)
        gate_ref[k:k + 1, :] = (gates[k] / gsum * ROUTE_SCALE).reshape(1, tm)
        rank_ref[k:k + 1, :] = _sum_axes(jnp.where(hit, rank3, 0.0), (0, 1)).reshape(1, tm).astype(jnp.int32)


def _moe_pre(x, mix, g, b, w_router, b_router, w_sh_gu, w_sh_down):
    m, d = x.shape
    bf16 = jnp.bfloat16
    tm = min(m, 512)
    row = lambda i: (i, 0)
    col = lambda i: (0, i)
    fixed = lambda i: (0, 0)
    d_sh2 = w_sh_gu.shape[1]
    return pl.pallas_call(
        _moe_pre_body,
        grid=(m // tm,),
        in_specs=[pl.BlockSpec((tm, d), row), pl.BlockSpec((tm, d), row),
                  pl.BlockSpec((1, d), fixed), pl.BlockSpec((1, d), fixed),
                  pl.BlockSpec((N_EXPERTS, d), fixed), pl.BlockSpec((N_EXPERTS, 1), fixed),
                  pl.BlockSpec((d, d_sh2), fixed), pl.BlockSpec((d_sh2 // 2, d), fixed)],
        out_specs=[pl.BlockSpec((tm, d), row), pl.BlockSpec((2, tm, PACK_W), lambda i: (0, i, 0)),
                   pl.BlockSpec((tm, d), row),
                   pl.BlockSpec((TOP_K, tm), col), pl.BlockSpec((TOP_K, tm), col), pl.BlockSpec((TOP_K, tm), col),
                   pl.BlockSpec((N_EXPERTS, LANE), fixed)],
        out_shape=[jax.ShapeDtypeStruct((m, d), jnp.float32), jax.ShapeDtypeStruct((2, m, PACK_W), jnp.int32),
                   jax.ShapeDtypeStruct((m, d), jnp.float32),
                   jax.ShapeDtypeStruct((TOP_K, m), jnp.int32), jax.ShapeDtypeStruct((TOP_K, m), jnp.float32),
                   jax.ShapeDtypeStruct((TOP_K, m), jnp.int32),
                   jax.ShapeDtypeStruct((N_EXPERTS, LANE), jnp.float32)],
        scratch_shapes=[pltpu.VMEM((N_EXPERTS, 1), jnp.float32)],
        compiler_params=pltpu.CompilerParams(dimension_semantics=("arbitrary",),
                                             vmem_limit_bytes=VMEM_LIMIT),
        name="moe_pre",
    )(x, mix, g.reshape(1, d), b.reshape(1, d), w_router.T.astype(bf16), b_router.reshape(N_EXPERTS, 1),
      w_sh_gu.astype(bf16), w_sh_down.astype(bf16))


def _moe_expert_body(exp_ref, first_ref, rows_ref, xs_ref, wgu_ref, wdn_ref, y_ref, wgu_bf, wdn_bf):
    i = pl.program_id(0)
    bf16 = jnp.bfloat16

    @pl.when(first_ref[i] == 1)
    def _():
        wgu_bf[...] = wgu_ref[0, 0].astype(bf16)
        wdn_bf[...] = wdn_ref[0, 0].astype(bf16)

    @pl.when(rows_ref[i] > 0)
    def _():
        live = lax.broadcasted_iota(jnp.int32, (xs_ref.shape[1], 1), 0) < rows_ref[i]
        h = None
        for hw in range(2):
            for q, xq in enumerate(_unpack_words(xs_ref[hw])):
                r0 = (2 * hw + q) * PACK_W
                part = _dot(jnp.where(live, xq, 0.0).astype(bf16), wgu_bf[r0:r0 + PACK_W, :])
                h = part if h is None else h + part
        d_e = h.shape[1] // 2
        act = (jax.nn.silu(h[:, :d_e]) * h[:, d_e:]).astype(bf16)
        y_ref[0], y_ref[1] = _pack_rows(_dot(act, wdn_bf[...]))

    @pl.when(rows_ref[i] == 0)
    def _():
        y_ref[...] = jnp.zeros(y_ref.shape, y_ref.dtype)


def _moe_experts(xs, blk_exp, blk_first, blk_rows, w_exp_gu, w_exp_down, layer, bm):
    n_slots = xs.shape[1]
    d = w_exp_gu.shape[2]
    n_blk = n_slots // bm
    d_e2 = w_exp_gu.shape[3]
    words = lambda i, e, f, a: (0, i, 0)
    grid_spec = pltpu.PrefetchScalarGridSpec(
        num_scalar_prefetch=3,
        grid=(n_blk,),
        in_specs=[pl.BlockSpec((2, bm, PACK_W), words),
                  pl.BlockSpec((1, 1, d, d_e2), lambda i, e, f, a: (layer, e[i], 0, 0)),
                  pl.BlockSpec((1, 1, d_e2 // 2, d), lambda i, e, f, a: (layer, e[i], 0, 0))],
        out_specs=pl.BlockSpec((2, bm, PACK_W), words),
        scratch_shapes=[pltpu.VMEM((d, d_e2), jnp.bfloat16), pltpu.VMEM((d_e2 // 2, d), jnp.bfloat16)])
    return pl.pallas_call(
        _moe_expert_body,
        grid_spec=grid_spec,
        out_shape=jax.ShapeDtypeStruct((2, n_slots, PACK_W), jnp.int32),
        compiler_params=pltpu.CompilerParams(dimension_semantics=("arbitrary",),
                                             vmem_limit_bytes=VMEM_LIMIT),
        name="moe_experts",
    )(blk_exp, blk_first, blk_rows, xs, w_exp_gu, w_exp_down)


def _combine_ln_body(x_ref, yg_ref, gt_ref, sh_ref, g_ref, b_ref, o_ref):
    gt = gt_ref[...]
    parts = []
    for hw in range(2):
        lo_acc = hi_acc = None
        for k in range(TOP_K):
            lo, hi = _unpack_words(yg_ref[hw, k])
            gk = gt[:, k:k + 1]
            lo_acc = lo * gk if lo_acc is None else lo_acc + lo * gk
            hi_acc = hi * gk if hi_acc is None else hi_acc + hi * gk
        parts += [lo_acc, hi_acc]
    routed = jnp.concatenate(parts, axis=1)
    o_ref[...] = _ln_rows(ALPHA * x_ref[...] + (routed + sh_ref[...]), g_ref[...], b_ref[...])


def _combine_ln(x, yg, gate_t, shared, g, b):
    m, d = x.shape
    tm = min(m, 256)
    row = lambda i: (i, 0)
    fixed = lambda i: (0, 0)
    return pl.pallas_call(
        _combine_ln_body,
        grid=(m // tm,),
        in_specs=[pl.BlockSpec((tm, d), row), pl.BlockSpec((2, TOP_K, tm, PACK_W), lambda i: (0, 0, i, 0)),
                  pl.BlockSpec((tm, TOP_K), row), pl.BlockSpec((tm, d), row),
                  pl.BlockSpec((1, d), fixed), pl.BlockSpec((1, d), fixed)],
        out_specs=pl.BlockSpec((tm, d), row),
        out_shape=jax.ShapeDtypeStruct((m, d), jnp.float32),
        compiler_params=pltpu.CompilerParams(dimension_semantics=("arbitrary",)),
        name="combine_ln",
    )(x, yg, gate_t, shared, g.reshape(1, d), b.reshape(1, d))


def _moe_layer(x, mix, ln1_g, ln1_b, ln2_g, ln2_b, w_router, b_router, w_exp_gu, w_exp_down, layer,
               w_sh_gu, w_sh_down):
    m, d = x.shape
    x1, xp, shared, eidx, gate8, rank8, counts = _moe_pre(x, mix, ln1_g, ln1_b, w_router, b_router,
                                                           w_sh_gu, w_sh_down)
    bm = 512 if m * TOP_K >= 512 * N_EXPERTS else MOE_BLK
    n_blk = (m * TOP_K) // bm + N_EXPERTS
    counts = counts[:, 0].astype(jnp.int32)
    padded = (counts + bm - 1) // bm * bm
    pad_end = jnp.cumsum(padded)
    pad_start = pad_end - padded
    start_of = jnp.sum(jnp.where(eidx[:, :, None] == jnp.arange(N_EXPERTS), pad_start, 0), axis=-1)
    dest = (start_of + rank8).reshape(-1)
    blk_start = jnp.arange(n_blk, dtype=jnp.int32) * bm
    blk_exp = jnp.minimum(jnp.sum(pad_end[None, :] <= blk_start[:, None], axis=1), N_EXPERTS - 1).astype(jnp.int32)
    blk_rows = jnp.clip(counts[blk_exp] - (blk_start - pad_start[blk_exp]), 0, bm).astype(jnp.int32)
    blk_first = jnp.concatenate([jnp.ones((1,), jnp.int32), (blk_exp[1:] != blk_exp[:-1]).astype(jnp.int32)])
    n_slots = n_blk * bm
    xs = _scatter_rows(xp.reshape(2 * m, PACK_W), jnp.concatenate([dest, dest + n_slots]), 2 * n_slots)
    y = _moe_experts(xs.reshape(2, n_slots, PACK_W), blk_exp, blk_first, blk_rows, w_exp_gu, w_exp_down, layer, bm)
    yg = _gather_rows(y.reshape(2 * n_slots, PACK_W), jnp.concatenate([dest, dest + n_slots]))
    return _combine_ln(x1, yg.reshape(2, TOP_K, m, PACK_W), gate8.T, shared, ln2_g, ln2_b)


def _trunk(x, pos, gla_state, nsa_cache, page_table, win_buf, conv_buf,
           w_in_ab, w_gla_gate, b_gla_gate, gla_norm_g, w_cmp_pool, w_out_ab,
           w_pw1, b_pw1, w_dw, b_dw, conv_ln_g, conv_ln_b, w_pw2, b_pw2,
           ln_g, ln_b, w_router, b_router, w_exp_gu, w_exp_down, w_sh_gu, w_sh_down):
    new_gla, new_rows, new_win, new_conv = [], [], [], []
    for layer in range(DEPTH):
        i = layer // 2
        if layer % 2 == 0:
            mix, s_a, rows, win = _ab_mixer(
                x, pos, w_in_ab[i], w_gla_gate[i], b_gla_gate[i], gla_norm_g[i], w_cmp_pool[i], w_out_ab[i],
                None if gla_state is None else gla_state[i],
                None if nsa_cache is None else nsa_cache[i], page_table,
                None if win_buf is None else win_buf[i])
            new_gla.append(s_a)
            new_rows.append(rows)
            new_win.append(win)
        else:
            mix, cb = _conv_module(x, None if conv_buf is None else conv_buf[i], w_pw1[i], b_pw1[i],
                                   w_dw[i], b_dw[i], conv_ln_g[i], conv_ln_b[i], w_pw2[i], b_pw2[i])
            new_conv.append(cb)
        bsz, t_, d = x.shape
        x = _moe_layer(x.reshape(-1, d), mix.reshape(-1, d), ln_g[layer, 0], ln_b[layer, 0],
                       ln_g[layer, 1], ln_b[layer, 1], w_router[layer], b_router[layer],
                       w_exp_gu, w_exp_down, layer, w_sh_gu[layer], w_sh_down[layer]).reshape(bsz, t_, d)
    return x, jnp.stack(new_gla), jnp.stack(new_rows), jnp.stack(new_win), jnp.stack(new_conv)


def kernel(x_prompt, x_sample, state_gla, cache_nsa_kv, state_nsa_win, state_conv, page_table,
           w_in_ab, w_gla_gate, b_gla_gate, gla_norm_g, w_cmp_pool, w_out_ab,
           w_pw1, b_pw1, w_dw, b_dw, conv_ln_g, conv_ln_b, w_pw2, b_pw2,
           ln_g, ln_b, w_router, b_router, w_exp_gu, w_exp_down, w_sh_gu, w_sh_down):
    weights = (w_in_ab, w_gla_gate, b_gla_gate, gla_norm_g, w_cmp_pool, w_out_ab,
               w_pw1, b_pw1, w_dw, b_dw, conv_ln_g, conv_ln_b, w_pw2, b_pw2,
               ln_g, ln_b, w_router, b_router, w_exp_gu, w_exp_down, w_sh_gu, w_sh_down)
    past_len = page_table.shape[1] * PAGE_SIZE
    pos_p = jnp.arange(x_prompt.shape[1])
    pos_s = past_len + jnp.arange(x_sample.shape[1])
    y_prompt, gla_p, rows_p, win_p, conv_p = _trunk(x_prompt, pos_p, None, None, None, None, None, *weights)
    y_sample, gla_s, rows_s, win_s, conv_s = _trunk(x_sample, pos_s, state_gla, cache_nsa_kv, page_table,
                                                    state_nsa_win, state_conv, *weights)
    return (y_prompt, y_sample, gla_p, gla_s, rows_p, rows_s, win_p, win_s, conv_p, conv_s)
```

```python
import functools
import math

import jax
import jax.numpy as jnp
import numpy as np
from jax import lax
from jax.experimental import pallas as pl
from jax.experimental.pallas import tpu as pltpu
from jax.experimental.pallas import tpu_sc as plsc

D_MODEL = 1024
DEPTH = 2
PAGE_SIZE = 128

GLA_HEADS = 4
GLA_DV = D_MODEL // 2 // GLA_HEADS
GLA_DK = GLA_DV // 2
GLA_RANK = 16
GLA_TAU = 16.0

NSA_HEADS = 8
NSA_KV_HEADS = 2
NSA_GROUP = NSA_HEADS // NSA_KV_HEADS
HEAD_DIM = D_MODEL // 2 // NSA_HEADS
CMP_BLK = 32
CMP_STRIDE = 16
SEL_BLK = 64
SEL_TOPN = 16
WINDOW = 512
Q_BLK = 128
FORCE_BONUS = 100.0
ROPE_DIM = HEAD_DIM // 4
ROPE_THETA = 500000.0

GLA_SIZES = (GLA_HEADS * GLA_DK, GLA_HEADS * GLA_DK, GLA_HEADS * GLA_DV, GLA_HEADS * GLA_DV, GLA_RANK)
NSA_SIZES = (NSA_HEADS * HEAD_DIM, 6 * NSA_KV_HEADS * HEAD_DIM, 3 * NSA_HEADS)

CONV_W = 31
D_CONV = D_MODEL

N_EXPERTS = 64
N_GROUPS = 8
TOPK_GROUPS = 4
TOP_K = 8
D_EXPERT = 256
ROUTE_SCALE = 2.5
MOE_BLK = 128

ALPHA = (2 * DEPTH) ** 0.25
LN_EPS = 1e-5

LANE = 128
SUBLANES = 8
V7X_VMEM_BYTES = 64 * 1024 * 1024
VMEM_LIMIT = V7X_VMEM_BYTES * 3 // 4


def _dot(a, b):
    return jnp.dot(a, b, preferred_element_type=jnp.float32)


def _dot_nt(a, b):
    return lax.dot_general(a, b, (((1,), (1,)), ((), ())), preferred_element_type=jnp.float32)


def _mm_body(x_ref, w_ref, o_ref):
    o_ref[...] = _dot(x_ref[...].astype(jnp.bfloat16), w_ref[...].astype(jnp.bfloat16))


def _mm(x, w, keep_pad=False):
    m, k = x.shape
    n = w.shape[1]
    n_pad = -(-n // LANE) * LANE
    w = w.astype(jnp.bfloat16)
    if n_pad != n:
        w = jnp.pad(w, ((0, 0), (0, n_pad - n)))
    tm = min(m, 512)
    out = pl.pallas_call(
        _mm_body,
        grid=(m // tm,),
        in_specs=[pl.BlockSpec((tm, k), lambda i: (i, 0)),
                  pl.BlockSpec((k, n_pad), lambda i: (0, 0))],
        out_specs=pl.BlockSpec((tm, n_pad), lambda i: (i, 0)),
        out_shape=jax.ShapeDtypeStruct((m, n_pad), jnp.float32),
        compiler_params=pltpu.CompilerParams(dimension_semantics=("arbitrary",),
                                             vmem_limit_bytes=VMEM_LIMIT),
        name="mm",
    )(x, w)
    return out if keep_pad or n_pad == n else out[:, :n]


def _mm_pair_body(a_ref, b_ref, w_ref, o_ref):
    ka = a_ref.shape[1]
    o_ref[...] = (_dot(a_ref[...].astype(jnp.bfloat16), w_ref[0:ka, :])
                  + _dot(b_ref[...].astype(jnp.bfloat16), w_ref[ka:, :]))


def _mm_pair(a, b, w):
    m, ka = a.shape
    kb = b.shape[1]
    n = w.shape[1]
    tm = min(m, 512)
    return pl.pallas_call(
        _mm_pair_body,
        grid=(m // tm,),
        in_specs=[pl.BlockSpec((tm, ka), lambda i: (i, 0)), pl.BlockSpec((tm, kb), lambda i: (i, 0)),
                  pl.BlockSpec((ka + kb, n), lambda i: (0, 0))],
        out_specs=pl.BlockSpec((tm, n), lambda i: (i, 0)),
        out_shape=jax.ShapeDtypeStruct((m, n), jnp.float32),
        compiler_params=pltpu.CompilerParams(dimension_semantics=("arbitrary",)),
        name="mm_pair",
    )(a, b, w.astype(jnp.bfloat16))


def _partial_rope(x, pos):
    half = ROPE_DIM // 2
    inv_freq = jnp.power(ROPE_THETA, -jnp.arange(half, dtype=jnp.float32) / half)
    ang = pos.astype(jnp.float32)[:, None] * inv_freq
    ang = ang.reshape(ang.shape[0], *([1] * (x.ndim - 3)), half)
    cos, sin = jnp.cos(ang), jnp.sin(ang)
    x1 = x[..., :half]
    x2 = x[..., half:ROPE_DIM]
    rot = jnp.concatenate([x1 * cos - x2 * sin, x2 * cos + x1 * sin], -1)
    return jnp.concatenate([rot, x[..., ROPE_DIM:]], -1)


NSA_ROWS = NSA_GROUP * Q_BLK
SEL_KT = 2048
N_SELB = 128
MASKED = -1e9
WIN_KEYS = WINDOW + Q_BLK
KK_W = 2 * HEAD_DIM + N_SELB


def _nsa_prompt_body(qr_ref, qo_ref, kc_ref, vct_ref, kk_ref, vvt_ref, g_ref, o_ref,
                     imp_ref, m_ref, l_ref, acc_ref, oct_ref, selb_ref):
    f32, bf16 = jnp.float32, jnp.bfloat16
    qb = pl.program_id(2)
    q0 = qb * Q_BLK
    qr_t = qr_ref[0, 0, 0]
    qo_t = qo_ref[0, 0, 0]
    n_cmp = kc_ref.shape[2]

    ratio = SEL_BLK // CMP_STRIDE
    chunk = min(Q_BLK, n_cmp)
    n_chunks = n_cmp // chunk

    def compressed_and_select(n_act):
        nc = n_act * chunk
        nb = nc // ratio
        s_c = _dot(kc_ref[0, 0, 0:nc, :], qr_t)
        n_idx = lax.broadcasted_iota(jnp.int32, (nc, NSA_ROWS), 0)
        qpos_c = q0 + (lax.broadcasted_iota(jnp.int32, (nc, NSA_ROWS), 1) & (Q_BLK - 1))
        cmask = (n_idx * CMP_STRIDE + (CMP_BLK - 1)) <= qpos_c
        s_c = jnp.where(cmask, s_c, MASKED)
        m_c = jnp.max(s_c, axis=0, keepdims=True)
        p_c = jnp.where(cmask, jnp.exp(s_c - m_c), 0.0)
        p_c = p_c / jnp.maximum(jnp.sum(p_c, axis=0, keepdims=True), 1e-30)
        oct_ref[...] = _dot(vct_ref[0, 0, :, 0:nc], p_c.astype(bf16))
        imp = (p_c[:, 0:Q_BLK] + p_c[:, Q_BLK:2 * Q_BLK]) + p_c[:, 2 * Q_BLK:3 * Q_BLK] + p_c[:, 3 * Q_BLK:]
        imp_ref[0:8, :] = jnp.zeros((8, Q_BLK), f32)
        imp_ref[8:8 + nc, :] = imp
        imp_s = imp_ref[pl.ds(7, nb, stride=ratio), :]
        for r in range(ratio):
            imp_s = imp_s + imp_ref[pl.ds(8 + r, nb, stride=ratio), :]
        blk = lax.broadcasted_iota(jnp.int32, (nb, Q_BLK), 0)
        qpos_s = q0 + lax.broadcasted_iota(jnp.int32, (nb, Q_BLK), 1)
        cur = lax.shift_right_logical(qpos_s, int(math.log2(SEL_BLK)))
        valid = blk * SEL_BLK <= qpos_s
        forced = (blk == 0) | (blk == cur) | (blk == cur - 1)
        score = jnp.where(valid, imp_s + jnp.where(forced, FORCE_BONUS, 0.0), -1e30)
        picked = jnp.zeros((nb, Q_BLK), f32)
        for _ in range(SEL_TOPN):
            best = jnp.max(score, axis=0, keepdims=True)
            first = jnp.min(jnp.where(score == best, blk, nb), axis=0, keepdims=True)
            hit = blk == first
            picked = jnp.where(hit, 1.0, picked)
            score = jnp.where(hit, -3e38, score)
        sel = jnp.where(valid, picked, 0.0)
        if nb < N_SELB:
            sel = jnp.concatenate([sel, jnp.zeros((N_SELB - nb, Q_BLK), f32)], axis=0)
        sel = ((sel - 1.0) * (-MASKED)).astype(bf16)
        selb_ref[...] = jnp.concatenate([sel] * NSA_GROUP, axis=1)

    need = jnp.minimum((q0 + Q_BLK - CMP_BLK) // (CMP_STRIDE * chunk) + 1, n_chunks)
    for n_act in range(1, n_chunks + 1):
        pl.when(need == n_act)(functools.partial(compressed_and_select, n_act))
    o_ct = oct_ref[...]
    selb_t = selb_ref[...]

    zeros_q = jnp.zeros((HEAD_DIM, NSA_ROWS), bf16)
    q_sel = jnp.concatenate([qo_t, zeros_q, selb_t], axis=0)
    q_win = jnp.concatenate([zeros_q, qo_t, jnp.zeros((N_SELB, NSA_ROWS), bf16)], axis=0)
    qpos_r = q0 + (lax.broadcasted_iota(jnp.int32, (1, NSA_ROWS), 1) & (Q_BLK - 1))

    def v_tiles(first, count):
        return jnp.concatenate([vvt_ref[0, 0, first + j] for j in range(count)], axis=1)

    m_ref[...] = jnp.full(m_ref.shape, MASKED, f32)
    l_ref[...] = jnp.zeros(l_ref.shape, f32)
    acc_ref[...] = jnp.zeros(acc_ref.shape, f32)

    def sel_tile(k0, kt, causal):
        s = _dot(kk_ref[0, 0, pl.ds(k0, kt), :], q_sel)
        if causal:
            kpos = k0 + lax.broadcasted_iota(jnp.int32, (kt, NSA_ROWS), 0)
            s = jnp.where(kpos <= qpos_r, s, MASKED)
        m_old = m_ref[...]
        m_new = jnp.maximum(m_old, jnp.max(s, axis=0, keepdims=True))
        alpha = jnp.exp(m_old - m_new)
        p = jnp.exp(s - m_new)
        l_ref[...] = alpha * l_ref[...] + jnp.sum(p, axis=0, keepdims=True)
        vt = v_tiles(k0 // Q_BLK, kt // Q_BLK)
        acc_ref[...] = alpha * acc_ref[...] + _dot(vt, p.astype(bf16))
        m_ref[...] = m_new

    n_full = q0 // SEL_KT

    def full_step(t, c):
        sel_tile(pl.multiple_of(t * SEL_KT, SEL_KT), SEL_KT, False)
        return c

    lax.fori_loop(0, n_full, full_step, 0)
    d0 = pl.multiple_of(n_full * SEL_KT, SEL_KT)
    short = q0 + Q_BLK - n_full * SEL_KT <= SEL_KT // 2

    @pl.when(short)
    def _():
        sel_tile(d0, SEL_KT // 2, True)

    @pl.when(jnp.logical_not(short))
    def _():
        sel_tile(d0, SEL_KT, True)
    o_st = acc_ref[0:HEAD_DIM, :] / l_ref[...]

    w0 = pl.multiple_of(jnp.maximum(q0 - WINDOW, 0), Q_BLK)
    s_w = _dot(kk_ref[0, 0, pl.ds(w0, WIN_KEYS), :], q_win)
    kpos_w = w0 + lax.broadcasted_iota(jnp.int32, (WIN_KEYS, NSA_ROWS), 0)
    s_w = jnp.where((kpos_w <= qpos_r) & (kpos_w > qpos_r - WINDOW), s_w, MASKED)
    p_w = jnp.exp(s_w - jnp.max(s_w, axis=0, keepdims=True))
    l_w = jnp.sum(p_w, axis=0, keepdims=True)
    acc_w = _dot(v_tiles(w0 // Q_BLK, WIN_KEYS // Q_BLK), p_w.astype(bf16))
    o_wt = acc_w[HEAD_DIM:2 * HEAD_DIM, :] / l_w

    g = g_ref[0, 0, 0]
    out_t = g[0:1, :] * o_ct + g[1:2, :] * o_st + g[2:3, :] * o_wt
    o_ref[0] = jnp.concatenate([out_t[:, g_ * Q_BLK:(g_ + 1) * Q_BLK] for g_ in range(NSA_GROUP)], axis=0).T


def _nsa_prompt(qr, qo, gt, kc_p, vct, kk, vvt):
    bsz, _, nqb = qr.shape[:3]
    t_ = nqb * Q_BLK
    n_cmp = kc_p.shape[2]
    per_blk = lambda b, h, i: (b, h, i, 0, 0)
    per_head = lambda b, h, i: (b, h, 0, 0)
    return pl.pallas_call(
        _nsa_prompt_body,
        grid=(bsz, NSA_KV_HEADS, nqb),
        in_specs=[pl.BlockSpec((1, 1, 1, HEAD_DIM, NSA_ROWS), per_blk),
                  pl.BlockSpec((1, 1, 1, HEAD_DIM, NSA_ROWS), per_blk),
                  pl.BlockSpec((1, 1, n_cmp, HEAD_DIM), per_head),
                  pl.BlockSpec((1, 1, HEAD_DIM, n_cmp), per_head),
                  pl.BlockSpec((1, 1, t_, KK_W), per_head),
                  pl.BlockSpec((1, 1, nqb, 2 * HEAD_DIM, Q_BLK), lambda b, h, i: (b, h, 0, 0, 0)),
                  pl.BlockSpec((1, 1, 1, 3, NSA_ROWS), per_blk)],
        out_specs=pl.BlockSpec((1, Q_BLK, NSA_GROUP * HEAD_DIM), lambda b, h, i: (b, i, h)),
        out_shape=jax.ShapeDtypeStruct((bsz, t_, NSA_HEADS * HEAD_DIM), jnp.float32),
        scratch_shapes=[pltpu.VMEM((8 + n_cmp, Q_BLK), jnp.float32),
                        pltpu.VMEM((1, NSA_ROWS), jnp.float32),
                        pltpu.VMEM((1, NSA_ROWS), jnp.float32),
                        pltpu.VMEM((2 * HEAD_DIM, NSA_ROWS), jnp.float32),
                        pltpu.VMEM((HEAD_DIM, NSA_ROWS), jnp.float32),
                        pltpu.VMEM((N_SELB, NSA_ROWS), jnp.bfloat16)],
        compiler_params=pltpu.CompilerParams(
            dimension_semantics=("arbitrary", "arbitrary", "arbitrary"),
            vmem_limit_bytes=VMEM_LIMIT),
        name="nsa_prompt",
    )(qr, qo, kc_p, vct, kk, vvt, gt)


GLA_SUB = 16
GLA_UNROLL = 8
GLA_TILE = 256
GLA_QK = GLA_HEADS * GLA_DK
GLA_V = GLA_HEADS * GLA_DV


def _dot_tn(a, b):
    return lax.dot_general(a, b, (((0,), (0,)), ((), ())), preferred_element_type=jnp.float32)


def _gla_body(q_ref, k_ref, v_ref, gr_ref, glr_ref, wg_ref, bg_ref, ng_ref, s0_ref, exp_ref,
              o_ref, sfin_ref, st_ref, b_ref, qd_ref, *, t_valid):
    f32, bf16 = jnp.float32, jnp.bfloat16
    tt = q_ref.shape[1]
    ti = pl.program_id(1)

    @pl.when(ti == 0)
    def _():
        st_ref[...] = s0_ref[0]

    row = lax.broadcasted_iota(jnp.int32, (tt, 1), 0)
    z = _dot(glr_ref[0][:, :GLA_RANK].astype(bf16), wg_ref[...]) + bg_ref[...]
    la = (jnp.minimum(z, 0.0) - jnp.log1p(jnp.exp(-jnp.abs(z)))) * (1.0 / GLA_TAU)
    la = jnp.where(ti * tt + row < t_valid, la, 0.0)
    seg = row & (GLA_SUB - 1)
    b = la
    for s in (1, 2, 4, 8):
        b = b + jnp.where(seg >= s, pltpu.roll(b, s, axis=0), 0.0)
    q = q_ref[0] * (GLA_DK ** -0.5)
    k = k_ref[0]
    v = v_ref[0]
    o = _dot((q * k).astype(bf16), exp_ref[...]) * v
    for d in range(1, GLA_SUB):
        decay = jnp.exp(jnp.minimum(b - pltpu.roll(b, d, axis=0), 0.0))
        w = jnp.where(seg >= d, q * pltpu.roll(k, d, axis=0) * decay, 0.0)
        o = o + _dot(w.astype(bf16), exp_ref[...]) * pltpu.roll(v, d, axis=0)
    o_ref[0] = o
    b_ref[...] = b
    qd_ref[...] = (q * jnp.exp(b)).astype(bf16)

    def block_step(c, carry):
        rows = pl.ds(pl.multiple_of(c * GLA_SUB, GLA_SUB), GLA_SUB)
        qd = qd_ref[rows, :]
        bc = b_ref[rows, :]
        bl = bc[GLA_SUB - 1:GLA_SUB, :]
        kc = (k_ref[0, rows, :] * jnp.exp(bl - bc)).astype(bf16)
        keep = jnp.exp(bl)
        vb = v_ref[0, rows, :].astype(bf16)
        outs = []
        for h in range(GLA_HEADS):
            dk = slice(h * GLA_DK, (h + 1) * GLA_DK)
            dv = slice(h * GLA_DV, (h + 1) * GLA_DV)
            st = st_ref[dv, :]
            outs.append(_dot_nt(qd[:, dk], st.astype(bf16)))
            st_ref[dv, :] = st * keep[:, dk] + _dot_tn(vb[:, dv], kc[:, dk])
        o_ref[0, rows, :] += jnp.concatenate(outs, axis=1)
        return carry

    lax.fori_loop(0, tt // GLA_SUB, block_step, 0, unroll=GLA_UNROLL)
    sfin_ref[0] = st_ref[...]
    gr = gr_ref[0]
    gate = gr * jax.nn.sigmoid(gr)
    for h in range(GLA_HEADS):
        cols = slice(h * GLA_DV, (h + 1) * GLA_DV)
        oh = o_ref[0, :, cols]
        ms = jnp.mean(oh * oh, axis=-1, keepdims=True)
        o_ref[0, :, cols] = oh * lax.rsqrt(ms + LN_EPS) * ng_ref[...] * gate[:, cols]


def _gla(h, w_gla_gate, b_gla_gate, gla_norm_g, gla_state):
    bsz, t_, n_in = h.shape
    tp = -(-t_ // GLA_SUB) * GLA_SUB
    if tp != t_:
        h = jnp.pad(h, ((0, 0), (0, tp - t_), (0, 0)))
    tt = min(tp, GLA_TILE)
    expand =np.repeat(np.repeat(np.eye(GLA_HEADS, dtype=np.float32), GLA_DK, 0), GLA_DV, 1)
    if gla_state is None:
        s0 = jnp.zeros((bsz, GLA_V, GLA_DK), jnp.float32)
    else:
        s0 = gla_state.transpose(0, 1, 3, 2).reshape(bsz, GLA_V, GLA_DK)
    tile = lambda width, blk: pl.BlockSpec((1, tt, width), lambda b, i: (b, i, blk))
    fixed2 = lambda shape: pl.BlockSpec(shape, lambda b, i: (0, 0))
    per_b = pl.BlockSpec((1, GLA_V, GLA_DK), lambda b, i: (b, 0, 0))
    o, s_t = pl.pallas_call(
        functools.partial(_gla_body, t_valid=t_),
        grid=(bsz, tp // tt),
        in_specs=[tile(GLA_QK, 0), tile(GLA_QK, 1), tile(GLA_V, 1), tile(GLA_V, 2),
                  tile(LANE, (2 * GLA_QK + 2 * GLA_V + NSA_SIZES[0] + NSA_SIZES[1]) // LANE),
                  fixed2((GLA_RANK, GLA_QK)), fixed2((1, GLA_QK)), fixed2((1, GLA_DV)), per_b,
                  fixed2((GLA_QK, GLA_V))],
        out_specs=[pl.BlockSpec((1, tt, GLA_V), lambda b, i: (b, i, 0)), per_b],
        out_shape=[jax.ShapeDtypeStruct((bsz, tp, GLA_V), jnp.float32),
                   jax.ShapeDtypeStruct((bsz, GLA_V, GLA_DK), jnp.float32)],
        scratch_shapes=[pltpu.VMEM((GLA_V, GLA_DK), jnp.float32), pltpu.VMEM((tt, GLA_QK), jnp.float32),
                        pltpu.VMEM((tt, GLA_QK), jnp.bfloat16)],
        compiler_params=pltpu.CompilerParams(dimension_semantics=("arbitrary", "arbitrary"),
                                             vmem_limit_bytes=VMEM_LIMIT),
        name="gla",
    )(h, h, h, h, h, w_gla_gate.astype(jnp.bfloat16), b_gla_gate.reshape(1, GLA_QK),
      gla_norm_g.reshape(1, GLA_DV), s0, jnp.asarray(expand, jnp.bfloat16))
    return o[:, :t_], s_t.reshape(bsz, GLA_HEADS, GLA_DV, GLA_DK).transpose(0, 1, 3, 2)


COL_NQ = 2 * GLA_QK + 2 * GLA_V
COL_NKV = COL_NQ + NSA_SIZES[0]
COL_TAIL = COL_NKV + NSA_SIZES[1]
TAIL_GATE = GLA_RANK
_ORIG = np.cumsum((0,) + GLA_SIZES + NSA_SIZES)
IN_AB_PERM = np.concatenate([np.arange(_ORIG[0], _ORIG[4]), np.arange(_ORIG[5], _ORIG[7]),
                             np.arange(_ORIG[4], _ORIG[5]), np.arange(_ORIG[7], _ORIG[8])])
SUBS = Q_BLK // CMP_STRIDE


def _nsa_prep_body(nq_ref, kv0_ref, kv1_ref, kv2_ref, tail_ref, rc_ref, ru_ref, rd_ref, pool_ref,
                   rows_ref, win_ref, kk_ref, vvt_ref, qr_ref, qo_ref, g_ref, pooled_ref):
    bf16 = jnp.bfloat16
    q0 = pl.program_id(1) * Q_BLK
    kv_w = NSA_KV_HEADS * HEAD_DIM

    def rope(x):
        reps = x.shape[1] // LANE
        wide = lambda r: jnp.concatenate([r[...]] * reps, axis=1) if reps > 1 else r[...]
        half = ROPE_DIM // 2
        return (x * wide(rc_ref) + pltpu.roll(x, half, axis=1) * wide(ru_ref)
                + pltpu.roll(x, x.shape[1] - half, axis=1) * wide(rd_ref))

    kv0, kv1, kv2 = kv0_ref[0], kv1_ref[0], kv2_ref[0]
    k_sel, v_sel = rope(kv1[:, :kv_w]), kv1[:, kv_w:]
    k_win, v_win = rope(kv2[:, :kv_w]), kv2[:, kv_w:]
    rows_ref[0] = jnp.concatenate([kv0, k_sel, v_sel], axis=1)
    win_ref[0] = jnp.concatenate([k_win, v_win], axis=1)
    blk_id = lax.shift_right_logical(q0 + lax.broadcasted_iota(jnp.int32, (Q_BLK, N_SELB), 0),
                                     int(math.log2(SEL_BLK)))
    onehot = jnp.where(lax.broadcasted_iota(jnp.int32, (Q_BLK, N_SELB), 1) == blk_id, 1.0, 0.0).astype(bf16)
    q = nq_ref[0] * (HEAD_DIM ** -0.5)
    q_rot = rope(q)
    gates_t = jax.nn.sigmoid(tail_ref[0]).T
    for h in range(NSA_KV_HEADS):
        hs = slice(h * HEAD_DIM, (h + 1) * HEAD_DIM)
        kk_ref[0, h] = jnp.concatenate([k_sel[:, hs].astype(bf16), k_win[:, hs].astype(bf16), onehot], axis=1)
        vvt_ref[0, h, 0] = jnp.concatenate([v_sel[:, hs], v_win[:, hs]], axis=1).T.astype(bf16)
        gw = NSA_GROUP * HEAD_DIM
        for src, dst in ((q, qr_ref), (q_rot, qo_ref)):
            t = src[:, h * gw:(h + 1) * gw].T
            dst[0, h, 0] = jnp.concatenate([t[g * HEAD_DIM:(g + 1) * HEAD_DIM] for g in range(NSA_GROUP)],
                                           axis=1).astype(bf16)
        base = TAIL_GATE + h * NSA_GROUP * 3
        g_ref[0, h, 0] = jnp.concatenate(
            [jnp.concatenate([gates_t[base + 3 * g + c:base + 3 * g + c + 1] for g in range(NSA_GROUP)], axis=1)
             for c in range(3)], axis=0)
    kc_in, vc_in = kv0[:, :kv_w].astype(bf16), kv0[:, kv_w:].astype(bf16)
    pooled_ref[0] = jnp.concatenate([_dot(pool_ref[0], kc_in), _dot(pool_ref[1], kc_in),
                                     _dot(pool_ref[2], vc_in), _dot(pool_ref[3], vc_in)], axis=1)


def _nsa_prep(h, pos, w_cmp_pool):
    bsz, t_, _ = h.shape
    nqb = t_ // Q_BLK
    bf16 = jnp.bfloat16
    half = ROPE_DIM // 2
    inv_freq = jnp.power(ROPE_THETA, -jnp.arange(half, dtype=jnp.float32) / half)
    ang = pos.astype(jnp.float32)[:, None] * inv_freq
    cos, sin = jnp.cos(ang), jnp.sin(ang)
    rest = HEAD_DIM - ROPE_DIM
    z8, zr = jnp.zeros((t_, half), jnp.float32), jnp.zeros((t_, rest), jnp.float32)
    two = lambda a: jnp.concatenate([a, a], axis=1)
    rc = two(jnp.concatenate([cos, cos, jnp.ones((t_, rest), jnp.float32)], axis=1))
    ru = two(jnp.concatenate([z8, sin, zr], axis=1))
    rd = two(jnp.concatenate([-sin, z8, zr], axis=1))
    pool = _pool_matrices(w_cmp_pool)
    kv_w = NSA_KV_HEADS * HEAD_DIM
    col = lambda width, off: pl.BlockSpec((1, Q_BLK, width), lambda b, i: (b, i, off // width))
    rows_t = pl.BlockSpec((Q_BLK, LANE), lambda b, i: (i, 0))
    head4 = lambda r, c: pl.BlockSpec((1, NSA_KV_HEADS, 1, r, c), lambda b, i: (b, 0, i, 0, 0))
    return pl.pallas_call(
        _nsa_prep_body,
        grid=(bsz, nqb),
        in_specs=[col(NSA_SIZES[0], COL_NQ), col(2 * kv_w, COL_NKV), col(2 * kv_w, COL_NKV + 2 * kv_w),
                  col(2 * kv_w, COL_NKV + 4 * kv_w), col(LANE, COL_TAIL), rows_t, rows_t, rows_t,
                  pl.BlockSpec((4, SUBS, Q_BLK), lambda b, i: (0, 0, 0))],
        out_specs=[pl.BlockSpec((1, Q_BLK, 4 * kv_w), lambda b, i: (b, i, 0)),
                   pl.BlockSpec((1, Q_BLK, 2 * kv_w), lambda b, i: (b, i, 0)),
                   pl.BlockSpec((1, NSA_KV_HEADS, Q_BLK, KK_W), lambda b, i: (b, 0, i, 0)),
                   head4(2 * HEAD_DIM, Q_BLK), head4(HEAD_DIM, NSA_ROWS), head4(HEAD_DIM, NSA_ROWS),
                   head4(3, NSA_ROWS),
                   pl.BlockSpec((1, SUBS, 4 * kv_w), lambda b, i: (b, i, 0))],
        out_shape=[jax.ShapeDtypeStruct((bsz, t_, 4 * kv_w), jnp.float32),
                   jax.ShapeDtypeStruct((bsz, t_, 2 * kv_w), jnp.float32),
                   jax.ShapeDtypeStruct((bsz, NSA_KV_HEADS, t_, KK_W), bf16),
                   jax.ShapeDtypeStruct((bsz, NSA_KV_HEADS, nqb, 2 * HEAD_DIM, Q_BLK), bf16),
                   jax.ShapeDtypeStruct((bsz, NSA_KV_HEADS, nqb, HEAD_DIM, NSA_ROWS), bf16),
                   jax.ShapeDtypeStruct((bsz, NSA_KV_HEADS, nqb, HEAD_DIM, NSA_ROWS), bf16),
                   jax.ShapeDtypeStruct((bsz, NSA_KV_HEADS, nqb, 3, NSA_ROWS), jnp.float32),
                   jax.ShapeDtypeStruct((bsz, t_ // CMP_STRIDE, 4 * kv_w), jnp.float32)],
        compiler_params=pltpu.CompilerParams(dimension_semantics=("arbitrary", "arbitrary")),
        name="nsa_prep",
    )(h, h, h, h, h, rc, ru, rd, pool)


PAGE_GROUP = 32
DEC_KEYS = PAGE_GROUP * PAGE_SIZE
POOL_ROWS = 2048
NEW_PAD = 8
KV_W = NSA_KV_HEADS * HEAD_DIM


def _dec_pool_body(pt_ref, *refs):
    page_refs, pool_ref, out_ref = refs[:PAGE_GROUP], refs[PAGE_GROUP], refs[PAGE_GROUP + 1]
    bf16 = jnp.bfloat16
    pages = [pr[0] for pr in page_refs]
    per = POOL_ROWS // PAGE_SIZE
    cols = []
    for g0 in range(0, PAGE_GROUP, per):
        kc_t = jnp.concatenate([p[:KV_W] for p in pages[g0:g0 + per]], axis=1).astype(bf16)
        vc_t = jnp.concatenate([p[KV_W:] for p in pages[g0:g0 + per]], axis=1).astype(bf16)
        cols.append(jnp.concatenate([_dot(kc_t, pool_ref[0]), _dot(kc_t, pool_ref[1]),
                                     _dot(vc_t, pool_ref[2]), _dot(vc_t, pool_ref[3])], axis=0))
    out_ref[0] = jnp.concatenate(cols, axis=1)


def _page_specs(n_pages, col_blk):
    def spec(i):
        return pl.BlockSpec((1, 2 * KV_W, PAGE_SIZE),
                            lambda b, j, pt: (pt[b * n_pages + j * PAGE_GROUP + i], col_blk, 0))
    return [spec(i) for i in range(PAGE_GROUP)]


def _dec_pool(cache, page_table, pool):
    bsz, n_pages = page_table.shape
    grid_spec = pltpu.PrefetchScalarGridSpec(
        num_scalar_prefetch=1, grid=(bsz, n_pages // PAGE_GROUP),
        in_specs=_page_specs(n_pages, 0) + [pl.BlockSpec(pool.shape, lambda b, j, pt: (0, 0, 0))],
        out_specs=pl.BlockSpec((1, 4 * KV_W, PAGE_GROUP * SUBS), lambda b, j, pt: (b, 0, j)))
    return pl.pallas_call(
        _dec_pool_body, grid_spec=grid_spec,
        out_shape=jax.ShapeDtypeStruct((bsz, 4 * KV_W, n_pages * SUBS), jnp.float32),
        compiler_params=pltpu.CompilerParams(dimension_semantics=("arbitrary", "arbitrary")),
        name="nsa_dec_pool",
    )(page_table.reshape(-1), *([cache] * PAGE_GROUP), pool)


def _dec_select_body(qr_ref, kct_ref, vc_ref, band_ref, oc_ref, selb_ref, *, qpos0, n_q, n_pick, n_blk):
    f32, bf16 = jnp.float32, jnp.bfloat16
    n_cmp = kct_ref.shape[3]
    rows = NSA_GROUP * n_q
    for sq, h in [(a, b) for a in range(qr_ref.shape[0]) for b in range(NSA_KV_HEADS)]:
        s_c = _dot(qr_ref[sq, h], kct_ref[sq, h])
        n_idx = lax.broadcasted_iota(jnp.int32, (rows, n_cmp), 1)
        qpos = qpos0 + (lax.broadcasted_iota(jnp.int32, (rows, n_cmp), 0) % n_q)
        cmask = (n_idx * CMP_STRIDE + (CMP_BLK - 1)) <= qpos
        s_c = jnp.where(cmask, s_c, MASKED)
        p_c = jnp.where(cmask, jnp.exp(s_c - jnp.max(s_c, axis=1, keepdims=True)), 0.0)
        p_c = p_c / jnp.maximum(jnp.sum(p_c, axis=1, keepdims=True), 1e-30)
        oc_ref[sq, h] = _dot(p_c.astype(bf16), vc_ref[sq, h])
        imp = p_c[0:n_q]
        for g in range(1, NSA_GROUP):
            imp = imp + p_c[g * n_q:(g + 1) * n_q]
        imp_s = jnp.zeros((n_q, N_SELB), f32)
        rem = imp
        for _ in range(3):
            part = rem.astype(bf16)
            imp_s = imp_s + _dot(part, band_ref[...])
            rem = rem - part.astype(f32)
        blk = lax.broadcasted_iota(jnp.int32, (n_q, N_SELB), 1)
        qpos_s = qpos0 + lax.broadcasted_iota(jnp.int32, (n_q, N_SELB), 0)
        cur = lax.shift_right_logical(qpos_s, int(math.log2(SEL_BLK)))
        valid = (blk * SEL_BLK <= qpos_s) & (blk < n_blk)
        forced = (blk == 0) | (blk == cur) | (blk == cur - 1)
        score = jnp.where(valid, imp_s + jnp.where(forced, FORCE_BONUS, 0.0), -1e30)
        picked = jnp.zeros((n_q, N_SELB), f32)
        for _ in range(n_pick):
            best = jnp.max(score, axis=1, keepdims=True)
            first = jnp.min(jnp.where(score == best, blk, N_SELB), axis=1, keepdims=True)
            hit = blk == first
            picked = jnp.where(hit, 1.0, picked)
            score = jnp.where(hit, -3e38, score)
        selb_ref[sq, h] = (jnp.where(valid, picked, 0.0) - 1.0) * (-MASKED)


def _dec_select(qr, kct, vc, n_q, qpos0, n_pick, n_blk):
    bsz = qr.shape[0]
    rows = NSA_GROUP * n_q
    n_cmp = kct.shape[3]
    ratio = SEL_BLK // CMP_STRIDE
    c_idx, j_idx = np.arange(n_cmp)[:, None], np.arange(N_SELB)[None, :]
    band = jnp.asarray(((c_idx >= ratio * j_idx - 1) & (c_idx <= ratio * j_idx + ratio - 1)), jnp.bfloat16)
    per_step = next(c for c in (4, 2, 1) if bsz % c == 0)
    per_b = lambda *tail: pl.BlockSpec((per_step, NSA_KV_HEADS) + tail, lambda b: (b, 0, 0, 0))
    return pl.pallas_call(
        functools.partial(_dec_select_body, qpos0=qpos0, n_q=n_q, n_pick=n_pick, n_blk=n_blk),
        grid=(bsz // per_step,),
        in_specs=[per_b(rows, HEAD_DIM), per_b(HEAD_DIM, n_cmp), per_b(n_cmp, HEAD_DIM),
                  pl.BlockSpec((n_cmp, N_SELB), lambda b: (0, 0))],
        out_specs=[per_b(rows, HEAD_DIM), per_b(n_q, N_SELB)],
        out_shape=[jax.ShapeDtypeStruct((bsz, NSA_KV_HEADS, rows, HEAD_DIM), jnp.float32),
                   jax.ShapeDtypeStruct((bsz, NSA_KV_HEADS, n_q, N_SELB), jnp.float32)],
        compiler_params=pltpu.CompilerParams(dimension_semantics=("arbitrary",)),
        name="nsa_dec_select",
    )(qr, kct, vc, band)


def _dec_attend_body(pt_ref, *refs, qpos0, n_q, past):
    page_refs = refs[:PAGE_GROUP]
    (qs_ref, qw_ref, knew_ref, vnew_ref, wbuf_ref, wnew_ref, oc_ref, g_ref,
     o_ref, m_ref, l_ref, acc_ref) = refs[PAGE_GROUP:]
    f32, bf16 = jnp.float32, jnp.bfloat16
    j = pl.program_id(1)
    n_rows = qs_ref.shape[1]

    @pl.when(j == 0)
    def _():
        m_ref[...] = jnp.full(m_ref.shape, MASKED, f32)
        l_ref[...] = jnp.zeros(l_ref.shape, f32)
        acc_ref[...] = jnp.zeros(acc_ref.shape, f32)

    def online(s, weigh):
        m_old = m_ref[...]
        m_new = jnp.maximum(m_old, jnp.max(s, axis=1, keepdims=True))
        alpha = jnp.exp(m_old - m_new)
        p = jnp.exp(s - m_new)
        l_ref[...] = alpha * l_ref[...] + jnp.sum(p, axis=1, keepdims=True)
        acc_ref[...] = alpha * acc_ref[...] + weigh(p.astype(bf16))
        m_ref[...] = m_new

    qs = qs_ref[0]
    pages = [pr[0] for pr in page_refs]
    keys_t = jnp.concatenate([p[:KV_W] for p in pages], axis=1).astype(bf16)
    vals_t = jnp.concatenate([p[KV_W:] for p in pages], axis=1).astype(bf16)
    blk_id = j * (DEC_KEYS // SEL_BLK) + lax.shift_right_logical(
        lax.broadcasted_iota(jnp.int32, (N_SELB, DEC_KEYS), 1), int(math.log2(SEL_BLK)))
    onehot_t = jnp.where(lax.broadcasted_iota(jnp.int32, (N_SELB, DEC_KEYS), 0) == blk_id, 1.0, 0.0).astype(bf16)
    online(_dot(qs, jnp.concatenate([keys_t, onehot_t], axis=0)), lambda p: _dot_nt(p, vals_t))

    @pl.when(j == pl.num_programs(1) - 1)
    def _():
        row_q = qpos0 + (lax.broadcasted_iota(jnp.int32, (n_rows, 1), 0) % n_q)
        qh = qw_ref[0]
        new_pos = past + lax.broadcasted_iota(jnp.int32, (n_rows, NEW_PAD), 1)
        new_ok = (new_pos <= row_q) & (new_pos < past + n_q)
        s_new = jnp.where(new_ok, _dot_nt(qh, knew_ref[0]), MASKED)
        online(s_new, lambda p: _dot(p, vnew_ref[0]))
        o_s = acc_ref[...] / l_ref[...]
        wbuf_t = wbuf_ref[0]
        wnew = wnew_ref[0]
        n_buf = wbuf_t.shape[1]
        s_b = _dot(qh, wbuf_t[:KV_W].astype(bf16))
        pos_b = (past - n_buf) + lax.broadcasted_iota(jnp.int32, (n_rows, n_buf), 1)
        s_b = jnp.where((pos_b > row_q - WINDOW) & (pos_b >= 0), s_b, MASKED)
        s_n = jnp.where(new_ok, _dot_nt(qh, wnew[:, :KV_W].astype(bf16)), MASKED)
        m_w = jnp.maximum(jnp.max(s_b, axis=1, keepdims=True), jnp.max(s_n, axis=1, keepdims=True))
        p_b, p_n = jnp.exp(s_b - m_w), jnp.exp(s_n - m_w)
        l_w = jnp.sum(p_b, axis=1, keepdims=True) + jnp.sum(p_n, axis=1, keepdims=True)
        o_w = (_dot_nt(p_b.astype(bf16), wbuf_t[KV_W:].astype(bf16))
               + _dot(p_n.astype(bf16), wnew[:, KV_W:].astype(bf16))) / l_w
        half = n_rows // NSA_KV_HEADS
        own = lambda a: jnp.concatenate([a[h * half:(h + 1) * half, h * HEAD_DIM:(h + 1) * HEAD_DIM]
                                         for h in range(NSA_KV_HEADS)], axis=0)
        g = g_ref[0]
        o_ref[0] = g[:, 0:1] * oc_ref[0] + g[:, 1:2] * own(o_s) + g[:, 2:3] * own(o_w)


def _dec_attend(cache, page_table, qs, qw, knew, vnew, wbuf, wnew, o_c, gates, n_q, qpos0):
    bsz, n_pages = page_table.shape
    n_rows = qs.shape[1]
    per_b = lambda *tail: pl.BlockSpec((1,) + tail, lambda b, j, pt: (b, 0, 0))
    grid_spec = pltpu.PrefetchScalarGridSpec(
        num_scalar_prefetch=1, grid=(bsz, n_pages // PAGE_GROUP),
        in_specs=_page_specs(n_pages, 1) + [
            per_b(n_rows, KV_W + N_SELB), per_b(n_rows, KV_W), per_b(NEW_PAD, KV_W), per_b(NEW_PAD, KV_W),
            per_b(2 * KV_W, wbuf.shape[2]), per_b(NEW_PAD, 2 * KV_W), per_b(n_rows, HEAD_DIM), per_b(n_rows, 3)],
        out_specs=per_b(n_rows, HEAD_DIM),
        scratch_shapes=[pltpu.VMEM((n_rows, 1), jnp.float32), pltpu.VMEM((n_rows, 1), jnp.float32),
                        pltpu.VMEM((n_rows, KV_W), jnp.float32)])
    return pl.pallas_call(
        functools.partial(_dec_attend_body, qpos0=qpos0, n_q=n_q, past=n_pages * PAGE_SIZE),
        grid_spec=grid_spec,
        out_shape=jax.ShapeDtypeStruct((bsz, n_rows, HEAD_DIM), jnp.float32),
        compiler_params=pltpu.CompilerParams(dimension_semantics=("arbitrary", "arbitrary")),
        name="nsa_dec_attend",
    )(page_table.reshape(-1), *([cache] * PAGE_GROUP), qs, qw, knew, vnew, wbuf, wnew, o_c, gates)


def _pool_matrices(w_cmp_pool, rows=Q_BLK):
    subs = rows // CMP_STRIDE
    sub = np.arange(rows) // CMP_STRIDE == np.arange(subs)[:, None]
    w_rep = jnp.tile(w_cmp_pool.reshape(2, 2, CMP_STRIDE), (1, 1, subs))
    return jnp.where(sub[None, None], w_rep[:, :, None, :], 0.0).reshape(4, subs, rows).astype(jnp.bfloat16)


def _nsa_decode(q_raw, q_rot, gates, rows_full, rows_win, cache, page_table, win_buf, w_cmp_pool, past):
    bsz, n_q = q_raw.shape[:2]
    bf16 = jnp.bfloat16
    n_blk = past // SEL_BLK
    assert past % DEC_KEYS == 0 and n_blk <= N_SELB and n_q <= NEW_PAD
    scale = HEAD_DIM ** -0.5
    cache2 = cache.transpose(0, 2, 3, 4, 1).reshape(cache.shape[0], 4 * KV_W, PAGE_SIZE)
    pooled_t = _dec_pool(cache2, page_table, _pool_matrices(w_cmp_pool, POOL_ROWS).transpose(0, 2, 1))
    pooled_t = pooled_t.reshape(bsz, 4, NSA_KV_HEADS, HEAD_DIM, -1)
    last = ((0, 0), (0, 0), (0, 0), (0, 1))
    kct = jnp.pad(pooled_t[:, 0, ..., :-1] + pooled_t[:, 1, ..., 1:], last)
    vc_p = jnp.pad(pooled_t[:, 2, ..., :-1] + pooled_t[:, 3, ..., 1:], last).transpose(0, 1, 3, 2)
    rows_of = lambda a: a.transpose(0, 2, 3, 1, 4).reshape(bsz, NSA_KV_HEADS, NSA_GROUP * n_q, a.shape[-1])
    qr = rows_of((q_raw * scale).astype(bf16))
    n_pick = min(SEL_TOPN, n_blk + 1) - 1
    o_c, selb = _dec_select(qr, kct.astype(bf16), vc_p.astype(bf16), n_q, past, n_pick, n_blk)
    qo = rows_of((q_rot * scale).astype(bf16))
    zero = jnp.zeros_like(qo[:, 0])
    qw = jnp.concatenate([jnp.concatenate([qo[:, 0], zero], -1), jnp.concatenate([zero, qo[:, 1]], -1)], axis=1)
    bias = jnp.tile(selb, (1, 1, NSA_GROUP, 1)).reshape(bsz, -1, N_SELB).astype(bf16)
    qs = jnp.concatenate([qw, bias], axis=-1)
    pad_new = lambda a: jnp.pad(a.reshape(bsz, n_q, -1), ((0, 0), (0, NEW_PAD - n_q), (0, 0)))
    knew = pad_new(rows_full[:, :, 2]).astype(bf16)
    vnew = pad_new(rows_full[:, :, 3]).astype(bf16)
    wnew = pad_new(rows_win)
    wbuf = win_buf.transpose(0, 2, 3, 4, 1).reshape(bsz, 2 * KV_W, win_buf.shape[1])
    gt = rows_of(gates).reshape(bsz, -1, 3)
    o = _dec_attend(cache2, page_table, qs, qw, knew, vnew, wbuf, wnew,
                    o_c.reshape(bsz, -1, HEAD_DIM), gt, n_q, past)
    o = o.reshape(bsz, NSA_KV_HEADS, NSA_GROUP, n_q, HEAD_DIM).transpose(0, 3, 1, 2, 4)
    return o.reshape(bsz, n_q, NSA_HEADS * HEAD_DIM)


def _ab_mixer(x, pos, w_in, w_gla_gate, b_gla_gate, gla_norm_g, w_cmp_pool, w_out,
              gla_state, nsa_cache, page_table, win_buf):
    bsz, t_, _ = x.shape
    h_in = _mm(x.reshape(bsz * t_, -1), w_in[:, IN_AB_PERM], keep_pad=True).reshape(bsz, t_, -1)
    o_a, s_a = _gla(h_in, w_gla_gate, b_gla_gate, gla_norm_g, gla_state)
    kv_w = NSA_KV_HEADS * HEAD_DIM
    if nsa_cache is None:
        rows2, win2, kk, vvt, qr, qo, gt, pooled = _nsa_prep(h_in, pos, w_cmp_pool)
        pooled = pooled.reshape(bsz, t_ // CMP_STRIDE, 4, NSA_KV_HEADS, HEAD_DIM)
        kc = pooled[:, :-1, 0] + pooled[:, 1:, 1]
        vc = pooled[:, :-1, 2] + pooled[:, 1:, 3]
        kc_p = jnp.pad(kc, ((0, 0), (0, 1), (0, 0), (0, 0))).transpose(0, 2, 1, 3).astype(jnp.bfloat16)
        vct = jnp.pad(vc, ((0, 0), (0, 1), (0, 0), (0, 0))).transpose(0, 2, 3, 1).astype(jnp.bfloat16)
        o_b = _nsa_prompt(qr, qo, gt, kc_p, vct, kk, vvt)
        rows_full = rows2.reshape(bsz, t_, 4, NSA_KV_HEADS, HEAD_DIM)
        new_win = win2[:, -min(WINDOW, t_):].reshape(bsz, -1, 2, NSA_KV_HEADS, HEAD_DIM)
    else:
        nq = h_in[..., COL_NQ:COL_NKV]
        nkv = h_in[..., COL_NKV:COL_TAIL]
        ngate = h_in[..., COL_TAIL + TAIL_GATE:COL_TAIL + TAIL_GATE + NSA_SIZES[2]]
        q_raw = nq.reshape(bsz, t_, NSA_KV_HEADS, NSA_GROUP, HEAD_DIM)
        q_rot = _partial_rope(q_raw, pos)
        kv = nkv.reshape(bsz, t_, 6, NSA_KV_HEADS, HEAD_DIM)
        k_sel = _partial_rope(kv[:, :, 2], pos)
        k_win = _partial_rope(kv[:, :, 4], pos)
        rows_full = jnp.stack([kv[:, :, 0], kv[:, :, 1], k_sel, kv[:, :, 3]], axis=2)
        rows_win = jnp.stack([k_win, kv[:, :, 5]], axis=2)
        gates = jax.nn.sigmoid(ngate).reshape(bsz, t_, NSA_KV_HEADS, NSA_GROUP, 3)
        past_len = page_table.shape[1] * PAGE_SIZE
        o_b = _nsa_decode(q_raw, q_rot, gates, rows_full, rows_win, nsa_cache, page_table, win_buf,
                          w_cmp_pool, past_len)
        w_buf = win_buf.shape[1]
        kw = jnp.concatenate([win_buf, rows_win], axis=1)
        new_win = kw[:, -w_buf:]
    y = _mm_pair(o_a.reshape(bsz * t_, -1), o_b.reshape(bsz * t_, -1), w_out).reshape(bsz, t_, -1)
    return y, s_a, rows_full, new_win


CONV_HALO = 32
CONV_LEAD = CONV_HALO - (CONV_W - 1)


def _conv_body(x_ref, buf0_ref, w1_ref, b1_ref, wdw_ref, bdw_ref, g_ref, b_ref, w2_ref, b2_ref,
               o_ref, tail_ref, ext_ref, z_ref, *, t_last):
    bf16 = jnp.bfloat16
    tt = x_ref.shape[1]
    i = pl.program_id(1)

    @pl.when(i == 0)
    def _():
        ext_ref[0:CONV_HALO, :] = buf0_ref[0]
        ext_ref[CONV_HALO + tt:CONV_HALO + tt + SUBLANES, :] = jnp.zeros((SUBLANES, D_CONV), jnp.float32)

    h = _dot(x_ref[0].astype(bf16), w1_ref[...]) + b1_ref[...]
    ext_ref[CONV_HALO:CONV_HALO + tt, :] = h[:, :D_CONV] * jax.nn.sigmoid(h[:, D_CONV:])
    c = jnp.zeros((tt, D_CONV), jnp.float32) + bdw_ref[...]
    for r in range(SUBLANES):
        z = None
        for a in range(CONV_HALO // SUBLANES + 1):
            k = SUBLANES * a + r - CONV_LEAD
            if 0 <= k < CONV_W:
                term = ext_ref[SUBLANES * a:SUBLANES * a + tt + SUBLANES, :] * wdw_ref[k:k + 1, :]
                z = term if z is None else z + term
        if r == 0:
            c = c + z[:tt]
        else:
            z_ref[...] = z
            c = c + z_ref[pl.ds(r, tt), :]
    c = _ln_rows(c, g_ref[...], b_ref[...])
    c = c * jax.nn.sigmoid(c)
    o_ref[0] = _dot(c.astype(bf16), w2_ref[...]) + b2_ref[...]
    tail_ref[0] = ext_ref[t_last:t_last + CONV_HALO, :]
    ext_ref[0:CONV_HALO, :] = ext_ref[tt:tt + CONV_HALO, :]


def _conv_module(x, conv_buf, w_pw1, b_pw1, w_dw, b_dw, ln_g, ln_b, w_pw2, b_pw2):
    bsz, t_, d = x.shape
    bf16 = jnp.bfloat16
    tp = -(-t_ // 8) * 8
    tt = min(tp, 256)
    n_t = tp // tt
    if tp != t_:
        x = jnp.pad(x, ((0, 0), (0, tp - t_), (0, 0)))
    if conv_buf is None:
        buf0 = jnp.zeros((bsz, CONV_HALO, D_CONV), jnp.float32)
    else:
        buf0 = jnp.pad(conv_buf, ((0, 0), (CONV_LEAD, 0), (0, 0)))
    fixed = lambda shape: pl.BlockSpec(shape, lambda b, i: (0,) * len(shape))
    per_b = pl.BlockSpec((1, CONV_HALO, D_CONV), lambda b, i: (b, 0, 0))
    out, tail = pl.pallas_call(
        functools.partial(_conv_body, t_last=t_ - (n_t - 1) * tt),
        grid=(bsz, n_t),
        in_specs=[pl.BlockSpec((1, tt, d), lambda b, i: (b, i, 0)), per_b,
                  fixed((d, 2 * D_CONV)), fixed((1, 2 * D_CONV)), fixed((CONV_HALO, D_CONV)), fixed((1, D_CONV)),
                  fixed((1, D_CONV)), fixed((1, D_CONV)), fixed((D_CONV, d)), fixed((1, d))],
        out_specs=[pl.BlockSpec((1, tt, d), lambda b, i: (b, i, 0)), per_b],
        out_shape=[jax.ShapeDtypeStruct((bsz, tp, d), jnp.float32),
                   jax.ShapeDtypeStruct((bsz, CONV_HALO, D_CONV), jnp.float32)],
        scratch_shapes=[pltpu.VMEM((CONV_HALO + tt + SUBLANES, D_CONV), jnp.float32),
                        pltpu.VMEM((tt + SUBLANES, D_CONV), jnp.float32)],
        compiler_params=pltpu.CompilerParams(dimension_semantics=("arbitrary", "arbitrary"),
                                             vmem_limit_bytes=VMEM_LIMIT),
        name="conv_module",
    )(x, buf0, w_pw1.astype(bf16), b_pw1.reshape(1, -1), jnp.pad(w_dw, ((0, CONV_HALO - CONV_W), (0, 0))),
      b_dw.reshape(1, -1), ln_g.reshape(1, -1), ln_b.reshape(1, -1), w_pw2.astype(bf16), b_pw2.reshape(1, -1))
    return out[:, :t_], tail[:, CONV_LEAD:]


PACK_W = 256
SC_WINDOW = 128
SC_TILES = 32


def _pack_rows(y):
    out = []
    for h in range(2):
        lo = lax.bitcast_convert_type(y[:, 2 * h * PACK_W:(2 * h + 1) * PACK_W].astype(jnp.bfloat16)
                                      .astype(jnp.float32), jnp.uint32)
        hi = lax.bitcast_convert_type(y[:, (2 * h + 1) * PACK_W:(2 * h + 2) * PACK_W].astype(jnp.bfloat16)
                                      .astype(jnp.float32), jnp.uint32)
        out.append(lax.bitcast_convert_type((lo >> 16) | hi, jnp.int32))
    return out


def _unpack_words(w):
    u = lax.bitcast_convert_type(w, jnp.uint32)
    lo = lax.bitcast_convert_type(u << 16, jnp.float32)
    hi = lax.bitcast_convert_type(u & jnp.uint32(0xFFFF0000), jnp.float32)
    return lo, hi


def _gather_rows(src, idx):
    n = idx.shape[0]
    if n % (SC_WINDOW * SC_TILES) != 0:
        return jnp.take(src, idx, axis=0)
    mesh = plsc.VectorSubcoreMesh(core_axis_name="core", subcore_axis_name="subcore")

    @pl.kernel(out_type=jax.ShapeDtypeStruct((n, src.shape[1]), src.dtype), mesh=mesh)
    def gather_kernel(src_hbm, idx_hbm, out_hbm):
        def step(idx_vmem, out_vmem):
            pltpu.sync_copy(src_hbm.at[idx_vmem.at[0]], out_vmem)

        pltpu.emit_pipeline(
            step, grid=(n // SC_WINDOW,),
            in_specs=[pl.BlockSpec((1, SC_WINDOW), index_map=lambda i: (0, i))],
            out_specs=[pl.BlockSpec((SC_WINDOW, src.shape[1]), index_map=lambda i: (i, 0))],
            core_axis_name=("core", "subcore"),
            dimension_semantics=(pltpu.PARALLEL,),
        )(idx_hbm, out_hbm)

    return gather_kernel(src, idx.reshape(1, n))


def _scatter_rows(src, idx, n_out):
    n = idx.shape[0]
    m = src.shape[0] // 2
    reps = n // (2 * m)
    if n % (SC_WINDOW * SC_TILES) != 0 or m % SC_WINDOW != 0:
        rows = jnp.arange(n, dtype=jnp.int32)
        src_row = (rows // (reps * m)) * m + rows % m
        return jnp.zeros((n_out, src.shape[1]), src.dtype).at[idx].set(jnp.take(src, src_row, axis=0))
    tiles = m // SC_WINDOW
    mesh = plsc.VectorSubcoreMesh(core_axis_name="core", subcore_axis_name="subcore")

    @pl.kernel(out_type=jax.ShapeDtypeStruct((n_out, src.shape[1]), src.dtype), mesh=mesh, scratch_types=[])
    def scatter_kernel(src_hbm, idx_hbm, out_hbm):
        def step(src_vmem, idx_vmem):
            pltpu.sync_copy(src_vmem, out_hbm.at[idx_vmem.at[0]])

        pltpu.emit_pipeline(
            step, grid=(n // SC_WINDOW,),
            in_specs=[pl.BlockSpec((SC_WINDOW, src.shape[1]),
                                   index_map=lambda i: ((i // (reps * tiles)) * tiles + i % tiles, 0)),
                      pl.BlockSpec((1, SC_WINDOW), index_map=lambda i: (0, i))],
            out_specs=[],
            core_axis_name=("core", "subcore"),
            dimension_semantics=(pltpu.PARALLEL,),
        )(src_hbm, idx_hbm)

    return scatter_kernel(src, idx.reshape(1, n))


PER_GROUP = N_EXPERTS // N_GROUPS
PICKED = -3e38


def _ln_rows(v, g, b):
    mu = jnp.mean(v, axis=-1, keepdims=True)
    c = v - mu
    var = jnp.mean(c * c, axis=-1, keepdims=True)
    return c * lax.rsqrt(var + LN_EPS) * g + b


def _first_max(v, ids, axes, sentinel):
    best = v
    for a in axes:
        best = jnp.max(best, axis=a, keepdims=True)
    first = jnp.where(v == best, ids, sentinel)
    for a in axes:
        first = jnp.min(first, axis=a, keepdims=True)
    return best, first


def _sum_axes(v, axes):
    for a in axes:
        v = jnp.sum(v, axis=a, keepdims=True)
    return v


def _moe_pre_body(x_ref, mix_ref, g_ref, b_ref, wr_ref, br_ref, wgu_ref, wdn_ref,
                  x1_ref, xp_ref, sh_ref, eidx_ref, gate_ref, rank_ref, cnt_ref, run_ref):
    f32, bf16 = jnp.float32, jnp.bfloat16
    tm = x_ref.shape[0]

    @pl.when(pl.program_id(0) == 0)
    def _():
        run_ref[...] = jnp.zeros(run_ref.shape, f32)

    x1 = _ln_rows(ALPHA * x_ref[...] + mix_ref[...], g_ref[...], b_ref[...])
    x1_ref[...] = x1
    x1b = x1.astype(bf16)
    xp_ref[0], xp_ref[1] = _pack_rows(x1)

    h = _dot(x1b, wgu_ref[...])
    d_sh = h.shape[1] // 2
    act = (jax.nn.silu(h[:, :d_sh]) * h[:, d_sh:]).astype(bf16)
    sh_ref[...] = _dot(act, wdn_ref[...])

    s = jax.nn.sigmoid(_dot_nt(wr_ref[...], x1b)).reshape(N_GROUPS, PER_GROUP, tm)
    sb = s + br_ref[...].reshape(N_GROUPS, PER_GROUP, 1)
    shape3 = (N_GROUPS, PER_GROUP, tm)
    pid = lax.broadcasted_iota(jnp.int32, shape3, 1)
    gid = lax.broadcasted_iota(jnp.int32, (N_GROUPS, 1, tm), 0)
    eid = lax.broadcasted_iota(jnp.int32, shape3, 0) * PER_GROUP + pid
    top1, i1 = _first_max(sb, pid, (1,), PER_GROUP)
    top2 = jnp.max(jnp.where(pid == i1, PICKED, sb), axis=1, keepdims=True)
    gscore = top1 + top2
    gsel = jnp.zeros((N_GROUPS, 1, tm), f32)
    for _ in range(TOPK_GROUPS):
        _, first = _first_max(gscore, gid, (0,), N_GROUPS)
        hit = gid == first
        gsel = jnp.where(hit, 1.0, gsel)
        gscore = jnp.where(hit, PICKED, gscore)
    cand = jnp.where(gsel > 0.0, sb, -1e30)
    firsts, gates = [], []
    picked = jnp.zeros(shape3, f32)
    for _ in range(TOP_K):
        _, first = _first_max(cand, eid, (0, 1), N_EXPERTS)
        hit = eid == first
        firsts.append(first)
        gates.append(_sum_axes(jnp.where(hit, s, 0.0), (0, 1)))
        picked = jnp.where(hit, 1.0, picked)
        cand = jnp.where(hit, PICKED, cand)
    gsum = gates[0]
    for gk in gates[1:]:
        gsum = gsum + gk
    earlier = (lax.broadcasted_iota(jnp.int32, (tm, tm), 0) < lax.broadcasted_iota(jnp.int32, (tm, tm), 1))
    picked2 = picked.reshape(N_EXPERTS, tm)
    rank = run_ref[...] + _dot(picked2.astype(bf16), jnp.where(earlier, 1.0, 0.0).astype(bf16))
    run_new = run_ref[...] + jnp.sum(picked2, axis=1, keepdims=True)
    run_ref[...] = run_new
    cnt_ref[...] = jnp.broadcast_to(run_new, cnt_ref.shape)
    rank3 = rank.reshape(shape3)
    for k in range(TOP_K):
        hit = eid == firsts[k]
        eidx_ref[k:k + 1, :] = firsts[k].reshape(1, tm)
        gate_ref[k:k + 1, :] = (gates[k] / gsum * ROUTE_SCALE).reshape(1, tm)
        rank_ref[k:k + 1, :] = _sum_axes(jnp.where(hit, rank3, 0.0), (0, 1)).reshape(1, tm).astype(jnp.int32)


def _moe_pre(x, mix, g, b, w_router, b_router, w_sh_gu, w_sh_down):
    m, d = x.shape
    bf16 = jnp.bfloat16
    tm = min(m, 512)
    row = lambda i: (i, 0)
    col = lambda i: (0, i)
    fixed = lambda i: (0, 0)
    d_sh2 = w_sh_gu.shape[1]
    return pl.pallas_call(
        _moe_pre_body,
        grid=(m // tm,),
        in_specs=[pl.BlockSpec((tm, d), row), pl.BlockSpec((tm, d), row),
                  pl.BlockSpec((1, d), fixed), pl.BlockSpec((1, d), fixed),
                  pl.BlockSpec((N_EXPERTS, d), fixed), pl.BlockSpec((N_EXPERTS, 1), fixed),
                  pl.BlockSpec((d, d_sh2), fixed), pl.BlockSpec((d_sh2 // 2, d), fixed)],
        out_specs=[pl.BlockSpec((tm, d), row), pl.BlockSpec((2, tm, PACK_W), lambda i: (0, i, 0)),
                   pl.BlockSpec((tm, d), row),
                   pl.BlockSpec((TOP_K, tm), col), pl.BlockSpec((TOP_K, tm), col), pl.BlockSpec((TOP_K, tm), col),
                   pl.BlockSpec((N_EXPERTS, LANE), fixed)],
        out_shape=[jax.ShapeDtypeStruct((m, d), jnp.float32), jax.ShapeDtypeStruct((2, m, PACK_W), jnp.int32),
                   jax.ShapeDtypeStruct((m, d), jnp.float32),
                   jax.ShapeDtypeStruct((TOP_K, m), jnp.int32), jax.ShapeDtypeStruct((TOP_K, m), jnp.float32),
                   jax.ShapeDtypeStruct((TOP_K, m), jnp.int32),
                   jax.ShapeDtypeStruct((N_EXPERTS, LANE), jnp.float32)],
        scratch_shapes=[pltpu.VMEM((N_EXPERTS, 1), jnp.float32)],
        compiler_params=pltpu.CompilerParams(dimension_semantics=("arbitrary",),
                                             vmem_limit_bytes=VMEM_LIMIT),
        name="moe_pre",
    )(x, mix, g.reshape(1, d), b.reshape(1, d), w_router.T.astype(bf16), b_router.reshape(N_EXPERTS, 1),
      w_sh_gu.astype(bf16), w_sh_down.astype(bf16))


def _moe_expert_body(exp_ref, first_ref, rows_ref, xs_ref, wgu_ref, wdn_ref, y_ref, wgu_bf, wdn_bf):
    i = pl.program_id(0)
    bf16 = jnp.bfloat16

    @pl.when(first_ref[i] == 1)
    def _():
        wgu_bf[...] = wgu_ref[0, 0].astype(bf16)
        wdn_bf[...] = wdn_ref[0, 0].astype(bf16)

    @pl.when(rows_ref[i] > 0)
    def _():
        live = lax.broadcasted_iota(jnp.int32, (xs_ref.shape[1], 1), 0) < rows_ref[i]
        h = None
        for hw in range(2):
            for q, xq in enumerate(_unpack_words(xs_ref[hw])):
                r0 = (2 * hw + q) * PACK_W
                part = _dot(jnp.where(live, xq, 0.0).astype(bf16), wgu_bf[r0:r0 + PACK_W, :])
                h = part if h is None else h + part
        d_e = h.shape[1] // 2
        act = (jax.nn.silu(h[:, :d_e]) * h[:, d_e:]).astype(bf16)
        y_ref[0], y_ref[1] = _pack_rows(_dot(act, wdn_bf[...]))

    @pl.when(rows_ref[i] == 0)
    def _():
        y_ref[...] = jnp.zeros(y_ref.shape, y_ref.dtype)


def _moe_experts(xs, blk_exp, blk_first, blk_rows, w_exp_gu, w_exp_down, layer, bm):
    n_slots = xs.shape[1]
    d = w_exp_gu.shape[2]
    n_blk = n_slots // bm
    d_e2 = w_exp_gu.shape[3]
    words = lambda i, e, f, a: (0, i, 0)
    grid_spec = pltpu.PrefetchScalarGridSpec(
        num_scalar_prefetch=3,
        grid=(n_blk,),
        in_specs=[pl.BlockSpec((2, bm, PACK_W), words),
                  pl.BlockSpec((1, 1, d, d_e2), lambda i, e, f, a: (layer, e[i], 0, 0)),
                  pl.BlockSpec((1, 1, d_e2 // 2, d), lambda i, e, f, a: (layer, e[i], 0, 0))],
        out_specs=pl.BlockSpec((2, bm, PACK_W), words),
        scratch_shapes=[pltpu.VMEM((d, d_e2), jnp.bfloat16), pltpu.VMEM((d_e2 // 2, d), jnp.bfloat16)])
    return pl.pallas_call(
        _moe_expert_body,
        grid_spec=grid_spec,
        out_shape=jax.ShapeDtypeStruct((2, n_slots, PACK_W), jnp.int32),
        compiler_params=pltpu.CompilerParams(dimension_semantics=("arbitrary",),
                                             vmem_limit_bytes=VMEM_LIMIT),
        name="moe_experts",
    )(blk_exp, blk_first, blk_rows, xs, w_exp_gu, w_exp_down)


def _combine_ln_body(x_ref, yg_ref, gt_ref, sh_ref, g_ref, b_ref, o_ref):
    gt = gt_ref[...]
    parts = []
    for hw in range(2):
        lo_acc = hi_acc = None
        for k in range(TOP_K):
            lo, hi = _unpack_words(yg_ref[hw, k])
            gk = gt[:, k:k + 1]
            lo_acc = lo * gk if lo_acc is None else lo_acc + lo * gk
            hi_acc = hi * gk if hi_acc is None else hi_acc + hi * gk
        parts += [lo_acc, hi_acc]
    routed = jnp.concatenate(parts, axis=1)
    o_ref[...] = _ln_rows(ALPHA * x_ref[...] + (routed + sh_ref[...]), g_ref[...], b_ref[...])


def _combine_ln(x, yg, gate_t, shared, g, b):
    m, d = x.shape
    tm = min(m, 256)
    row = lambda i: (i, 0)
    fixed = lambda i: (0, 0)
    return pl.pallas_call(
        _combine_ln_body,
        grid=(m // tm,),
        in_specs=[pl.BlockSpec((tm, d), row), pl.BlockSpec((2, TOP_K, tm, PACK_W), lambda i: (0, 0, i, 0)),
                  pl.BlockSpec((tm, TOP_K), row), pl.BlockSpec((tm, d), row),
                  pl.BlockSpec((1, d), fixed), pl.BlockSpec((1, d), fixed)],
        out_specs=pl.BlockSpec((tm, d), row),
        out_shape=jax.ShapeDtypeStruct((m, d), jnp.float32),
        compiler_params=pltpu.CompilerParams(dimension_semantics=("arbitrary",)),
        name="combine_ln",
    )(x, yg, gate_t, shared, g.reshape(1, d), b.reshape(1, d))


def _moe_layer(streams, ln1_g, ln1_b, ln2_g, ln2_b, w_router, b_router, w_exp_gu, w_exp_down, layer,
               w_sh_gu, w_sh_down):
    pre = [_moe_pre(x, mix, ln1_g, ln1_b, w_router, b_router, w_sh_gu, w_sh_down) for x, mix in streams]
    m_all = sum(x.shape[0] for x, _ in streams)
    bm = 512 if m_all * TOP_K >= 512 * N_EXPERTS else MOE_BLK
    n_blk = (m_all * TOP_K) // bm + N_EXPERTS
    n_slots = n_blk * bm
    counts_of = [p[6][:, 0].astype(jnp.int32) for p in pre]
    counts = sum(counts_of)
    padded = (counts + bm - 1) // bm * bm
    pad_end = jnp.cumsum(padded)
    pad_start = pad_end - padded
    dests, before = [], jnp.zeros_like(counts)
    for p, cnt in zip(pre, counts_of):
        eidx, rank8 = p[3], p[5]
        start_of = jnp.sum(jnp.where(eidx[:, :, None] == jnp.arange(N_EXPERTS), pad_start + before, 0), axis=-1)
        dest = (start_of + rank8).reshape(-1)
        dests.append(jnp.concatenate([dest, dest + n_slots]))
        before = before + cnt
    blk_start = jnp.arange(n_blk, dtype=jnp.int32) * bm
    blk_exp = jnp.minimum(jnp.sum(pad_end[None, :] <= blk_start[:, None], axis=1), N_EXPERTS - 1).astype(jnp.int32)
    blk_rows = jnp.clip(counts[blk_exp] - (blk_start - pad_start[blk_exp]), 0, bm).astype(jnp.int32)
    blk_first = jnp.concatenate([jnp.ones((1,), jnp.int32), (blk_exp[1:] != blk_exp[:-1]).astype(jnp.int32)])
    words = [p[1].reshape(-1, PACK_W) for p in pre]
    xs = _scatter_rows(words[0], dests[0], 2 * n_slots)
    for w_i, d_i in zip(words[1:], dests[1:]):
        m_i = w_i.shape[0] // 2
        pair = jnp.arange(d_i.shape[0], dtype=jnp.int32)
        xs = xs.at[d_i].set(jnp.take(w_i, (pair // (TOP_K * m_i)) * m_i + pair % m_i, axis=0))
    y = _moe_experts(xs.reshape(2, n_slots, PACK_W), blk_exp, blk_first, blk_rows, w_exp_gu, w_exp_down, layer, bm)
    y = y.reshape(2 * n_slots, PACK_W)
    outs = []
    for p, d_i, (x, _) in zip(pre, dests, streams):
        yg = _gather_rows(y, d_i).reshape(2, TOP_K, x.shape[0], PACK_W)
        outs.append(_combine_ln(p[0], yg, p[4].T, p[2], ln2_g, ln2_b))
    return outs


def _trunks(x_p, x_s, pos_p, pos_s, gla_state, nsa_cache, page_table, win_buf, conv_buf,
            w_in_ab, w_gla_gate, b_gla_gate, gla_norm_g, w_cmp_pool, w_out_ab,
            w_pw1, b_pw1, w_dw, b_dw, conv_ln_g, conv_ln_b, w_pw2, b_pw2,
            ln_g, ln_b, w_router, b_router, w_exp_gu, w_exp_down, w_sh_gu, w_sh_down):
    xs = [x_p, x_s]
    states = [(None, None, None, None), (gla_state, nsa_cache, win_buf, conv_buf)]
    poss = [pos_p, pos_s]
    new = [dict(gla=[], rows=[], win=[], conv=[]) for _ in xs]
    for layer in range(DEPTH):
        i = layer // 2
        mixes = []
        for x, pos, (g_st, cache, wbuf, cbuf), out in zip(xs, poss, states, new):
            if layer % 2 == 0:
                mix, s_a, rows, win = _ab_mixer(
                    x, pos, w_in_ab[i], w_gla_gate[i], b_gla_gate[i], gla_norm_g[i], w_cmp_pool[i], w_out_ab[i],
                    None if g_st is None else g_st[i], None if cache is None else cache[i], page_table,
                    None if wbuf is None else wbuf[i])
                out["gla"].append(s_a)
                out["rows"].append(rows)
                out["win"].append(win)
            else:
                mix, cb = _conv_module(x, None if cbuf is None else cbuf[i], w_pw1[i], b_pw1[i],
                                       w_dw[i], b_dw[i], conv_ln_g[i], conv_ln_b[i], w_pw2[i], b_pw2[i])
                out["conv"].append(cb)
            mixes.append(mix)
        d = xs[0].shape[-1]
        ys = _moe_layer([(x.reshape(-1, d), mix.reshape(-1, d)) for x, mix in zip(xs, mixes)],
                        ln_g[layer, 0], ln_b[layer, 0], ln_g[layer, 1], ln_b[layer, 1],
                        w_router[layer], b_router[layer], w_exp_gu, w_exp_down, layer,
                        w_sh_gu[layer], w_sh_down[layer])
        xs = [y.reshape(x.shape) for y, x in zip(ys, xs)]
    return [(x, jnp.stack(o["gla"]), jnp.stack(o["rows"]), jnp.stack(o["win"]), jnp.stack(o["conv"]))
            for x, o in zip(xs, new)]


def kernel(x_prompt, x_sample, state_gla, cache_nsa_kv, state_nsa_win, state_conv, page_table,
           w_in_ab, w_gla_gate, b_gla_gate, gla_norm_g, w_cmp_pool, w_out_ab,
           w_pw1, b_pw1, w_dw, b_dw, conv_ln_g, conv_ln_b, w_pw2, b_pw2,
           ln_g, ln_b, w_router, b_router, w_exp_gu, w_exp_down, w_sh_gu, w_sh_down):
    weights = (w_in_ab, w_gla_gate, b_gla_gate, gla_norm_g, w_cmp_pool, w_out_ab,
               w_pw1, b_pw1, w_dw, b_dw, conv_ln_g, conv_ln_b, w_pw2, b_pw2,
               ln_g, ln_b, w_router, b_router, w_exp_gu, w_exp_down, w_sh_gu, w_sh_down)
    past_len = page_table.shape[1] * PAGE_SIZE
    pos_p = jnp.arange(x_prompt.shape[1])
    pos_s = past_len + jnp.arange(x_sample.shape[1])
    (y_prompt, gla_p, rows_p, win_p, conv_p), (y_sample, gla_s, rows_s, win_s, conv_s) = _trunks(
        x_prompt, x_sample, pos_p, pos_s, state_gla, cache_nsa_kv, page_table, state_nsa_win, state_conv, *weights)
    return (y_prompt, y_sample, gla_p, gla_s, rows_p, rows_s, win_p, win_s, conv_p, conv_s)
```

```python
import functools
import math

import jax
import jax.numpy as jnp
import numpy as np
from jax import lax
from jax.experimental import pallas as pl
from jax.experimental.pallas import tpu as pltpu
from jax.experimental.pallas import tpu_sc as plsc

D_MODEL = 1024
DEPTH = 2
PAGE_SIZE = 128

GLA_HEADS = 4
GLA_DV = D_MODEL // 2 // GLA_HEADS
GLA_DK = GLA_DV // 2
GLA_RANK = 16
GLA_TAU = 16.0

NSA_HEADS = 8
NSA_KV_HEADS = 2
NSA_GROUP = NSA_HEADS // NSA_KV_HEADS
HEAD_DIM = D_MODEL // 2 // NSA_HEADS
CMP_BLK = 32
CMP_STRIDE = 16
SEL_BLK = 64
SEL_TOPN = 16
WINDOW = 512
Q_BLK = 128
FORCE_BONUS = 100.0
ROPE_DIM = HEAD_DIM // 4
ROPE_THETA = 500000.0

GLA_SIZES = (GLA_HEADS * GLA_DK, GLA_HEADS * GLA_DK, GLA_HEADS * GLA_DV, GLA_HEADS * GLA_DV, GLA_RANK)
NSA_SIZES = (NSA_HEADS * HEAD_DIM, 6 * NSA_KV_HEADS * HEAD_DIM, 3 * NSA_HEADS)

CONV_W = 31
D_CONV = D_MODEL

N_EXPERTS = 64
N_GROUPS = 8
TOPK_GROUPS = 4
TOP_K = 8
D_EXPERT = 256
ROUTE_SCALE = 2.5
MOE_BLK = 128

ALPHA = (2 * DEPTH) ** 0.25
LN_EPS = 1e-5

LANE = 128
SUBLANES = 8
V7X_VMEM_BYTES = 64 * 1024 * 1024
VMEM_LIMIT = V7X_VMEM_BYTES * 3 // 4


def _dot(a, b):
    return jnp.dot(a, b, preferred_element_type=jnp.float32)


def _dot_nt(a, b):
    return lax.dot_general(a, b, (((1,), (1,)), ((), ())), preferred_element_type=jnp.float32)


def _mm_body(x_ref, w_ref, o_ref):
    o_ref[...] = _dot(x_ref[...].astype(jnp.bfloat16), w_ref[...].astype(jnp.bfloat16))


def _mm(x, w, keep_pad=False):
    m, k = x.shape
    n = w.shape[1]
    n_pad = -(-n // LANE) * LANE
    w = w.astype(jnp.bfloat16)
    if n_pad != n:
        w = jnp.pad(w, ((0, 0), (0, n_pad - n)))
    tm = min(m, 512)
    out = pl.pallas_call(
        _mm_body,
        grid=(m // tm,),
        in_specs=[pl.BlockSpec((tm, k), lambda i: (i, 0)),
                  pl.BlockSpec((k, n_pad), lambda i: (0, 0))],
        out_specs=pl.BlockSpec((tm, n_pad), lambda i: (i, 0)),
        out_shape=jax.ShapeDtypeStruct((m, n_pad), jnp.float32),
        compiler_params=pltpu.CompilerParams(dimension_semantics=("arbitrary",),
                                             vmem_limit_bytes=VMEM_LIMIT),
        name="mm",
    )(x, w)
    return out if keep_pad or n_pad == n else out[:, :n]


def _mm_pair_body(a_ref, b_ref, w_ref, o_ref):
    ka = a_ref.shape[1]
    o_ref[...] = (_dot(a_ref[...].astype(jnp.bfloat16), w_ref[0:ka, :])
                  + _dot(b_ref[...].astype(jnp.bfloat16), w_ref[ka:, :]))


def _mm_pair(a, b, w):
    m, ka = a.shape
    kb = b.shape[1]
    n = w.shape[1]
    tm = min(m, 512)
    return pl.pallas_call(
        _mm_pair_body,
        grid=(m // tm,),
        in_specs=[pl.BlockSpec((tm, ka), lambda i: (i, 0)), pl.BlockSpec((tm, kb), lambda i: (i, 0)),
                  pl.BlockSpec((ka + kb, n), lambda i: (0, 0))],
        out_specs=pl.BlockSpec((tm, n), lambda i: (i, 0)),
        out_shape=jax.ShapeDtypeStruct((m, n), jnp.float32),
        compiler_params=pltpu.CompilerParams(dimension_semantics=("arbitrary",)),
        name="mm_pair",
    )(a, b, w.astype(jnp.bfloat16))


def _partial_rope(x, pos):
    half = ROPE_DIM // 2
    inv_freq = jnp.power(ROPE_THETA, -jnp.arange(half, dtype=jnp.float32) / half)
    ang = pos.astype(jnp.float32)[:, None] * inv_freq
    ang = ang.reshape(ang.shape[0], *([1] * (x.ndim - 3)), half)
    cos, sin = jnp.cos(ang), jnp.sin(ang)
    x1 = x[..., :half]
    x2 = x[..., half:ROPE_DIM]
    rot = jnp.concatenate([x1 * cos - x2 * sin, x2 * cos + x1 * sin], -1)
    return jnp.concatenate([rot, x[..., ROPE_DIM:]], -1)


NSA_ROWS = NSA_GROUP * Q_BLK
SEL_KT = 2048
N_SELB = 128
MASKED = -1e9
WIN_KEYS = WINDOW + Q_BLK
KK_W = 2 * HEAD_DIM + N_SELB


def _nsa_prompt_body(qr_ref, qo_ref, kc_ref, vct_ref, kk_ref, vvt_ref, g_ref, o_ref,
                     imp_ref, m_ref, l_ref, acc_ref, oct_ref, selb_ref):
    f32, bf16 = jnp.float32, jnp.bfloat16
    qb = pl.program_id(2)
    q0 = qb * Q_BLK
    qr_t = qr_ref[0, 0, 0]
    qo_t = qo_ref[0, 0, 0]
    n_cmp = kc_ref.shape[2]

    ratio = SEL_BLK // CMP_STRIDE
    chunk = min(Q_BLK, n_cmp)
    n_chunks = n_cmp // chunk

    def compressed_and_select(n_act):
        nc = n_act * chunk
        nb = nc // ratio
        s_c = _dot(kc_ref[0, 0, 0:nc, :], qr_t)
        n_idx = lax.broadcasted_iota(jnp.int32, (nc, NSA_ROWS), 0)
        qpos_c = q0 + (lax.broadcasted_iota(jnp.int32, (nc, NSA_ROWS), 1) & (Q_BLK - 1))
        cmask = (n_idx * CMP_STRIDE + (CMP_BLK - 1)) <= qpos_c
        s_c = jnp.where(cmask, s_c, MASKED)
        m_c = jnp.max(s_c, axis=0, keepdims=True)
        p_c = jnp.where(cmask, jnp.exp(s_c - m_c), 0.0)
        p_c = p_c / jnp.maximum(jnp.sum(p_c, axis=0, keepdims=True), 1e-30)
        oct_ref[...] = _dot(vct_ref[0, 0, :, 0:nc], p_c.astype(bf16))
        imp = (p_c[:, 0:Q_BLK] + p_c[:, Q_BLK:2 * Q_BLK]) + p_c[:, 2 * Q_BLK:3 * Q_BLK] + p_c[:, 3 * Q_BLK:]
        imp_ref[0:8, :] = jnp.zeros((8, Q_BLK), f32)
        imp_ref[8:8 + nc, :] = imp
        imp_s = imp_ref[pl.ds(7, nb, stride=ratio), :]
        for r in range(ratio):
            imp_s = imp_s + imp_ref[pl.ds(8 + r, nb, stride=ratio), :]
        blk = lax.broadcasted_iota(jnp.int32, (nb, Q_BLK), 0)
        qpos_s = q0 + lax.broadcasted_iota(jnp.int32, (nb, Q_BLK), 1)
        cur = lax.shift_right_logical(qpos_s, int(math.log2(SEL_BLK)))
        valid = blk * SEL_BLK <= qpos_s
        forced = (blk == 0) | (blk == cur) | (blk == cur - 1)
        score = jnp.where(valid, imp_s + jnp.where(forced, FORCE_BONUS, 0.0), -1e30)
        picked = jnp.zeros((nb, Q_BLK), f32)
        for _ in range(SEL_TOPN):
            best = jnp.max(score, axis=0, keepdims=True)
            first = jnp.min(jnp.where(score == best, blk, nb), axis=0, keepdims=True)
            hit = blk == first
            picked = jnp.where(hit, 1.0, picked)
            score = jnp.where(hit, -3e38, score)
        sel = jnp.where(valid, picked, 0.0)
        if nb < N_SELB:
            sel = jnp.concatenate([sel, jnp.zeros((N_SELB - nb, Q_BLK), f32)], axis=0)
        sel = ((sel - 1.0) * (-MASKED)).astype(bf16)
        selb_ref[...] = jnp.concatenate([sel] * NSA_GROUP, axis=1)

    need = jnp.minimum((q0 + Q_BLK - CMP_BLK) // (CMP_STRIDE * chunk) + 1, n_chunks)
    for n_act in range(1, n_chunks + 1):
        pl.when(need == n_act)(functools.partial(compressed_and_select, n_act))
    o_ct = oct_ref[...]
    selb_t = selb_ref[...]

    zeros_q = jnp.zeros((HEAD_DIM, NSA_ROWS), bf16)
    q_sel = jnp.concatenate([qo_t, zeros_q, selb_t], axis=0)
    q_win = jnp.concatenate([zeros_q, qo_t, jnp.zeros((N_SELB, NSA_ROWS), bf16)], axis=0)
    qpos_r = q0 + (lax.broadcasted_iota(jnp.int32, (1, NSA_ROWS), 1) & (Q_BLK - 1))

    def v_tiles(first, count):
        return jnp.concatenate([vvt_ref[0, 0, first + j] for j in range(count)], axis=1)

    m_ref[...] = jnp.full(m_ref.shape, MASKED, f32)
    l_ref[...] = jnp.zeros(l_ref.shape, f32)
    acc_ref[...] = jnp.zeros(acc_ref.shape, f32)

    def sel_tile(k0, kt, causal):
        s = _dot(kk_ref[0, 0, pl.ds(k0, kt), :], q_sel)
        if causal:
            kpos = k0 + lax.broadcasted_iota(jnp.int32, (kt, NSA_ROWS), 0)
            s = jnp.where(kpos <= qpos_r, s, MASKED)
        m_old = m_ref[...]
        m_new = jnp.maximum(m_old, jnp.max(s, axis=0, keepdims=True))
        alpha = jnp.exp(m_old - m_new)
        p = jnp.exp(s - m_new)
        l_ref[...] = alpha * l_ref[...] + jnp.sum(p, axis=0, keepdims=True)
        vt = v_tiles(k0 // Q_BLK, kt // Q_BLK)
        acc_ref[...] = alpha * acc_ref[...] + _dot(vt, p.astype(bf16))
        m_ref[...] = m_new

    n_full = q0 // SEL_KT

    def full_step(t, c):
        sel_tile(pl.multiple_of(t * SEL_KT, SEL_KT), SEL_KT, False)
        return c

    lax.fori_loop(0, n_full, full_step, 0)
    d0 = pl.multiple_of(n_full * SEL_KT, SEL_KT)
    short = q0 + Q_BLK - n_full * SEL_KT <= SEL_KT // 2

    @pl.when(short)
    def _():
        sel_tile(d0, SEL_KT // 2, True)

    @pl.when(jnp.logical_not(short))
    def _():
        sel_tile(d0, SEL_KT, True)
    o_st = acc_ref[0:HEAD_DIM, :] / l_ref[...]

    w0 = pl.multiple_of(jnp.maximum(q0 - WINDOW, 0), Q_BLK)
    s_w = _dot(kk_ref[0, 0, pl.ds(w0, WIN_KEYS), :], q_win)
    kpos_w = w0 + lax.broadcasted_iota(jnp.int32, (WIN_KEYS, NSA_ROWS), 0)
    s_w = jnp.where((kpos_w <= qpos_r) & (kpos_w > qpos_r - WINDOW), s_w, MASKED)
    p_w = jnp.exp(s_w - jnp.max(s_w, axis=0, keepdims=True))
    l_w = jnp.sum(p_w, axis=0, keepdims=True)
    acc_w = _dot(v_tiles(w0 // Q_BLK, WIN_KEYS // Q_BLK), p_w.astype(bf16))
    o_wt = acc_w[HEAD_DIM:2 * HEAD_DIM, :] / l_w

    g = g_ref[0, 0, 0]
    out_t = g[0:1, :] * o_ct + g[1:2, :] * o_st + g[2:3, :] * o_wt
    o_ref[0] = jnp.concatenate([out_t[:, g_ * Q_BLK:(g_ + 1) * Q_BLK] for g_ in range(NSA_GROUP)], axis=0).T


def _nsa_prompt(qr, qo, gt, kc_p, vct, kk, vvt):
    bsz, _, nqb = qr.shape[:3]
    t_ = nqb * Q_BLK
    n_cmp = kc_p.shape[2]
    per_blk = lambda b, h, i: (b, h, i, 0, 0)
    per_head = lambda b, h, i: (b, h, 0, 0)
    return pl.pallas_call(
        _nsa_prompt_body,
        grid=(bsz, NSA_KV_HEADS, nqb),
        in_specs=[pl.BlockSpec((1, 1, 1, HEAD_DIM, NSA_ROWS), per_blk),
                  pl.BlockSpec((1, 1, 1, HEAD_DIM, NSA_ROWS), per_blk),
                  pl.BlockSpec((1, 1, n_cmp, HEAD_DIM), per_head),
                  pl.BlockSpec((1, 1, HEAD_DIM, n_cmp), per_head),
                  pl.BlockSpec((1, 1, t_, KK_W), per_head),
                  pl.BlockSpec((1, 1, nqb, 2 * HEAD_DIM, Q_BLK), lambda b, h, i: (b, h, 0, 0, 0)),
                  pl.BlockSpec((1, 1, 1, 3, NSA_ROWS), per_blk)],
        out_specs=pl.BlockSpec((1, Q_BLK, NSA_GROUP * HEAD_DIM), lambda b, h, i: (b, i, h)),
        out_shape=jax.ShapeDtypeStruct((bsz, t_, NSA_HEADS * HEAD_DIM), jnp.float32),
        scratch_shapes=[pltpu.VMEM((8 + n_cmp, Q_BLK), jnp.float32),
                        pltpu.VMEM((1, NSA_ROWS), jnp.float32),
                        pltpu.VMEM((1, NSA_ROWS), jnp.float32),
                        pltpu.VMEM((2 * HEAD_DIM, NSA_ROWS), jnp.float32),
                        pltpu.VMEM((HEAD_DIM, NSA_ROWS), jnp.float32),
                        pltpu.VMEM((N_SELB, NSA_ROWS), jnp.bfloat16)],
        compiler_params=pltpu.CompilerParams(
            dimension_semantics=("arbitrary", "arbitrary", "arbitrary"),
            vmem_limit_bytes=VMEM_LIMIT),
        name="nsa_prompt",
    )(qr, qo, kc_p, vct, kk, vvt, gt)


GLA_SUB = 16
GLA_UNROLL = 8
GLA_TILE = 256
GLA_QK = GLA_HEADS * GLA_DK
GLA_V = GLA_HEADS * GLA_DV


def _dot_tn(a, b):
    return lax.dot_general(a, b, (((0,), (0,)), ((), ())), preferred_element_type=jnp.float32)


def _gla_body(q_ref, k_ref, v_ref, gr_ref, glr_ref, wg_ref, bg_ref, ng_ref, s0_ref, exp_ref,
              o_ref, sfin_ref, st_ref, b_ref, qd_ref, *, t_valid):
    f32, bf16 = jnp.float32, jnp.bfloat16
    tt = q_ref.shape[1]
    ti = pl.program_id(1)

    @pl.when(ti == 0)
    def _():
        st_ref[...] = s0_ref[0]

    row = lax.broadcasted_iota(jnp.int32, (tt, 1), 0)
    z = _dot(glr_ref[0][:, :GLA_RANK].astype(bf16), wg_ref[...]) + bg_ref[...]
    la = (jnp.minimum(z, 0.0) - jnp.log1p(jnp.exp(-jnp.abs(z)))) * (1.0 / GLA_TAU)
    la = jnp.where(ti * tt + row < t_valid, la, 0.0)
    seg = row & (GLA_SUB - 1)
    b = la
    for s in (1, 2, 4, 8):
        b = b + jnp.where(seg >= s, pltpu.roll(b, s, axis=0), 0.0)
    q = q_ref[0] * (GLA_DK ** -0.5)
    k = k_ref[0]
    v = v_ref[0]
    o = _dot((q * k).astype(bf16), exp_ref[...]) * v
    for d in range(1, GLA_SUB):
        decay = jnp.exp(jnp.minimum(b - pltpu.roll(b, d, axis=0), 0.0))
        w = jnp.where(seg >= d, q * pltpu.roll(k, d, axis=0) * decay, 0.0)
        o = o + _dot(w.astype(bf16), exp_ref[...]) * pltpu.roll(v, d, axis=0)
    o_ref[0] = o
    b_ref[...] = b
    qd_ref[...] = (q * jnp.exp(b)).astype(bf16)

    def block_step(c, carry):
        rows = pl.ds(pl.multiple_of(c * GLA_SUB, GLA_SUB), GLA_SUB)
        qd = qd_ref[rows, :]
        bc = b_ref[rows, :]
        bl = bc[GLA_SUB - 1:GLA_SUB, :]
        kc = (k_ref[0, rows, :] * jnp.exp(bl - bc)).astype(bf16)
        keep = jnp.exp(bl)
        vb = v_ref[0, rows, :].astype(bf16)
        outs = []
        for h in range(GLA_HEADS):
            dk = slice(h * GLA_DK, (h + 1) * GLA_DK)
            dv = slice(h * GLA_DV, (h + 1) * GLA_DV)
            st = st_ref[dv, :]
            outs.append(_dot_nt(qd[:, dk], st.astype(bf16)))
            st_ref[dv, :] = st * keep[:, dk] + _dot_tn(vb[:, dv], kc[:, dk])
        o_ref[0, rows, :] += jnp.concatenate(outs, axis=1)
        return carry

    lax.fori_loop(0, tt // GLA_SUB, block_step, 0, unroll=GLA_UNROLL)
    sfin_ref[0] = st_ref[...]
    gr = gr_ref[0]
    gate = gr * jax.nn.sigmoid(gr)
    for h in range(GLA_HEADS):
        cols = slice(h * GLA_DV, (h + 1) * GLA_DV)
        oh = o_ref[0, :, cols]
        ms = jnp.mean(oh * oh, axis=-1, keepdims=True)
        o_ref[0, :, cols] = oh * lax.rsqrt(ms + LN_EPS) * ng_ref[...] * gate[:, cols]


def _gla(h, w_gla_gate, b_gla_gate, gla_norm_g, gla_state):
    bsz, t_, n_in = h.shape
    tp = -(-t_ // GLA_SUB) * GLA_SUB
    if tp != t_:
        h = jnp.pad(h, ((0, 0), (0, tp - t_), (0, 0)))
    tt = min(tp, GLA_TILE)
    expand =np.repeat(np.repeat(np.eye(GLA_HEADS, dtype=np.float32), GLA_DK, 0), GLA_DV, 1)
    if gla_state is None:
        s0 = jnp.zeros((bsz, GLA_V, GLA_DK), jnp.float32)
    else:
        s0 = gla_state.transpose(0, 1, 3, 2).reshape(bsz, GLA_V, GLA_DK)
    tile = lambda width, blk: pl.BlockSpec((1, tt, width), lambda b, i: (b, i, blk))
    fixed2 = lambda shape: pl.BlockSpec(shape, lambda b, i: (0, 0))
    per_b = pl.BlockSpec((1, GLA_V, GLA_DK), lambda b, i: (b, 0, 0))
    o, s_t = pl.pallas_call(
        functools.partial(_gla_body, t_valid=t_),
        grid=(bsz, tp // tt),
        in_specs=[tile(GLA_QK, 0), tile(GLA_QK, 1), tile(GLA_V, 1), tile(GLA_V, 2),
                  tile(LANE, (2 * GLA_QK + 2 * GLA_V + NSA_SIZES[0] + NSA_SIZES[1]) // LANE),
                  fixed2((GLA_RANK, GLA_QK)), fixed2((1, GLA_QK)), fixed2((1, GLA_DV)), per_b,
                  fixed2((GLA_QK, GLA_V))],
        out_specs=[pl.BlockSpec((1, tt, GLA_V), lambda b, i: (b, i, 0)), per_b],
        out_shape=[jax.ShapeDtypeStruct((bsz, tp, GLA_V), jnp.float32),
                   jax.ShapeDtypeStruct((bsz, GLA_V, GLA_DK), jnp.float32)],
        scratch_shapes=[pltpu.VMEM((GLA_V, GLA_DK), jnp.float32), pltpu.VMEM((tt, GLA_QK), jnp.float32),
                        pltpu.VMEM((tt, GLA_QK), jnp.bfloat16)],
        compiler_params=pltpu.CompilerParams(dimension_semantics=("arbitrary", "arbitrary"),
                                             vmem_limit_bytes=VMEM_LIMIT),
        name="gla",
    )(h, h, h, h, h, w_gla_gate.astype(jnp.bfloat16), b_gla_gate.reshape(1, GLA_QK),
      gla_norm_g.reshape(1, GLA_DV), s0, jnp.asarray(expand, jnp.bfloat16))
    return o[:, :t_], s_t.reshape(bsz, GLA_HEADS, GLA_DV, GLA_DK).transpose(0, 1, 3, 2)


COL_NQ = 2 * GLA_QK + 2 * GLA_V
COL_NKV = COL_NQ + NSA_SIZES[0]
COL_TAIL = COL_NKV + NSA_SIZES[1]
TAIL_GATE = GLA_RANK
_ORIG = np.cumsum((0,) + GLA_SIZES + NSA_SIZES)
IN_AB_PERM = np.concatenate([np.arange(_ORIG[0], _ORIG[4]), np.arange(_ORIG[5], _ORIG[7]),
                             np.arange(_ORIG[4], _ORIG[5]), np.arange(_ORIG[7], _ORIG[8])])
SUBS = Q_BLK // CMP_STRIDE


def _nsa_prep_body(nq_ref, kv0_ref, kv1_ref, kv2_ref, tail_ref, rc_ref, ru_ref, rd_ref, pool_ref,
                   rows_ref, win_ref, kk_ref, vvt_ref, qr_ref, qo_ref, g_ref, pooled_ref):
    bf16 = jnp.bfloat16
    q0 = pl.program_id(1) * Q_BLK
    kv_w = NSA_KV_HEADS * HEAD_DIM

    def rope(x):
        reps = x.shape[1] // LANE
        wide = lambda r: jnp.concatenate([r[...]] * reps, axis=1) if reps > 1 else r[...]
        half = ROPE_DIM // 2
        return (x * wide(rc_ref) + pltpu.roll(x, half, axis=1) * wide(ru_ref)
                + pltpu.roll(x, x.shape[1] - half, axis=1) * wide(rd_ref))

    kv0, kv1, kv2 = kv0_ref[0], kv1_ref[0], kv2_ref[0]
    k_sel, v_sel = rope(kv1[:, :kv_w]), kv1[:, kv_w:]
    k_win, v_win = rope(kv2[:, :kv_w]), kv2[:, kv_w:]
    rows_ref[0] = jnp.concatenate([kv0, k_sel, v_sel], axis=1)
    win_ref[0] = jnp.concatenate([k_win, v_win], axis=1)
    blk_id = lax.shift_right_logical(q0 + lax.broadcasted_iota(jnp.int32, (Q_BLK, N_SELB), 0),
                                     int(math.log2(SEL_BLK)))
    onehot = jnp.where(lax.broadcasted_iota(jnp.int32, (Q_BLK, N_SELB), 1) == blk_id, 1.0, 0.0).astype(bf16)
    q = nq_ref[0] * (HEAD_DIM ** -0.5)
    q_rot = rope(q)
    gates_t = jax.nn.sigmoid(tail_ref[0]).T
    for h in range(NSA_KV_HEADS):
        hs = slice(h * HEAD_DIM, (h + 1) * HEAD_DIM)
        kk_ref[0, h] = jnp.concatenate([k_sel[:, hs].astype(bf16), k_win[:, hs].astype(bf16), onehot], axis=1)
        vvt_ref[0, h, 0] = jnp.concatenate([v_sel[:, hs], v_win[:, hs]], axis=1).T.astype(bf16)
        gw = NSA_GROUP * HEAD_DIM
        for src, dst in ((q, qr_ref), (q_rot, qo_ref)):
            t = src[:, h * gw:(h + 1) * gw].T
            dst[0, h, 0] = jnp.concatenate([t[g * HEAD_DIM:(g + 1) * HEAD_DIM] for g in range(NSA_GROUP)],
                                           axis=1).astype(bf16)
        base = TAIL_GATE + h * NSA_GROUP * 3
        g_ref[0, h, 0] = jnp.concatenate(
            [jnp.concatenate([gates_t[base + 3 * g + c:base + 3 * g + c + 1] for g in range(NSA_GROUP)], axis=1)
             for c in range(3)], axis=0)
    kc_in, vc_in = kv0[:, :kv_w].astype(bf16), kv0[:, kv_w:].astype(bf16)
    pooled_ref[0] = jnp.concatenate([_dot(pool_ref[0], kc_in), _dot(pool_ref[1], kc_in),
                                     _dot(pool_ref[2], vc_in), _dot(pool_ref[3], vc_in)], axis=1)


def _nsa_prep(h, pos, w_cmp_pool):
    bsz, t_, _ = h.shape
    nqb = t_ // Q_BLK
    bf16 = jnp.bfloat16
    half = ROPE_DIM // 2
    inv_freq = jnp.power(ROPE_THETA, -jnp.arange(half, dtype=jnp.float32) / half)
    ang = pos.astype(jnp.float32)[:, None] * inv_freq
    cos, sin = jnp.cos(ang), jnp.sin(ang)
    rest = HEAD_DIM - ROPE_DIM
    z8, zr = jnp.zeros((t_, half), jnp.float32), jnp.zeros((t_, rest), jnp.float32)
    two = lambda a: jnp.concatenate([a, a], axis=1)
    rc = two(jnp.concatenate([cos, cos, jnp.ones((t_, rest), jnp.float32)], axis=1))
    ru = two(jnp.concatenate([z8, sin, zr], axis=1))
    rd = two(jnp.concatenate([-sin, z8, zr], axis=1))
    pool = _pool_matrices(w_cmp_pool)
    kv_w = NSA_KV_HEADS * HEAD_DIM
    col = lambda width, off: pl.BlockSpec((1, Q_BLK, width), lambda b, i: (b, i, off // width))
    rows_t = pl.BlockSpec((Q_BLK, LANE), lambda b, i: (i, 0))
    head4 = lambda r, c: pl.BlockSpec((1, NSA_KV_HEADS, 1, r, c), lambda b, i: (b, 0, i, 0, 0))
    return pl.pallas_call(
        _nsa_prep_body,
        grid=(bsz, nqb),
        in_specs=[col(NSA_SIZES[0], COL_NQ), col(2 * kv_w, COL_NKV), col(2 * kv_w, COL_NKV + 2 * kv_w),
                  col(2 * kv_w, COL_NKV + 4 * kv_w), col(LANE, COL_TAIL), rows_t, rows_t, rows_t,
                  pl.BlockSpec((4, SUBS, Q_BLK), lambda b, i: (0, 0, 0))],
        out_specs=[pl.BlockSpec((1, Q_BLK, 4 * kv_w), lambda b, i: (b, i, 0)),
                   pl.BlockSpec((1, Q_BLK, 2 * kv_w), lambda b, i: (b, i, 0)),
                   pl.BlockSpec((1, NSA_KV_HEADS, Q_BLK, KK_W), lambda b, i: (b, 0, i, 0)),
                   head4(2 * HEAD_DIM, Q_BLK), head4(HEAD_DIM, NSA_ROWS), head4(HEAD_DIM, NSA_ROWS),
                   head4(3, NSA_ROWS),
                   pl.BlockSpec((1, SUBS, 4 * kv_w), lambda b, i: (b, i, 0))],
        out_shape=[jax.ShapeDtypeStruct((bsz, t_, 4 * kv_w), jnp.float32),
                   jax.ShapeDtypeStruct((bsz, t_, 2 * kv_w), jnp.float32),
                   jax.ShapeDtypeStruct((bsz, NSA_KV_HEADS, t_, KK_W), bf16),
                   jax.ShapeDtypeStruct((bsz, NSA_KV_HEADS, nqb, 2 * HEAD_DIM, Q_BLK), bf16),
                   jax.ShapeDtypeStruct((bsz, NSA_KV_HEADS, nqb, HEAD_DIM, NSA_ROWS), bf16),
                   jax.ShapeDtypeStruct((bsz, NSA_KV_HEADS, nqb, HEAD_DIM, NSA_ROWS), bf16),
                   jax.ShapeDtypeStruct((bsz, NSA_KV_HEADS, nqb, 3, NSA_ROWS), jnp.float32),
                   jax.ShapeDtypeStruct((bsz, t_ // CMP_STRIDE, 4 * kv_w), jnp.float32)],
        compiler_params=pltpu.CompilerParams(dimension_semantics=("arbitrary", "arbitrary")),
        name="nsa_prep",
    )(h, h, h, h, h, rc, ru, rd, pool)


PAGE_GROUP = 32
DEC_KEYS = PAGE_GROUP * PAGE_SIZE
POOL_ROWS = 2048
NEW_PAD = 8
KV_W = NSA_KV_HEADS * HEAD_DIM


def _dec_pool_body(pt_ref, *refs):
    page_refs, pool_ref, out_ref = refs[:PAGE_GROUP], refs[PAGE_GROUP], refs[PAGE_GROUP + 1]
    bf16 = jnp.bfloat16
    pages = [pr[0] for pr in page_refs]
    per = POOL_ROWS // PAGE_SIZE
    cols = []
    for g0 in range(0, PAGE_GROUP, per):
        kc_t = jnp.concatenate([p[:KV_W] for p in pages[g0:g0 + per]], axis=1).astype(bf16)
        vc_t = jnp.concatenate([p[KV_W:] for p in pages[g0:g0 + per]], axis=1).astype(bf16)
        cols.append(jnp.concatenate([_dot(kc_t, pool_ref[0]), _dot(kc_t, pool_ref[1]),
                                     _dot(vc_t, pool_ref[2]), _dot(vc_t, pool_ref[3])], axis=0))
    out_ref[0] = jnp.concatenate(cols, axis=1)


def _page_specs(n_pages, col_blk):
    def spec(i):
        return pl.BlockSpec((1, 2 * KV_W, PAGE_SIZE),
                            lambda b, j, pt: (pt[b * n_pages + j * PAGE_GROUP + i], col_blk, 0))
    return [spec(i) for i in range(PAGE_GROUP)]


def _dec_pool(cache, page_table, pool):
    bsz, n_pages = page_table.shape
    grid_spec = pltpu.PrefetchScalarGridSpec(
        num_scalar_prefetch=1, grid=(bsz, n_pages // PAGE_GROUP),
        in_specs=_page_specs(n_pages, 0) + [pl.BlockSpec(pool.shape, lambda b, j, pt: (0, 0, 0))],
        out_specs=pl.BlockSpec((1, 4 * KV_W, PAGE_GROUP * SUBS), lambda b, j, pt: (b, 0, j)))
    return pl.pallas_call(
        _dec_pool_body, grid_spec=grid_spec,
        out_shape=jax.ShapeDtypeStruct((bsz, 4 * KV_W, n_pages * SUBS), jnp.float32),
        compiler_params=pltpu.CompilerParams(dimension_semantics=("arbitrary", "arbitrary")),
        name="nsa_dec_pool",
    )(page_table.reshape(-1), *([cache] * PAGE_GROUP), pool)


def _dec_select_body(qr_ref, kct_ref, vc_ref, band_ref, oc_ref, selb_ref, *, qpos0, n_q, n_pick, n_blk):
    f32, bf16 = jnp.float32, jnp.bfloat16
    n_cmp = kct_ref.shape[3]
    rows = NSA_GROUP * n_q
    for sq, h in [(a, b) for a in range(qr_ref.shape[0]) for b in range(NSA_KV_HEADS)]:
        s_c = _dot(qr_ref[sq, h], kct_ref[sq, h])
        n_idx = lax.broadcasted_iota(jnp.int32, (rows, n_cmp), 1)
        qpos = qpos0 + (lax.broadcasted_iota(jnp.int32, (rows, n_cmp), 0) % n_q)
        cmask = (n_idx * CMP_STRIDE + (CMP_BLK - 1)) <= qpos
        s_c = jnp.where(cmask, s_c, MASKED)
        p_c = jnp.where(cmask, jnp.exp(s_c - jnp.max(s_c, axis=1, keepdims=True)), 0.0)
        p_c = p_c / jnp.maximum(jnp.sum(p_c, axis=1, keepdims=True), 1e-30)
        oc_ref[sq, h] = _dot(p_c.astype(bf16), vc_ref[sq, h])
        imp = p_c[0:n_q]
        for g in range(1, NSA_GROUP):
            imp = imp + p_c[g * n_q:(g + 1) * n_q]
        imp_s = jnp.zeros((n_q, N_SELB), f32)
        rem = imp
        for _ in range(3):
            part = rem.astype(bf16)
            imp_s = imp_s + _dot(part, band_ref[...])
            rem = rem - part.astype(f32)
        blk = lax.broadcasted_iota(jnp.int32, (n_q, N_SELB), 1)
        qpos_s = qpos0 + lax.broadcasted_iota(jnp.int32, (n_q, N_SELB), 0)
        cur = lax.shift_right_logical(qpos_s, int(math.log2(SEL_BLK)))
        valid = (blk * SEL_BLK <= qpos_s) & (blk < n_blk)
        forced = (blk == 0) | (blk == cur) | (blk == cur - 1)
        score = jnp.where(valid, imp_s + jnp.where(forced, FORCE_BONUS, 0.0), -1e30)
        picked = jnp.zeros((n_q, N_SELB), f32)
        for _ in range(n_pick):
            best = jnp.max(score, axis=1, keepdims=True)
            first = jnp.min(jnp.where(score == best, blk, N_SELB), axis=1, keepdims=True)
            hit = blk == first
            picked = jnp.where(hit, 1.0, picked)
            score = jnp.where(hit, -3e38, score)
        selb_ref[sq, h] = (jnp.where(valid, picked, 0.0) - 1.0) * (-MASKED)


def _dec_select(qr, kct, vc, n_q, qpos0, n_pick, n_blk):
    bsz = qr.shape[0]
    rows = NSA_GROUP * n_q
    n_cmp = kct.shape[3]
    ratio = SEL_BLK // CMP_STRIDE
    c_idx, j_idx = np.arange(n_cmp)[:, None], np.arange(N_SELB)[None, :]
    band = jnp.asarray(((c_idx >= ratio * j_idx - 1) & (c_idx <= ratio * j_idx + ratio - 1)), jnp.bfloat16)
    per_step = next(c for c in (4, 2, 1) if bsz % c == 0)
    per_b = lambda *tail: pl.BlockSpec((per_step, NSA_KV_HEADS) + tail, lambda b: (b, 0, 0, 0))
    return pl.pallas_call(
        functools.partial(_dec_select_body, qpos0=qpos0, n_q=n_q, n_pick=n_pick, n_blk=n_blk),
        grid=(bsz // per_step,),
        in_specs=[per_b(rows, HEAD_DIM), per_b(HEAD_DIM, n_cmp), per_b(n_cmp, HEAD_DIM),
                  pl.BlockSpec((n_cmp, N_SELB), lambda b: (0, 0))],
        out_specs=[per_b(rows, HEAD_DIM), per_b(n_q, N_SELB)],
        out_shape=[jax.ShapeDtypeStruct((bsz, NSA_KV_HEADS, rows, HEAD_DIM), jnp.float32),
                   jax.ShapeDtypeStruct((bsz, NSA_KV_HEADS, n_q, N_SELB), jnp.float32)],
        compiler_params=pltpu.CompilerParams(dimension_semantics=("arbitrary",)),
        name="nsa_dec_select",
    )(qr, kct, vc, band)


def _dec_attend_body(pt_ref, *refs, qpos0, n_q, past):
    page_refs = refs[:PAGE_GROUP]
    (qs_ref, qw_ref, knew_ref, vnew_ref, wbuf_ref, wnew_ref, oc_ref, g_ref,
     o_ref, m_ref, l_ref, acc_ref) = refs[PAGE_GROUP:]
    f32, bf16 = jnp.float32, jnp.bfloat16
    j = pl.program_id(1)
    n_rows = qs_ref.shape[1]

    @pl.when(j == 0)
    def _():
        m_ref[...] = jnp.full(m_ref.shape, MASKED, f32)
        l_ref[...] = jnp.zeros(l_ref.shape, f32)
        acc_ref[...] = jnp.zeros(acc_ref.shape, f32)

    def online(s, weigh):
        m_old = m_ref[...]
        m_new = jnp.maximum(m_old, jnp.max(s, axis=1, keepdims=True))
        alpha = jnp.exp(m_old - m_new)
        p = jnp.exp(s - m_new)
        l_ref[...] = alpha * l_ref[...] + jnp.sum(p, axis=1, keepdims=True)
        acc_ref[...] = alpha * acc_ref[...] + weigh(p.astype(bf16))
        m_ref[...] = m_new

    qs = qs_ref[0]
    pages = [pr[0] for pr in page_refs]
    keys_t = jnp.concatenate([p[:KV_W] for p in pages], axis=1).astype(bf16)
    vals_t = jnp.concatenate([p[KV_W:] for p in pages], axis=1).astype(bf16)
    blk_id = j * (DEC_KEYS // SEL_BLK) + lax.shift_right_logical(
        lax.broadcasted_iota(jnp.int32, (N_SELB, DEC_KEYS), 1), int(math.log2(SEL_BLK)))
    onehot_t = jnp.where(lax.broadcasted_iota(jnp.int32, (N_SELB, DEC_KEYS), 0) == blk_id, 1.0, 0.0).astype(bf16)
    online(_dot(qs, jnp.concatenate([keys_t, onehot_t], axis=0)), lambda p: _dot_nt(p, vals_t))

    @pl.when(j == pl.num_programs(1) - 1)
    def _():
        row_q = qpos0 + (lax.broadcasted_iota(jnp.int32, (n_rows, 1), 0) % n_q)
        qh = qw_ref[0]
        new_pos = past + lax.broadcasted_iota(jnp.int32, (n_rows, NEW_PAD), 1)
        new_ok = (new_pos <= row_q) & (new_pos < past + n_q)
        s_new = jnp.where(new_ok, _dot_nt(qh, knew_ref[0]), MASKED)
        online(s_new, lambda p: _dot(p, vnew_ref[0]))
        o_s = acc_ref[...] / l_ref[...]
        wbuf_t = wbuf_ref[0]
        wnew = wnew_ref[0]
        n_buf = wbuf_t.shape[1]
        s_b = _dot(qh, wbuf_t[:KV_W].astype(bf16))
        pos_b = (past - n_buf) + lax.broadcasted_iota(jnp.int32, (n_rows, n_buf), 1)
        s_b = jnp.where((pos_b > row_q - WINDOW) & (pos_b >= 0), s_b, MASKED)
        s_n = jnp.where(new_ok, _dot_nt(qh, wnew[:, :KV_W].astype(bf16)), MASKED)
        m_w = jnp.maximum(jnp.max(s_b, axis=1, keepdims=True), jnp.max(s_n, axis=1, keepdims=True))
        p_b, p_n = jnp.exp(s_b - m_w), jnp.exp(s_n - m_w)
        l_w = jnp.sum(p_b, axis=1, keepdims=True) + jnp.sum(p_n, axis=1, keepdims=True)
        o_w = (_dot_nt(p_b.astype(bf16), wbuf_t[KV_W:].astype(bf16))
               + _dot(p_n.astype(bf16), wnew[:, KV_W:].astype(bf16))) / l_w
        half = n_rows // NSA_KV_HEADS
        own = lambda a: jnp.concatenate([a[h * half:(h + 1) * half, h * HEAD_DIM:(h + 1) * HEAD_DIM]
                                         for h in range(NSA_KV_HEADS)], axis=0)
        g = g_ref[0]
        o_ref[0] = g[:, 0:1] * oc_ref[0] + g[:, 1:2] * own(o_s) + g[:, 2:3] * own(o_w)


def _dec_attend(cache, page_table, qs, qw, knew, vnew, wbuf, wnew, o_c, gates, n_q, qpos0):
    bsz, n_pages = page_table.shape
    n_rows = qs.shape[1]
    per_b = lambda *tail: pl.BlockSpec((1,) + tail, lambda b, j, pt: (b, 0, 0))
    grid_spec = pltpu.PrefetchScalarGridSpec(
        num_scalar_prefetch=1, grid=(bsz, n_pages // PAGE_GROUP),
        in_specs=_page_specs(n_pages, 1) + [
            per_b(n_rows, KV_W + N_SELB), per_b(n_rows, KV_W), per_b(NEW_PAD, KV_W), per_b(NEW_PAD, KV_W),
            per_b(2 * KV_W, wbuf.shape[2]), per_b(NEW_PAD, 2 * KV_W), per_b(n_rows, HEAD_DIM), per_b(n_rows, 3)],
        out_specs=per_b(n_rows, HEAD_DIM),
        scratch_shapes=[pltpu.VMEM((n_rows, 1), jnp.float32), pltpu.VMEM((n_rows, 1), jnp.float32),
                        pltpu.VMEM((n_rows, KV_W), jnp.float32)])
    return pl.pallas_call(
        functools.partial(_dec_attend_body, qpos0=qpos0, n_q=n_q, past=n_pages * PAGE_SIZE),
        grid_spec=grid_spec,
        out_shape=jax.ShapeDtypeStruct((bsz, n_rows, HEAD_DIM), jnp.float32),
        compiler_params=pltpu.CompilerParams(dimension_semantics=("arbitrary", "arbitrary")),
        name="nsa_dec_attend",
    )(page_table.reshape(-1), *([cache] * PAGE_GROUP), qs, qw, knew, vnew, wbuf, wnew, o_c, gates)


def _pool_matrices(w_cmp_pool, rows=Q_BLK):
    subs = rows // CMP_STRIDE
    sub = np.arange(rows) // CMP_STRIDE == np.arange(subs)[:, None]
    w_rep = jnp.tile(w_cmp_pool.reshape(2, 2, CMP_STRIDE), (1, 1, subs))
    return jnp.where(sub[None, None], w_rep[:, :, None, :], 0.0).reshape(4, subs, rows).astype(jnp.bfloat16)


def _nsa_decode(q_raw, q_rot, gates, rows_full, rows_win, cache, page_table, win_buf, w_cmp_pool, past):
    bsz, n_q = q_raw.shape[:2]
    bf16 = jnp.bfloat16
    n_blk = past // SEL_BLK
    assert past % DEC_KEYS == 0 and n_blk <= N_SELB and n_q <= NEW_PAD
    scale = HEAD_DIM ** -0.5
    cache2 = cache.transpose(0, 2, 3, 4, 1).reshape(cache.shape[0], 4 * KV_W, PAGE_SIZE)
    pooled_t = _dec_pool(cache2, page_table, _pool_matrices(w_cmp_pool, POOL_ROWS).transpose(0, 2, 1))
    pooled_t = pooled_t.reshape(bsz, 4, NSA_KV_HEADS, HEAD_DIM, -1)
    last = ((0, 0), (0, 0), (0, 0), (0, 1))
    kct = jnp.pad(pooled_t[:, 0, ..., :-1] + pooled_t[:, 1, ..., 1:], last)
    vc_p = jnp.pad(pooled_t[:, 2, ..., :-1] + pooled_t[:, 3, ..., 1:], last).transpose(0, 1, 3, 2)
    rows_of = lambda a: a.transpose(0, 2, 3, 1, 4).reshape(bsz, NSA_KV_HEADS, NSA_GROUP * n_q, a.shape[-1])
    qr = rows_of((q_raw * scale).astype(bf16))
    n_pick = min(SEL_TOPN, n_blk + 1) - 1
    o_c, selb = _dec_select(qr, kct.astype(bf16), vc_p.astype(bf16), n_q, past, n_pick, n_blk)
    qo = rows_of((q_rot * scale).astype(bf16))
    zero = jnp.zeros_like(qo[:, 0])
    qw = jnp.concatenate([jnp.concatenate([qo[:, 0], zero], -1), jnp.concatenate([zero, qo[:, 1]], -1)], axis=1)
    bias = jnp.tile(selb, (1, 1, NSA_GROUP, 1)).reshape(bsz, -1, N_SELB).astype(bf16)
    qs = jnp.concatenate([qw, bias], axis=-1)
    pad_new = lambda a: jnp.pad(a.reshape(bsz, n_q, -1), ((0, 0), (0, NEW_PAD - n_q), (0, 0)))
    knew = pad_new(rows_full[:, :, 2]).astype(bf16)
    vnew = pad_new(rows_full[:, :, 3]).astype(bf16)
    wnew = pad_new(rows_win)
    wbuf = win_buf.transpose(0, 2, 3, 4, 1).reshape(bsz, 2 * KV_W, win_buf.shape[1])
    gt = rows_of(gates).reshape(bsz, -1, 3)
    o = _dec_attend(cache2, page_table, qs, qw, knew, vnew, wbuf, wnew,
                    o_c.reshape(bsz, -1, HEAD_DIM), gt, n_q, past)
    o = o.reshape(bsz, NSA_KV_HEADS, NSA_GROUP, n_q, HEAD_DIM).transpose(0, 3, 1, 2, 4)
    return o.reshape(bsz, n_q, NSA_HEADS * HEAD_DIM)


def _ab_mixer(x, pos, w_in, w_gla_gate, b_gla_gate, gla_norm_g, w_cmp_pool, w_out,
              gla_state, nsa_cache, page_table, win_buf):
    bsz, t_, _ = x.shape
    h_in = _mm(x.reshape(bsz * t_, -1), w_in[:, IN_AB_PERM], keep_pad=True).reshape(bsz, t_, -1)
    o_a, s_a = _gla(h_in, w_gla_gate, b_gla_gate, gla_norm_g, gla_state)
    kv_w = NSA_KV_HEADS * HEAD_DIM
    if nsa_cache is None:
        rows2, win2, kk, vvt, qr, qo, gt, pooled = _nsa_prep(h_in, pos, w_cmp_pool)
        pooled = pooled.reshape(bsz, t_ // CMP_STRIDE, 4, NSA_KV_HEADS, HEAD_DIM)
        kc = pooled[:, :-1, 0] + pooled[:, 1:, 1]
        vc = pooled[:, :-1, 2] + pooled[:, 1:, 3]
        kc_p = jnp.pad(kc, ((0, 0), (0, 1), (0, 0), (0, 0))).transpose(0, 2, 1, 3).astype(jnp.bfloat16)
        vct = jnp.pad(vc, ((0, 0), (0, 1), (0, 0), (0, 0))).transpose(0, 2, 3, 1).astype(jnp.bfloat16)
        o_b = _nsa_prompt(qr, qo, gt, kc_p, vct, kk, vvt)
        rows_full = rows2.reshape(bsz, t_, 4, NSA_KV_HEADS, HEAD_DIM)
        new_win = win2[:, -min(WINDOW, t_):].reshape(bsz, -1, 2, NSA_KV_HEADS, HEAD_DIM)
    else:
        nq = h_in[..., COL_NQ:COL_NKV]
        nkv = h_in[..., COL_NKV:COL_TAIL]
        ngate = h_in[..., COL_TAIL + TAIL_GATE:COL_TAIL + TAIL_GATE + NSA_SIZES[2]]
        q_raw = nq.reshape(bsz, t_, NSA_KV_HEADS, NSA_GROUP, HEAD_DIM)
        q_rot = _partial_rope(q_raw, pos)
        kv = nkv.reshape(bsz, t_, 6, NSA_KV_HEADS, HEAD_DIM)
        k_sel = _partial_rope(kv[:, :, 2], pos)
        k_win = _partial_rope(kv[:, :, 4], pos)
        rows_full = jnp.stack([kv[:, :, 0], kv[:, :, 1], k_sel, kv[:, :, 3]], axis=2)
        rows_win = jnp.stack([k_win, kv[:, :, 5]], axis=2)
        gates = jax.nn.sigmoid(ngate).reshape(bsz, t_, NSA_KV_HEADS, NSA_GROUP, 3)
        past_len = page_table.shape[1] * PAGE_SIZE
        o_b = _nsa_decode(q_raw, q_rot, gates, rows_full, rows_win, nsa_cache, page_table, win_buf,
                          w_cmp_pool, past_len)
        w_buf = win_buf.shape[1]
        kw = jnp.concatenate([win_buf, rows_win], axis=1)
        new_win = kw[:, -w_buf:]
    y = _mm_pair(o_a.reshape(bsz * t_, -1), o_b.reshape(bsz * t_, -1), w_out).reshape(bsz, t_, -1)
    return y, s_a, rows_full, new_win


CONV_HALO = 32
CONV_LEAD = CONV_HALO - (CONV_W - 1)


def _conv_body(x_ref, buf0_ref, w1_ref, b1_ref, wdw_ref, bdw_ref, g_ref, b_ref, w2_ref, b2_ref,
               o_ref, tail_ref, ext_ref, z_ref, *, t_last):
    bf16 = jnp.bfloat16
    tt = x_ref.shape[1]
    i = pl.program_id(1)

    @pl.when(i == 0)
    def _():
        ext_ref[0:CONV_HALO, :] = buf0_ref[0]
        ext_ref[CONV_HALO + tt:CONV_HALO + tt + SUBLANES, :] = jnp.zeros((SUBLANES, D_CONV), jnp.float32)

    h = _dot(x_ref[0].astype(bf16), w1_ref[...]) + b1_ref[...]
    ext_ref[CONV_HALO:CONV_HALO + tt, :] = h[:, :D_CONV] * jax.nn.sigmoid(h[:, D_CONV:])
    c = jnp.zeros((tt, D_CONV), jnp.float32) + bdw_ref[...]
    for r in range(SUBLANES):
        z = None
        for a in range(CONV_HALO // SUBLANES + 1):
            k = SUBLANES * a + r - CONV_LEAD
            if 0 <= k < CONV_W:
                term = ext_ref[SUBLANES * a:SUBLANES * a + tt + SUBLANES, :] * wdw_ref[k:k + 1, :]
                z = term if z is None else z + term
        if r == 0:
            c = c + z[:tt]
        else:
            z_ref[...] = z
            c = c + z_ref[pl.ds(r, tt), :]
    c = _ln_rows(c, g_ref[...], b_ref[...])
    c = c * jax.nn.sigmoid(c)
    o_ref[0] = _dot(c.astype(bf16), w2_ref[...]) + b2_ref[...]
    tail_ref[0] = ext_ref[t_last:t_last + CONV_HALO, :]
    ext_ref[0:CONV_HALO, :] = ext_ref[tt:tt + CONV_HALO, :]


def _conv_module(x, conv_buf, w_pw1, b_pw1, w_dw, b_dw, ln_g, ln_b, w_pw2, b_pw2):
    bsz, t_, d = x.shape
    bf16 = jnp.bfloat16
    tp = -(-t_ // 8) * 8
    tt = min(tp, 256)
    n_t = tp // tt
    if tp != t_:
        x = jnp.pad(x, ((0, 0), (0, tp - t_), (0, 0)))
    if conv_buf is None:
        buf0 = jnp.zeros((bsz, CONV_HALO, D_CONV), jnp.float32)
    else:
        buf0 = jnp.pad(conv_buf, ((0, 0), (CONV_LEAD, 0), (0, 0)))
    fixed = lambda shape: pl.BlockSpec(shape, lambda b, i: (0,) * len(shape))
    per_b = pl.BlockSpec((1, CONV_HALO, D_CONV), lambda b, i: (b, 0, 0))
    out, tail = pl.pallas_call(
        functools.partial(_conv_body, t_last=t_ - (n_t - 1) * tt),
        grid=(bsz, n_t),
        in_specs=[pl.BlockSpec((1, tt, d), lambda b, i: (b, i, 0)), per_b,
                  fixed((d, 2 * D_CONV)), fixed((1, 2 * D_CONV)), fixed((CONV_HALO, D_CONV)), fixed((1, D_CONV)),
                  fixed((1, D_CONV)), fixed((1, D_CONV)), fixed((D_CONV, d)), fixed((1, d))],
        out_specs=[pl.BlockSpec((1, tt, d), lambda b, i: (b, i, 0)), per_b],
        out_shape=[jax.ShapeDtypeStruct((bsz, tp, d), jnp.float32),
                   jax.ShapeDtypeStruct((bsz, CONV_HALO, D_CONV), jnp.float32)],
        scratch_shapes=[pltpu.VMEM((CONV_HALO + tt + SUBLANES, D_CONV), jnp.float32),
                        pltpu.VMEM((tt + SUBLANES, D_CONV), jnp.float32)],
        compiler_params=pltpu.CompilerParams(dimension_semantics=("arbitrary", "arbitrary"),
                                             vmem_limit_bytes=VMEM_LIMIT),
        name="conv_module",
    )(x, buf0, w_pw1.astype(bf16), b_pw1.reshape(1, -1), jnp.pad(w_dw, ((0, CONV_HALO - CONV_W), (0, 0))),
      b_dw.reshape(1, -1), ln_g.reshape(1, -1), ln_b.reshape(1, -1), w_pw2.astype(bf16), b_pw2.reshape(1, -1))
    return out[:, :t_], tail[:, CONV_LEAD:]


PACK_W = 256
SC_WINDOW = 128
SC_TILES = 32
MOE_ALIGN = SC_WINDOW * SC_TILES // (2 * TOP_K)


def _pack_rows(y):
    out = []
    for h in range(2):
        lo = lax.bitcast_convert_type(y[:, 2 * h * PACK_W:(2 * h + 1) * PACK_W].astype(jnp.bfloat16)
                                      .astype(jnp.float32), jnp.uint32)
        hi = lax.bitcast_convert_type(y[:, (2 * h + 1) * PACK_W:(2 * h + 2) * PACK_W].astype(jnp.bfloat16)
                                      .astype(jnp.float32), jnp.uint32)
        out.append(lax.bitcast_convert_type((lo >> 16) | hi, jnp.int32))
    return out


def _unpack_words(w):
    u = lax.bitcast_convert_type(w, jnp.uint32)
    lo = lax.bitcast_convert_type(u << 16, jnp.float32)
    hi = lax.bitcast_convert_type(u & jnp.uint32(0xFFFF0000), jnp.float32)
    return lo, hi


def _gather_rows(src, idx):
    n = idx.shape[0]
    if n % (SC_WINDOW * SC_TILES) != 0:
        return jnp.take(src, idx, axis=0)
    mesh = plsc.VectorSubcoreMesh(core_axis_name="core", subcore_axis_name="subcore")

    @pl.kernel(out_type=jax.ShapeDtypeStruct((n, src.shape[1]), src.dtype), mesh=mesh)
    def gather_kernel(src_hbm, idx_hbm, out_hbm):
        def step(idx_vmem, out_vmem):
            pltpu.sync_copy(src_hbm.at[idx_vmem.at[0]], out_vmem)

        pltpu.emit_pipeline(
            step, grid=(n // SC_WINDOW,),
            in_specs=[pl.BlockSpec((1, SC_WINDOW), index_map=lambda i: (0, i))],
            out_specs=[pl.BlockSpec((SC_WINDOW, src.shape[1]), index_map=lambda i: (i, 0))],
            core_axis_name=("core", "subcore"),
            dimension_semantics=(pltpu.PARALLEL,),
        )(idx_hbm, out_hbm)

    return gather_kernel(src, idx.reshape(1, n))


def _scatter_rows(src, idx, n_out):
    n = idx.shape[0]
    m = src.shape[0] // 2
    reps = n // (2 * m)
    if n % (SC_WINDOW * SC_TILES) != 0 or m % SC_WINDOW != 0:
        rows = jnp.arange(n, dtype=jnp.int32)
        src_row = (rows // (reps * m)) * m + rows % m
        return jnp.zeros((n_out, src.shape[1]), src.dtype).at[idx].set(jnp.take(src, src_row, axis=0))
    tiles = m // SC_WINDOW
    mesh = plsc.VectorSubcoreMesh(core_axis_name="core", subcore_axis_name="subcore")

    @pl.kernel(out_type=jax.ShapeDtypeStruct((n_out, src.shape[1]), src.dtype), mesh=mesh, scratch_types=[])
    def scatter_kernel(src_hbm, idx_hbm, out_hbm):
        def step(src_vmem, idx_vmem):
            pltpu.sync_copy(src_vmem, out_hbm.at[idx_vmem.at[0]])

        pltpu.emit_pipeline(
            step, grid=(n // SC_WINDOW,),
            in_specs=[pl.BlockSpec((SC_WINDOW, src.shape[1]),
                                   index_map=lambda i: ((i // (reps * tiles)) * tiles + i % tiles, 0)),
                      pl.BlockSpec((1, SC_WINDOW), index_map=lambda i: (0, i))],
            out_specs=[],
            core_axis_name=("core", "subcore"),
            dimension_semantics=(pltpu.PARALLEL,),
        )(src_hbm, idx_hbm)

    return scatter_kernel(src, idx.reshape(1, n))


PER_GROUP = N_EXPERTS // N_GROUPS
PICKED = -3e38


def _ln_rows(v, g, b):
    mu = jnp.mean(v, axis=-1, keepdims=True)
    c = v - mu
    var = jnp.mean(c * c, axis=-1, keepdims=True)
    return c * lax.rsqrt(var + LN_EPS) * g + b


def _first_max(v, ids, axes, sentinel):
    best = v
    for a in axes:
        best = jnp.max(best, axis=a, keepdims=True)
    first = jnp.where(v == best, ids, sentinel)
    for a in axes:
        first = jnp.min(first, axis=a, keepdims=True)
    return best, first


def _sum_axes(v, axes):
    for a in axes:
        v = jnp.sum(v, axis=a, keepdims=True)
    return v


def _moe_pre_body(x_ref, mix_ref, g_ref, b_ref, wr_ref, br_ref, wgu_ref, wdn_ref,
                  x1_ref, xp_ref, sh_ref, eidx_ref, gate_ref, rank_ref, cnt_ref, run_ref):
    f32, bf16 = jnp.float32, jnp.bfloat16
    tm = x_ref.shape[0]

    @pl.when(pl.program_id(0) == 0)
    def _():
        run_ref[...] = jnp.zeros(run_ref.shape, f32)

    x1 = _ln_rows(ALPHA * x_ref[...] + mix_ref[...], g_ref[...], b_ref[...])
    x1_ref[...] = x1
    x1b = x1.astype(bf16)
    xp_ref[0], xp_ref[1] = _pack_rows(x1)

    h = _dot(x1b, wgu_ref[...])
    d_sh = h.shape[1] // 2
    act = (jax.nn.silu(h[:, :d_sh]) * h[:, d_sh:]).astype(bf16)
    sh_ref[...] = _dot(act, wdn_ref[...])

    s = jax.nn.sigmoid(_dot_nt(wr_ref[...], x1b)).reshape(N_GROUPS, PER_GROUP, tm)
    sb = s + br_ref[...].reshape(N_GROUPS, PER_GROUP, 1)
    shape3 = (N_GROUPS, PER_GROUP, tm)
    pid = lax.broadcasted_iota(jnp.int32, shape3, 1)
    gid = lax.broadcasted_iota(jnp.int32, (N_GROUPS, 1, tm), 0)
    eid = lax.broadcasted_iota(jnp.int32, shape3, 0) * PER_GROUP + pid
    top1, i1 = _first_max(sb, pid, (1,), PER_GROUP)
    top2 = jnp.max(jnp.where(pid == i1, PICKED, sb), axis=1, keepdims=True)
    gscore = top1 + top2
    gsel = jnp.zeros((N_GROUPS, 1, tm), f32)
    for _ in range(TOPK_GROUPS):
        _, first = _first_max(gscore, gid, (0,), N_GROUPS)
        hit = gid == first
        gsel = jnp.where(hit, 1.0, gsel)
        gscore = jnp.where(hit, PICKED, gscore)
    cand = jnp.where(gsel > 0.0, sb, -1e30)
    firsts, gates = [], []
    picked = jnp.zeros(shape3, f32)
    for _ in range(TOP_K):
        _, first = _first_max(cand, eid, (0, 1), N_EXPERTS)
        hit = eid == first
        firsts.append(first)
        gates.append(_sum_axes(jnp.where(hit, s, 0.0), (0, 1)))
        picked = jnp.where(hit, 1.0, picked)
        cand = jnp.where(hit, PICKED, cand)
    gsum = gates[0]
    for gk in gates[1:]:
        gsum = gsum + gk
    earlier = (lax.broadcasted_iota(jnp.int32, (tm, tm), 0) < lax.broadcasted_iota(jnp.int32, (tm, tm), 1))
    picked2 = picked.reshape(N_EXPERTS, tm)
    rank = run_ref[...] + _dot(picked2.astype(bf16), jnp.where(earlier, 1.0, 0.0).astype(bf16))
    run_new = run_ref[...] + jnp.sum(picked2, axis=1, keepdims=True)
    run_ref[...] = run_new
    cnt_ref[...] = jnp.broadcast_to(run_new, cnt_ref.shape)
    rank3 = rank.reshape(shape3)
    for k in range(TOP_K):
        hit = eid == firsts[k]
        eidx_ref[k:k + 1, :] = firsts[k].reshape(1, tm)
        gate_ref[k:k + 1, :] = (gates[k] / gsum * ROUTE_SCALE).reshape(1, tm)
        rank_ref[k:k + 1, :] = _sum_axes(jnp.where(hit, rank3, 0.0), (0, 1)).reshape(1, tm).astype(jnp.int32)


def _moe_pre(x, mix, g, b, w_router, b_router, w_sh_gu, w_sh_down):
    m, d = x.shape
    bf16 = jnp.bfloat16
    tm = min(m, 512)
    row = lambda i: (i, 0)
    col = lambda i: (0, i)
    fixed = lambda i: (0, 0)
    d_sh2 = w_sh_gu.shape[1]
    return pl.pallas_call(
        _moe_pre_body,
        grid=(m // tm,),
        in_specs=[pl.BlockSpec((tm, d), row), pl.BlockSpec((tm, d), row),
                  pl.BlockSpec((1, d), fixed), pl.BlockSpec((1, d), fixed),
                  pl.BlockSpec((N_EXPERTS, d), fixed), pl.BlockSpec((N_EXPERTS, 1), fixed),
                  pl.BlockSpec((d, d_sh2), fixed), pl.BlockSpec((d_sh2 // 2, d), fixed)],
        out_specs=[pl.BlockSpec((tm, d), row), pl.BlockSpec((2, tm, PACK_W), lambda i: (0, i, 0)),
                   pl.BlockSpec((tm, d), row),
                   pl.BlockSpec((TOP_K, tm), col), pl.BlockSpec((TOP_K, tm), col), pl.BlockSpec((TOP_K, tm), col),
                   pl.BlockSpec((N_EXPERTS, LANE), fixed)],
        out_shape=[jax.ShapeDtypeStruct((m, d), jnp.float32), jax.ShapeDtypeStruct((2, m, PACK_W), jnp.int32),
                   jax.ShapeDtypeStruct((m, d), jnp.float32),
                   jax.ShapeDtypeStruct((TOP_K, m), jnp.int32), jax.ShapeDtypeStruct((TOP_K, m), jnp.float32),
                   jax.ShapeDtypeStruct((TOP_K, m), jnp.int32),
                   jax.ShapeDtypeStruct((N_EXPERTS, LANE), jnp.float32)],
        scratch_shapes=[pltpu.VMEM((N_EXPERTS, 1), jnp.float32)],
        compiler_params=pltpu.CompilerParams(dimension_semantics=("arbitrary",),
                                             vmem_limit_bytes=VMEM_LIMIT),
        name="moe_pre",
    )(x, mix, g.reshape(1, d), b.reshape(1, d), w_router.T.astype(bf16), b_router.reshape(N_EXPERTS, 1),
      w_sh_gu.astype(bf16), w_sh_down.astype(bf16))


def _moe_expert_body(exp_ref, first_ref, rows_ref, xs_ref, wgu_ref, wdn_ref, y_ref, wgu_bf, wdn_bf):
    i = pl.program_id(0)
    bf16 = jnp.bfloat16

    @pl.when(first_ref[i] == 1)
    def _():
        wgu_bf[...] = wgu_ref[0, 0].astype(bf16)
        wdn_bf[...] = wdn_ref[0, 0].astype(bf16)

    @pl.when(rows_ref[i] > 0)
    def _():
        live = lax.broadcasted_iota(jnp.int32, (xs_ref.shape[1], 1), 0) < rows_ref[i]
        h = None
        for hw in range(2):
            for q, xq in enumerate(_unpack_words(xs_ref[hw])):
                r0 = (2 * hw + q) * PACK_W
                part = _dot(jnp.where(live, xq, 0.0).astype(bf16), wgu_bf[r0:r0 + PACK_W, :])
                h = part if h is None else h + part
        d_e = h.shape[1] // 2
        act = (jax.nn.silu(h[:, :d_e]) * h[:, d_e:]).astype(bf16)
        y_ref[0], y_ref[1] = _pack_rows(_dot(act, wdn_bf[...]))

    @pl.when(rows_ref[i] == 0)
    def _():
        y_ref[...] = jnp.zeros(y_ref.shape, y_ref.dtype)


def _moe_experts(xs, blk_exp, blk_first, blk_rows, w_exp_gu, w_exp_down, layer, bm):
    n_slots = xs.shape[1]
    d = w_exp_gu.shape[2]
    n_blk = n_slots // bm
    d_e2 = w_exp_gu.shape[3]
    words = lambda i, e, f, a: (0, i, 0)
    grid_spec = pltpu.PrefetchScalarGridSpec(
        num_scalar_prefetch=3,
        grid=(n_blk,),
        in_specs=[pl.BlockSpec((2, bm, PACK_W), words),
                  pl.BlockSpec((1, 1, d, d_e2), lambda i, e, f, a: (layer, e[i], 0, 0)),
                  pl.BlockSpec((1, 1, d_e2 // 2, d), lambda i, e, f, a: (layer, e[i], 0, 0))],
        out_specs=pl.BlockSpec((2, bm, PACK_W), words),
        scratch_shapes=[pltpu.VMEM((d, d_e2), jnp.bfloat16), pltpu.VMEM((d_e2 // 2, d), jnp.bfloat16)])
    return pl.pallas_call(
        _moe_expert_body,
        grid_spec=grid_spec,
        out_shape=jax.ShapeDtypeStruct((2, n_slots, PACK_W), jnp.int32),
        compiler_params=pltpu.CompilerParams(dimension_semantics=("arbitrary",),
                                             vmem_limit_bytes=VMEM_LIMIT),
        name="moe_experts",
    )(blk_exp, blk_first, blk_rows, xs, w_exp_gu, w_exp_down)


def _combine_ln_body(x_ref, yg_ref, gt_ref, sh_ref, g_ref, b_ref, o_ref):
    gt = gt_ref[...]
    parts = []
    for hw in range(2):
        lo_acc = hi_acc = None
        for k in range(TOP_K):
            lo, hi = _unpack_words(yg_ref[hw, k])
            gk = gt[:, k:k + 1]
            lo_acc = lo * gk if lo_acc is None else lo_acc + lo * gk
            hi_acc = hi * gk if hi_acc is None else hi_acc + hi * gk
        parts += [lo_acc, hi_acc]
    routed = jnp.concatenate(parts, axis=1)
    o_ref[...] = _ln_rows(ALPHA * x_ref[...] + (routed + sh_ref[...]), g_ref[...], b_ref[...])


def _combine_ln(x, yg, gate_t, shared, g, b):
    m, d = x.shape
    tm = min(m, 256)
    row = lambda i: (i, 0)
    fixed = lambda i: (0, 0)
    return pl.pallas_call(
        _combine_ln_body,
        grid=(m // tm,),
        in_specs=[pl.BlockSpec((tm, d), row), pl.BlockSpec((2, TOP_K, tm, PACK_W), lambda i: (0, 0, i, 0)),
                  pl.BlockSpec((tm, TOP_K), row), pl.BlockSpec((tm, d), row),
                  pl.BlockSpec((1, d), fixed), pl.BlockSpec((1, d), fixed)],
        out_specs=pl.BlockSpec((tm, d), row),
        out_shape=jax.ShapeDtypeStruct((m, d), jnp.float32),
        compiler_params=pltpu.CompilerParams(dimension_semantics=("arbitrary",)),
        name="combine_ln",
    )(x, yg, gate_t, shared, g.reshape(1, d), b.reshape(1, d))


def _moe_layer(streams, ln1_g, ln1_b, ln2_g, ln2_b, w_router, b_router, w_exp_gu, w_exp_down, layer,
               w_sh_gu, w_sh_down):
    pre = [_moe_pre(x, mix, ln1_g, ln1_b, w_router, b_router, w_sh_gu, w_sh_down) for x, mix in streams]
    m_all = sum(x.shape[0] for x, _ in streams)
    bm = 512 if m_all * TOP_K >= 512 * N_EXPERTS else MOE_BLK
    m_pad = -(-m_all // MOE_ALIGN) * MOE_ALIGN
    spare = -(-(TOP_K * (m_pad - m_all)) // bm)
    n_blk = (m_all * TOP_K) // bm + N_EXPERTS + spare
    n_slots = n_blk * bm
    counts_of = [p[6][:, 0].astype(jnp.int32) for p in pre]
    counts = sum(counts_of)
    padded = (counts + bm - 1) // bm * bm
    pad_end = jnp.cumsum(padded)
    pad_start = pad_end - padded
    dests, before = [], jnp.zeros_like(counts)
    for p, cnt in zip(pre, counts_of):
        eidx, rank8 = p[3], p[5]
        start_of = jnp.sum(jnp.where(eidx[:, :, None] == jnp.arange(N_EXPERTS), pad_start + before, 0), axis=-1)
        dests.append(start_of + rank8)
        before = before + cnt
    blk_start = jnp.arange(n_blk, dtype=jnp.int32) * bm
    blk_exp = jnp.minimum(jnp.sum(pad_end[None, :] <= blk_start[:, None], axis=1), N_EXPERTS - 1).astype(jnp.int32)
    blk_rows = jnp.clip(counts[blk_exp] - (blk_start - pad_start[blk_exp]), 0, bm).astype(jnp.int32)
    blk_first = jnp.concatenate([jnp.ones((1,), jnp.int32), (blk_exp[1:] != blk_exp[:-1]).astype(jnp.int32)])
    spare_slots = (n_slots - spare * bm) + jnp.arange(TOP_K * (m_pad - m_all), dtype=jnp.int32)
    dest_all = jnp.concatenate(dests + [spare_slots.reshape(TOP_K, m_pad - m_all)], axis=1).reshape(-1)
    fill = jnp.zeros((2, m_pad - m_all, PACK_W), jnp.int32)
    words = jnp.concatenate([p[1] for p in pre] + [fill], axis=1).reshape(2 * m_pad, PACK_W)
    xs = _scatter_rows(words, jnp.concatenate([dest_all, dest_all + n_slots]), 2 * n_slots)
    y = _moe_experts(xs.reshape(2, n_slots, PACK_W), blk_exp, blk_first, blk_rows, w_exp_gu, w_exp_down, layer, bm)
    y = y.reshape(2 * n_slots, PACK_W)
    outs = []
    for p, d_i, (x, _) in zip(pre, dests, streams):
        d_i = d_i.reshape(-1)
        yg = _gather_rows(y, jnp.concatenate([d_i, d_i + n_slots])).reshape(2, TOP_K, x.shape[0], PACK_W)
        outs.append(_combine_ln(p[0], yg, p[4].T, p[2], ln2_g, ln2_b))
    return outs


def _trunks(x_p, x_s, pos_p, pos_s, gla_state, nsa_cache, page_table, win_buf, conv_buf,
            w_in_ab, w_gla_gate, b_gla_gate, gla_norm_g, w_cmp_pool, w_out_ab,
            w_pw1, b_pw1, w_dw, b_dw, conv_ln_g, conv_ln_b, w_pw2, b_pw2,
            ln_g, ln_b, w_router, b_router, w_exp_gu, w_exp_down, w_sh_gu, w_sh_down):
    xs = [x_p, x_s]
    states = [(None, None, None, None), (gla_state, nsa_cache, win_buf, conv_buf)]
    poss = [pos_p, pos_s]
    new = [dict(gla=[], rows=[], win=[], conv=[]) for _ in xs]
    for layer in range(DEPTH):
        i = layer // 2
        mixes = []
        for x, pos, (g_st, cache, wbuf, cbuf), out in zip(xs, poss, states, new):
            if layer % 2 == 0:
                mix, s_a, rows, win = _ab_mixer(
                    x, pos, w_in_ab[i], w_gla_gate[i], b_gla_gate[i], gla_norm_g[i], w_cmp_pool[i], w_out_ab[i],
                    None if g_st is None else g_st[i], None if cache is None else cache[i], page_table,
                    None if wbuf is None else wbuf[i])
                out["gla"].append(s_a)
                out["rows"].append(rows)
                out["win"].append(win)
            else:
                mix, cb = _conv_module(x, None if cbuf is None else cbuf[i], w_pw1[i], b_pw1[i],
                                       w_dw[i], b_dw[i], conv_ln_g[i], conv_ln_b[i], w_pw2[i], b_pw2[i])
                out["conv"].append(cb)
            mixes.append(mix)
        d = xs[0].shape[-1]
        ys = _moe_layer([(x.reshape(-1, d), mix.reshape(-1, d)) for x, mix in zip(xs, mixes)],
                        ln_g[layer, 0], ln_b[layer, 0], ln_g[layer, 1], ln_b[layer, 1],
                        w_router[layer], b_router[layer], w_exp_gu, w_exp_down, layer,
                        w_sh_gu[layer], w_sh_down[layer])
        xs = [y.reshape(x.shape) for y, x in zip(ys, xs)]
    return [(x, jnp.stack(o["gla"]), jnp.stack(o["rows"]), jnp.stack(o["win"]), jnp.stack(o["conv"]))
            for x, o in zip(xs, new)]


def kernel(x_prompt, x_sample, state_gla, cache_nsa_kv, state_nsa_win, state_conv, page_table,
           w_in_ab, w_gla_gate, b_gla_gate, gla_norm_g, w_cmp_pool, w_out_ab,
           w_pw1, b_pw1, w_dw, b_dw, conv_ln_g, conv_ln_b, w_pw2, b_pw2,
           ln_g, ln_b, w_router, b_router, w_exp_gu, w_exp_down, w_sh_gu, w_sh_down):
    weights = (w_in_ab, w_gla_gate, b_gla_gate, gla_norm_g, w_cmp_pool, w_out_ab,
               w_pw1, b_pw1, w_dw, b_dw, conv_ln_g, conv_ln_b, w_pw2, b_pw2,
               ln_g, ln_b, w_router, b_router, w_exp_gu, w_exp_down, w_sh_gu, w_sh_down)
    past_len = page_table.shape[1] * PAGE_SIZE
    pos_p = jnp.arange(x_prompt.shape[1])
    pos_s = past_len + jnp.arange(x_sample.shape[1])
    (y_prompt, gla_p, rows_p, win_p, conv_p), (y_sample, gla_s, rows_s, win_s, conv_s) = _trunks(
        x_prompt, x_sample, pos_p, pos_s, state_gla, cache_nsa_kv, page_table, state_nsa_win, state_conv, *weights)
    return (y_prompt, y_sample, gla_p, gla_s, rows_p, rows_s, win_p, win_s, conv_p, conv_s)
```

```python
import functools
import math

import jax
import jax.numpy as jnp
import numpy as np
from jax import lax
from jax.experimental import pallas as pl
from jax.experimental.pallas import tpu as pltpu
from jax.experimental.pallas import tpu_sc as plsc

D_MODEL = 1024
DEPTH = 2
PAGE_SIZE = 128

GLA_HEADS = 4
GLA_DV = D_MODEL // 2 // GLA_HEADS
GLA_DK = GLA_DV // 2
GLA_RANK = 16
GLA_TAU = 16.0

NSA_HEADS = 8
NSA_KV_HEADS = 2
NSA_GROUP = NSA_HEADS // NSA_KV_HEADS
HEAD_DIM = D_MODEL // 2 // NSA_HEADS
CMP_BLK = 32
CMP_STRIDE = 16
SEL_BLK = 64
SEL_TOPN = 16
WINDOW = 512
Q_BLK = 128
FORCE_BONUS = 100.0
ROPE_DIM = HEAD_DIM // 4
ROPE_THETA = 500000.0

GLA_SIZES = (GLA_HEADS * GLA_DK, GLA_HEADS * GLA_DK, GLA_HEADS * GLA_DV, GLA_HEADS * GLA_DV, GLA_RANK)
NSA_SIZES = (NSA_HEADS * HEAD_DIM, 6 * NSA_KV_HEADS * HEAD_DIM, 3 * NSA_HEADS)

CONV_W = 31
D_CONV = D_MODEL

N_EXPERTS = 64
N_GROUPS = 8
TOPK_GROUPS = 4
TOP_K = 8
D_EXPERT = 256
ROUTE_SCALE = 2.5
MOE_BLK = 128

ALPHA = (2 * DEPTH) ** 0.25
LN_EPS = 1e-5

LANE = 128
SUBLANES = 8
V7X_VMEM_BYTES = 64 * 1024 * 1024
VMEM_LIMIT = V7X_VMEM_BYTES * 3 // 4


def _dot(a, b):
    return jnp.dot(a, b, preferred_element_type=jnp.float32)


def _dot_nt(a, b):
    return lax.dot_general(a, b, (((1,), (1,)), ((), ())), preferred_element_type=jnp.float32)


def _mm_body(x_ref, w_ref, o_ref):
    o_ref[...] = _dot(x_ref[...].astype(jnp.bfloat16), w_ref[...].astype(jnp.bfloat16))


def _mm(x, w, keep_pad=False):
    m, k = x.shape
    n = w.shape[1]
    n_pad = -(-n // LANE) * LANE
    w = w.astype(jnp.bfloat16)
    if n_pad != n:
        w = jnp.pad(w, ((0, 0), (0, n_pad - n)))
    tm = min(m, 512)
    out = pl.pallas_call(
        _mm_body,
        grid=(m // tm,),
        in_specs=[pl.BlockSpec((tm, k), lambda i: (i, 0)),
                  pl.BlockSpec((k, n_pad), lambda i: (0, 0))],
        out_specs=pl.BlockSpec((tm, n_pad), lambda i: (i, 0)),
        out_shape=jax.ShapeDtypeStruct((m, n_pad), jnp.float32),
        compiler_params=pltpu.CompilerParams(dimension_semantics=("arbitrary",),
                                             vmem_limit_bytes=VMEM_LIMIT),
        name="mm",
    )(x, w)
    return out if keep_pad or n_pad == n else out[:, :n]


def _mm_pair_body(a_ref, b_ref, w_ref, o_ref):
    ka = a_ref.shape[1]
    o_ref[...] = (_dot(a_ref[...].astype(jnp.bfloat16), w_ref[0:ka, :])
                  + _dot(b_ref[...].astype(jnp.bfloat16), w_ref[ka:, :]))


def _mm_pair(a, b, w):
    m, ka = a.shape
    kb = b.shape[1]
    n = w.shape[1]
    tm = min(m, 512)
    return pl.pallas_call(
        _mm_pair_body,
        grid=(m // tm,),
        in_specs=[pl.BlockSpec((tm, ka), lambda i: (i, 0)), pl.BlockSpec((tm, kb), lambda i: (i, 0)),
                  pl.BlockSpec((ka + kb, n), lambda i: (0, 0))],
        out_specs=pl.BlockSpec((tm, n), lambda i: (i, 0)),
        out_shape=jax.ShapeDtypeStruct((m, n), jnp.float32),
        compiler_params=pltpu.CompilerParams(dimension_semantics=("arbitrary",)),
        name="mm_pair",
    )(a, b, w.astype(jnp.bfloat16))


def _partial_rope(x, pos):
    half = ROPE_DIM // 2
    inv_freq = jnp.power(ROPE_THETA, -jnp.arange(half, dtype=jnp.float32) / half)
    ang = pos.astype(jnp.float32)[:, None] * inv_freq
    ang = ang.reshape(ang.shape[0], *([1] * (x.ndim - 3)), half)
    cos, sin = jnp.cos(ang), jnp.sin(ang)
    x1 = x[..., :half]
    x2 = x[..., half:ROPE_DIM]
    rot = jnp.concatenate([x1 * cos - x2 * sin, x2 * cos + x1 * sin], -1)
    return jnp.concatenate([rot, x[..., ROPE_DIM:]], -1)


NSA_ROWS = NSA_GROUP * Q_BLK
SEL_KT = 2048
N_SELB = 128
MASKED = -1e9
WIN_KEYS = WINDOW + Q_BLK
KK_W = 2 * HEAD_DIM + N_SELB


def _nsa_prompt_body(qr_ref, qo_ref, kc_ref, vct_ref, kk_ref, vvt_ref, g_ref, o_ref,
                     imp_ref, m_ref, l_ref, acc_ref, oct_ref, selb_ref):
    f32, bf16 = jnp.float32, jnp.bfloat16
    qb = pl.program_id(2)
    q0 = qb * Q_BLK
    qr_t = qr_ref[0, 0, 0]
    qo_t = qo_ref[0, 0, 0]
    n_cmp = kc_ref.shape[2]

    ratio = SEL_BLK // CMP_STRIDE
    chunk = min(Q_BLK, n_cmp)
    n_chunks = n_cmp // chunk

    def compressed_and_select(n_act):
        nc = n_act * chunk
        nb = nc // ratio
        s_c = _dot(kc_ref[0, 0, 0:nc, :], qr_t)
        n_idx = lax.broadcasted_iota(jnp.int32, (nc, NSA_ROWS), 0)
        qpos_c = q0 + (lax.broadcasted_iota(jnp.int32, (nc, NSA_ROWS), 1) & (Q_BLK - 1))
        cmask = (n_idx * CMP_STRIDE + (CMP_BLK - 1)) <= qpos_c
        s_c = jnp.where(cmask, s_c, MASKED)
        m_c = jnp.max(s_c, axis=0, keepdims=True)
        p_c = jnp.where(cmask, jnp.exp(s_c - m_c), 0.0)
        p_c = p_c / jnp.maximum(jnp.sum(p_c, axis=0, keepdims=True), 1e-30)
        oct_ref[...] = _dot(vct_ref[0, 0, :, 0:nc], p_c.astype(bf16))
        imp = (p_c[:, 0:Q_BLK] + p_c[:, Q_BLK:2 * Q_BLK]) + p_c[:, 2 * Q_BLK:3 * Q_BLK] + p_c[:, 3 * Q_BLK:]
        imp_ref[0:8, :] = jnp.zeros((8, Q_BLK), f32)
        imp_ref[8:8 + nc, :] = imp
        imp_s = imp_ref[pl.ds(7, nb, stride=ratio), :]
        for r in range(ratio):
            imp_s = imp_s + imp_ref[pl.ds(8 + r, nb, stride=ratio), :]
        blk = lax.broadcasted_iota(jnp.int32, (nb, Q_BLK), 0)
        qpos_s = q0 + lax.broadcasted_iota(jnp.int32, (nb, Q_BLK), 1)
        cur = lax.shift_right_logical(qpos_s, int(math.log2(SEL_BLK)))
        valid = blk * SEL_BLK <= qpos_s
        forced = (blk == 0) | (blk == cur) | (blk == cur - 1)
        score = jnp.where(valid, imp_s + jnp.where(forced, FORCE_BONUS, 0.0), -1e30)
        picked = jnp.zeros((nb, Q_BLK), f32)
        for _ in range(SEL_TOPN):
            best = jnp.max(score, axis=0, keepdims=True)
            first = jnp.min(jnp.where(score == best, blk, nb), axis=0, keepdims=True)
            hit = blk == first
            picked = jnp.where(hit, 1.0, picked)
            score = jnp.where(hit, -3e38, score)
        sel = jnp.where(valid, picked, 0.0)
        if nb < N_SELB:
            sel = jnp.concatenate([sel, jnp.zeros((N_SELB - nb, Q_BLK), f32)], axis=0)
        sel = ((sel - 1.0) * (-MASKED)).astype(bf16)
        selb_ref[...] = jnp.concatenate([sel] * NSA_GROUP, axis=1)

    need = jnp.minimum((q0 + Q_BLK - CMP_BLK) // (CMP_STRIDE * chunk) + 1, n_chunks)
    for n_act in range(1, n_chunks + 1):
        pl.when(need == n_act)(functools.partial(compressed_and_select, n_act))
    o_ct = oct_ref[...]
    selb_t = selb_ref[...]

    zeros_q = jnp.zeros((HEAD_DIM, NSA_ROWS), bf16)
    q_sel = jnp.concatenate([qo_t, zeros_q, selb_t], axis=0)
    q_win = jnp.concatenate([zeros_q, qo_t, jnp.zeros((N_SELB, NSA_ROWS), bf16)], axis=0)
    qpos_r = q0 + (lax.broadcasted_iota(jnp.int32, (1, NSA_ROWS), 1) & (Q_BLK - 1))

    def v_tiles(first, count):
        return jnp.concatenate([vvt_ref[0, 0, first + j] for j in range(count)], axis=1)

    m_ref[...] = jnp.full(m_ref.shape, MASKED, f32)
    l_ref[...] = jnp.zeros(l_ref.shape, f32)
    acc_ref[...] = jnp.zeros(acc_ref.shape, f32)

    def sel_tile(k0, kt, causal):
        s = _dot(kk_ref[0, 0, pl.ds(k0, kt), :], q_sel)
        if causal:
            kpos = k0 + lax.broadcasted_iota(jnp.int32, (kt, NSA_ROWS), 0)
            s = jnp.where(kpos <= qpos_r, s, MASKED)
        m_old = m_ref[...]
        m_new = jnp.maximum(m_old, jnp.max(s, axis=0, keepdims=True))
        alpha = jnp.exp(m_old - m_new)
        p = jnp.exp(s - m_new)
        l_ref[...] = alpha * l_ref[...] + jnp.sum(p, axis=0, keepdims=True)
        vt = v_tiles(k0 // Q_BLK, kt // Q_BLK)
        acc_ref[...] = alpha * acc_ref[...] + _dot(vt, p.astype(bf16))
        m_ref[...] = m_new

    n_full = q0 // SEL_KT

    def full_step(t, c):
        sel_tile(pl.multiple_of(t * SEL_KT, SEL_KT), SEL_KT, False)
        return c

    lax.fori_loop(0, n_full, full_step, 0)
    d0 = pl.multiple_of(n_full * SEL_KT, SEL_KT)
    short = q0 + Q_BLK - n_full * SEL_KT <= SEL_KT // 2

    @pl.when(short)
    def _():
        sel_tile(d0, SEL_KT // 2, True)

    @pl.when(jnp.logical_not(short))
    def _():
        sel_tile(d0, SEL_KT, True)
    o_st = acc_ref[0:HEAD_DIM, :] / l_ref[...]

    w0 = pl.multiple_of(jnp.maximum(q0 - WINDOW, 0), Q_BLK)
    s_w = _dot(kk_ref[0, 0, pl.ds(w0, WIN_KEYS), :], q_win)
    kpos_w = w0 + lax.broadcasted_iota(jnp.int32, (WIN_KEYS, NSA_ROWS), 0)
    s_w = jnp.where((kpos_w <= qpos_r) & (kpos_w > qpos_r - WINDOW), s_w, MASKED)
    p_w = jnp.exp(s_w - jnp.max(s_w, axis=0, keepdims=True))
    l_w = jnp.sum(p_w, axis=0, keepdims=True)
    acc_w = _dot(v_tiles(w0 // Q_BLK, WIN_KEYS // Q_BLK), p_w.astype(bf16))
    o_wt = acc_w[HEAD_DIM:2 * HEAD_DIM, :] / l_w

    g = g_ref[0, 0, 0]
    out_t = g[0:1, :] * o_ct + g[1:2, :] * o_st + g[2:3, :] * o_wt
    o_ref[0] = jnp.concatenate([out_t[:, g_ * Q_BLK:(g_ + 1) * Q_BLK] for g_ in range(NSA_GROUP)], axis=0).T


def _nsa_prompt(qr, qo, gt, kc_p, vct, kk, vvt):
    bsz, _, nqb = qr.shape[:3]
    t_ = nqb * Q_BLK
    n_cmp = kc_p.shape[2]
    per_blk = lambda b, h, i: (b, h, i, 0, 0)
    per_head = lambda b, h, i: (b, h, 0, 0)
    return pl.pallas_call(
        _nsa_prompt_body,
        grid=(bsz, NSA_KV_HEADS, nqb),
        in_specs=[pl.BlockSpec((1, 1, 1, HEAD_DIM, NSA_ROWS), per_blk),
                  pl.BlockSpec((1, 1, 1, HEAD_DIM, NSA_ROWS), per_blk),
                  pl.BlockSpec((1, 1, n_cmp, HEAD_DIM), per_head),
                  pl.BlockSpec((1, 1, HEAD_DIM, n_cmp), per_head),
                  pl.BlockSpec((1, 1, t_, KK_W), per_head),
                  pl.BlockSpec((1, 1, nqb, 2 * HEAD_DIM, Q_BLK), lambda b, h, i: (b, h, 0, 0, 0)),
                  pl.BlockSpec((1, 1, 1, 3, NSA_ROWS), per_blk)],
        out_specs=pl.BlockSpec((1, Q_BLK, NSA_GROUP * HEAD_DIM), lambda b, h, i: (b, i, h)),
        out_shape=jax.ShapeDtypeStruct((bsz, t_, NSA_HEADS * HEAD_DIM), jnp.float32),
        scratch_shapes=[pltpu.VMEM((8 + n_cmp, Q_BLK), jnp.float32),
                        pltpu.VMEM((1, NSA_ROWS), jnp.float32),
                        pltpu.VMEM((1, NSA_ROWS), jnp.float32),
                        pltpu.VMEM((2 * HEAD_DIM, NSA_ROWS), jnp.float32),
                        pltpu.VMEM((HEAD_DIM, NSA_ROWS), jnp.float32),
                        pltpu.VMEM((N_SELB, NSA_ROWS), jnp.bfloat16)],
        compiler_params=pltpu.CompilerParams(
            dimension_semantics=("arbitrary", "arbitrary", "arbitrary"),
            vmem_limit_bytes=VMEM_LIMIT),
        name="nsa_prompt",
    )(qr, qo, kc_p, vct, kk, vvt, gt)


GLA_SUB = 16
GLA_UNROLL = 8
GLA_TILE = 256
GLA_QK = GLA_HEADS * GLA_DK
GLA_V = GLA_HEADS * GLA_DV


def _dot_tn(a, b):
    return lax.dot_general(a, b, (((0,), (0,)), ((), ())), preferred_element_type=jnp.float32)


def _gla_body(q_ref, k_ref, v_ref, gr_ref, glr_ref, wg_ref, bg_ref, ng_ref, s0_ref, exp_ref,
              o_ref, sfin_ref, st_ref, b_ref, qd_ref, *, t_valid):
    f32, bf16 = jnp.float32, jnp.bfloat16
    tt = q_ref.shape[1]
    ti = pl.program_id(1)

    @pl.when(ti == 0)
    def _():
        st_ref[...] = s0_ref[0]

    row = lax.broadcasted_iota(jnp.int32, (tt, 1), 0)
    z = _dot(glr_ref[0][:, :GLA_RANK].astype(bf16), wg_ref[...]) + bg_ref[...]
    la = (jnp.minimum(z, 0.0) - jnp.log1p(jnp.exp(-jnp.abs(z)))) * (1.0 / GLA_TAU)
    la = jnp.where(ti * tt + row < t_valid, la, 0.0)
    seg = row & (GLA_SUB - 1)
    b = la
    for s in (1, 2, 4, 8):
        b = b + jnp.where(seg >= s, pltpu.roll(b, s, axis=0), 0.0)
    q = q_ref[0] * (GLA_DK ** -0.5)
    k = k_ref[0]
    v = v_ref[0]
    o = _dot((q * k).astype(bf16), exp_ref[...]) * v
    for d in range(1, GLA_SUB):
        decay = jnp.exp(jnp.minimum(b - pltpu.roll(b, d, axis=0), 0.0))
        w = jnp.where(seg >= d, q * pltpu.roll(k, d, axis=0) * decay, 0.0)
        o = o + _dot(w.astype(bf16), exp_ref[...]) * pltpu.roll(v, d, axis=0)
    o_ref[0] = o
    b_ref[...] = b
    qd_ref[...] = (q * jnp.exp(b)).astype(bf16)

    def block_step(c, carry):
        rows = pl.ds(pl.multiple_of(c * GLA_SUB, GLA_SUB), GLA_SUB)
        qd = qd_ref[rows, :]
        bc = b_ref[rows, :]
        bl = bc[GLA_SUB - 1:GLA_SUB, :]
        kc = (k_ref[0, rows, :] * jnp.exp(bl - bc)).astype(bf16)
        keep = jnp.exp(bl)
        vb = v_ref[0, rows, :].astype(bf16)
        outs = []
        for h in range(GLA_HEADS):
            dk = slice(h * GLA_DK, (h + 1) * GLA_DK)
            dv = slice(h * GLA_DV, (h + 1) * GLA_DV)
            st = st_ref[dv, :]
            outs.append(_dot_nt(qd[:, dk], st.astype(bf16)))
            st_ref[dv, :] = st * keep[:, dk] + _dot_tn(vb[:, dv], kc[:, dk])
        o_ref[0, rows, :] += jnp.concatenate(outs, axis=1)
        return carry

    lax.fori_loop(0, tt // GLA_SUB, block_step, 0, unroll=GLA_UNROLL)
    sfin_ref[0] = st_ref[...]
    gr = gr_ref[0]
    gate = gr * jax.nn.sigmoid(gr)
    for h in range(GLA_HEADS):
        cols = slice(h * GLA_DV, (h + 1) * GLA_DV)
        oh = o_ref[0, :, cols]
        ms = jnp.mean(oh * oh, axis=-1, keepdims=True)
        o_ref[0, :, cols] = oh * lax.rsqrt(ms + LN_EPS) * ng_ref[...] * gate[:, cols]


def _gla(h, w_gla_gate, b_gla_gate, gla_norm_g, gla_state):
    bsz, t_, n_in = h.shape
    tp = -(-t_ // GLA_SUB) * GLA_SUB
    if tp != t_:
        h = jnp.pad(h, ((0, 0), (0, tp - t_), (0, 0)))
    tt = min(tp, GLA_TILE)
    expand =np.repeat(np.repeat(np.eye(GLA_HEADS, dtype=np.float32), GLA_DK, 0), GLA_DV, 1)
    if gla_state is None:
        s0 = jnp.zeros((bsz, GLA_V, GLA_DK), jnp.float32)
    else:
        s0 = gla_state.transpose(0, 1, 3, 2).reshape(bsz, GLA_V, GLA_DK)
    tile = lambda width, blk: pl.BlockSpec((1, tt, width), lambda b, i: (b, i, blk))
    fixed2 = lambda shape: pl.BlockSpec(shape, lambda b, i: (0, 0))
    per_b = pl.BlockSpec((1, GLA_V, GLA_DK), lambda b, i: (b, 0, 0))
    o, s_t = pl.pallas_call(
        functools.partial(_gla_body, t_valid=t_),
        grid=(bsz, tp // tt),
        in_specs=[tile(GLA_QK, 0), tile(GLA_QK, 1), tile(GLA_V, 1), tile(GLA_V, 2),
                  tile(LANE, (2 * GLA_QK + 2 * GLA_V + NSA_SIZES[0] + NSA_SIZES[1]) // LANE),
                  fixed2((GLA_RANK, GLA_QK)), fixed2((1, GLA_QK)), fixed2((1, GLA_DV)), per_b,
                  fixed2((GLA_QK, GLA_V))],
        out_specs=[pl.BlockSpec((1, tt, GLA_V), lambda b, i: (b, i, 0)), per_b],
        out_shape=[jax.ShapeDtypeStruct((bsz, tp, GLA_V), jnp.float32),
                   jax.ShapeDtypeStruct((bsz, GLA_V, GLA_DK), jnp.float32)],
        scratch_shapes=[pltpu.VMEM((GLA_V, GLA_DK), jnp.float32), pltpu.VMEM((tt, GLA_QK), jnp.float32),
                        pltpu.VMEM((tt, GLA_QK), jnp.bfloat16)],
        compiler_params=pltpu.CompilerParams(dimension_semantics=("arbitrary", "arbitrary"),
                                             vmem_limit_bytes=VMEM_LIMIT),
        name="gla",
    )(h, h, h, h, h, w_gla_gate.astype(jnp.bfloat16), b_gla_gate.reshape(1, GLA_QK),
      gla_norm_g.reshape(1, GLA_DV), s0, jnp.asarray(expand, jnp.bfloat16))
    return o[:, :t_], s_t.reshape(bsz, GLA_HEADS, GLA_DV, GLA_DK).transpose(0, 1, 3, 2)


COL_NQ = 2 * GLA_QK + 2 * GLA_V
COL_NKV = COL_NQ + NSA_SIZES[0]
COL_TAIL = COL_NKV + NSA_SIZES[1]
TAIL_GATE = GLA_RANK
_ORIG = np.cumsum((0,) + GLA_SIZES + NSA_SIZES)
IN_AB_PERM = np.concatenate([np.arange(_ORIG[0], _ORIG[4]), np.arange(_ORIG[5], _ORIG[7]),
                             np.arange(_ORIG[4], _ORIG[5]), np.arange(_ORIG[7], _ORIG[8])])
SUBS = Q_BLK // CMP_STRIDE


def _nsa_prep_body(nq_ref, kv0_ref, kv1_ref, kv2_ref, tail_ref, rc_ref, ru_ref, rd_ref, pool_ref,
                   rows_ref, win_ref, kk_ref, vvt_ref, qr_ref, qo_ref, g_ref, pooled_ref):
    bf16 = jnp.bfloat16
    q0 = pl.program_id(1) * Q_BLK
    kv_w = NSA_KV_HEADS * HEAD_DIM

    def rope(x):
        reps = x.shape[1] // LANE
        wide = lambda r: jnp.concatenate([r[...]] * reps, axis=1) if reps > 1 else r[...]
        half = ROPE_DIM // 2
        return (x * wide(rc_ref) + pltpu.roll(x, half, axis=1) * wide(ru_ref)
                + pltpu.roll(x, x.shape[1] - half, axis=1) * wide(rd_ref))

    kv0, kv1, kv2 = kv0_ref[0], kv1_ref[0], kv2_ref[0]
    k_sel, v_sel = rope(kv1[:, :kv_w]), kv1[:, kv_w:]
    k_win, v_win = rope(kv2[:, :kv_w]), kv2[:, kv_w:]
    rows_ref[0] = jnp.concatenate([kv0, k_sel, v_sel], axis=1)
    win_ref[0] = jnp.concatenate([k_win, v_win], axis=1)
    blk_id = lax.shift_right_logical(q0 + lax.broadcasted_iota(jnp.int32, (Q_BLK, N_SELB), 0),
                                     int(math.log2(SEL_BLK)))
    onehot = jnp.where(lax.broadcasted_iota(jnp.int32, (Q_BLK, N_SELB), 1) == blk_id, 1.0, 0.0).astype(bf16)
    q = nq_ref[0] * (HEAD_DIM ** -0.5)
    q_rot = rope(q)
    gates_t = jax.nn.sigmoid(tail_ref[0]).T
    for h in range(NSA_KV_HEADS):
        hs = slice(h * HEAD_DIM, (h + 1) * HEAD_DIM)
        kk_ref[0, h] = jnp.concatenate([k_sel[:, hs].astype(bf16), k_win[:, hs].astype(bf16), onehot], axis=1)
        vvt_ref[0, h, 0] = jnp.concatenate([v_sel[:, hs], v_win[:, hs]], axis=1).T.astype(bf16)
        gw = NSA_GROUP * HEAD_DIM
        for src, dst in ((q, qr_ref), (q_rot, qo_ref)):
            t = src[:, h * gw:(h + 1) * gw].T
            dst[0, h, 0] = jnp.concatenate([t[g * HEAD_DIM:(g + 1) * HEAD_DIM] for g in range(NSA_GROUP)],
                                           axis=1).astype(bf16)
        base = TAIL_GATE + h * NSA_GROUP * 3
        g_ref[0, h, 0] = jnp.concatenate(
            [jnp.concatenate([gates_t[base + 3 * g + c:base + 3 * g + c + 1] for g in range(NSA_GROUP)], axis=1)
             for c in range(3)], axis=0)
    kc_in, vc_in = kv0[:, :kv_w].astype(bf16), kv0[:, kv_w:].astype(bf16)
    pooled_ref[0] = jnp.concatenate([_dot(pool_ref[0], kc_in), _dot(pool_ref[1], kc_in),
                                     _dot(pool_ref[2], vc_in), _dot(pool_ref[3], vc_in)], axis=1)


def _nsa_prep(h, pos, w_cmp_pool):
    bsz, t_, _ = h.shape
    nqb = t_ // Q_BLK
    bf16 = jnp.bfloat16
    half = ROPE_DIM // 2
    inv_freq = jnp.power(ROPE_THETA, -jnp.arange(half, dtype=jnp.float32) / half)
    ang = pos.astype(jnp.float32)[:, None] * inv_freq
    cos, sin = jnp.cos(ang), jnp.sin(ang)
    rest = HEAD_DIM - ROPE_DIM
    z8, zr = jnp.zeros((t_, half), jnp.float32), jnp.zeros((t_, rest), jnp.float32)
    two = lambda a: jnp.concatenate([a, a], axis=1)
    rc = two(jnp.concatenate([cos, cos, jnp.ones((t_, rest), jnp.float32)], axis=1))
    ru = two(jnp.concatenate([z8, sin, zr], axis=1))
    rd = two(jnp.concatenate([-sin, z8, zr], axis=1))
    pool = _pool_matrices(w_cmp_pool)
    kv_w = NSA_KV_HEADS * HEAD_DIM
    col = lambda width, off: pl.BlockSpec((1, Q_BLK, width), lambda b, i: (b, i, off // width))
    rows_t = pl.BlockSpec((Q_BLK, LANE), lambda b, i: (i, 0))
    head4 = lambda r, c: pl.BlockSpec((1, NSA_KV_HEADS, 1, r, c), lambda b, i: (b, 0, i, 0, 0))
    return pl.pallas_call(
        _nsa_prep_body,
        grid=(bsz, nqb),
        in_specs=[col(NSA_SIZES[0], COL_NQ), col(2 * kv_w, COL_NKV), col(2 * kv_w, COL_NKV + 2 * kv_w),
                  col(2 * kv_w, COL_NKV + 4 * kv_w), col(LANE, COL_TAIL), rows_t, rows_t, rows_t,
                  pl.BlockSpec((4, SUBS, Q_BLK), lambda b, i: (0, 0, 0))],
        out_specs=[pl.BlockSpec((1, Q_BLK, 4 * kv_w), lambda b, i: (b, i, 0)),
                   pl.BlockSpec((1, Q_BLK, 2 * kv_w), lambda b, i: (b, i, 0)),
                   pl.BlockSpec((1, NSA_KV_HEADS, Q_BLK, KK_W), lambda b, i: (b, 0, i, 0)),
                   head4(2 * HEAD_DIM, Q_BLK), head4(HEAD_DIM, NSA_ROWS), head4(HEAD_DIM, NSA_ROWS),
                   head4(3, NSA_ROWS),
                   pl.BlockSpec((1, SUBS, 4 * kv_w), lambda b, i: (b, i, 0))],
        out_shape=[jax.ShapeDtypeStruct((bsz, t_, 4 * kv_w), jnp.float32),
                   jax.ShapeDtypeStruct((bsz, t_, 2 * kv_w), jnp.float32),
                   jax.ShapeDtypeStruct((bsz, NSA_KV_HEADS, t_, KK_W), bf16),
                   jax.ShapeDtypeStruct((bsz, NSA_KV_HEADS, nqb, 2 * HEAD_DIM, Q_BLK), bf16),
                   jax.ShapeDtypeStruct((bsz, NSA_KV_HEADS, nqb, HEAD_DIM, NSA_ROWS), bf16),
                   jax.ShapeDtypeStruct((bsz, NSA_KV_HEADS, nqb, HEAD_DIM, NSA_ROWS), bf16),
                   jax.ShapeDtypeStruct((bsz, NSA_KV_HEADS, nqb, 3, NSA_ROWS), jnp.float32),
                   jax.ShapeDtypeStruct((bsz, t_ // CMP_STRIDE, 4 * kv_w), jnp.float32)],
        compiler_params=pltpu.CompilerParams(dimension_semantics=("arbitrary", "arbitrary")),
        name="nsa_prep",
    )(h, h, h, h, h, rc, ru, rd, pool)


PAGE_GROUP = 32
DEC_KEYS = PAGE_GROUP * PAGE_SIZE
POOL_ROWS = 2048
NEW_PAD = 8
KV_W = NSA_KV_HEADS * HEAD_DIM


def _dec_pool_body(pt_ref, *refs):
    page_refs, pool_ref, out_ref = refs[:PAGE_GROUP], refs[PAGE_GROUP], refs[PAGE_GROUP + 1]
    bf16 = jnp.bfloat16
    pages = [pr[0] for pr in page_refs]
    per = POOL_ROWS // PAGE_SIZE
    cols = []
    for g0 in range(0, PAGE_GROUP, per):
        kc_t = jnp.concatenate([p[:KV_W] for p in pages[g0:g0 + per]], axis=1).astype(bf16)
        vc_t = jnp.concatenate([p[KV_W:] for p in pages[g0:g0 + per]], axis=1).astype(bf16)
        cols.append(jnp.concatenate([_dot(kc_t, pool_ref[0]), _dot(kc_t, pool_ref[1]),
                                     _dot(vc_t, pool_ref[2]), _dot(vc_t, pool_ref[3])], axis=0))
    out_ref[0] = jnp.concatenate(cols, axis=1)


def _page_specs(n_pages, col_blk):
    def spec(i):
        return pl.BlockSpec((1, 2 * KV_W, PAGE_SIZE),
                            lambda b, j, pt: (pt[b * n_pages + j * PAGE_GROUP + i], col_blk, 0))
    return [spec(i) for i in range(PAGE_GROUP)]


def _dec_pool(cache, page_table, pool):
    bsz, n_pages = page_table.shape
    grid_spec = pltpu.PrefetchScalarGridSpec(
        num_scalar_prefetch=1, grid=(bsz, n_pages // PAGE_GROUP),
        in_specs=_page_specs(n_pages, 0) + [pl.BlockSpec(pool.shape, lambda b, j, pt: (0, 0, 0))],
        out_specs=pl.BlockSpec((1, 4 * KV_W, PAGE_GROUP * SUBS), lambda b, j, pt: (b, 0, j)))
    return pl.pallas_call(
        _dec_pool_body, grid_spec=grid_spec,
        out_shape=jax.ShapeDtypeStruct((bsz, 4 * KV_W, n_pages * SUBS), jnp.float32),
        compiler_params=pltpu.CompilerParams(dimension_semantics=("arbitrary", "arbitrary")),
        name="nsa_dec_pool",
    )(page_table.reshape(-1), *([cache] * PAGE_GROUP), pool)


def _dec_select_body(qr_ref, kct_ref, vc_ref, band_ref, oc_ref, selb_ref, *, qpos0, n_q, n_pick, n_blk):
    f32, bf16 = jnp.float32, jnp.bfloat16
    n_cmp = kct_ref.shape[3]
    rows = NSA_GROUP * n_q
    for sq, h in [(a, b) for a in range(qr_ref.shape[0]) for b in range(NSA_KV_HEADS)]:
        s_c = _dot(qr_ref[sq, h], kct_ref[sq, h])
        n_idx = lax.broadcasted_iota(jnp.int32, (rows, n_cmp), 1)
        qpos = qpos0 + (lax.broadcasted_iota(jnp.int32, (rows, n_cmp), 0) % n_q)
        cmask = (n_idx * CMP_STRIDE + (CMP_BLK - 1)) <= qpos
        s_c = jnp.where(cmask, s_c, MASKED)
        p_c = jnp.where(cmask, jnp.exp(s_c - jnp.max(s_c, axis=1, keepdims=True)), 0.0)
        p_c = p_c / jnp.maximum(jnp.sum(p_c, axis=1, keepdims=True), 1e-30)
        oc_ref[sq, h] = _dot(p_c.astype(bf16), vc_ref[sq, h])
        imp = p_c[0:n_q]
        for g in range(1, NSA_GROUP):
            imp = imp + p_c[g * n_q:(g + 1) * n_q]
        imp_s = jnp.zeros((n_q, N_SELB), f32)
        rem = imp
        for _ in range(3):
            part = rem.astype(bf16)
            imp_s = imp_s + _dot(part, band_ref[...])
            rem = rem - part.astype(f32)
        blk = lax.broadcasted_iota(jnp.int32, (n_q, N_SELB), 1)
        qpos_s = qpos0 + lax.broadcasted_iota(jnp.int32, (n_q, N_SELB), 0)
        cur = lax.shift_right_logical(qpos_s, int(math.log2(SEL_BLK)))
        valid = (blk * SEL_BLK <= qpos_s) & (blk < n_blk)
        forced = (blk == 0) | (blk == cur) | (blk == cur - 1)
        score = jnp.where(valid, imp_s + jnp.where(forced, FORCE_BONUS, 0.0), -1e30)
        picked = jnp.zeros((n_q, N_SELB), f32)
        for _ in range(n_pick):
            best = jnp.max(score, axis=1, keepdims=True)
            first = jnp.min(jnp.where(score == best, blk, N_SELB), axis=1, keepdims=True)
            hit = blk == first
            picked = jnp.where(hit, 1.0, picked)
            score = jnp.where(hit, -3e38, score)
        selb_ref[sq, h] = (jnp.where(valid, picked, 0.0) - 1.0) * (-MASKED)


def _dec_select(qr, kct, vc, n_q, qpos0, n_pick, n_blk):
    bsz = qr.shape[0]
    rows = NSA_GROUP * n_q
    n_cmp = kct.shape[3]
    ratio = SEL_BLK // CMP_STRIDE
    c_idx, j_idx = np.arange(n_cmp)[:, None], np.arange(N_SELB)[None, :]
    band = jnp.asarray(((c_idx >= ratio * j_idx - 1) & (c_idx <= ratio * j_idx + ratio - 1)), jnp.bfloat16)
    per_step = next(c for c in (4, 2, 1) if bsz % c == 0)
    per_b = lambda *tail: pl.BlockSpec((per_step, NSA_KV_HEADS) + tail, lambda b: (b, 0, 0, 0))
    return pl.pallas_call(
        functools.partial(_dec_select_body, qpos0=qpos0, n_q=n_q, n_pick=n_pick, n_blk=n_blk),
        grid=(bsz // per_step,),
        in_specs=[per_b(rows, HEAD_DIM), per_b(HEAD_DIM, n_cmp), per_b(n_cmp, HEAD_DIM),
                  pl.BlockSpec((n_cmp, N_SELB), lambda b: (0, 0))],
        out_specs=[per_b(rows, HEAD_DIM), per_b(n_q, N_SELB)],
        out_shape=[jax.ShapeDtypeStruct((bsz, NSA_KV_HEADS, rows, HEAD_DIM), jnp.float32),
                   jax.ShapeDtypeStruct((bsz, NSA_KV_HEADS, n_q, N_SELB), jnp.float32)],
        compiler_params=pltpu.CompilerParams(dimension_semantics=("arbitrary",)),
        name="nsa_dec_select",
    )(qr, kct, vc, band)


def _dec_attend_body(pt_ref, *refs, qpos0, n_q, past):
    page_refs = refs[:PAGE_GROUP]
    (qs_ref, qw_ref, knew_ref, vnew_ref, wbuf_ref, wnew_ref, oc_ref, g_ref,
     o_ref, m_ref, l_ref, acc_ref) = refs[PAGE_GROUP:]
    f32, bf16 = jnp.float32, jnp.bfloat16
    j = pl.program_id(1)
    n_rows = qs_ref.shape[1]

    @pl.when(j == 0)
    def _():
        m_ref[...] = jnp.full(m_ref.shape, MASKED, f32)
        l_ref[...] = jnp.zeros(l_ref.shape, f32)
        acc_ref[...] = jnp.zeros(acc_ref.shape, f32)

    def online(s, weigh):
        m_old = m_ref[...]
        m_new = jnp.maximum(m_old, jnp.max(s, axis=1, keepdims=True))
        alpha = jnp.exp(m_old - m_new)
        p = jnp.exp(s - m_new)
        l_ref[...] = alpha * l_ref[...] + jnp.sum(p, axis=1, keepdims=True)
        acc_ref[...] = alpha * acc_ref[...] + weigh(p.astype(bf16))
        m_ref[...] = m_new

    qs = qs_ref[0]
    pages = [pr[0] for pr in page_refs]
    keys_t = jnp.concatenate([p[:KV_W] for p in pages], axis=1).astype(bf16)
    vals_t = jnp.concatenate([p[KV_W:] for p in pages], axis=1).astype(bf16)
    blk_id = j * (DEC_KEYS // SEL_BLK) + lax.shift_right_logical(
        lax.broadcasted_iota(jnp.int32, (N_SELB, DEC_KEYS), 1), int(math.log2(SEL_BLK)))
    onehot_t = jnp.where(lax.broadcasted_iota(jnp.int32, (N_SELB, DEC_KEYS), 0) == blk_id, 1.0, 0.0).astype(bf16)
    online(_dot(qs, jnp.concatenate([keys_t, onehot_t], axis=0)), lambda p: _dot_nt(p, vals_t))

    @pl.when(j == pl.num_programs(1) - 1)
    def _():
        row_q = qpos0 + (lax.broadcasted_iota(jnp.int32, (n_rows, 1), 0) % n_q)
        qh = qw_ref[0]
        new_pos = past + lax.broadcasted_iota(jnp.int32, (n_rows, NEW_PAD), 1)
        new_ok = (new_pos <= row_q) & (new_pos < past + n_q)
        s_new = jnp.where(new_ok, _dot_nt(qh, knew_ref[0]), MASKED)
        online(s_new, lambda p: _dot(p, vnew_ref[0]))
        o_s = acc_ref[...] / l_ref[...]
        wbuf_t = wbuf_ref[0]
        wnew = wnew_ref[0]
        n_buf = wbuf_t.shape[1]
        s_b = _dot(qh, wbuf_t[:KV_W].astype(bf16))
        pos_b = (past - n_buf) + lax.broadcasted_iota(jnp.int32, (n_rows, n_buf), 1)
        s_b = jnp.where((pos_b > row_q - WINDOW) & (pos_b >= 0), s_b, MASKED)
        s_n = jnp.where(new_ok, _dot_nt(qh, wnew[:, :KV_W].astype(bf16)), MASKED)
        m_w = jnp.maximum(jnp.max(s_b, axis=1, keepdims=True), jnp.max(s_n, axis=1, keepdims=True))
        p_b, p_n = jnp.exp(s_b - m_w), jnp.exp(s_n - m_w)
        l_w = jnp.sum(p_b, axis=1, keepdims=True) + jnp.sum(p_n, axis=1, keepdims=True)
        o_w = (_dot_nt(p_b.astype(bf16), wbuf_t[KV_W:].astype(bf16))
               + _dot(p_n.astype(bf16), wnew[:, KV_W:].astype(bf16))) / l_w
        half = n_rows // NSA_KV_HEADS
        own = lambda a: jnp.concatenate([a[h * half:(h + 1) * half, h * HEAD_DIM:(h + 1) * HEAD_DIM]
                                         for h in range(NSA_KV_HEADS)], axis=0)
        g = g_ref[0]
        o_ref[0] = g[:, 0:1] * oc_ref[0] + g[:, 1:2] * own(o_s) + g[:, 2:3] * own(o_w)


def _dec_attend(cache, page_table, qs, qw, knew, vnew, wbuf, wnew, o_c, gates, n_q, qpos0):
    bsz, n_pages = page_table.shape
    n_rows = qs.shape[1]
    per_b = lambda *tail: pl.BlockSpec((1,) + tail, lambda b, j, pt: (b, 0, 0))
    grid_spec = pltpu.PrefetchScalarGridSpec(
        num_scalar_prefetch=1, grid=(bsz, n_pages // PAGE_GROUP),
        in_specs=_page_specs(n_pages, 1) + [
            per_b(n_rows, KV_W + N_SELB), per_b(n_rows, KV_W), per_b(NEW_PAD, KV_W), per_b(NEW_PAD, KV_W),
            per_b(2 * KV_W, wbuf.shape[2]), per_b(NEW_PAD, 2 * KV_W), per_b(n_rows, HEAD_DIM), per_b(n_rows, 3)],
        out_specs=per_b(n_rows, HEAD_DIM),
        scratch_shapes=[pltpu.VMEM((n_rows, 1), jnp.float32), pltpu.VMEM((n_rows, 1), jnp.float32),
                        pltpu.VMEM((n_rows, KV_W), jnp.float32)])
    return pl.pallas_call(
        functools.partial(_dec_attend_body, qpos0=qpos0, n_q=n_q, past=n_pages * PAGE_SIZE),
        grid_spec=grid_spec,
        out_shape=jax.ShapeDtypeStruct((bsz, n_rows, HEAD_DIM), jnp.float32),
        compiler_params=pltpu.CompilerParams(dimension_semantics=("arbitrary", "arbitrary")),
        name="nsa_dec_attend",
    )(page_table.reshape(-1), *([cache] * PAGE_GROUP), qs, qw, knew, vnew, wbuf, wnew, o_c, gates)


def _pool_matrices(w_cmp_pool, rows=Q_BLK):
    subs = rows // CMP_STRIDE
    sub = np.arange(rows) // CMP_STRIDE == np.arange(subs)[:, None]
    w_rep = jnp.tile(w_cmp_pool.reshape(2, 2, CMP_STRIDE), (1, 1, subs))
    return jnp.where(sub[None, None], w_rep[:, :, None, :], 0.0).reshape(4, subs, rows).astype(jnp.bfloat16)


def _nsa_decode(q_raw, q_rot, gates, rows_full, rows_win, cache, page_table, win_buf, w_cmp_pool, past):
    bsz, n_q = q_raw.shape[:2]
    bf16 = jnp.bfloat16
    n_blk = past // SEL_BLK
    assert past % DEC_KEYS == 0 and n_blk <= N_SELB and n_q <= NEW_PAD
    scale = HEAD_DIM ** -0.5
    cache2 = cache.transpose(0, 2, 3, 4, 1).reshape(cache.shape[0], 4 * KV_W, PAGE_SIZE)
    pooled_t = _dec_pool(cache2, page_table, _pool_matrices(w_cmp_pool, POOL_ROWS).transpose(0, 2, 1))
    pooled_t = pooled_t.reshape(bsz, 4, NSA_KV_HEADS, HEAD_DIM, -1)
    last = ((0, 0), (0, 0), (0, 0), (0, 1))
    kct = jnp.pad(pooled_t[:, 0, ..., :-1] + pooled_t[:, 1, ..., 1:], last)
    vc_p = jnp.pad(pooled_t[:, 2, ..., :-1] + pooled_t[:, 3, ..., 1:], last).transpose(0, 1, 3, 2)
    rows_of = lambda a: a.transpose(0, 2, 3, 1, 4).reshape(bsz, NSA_KV_HEADS, NSA_GROUP * n_q, a.shape[-1])
    qr = rows_of((q_raw * scale).astype(bf16))
    n_pick = min(SEL_TOPN, n_blk + 1) - 1
    o_c, selb = _dec_select(qr, kct.astype(bf16), vc_p.astype(bf16), n_q, past, n_pick, n_blk)
    qo = rows_of((q_rot * scale).astype(bf16))
    zero = jnp.zeros_like(qo[:, 0])
    qw = jnp.concatenate([jnp.concatenate([qo[:, 0], zero], -1), jnp.concatenate([zero, qo[:, 1]], -1)], axis=1)
    bias = jnp.tile(selb, (1, 1, NSA_GROUP, 1)).reshape(bsz, -1, N_SELB).astype(bf16)
    qs = jnp.concatenate([qw, bias], axis=-1)
    pad_new = lambda a: jnp.pad(a.reshape(bsz, n_q, -1), ((0, 0), (0, NEW_PAD - n_q), (0, 0)))
    knew = pad_new(rows_full[:, :, 2]).astype(bf16)
    vnew = pad_new(rows_full[:, :, 3]).astype(bf16)
    wnew = pad_new(rows_win)
    wbuf = win_buf.transpose(0, 2, 3, 4, 1).reshape(bsz, 2 * KV_W, win_buf.shape[1])
    gt = rows_of(gates).reshape(bsz, -1, 3)
    o = _dec_attend(cache2, page_table, qs, qw, knew, vnew, wbuf, wnew,
                    o_c.reshape(bsz, -1, HEAD_DIM), gt, n_q, past)
    o = o.reshape(bsz, NSA_KV_HEADS, NSA_GROUP, n_q, HEAD_DIM).transpose(0, 3, 1, 2, 4)
    return o.reshape(bsz, n_q, NSA_HEADS * HEAD_DIM)


def _ab_mixer(x, pos, w_in, w_gla_gate, b_gla_gate, gla_norm_g, w_cmp_pool, w_out,
              gla_state, nsa_cache, page_table, win_buf):
    bsz, t_, _ = x.shape
    h_in = _mm(x.reshape(bsz * t_, -1), w_in[:, IN_AB_PERM], keep_pad=True).reshape(bsz, t_, -1)
    o_a, s_a = _gla(h_in, w_gla_gate, b_gla_gate, gla_norm_g, gla_state)
    kv_w = NSA_KV_HEADS * HEAD_DIM
    if nsa_cache is None:
        rows2, win2, kk, vvt, qr, qo, gt, pooled = _nsa_prep(h_in, pos, w_cmp_pool)
        pooled = pooled.reshape(bsz, t_ // CMP_STRIDE, 4, NSA_KV_HEADS, HEAD_DIM)
        kc = pooled[:, :-1, 0] + pooled[:, 1:, 1]
        vc = pooled[:, :-1, 2] + pooled[:, 1:, 3]
        kc_p = jnp.pad(kc, ((0, 0), (0, 1), (0, 0), (0, 0))).transpose(0, 2, 1, 3).astype(jnp.bfloat16)
        vct = jnp.pad(vc, ((0, 0), (0, 1), (0, 0), (0, 0))).transpose(0, 2, 3, 1).astype(jnp.bfloat16)
        o_b = _nsa_prompt(qr, qo, gt, kc_p, vct, kk, vvt)
        rows_full = rows2.reshape(bsz, t_, 4, NSA_KV_HEADS, HEAD_DIM)
        new_win = win2[:, -min(WINDOW, t_):].reshape(bsz, -1, 2, NSA_KV_HEADS, HEAD_DIM)
    else:
        nq = h_in[..., COL_NQ:COL_NKV]
        nkv = h_in[..., COL_NKV:COL_TAIL]
        ngate = h_in[..., COL_TAIL + TAIL_GATE:COL_TAIL + TAIL_GATE + NSA_SIZES[2]]
        q_raw = nq.reshape(bsz, t_, NSA_KV_HEADS, NSA_GROUP, HEAD_DIM)
        q_rot = _partial_rope(q_raw, pos)
        kv = nkv.reshape(bsz, t_, 6, NSA_KV_HEADS, HEAD_DIM)
        k_sel = _partial_rope(kv[:, :, 2], pos)
        k_win = _partial_rope(kv[:, :, 4], pos)
        rows_full = jnp.stack([kv[:, :, 0], kv[:, :, 1], k_sel, kv[:, :, 3]], axis=2)
        rows_win = jnp.stack([k_win, kv[:, :, 5]], axis=2)
        gates = jax.nn.sigmoid(ngate).reshape(bsz, t_, NSA_KV_HEADS, NSA_GROUP, 3)
        past_len = page_table.shape[1] * PAGE_SIZE
        o_b = _nsa_decode(q_raw, q_rot, gates, rows_full, rows_win, nsa_cache, page_table, win_buf,
                          w_cmp_pool, past_len)
        w_buf = win_buf.shape[1]
        kw = jnp.concatenate([win_buf, rows_win], axis=1)
        new_win = kw[:, -w_buf:]
    y = _mm_pair(o_a.reshape(bsz * t_, -1), o_b.reshape(bsz * t_, -1), w_out).reshape(bsz, t_, -1)
    return y, s_a, rows_full, new_win


CONV_HALO = 32
CONV_LEAD = CONV_HALO - (CONV_W - 1)


def _conv_body(x_ref, buf0_ref, w1_ref, b1_ref, wdw_ref, bdw_ref, g_ref, b_ref, w2_ref, b2_ref,
               o_ref, tail_ref, ext_ref, z_ref, *, t_last):
    bf16 = jnp.bfloat16
    tt = x_ref.shape[1]
    i = pl.program_id(1)

    @pl.when(i == 0)
    def _():
        ext_ref[0:CONV_HALO, :] = buf0_ref[0]
        ext_ref[CONV_HALO + tt:CONV_HALO + tt + SUBLANES, :] = jnp.zeros((SUBLANES, D_CONV), jnp.float32)

    h = _dot(x_ref[0].astype(bf16), w1_ref[...]) + b1_ref[...]
    ext_ref[CONV_HALO:CONV_HALO + tt, :] = h[:, :D_CONV] * jax.nn.sigmoid(h[:, D_CONV:])
    c = jnp.zeros((tt, D_CONV), jnp.float32) + bdw_ref[...]
    for r in range(SUBLANES):
        z = None
        for a in range(CONV_HALO // SUBLANES + 1):
            k = SUBLANES * a + r - CONV_LEAD
            if 0 <= k < CONV_W:
                term = ext_ref[SUBLANES * a:SUBLANES * a + tt + SUBLANES, :] * wdw_ref[k:k + 1, :]
                z = term if z is None else z + term
        if r == 0:
            c = c + z[:tt]
        else:
            z_ref[...] = z
            c = c + z_ref[pl.ds(r, tt), :]
    c = _ln_rows(c, g_ref[...], b_ref[...])
    c = c * jax.nn.sigmoid(c)
    o_ref[0] = _dot(c.astype(bf16), w2_ref[...]) + b2_ref[...]
    tail_ref[0] = ext_ref[t_last:t_last + CONV_HALO, :]
    ext_ref[0:CONV_HALO, :] = ext_ref[tt:tt + CONV_HALO, :]


def _conv_module(x, conv_buf, w_pw1, b_pw1, w_dw, b_dw, ln_g, ln_b, w_pw2, b_pw2):
    bsz, t_, d = x.shape
    bf16 = jnp.bfloat16
    tp = -(-t_ // 8) * 8
    tt = min(tp, 256)
    n_t = tp // tt
    if tp != t_:
        x = jnp.pad(x, ((0, 0), (0, tp - t_), (0, 0)))
    if conv_buf is None:
        buf0 = jnp.zeros((bsz, CONV_HALO, D_CONV), jnp.float32)
    else:
        buf0 = jnp.pad(conv_buf, ((0, 0), (CONV_LEAD, 0), (0, 0)))
    fixed = lambda shape: pl.BlockSpec(shape, lambda b, i: (0,) * len(shape))
    per_b = pl.BlockSpec((1, CONV_HALO, D_CONV), lambda b, i: (b, 0, 0))
    out, tail = pl.pallas_call(
        functools.partial(_conv_body, t_last=t_ - (n_t - 1) * tt),
        grid=(bsz, n_t),
        in_specs=[pl.BlockSpec((1, tt, d), lambda b, i: (b, i, 0)), per_b,
                  fixed((d, 2 * D_CONV)), fixed((1, 2 * D_CONV)), fixed((CONV_HALO, D_CONV)), fixed((1, D_CONV)),
                  fixed((1, D_CONV)), fixed((1, D_CONV)), fixed((D_CONV, d)), fixed((1, d))],
        out_specs=[pl.BlockSpec((1, tt, d), lambda b, i: (b, i, 0)), per_b],
        out_shape=[jax.ShapeDtypeStruct((bsz, tp, d), jnp.float32),
                   jax.ShapeDtypeStruct((bsz, CONV_HALO, D_CONV), jnp.float32)],
        scratch_shapes=[pltpu.VMEM((CONV_HALO + tt + SUBLANES, D_CONV), jnp.float32),
                        pltpu.VMEM((tt + SUBLANES, D_CONV), jnp.float32)],
        compiler_params=pltpu.CompilerParams(dimension_semantics=("arbitrary", "arbitrary"),
                                             vmem_limit_bytes=VMEM_LIMIT),
        name="conv_module",
    )(x, buf0, w_pw1.astype(bf16), b_pw1.reshape(1, -1), jnp.pad(w_dw, ((0, CONV_HALO - CONV_W), (0, 0))),
      b_dw.reshape(1, -1), ln_g.reshape(1, -1), ln_b.reshape(1, -1), w_pw2.astype(bf16), b_pw2.reshape(1, -1))
    return out[:, :t_], tail[:, CONV_LEAD:]


PACK_W = 256
SC_WINDOW = 128
SC_TILES = 32
MOE_ALIGN = SC_WINDOW * SC_TILES // (2 * TOP_K)


def _pack_rows(y):
    out = []
    for h in range(2):
        lo = lax.bitcast_convert_type(y[:, 2 * h * PACK_W:(2 * h + 1) * PACK_W].astype(jnp.bfloat16)
                                      .astype(jnp.float32), jnp.uint32)
        hi = lax.bitcast_convert_type(y[:, (2 * h + 1) * PACK_W:(2 * h + 2) * PACK_W].astype(jnp.bfloat16)
                                      .astype(jnp.float32), jnp.uint32)
        out.append(lax.bitcast_convert_type((lo >> 16) | hi, jnp.int32))
    return out


def _unpack_words(w):
    u = lax.bitcast_convert_type(w, jnp.uint32)
    lo = lax.bitcast_convert_type(u << 16, jnp.float32)
    hi = lax.bitcast_convert_type(u & jnp.uint32(0xFFFF0000), jnp.float32)
    return lo, hi


def _gather_rows(src, idx):
    n = idx.shape[0]
    if n % (SC_WINDOW * SC_TILES) != 0:
        return jnp.take(src, idx, axis=0)
    mesh = plsc.VectorSubcoreMesh(core_axis_name="core", subcore_axis_name="subcore")

    @pl.kernel(out_type=jax.ShapeDtypeStruct((n, src.shape[1]), src.dtype), mesh=mesh)
    def gather_kernel(src_hbm, idx_hbm, out_hbm):
        def step(idx_vmem, out_vmem):
            pltpu.sync_copy(src_hbm.at[idx_vmem.at[0]], out_vmem)

        pltpu.emit_pipeline(
            step, grid=(n // SC_WINDOW,),
            in_specs=[pl.BlockSpec((1, SC_WINDOW), index_map=lambda i: (0, i))],
            out_specs=[pl.BlockSpec((SC_WINDOW, src.shape[1]), index_map=lambda i: (i, 0))],
            core_axis_name=("core", "subcore"),
            dimension_semantics=(pltpu.PARALLEL,),
        )(idx_hbm, out_hbm)

    return gather_kernel(src, idx.reshape(1, n))


def _scatter_rows(src, idx, n_out):
    n = idx.shape[0]
    m = src.shape[0] // 2
    reps = n // (2 * m)
    if n % (SC_WINDOW * SC_TILES) != 0 or m % SC_WINDOW != 0:
        rows = jnp.arange(n, dtype=jnp.int32)
        src_row = (rows // (reps * m)) * m + rows % m
        return jnp.zeros((n_out, src.shape[1]), src.dtype).at[idx].set(jnp.take(src, src_row, axis=0))
    tiles = m // SC_WINDOW
    mesh = plsc.VectorSubcoreMesh(core_axis_name="core", subcore_axis_name="subcore")

    @pl.kernel(out_type=jax.ShapeDtypeStruct((n_out, src.shape[1]), src.dtype), mesh=mesh, scratch_types=[])
    def scatter_kernel(src_hbm, idx_hbm, out_hbm):
        def step(src_vmem, idx_vmem):
            pltpu.sync_copy(src_vmem, out_hbm.at[idx_vmem.at[0]])

        pltpu.emit_pipeline(
            step, grid=(n // SC_WINDOW,),
            in_specs=[pl.BlockSpec((SC_WINDOW, src.shape[1]),
                                   index_map=lambda i: ((i // (reps * tiles)) * tiles + i % tiles, 0)),
                      pl.BlockSpec((1, SC_WINDOW), index_map=lambda i: (0, i))],
            out_specs=[],
            core_axis_name=("core", "subcore"),
            dimension_semantics=(pltpu.PARALLEL,),
        )(src_hbm, idx_hbm)

    return scatter_kernel(src, idx.reshape(1, n))


PER_GROUP = N_EXPERTS // N_GROUPS
PICKED = -3e38


def _ln_rows(v, g, b):
    mu = jnp.mean(v, axis=-1, keepdims=True)
    c = v - mu
    var = jnp.mean(c * c, axis=-1, keepdims=True)
    return c * lax.rsqrt(var + LN_EPS) * g + b


def _first_max(v, ids, axes, sentinel):
    best = v
    for a in axes:
        best = jnp.max(best, axis=a, keepdims=True)
    first = jnp.where(v == best, ids, sentinel)
    for a in axes:
        first = jnp.min(first, axis=a, keepdims=True)
    return best, first


def _sum_axes(v, axes):
    for a in axes:
        v = jnp.sum(v, axis=a, keepdims=True)
    return v


def _moe_pre_body(x_ref, mix_ref, g_ref, b_ref, wr_ref, br_ref,
                  x1_ref, xp_ref, eidx_ref, gate_ref, rank_ref, cnt_ref, run_ref):
    f32, bf16 = jnp.float32, jnp.bfloat16
    tm = x_ref.shape[0]

    @pl.when(pl.program_id(0) == 0)
    def _():
        run_ref[...] = jnp.zeros(run_ref.shape, f32)

    x1 = _ln_rows(ALPHA * x_ref[...] + mix_ref[...], g_ref[...], b_ref[...])
    x1_ref[...] = x1
    x1b = x1.astype(bf16)
    xp_ref[0], xp_ref[1] = _pack_rows(x1)

    s = jax.nn.sigmoid(_dot_nt(wr_ref[...], x1b)).reshape(N_GROUPS, PER_GROUP, tm)
    sb = s + br_ref[...].reshape(N_GROUPS, PER_GROUP, 1)
    shape3 = (N_GROUPS, PER_GROUP, tm)
    pid = lax.broadcasted_iota(jnp.int32, shape3, 1)
    gid = lax.broadcasted_iota(jnp.int32, (N_GROUPS, 1, tm), 0)
    eid = lax.broadcasted_iota(jnp.int32, shape3, 0) * PER_GROUP + pid
    top1, i1 = _first_max(sb, pid, (1,), PER_GROUP)
    top2 = jnp.max(jnp.where(pid == i1, PICKED, sb), axis=1, keepdims=True)
    gscore = top1 + top2
    gsel = jnp.zeros((N_GROUPS, 1, tm), f32)
    for _ in range(TOPK_GROUPS):
        _, first = _first_max(gscore, gid, (0,), N_GROUPS)
        hit = gid == first
        gsel = jnp.where(hit, 1.0, gsel)
        gscore = jnp.where(hit, PICKED, gscore)
    cand = jnp.where(gsel > 0.0, sb, -1e30)
    firsts, gates = [], []
    picked = jnp.zeros(shape3, f32)
    for _ in range(TOP_K):
        _, first = _first_max(cand, eid, (0, 1), N_EXPERTS)
        hit = eid == first
        firsts.append(first)
        gates.append(_sum_axes(jnp.where(hit, s, 0.0), (0, 1)))
        picked = jnp.where(hit, 1.0, picked)
        cand = jnp.where(hit, PICKED, cand)
    gsum = gates[0]
    for gk in gates[1:]:
        gsum = gsum + gk
    earlier = (lax.broadcasted_iota(jnp.int32, (tm, tm), 0) < lax.broadcasted_iota(jnp.int32, (tm, tm), 1))
    picked2 = picked.reshape(N_EXPERTS, tm)
    rank = run_ref[...] + _dot(picked2.astype(bf16), jnp.where(earlier, 1.0, 0.0).astype(bf16))
    run_new = run_ref[...] + jnp.sum(picked2, axis=1, keepdims=True)
    run_ref[...] = run_new
    cnt_ref[...] = jnp.broadcast_to(run_new, cnt_ref.shape)
    rank3 = rank.reshape(shape3)
    for k in range(TOP_K):
        hit = eid == firsts[k]
        eidx_ref[k:k + 1, :] = firsts[k].reshape(1, tm)
        gate_ref[k:k + 1, :] = (gates[k] / gsum * ROUTE_SCALE).reshape(1, tm)
        rank_ref[k:k + 1, :] = _sum_axes(jnp.where(hit, rank3, 0.0), (0, 1)).reshape(1, tm).astype(jnp.int32)


def _moe_pre(x, mix, g, b, w_router, b_router):
    m, d = x.shape
    bf16 = jnp.bfloat16
    tm = min(m, 512)
    row = lambda i: (i, 0)
    col = lambda i: (0, i)
    fixed = lambda i: (0, 0)
    return pl.pallas_call(
        _moe_pre_body,
        grid=(m // tm,),
        in_specs=[pl.BlockSpec((tm, d), row), pl.BlockSpec((tm, d), row),
                  pl.BlockSpec((1, d), fixed), pl.BlockSpec((1, d), fixed),
                  pl.BlockSpec((N_EXPERTS, d), fixed), pl.BlockSpec((N_EXPERTS, 1), fixed)],
        out_specs=[pl.BlockSpec((tm, d), row), pl.BlockSpec((2, tm, PACK_W), lambda i: (0, i, 0)),
                   pl.BlockSpec((TOP_K, tm), col), pl.BlockSpec((TOP_K, tm), col), pl.BlockSpec((TOP_K, tm), col),
                   pl.BlockSpec((N_EXPERTS, LANE), fixed)],
        out_shape=[jax.ShapeDtypeStruct((m, d), jnp.float32), jax.ShapeDtypeStruct((2, m, PACK_W), jnp.int32),
                   jax.ShapeDtypeStruct((TOP_K, m), jnp.int32), jax.ShapeDtypeStruct((TOP_K, m), jnp.float32),
                   jax.ShapeDtypeStruct((TOP_K, m), jnp.int32),
                   jax.ShapeDtypeStruct((N_EXPERTS, LANE), jnp.float32)],
        scratch_shapes=[pltpu.VMEM((N_EXPERTS, 1), jnp.float32)],
        compiler_params=pltpu.CompilerParams(dimension_semantics=("arbitrary",),
                                             vmem_limit_bytes=VMEM_LIMIT),
        name="moe_pre",
    )(x, mix, g.reshape(1, d), b.reshape(1, d), w_router.T.astype(bf16), b_router.reshape(N_EXPERTS, 1))


def _moe_expert_body(exp_ref, first_ref, rows_ref, xs_ref, wgu_ref, wdn_ref, y_ref, wgu_bf, wdn_bf):
    i = pl.program_id(0)
    bf16 = jnp.bfloat16

    @pl.when(first_ref[i] == 1)
    def _():
        wgu_bf[...] = wgu_ref[0, 0].astype(bf16)
        wdn_bf[...] = wdn_ref[0, 0].astype(bf16)

    @pl.when(rows_ref[i] > 0)
    def _():
        live = lax.broadcasted_iota(jnp.int32, (xs_ref.shape[1], 1), 0) < rows_ref[i]
        h = None
        for hw in range(2):
            for q, xq in enumerate(_unpack_words(xs_ref[hw])):
                r0 = (2 * hw + q) * PACK_W
                part = _dot(jnp.where(live, xq, 0.0).astype(bf16), wgu_bf[r0:r0 + PACK_W, :])
                h = part if h is None else h + part
        d_e = h.shape[1] // 2
        act = (jax.nn.silu(h[:, :d_e]) * h[:, d_e:]).astype(bf16)
        y_ref[0], y_ref[1] = _pack_rows(_dot(act, wdn_bf[...]))

    @pl.when(rows_ref[i] == 0)
    def _():
        y_ref[...] = jnp.zeros(y_ref.shape, y_ref.dtype)


def _moe_experts(xs, blk_exp, blk_first, blk_rows, w_exp_gu, w_exp_down, layer, bm):
    n_slots = xs.shape[1]
    d = w_exp_gu.shape[2]
    n_blk = n_slots // bm
    d_e2 = w_exp_gu.shape[3]
    words = lambda i, e, f, a: (0, i, 0)
    grid_spec = pltpu.PrefetchScalarGridSpec(
        num_scalar_prefetch=3,
        grid=(n_blk,),
        in_specs=[pl.BlockSpec((2, bm, PACK_W), words),
                  pl.BlockSpec((1, 1, d, d_e2), lambda i, e, f, a: (layer, e[i], 0, 0)),
                  pl.BlockSpec((1, 1, d_e2 // 2, d), lambda i, e, f, a: (layer, e[i], 0, 0))],
        out_specs=pl.BlockSpec((2, bm, PACK_W), words),
        scratch_shapes=[pltpu.VMEM((d, d_e2), jnp.bfloat16), pltpu.VMEM((d_e2 // 2, d), jnp.bfloat16)])
    return pl.pallas_call(
        _moe_expert_body,
        grid_spec=grid_spec,
        out_shape=jax.ShapeDtypeStruct((2, n_slots, PACK_W), jnp.int32),
        compiler_params=pltpu.CompilerParams(dimension_semantics=("arbitrary",),
                                             vmem_limit_bytes=VMEM_LIMIT),
        name="moe_experts",
    )(blk_exp, blk_first, blk_rows, xs, w_exp_gu, w_exp_down)


def _combine_ln_body(x_ref, yg_ref, gt_ref, wgu_ref, wdn_ref, g_ref, b_ref, o_ref):
    bf16 = jnp.bfloat16
    x1 = x_ref[...]
    h = _dot(x1.astype(bf16), wgu_ref[...])
    d_sh = h.shape[1] // 2
    shared = _dot((jax.nn.silu(h[:, :d_sh]) * h[:, d_sh:]).astype(bf16), wdn_ref[...])
    gt = gt_ref[...]
    parts = []
    for hw in range(2):
        lo_acc = hi_acc = None
        for k in range(TOP_K):
            lo, hi = _unpack_words(yg_ref[hw, k])
            gk = gt[:, k:k + 1]
            lo_acc = lo * gk if lo_acc is None else lo_acc + lo * gk
            hi_acc = hi * gk if hi_acc is None else hi_acc + hi * gk
        parts += [lo_acc, hi_acc]
    routed = jnp.concatenate(parts, axis=1)
    o_ref[...] = _ln_rows(ALPHA * x1 + (routed + shared), g_ref[...], b_ref[...])


def _combine_ln(x, yg, gate_t, w_sh_gu, w_sh_down, g, b):
    m, d = x.shape
    d_sh2 = w_sh_gu.shape[1]
    tm = min(m, 256)
    row = lambda i: (i, 0)
    fixed = lambda i: (0, 0)
    return pl.pallas_call(
        _combine_ln_body,
        grid=(m // tm,),
        in_specs=[pl.BlockSpec((tm, d), row), pl.BlockSpec((2, TOP_K, tm, PACK_W), lambda i: (0, 0, i, 0)),
                  pl.BlockSpec((tm, TOP_K), row), pl.BlockSpec((d, d_sh2), fixed), pl.BlockSpec((d_sh2 // 2, d), fixed),
                  pl.BlockSpec((1, d), fixed), pl.BlockSpec((1, d), fixed)],
        out_specs=pl.BlockSpec((tm, d), row),
        out_shape=jax.ShapeDtypeStruct((m, d), jnp.float32),
        compiler_params=pltpu.CompilerParams(dimension_semantics=("arbitrary",)),
        name="combine_ln",
    )(x, yg, gate_t, w_sh_gu.astype(jnp.bfloat16), w_sh_down.astype(jnp.bfloat16), g.reshape(1, d), b.reshape(1, d))


def _moe_layer(streams, ln1_g, ln1_b, ln2_g, ln2_b, w_router, b_router, w_exp_gu, w_exp_down, layer,
               w_sh_gu, w_sh_down):
    pre = [_moe_pre(x, mix, ln1_g, ln1_b, w_router, b_router) for x, mix in streams]
    m_all = sum(x.shape[0] for x, _ in streams)
    bm = 512 if m_all * TOP_K >= 512 * N_EXPERTS else MOE_BLK
    m_pad = -(-m_all // MOE_ALIGN) * MOE_ALIGN
    spare = -(-(TOP_K * (m_pad - m_all)) // bm)
    n_blk = (m_all * TOP_K) // bm + N_EXPERTS + spare
    n_slots = n_blk * bm
    counts_of = [p[5][:, 0].astype(jnp.int32) for p in pre]
    counts = sum(counts_of)
    padded = (counts + bm - 1) // bm * bm
    pad_end = jnp.cumsum(padded)
    pad_start = pad_end - padded
    dests, before = [], jnp.zeros_like(counts)
    for p, cnt in zip(pre, counts_of):
        eidx, rank8 = p[2], p[4]
        start_of = jnp.sum(jnp.where(eidx[:, :, None] == jnp.arange(N_EXPERTS), pad_start + before, 0), axis=-1)
        dests.append(start_of + rank8)
        before = before + cnt
    blk_start = jnp.arange(n_blk, dtype=jnp.int32) * bm
    blk_exp = jnp.minimum(jnp.sum(pad_end[None, :] <= blk_start[:, None], axis=1), N_EXPERTS - 1).astype(jnp.int32)
    blk_rows = jnp.clip(counts[blk_exp] - (blk_start - pad_start[blk_exp]), 0, bm).astype(jnp.int32)
    blk_first = jnp.concatenate([jnp.ones((1,), jnp.int32), (blk_exp[1:] != blk_exp[:-1]).astype(jnp.int32)])
    spare_slots = (n_slots - spare * bm) + jnp.arange(TOP_K * (m_pad - m_all), dtype=jnp.int32)
    dest_all = jnp.concatenate(dests + [spare_slots.reshape(TOP_K, m_pad - m_all)], axis=1).reshape(-1)
    fill = jnp.zeros((2, m_pad - m_all, PACK_W), jnp.int32)
    words = jnp.concatenate([p[1] for p in pre] + [fill], axis=1).reshape(2 * m_pad, PACK_W)
    xs = _scatter_rows(words, jnp.concatenate([dest_all, dest_all + n_slots]), 2 * n_slots)
    y = _moe_experts(xs.reshape(2, n_slots, PACK_W), blk_exp, blk_first, blk_rows, w_exp_gu, w_exp_down, layer, bm)
    y = y.reshape(2 * n_slots, PACK_W)
    outs = []
    for p, d_i, (x, _) in zip(pre, dests, streams):
        d_i = d_i.reshape(-1)
        yg = _gather_rows(y, jnp.concatenate([d_i, d_i + n_slots])).reshape(2, TOP_K, x.shape[0], PACK_W)
        outs.append(_combine_ln(p[0], yg, p[3].T, w_sh_gu, w_sh_down, ln2_g, ln2_b))
    return outs


def _trunks(x_p, x_s, pos_p, pos_s, gla_state, nsa_cache, page_table, win_buf, conv_buf,
            w_in_ab, w_gla_gate, b_gla_gate, gla_norm_g, w_cmp_pool, w_out_ab,
            w_pw1, b_pw1, w_dw, b_dw, conv_ln_g, conv_ln_b, w_pw2, b_pw2,
            ln_g, ln_b, w_router, b_router, w_exp_gu, w_exp_down, w_sh_gu, w_sh_down):
    xs = [x_p, x_s]
    states = [(None, None, None, None), (gla_state, nsa_cache, win_buf, conv_buf)]
    poss = [pos_p, pos_s]
    new = [dict(gla=[], rows=[], win=[], conv=[]) for _ in xs]
    for layer in range(DEPTH):
        i = layer // 2
        mixes = []
        for x, pos, (g_st, cache, wbuf, cbuf), out in zip(xs, poss, states, new):
            if layer % 2 == 0:
                mix, s_a, rows, win = _ab_mixer(
                    x, pos, w_in_ab[i], w_gla_gate[i], b_gla_gate[i], gla_norm_g[i], w_cmp_pool[i], w_out_ab[i],
                    None if g_st is None else g_st[i], None if cache is None else cache[i], page_table,
                    None if wbuf is None else wbuf[i])
                out["gla"].append(s_a)
                out["rows"].append(rows)
                out["win"].append(win)
            else:
                mix, cb = _conv_module(x, None if cbuf is None else cbuf[i], w_pw1[i], b_pw1[i],
                                       w_dw[i], b_dw[i], conv_ln_g[i], conv_ln_b[i], w_pw2[i], b_pw2[i])
                out["conv"].append(cb)
            mixes.append(mix)
        d = xs[0].shape[-1]
        ys = _moe_layer([(x.reshape(-1, d), mix.reshape(-1, d)) for x, mix in zip(xs, mixes)],
                        ln_g[layer, 0], ln_b[layer, 0], ln_g[layer, 1], ln_b[layer, 1],
                        w_router[layer], b_router[layer], w_exp_gu, w_exp_down, layer,
                        w_sh_gu[layer], w_sh_down[layer])
        xs = [y.reshape(x.shape) for y, x in zip(ys, xs)]
    return [(x, jnp.stack(o["gla"]), jnp.stack(o["rows"]), jnp.stack(o["win"]), jnp.stack(o["conv"]))
            for x, o in zip(xs, new)]


def kernel(x_prompt, x_sample, state_gla, cache_nsa_kv, state_nsa_win, state_conv, page_table,
           w_in_ab, w_gla_gate, b_gla_gate, gla_norm_g, w_cmp_pool, w_out_ab,
           w_pw1, b_pw1, w_dw, b_dw, conv_ln_g, conv_ln_b, w_pw2, b_pw2,
           ln_g, ln_b, w_router, b_router, w_exp_gu, w_exp_down, w_sh_gu, w_sh_down):
    weights = (w_in_ab, w_gla_gate, b_gla_gate, gla_norm_g, w_cmp_pool, w_out_ab,
               w_pw1, b_pw1, w_dw, b_dw, conv_ln_g, conv_ln_b, w_pw2, b_pw2,
               ln_g, ln_b, w_router, b_router, w_exp_gu, w_exp_down, w_sh_gu, w_sh_down)
    past_len = page_table.shape[1] * PAGE_SIZE
    pos_p = jnp.arange(x_prompt.shape[1])
    pos_s = past_len + jnp.arange(x_sample.shape[1])
    (y_prompt, gla_p, rows_p, win_p, conv_p), (y_sample, gla_s, rows_s, win_s, conv_s) = _trunks(
        x_prompt, x_sample, pos_p, pos_s, state_gla, cache_nsa_kv, page_table, state_nsa_win, state_conv, *weights)
    return (y_prompt, y_sample, gla_p, gla_s, rows_p, rows_s, win_p, win_s, conv_p, conv_s)
```

```python
import functools
import math

import jax
import jax.numpy as jnp
import numpy as np
from jax import lax
from jax.experimental import pallas as pl
from jax.experimental.pallas import tpu as pltpu
from jax.experimental.pallas import tpu_sc as plsc

D_MODEL = 1024
DEPTH = 2
PAGE_SIZE = 128

GLA_HEADS = 4
GLA_DV = D_MODEL // 2 // GLA_HEADS
GLA_DK = GLA_DV // 2
GLA_RANK = 16
GLA_TAU = 16.0

NSA_HEADS = 8
NSA_KV_HEADS = 2
NSA_GROUP = NSA_HEADS // NSA_KV_HEADS
HEAD_DIM = D_MODEL // 2 // NSA_HEADS
CMP_BLK = 32
CMP_STRIDE = 16
SEL_BLK = 64
SEL_TOPN = 16
WINDOW = 512
Q_BLK = 128
FORCE_BONUS = 100.0
ROPE_DIM = HEAD_DIM // 4
ROPE_THETA = 500000.0

GLA_SIZES = (GLA_HEADS * GLA_DK, GLA_HEADS * GLA_DK, GLA_HEADS * GLA_DV, GLA_HEADS * GLA_DV, GLA_RANK)
NSA_SIZES = (NSA_HEADS * HEAD_DIM, 6 * NSA_KV_HEADS * HEAD_DIM, 3 * NSA_HEADS)

CONV_W = 31
D_CONV = D_MODEL

N_EXPERTS = 64
N_GROUPS = 8
TOPK_GROUPS = 4
TOP_K = 8
D_EXPERT = 256
ROUTE_SCALE = 2.5
MOE_BLK = 128

ALPHA = (2 * DEPTH) ** 0.25
LN_EPS = 1e-5

LANE = 128
SUBLANES = 8
V7X_VMEM_BYTES = 64 * 1024 * 1024
VMEM_LIMIT = V7X_VMEM_BYTES * 3 // 4


def _dot(a, b):
    return jnp.dot(a, b, preferred_element_type=jnp.float32)


def _dot_nt(a, b):
    return lax.dot_general(a, b, (((1,), (1,)), ((), ())), preferred_element_type=jnp.float32)


def _mm_body(x_ref, w_ref, o_ref):
    o_ref[...] = _dot(x_ref[...].astype(jnp.bfloat16), w_ref[...].astype(jnp.bfloat16))


def _mm(x, w, keep_pad=False):
    m, k = x.shape
    n = w.shape[1]
    n_pad = -(-n // LANE) * LANE
    w = w.astype(jnp.bfloat16)
    if n_pad != n:
        w = jnp.pad(w, ((0, 0), (0, n_pad - n)))
    tm = min(m, 512)
    out = pl.pallas_call(
        _mm_body,
        grid=(m // tm,),
        in_specs=[pl.BlockSpec((tm, k), lambda i: (i, 0)),
                  pl.BlockSpec((k, n_pad), lambda i: (0, 0))],
        out_specs=pl.BlockSpec((tm, n_pad), lambda i: (i, 0)),
        out_shape=jax.ShapeDtypeStruct((m, n_pad), jnp.float32),
        compiler_params=pltpu.CompilerParams(dimension_semantics=("arbitrary",),
                                             vmem_limit_bytes=VMEM_LIMIT),
        name="mm",
    )(x, w)
    return out if keep_pad or n_pad == n else out[:, :n]


def _mm_pair_body(a_ref, b_ref, w_ref, o_ref):
    ka = a_ref.shape[1]
    o_ref[...] = (_dot(a_ref[...].astype(jnp.bfloat16), w_ref[0:ka, :])
                  + _dot(b_ref[...].astype(jnp.bfloat16), w_ref[ka:, :]))


def _mm_pair(a, b, w):
    m, ka = a.shape
    kb = b.shape[1]
    n = w.shape[1]
    tm = min(m, 512)
    return pl.pallas_call(
        _mm_pair_body,
        grid=(m // tm,),
        in_specs=[pl.BlockSpec((tm, ka), lambda i: (i, 0)), pl.BlockSpec((tm, kb), lambda i: (i, 0)),
                  pl.BlockSpec((ka + kb, n), lambda i: (0, 0))],
        out_specs=pl.BlockSpec((tm, n), lambda i: (i, 0)),
        out_shape=jax.ShapeDtypeStruct((m, n), jnp.float32),
        compiler_params=pltpu.CompilerParams(dimension_semantics=("arbitrary",)),
        name="mm_pair",
    )(a, b, w.astype(jnp.bfloat16))


def _partial_rope(x, pos):
    half = ROPE_DIM // 2
    inv_freq = jnp.power(ROPE_THETA, -jnp.arange(half, dtype=jnp.float32) / half)
    ang = pos.astype(jnp.float32)[:, None] * inv_freq
    ang = ang.reshape(ang.shape[0], *([1] * (x.ndim - 3)), half)
    cos, sin = jnp.cos(ang), jnp.sin(ang)
    x1 = x[..., :half]
    x2 = x[..., half:ROPE_DIM]
    rot = jnp.concatenate([x1 * cos - x2 * sin, x2 * cos + x1 * sin], -1)
    return jnp.concatenate([rot, x[..., ROPE_DIM:]], -1)


NSA_ROWS = NSA_GROUP * Q_BLK
SEL_KT = 2048
N_SELB = 128
MASKED = -1e9
WIN_KEYS = WINDOW + Q_BLK
KK_W = 2 * HEAD_DIM + N_SELB


def _nsa_prompt_body(qr_ref, qo_ref, kc_ref, vct_ref, kk_ref, vvt_ref, g_ref, o_ref,
                     imp_ref, m_ref, l_ref, acc_ref, oct_ref, selb_ref):
    f32, bf16 = jnp.float32, jnp.bfloat16
    qb = pl.program_id(2)
    q0 = qb * Q_BLK
    qr_t = qr_ref[0, 0, 0]
    qo_t = qo_ref[0, 0, 0]
    n_cmp = kc_ref.shape[2]

    ratio = SEL_BLK // CMP_STRIDE
    chunk = min(Q_BLK, n_cmp)
    n_chunks = n_cmp // chunk

    def compressed_and_select(n_act):
        nc = n_act * chunk
        nb = nc // ratio
        s_c = _dot(kc_ref[0, 0, 0:nc, :], qr_t)
        n_idx = lax.broadcasted_iota(jnp.int32, (nc, NSA_ROWS), 0)
        qpos_c = q0 + (lax.broadcasted_iota(jnp.int32, (nc, NSA_ROWS), 1) & (Q_BLK - 1))
        cmask = (n_idx * CMP_STRIDE + (CMP_BLK - 1)) <= qpos_c
        s_c = jnp.where(cmask, s_c, MASKED)
        m_c = jnp.max(s_c, axis=0, keepdims=True)
        p_c = jnp.where(cmask, jnp.exp(s_c - m_c), 0.0)
        p_c = p_c / jnp.maximum(jnp.sum(p_c, axis=0, keepdims=True), 1e-30)
        oct_ref[...] = _dot(vct_ref[0, 0, :, 0:nc], p_c.astype(bf16))
        imp = (p_c[:, 0:Q_BLK] + p_c[:, Q_BLK:2 * Q_BLK]) + p_c[:, 2 * Q_BLK:3 * Q_BLK] + p_c[:, 3 * Q_BLK:]
        imp_ref[0:8, :] = jnp.zeros((8, Q_BLK), f32)
        imp_ref[8:8 + nc, :] = imp
        imp_s = imp_ref[pl.ds(7, nb, stride=ratio), :]
        for r in range(ratio):
            imp_s = imp_s + imp_ref[pl.ds(8 + r, nb, stride=ratio), :]
        blk = lax.broadcasted_iota(jnp.int32, (nb, Q_BLK), 0)
        qpos_s = q0 + lax.broadcasted_iota(jnp.int32, (nb, Q_BLK), 1)
        cur = lax.shift_right_logical(qpos_s, int(math.log2(SEL_BLK)))
        valid = blk * SEL_BLK <= qpos_s
        forced = (blk == 0) | (blk == cur) | (blk == cur - 1)
        score = jnp.where(valid, imp_s + jnp.where(forced, FORCE_BONUS, 0.0), -1e30)
        picked = jnp.zeros((nb, Q_BLK), f32)
        for _ in range(SEL_TOPN):
            best = jnp.max(score, axis=0, keepdims=True)
            first = jnp.min(jnp.where(score == best, blk, nb), axis=0, keepdims=True)
            hit = blk == first
            picked = jnp.where(hit, 1.0, picked)
            score = jnp.where(hit, -3e38, score)
        sel = jnp.where(valid, picked, 0.0)
        if nb < N_SELB:
            sel = jnp.concatenate([sel, jnp.zeros((N_SELB - nb, Q_BLK), f32)], axis=0)
        sel = ((sel - 1.0) * (-MASKED)).astype(bf16)
        selb_ref[...] = jnp.concatenate([sel] * NSA_GROUP, axis=1)

    need = jnp.minimum((q0 + Q_BLK - CMP_BLK) // (CMP_STRIDE * chunk) + 1, n_chunks)
    for n_act in range(1, n_chunks + 1):
        pl.when(need == n_act)(functools.partial(compressed_and_select, n_act))
    o_ct = oct_ref[...]
    selb_t = selb_ref[...]

    zeros_q = jnp.zeros((HEAD_DIM, NSA_ROWS), bf16)
    q_sel = jnp.concatenate([qo_t, zeros_q, selb_t], axis=0)
    q_win = jnp.concatenate([zeros_q, qo_t, jnp.zeros((N_SELB, NSA_ROWS), bf16)], axis=0)
    qpos_r = q0 + (lax.broadcasted_iota(jnp.int32, (1, NSA_ROWS), 1) & (Q_BLK - 1))

    def v_tiles(first, count):
        return jnp.concatenate([vvt_ref[0, 0, first + j] for j in range(count)], axis=1)

    m_ref[...] = jnp.full(m_ref.shape, MASKED, f32)
    l_ref[...] = jnp.zeros(l_ref.shape, f32)
    acc_ref[...] = jnp.zeros(acc_ref.shape, f32)

    def sel_tile(k0, kt, causal):
        s = _dot(kk_ref[0, 0, pl.ds(k0, kt), :], q_sel)
        if causal:
            kpos = k0 + lax.broadcasted_iota(jnp.int32, (kt, NSA_ROWS), 0)
            s = jnp.where(kpos <= qpos_r, s, MASKED)
        m_old = m_ref[...]
        m_new = jnp.maximum(m_old, jnp.max(s, axis=0, keepdims=True))
        alpha = jnp.exp(m_old - m_new)
        p = jnp.exp(s - m_new)
        l_ref[...] = alpha * l_ref[...] + jnp.sum(p, axis=0, keepdims=True)
        vt = v_tiles(k0 // Q_BLK, kt // Q_BLK)
        acc_ref[...] = alpha * acc_ref[...] + _dot(vt, p.astype(bf16))
        m_ref[...] = m_new

    n_full = q0 // SEL_KT

    def full_step(t, c):
        sel_tile(pl.multiple_of(t * SEL_KT, SEL_KT), SEL_KT, False)
        return c

    lax.fori_loop(0, n_full, full_step, 0)
    d0 = pl.multiple_of(n_full * SEL_KT, SEL_KT)
    short = q0 + Q_BLK - n_full * SEL_KT <= SEL_KT // 2

    @pl.when(short)
    def _():
        sel_tile(d0, SEL_KT // 2, True)

    @pl.when(jnp.logical_not(short))
    def _():
        sel_tile(d0, SEL_KT, True)
    o_st = acc_ref[0:HEAD_DIM, :] / l_ref[...]

    w0 = pl.multiple_of(jnp.maximum(q0 - WINDOW, 0), Q_BLK)
    s_w = _dot(kk_ref[0, 0, pl.ds(w0, WIN_KEYS), :], q_win)
    kpos_w = w0 + lax.broadcasted_iota(jnp.int32, (WIN_KEYS, NSA_ROWS), 0)
    s_w = jnp.where((kpos_w <= qpos_r) & (kpos_w > qpos_r - WINDOW), s_w, MASKED)
    p_w = jnp.exp(s_w - jnp.max(s_w, axis=0, keepdims=True))
    l_w = jnp.sum(p_w, axis=0, keepdims=True)
    acc_w = _dot(v_tiles(w0 // Q_BLK, WIN_KEYS // Q_BLK), p_w.astype(bf16))
    o_wt = acc_w[HEAD_DIM:2 * HEAD_DIM, :] / l_w

    g = g_ref[0, 0, 0]
    out_t = g[0:1, :] * o_ct + g[1:2, :] * o_st + g[2:3, :] * o_wt
    o_ref[0] = jnp.concatenate([out_t[:, g_ * Q_BLK:(g_ + 1) * Q_BLK] for g_ in range(NSA_GROUP)], axis=0).T


def _nsa_prompt(qr, qo, gt, kc_p, vct, kk, vvt):
    bsz, _, nqb = qr.shape[:3]
    t_ = nqb * Q_BLK
    n_cmp = kc_p.shape[2]
    per_blk = lambda b, h, i: (b, h, i, 0, 0)
    per_head = lambda b, h, i: (b, h, 0, 0)
    return pl.pallas_call(
        _nsa_prompt_body,
        grid=(bsz, NSA_KV_HEADS, nqb),
        in_specs=[pl.BlockSpec((1, 1, 1, HEAD_DIM, NSA_ROWS), per_blk),
                  pl.BlockSpec((1, 1, 1, HEAD_DIM, NSA_ROWS), per_blk),
                  pl.BlockSpec((1, 1, n_cmp, HEAD_DIM), per_head),
                  pl.BlockSpec((1, 1, HEAD_DIM, n_cmp), per_head),
                  pl.BlockSpec((1, 1, t_, KK_W), per_head),
                  pl.BlockSpec((1, 1, nqb, 2 * HEAD_DIM, Q_BLK), lambda b, h, i: (b, h, 0, 0, 0)),
                  pl.BlockSpec((1, 1, 1, 3, NSA_ROWS), per_blk)],
        out_specs=pl.BlockSpec((1, Q_BLK, NSA_GROUP * HEAD_DIM), lambda b, h, i: (b, i, h)),
        out_shape=jax.ShapeDtypeStruct((bsz, t_, NSA_HEADS * HEAD_DIM), jnp.float32),
        scratch_shapes=[pltpu.VMEM((8 + n_cmp, Q_BLK), jnp.float32),
                        pltpu.VMEM((1, NSA_ROWS), jnp.float32),
                        pltpu.VMEM((1, NSA_ROWS), jnp.float32),
                        pltpu.VMEM((2 * HEAD_DIM, NSA_ROWS), jnp.float32),
                        pltpu.VMEM((HEAD_DIM, NSA_ROWS), jnp.float32),
                        pltpu.VMEM((N_SELB, NSA_ROWS), jnp.bfloat16)],
        compiler_params=pltpu.CompilerParams(
            dimension_semantics=("arbitrary", "arbitrary", "arbitrary"),
            vmem_limit_bytes=VMEM_LIMIT),
        name="nsa_prompt",
    )(qr, qo, kc_p, vct, kk, vvt, gt)


GLA_SUB = 16
GLA_UNROLL = 8
GLA_TILE = 256
GLA_QK = GLA_HEADS * GLA_DK
GLA_V = GLA_HEADS * GLA_DV


def _dot_tn(a, b):
    return lax.dot_general(a, b, (((0,), (0,)), ((), ())), preferred_element_type=jnp.float32)


def _gla_body(q_ref, k_ref, v_ref, gr_ref, glr_ref, wg_ref, bg_ref, ng_ref, s0_ref, exp_ref,
              o_ref, sfin_ref, st_ref, b_ref, qd_ref, *, t_valid):
    f32, bf16 = jnp.float32, jnp.bfloat16
    tt = q_ref.shape[1]
    ti = pl.program_id(1)

    @pl.when(ti == 0)
    def _():
        st_ref[...] = s0_ref[0]

    row = lax.broadcasted_iota(jnp.int32, (tt, 1), 0)
    z = _dot(glr_ref[0][:, :GLA_RANK].astype(bf16), wg_ref[...]) + bg_ref[...]
    la = (jnp.minimum(z, 0.0) - jnp.log1p(jnp.exp(-jnp.abs(z)))) * (1.0 / GLA_TAU)
    la = jnp.where(ti * tt + row < t_valid, la, 0.0)
    seg = row & (GLA_SUB - 1)
    b = la
    for s in (1, 2, 4, 8):
        b = b + jnp.where(seg >= s, pltpu.roll(b, s, axis=0), 0.0)
    q = q_ref[0] * (GLA_DK ** -0.5)
    k = k_ref[0]
    v = v_ref[0]
    o = _dot((q * k).astype(bf16), exp_ref[...]) * v
    for d in range(1, GLA_SUB):
        decay = jnp.exp(jnp.minimum(b - pltpu.roll(b, d, axis=0), 0.0))
        w = jnp.where(seg >= d, q * pltpu.roll(k, d, axis=0) * decay, 0.0)
        o = o + _dot(w.astype(bf16), exp_ref[...]) * pltpu.roll(v, d, axis=0)
    o_ref[0] = o
    b_ref[...] = b
    qd_ref[...] = (q * jnp.exp(b)).astype(bf16)

    def block_step(c, carry):
        rows = pl.ds(pl.multiple_of(c * GLA_SUB, GLA_SUB), GLA_SUB)
        qd = qd_ref[rows, :]
        bc = b_ref[rows, :]
        bl = bc[GLA_SUB - 1:GLA_SUB, :]
        kc = (k_ref[0, rows, :] * jnp.exp(bl - bc)).astype(bf16)
        keep = jnp.exp(bl)
        vb = v_ref[0, rows, :].astype(bf16)
        outs = []
        for h in range(GLA_HEADS):
            dk = slice(h * GLA_DK, (h + 1) * GLA_DK)
            dv = slice(h * GLA_DV, (h + 1) * GLA_DV)
            st = st_ref[dv, :]
            outs.append(_dot_nt(qd[:, dk], st.astype(bf16)))
            st_ref[dv, :] = st * keep[:, dk] + _dot_tn(vb[:, dv], kc[:, dk])
        o_ref[0, rows, :] += jnp.concatenate(outs, axis=1)
        return carry

    lax.fori_loop(0, tt // GLA_SUB, block_step, 0, unroll=GLA_UNROLL)
    sfin_ref[0] = st_ref[...]
    gr = gr_ref[0]
    gate = gr * jax.nn.sigmoid(gr)
    for h in range(GLA_HEADS):
        cols = slice(h * GLA_DV, (h + 1) * GLA_DV)
        oh = o_ref[0, :, cols]
        ms = jnp.mean(oh * oh, axis=-1, keepdims=True)
        o_ref[0, :, cols] = oh * lax.rsqrt(ms + LN_EPS) * ng_ref[...] * gate[:, cols]


def _gla(h, w_gla_gate, b_gla_gate, gla_norm_g, gla_state):
    bsz, t_, n_in = h.shape
    tp = -(-t_ // GLA_SUB) * GLA_SUB
    if tp != t_:
        h = jnp.pad(h, ((0, 0), (0, tp - t_), (0, 0)))
    tt = min(tp, GLA_TILE)
    expand =np.repeat(np.repeat(np.eye(GLA_HEADS, dtype=np.float32), GLA_DK, 0), GLA_DV, 1)
    if gla_state is None:
        s0 = jnp.zeros((bsz, GLA_V, GLA_DK), jnp.float32)
    else:
        s0 = gla_state.transpose(0, 1, 3, 2).reshape(bsz, GLA_V, GLA_DK)
    tile = lambda width, blk: pl.BlockSpec((1, tt, width), lambda b, i: (b, i, blk))
    fixed2 = lambda shape: pl.BlockSpec(shape, lambda b, i: (0, 0))
    per_b = pl.BlockSpec((1, GLA_V, GLA_DK), lambda b, i: (b, 0, 0))
    o, s_t = pl.pallas_call(
        functools.partial(_gla_body, t_valid=t_),
        grid=(bsz, tp // tt),
        in_specs=[tile(GLA_QK, 0), tile(GLA_QK, 1), tile(GLA_V, 1), tile(GLA_V, 2),
                  tile(LANE, (2 * GLA_QK + 2 * GLA_V + NSA_SIZES[0] + NSA_SIZES[1]) // LANE),
                  fixed2((GLA_RANK, GLA_QK)), fixed2((1, GLA_QK)), fixed2((1, GLA_DV)), per_b,
                  fixed2((GLA_QK, GLA_V))],
        out_specs=[pl.BlockSpec((1, tt, GLA_V), lambda b, i: (b, i, 0)), per_b],
        out_shape=[jax.ShapeDtypeStruct((bsz, tp, GLA_V), jnp.float32),
                   jax.ShapeDtypeStruct((bsz, GLA_V, GLA_DK), jnp.float32)],
        scratch_shapes=[pltpu.VMEM((GLA_V, GLA_DK), jnp.float32), pltpu.VMEM((tt, GLA_QK), jnp.float32),
                        pltpu.VMEM((tt, GLA_QK), jnp.bfloat16)],
        compiler_params=pltpu.CompilerParams(dimension_semantics=("arbitrary", "arbitrary"),
                                             vmem_limit_bytes=VMEM_LIMIT),
        name="gla",
    )(h, h, h, h, h, w_gla_gate.astype(jnp.bfloat16), b_gla_gate.reshape(1, GLA_QK),
      gla_norm_g.reshape(1, GLA_DV), s0, jnp.asarray(expand, jnp.bfloat16))
    return o[:, :t_], s_t.reshape(bsz, GLA_HEADS, GLA_DV, GLA_DK).transpose(0, 1, 3, 2)


COL_NQ = 2 * GLA_QK + 2 * GLA_V
COL_NKV = COL_NQ + NSA_SIZES[0]
COL_TAIL = COL_NKV + NSA_SIZES[1]
TAIL_GATE = GLA_RANK
_ORIG = np.cumsum((0,) + GLA_SIZES + NSA_SIZES)
IN_AB_PERM = np.concatenate([np.arange(_ORIG[0], _ORIG[4]), np.arange(_ORIG[5], _ORIG[7]),
                             np.arange(_ORIG[4], _ORIG[5]), np.arange(_ORIG[7], _ORIG[8])])
SUBS = Q_BLK // CMP_STRIDE


def _nsa_prep_body(nq_ref, kv0_ref, kv1_ref, kv2_ref, tail_ref, rc_ref, ru_ref, rd_ref, pool_ref,
                   rows_ref, win_ref, kk_ref, vvt_ref, qr_ref, qo_ref, g_ref, pooled_ref):
    bf16 = jnp.bfloat16
    q0 = pl.program_id(1) * Q_BLK
    kv_w = NSA_KV_HEADS * HEAD_DIM

    def rope(x):
        reps = x.shape[1] // LANE
        wide = lambda r: jnp.concatenate([r[...]] * reps, axis=1) if reps > 1 else r[...]
        half = ROPE_DIM // 2
        return (x * wide(rc_ref) + pltpu.roll(x, half, axis=1) * wide(ru_ref)
                + pltpu.roll(x, x.shape[1] - half, axis=1) * wide(rd_ref))

    kv0, kv1, kv2 = kv0_ref[0], kv1_ref[0], kv2_ref[0]
    k_sel, v_sel = rope(kv1[:, :kv_w]), kv1[:, kv_w:]
    k_win, v_win = rope(kv2[:, :kv_w]), kv2[:, kv_w:]
    rows_ref[0] = jnp.concatenate([kv0, k_sel, v_sel], axis=1)
    win_ref[0] = jnp.concatenate([k_win, v_win], axis=1)
    blk_id = lax.shift_right_logical(q0 + lax.broadcasted_iota(jnp.int32, (Q_BLK, N_SELB), 0),
                                     int(math.log2(SEL_BLK)))
    onehot = jnp.where(lax.broadcasted_iota(jnp.int32, (Q_BLK, N_SELB), 1) == blk_id, 1.0, 0.0).astype(bf16)
    q = nq_ref[0] * (HEAD_DIM ** -0.5)
    q_rot = rope(q)
    gates_t = jax.nn.sigmoid(tail_ref[0]).T
    for h in range(NSA_KV_HEADS):
        hs = slice(h * HEAD_DIM, (h + 1) * HEAD_DIM)
        kk_ref[0, h] = jnp.concatenate([k_sel[:, hs].astype(bf16), k_win[:, hs].astype(bf16), onehot], axis=1)
        vvt_ref[0, h, 0] = jnp.concatenate([v_sel[:, hs], v_win[:, hs]], axis=1).T.astype(bf16)
        gw = NSA_GROUP * HEAD_DIM
        for src, dst in ((q, qr_ref), (q_rot, qo_ref)):
            t = src[:, h * gw:(h + 1) * gw].T
            dst[0, h, 0] = jnp.concatenate([t[g * HEAD_DIM:(g + 1) * HEAD_DIM] for g in range(NSA_GROUP)],
                                           axis=1).astype(bf16)
        base = TAIL_GATE + h * NSA_GROUP * 3
        g_ref[0, h, 0] = jnp.concatenate(
            [jnp.concatenate([gates_t[base + 3 * g + c:base + 3 * g + c + 1] for g in range(NSA_GROUP)], axis=1)
             for c in range(3)], axis=0)
    kc_in, vc_in = kv0[:, :kv_w].astype(bf16), kv0[:, kv_w:].astype(bf16)
    pooled_ref[0] = jnp.concatenate([_dot(pool_ref[0], kc_in), _dot(pool_ref[1], kc_in),
                                     _dot(pool_ref[2], vc_in), _dot(pool_ref[3], vc_in)], axis=1)


def _nsa_prep(h, pos, w_cmp_pool):
    bsz, t_, _ = h.shape
    nqb = t_ // Q_BLK
    bf16 = jnp.bfloat16
    half = ROPE_DIM // 2
    inv_freq = jnp.power(ROPE_THETA, -jnp.arange(half, dtype=jnp.float32) / half)
    ang = pos.astype(jnp.float32)[:, None] * inv_freq
    cos, sin = jnp.cos(ang), jnp.sin(ang)
    rest = HEAD_DIM - ROPE_DIM
    z8, zr = jnp.zeros((t_, half), jnp.float32), jnp.zeros((t_, rest), jnp.float32)
    two = lambda a: jnp.concatenate([a, a], axis=1)
    rc = two(jnp.concatenate([cos, cos, jnp.ones((t_, rest), jnp.float32)], axis=1))
    ru = two(jnp.concatenate([z8, sin, zr], axis=1))
    rd = two(jnp.concatenate([-sin, z8, zr], axis=1))
    pool = _pool_matrices(w_cmp_pool)
    kv_w = NSA_KV_HEADS * HEAD_DIM
    col = lambda width, off: pl.BlockSpec((1, Q_BLK, width), lambda b, i: (b, i, off // width))
    rows_t = pl.BlockSpec((Q_BLK, LANE), lambda b, i: (i, 0))
    head4 = lambda r, c: pl.BlockSpec((1, NSA_KV_HEADS, 1, r, c), lambda b, i: (b, 0, i, 0, 0))
    return pl.pallas_call(
        _nsa_prep_body,
        grid=(bsz, nqb),
        in_specs=[col(NSA_SIZES[0], COL_NQ), col(2 * kv_w, COL_NKV), col(2 * kv_w, COL_NKV + 2 * kv_w),
                  col(2 * kv_w, COL_NKV + 4 * kv_w), col(LANE, COL_TAIL), rows_t, rows_t, rows_t,
                  pl.BlockSpec((4, SUBS, Q_BLK), lambda b, i: (0, 0, 0))],
        out_specs=[pl.BlockSpec((1, Q_BLK, 4 * kv_w), lambda b, i: (b, i, 0)),
                   pl.BlockSpec((1, Q_BLK, 2 * kv_w), lambda b, i: (b, i, 0)),
                   pl.BlockSpec((1, NSA_KV_HEADS, Q_BLK, KK_W), lambda b, i: (b, 0, i, 0)),
                   head4(2 * HEAD_DIM, Q_BLK), head4(HEAD_DIM, NSA_ROWS), head4(HEAD_DIM, NSA_ROWS),
                   head4(3, NSA_ROWS),
                   pl.BlockSpec((1, SUBS, 4 * kv_w), lambda b, i: (b, i, 0))],
        out_shape=[jax.ShapeDtypeStruct((bsz, t_, 4 * kv_w), jnp.float32),
                   jax.ShapeDtypeStruct((bsz, t_, 2 * kv_w), jnp.float32),
                   jax.ShapeDtypeStruct((bsz, NSA_KV_HEADS, t_, KK_W), bf16),
                   jax.ShapeDtypeStruct((bsz, NSA_KV_HEADS, nqb, 2 * HEAD_DIM, Q_BLK), bf16),
                   jax.ShapeDtypeStruct((bsz, NSA_KV_HEADS, nqb, HEAD_DIM, NSA_ROWS), bf16),
                   jax.ShapeDtypeStruct((bsz, NSA_KV_HEADS, nqb, HEAD_DIM, NSA_ROWS), bf16),
                   jax.ShapeDtypeStruct((bsz, NSA_KV_HEADS, nqb, 3, NSA_ROWS), jnp.float32),
                   jax.ShapeDtypeStruct((bsz, t_ // CMP_STRIDE, 4 * kv_w), jnp.float32)],
        compiler_params=pltpu.CompilerParams(dimension_semantics=("arbitrary", "arbitrary")),
        name="nsa_prep",
    )(h, h, h, h, h, rc, ru, rd, pool)


PAGE_GROUP = 32
DEC_KEYS = PAGE_GROUP * PAGE_SIZE
POOL_ROWS = 2048
NEW_PAD = 8
KV_W = NSA_KV_HEADS * HEAD_DIM


def _dec_pool_body(pt_ref, *refs):
    page_refs, pool_ref, out_ref = refs[:PAGE_GROUP], refs[PAGE_GROUP], refs[PAGE_GROUP + 1]
    bf16 = jnp.bfloat16
    pages = [pr[0] for pr in page_refs]
    per = POOL_ROWS // PAGE_SIZE
    cols = []
    for g0 in range(0, PAGE_GROUP, per):
        kc_t = jnp.concatenate([p[:KV_W] for p in pages[g0:g0 + per]], axis=1).astype(bf16)
        vc_t = jnp.concatenate([p[KV_W:] for p in pages[g0:g0 + per]], axis=1).astype(bf16)
        cols.append(jnp.concatenate([_dot(kc_t, pool_ref[0]), _dot(kc_t, pool_ref[1]),
                                     _dot(vc_t, pool_ref[2]), _dot(vc_t, pool_ref[3])], axis=0))
    out_ref[0] = jnp.concatenate(cols, axis=1)


def _page_specs(n_pages, col_blk):
    def spec(i):
        return pl.BlockSpec((1, 2 * KV_W, PAGE_SIZE),
                            lambda b, j, pt: (pt[b * n_pages + j * PAGE_GROUP + i], col_blk, 0))
    return [spec(i) for i in range(PAGE_GROUP)]


def _dec_pool(cache, page_table, pool):
    bsz, n_pages = page_table.shape
    grid_spec = pltpu.PrefetchScalarGridSpec(
        num_scalar_prefetch=1, grid=(bsz, n_pages // PAGE_GROUP),
        in_specs=_page_specs(n_pages, 0) + [pl.BlockSpec(pool.shape, lambda b, j, pt: (0, 0, 0))],
        out_specs=pl.BlockSpec((1, 4 * KV_W, PAGE_GROUP * SUBS), lambda b, j, pt: (b, 0, j)))
    return pl.pallas_call(
        _dec_pool_body, grid_spec=grid_spec,
        out_shape=jax.ShapeDtypeStruct((bsz, 4 * KV_W, n_pages * SUBS), jnp.float32),
        compiler_params=pltpu.CompilerParams(dimension_semantics=("arbitrary", "arbitrary")),
        name="nsa_dec_pool",
    )(page_table.reshape(-1), *([cache] * PAGE_GROUP), pool)


def _dec_select_body(qr_ref, kct_ref, vc_ref, band_ref, oc_ref, selb_ref, *, qpos0, n_q, n_pick, n_blk):
    f32, bf16 = jnp.float32, jnp.bfloat16
    n_cmp = kct_ref.shape[3]
    rows = NSA_GROUP * n_q
    for sq, h in [(a, b) for a in range(qr_ref.shape[0]) for b in range(NSA_KV_HEADS)]:
        s_c = _dot(qr_ref[sq, h], kct_ref[sq, h])
        n_idx = lax.broadcasted_iota(jnp.int32, (rows, n_cmp), 1)
        qpos = qpos0 + (lax.broadcasted_iota(jnp.int32, (rows, n_cmp), 0) % n_q)
        cmask = (n_idx * CMP_STRIDE + (CMP_BLK - 1)) <= qpos
        s_c = jnp.where(cmask, s_c, MASKED)
        p_c = jnp.where(cmask, jnp.exp(s_c - jnp.max(s_c, axis=1, keepdims=True)), 0.0)
        p_c = p_c / jnp.maximum(jnp.sum(p_c, axis=1, keepdims=True), 1e-30)
        oc_ref[sq, h] = _dot(p_c.astype(bf16), vc_ref[sq, h])
        imp = p_c[0:n_q]
        for g in range(1, NSA_GROUP):
            imp = imp + p_c[g * n_q:(g + 1) * n_q]
        imp_s = jnp.zeros((n_q, N_SELB), f32)
        rem = imp
        for _ in range(3):
            part = rem.astype(bf16)
            imp_s = imp_s + _dot(part, band_ref[...])
            rem = rem - part.astype(f32)
        blk = lax.broadcasted_iota(jnp.int32, (n_q, N_SELB), 1)
        qpos_s = qpos0 + lax.broadcasted_iota(jnp.int32, (n_q, N_SELB), 0)
        cur = lax.shift_right_logical(qpos_s, int(math.log2(SEL_BLK)))
        valid = (blk * SEL_BLK <= qpos_s) & (blk < n_blk)
        forced = (blk == 0) | (blk == cur) | (blk == cur - 1)
        score = jnp.where(valid, imp_s + jnp.where(forced, FORCE_BONUS, 0.0), -1e30)
        picked = jnp.zeros((n_q, N_SELB), f32)
        for _ in range(n_pick):
            best = jnp.max(score, axis=1, keepdims=True)
            first = jnp.min(jnp.where(score == best, blk, N_SELB), axis=1, keepdims=True)
            hit = blk == first
            picked = jnp.where(hit, 1.0, picked)
            score = jnp.where(hit, -3e38, score)
        selb_ref[sq, h] = (jnp.where(valid, picked, 0.0) - 1.0) * (-MASKED)


def _dec_select(qr, kct, vc, n_q, qpos0, n_pick, n_blk):
    bsz = qr.shape[0]
    rows = NSA_GROUP * n_q
    n_cmp = kct.shape[3]
    ratio = SEL_BLK // CMP_STRIDE
    c_idx, j_idx = np.arange(n_cmp)[:, None], np.arange(N_SELB)[None, :]
    band = jnp.asarray(((c_idx >= ratio * j_idx - 1) & (c_idx <= ratio * j_idx + ratio - 1)), jnp.bfloat16)
    per_step = next(c for c in (4, 2, 1) if bsz % c == 0)
    per_b = lambda *tail: pl.BlockSpec((per_step, NSA_KV_HEADS) + tail, lambda b: (b, 0, 0, 0))
    return pl.pallas_call(
        functools.partial(_dec_select_body, qpos0=qpos0, n_q=n_q, n_pick=n_pick, n_blk=n_blk),
        grid=(bsz // per_step,),
        in_specs=[per_b(rows, HEAD_DIM), per_b(HEAD_DIM, n_cmp), per_b(n_cmp, HEAD_DIM),
                  pl.BlockSpec((n_cmp, N_SELB), lambda b: (0, 0))],
        out_specs=[per_b(rows, HEAD_DIM), per_b(n_q, N_SELB)],
        out_shape=[jax.ShapeDtypeStruct((bsz, NSA_KV_HEADS, rows, HEAD_DIM), jnp.float32),
                   jax.ShapeDtypeStruct((bsz, NSA_KV_HEADS, n_q, N_SELB), jnp.float32)],
        compiler_params=pltpu.CompilerParams(dimension_semantics=("arbitrary",)),
        name="nsa_dec_select",
    )(qr, kct, vc, band)


def _dec_attend_body(pt_ref, *refs, qpos0, n_q, past):
    page_refs = refs[:PAGE_GROUP]
    (qs_ref, qw_ref, knew_ref, vnew_ref, wbuf_ref, wnew_ref, oc_ref, g_ref,
     o_ref, m_ref, l_ref, acc_ref) = refs[PAGE_GROUP:]
    f32, bf16 = jnp.float32, jnp.bfloat16
    j = pl.program_id(1)
    n_rows = qs_ref.shape[1]

    @pl.when(j == 0)
    def _():
        m_ref[...] = jnp.full(m_ref.shape, MASKED, f32)
        l_ref[...] = jnp.zeros(l_ref.shape, f32)
        acc_ref[...] = jnp.zeros(acc_ref.shape, f32)

    def online(s, weigh):
        m_old = m_ref[...]
        m_new = jnp.maximum(m_old, jnp.max(s, axis=1, keepdims=True))
        alpha = jnp.exp(m_old - m_new)
        p = jnp.exp(s - m_new)
        l_ref[...] = alpha * l_ref[...] + jnp.sum(p, axis=1, keepdims=True)
        acc_ref[...] = alpha * acc_ref[...] + weigh(p.astype(bf16))
        m_ref[...] = m_new

    qs = qs_ref[0]
    pages = [pr[0] for pr in page_refs]
    keys_t = jnp.concatenate([p[:KV_W] for p in pages], axis=1).astype(bf16)
    vals_t = jnp.concatenate([p[KV_W:] for p in pages], axis=1).astype(bf16)
    blk_id = j * (DEC_KEYS // SEL_BLK) + lax.shift_right_logical(
        lax.broadcasted_iota(jnp.int32, (N_SELB, DEC_KEYS), 1), int(math.log2(SEL_BLK)))
    onehot_t = jnp.where(lax.broadcasted_iota(jnp.int32, (N_SELB, DEC_KEYS), 0) == blk_id, 1.0, 0.0).astype(bf16)
    online(_dot(qs, jnp.concatenate([keys_t, onehot_t], axis=0)), lambda p: _dot_nt(p, vals_t))

    @pl.when(j == pl.num_programs(1) - 1)
    def _():
        row_q = qpos0 + (lax.broadcasted_iota(jnp.int32, (n_rows, 1), 0) % n_q)
        qh = qw_ref[0]
        new_pos = past + lax.broadcasted_iota(jnp.int32, (n_rows, NEW_PAD), 1)
        new_ok = (new_pos <= row_q) & (new_pos < past + n_q)
        s_new = jnp.where(new_ok, _dot_nt(qh, knew_ref[0]), MASKED)
        online(s_new, lambda p: _dot(p, vnew_ref[0]))
        o_s = acc_ref[...] / l_ref[...]
        wbuf_t = wbuf_ref[0]
        wnew = wnew_ref[0]
        n_buf = wbuf_t.shape[1]
        s_b = _dot(qh, wbuf_t[:KV_W].astype(bf16))
        pos_b = (past - n_buf) + lax.broadcasted_iota(jnp.int32, (n_rows, n_buf), 1)
        s_b = jnp.where((pos_b > row_q - WINDOW) & (pos_b >= 0), s_b, MASKED)
        s_n = jnp.where(new_ok, _dot_nt(qh, wnew[:, :KV_W].astype(bf16)), MASKED)
        m_w = jnp.maximum(jnp.max(s_b, axis=1, keepdims=True), jnp.max(s_n, axis=1, keepdims=True))
        p_b, p_n = jnp.exp(s_b - m_w), jnp.exp(s_n - m_w)
        l_w = jnp.sum(p_b, axis=1, keepdims=True) + jnp.sum(p_n, axis=1, keepdims=True)
        o_w = (_dot_nt(p_b.astype(bf16), wbuf_t[KV_W:].astype(bf16))
               + _dot(p_n.astype(bf16), wnew[:, KV_W:].astype(bf16))) / l_w
        half = n_rows // NSA_KV_HEADS
        own = lambda a: jnp.concatenate([a[h * half:(h + 1) * half, h * HEAD_DIM:(h + 1) * HEAD_DIM]
                                         for h in range(NSA_KV_HEADS)], axis=0)
        g = g_ref[0]
        o_ref[0] = g[:, 0:1] * oc_ref[0] + g[:, 1:2] * own(o_s) + g[:, 2:3] * own(o_w)


def _dec_attend(cache, page_table, qs, qw, knew, vnew, wbuf, wnew, o_c, gates, n_q, qpos0):
    bsz, n_pages = page_table.shape
    n_rows = qs.shape[1]
    per_b = lambda *tail: pl.BlockSpec((1,) + tail, lambda b, j, pt: (b, 0, 0))
    grid_spec = pltpu.PrefetchScalarGridSpec(
        num_scalar_prefetch=1, grid=(bsz, n_pages // PAGE_GROUP),
        in_specs=_page_specs(n_pages, 1) + [
            per_b(n_rows, KV_W + N_SELB), per_b(n_rows, KV_W), per_b(NEW_PAD, KV_W), per_b(NEW_PAD, KV_W),
            per_b(2 * KV_W, wbuf.shape[2]), per_b(NEW_PAD, 2 * KV_W), per_b(n_rows, HEAD_DIM), per_b(n_rows, 3)],
        out_specs=per_b(n_rows, HEAD_DIM),
        scratch_shapes=[pltpu.VMEM((n_rows, 1), jnp.float32), pltpu.VMEM((n_rows, 1), jnp.float32),
                        pltpu.VMEM((n_rows, KV_W), jnp.float32)])
    return pl.pallas_call(
        functools.partial(_dec_attend_body, qpos0=qpos0, n_q=n_q, past=n_pages * PAGE_SIZE),
        grid_spec=grid_spec,
        out_shape=jax.ShapeDtypeStruct((bsz, n_rows, HEAD_DIM), jnp.float32),
        compiler_params=pltpu.CompilerParams(dimension_semantics=("arbitrary", "arbitrary")),
        name="nsa_dec_attend",
    )(page_table.reshape(-1), *([cache] * PAGE_GROUP), qs, qw, knew, vnew, wbuf, wnew, o_c, gates)


def _pool_matrices(w_cmp_pool, rows=Q_BLK):
    subs = rows // CMP_STRIDE
    sub = np.arange(rows) // CMP_STRIDE == np.arange(subs)[:, None]
    w_rep = jnp.tile(w_cmp_pool.reshape(2, 2, CMP_STRIDE), (1, 1, subs))
    return jnp.where(sub[None, None], w_rep[:, :, None, :], 0.0).reshape(4, subs, rows).astype(jnp.bfloat16)


def _nsa_decode(q_raw, q_rot, gates, rows_full, rows_win, cache, page_table, win_buf, w_cmp_pool, past):
    bsz, n_q = q_raw.shape[:2]
    bf16 = jnp.bfloat16
    n_blk = past // SEL_BLK
    assert past % DEC_KEYS == 0 and n_blk <= N_SELB and n_q <= NEW_PAD
    scale = HEAD_DIM ** -0.5
    cache2 = cache.transpose(0, 2, 3, 4, 1).reshape(cache.shape[0], 4 * KV_W, PAGE_SIZE)
    pooled_t = _dec_pool(cache2, page_table, _pool_matrices(w_cmp_pool, POOL_ROWS).transpose(0, 2, 1))
    pooled_t = pooled_t.reshape(bsz, 4, NSA_KV_HEADS, HEAD_DIM, -1)
    last = ((0, 0), (0, 0), (0, 0), (0, 1))
    kct = jnp.pad(pooled_t[:, 0, ..., :-1] + pooled_t[:, 1, ..., 1:], last)
    vc_p = jnp.pad(pooled_t[:, 2, ..., :-1] + pooled_t[:, 3, ..., 1:], last).transpose(0, 1, 3, 2)
    rows_of = lambda a: a.transpose(0, 2, 3, 1, 4).reshape(bsz, NSA_KV_HEADS, NSA_GROUP * n_q, a.shape[-1])
    qr = rows_of((q_raw * scale).astype(bf16))
    n_pick = min(SEL_TOPN, n_blk + 1) - 1
    o_c, selb = _dec_select(qr, kct.astype(bf16), vc_p.astype(bf16), n_q, past, n_pick, n_blk)
    qo = rows_of((q_rot * scale).astype(bf16))
    zero = jnp.zeros_like(qo[:, 0])
    qw = jnp.concatenate([jnp.concatenate([qo[:, 0], zero], -1), jnp.concatenate([zero, qo[:, 1]], -1)], axis=1)
    bias = jnp.tile(selb, (1, 1, NSA_GROUP, 1)).reshape(bsz, -1, N_SELB).astype(bf16)
    qs = jnp.concatenate([qw, bias], axis=-1)
    pad_new = lambda a: jnp.pad(a.reshape(bsz, n_q, -1), ((0, 0), (0, NEW_PAD - n_q), (0, 0)))
    knew = pad_new(rows_full[:, :, 2]).astype(bf16)
    vnew = pad_new(rows_full[:, :, 3]).astype(bf16)
    wnew = pad_new(rows_win)
    wbuf = win_buf.transpose(0, 2, 3, 4, 1).reshape(bsz, 2 * KV_W, win_buf.shape[1])
    gt = rows_of(gates).reshape(bsz, -1, 3)
    o = _dec_attend(cache2, page_table, qs, qw, knew, vnew, wbuf, wnew,
                    o_c.reshape(bsz, -1, HEAD_DIM), gt, n_q, past)
    o = o.reshape(bsz, NSA_KV_HEADS, NSA_GROUP, n_q, HEAD_DIM).transpose(0, 3, 1, 2, 4)
    return o.reshape(bsz, n_q, NSA_HEADS * HEAD_DIM)


def _ab_mixer(x, pos, w_in, w_gla_gate, b_gla_gate, gla_norm_g, w_cmp_pool, w_out,
              gla_state, nsa_cache, page_table, win_buf):
    bsz, t_, _ = x.shape
    h_in = _mm(x.reshape(bsz * t_, -1), w_in[:, IN_AB_PERM], keep_pad=True).reshape(bsz, t_, -1)
    o_a, s_a = _gla(h_in, w_gla_gate, b_gla_gate, gla_norm_g, gla_state)
    kv_w = NSA_KV_HEADS * HEAD_DIM
    if nsa_cache is None:
        rows2, win2, kk, vvt, qr, qo, gt, pooled = _nsa_prep(h_in, pos, w_cmp_pool)
        pooled = pooled.reshape(bsz, t_ // CMP_STRIDE, 4, NSA_KV_HEADS, HEAD_DIM)
        kc = pooled[:, :-1, 0] + pooled[:, 1:, 1]
        vc = pooled[:, :-1, 2] + pooled[:, 1:, 3]
        kc_p = jnp.pad(kc, ((0, 0), (0, 1), (0, 0), (0, 0))).transpose(0, 2, 1, 3).astype(jnp.bfloat16)
        vct = jnp.pad(vc, ((0, 0), (0, 1), (0, 0), (0, 0))).transpose(0, 2, 3, 1).astype(jnp.bfloat16)
        o_b = _nsa_prompt(qr, qo, gt, kc_p, vct, kk, vvt)
        rows_full = rows2.reshape(bsz, t_, 4, NSA_KV_HEADS, HEAD_DIM)
        new_win = win2[:, -min(WINDOW, t_):].reshape(bsz, -1, 2, NSA_KV_HEADS, HEAD_DIM)
    else:
        nq = h_in[..., COL_NQ:COL_NKV]
        nkv = h_in[..., COL_NKV:COL_TAIL]
        ngate = h_in[..., COL_TAIL + TAIL_GATE:COL_TAIL + TAIL_GATE + NSA_SIZES[2]]
        q_raw = nq.reshape(bsz, t_, NSA_KV_HEADS, NSA_GROUP, HEAD_DIM)
        q_rot = _partial_rope(q_raw, pos)
        kv = nkv.reshape(bsz, t_, 6, NSA_KV_HEADS, HEAD_DIM)
        k_sel = _partial_rope(kv[:, :, 2], pos)
        k_win = _partial_rope(kv[:, :, 4], pos)
        rows_full = jnp.stack([kv[:, :, 0], kv[:, :, 1], k_sel, kv[:, :, 3]], axis=2)
        rows_win = jnp.stack([k_win, kv[:, :, 5]], axis=2)
        gates = jax.nn.sigmoid(ngate).reshape(bsz, t_, NSA_KV_HEADS, NSA_GROUP, 3)
        past_len = page_table.shape[1] * PAGE_SIZE
        o_b = _nsa_decode(q_raw, q_rot, gates, rows_full, rows_win, nsa_cache, page_table, win_buf,
                          w_cmp_pool, past_len)
        w_buf = win_buf.shape[1]
        kw = jnp.concatenate([win_buf, rows_win], axis=1)
        new_win = kw[:, -w_buf:]
    y = _mm_pair(o_a.reshape(bsz * t_, -1), o_b.reshape(bsz * t_, -1), w_out).reshape(bsz, t_, -1)
    return y, s_a, rows_full, new_win


CONV_HALO = 32
CONV_LEAD = CONV_HALO - (CONV_W - 1)


def _conv_body(x_ref, buf0_ref, w1_ref, b1_ref, wdw_ref, bdw_ref, g_ref, b_ref, w2_ref, b2_ref,
               o_ref, tail_ref, ext_ref, z_ref, *, t_last):
    bf16 = jnp.bfloat16
    tt = x_ref.shape[1]
    i = pl.program_id(1)

    @pl.when(i == 0)
    def _():
        ext_ref[0:CONV_HALO, :] = buf0_ref[0]
        ext_ref[CONV_HALO + tt:CONV_HALO + tt + SUBLANES, :] = jnp.zeros((SUBLANES, D_CONV), jnp.float32)

    h = _dot(x_ref[0].astype(bf16), w1_ref[...]) + b1_ref[...]
    ext_ref[CONV_HALO:CONV_HALO + tt, :] = h[:, :D_CONV] * jax.nn.sigmoid(h[:, D_CONV:])
    c = jnp.zeros((tt, D_CONV), jnp.float32) + bdw_ref[...]
    for r in range(SUBLANES):
        z = None
        for a in range(CONV_HALO // SUBLANES + 1):
            k = SUBLANES * a + r - CONV_LEAD
            if 0 <= k < CONV_W:
                term = ext_ref[SUBLANES * a:SUBLANES * a + tt + SUBLANES, :] * wdw_ref[k:k + 1, :]
                z = term if z is None else z + term
        if r == 0:
            c = c + z[:tt]
        else:
            z_ref[...] = z
            c = c + z_ref[pl.ds(r, tt), :]
    c = _ln_rows(c, g_ref[...], b_ref[...])
    c = c * jax.nn.sigmoid(c)
    o_ref[0] = _dot(c.astype(bf16), w2_ref[...]) + b2_ref[...]
    tail_ref[0] = ext_ref[t_last:t_last + CONV_HALO, :]
    ext_ref[0:CONV_HALO, :] = ext_ref[tt:tt + CONV_HALO, :]


def _conv_module(x, conv_buf, w_pw1, b_pw1, w_dw, b_dw, ln_g, ln_b, w_pw2, b_pw2):
    bsz, t_, d = x.shape
    bf16 = jnp.bfloat16
    tp = -(-t_ // 8) * 8
    tt = min(tp, 256)
    n_t = tp // tt
    if tp != t_:
        x = jnp.pad(x, ((0, 0), (0, tp - t_), (0, 0)))
    if conv_buf is None:
        buf0 = jnp.zeros((bsz, CONV_HALO, D_CONV), jnp.float32)
    else:
        buf0 = jnp.pad(conv_buf, ((0, 0), (CONV_LEAD, 0), (0, 0)))
    fixed = lambda shape: pl.BlockSpec(shape, lambda b, i: (0,) * len(shape))
    per_b = pl.BlockSpec((1, CONV_HALO, D_CONV), lambda b, i: (b, 0, 0))
    out, tail = pl.pallas_call(
        functools.partial(_conv_body, t_last=t_ - (n_t - 1) * tt),
        grid=(bsz, n_t),
        in_specs=[pl.BlockSpec((1, tt, d), lambda b, i: (b, i, 0)), per_b,
                  fixed((d, 2 * D_CONV)), fixed((1, 2 * D_CONV)), fixed((CONV_HALO, D_CONV)), fixed((1, D_CONV)),
                  fixed((1, D_CONV)), fixed((1, D_CONV)), fixed((D_CONV, d)), fixed((1, d))],
        out_specs=[pl.BlockSpec((1, tt, d), lambda b, i: (b, i, 0)), per_b],
        out_shape=[jax.ShapeDtypeStruct((bsz, tp, d), jnp.float32),
                   jax.ShapeDtypeStruct((bsz, CONV_HALO, D_CONV), jnp.float32)],
        scratch_shapes=[pltpu.VMEM((CONV_HALO + tt + SUBLANES, D_CONV), jnp.float32),
                        pltpu.VMEM((tt + SUBLANES, D_CONV), jnp.float32)],
        compiler_params=pltpu.CompilerParams(dimension_semantics=("arbitrary", "arbitrary"),
                                             vmem_limit_bytes=VMEM_LIMIT),
        name="conv_module",
    )(x, buf0, w_pw1.astype(bf16), b_pw1.reshape(1, -1), jnp.pad(w_dw, ((0, CONV_HALO - CONV_W), (0, 0))),
      b_dw.reshape(1, -1), ln_g.reshape(1, -1), ln_b.reshape(1, -1), w_pw2.astype(bf16), b_pw2.reshape(1, -1))
    return out[:, :t_], tail[:, CONV_LEAD:]


PACK_W = 256
SC_WINDOW = 128
SC_TILES = 32
MOE_ALIGN = SC_WINDOW * SC_TILES // (2 * TOP_K)


def _pack_rows(y):
    out = []
    for h in range(2):
        lo = lax.bitcast_convert_type(y[:, 2 * h * PACK_W:(2 * h + 1) * PACK_W].astype(jnp.bfloat16)
                                      .astype(jnp.float32), jnp.uint32)
        hi = lax.bitcast_convert_type(y[:, (2 * h + 1) * PACK_W:(2 * h + 2) * PACK_W].astype(jnp.bfloat16)
                                      .astype(jnp.float32), jnp.uint32)
        out.append(lax.bitcast_convert_type((lo >> 16) | hi, jnp.int32))
    return out


def _unpack_words(w):
    u = lax.bitcast_convert_type(w, jnp.uint32)
    lo = lax.bitcast_convert_type(u << 16, jnp.float32)
    hi = lax.bitcast_convert_type(u & jnp.uint32(0xFFFF0000), jnp.float32)
    return lo, hi


def _gather_rows(src, idx):
    n = idx.shape[0]
    if n % (SC_WINDOW * SC_TILES) != 0:
        return jnp.take(src, idx, axis=0)
    mesh = plsc.VectorSubcoreMesh(core_axis_name="core", subcore_axis_name="subcore")

    @pl.kernel(out_type=jax.ShapeDtypeStruct((n, src.shape[1]), src.dtype), mesh=mesh)
    def gather_kernel(src_hbm, idx_hbm, out_hbm):
        def step(idx_vmem, out_vmem):
            pltpu.sync_copy(src_hbm.at[idx_vmem.at[0]], out_vmem)

        pltpu.emit_pipeline(
            step, grid=(n // SC_WINDOW,),
            in_specs=[pl.BlockSpec((1, SC_WINDOW), index_map=lambda i: (0, i))],
            out_specs=[pl.BlockSpec((SC_WINDOW, src.shape[1]), index_map=lambda i: (i, 0))],
            core_axis_name=("core", "subcore"),
            dimension_semantics=(pltpu.PARALLEL,),
        )(idx_hbm, out_hbm)

    return gather_kernel(src, idx.reshape(1, n))


def _scatter_rows(src, idx, n_out):
    n = idx.shape[0]
    m = src.shape[0] // 2
    reps = n // (2 * m)
    if n % (SC_WINDOW * SC_TILES) != 0 or m % SC_WINDOW != 0:
        rows = jnp.arange(n, dtype=jnp.int32)
        src_row = (rows // (reps * m)) * m + rows % m
        return jnp.zeros((n_out, src.shape[1]), src.dtype).at[idx].set(jnp.take(src, src_row, axis=0))
    tiles = m // SC_WINDOW
    mesh = plsc.VectorSubcoreMesh(core_axis_name="core", subcore_axis_name="subcore")

    @pl.kernel(out_type=jax.ShapeDtypeStruct((n_out, src.shape[1]), src.dtype), mesh=mesh, scratch_types=[])
    def scatter_kernel(src_hbm, idx_hbm, out_hbm):
        def step(src_vmem, idx_vmem):
            pltpu.sync_copy(src_vmem, out_hbm.at[idx_vmem.at[0]])

        pltpu.emit_pipeline(
            step, grid=(n // SC_WINDOW,),
            in_specs=[pl.BlockSpec((SC_WINDOW, src.shape[1]),
                                   index_map=lambda i: ((i // (reps * tiles)) * tiles + i % tiles, 0)),
                      pl.BlockSpec((1, SC_WINDOW), index_map=lambda i: (0, i))],
            out_specs=[],
            core_axis_name=("core", "subcore"),
            dimension_semantics=(pltpu.PARALLEL,),
        )(src_hbm, idx_hbm)

    return scatter_kernel(src, idx.reshape(1, n))


PER_GROUP = N_EXPERTS // N_GROUPS
PICKED = -3e38


def _ln_rows(v, g, b):
    mu = jnp.mean(v, axis=-1, keepdims=True)
    c = v - mu
    var = jnp.mean(c * c, axis=-1, keepdims=True)
    return c * lax.rsqrt(var + LN_EPS) * g + b


def _first_max(v, ids, axes, sentinel):
    best = v
    for a in axes:
        best = jnp.max(best, axis=a, keepdims=True)
    first = jnp.where(v == best, ids, sentinel)
    for a in axes:
        first = jnp.min(first, axis=a, keepdims=True)
    return best, first


def _sum_axes(v, axes):
    for a in axes:
        v = jnp.sum(v, axis=a, keepdims=True)
    return v


def _moe_pre_body(x_ref, mix_ref, g_ref, b_ref, wr_ref, br_ref,
                  x1_ref, xp_ref, eidx_ref, gate_ref, rank_ref, cnt_ref, run_ref):
    f32, bf16 = jnp.float32, jnp.bfloat16
    tm = x_ref.shape[0]

    @pl.when(pl.program_id(0) == 0)
    def _():
        run_ref[...] = jnp.zeros(run_ref.shape, f32)

    x1 = _ln_rows(ALPHA * x_ref[...] + mix_ref[...], g_ref[...], b_ref[...])
    x1_ref[...] = x1
    x1b = x1.astype(bf16)
    xp_ref[0], xp_ref[1] = _pack_rows(x1)

    s = jax.nn.sigmoid(_dot_nt(wr_ref[...], x1b)).reshape(N_GROUPS, PER_GROUP, tm)
    sb = s + br_ref[...].reshape(N_GROUPS, PER_GROUP, 1)
    shape3 = (N_GROUPS, PER_GROUP, tm)
    pid = lax.broadcasted_iota(jnp.int32, shape3, 1)
    gid = lax.broadcasted_iota(jnp.int32, (N_GROUPS, 1, tm), 0)
    eid = lax.broadcasted_iota(jnp.int32, shape3, 0) * PER_GROUP + pid
    top1, i1 = _first_max(sb, pid, (1,), PER_GROUP)
    top2 = jnp.max(jnp.where(pid == i1, PICKED, sb), axis=1, keepdims=True)
    gscore = top1 + top2
    gsel = jnp.zeros((N_GROUPS, 1, tm), f32)
    for _ in range(TOPK_GROUPS):
        _, first = _first_max(gscore, gid, (0,), N_GROUPS)
        hit = gid == first
        gsel = jnp.where(hit, 1.0, gsel)
        gscore = jnp.where(hit, PICKED, gscore)
    cand = jnp.where(gsel > 0.0, sb, -1e30)
    firsts, gates = [], []
    picked = jnp.zeros(shape3, f32)
    for _ in range(TOP_K):
        _, first = _first_max(cand, eid, (0, 1), N_EXPERTS)
        hit = eid == first
        firsts.append(first)
        gates.append(_sum_axes(jnp.where(hit, s, 0.0), (0, 1)))
        picked = jnp.where(hit, 1.0, picked)
        cand = jnp.where(hit, PICKED, cand)
    gsum = gates[0]
    for gk in gates[1:]:
        gsum = gsum + gk
    earlier = (lax.broadcasted_iota(jnp.int32, (tm, tm), 0) < lax.broadcasted_iota(jnp.int32, (tm, tm), 1))
    picked2 = picked.reshape(N_EXPERTS, tm)
    rank = run_ref[...] + _dot(picked2.astype(bf16), jnp.where(earlier, 1.0, 0.0).astype(bf16))
    run_new = run_ref[...] + jnp.sum(picked2, axis=1, keepdims=True)
    run_ref[...] = run_new
    cnt_ref[...] = jnp.broadcast_to(run_new, cnt_ref.shape)
    rank3 = rank.reshape(shape3)
    for k in range(TOP_K):
        hit = eid == firsts[k]
        eidx_ref[k:k + 1, :] = firsts[k].reshape(1, tm)
        gate_ref[k:k + 1, :] = (gates[k] / gsum * ROUTE_SCALE).reshape(1, tm)
        rank_ref[k:k + 1, :] = _sum_axes(jnp.where(hit, rank3, 0.0), (0, 1)).reshape(1, tm).astype(jnp.int32)


def _moe_pre(x, mix, g, b, w_router, b_router):
    m, d = x.shape
    bf16 = jnp.bfloat16
    tm = min(m, 512)
    row = lambda i: (i, 0)
    col = lambda i: (0, i)
    fixed = lambda i: (0, 0)
    return pl.pallas_call(
        _moe_pre_body,
        grid=(m // tm,),
        in_specs=[pl.BlockSpec((tm, d), row), pl.BlockSpec((tm, d), row),
                  pl.BlockSpec((1, d), fixed), pl.BlockSpec((1, d), fixed),
                  pl.BlockSpec((N_EXPERTS, d), fixed), pl.BlockSpec((N_EXPERTS, 1), fixed)],
        out_specs=[pl.BlockSpec((tm, d), row), pl.BlockSpec((2, tm, PACK_W), lambda i: (0, i, 0)),
                   pl.BlockSpec((TOP_K, tm), col), pl.BlockSpec((TOP_K, tm), col), pl.BlockSpec((TOP_K, tm), col),
                   pl.BlockSpec((N_EXPERTS, LANE), fixed)],
        out_shape=[jax.ShapeDtypeStruct((m, d), jnp.float32), jax.ShapeDtypeStruct((2, m, PACK_W), jnp.int32),
                   jax.ShapeDtypeStruct((TOP_K, m), jnp.int32), jax.ShapeDtypeStruct((TOP_K, m), jnp.float32),
                   jax.ShapeDtypeStruct((TOP_K, m), jnp.int32),
                   jax.ShapeDtypeStruct((N_EXPERTS, LANE), jnp.float32)],
        scratch_shapes=[pltpu.VMEM((N_EXPERTS, 1), jnp.float32)],
        compiler_params=pltpu.CompilerParams(dimension_semantics=("arbitrary",),
                                             vmem_limit_bytes=VMEM_LIMIT),
        name="moe_pre",
    )(x, mix, g.reshape(1, d), b.reshape(1, d), w_router.T.astype(bf16), b_router.reshape(N_EXPERTS, 1))


def _moe_expert_body(exp_ref, first_ref, rows_ref, xs_ref, wgu_ref, wdn_ref, y_ref, wgu_bf, wdn_bf):
    i = pl.program_id(0)
    bf16 = jnp.bfloat16

    @pl.when(first_ref[i] == 1)
    def _():
        wgu_bf[...] = wgu_ref[0, 0].astype(bf16)
        wdn_bf[...] = wdn_ref[0, 0].astype(bf16)

    @pl.when(rows_ref[i] > 0)
    def _():
        live = lax.broadcasted_iota(jnp.int32, (xs_ref.shape[1], 1), 0) < rows_ref[i]
        h = None
        for hw in range(2):
            for q, xq in enumerate(_unpack_words(xs_ref[hw])):
                r0 = (2 * hw + q) * PACK_W
                part = _dot(jnp.where(live, xq, 0.0).astype(bf16), wgu_bf[r0:r0 + PACK_W, :])
                h = part if h is None else h + part
        d_e = h.shape[1] // 2
        act = (jax.nn.silu(h[:, :d_e]) * h[:, d_e:]).astype(bf16)
        y_ref[0], y_ref[1] = _pack_rows(_dot(act, wdn_bf[...]))

    @pl.when(rows_ref[i] == 0)
    def _():
        y_ref[...] = jnp.zeros(y_ref.shape, y_ref.dtype)


def _moe_experts(xs, blk_exp, blk_first, blk_rows, w_exp_gu, w_exp_down, layer, bm):
    n_slots = xs.shape[1]
    d = w_exp_gu.shape[2]
    n_blk = n_slots // bm
    d_e2 = w_exp_gu.shape[3]
    words = lambda i, e, f, a: (0, i, 0)
    grid_spec = pltpu.PrefetchScalarGridSpec(
        num_scalar_prefetch=3,
        grid=(n_blk,),
        in_specs=[pl.BlockSpec((2, bm, PACK_W), words),
                  pl.BlockSpec((1, 1, d, d_e2), lambda i, e, f, a: (layer, e[i], 0, 0)),
                  pl.BlockSpec((1, 1, d_e2 // 2, d), lambda i, e, f, a: (layer, e[i], 0, 0))],
        out_specs=pl.BlockSpec((2, bm, PACK_W), words),
        scratch_shapes=[pltpu.VMEM((d, d_e2), jnp.bfloat16), pltpu.VMEM((d_e2 // 2, d), jnp.bfloat16)])
    return pl.pallas_call(
        _moe_expert_body,
        grid_spec=grid_spec,
        out_shape=jax.ShapeDtypeStruct((2, n_slots, PACK_W), jnp.int32),
        compiler_params=pltpu.CompilerParams(dimension_semantics=("arbitrary",),
                                             vmem_limit_bytes=VMEM_LIMIT),
        name="moe_experts",
    )(blk_exp, blk_first, blk_rows, xs, w_exp_gu, w_exp_down)


def _combine_ln_body(x_ref, yg_ref, gt_ref, wgu_ref, wdn_ref, g_ref, b_ref, o_ref):
    bf16 = jnp.bfloat16
    x1 = x_ref[...]
    h = _dot(x1.astype(bf16), wgu_ref[...])
    d_sh = h.shape[1] // 2
    shared = _dot((jax.nn.silu(h[:, :d_sh]) * h[:, d_sh:]).astype(bf16), wdn_ref[...])
    gt = gt_ref[...]
    parts = []
    for hw in range(2):
        lo_acc = hi_acc = None
        for k in range(TOP_K):
            lo, hi = _unpack_words(yg_ref[hw, k])
            gk = gt[:, k:k + 1]
            lo_acc = lo * gk if lo_acc is None else lo_acc + lo * gk
            hi_acc = hi * gk if hi_acc is None else hi_acc + hi * gk
        parts += [lo_acc, hi_acc]
    routed = jnp.concatenate(parts, axis=1)
    o_ref[...] = _ln_rows(ALPHA * x1 + (routed + shared), g_ref[...], b_ref[...])


def _combine_ln(x, yg, gate_t, w_sh_gu, w_sh_down, g, b):
    m, d = x.shape
    d_sh2 = w_sh_gu.shape[1]
    tm = min(m, 256)
    row = lambda i: (i, 0)
    fixed = lambda i: (0, 0)
    return pl.pallas_call(
        _combine_ln_body,
        grid=(m // tm,),
        in_specs=[pl.BlockSpec((tm, d), row), pl.BlockSpec((2, TOP_K, tm, PACK_W), lambda i: (0, 0, i, 0)),
                  pl.BlockSpec((tm, TOP_K), row), pl.BlockSpec((d, d_sh2), fixed), pl.BlockSpec((d_sh2 // 2, d), fixed),
                  pl.BlockSpec((1, d), fixed), pl.BlockSpec((1, d), fixed)],
        out_specs=pl.BlockSpec((tm, d), row),
        out_shape=jax.ShapeDtypeStruct((m, d), jnp.float32),
        compiler_params=pltpu.CompilerParams(dimension_semantics=("arbitrary",)),
        name="combine_ln",
    )(x, yg, gate_t, w_sh_gu.astype(jnp.bfloat16), w_sh_down.astype(jnp.bfloat16), g.reshape(1, d), b.reshape(1, d))


def _moe_slots_body(base_ref, eidx_ref, rank_ref, o_ref):
    eidx = eidx_ref[...]
    slot = rank_ref[...]
    for e in range(N_EXPERTS):
        slot = slot + jnp.where(eidx == e, base_ref[e], 0)
    o_ref[...] = slot


def _moe_slots(base, eidx, rank):
    k, m = eidx.shape
    tm = min(m, 2048)
    tile = pl.BlockSpec((k, tm), lambda i, b: (0, i))
    return pl.pallas_call(
        _moe_slots_body,
        grid_spec=pltpu.PrefetchScalarGridSpec(num_scalar_prefetch=1, grid=(m // tm,),
                                               in_specs=[tile, tile], out_specs=tile),
        out_shape=jax.ShapeDtypeStruct((k, m), jnp.int32),
        compiler_params=pltpu.CompilerParams(dimension_semantics=("arbitrary",)),
        name="moe_slots",
    )(base.astype(jnp.int32), eidx, rank)


def _moe_layer(streams, ln1_g, ln1_b, ln2_g, ln2_b, w_router, b_router, w_exp_gu, w_exp_down, layer,
               w_sh_gu, w_sh_down):
    pre = [_moe_pre(x, mix, ln1_g, ln1_b, w_router, b_router) for x, mix in streams]
    m_all = sum(x.shape[0] for x, _ in streams)
    bm = 512 if m_all * TOP_K >= 512 * N_EXPERTS else MOE_BLK
    m_pad = -(-m_all // MOE_ALIGN) * MOE_ALIGN
    spare = -(-(TOP_K * (m_pad - m_all)) // bm)
    n_blk = (m_all * TOP_K) // bm + N_EXPERTS + spare
    n_slots = n_blk * bm
    counts_of = [p[5][:, 0].astype(jnp.int32) for p in pre]
    counts = sum(counts_of)
    padded = (counts + bm - 1) // bm * bm
    pad_end = jnp.cumsum(padded)
    pad_start = pad_end - padded
    dests, before = [], jnp.zeros_like(counts)
    for p, cnt in zip(pre, counts_of):
        dests.append(_moe_slots(pad_start + before, p[2], p[4]))
        before = before + cnt
    blk_start = jnp.arange(n_blk, dtype=jnp.int32) * bm
    blk_exp = jnp.minimum(jnp.sum(pad_end[None, :] <= blk_start[:, None], axis=1), N_EXPERTS - 1).astype(jnp.int32)
    blk_rows = jnp.clip(counts[blk_exp] - (blk_start - pad_start[blk_exp]), 0, bm).astype(jnp.int32)
    blk_first = jnp.concatenate([jnp.ones((1,), jnp.int32), (blk_exp[1:] != blk_exp[:-1]).astype(jnp.int32)])
    spare_slots = (n_slots - spare * bm) + jnp.arange(TOP_K * (m_pad - m_all), dtype=jnp.int32)
    dest_all = jnp.concatenate(dests + [spare_slots.reshape(TOP_K, m_pad - m_all)], axis=1).reshape(-1)
    fill = jnp.zeros((2, m_pad - m_all, PACK_W), jnp.int32)
    words = jnp.concatenate([p[1] for p in pre] + [fill], axis=1).reshape(2 * m_pad, PACK_W)
    xs = _scatter_rows(words, jnp.concatenate([dest_all, dest_all + n_slots]), 2 * n_slots)
    y = _moe_experts(xs.reshape(2, n_slots, PACK_W), blk_exp, blk_first, blk_rows, w_exp_gu, w_exp_down, layer, bm)
    y = y.reshape(2 * n_slots, PACK_W)
    outs = []
    for p, d_i, (x, _) in zip(pre, dests, streams):
        d_i = d_i.reshape(-1)
        yg = _gather_rows(y, jnp.concatenate([d_i, d_i + n_slots])).reshape(2, TOP_K, x.shape[0], PACK_W)
        outs.append(_combine_ln(p[0], yg, p[3].T, w_sh_gu, w_sh_down, ln2_g, ln2_b))
    return outs


def _trunks(x_p, x_s, pos_p, pos_s, gla_state, nsa_cache, page_table, win_buf, conv_buf,
            w_in_ab, w_gla_gate, b_gla_gate, gla_norm_g, w_cmp_pool, w_out_ab,
            w_pw1, b_pw1, w_dw, b_dw, conv_ln_g, conv_ln_b, w_pw2, b_pw2,
            ln_g, ln_b, w_router, b_router, w_exp_gu, w_exp_down, w_sh_gu, w_sh_down):
    xs = [x_p, x_s]
    states = [(None, None, None, None), (gla_state, nsa_cache, win_buf, conv_buf)]
    poss = [pos_p, pos_s]
    new = [dict(gla=[], rows=[], win=[], conv=[]) for _ in xs]
    for layer in range(DEPTH):
        i = layer // 2
        mixes = []
        for x, pos, (g_st, cache, wbuf, cbuf), out in zip(xs, poss, states, new):
            if layer % 2 == 0:
                mix, s_a, rows, win = _ab_mixer(
                    x, pos, w_in_ab[i], w_gla_gate[i], b_gla_gate[i], gla_norm_g[i], w_cmp_pool[i], w_out_ab[i],
                    None if g_st is None else g_st[i], None if cache is None else cache[i], page_table,
                    None if wbuf is None else wbuf[i])
                out["gla"].append(s_a)
                out["rows"].append(rows)
                out["win"].append(win)
            else:
                mix, cb = _conv_module(x, None if cbuf is None else cbuf[i], w_pw1[i], b_pw1[i],
                                       w_dw[i], b_dw[i], conv_ln_g[i], conv_ln_b[i], w_pw2[i], b_pw2[i])
                out["conv"].append(cb)
            mixes.append(mix)
        d = xs[0].shape[-1]
        ys = _moe_layer([(x.reshape(-1, d), mix.reshape(-1, d)) for x, mix in zip(xs, mixes)],
                        ln_g[layer, 0], ln_b[layer, 0], ln_g[layer, 1], ln_b[layer, 1],
                        w_router[layer], b_router[layer], w_exp_gu, w_exp_down, layer,
                        w_sh_gu[layer], w_sh_down[layer])
        xs = [y.reshape(x.shape) for y, x in zip(ys, xs)]
    return [(x, jnp.stack(o["gla"]), jnp.stack(o["rows"]), jnp.stack(o["win"]), jnp.stack(o["conv"]))
            for x, o in zip(xs, new)]


def kernel(x_prompt, x_sample, state_gla, cache_nsa_kv, state_nsa_win, state_conv, page_table,
           w_in_ab, w_gla_gate, b_gla_gate, gla_norm_g, w_cmp_pool, w_out_ab,
           w_pw1, b_pw1, w_dw, b_dw, conv_ln_g, conv_ln_b, w_pw2, b_pw2,
           ln_g, ln_b, w_router, b_router, w_exp_gu, w_exp_down, w_sh_gu, w_sh_down):
    weights = (w_in_ab, w_gla_gate, b_gla_gate, gla_norm_g, w_cmp_pool, w_out_ab,
               w_pw1, b_pw1, w_dw, b_dw, conv_ln_g, conv_ln_b, w_pw2, b_pw2,
               ln_g, ln_b, w_router, b_router, w_exp_gu, w_exp_down, w_sh_gu, w_sh_down)
    past_len = page_table.shape[1] * PAGE_SIZE
    pos_p = jnp.arange(x_prompt.shape[1])
    pos_s = past_len + jnp.arange(x_sample.shape[1])
    (y_prompt, gla_p, rows_p, win_p, conv_p), (y_sample, gla_s, rows_s, win_s, conv_s) = _trunks(
        x_prompt, x_sample, pos_p, pos_s, state_gla, cache_nsa_kv, page_table, state_nsa_win, state_conv, *weights)
    return (y_prompt, y_sample, gla_p, gla_s, rows_p, rows_s, win_p, win_s, conv_p, conv_s)
```
